```python
import jax, jax.numpy as jnp
from jax import lax
import numpy as np

D_MODEL = 2048
BATCH = 2
SEQ = 4096
DEPTH = 2
DEC_BATCH = 8
DEC_SEQ = 1
PAST_LEN = 16384
PAGE_SIZE = 128

MIX_WIDTH = D_MODEL
CONV_A_WIDTH = MIX_WIDTH // 2
CONV_B_WIDTH = MIX_WIDTH - CONV_A_WIDTH
CONV_A_TAPS = 31
CONV_B_TAPS = 3
N_HEADS = 8
HEAD_DIM = 128
ATTN_WIDTH = N_HEADS * HEAD_DIM
POOL_WIDTH = MIX_WIDTH - ATTN_WIDTH
POOL_WINDOWS = (2, 4, 8, 16)
POOL_GROUP = POOL_WIDTH // len(POOL_WINDOWS)
POOL_MAX = max(POOL_WINDOWS)
MOBA_BLOCK = 256
MOBA_TOPK = 3
Q_CHUNK = 64
D_FF = ((8 * D_MODEL + 3 * 256 - 1) // (3 * 256)) * 256
N_CONV_LAYERS = (DEPTH + 1) // 2
N_ATTN_LAYERS = DEPTH // 2
EPS = 1e-6

kernel_name = 'hybrid_conv_moba_pool_decoder'


def rms_norm(x, g):
    xf = x.astype(jnp.float32)
    y = xf * lax.rsqrt(jnp.mean(xf * xf, axis=-1, keepdims=True) + EPS)
    return (y * g.astype(jnp.float32)).astype(x.dtype)


def layer_norm(x, g, b):
    xf = x.astype(jnp.float32)
    mu = jnp.mean(xf, axis=-1, keepdims=True)
    xc = xf - mu
    y = xc * lax.rsqrt(jnp.mean(xc * xc, axis=-1, keepdims=True) + EPS)
    return (y * g.astype(jnp.float32) + b.astype(jnp.float32)).astype(x.dtype)


def causal_dwconv(prefix, u, w):
    xcat = jnp.concatenate([prefix.astype(u.dtype), u], axis=1)
    y = lax.conv_general_dilated(xcat, w[:, None, :].astype(u.dtype), window_strides=(1,), padding='VALID',
                                 dimension_numbers=('NWC', 'WIO', 'NWC'), feature_group_count=u.shape[-1])
    return y, xcat[:, -(w.shape[0] - 1):]


def swiglu(h, w_gate, w_up, w_down):
    return (jax.nn.silu(h @ w_gate) * (h @ w_up)) @ w_down


def conv_mixer(h, pre_a, pre_b, w_in, a_dw, a_dw_b, a_ln_g, a_ln_b, b_dw, w_out):
    A, Bw = CONV_A_WIDTH, CONV_B_WIDTH
    z = h @ w_in
    a_val, a_gate, b_h, b_bg, b_cg = jnp.split(z, [A, 2 * A, 2 * A + Bw, 2 * A + 2 * Bw], axis=-1)
    a = a_val * jax.nn.sigmoid(a_gate)
    a_conv, new_a = causal_dwconv(pre_a, a, a_dw)
    a_out = jax.nn.silu(layer_norm(a_conv + a_dw_b, a_ln_g, a_ln_b))
    b_conv, new_b = causal_dwconv(pre_b, b_cg * b_h, b_dw)
    b_out = b_bg * b_conv
    y = jnp.concatenate([a_out, b_out], axis=-1) @ w_out
    return y, new_a, new_b


def moba_attention(q, k_all, v_all, q_start, q_chunk):
    Bn, Lq, H, Dh = q.shape
    nb = k_all.shape[1] // MOBA_BLOCK
    top = min(MOBA_TOPK, nb)
    kb = k_all.reshape(Bn, nb, MOBA_BLOCK, H, Dh).transpose(0, 3, 1, 2, 4)
    vb = v_all.reshape(Bn, nb, MOBA_BLOCK, H, Dh).transpose(0, 3, 1, 2, 4)
    kmean = jnp.mean(kb.astype(jnp.float32), axis=3)
    slopes = jnp.exp2(-8.0 * jnp.arange(1, H + 1, dtype=jnp.float32) / H)
    n_chunks = Lq // q_chunk
    qs = q.reshape(Bn, n_chunks, q_chunk, H, Dh).transpose(1, 0, 3, 2, 4)
    qpos = (q_start + jnp.arange(Lq, dtype=jnp.int32)).reshape(n_chunks, q_chunk)
    b_idx = jnp.arange(Bn)[:, None, None, None]
    h_idx = jnp.arange(H)[None, :, None, None]
    blk_ids = jnp.arange(nb, dtype=jnp.int32)
    row_ids = jnp.arange(MOBA_BLOCK, dtype=jnp.int32)

    def one_chunk(args):
        qq, qp = args
        own = qp // MOBA_BLOCK
        own4 = own[None, None, :, None]
        gate = jnp.einsum('bhqd,bhnd->bhqn', qq.astype(jnp.float32), kmean)
        gate = jnp.where(blk_ids[None, None, None, :] < own4, gate, -jnp.inf)
        _, sel = lax.top_k(gate, top)
        sel = sel.astype(jnp.int32)
        own_b = jnp.broadcast_to(own4, (Bn, H, q_chunk, 1))
        idx = jnp.concatenate([sel, own_b], axis=-1)
        slot_ok = jnp.concatenate([sel < own4, jnp.ones(own_b.shape, dtype=bool)], axis=-1)
        kg = kb[b_idx, h_idx, idx]
        vg = vb[b_idx, h_idx, idx]
        kpos = idx[..., None] * MOBA_BLOCK + row_ids
        dist = (qp[None, None, :, None, None] - kpos).astype(jnp.float32)
        s = jnp.einsum('bhqd,bhqnsd->bhqns', qq, kg, preferred_element_type=jnp.float32) * (HEAD_DIM ** -0.5)
        s = s - slopes[None, :, None, None, None] * dist
        ok = slot_ok[..., None] & (dist >= 0.0)
        s = jnp.where(ok, s, -jnp.inf).reshape(Bn, H, q_chunk, -1)
        p = jax.nn.softmax(s, axis=-1).reshape(Bn, H, q_chunk, top + 1, MOBA_BLOCK)
        return jnp.einsum('bhqns,bhqnsd->bhqd', p.astype(vg.dtype), vg, preferred_element_type=jnp.float32)

    o = lax.map(one_chunk, (qs, qpos))
    return o.transpose(1, 0, 3, 2, 4).reshape(Bn, Lq, H * Dh)


def pool_mixer(u, prefix, start_pos, pool_w, pool_scale):
    Bn, L, C = u.shape
    P = POOL_MAX - 1
    xcat = jnp.concatenate([prefix.astype(u.dtype), u], axis=1)
    xf = xcat.astype(jnp.float32)
    cs = jnp.concatenate([jnp.zeros((Bn, 1, C), jnp.float32), jnp.cumsum(xf, axis=1)], axis=1)
    pos = (start_pos + jnp.arange(L, dtype=jnp.int32)).astype(jnp.float32)
    groups = []
    for g, w in enumerate(POOL_WINDOWS):
        c0, c1 = g * POOL_GROUP, (g + 1) * POOL_GROUP
        wsum = cs[:, P + 1:P + 1 + L, c0:c1] - cs[:, P + 1 - w:P + 1 - w + L, c0:c1]
        count = jnp.minimum(pos + 1.0, float(w))
        groups.append(wsum / count[None, :, None])
    d = jnp.concatenate(groups, axis=-1) - xf[:, P:]
    d = d.reshape(Bn, L, len(POOL_WINDOWS), POOL_GROUP).astype(u.dtype)
    y = jnp.einsum('blgc,gcd->blgd', d, pool_w).reshape(Bn, L, C) * pool_scale
    return y, xcat[:, -P:]


def attn_pool_mixer(h, start_pos, q_chunk, past_k, past_v, pre_pool, w_in, pool_w, pool_scale, w_out):
    Bn, L, _ = h.shape
    z = h @ w_in
    q, k, v, u = jnp.split(z, [ATTN_WIDTH, 2 * ATTN_WIDTH, 3 * ATTN_WIDTH], axis=-1)
    q = q.reshape(Bn, L, N_HEADS, HEAD_DIM)
    k = k.reshape(Bn, L, N_HEADS, HEAD_DIM)
    v = v.reshape(Bn, L, N_HEADS, HEAD_DIM)
    if past_k is None:
        k_all, v_all = k, v
    else:
        k_all = jnp.concatenate([past_k.astype(k.dtype), k], axis=1)
        v_all = jnp.concatenate([past_v.astype(v.dtype), v], axis=1)
    pad = (-k_all.shape[1]) % MOBA_BLOCK
    k_all = jnp.pad(k_all, ((0, 0), (0, pad), (0, 0), (0, 0)))
    v_all = jnp.pad(v_all, ((0, 0), (0, pad), (0, 0), (0, 0)))
    o = moba_attention(q, k_all, v_all, start_pos, q_chunk).astype(h.dtype)
    y_pool, new_pool = pool_mixer(u, pre_pool, start_pos, pool_w, pool_scale)
    y = jnp.concatenate([o, y_pool], axis=-1) @ w_out
    return y, k, v, new_pool


def trunk(x, start_pos, q_chunk, pre_a, pre_b, past_k, past_v, pre_pool,
          conv_norm_g, conv_w_in, conv_a_dw, conv_a_dw_b, conv_a_ln_g, conv_a_ln_b, conv_b_dw, conv_w_out,
          attn_norm_g, attn_w_in, pool_w, pool_scale, attn_w_out,
          ffn_norm_g, ffn_w_gate, ffn_w_up, ffn_w_down, final_norm_g):
    new_a, new_b, new_k, new_v, new_pool = [], [], [], [], []
    for layer in range(DEPTH):
        i = layer // 2
        if layer % 2 == 0:
            y, sa, sb = conv_mixer(rms_norm(x, conv_norm_g[i]), pre_a[i], pre_b[i], conv_w_in[i], conv_a_dw[i],
                                   conv_a_dw_b[i], conv_a_ln_g[i], conv_a_ln_b[i], conv_b_dw[i], conv_w_out[i])
            new_a.append(sa)
            new_b.append(sb)
        else:
            y, k, v, sp = attn_pool_mixer(rms_norm(x, attn_norm_g[i]), start_pos, q_chunk, past_k[i], past_v[i],
                                          pre_pool[i], attn_w_in[i], pool_w[i], pool_scale[i], attn_w_out[i])
            new_k.append(k)
            new_v.append(v)
            new_pool.append(sp)
        x = x + y
        x = x + swiglu(rms_norm(x, ffn_norm_g[layer]), ffn_w_gate[layer], ffn_w_up[layer], ffn_w_down[layer])
    return rms_norm(x, final_norm_g), new_a, new_b, new_k, new_v, new_pool


def setup_inputs(seed: int = 0) -> dict:
    key = jax.random.key(seed)
    ks = iter(jax.random.split(key, 40))

    def nrm(shape, scale):
        return jax.random.normal(next(ks), shape, jnp.float32) * scale

    NC, NA = N_CONV_LAYERS, N_ATTN_LAYERS
    n_pages = PAST_LEN // PAGE_SIZE
    n_pool_pages = (DEC_BATCH * n_pages * 5) // 4
    page_table = jax.random.permutation(next(ks), n_pool_pages)[:DEC_BATCH * n_pages]
    page_table = page_table.reshape(DEC_BATCH, n_pages).astype(jnp.int32)
    return {
        'x_prompt': nrm((BATCH, SEQ, D_MODEL), 1.0),
        'x_sample': nrm((DEC_BATCH, DEC_SEQ, D_MODEL), 1.0),
        'state_conv_a': nrm((NC, DEC_BATCH, CONV_A_TAPS - 1, CONV_A_WIDTH), 0.5),
        'state_conv_b': nrm((NC, DEC_BATCH, CONV_B_TAPS - 1, CONV_B_WIDTH), 1.0),
        'cache_k': nrm((NA, n_pool_pages, PAGE_SIZE, N_HEADS, HEAD_DIM), 1.0),
        'cache_v': nrm((NA, n_pool_pages, PAGE_SIZE, N_HEADS, HEAD_DIM), 1.0),
        'state_pool': nrm((NA, DEC_BATCH, POOL_MAX - 1, POOL_WIDTH), 1.0),
        'page_table': page_table,
        'conv_norm_g': 1.0 + nrm((NC, D_MODEL), 0.02),
        'conv_w_in': nrm((NC, D_MODEL, 2 * CONV_A_WIDTH + 3 * CONV_B_WIDTH), D_MODEL ** -0.5),
        'conv_a_dw': nrm((NC, CONV_A_TAPS, CONV_A_WIDTH), CONV_A_TAPS ** -0.5),
        'conv_a_dw_b': nrm((NC, CONV_A_WIDTH), 0.02),
        'conv_a_ln_g': 1.0 + nrm((NC, CONV_A_WIDTH), 0.02),
        'conv_a_ln_b': nrm((NC, CONV_A_WIDTH), 0.02),
        'conv_b_dw': nrm((NC, CONV_B_TAPS, CONV_B_WIDTH), CONV_B_TAPS ** -0.5),
        'conv_w_out': nrm((NC, MIX_WIDTH, D_MODEL), MIX_WIDTH ** -0.5),
        'attn_norm_g': 1.0 + nrm((NA, D_MODEL), 0.02),
        'attn_w_in': nrm((NA, D_MODEL, 3 * ATTN_WIDTH + POOL_WIDTH), D_MODEL ** -0.5),
        'pool_w': nrm((NA, len(POOL_WINDOWS), POOL_GROUP, POOL_GROUP), POOL_GROUP ** -0.5),
        'pool_scale': 1.0 + nrm((NA, POOL_WIDTH), 0.02),
        'attn_w_out': nrm((NA, MIX_WIDTH, D_MODEL), MIX_WIDTH ** -0.5),
        'ffn_norm_g': 1.0 + nrm((DEPTH, D_MODEL), 0.02),
        'ffn_w_gate': nrm((DEPTH, D_MODEL, D_FF), D_MODEL ** -0.5),
        'ffn_w_up': nrm((DEPTH, D_MODEL, D_FF), D_MODEL ** -0.5),
        'ffn_w_down': nrm((DEPTH, D_FF, D_MODEL), D_FF ** -0.5),
        'final_norm_g': 1.0 + nrm((D_MODEL,), 0.02),
    }


def reference(x_prompt, x_sample, state_conv_a, state_conv_b, cache_k, cache_v, state_pool, page_table,
              conv_norm_g, conv_w_in, conv_a_dw, conv_a_dw_b, conv_a_ln_g, conv_a_ln_b, conv_b_dw, conv_w_out,
              attn_norm_g, attn_w_in, pool_w, pool_scale, attn_w_out,
              ffn_norm_g, ffn_w_gate, ffn_w_up, ffn_w_down, final_norm_g):
    n_pages = PAST_LEN // PAGE_SIZE
    params = (conv_norm_g, conv_w_in, conv_a_dw, conv_a_dw_b, conv_a_ln_g, conv_a_ln_b, conv_b_dw, conv_w_out,
              attn_norm_g, attn_w_in, pool_w, pool_scale, attn_w_out,
              ffn_norm_g, ffn_w_gate, ffn_w_up, ffn_w_down, final_norm_g)
    dt = x_prompt.dtype
    zeros_a = [jnp.zeros((BATCH, CONV_A_TAPS - 1, CONV_A_WIDTH), dt)] * N_CONV_LAYERS
    zeros_b = [jnp.zeros((BATCH, CONV_B_TAPS - 1, CONV_B_WIDTH), dt)] * N_CONV_LAYERS
    zeros_pool = [jnp.zeros((BATCH, POOL_MAX - 1, POOL_WIDTH), dt)] * N_ATTN_LAYERS
    no_kv = [None] * N_ATTN_LAYERS
    y_prompt, pa, pb, pk, pv, pp = trunk(x_prompt, 0, min(Q_CHUNK, SEQ), zeros_a, zeros_b, no_kv, no_kv,
                                         zeros_pool, *params)
    past_k = [cache_k[i][page_table].reshape(DEC_BATCH, n_pages * PAGE_SIZE, N_HEADS, HEAD_DIM)
              for i in range(N_ATTN_LAYERS)]
    past_v = [cache_v[i][page_table].reshape(DEC_BATCH, n_pages * PAGE_SIZE, N_HEADS, HEAD_DIM)
              for i in range(N_ATTN_LAYERS)]
    pre_a = [state_conv_a[i] for i in range(N_CONV_LAYERS)]
    pre_b = [state_conv_b[i] for i in range(N_CONV_LAYERS)]
    pre_pool = [state_pool[i] for i in range(N_ATTN_LAYERS)]
    y_sample, sa, sb, sk, sv, sp = trunk(x_sample, PAST_LEN, DEC_SEQ, pre_a, pre_b, past_k, past_v,
                                         pre_pool, *params)
    return (y_prompt, y_sample, jnp.stack(pa), jnp.stack(sa), jnp.stack(pb), jnp.stack(sb),
            jnp.stack(pk), jnp.stack(sk), jnp.stack(pv), jnp.stack(sv), jnp.stack(pp), jnp.stack(sp))
```

```python
import functools

import jax
import jax.numpy as jnp
from jax import lax
from jax.experimental import pallas as pl
from jax.experimental.pallas import tpu as pltpu

EPS = 1e-6
PAGE_SIZE = 128
MOBA_BLOCK = 256
MOBA_TOPK = 3
HEAD_DIM = 128
POOL_WINDOWS = (2, 4, 8, 16)
POOL_MAX = max(POOL_WINDOWS)
MASKED = -1e30

V7X_VMEM_BYTES = 64 * 1024 * 1024
VMEM_LIMIT = V7X_VMEM_BYTES - 8 * 1024 * 1024
LANES = 128
COL_GROUP = 1024

BF16 = jnp.bfloat16
F32 = jnp.float32


def _params(*sem):
    return pltpu.CompilerParams(dimension_semantics=sem, vmem_limit_bytes=VMEM_LIMIT)


def _rms(x, g):
    ms = jnp.mean(x * x, axis=-1, keepdims=True)
    return x * lax.rsqrt(ms + EPS) * g


def _sigmoid(x):
    return 1.0 / (1.0 + jnp.exp(-x))


def _norm_matmul_kernel(x_ref, g_ref, w_ref, *rest, n_out):
    out_refs, h_ref = rest[:n_out], rest[n_out]
    j = pl.program_id(1)

    @pl.when(j == 0)
    def _():
        h_ref[...] = _rms(x_ref[...], g_ref[...]).astype(h_ref.dtype)

    r = jnp.dot(h_ref[...], w_ref[...], preferred_element_type=F32)
    for c in range(n_out):
        @pl.when(j == c)
        def _(c=c):
            out_refs[c][...] = r


def norm_matmul(x, g, w, tm):
    m, d = x.shape
    n_out = w.shape[1] // COL_GROUP
    tm = min(tm, m)
    return pl.pallas_call(
        functools.partial(_norm_matmul_kernel, n_out=n_out),
        grid=(m // tm, n_out),
        in_specs=[pl.BlockSpec((tm, d), lambda i, j: (i, 0)),
                  pl.BlockSpec((1, d), lambda i, j: (0, 0)),
                  pl.BlockSpec((d, COL_GROUP), lambda i, j: (0, j))],
        out_specs=[pl.BlockSpec((tm, COL_GROUP), lambda i, j: (i, 0))] * n_out,
        out_shape=[jax.ShapeDtypeStruct((m, COL_GROUP), F32)] * n_out,
        scratch_shapes=[pltpu.VMEM((tm, d), BF16)],
        compiler_params=_params("parallel", "arbitrary"),
        name="norm_matmul",
    )(x, g.reshape(1, d), w)


def _ffn_kernel(x_ref, g_ref, wg_ref, wu_ref, wd_ref, fg_ref, o_ref, h_ref, acc_ref, *, final_norm):
    f = pl.program_id(1)

    @pl.when(f == 0)
    def _():
        h_ref[...] = _rms(x_ref[...], g_ref[...]).astype(h_ref.dtype)
        acc_ref[...] = jnp.zeros_like(acc_ref)

    h = h_ref[...]
    gate = jnp.dot(h, wg_ref[...], preferred_element_type=F32)
    up = jnp.dot(h, wu_ref[...], preferred_element_type=F32)
    act = (gate * _sigmoid(gate) * up).astype(BF16)
    acc_ref[...] += jnp.dot(act, wd_ref[...], preferred_element_type=F32)

    @pl.when(f == pl.num_programs(1) - 1)
    def _():
        y = x_ref[...] + acc_ref[...]
        if final_norm:
            y = _rms(y, fg_ref[...])
        o_ref[...] = y


def ffn(x, g, wg, wu, wd, final_g, tm, tf, final_norm):
    m, d = x.shape
    dff = wg.shape[1]
    tm = min(tm, m)
    return pl.pallas_call(
        functools.partial(_ffn_kernel, final_norm=final_norm),
        grid=(m // tm, dff // tf),
        in_specs=[pl.BlockSpec((tm, d), lambda i, f: (i, 0)),
                  pl.BlockSpec((1, d), lambda i, f: (0, 0)),
                  pl.BlockSpec((d, tf), lambda i, f: (0, f)),
                  pl.BlockSpec((d, tf), lambda i, f: (0, f)),
                  pl.BlockSpec((tf, d), lambda i, f: (f, 0)),
                  pl.BlockSpec((1, d), lambda i, f: (0, 0))],
        out_specs=pl.BlockSpec((tm, d), lambda i, f: (i, 0)),
        out_shape=jax.ShapeDtypeStruct((m, d), F32),
        scratch_shapes=[pltpu.VMEM((tm, d), BF16), pltpu.VMEM((tm, d), F32)],
        compiler_params=_params("parallel", "arbitrary"),
        name="ffn",
    )(x, g.reshape(1, d), wg, wu, wd, final_g.reshape(1, d))


A_HALO = 32
B_HALO = 8


def _layer_norm_silu(x, g, b):
    mu = jnp.mean(x, axis=-1, keepdims=True)
    xc = x - mu
    y = xc * lax.rsqrt(jnp.mean(xc * xc, axis=-1, keepdims=True) + EPS) * g + b
    return y * _sigmoid(y)


def _conv_mix_kernel(av_ref, ag_ref, bh_ref, bb_ref, bc_ref, x_ref, adw_ref, adwb_ref, lng_ref, lnb_ref,
                     bdw_ref, wo_ref, y_ref, newa_ref, newb_ref, aext, cext, aconv, mix, *, ta, tb, rc):
    l = pl.program_id(1)
    tl = av_ref.shape[1]
    wa = av_ref.shape[2]

    @pl.when(l == 0)
    def _():
        aext[0:A_HALO, :] = jnp.zeros((A_HALO, wa), F32)
        cext[0:B_HALO, :] = jnp.zeros((B_HALO, wa), F32)

    @pl.when(l > 0)
    def _():
        aext[0:A_HALO, :] = aext[tl:tl + A_HALO, :]
        cext[0:B_HALO, :] = cext[tl:tl + B_HALO, :]

    aext[A_HALO:A_HALO + tl, :] = av_ref[0] * _sigmoid(ag_ref[0])
    cext[B_HALO:B_HALO + tl, :] = bc_ref[0] * bh_ref[0]

    for r0 in range(0, tl, rc):
        for c0 in range(0, wa, LANES):
            acc = jnp.broadcast_to(adwb_ref[:, c0:c0 + LANES], (rc, LANES))
            for k in range(ta):
                off = A_HALO - (ta - 1) + r0 + k
                acc = acc + adw_ref[k:k + 1, c0:c0 + LANES] * aext[off:off + rc, c0:c0 + LANES]
            aconv[r0:r0 + rc, c0:c0 + LANES] = acc
    mix[:, 0:wa] = _layer_norm_silu(aconv[...], lng_ref[...], lnb_ref[...]).astype(BF16)

    bconv = jnp.zeros((tl, wa), F32)
    for k in range(tb):
        off = B_HALO - (tb - 1) + k
        bconv = bconv + bdw_ref[k:k + 1, :] * cext[off:off + tl, :]
    mix[:, wa:] = (bb_ref[0] * bconv).astype(BF16)

    y_ref[0] = x_ref[0] + jnp.dot(mix[...], wo_ref[...], preferred_element_type=F32)

    @pl.when(l == pl.num_programs(1) - 1)
    def _():
        newa_ref[0] = aext[A_HALO + tl - (ta - 1):A_HALO + tl, :]
        newb_ref[0] = cext[B_HALO + tl - (tb - 1):B_HALO + tl, :]


def conv_mix(parts, x, a_dw, a_dw_b, ln_g, ln_b, b_dw, w_out, tl):
    bsz, seq, d = x.shape
    wa = parts[0].shape[-1]
    ta, tb = a_dw.shape[0], b_dw.shape[0]
    tl = min(tl, seq)
    assert seq % tl == 0 and tl >= A_HALO and ta - 1 <= A_HALO and tb - 1 <= B_HALO
    part_spec = pl.BlockSpec((1, tl, wa), lambda b, l: (b, l, 0))
    const = lambda shape: pl.BlockSpec(shape, lambda b, l: (0,) * len(shape))
    return pl.pallas_call(
        functools.partial(_conv_mix_kernel, ta=ta, tb=tb, rc=min(128, tl)),
        grid=(bsz, seq // tl),
        in_specs=[part_spec] * 5 + [pl.BlockSpec((1, tl, d), lambda b, l: (b, l, 0)),
                                    const((ta, wa)), const((1, wa)), const((1, wa)), const((1, wa)),
                                    const((tb, wa)), const(w_out.shape)],
        out_specs=[pl.BlockSpec((1, tl, d), lambda b, l: (b, l, 0)),
                   pl.BlockSpec((1, ta - 1, wa), lambda b, l: (b, 0, 0)),
                   pl.BlockSpec((1, tb - 1, wa), lambda b, l: (b, 0, 0))],
        out_shape=[jax.ShapeDtypeStruct((bsz, seq, d), F32),
                   jax.ShapeDtypeStruct((bsz, ta - 1, wa), F32),
                   jax.ShapeDtypeStruct((bsz, tb - 1, wa), F32)],
        scratch_shapes=[pltpu.VMEM((A_HALO + tl, wa), F32), pltpu.VMEM((B_HALO + tl, wa), F32),
                        pltpu.VMEM((tl, wa), F32), pltpu.VMEM((tl, 2 * wa), BF16)],
        compiler_params=_params("parallel", "arbitrary"),
        name="conv_mix",
    )(*parts, x, a_dw, a_dw_b.reshape(1, wa), ln_g.reshape(1, wa), ln_b.reshape(1, wa), b_dw, w_out)


def _conv_step_kernel(av_ref, ag_ref, bh_ref, bb_ref, bc_ref, x_ref, sta_ref, stb_ref, adw_ref, adwb_ref,
                      lng_ref, lnb_ref, bdw_ref, wo_ref, y_ref, anew_ref, cnew_ref, *, ta, tb):
    a = av_ref[...] * _sigmoid(ag_ref[...])
    c = bc_ref[...] * bh_ref[...]
    anew_ref[...] = a
    cnew_ref[...] = c
    acc = adwb_ref[...] + adw_ref[ta - 1:ta, :] * a
    for k in range(ta - 1):
        acc = acc + adw_ref[k:k + 1, :] * sta_ref[k]
    a_out = _layer_norm_silu(acc, lng_ref[...], lnb_ref[...])
    bconv = bdw_ref[tb - 1:tb, :] * c
    for k in range(tb - 1):
        bconv = bconv + bdw_ref[k:k + 1, :] * stb_ref[k]
    mix = jnp.concatenate([a_out, bb_ref[...] * bconv], axis=-1).astype(BF16)
    y_ref[...] = x_ref[...] + jnp.dot(mix, wo_ref[...], preferred_element_type=F32)


def conv_step(parts, x, st_a, st_b, a_dw, a_dw_b, ln_g, ln_b, b_dw, w_out):
    bd, d = x.shape
    wa = parts[0].shape[-1]
    ta, tb = a_dw.shape[0], b_dw.shape[0]
    return pl.pallas_call(
        functools.partial(_conv_step_kernel, ta=ta, tb=tb),
        out_shape=[jax.ShapeDtypeStruct((bd, d), F32), jax.ShapeDtypeStruct((bd, wa), F32),
                   jax.ShapeDtypeStruct((bd, wa), F32)],
        compiler_params=pltpu.CompilerParams(vmem_limit_bytes=VMEM_LIMIT),
        name="conv_step",
    )(*parts, x, st_a, st_b, a_dw, a_dw_b.reshape(1, wa), ln_g.reshape(1, wa), ln_b.reshape(1, wa), b_dw, w_out)


def _top_blocks(gate, blk, nb):
    sels = []
    for _ in range(MOBA_TOPK):
        m = jnp.max(gate, axis=1, keepdims=True)
        idx = jnp.min(jnp.where(gate == m, blk, nb), axis=1, keepdims=True)
        sels.append(idx)
        gate = jnp.where(blk == idx, -jnp.inf, gate)
    return sels


def _moba_kernel(q_ref, k_ref, v_ref, o_ref, kb, vb, kmean, *, n_heads):
    h = pl.program_id(1)
    i = pl.program_id(2)
    bs = MOBA_BLOCK
    nb = kmean.shape[0]

    @pl.when(i == 0)
    def _():
        k = k_ref[0]
        kb[...] = k.astype(BF16)
        vb[...] = v_ref[0].astype(BF16)
        kmean[...] = jnp.mean(k.reshape(nb, bs, HEAD_DIM), axis=1)

    q = q_ref[0]
    gate = lax.dot_general(q, kmean[...], (((1,), (1,)), ((), ())), precision=lax.Precision.HIGHEST,
                           preferred_element_type=F32)
    blk = lax.broadcasted_iota(jnp.int32, (bs, nb), 1)
    sels = _top_blocks(jnp.where(blk < i, gate, -jnp.inf), blk, nb)

    slope = jnp.exp2((h + 1).astype(F32) * (-8.0 / n_heads) + jnp.zeros((1, 1), F32))
    rel = (lax.broadcasted_iota(jnp.int32, (bs, bs), 0) - lax.broadcasted_iota(jnp.int32, (bs, bs), 1))
    alibi = -slope * rel.astype(F32)
    qs = (q * (HEAD_DIM ** -0.5)).astype(BF16)
    nt = (((1,), (1,)), ((), ()))

    start = pl.multiple_of(i * bs, bs)
    s = lax.dot_general(qs, kb[pl.ds(start, bs), :], nt, preferred_element_type=F32) + alibi
    s = jnp.where(rel >= 0, s, MASKED)
    m0 = jnp.max(s, axis=1, keepdims=True)
    p = jnp.exp(s - m0)
    l0 = jnp.sum(p, axis=1, keepdims=True)
    acc0 = jnp.dot(p.astype(BF16), vb[pl.ds(start, bs), :], preferred_element_type=F32)

    def past_block(n, carry):
        m, l, acc = carry
        chosen = (sels[0] == n) | (sels[1] == n) | (sels[2] == n)
        row = jnp.where(chosen, -slope * ((i - n) * bs).astype(F32), MASKED)
        st = pl.multiple_of(n * bs, bs)
        s = lax.dot_general(qs, kb[pl.ds(st, bs), :], nt, preferred_element_type=F32) + alibi + row
        m_new = jnp.maximum(m, jnp.max(s, axis=1, keepdims=True))
        alpha = jnp.exp(m - m_new)
        p = jnp.exp(s - m_new)
        l = alpha * l + jnp.sum(p, axis=1, keepdims=True)
        acc = alpha * acc + jnp.dot(p.astype(BF16), vb[pl.ds(st, bs), :], preferred_element_type=F32)
        return m_new, l, acc

    _, l, acc = lax.fori_loop(0, i, past_block, (m0, l0, acc0))
    o_ref[0] = acc / l


def moba_prompt(q, k, v):
    bsz, seq, width = q.shape
    n_heads = width // HEAD_DIM
    assert seq % MOBA_BLOCK == 0
    nb = seq // MOBA_BLOCK
    full = pl.BlockSpec((1, seq, HEAD_DIM), lambda b, h, i: (b, 0, h))
    tile = pl.BlockSpec((1, MOBA_BLOCK, HEAD_DIM), lambda b, h, i: (b, i, h))
    return pl.pallas_call(
        functools.partial(_moba_kernel, n_heads=n_heads),
        grid=(bsz, n_heads, nb),
        in_specs=[tile, full, full],
        out_specs=tile,
        out_shape=jax.ShapeDtypeStruct((bsz, seq, width), F32),
        scratch_shapes=[pltpu.VMEM((seq, HEAD_DIM), BF16), pltpu.VMEM((seq, HEAD_DIM), BF16),
                        pltpu.VMEM((nb, HEAD_DIM), F32)],
        compiler_params=_params("parallel", "parallel", "arbitrary"),
        name="moba_prompt",
    )(q, k, v)


def _gate_step_kernel(pt_ref, k_ref, q_ref, sel_ref, ksum, *, pages_per_block, n_heads):
    p = pl.program_id(1)
    nblk = ksum.shape[0]

    @pl.when(p == 0)
    def _():
        ksum[...] = jnp.zeros_like(ksum)

    ksum[p // pages_per_block] += jnp.sum(k_ref[...], axis=0)

    @pl.when(p == pl.num_programs(1) - 1)
    def _():
        gate = jnp.sum(ksum[...] * (1.0 / MOBA_BLOCK) * q_ref[...][None], axis=-1)
        blk = lax.broadcasted_iota(jnp.int32, gate.shape, 0)
        sel_ref[...] = jnp.zeros_like(sel_ref)
        for r in range(MOBA_TOPK):
            m = jnp.max(gate, axis=0, keepdims=True)
            idx = jnp.min(jnp.where(gate == m, blk, nblk), axis=0, keepdims=True)
            sel_ref[r:r + 1, 0:n_heads] = idx
            gate = jnp.where(blk == idx, -jnp.inf, gate)


SEL_ROWS = 8


def gate_step(cache_k, layer, page_table, q):
    bd, n_pages = page_table.shape
    n_heads = q.shape[1]
    pages_per_block = MOBA_BLOCK // PAGE_SIZE
    nblk = n_pages // pages_per_block
    assert MOBA_TOPK <= SEL_ROWS and n_heads <= LANES
    grid_spec = pltpu.PrefetchScalarGridSpec(
        num_scalar_prefetch=1,
        grid=(bd, n_pages),
        in_specs=[pl.BlockSpec((None, None, PAGE_SIZE, n_heads, HEAD_DIM),
                               lambda b, p, pt: (layer, pt[b * n_pages + p], 0, 0, 0)),
                  pl.BlockSpec((None, n_heads, HEAD_DIM), lambda b, p, pt: (b, 0, 0))],
        out_specs=pl.BlockSpec((None, SEL_ROWS, LANES), lambda b, p, pt: (b, 0, 0)),
        scratch_shapes=[pltpu.VMEM((nblk, n_heads, HEAD_DIM), F32)],
    )
    sel = pl.pallas_call(
        functools.partial(_gate_step_kernel, pages_per_block=pages_per_block, n_heads=n_heads),
        grid_spec=grid_spec,
        out_shape=jax.ShapeDtypeStruct((bd, SEL_ROWS, LANES), jnp.int32),
        compiler_params=_params("parallel", "arbitrary"),
        name="gate_step",
    )(page_table.reshape(-1), cache_k, q)
    return sel[:, :MOBA_TOPK, :n_heads]


def _attn_step_kernel(pt_ref, sel_ref, k_ref, v_ref, q_ref, kn_ref, vn_ref, o_ref, m_ref, l_ref, acc_ref,
                      *, n_heads, q_pos, pages_per_block):
    b, h, j = pl.program_id(0), pl.program_id(1), pl.program_id(2)
    scale = HEAD_DIM ** -0.5
    q = q_ref[...][None]
    hidx = lax.broadcasted_iota(jnp.int32, (1, n_heads, 1), 1)

    @pl.when((h == 0) & (j == 0))
    def _():
        m_ref[...] = jnp.sum(q * kn_ref[...][None], axis=-1, keepdims=True) * scale
        l_ref[...] = jnp.ones_like(l_ref)
        acc_ref[...] = vn_ref[...][None]

    own = q_pos // MOBA_BLOCK
    n = sel_ref[(b * MOBA_TOPK + j // pages_per_block) * n_heads + h]
    slope = jnp.exp2((hidx + 1).astype(F32) * (-8.0 / n_heads))
    kpos = (n * MOBA_BLOCK + (j % pages_per_block) * PAGE_SIZE
            + lax.broadcasted_iota(jnp.int32, (PAGE_SIZE, 1, 1), 0))
    dist = (q_pos - kpos).astype(F32)
    s = jnp.sum(k_ref[...] * q, axis=-1, keepdims=True) * scale - slope * dist
    s = jnp.where((dist >= 0.0) & (hidx == h) & (n < own), s, MASKED)
    m = m_ref[...]
    m_new = jnp.maximum(m, jnp.max(s, axis=0, keepdims=True))
    alpha = jnp.exp(m - m_new)
    p = jnp.exp(s - m_new)
    l_ref[...] = alpha * l_ref[...] + jnp.sum(p, axis=0, keepdims=True)
    acc_ref[...] = alpha * acc_ref[...] + jnp.sum(p * v_ref[...], axis=0, keepdims=True)
    m_ref[...] = m_new

    @pl.when((h == n_heads - 1) & (j == pl.num_programs(2) - 1))
    def _():
        o_ref[...] = (acc_ref[...] / l_ref[...])[0]


def attn_step(cache_k, cache_v, layer, page_table, sel, q, k_new, v_new):
    bd, n_pages = page_table.shape
    n_heads = q.shape[1]
    pages_per_block = MOBA_BLOCK // PAGE_SIZE
    assert n_pages % pages_per_block == 0

    def page_index(b, h, j, pt, sl):
        blk = sl[(b * MOBA_TOPK + j // pages_per_block) * n_heads + h]
        return (layer, pt[b * n_pages + blk * pages_per_block + j % pages_per_block], 0, 0, 0)

    page = pl.BlockSpec((None, None, PAGE_SIZE, n_heads, HEAD_DIM), page_index)
    vec = pl.BlockSpec((None, n_heads, HEAD_DIM), lambda b, h, j, pt, sl: (b, 0, 0))
    grid_spec = pltpu.PrefetchScalarGridSpec(
        num_scalar_prefetch=2,
        grid=(bd, n_heads, MOBA_TOPK * pages_per_block),
        in_specs=[page, page, vec, vec, vec],
        out_specs=vec,
        scratch_shapes=[pltpu.VMEM((1, n_heads, 1), F32), pltpu.VMEM((1, n_heads, 1), F32),
                        pltpu.VMEM((1, n_heads, HEAD_DIM), F32)],
    )
    return pl.pallas_call(
        functools.partial(_attn_step_kernel, n_heads=n_heads, q_pos=n_pages * PAGE_SIZE,
                          pages_per_block=pages_per_block),
        grid_spec=grid_spec,
        out_shape=jax.ShapeDtypeStruct((bd, n_heads, HEAD_DIM), F32),
        compiler_params=_params("parallel", "arbitrary", "arbitrary"),
        name="attn_step",
    )(page_table.reshape(-1), sel.reshape(-1), cache_k, cache_v, q, k_new, v_new)


P_HALO = 16


def _pool_out_kernel(o_ref, u_ref, x_ref, pw_ref, ps_ref, wo_ref, y_ref, newp_ref, uext, mix):
    l = pl.program_id(1)
    tl = u_ref.shape[1]
    wp = u_ref.shape[2]
    wo_attn = o_ref.shape[2]
    pg = wp // len(POOL_WINDOWS)

    @pl.when(l == 0)
    def _():
        uext[0:P_HALO, :] = jnp.zeros((P_HALO, wp), F32)

    @pl.when(l > 0)
    def _():
        uext[0:P_HALO, :] = uext[tl:tl + P_HALO, :]

    uext[P_HALO:P_HALO + tl, :] = u_ref[0]
    mix[:, 0:wo_attn] = o_ref[0].astype(BF16)
    pos = l * tl + lax.broadcasted_iota(jnp.int32, (tl, pg), 0)
    for g, w in enumerate(POOL_WINDOWS):
        c0 = g * pg
        wsum = uext[P_HALO:P_HALO + tl, c0:c0 + pg]
        for j in range(1, w):
            wsum = wsum + uext[P_HALO - j:P_HALO - j + tl, c0:c0 + pg]
        count = jnp.minimum(pos + 1, w).astype(F32)
        dlt = (wsum / count - uext[P_HALO:P_HALO + tl, c0:c0 + pg]).astype(BF16)
        yp = jnp.dot(dlt, pw_ref[g], preferred_element_type=F32) * ps_ref[:, c0:c0 + pg]
        mix[:, wo_attn + c0:wo_attn + c0 + pg] = yp.astype(BF16)

    y_ref[0] = x_ref[0] + jnp.dot(mix[...], wo_ref[...], preferred_element_type=F32)

    @pl.when(l == pl.num_programs(1) - 1)
    def _():
        newp_ref[0] = uext[P_HALO + tl - (POOL_MAX - 1):P_HALO + tl, :]


def pool_out(o, u, x, pool_w, pool_scale, w_out, tl):
    bsz, seq, d = x.shape
    wp = u.shape[-1]
    tl = min(tl, seq)
    assert seq % tl == 0 and tl >= P_HALO
    const = lambda shape: pl.BlockSpec(shape, lambda b, l: (0,) * len(shape))
    return pl.pallas_call(
        _pool_out_kernel,
        grid=(bsz, seq // tl),
        in_specs=[pl.BlockSpec((1, tl, o.shape[-1]), lambda b, l: (b, l, 0)),
                  pl.BlockSpec((1, tl, wp), lambda b, l: (b, l, 0)),
                  pl.BlockSpec((1, tl, d), lambda b, l: (b, l, 0)),
                  const(pool_w.shape), const((1, wp)), const(w_out.shape)],
        out_specs=[pl.BlockSpec((1, tl, d), lambda b, l: (b, l, 0)),
                   pl.BlockSpec((1, POOL_MAX - 1, wp), lambda b, l: (b, 0, 0))],
        out_shape=[jax.ShapeDtypeStruct((bsz, seq, d), F32),
                   jax.ShapeDtypeStruct((bsz, POOL_MAX - 1, wp), F32)],
        scratch_shapes=[pltpu.VMEM((P_HALO + tl, wp), F32), pltpu.VMEM((tl, o.shape[-1] + wp), BF16)],
        compiler_params=_params("parallel", "arbitrary"),
        name="pool_out",
    )(o, u, x, pool_w, pool_scale.reshape(1, wp), w_out)


def _pool_step_kernel(o_ref, u_ref, x_ref, stp_ref, pw_ref, ps_ref, wo_ref, y_ref, *, start_pos):
    wp = u_ref.shape[1]
    pg = wp // len(POOL_WINDOWS)
    u = u_ref[...]
    parts = [o_ref[...].astype(BF16)]
    for g, w in enumerate(POOL_WINDOWS):
        c0 = g * pg
        wsum = u[:, c0:c0 + pg]
        for j in range(1, w):
            wsum = wsum + stp_ref[POOL_MAX - 1 - j][:, c0:c0 + pg]
        count = float(min(start_pos + 1, w))
        dlt = (wsum / count - u[:, c0:c0 + pg]).astype(BF16)
        yp = jnp.dot(dlt, pw_ref[g], preferred_element_type=F32) * ps_ref[:, c0:c0 + pg]
        parts.append(yp.astype(BF16))
    mix = jnp.concatenate(parts, axis=-1)
    y_ref[...] = x_ref[...] + jnp.dot(mix, wo_ref[...], preferred_element_type=F32)


def pool_step(o, u, x, st_p, pool_w, pool_scale, w_out, start_pos):
    bd, d = x.shape
    return pl.pallas_call(
        functools.partial(_pool_step_kernel, start_pos=start_pos),
        out_shape=jax.ShapeDtypeStruct((bd, d), F32),
        compiler_params=pltpu.CompilerParams(vmem_limit_bytes=VMEM_LIMIT),
        name="pool_step",
    )(o, u, x, st_p, pool_w, pool_scale.reshape(1, -1), w_out)


TM_PROMPT = 512
TF = 512
TL_MIX = 256


def kernel(x_prompt, x_sample, state_conv_a, state_conv_b, cache_k, cache_v, state_pool, page_table, conv_norm_g, conv_w_in, conv_a_dw, conv_a_dw_b, conv_a_ln_g, conv_a_ln_b, conv_b_dw, conv_w_out, attn_norm_g, attn_w_in, pool_w, pool_scale, attn_w_out, ffn_norm_g, ffn_w_gate, ffn_w_up, ffn_w_down, final_norm_g):
    bsz, seq, d = x_prompt.shape
    bd, dec_seq, _ = x_sample.shape
    assert dec_seq == 1
    depth = ffn_norm_g.shape[0]
    n_heads = cache_k.shape[3]
    wattn = n_heads * HEAD_DIM
    past_len = page_table.shape[1] * PAGE_SIZE

    to_bf16 = lambda w: w.astype(BF16)
    conv_w_in_b, conv_w_out_b = to_bf16(conv_w_in), to_bf16(conv_w_out)
    attn_w_in_b, attn_w_out_b, pool_w_b = to_bf16(attn_w_in), to_bf16(attn_w_out), to_bf16(pool_w)
    wg_b, wu_b, wd_b = to_bf16(ffn_w_gate), to_bf16(ffn_w_up), to_bf16(ffn_w_down)

    xp = x_prompt.reshape(bsz * seq, d)
    xs = x_sample.reshape(bd, d)
    pa, pb, pk, pv, pp = [], [], [], [], []
    sa, sb, sk, sv, sp = [], [], [], [], []
    for layer in range(depth):
        i = layer // 2
        if layer % 2 == 0:
            parts = norm_matmul(xp, conv_norm_g[i], conv_w_in_b[i], TM_PROMPT)
            wa = parts[0].shape[-1]
            y, na, nb_ = conv_mix([t.reshape(bsz, seq, wa) for t in parts], xp.reshape(bsz, seq, d),
                                  conv_a_dw[i], conv_a_dw_b[i], conv_a_ln_g[i], conv_a_ln_b[i], conv_b_dw[i],
                                  conv_w_out_b[i], TL_MIX)
            xp = y.reshape(bsz * seq, d)
            pa.append(na)
            pb.append(nb_)

            parts = norm_matmul(xs, conv_norm_g[i], conv_w_in_b[i], TM_PROMPT)
            st_a = jnp.swapaxes(state_conv_a[i], 0, 1)
            st_b = jnp.swapaxes(state_conv_b[i], 0, 1)
            xs, a_new, c_new = conv_step(parts, xs, st_a, st_b, conv_a_dw[i], conv_a_dw_b[i], conv_a_ln_g[i],
                                         conv_a_ln_b[i], conv_b_dw[i], conv_w_out_b[i])
            sa.append(jnp.concatenate([state_conv_a[i][:, 1:], a_new[:, None]], axis=1))
            sb.append(jnp.concatenate([state_conv_b[i][:, 1:], c_new[:, None]], axis=1))
        else:
            q, k, v, u = norm_matmul(xp, attn_norm_g[i], attn_w_in_b[i], TM_PROMPT)
            shp = (bsz, seq, wattn)
            o = moba_prompt(q.reshape(shp), k.reshape(shp), v.reshape(shp))
            y, npool = pool_out(o, u.reshape(bsz, seq, -1), xp.reshape(bsz, seq, d), pool_w_b[i], pool_scale[i],
                                attn_w_out_b[i], TL_MIX)
            xp = y.reshape(bsz * seq, d)
            pk.append(k.reshape(bsz, seq, n_heads, HEAD_DIM))
            pv.append(v.reshape(bsz, seq, n_heads, HEAD_DIM))
            pp.append(npool)

            q, k, v, u = norm_matmul(xs, attn_norm_g[i], attn_w_in_b[i], TM_PROMPT)
            heads = lambda t: t.reshape(bd, n_heads, HEAD_DIM)
            sel = gate_step(cache_k, i, page_table, heads(q))
            o = attn_step(cache_k, cache_v, i, page_table, sel, heads(q), heads(k), heads(v))
            st_p = jnp.swapaxes(state_pool[i], 0, 1)
            xs = pool_step(o.reshape(bd, wattn), u, xs, st_p, pool_w_b[i], pool_scale[i], attn_w_out_b[i], past_len)
            sk.append(k.reshape(bd, 1, n_heads, HEAD_DIM))
            sv.append(v.reshape(bd, 1, n_heads, HEAD_DIM))
            sp.append(jnp.concatenate([state_pool[i][:, 1:], u[:, None]], axis=1))
        last = layer == depth - 1
        xp = ffn(xp, ffn_norm_g[layer], wg_b[layer], wu_b[layer], wd_b[layer], final_norm_g, TM_PROMPT, TF, last)
        xs = ffn(xs, ffn_norm_g[layer], wg_b[layer], wu_b[layer], wd_b[layer], final_norm_g, TM_PROMPT, TF, last)
    return (xp.reshape(bsz, seq, d), xs.reshape(bd, 1, d), jnp.stack(pa), jnp.stack(sa), jnp.stack(pb),
            jnp.stack(sb), jnp.stack(pk), jnp.stack(sk), jnp.stack(pv), jnp.stack(sv), jnp.stack(pp), jnp.stack(sp))
```

```python
import functools

import jax
import jax.numpy as jnp
from jax import lax
from jax.experimental import pallas as pl
from jax.experimental.pallas import tpu as pltpu

EPS = 1e-6
PAGE_SIZE = 128
MOBA_BLOCK = 256
MOBA_TOPK = 3
HEAD_DIM = 128
POOL_WINDOWS = (2, 4, 8, 16)
POOL_MAX = max(POOL_WINDOWS)
MASKED = -1e30

V7X_VMEM_BYTES = 64 * 1024 * 1024
VMEM_LIMIT = V7X_VMEM_BYTES - 8 * 1024 * 1024
LANES = 128
COL_GROUP = 1024

BF16 = jnp.bfloat16
F32 = jnp.float32


def _params(*sem):
    return pltpu.CompilerParams(dimension_semantics=sem, vmem_limit_bytes=VMEM_LIMIT)


def _rms(x, g):
    ms = jnp.mean(x * x, axis=-1, keepdims=True)
    return x * lax.rsqrt(ms + EPS) * g


def _sigmoid(x):
    return 1.0 / (1.0 + jnp.exp(-x))


def _norm_matmul_kernel(x_ref, g_ref, w_ref, *rest, n_out):
    out_refs, h_ref = rest[:n_out], rest[n_out]
    j = pl.program_id(1)

    @pl.when(j == 0)
    def _():
        h_ref[...] = _rms(x_ref[...], g_ref[...]).astype(h_ref.dtype)

    r = jnp.dot(h_ref[...], w_ref[...], preferred_element_type=F32)
    for c in range(n_out):
        @pl.when(j == c)
        def _(c=c):
            out_refs[c][...] = r


def norm_matmul(x, g, w, tm):
    m, d = x.shape
    n_out = w.shape[1] // COL_GROUP
    tm = min(tm, m)
    return pl.pallas_call(
        functools.partial(_norm_matmul_kernel, n_out=n_out),
        grid=(m // tm, n_out),
        in_specs=[pl.BlockSpec((tm, d), lambda i, j: (i, 0)),
                  pl.BlockSpec((1, d), lambda i, j: (0, 0)),
                  pl.BlockSpec((d, COL_GROUP), lambda i, j: (0, j))],
        out_specs=[pl.BlockSpec((tm, COL_GROUP), lambda i, j: (i, 0))] * n_out,
        out_shape=[jax.ShapeDtypeStruct((m, COL_GROUP), F32)] * n_out,
        scratch_shapes=[pltpu.VMEM((tm, d), BF16)],
        compiler_params=_params("parallel", "arbitrary"),
        name="norm_matmul",
    )(x, g.reshape(1, d), w)


def _ffn_kernel(x_ref, g_ref, wg_ref, wu_ref, wd_ref, fg_ref, o_ref, h_ref, acc_ref, *, final_norm):
    f = pl.program_id(1)

    @pl.when(f == 0)
    def _():
        h_ref[...] = _rms(x_ref[...], g_ref[...]).astype(h_ref.dtype)
        acc_ref[...] = jnp.zeros_like(acc_ref)

    h = h_ref[...]
    gate = jnp.dot(h, wg_ref[...], preferred_element_type=F32)
    up = jnp.dot(h, wu_ref[...], preferred_element_type=F32)
    act = (gate * _sigmoid(gate) * up).astype(BF16)
    acc_ref[...] += jnp.dot(act, wd_ref[...], preferred_element_type=F32)

    @pl.when(f == pl.num_programs(1) - 1)
    def _():
        y = x_ref[...] + acc_ref[...]
        if final_norm:
            y = _rms(y, fg_ref[...])
        o_ref[...] = y


def ffn(x, g, wg, wu, wd, final_g, tm, tf, final_norm):
    m, d = x.shape
    dff = wg.shape[1]
    tm = min(tm, m)
    return pl.pallas_call(
        functools.partial(_ffn_kernel, final_norm=final_norm),
        grid=(m // tm, dff // tf),
        in_specs=[pl.BlockSpec((tm, d), lambda i, f: (i, 0)),
                  pl.BlockSpec((1, d), lambda i, f: (0, 0)),
                  pl.BlockSpec((d, tf), lambda i, f: (0, f)),
                  pl.BlockSpec((d, tf), lambda i, f: (0, f)),
                  pl.BlockSpec((tf, d), lambda i, f: (f, 0)),
                  pl.BlockSpec((1, d), lambda i, f: (0, 0))],
        out_specs=pl.BlockSpec((tm, d), lambda i, f: (i, 0)),
        out_shape=jax.ShapeDtypeStruct((m, d), F32),
        scratch_shapes=[pltpu.VMEM((tm, d), BF16), pltpu.VMEM((tm, d), F32)],
        compiler_params=_params("parallel", "arbitrary"),
        name="ffn",
    )(x, g.reshape(1, d), wg, wu, wd, final_g.reshape(1, d))


A_HALO = 32
B_HALO = 8


def _layer_norm_silu(x, g, b):
    mu = jnp.mean(x, axis=-1, keepdims=True)
    xc = x - mu
    y = xc * lax.rsqrt(jnp.mean(xc * xc, axis=-1, keepdims=True) + EPS) * g + b
    return y * _sigmoid(y)


def _conv_mix_kernel(av_ref, ag_ref, bh_ref, bb_ref, bc_ref, x_ref, adw_ref, adwb_ref, lng_ref, lnb_ref,
                     bdw_ref, wo_ref, y_ref, newa_ref, newb_ref, aext, cext, aconv, mix, *, ta, tb, rc):
    l = pl.program_id(1)
    tl = av_ref.shape[1]
    wa = av_ref.shape[2]

    @pl.when(l == 0)
    def _():
        aext[0:A_HALO, :] = jnp.zeros((A_HALO, wa), F32)
        cext[0:B_HALO, :] = jnp.zeros((B_HALO, wa), F32)

    @pl.when(l > 0)
    def _():
        aext[0:A_HALO, :] = aext[tl:tl + A_HALO, :]
        cext[0:B_HALO, :] = cext[tl:tl + B_HALO, :]

    aext[A_HALO:A_HALO + tl, :] = av_ref[0] * _sigmoid(ag_ref[0])
    cext[B_HALO:B_HALO + tl, :] = bc_ref[0] * bh_ref[0]

    for r0 in range(0, tl, rc):
        for c0 in range(0, wa, LANES):
            acc = jnp.broadcast_to(adwb_ref[:, c0:c0 + LANES], (rc, LANES))
            for k in range(ta):
                off = A_HALO - (ta - 1) + r0 + k
                acc = acc + adw_ref[k:k + 1, c0:c0 + LANES] * aext[off:off + rc, c0:c0 + LANES]
            aconv[r0:r0 + rc, c0:c0 + LANES] = acc
    mix[:, 0:wa] = _layer_norm_silu(aconv[...], lng_ref[...], lnb_ref[...]).astype(BF16)

    bconv = jnp.zeros((tl, wa), F32)
    for k in range(tb):
        off = B_HALO - (tb - 1) + k
        bconv = bconv + bdw_ref[k:k + 1, :] * cext[off:off + tl, :]
    mix[:, wa:] = (bb_ref[0] * bconv).astype(BF16)

    y_ref[0] = x_ref[0] + jnp.dot(mix[...], wo_ref[...], preferred_element_type=F32)

    @pl.when(l == pl.num_programs(1) - 1)
    def _():
        newa_ref[0] = aext[A_HALO + tl - (ta - 1):A_HALO + tl, :]
        newb_ref[0] = cext[B_HALO + tl - (tb - 1):B_HALO + tl, :]


def conv_mix(parts, x, a_dw, a_dw_b, ln_g, ln_b, b_dw, w_out, tl):
    bsz, seq, d = x.shape
    wa = parts[0].shape[-1]
    ta, tb = a_dw.shape[0], b_dw.shape[0]
    tl = min(tl, seq)
    assert seq % tl == 0 and tl >= A_HALO and ta - 1 <= A_HALO and tb - 1 <= B_HALO
    part_spec = pl.BlockSpec((1, tl, wa), lambda b, l: (b, l, 0))
    const = lambda shape: pl.BlockSpec(shape, lambda b, l: (0,) * len(shape))
    return pl.pallas_call(
        functools.partial(_conv_mix_kernel, ta=ta, tb=tb, rc=min(128, tl)),
        grid=(bsz, seq // tl),
        in_specs=[part_spec] * 5 + [pl.BlockSpec((1, tl, d), lambda b, l: (b, l, 0)),
                                    const((ta, wa)), const((1, wa)), const((1, wa)), const((1, wa)),
                                    const((tb, wa)), const(w_out.shape)],
        out_specs=[pl.BlockSpec((1, tl, d), lambda b, l: (b, l, 0)),
                   pl.BlockSpec((1, ta - 1, wa), lambda b, l: (b, 0, 0)),
                   pl.BlockSpec((1, tb - 1, wa), lambda b, l: (b, 0, 0))],
        out_shape=[jax.ShapeDtypeStruct((bsz, seq, d), F32),
                   jax.ShapeDtypeStruct((bsz, ta - 1, wa), F32),
                   jax.ShapeDtypeStruct((bsz, tb - 1, wa), F32)],
        scratch_shapes=[pltpu.VMEM((A_HALO + tl, wa), F32), pltpu.VMEM((B_HALO + tl, wa), F32),
                        pltpu.VMEM((tl, wa), F32), pltpu.VMEM((tl, 2 * wa), BF16)],
        compiler_params=_params("parallel", "arbitrary"),
        name="conv_mix",
    )(*parts, x, a_dw, a_dw_b.reshape(1, wa), ln_g.reshape(1, wa), ln_b.reshape(1, wa), b_dw, w_out)


def _conv_step_kernel(av_ref, ag_ref, bh_ref, bb_ref, bc_ref, x_ref, sta_ref, stb_ref, adw_ref, adwb_ref,
                      lng_ref, lnb_ref, bdw_ref, wo_ref, y_ref, anew_ref, cnew_ref, *, ta, tb):
    a = av_ref[...] * _sigmoid(ag_ref[...])
    c = bc_ref[...] * bh_ref[...]
    anew_ref[...] = a
    cnew_ref[...] = c
    acc = adwb_ref[...] + adw_ref[ta - 1:ta, :] * a
    for k in range(ta - 1):
        acc = acc + adw_ref[k:k + 1, :] * sta_ref[k]
    a_out = _layer_norm_silu(acc, lng_ref[...], lnb_ref[...])
    bconv = bdw_ref[tb - 1:tb, :] * c
    for k in range(tb - 1):
        bconv = bconv + bdw_ref[k:k + 1, :] * stb_ref[k]
    mix = jnp.concatenate([a_out, bb_ref[...] * bconv], axis=-1).astype(BF16)
    y_ref[...] = x_ref[...] + jnp.dot(mix, wo_ref[...], preferred_element_type=F32)


def conv_step(parts, x, st_a, st_b, a_dw, a_dw_b, ln_g, ln_b, b_dw, w_out):
    bd, d = x.shape
    wa = parts[0].shape[-1]
    ta, tb = a_dw.shape[0], b_dw.shape[0]
    return pl.pallas_call(
        functools.partial(_conv_step_kernel, ta=ta, tb=tb),
        out_shape=[jax.ShapeDtypeStruct((bd, d), F32), jax.ShapeDtypeStruct((bd, wa), F32),
                   jax.ShapeDtypeStruct((bd, wa), F32)],
        compiler_params=pltpu.CompilerParams(vmem_limit_bytes=VMEM_LIMIT),
        name="conv_step",
    )(*parts, x, st_a, st_b, a_dw, a_dw_b.reshape(1, wa), ln_g.reshape(1, wa), ln_b.reshape(1, wa), b_dw, w_out)


LOG2E = 1.4426950408889634
TQ = 2 * MOBA_BLOCK
DA = 2 * HEAD_DIM


def _split3(x):
    hi = x.astype(BF16).astype(F32)
    mid = (x - hi).astype(BF16).astype(F32)
    lo = (x - hi - mid).astype(BF16).astype(F32)
    return hi, mid, lo


def _moba_setup(q_ref, k_ref, v_ref, qaug, kaug, vaug, kmean, slope2, nb):
    seq = k_ref.shape[1]
    bs = MOBA_BLOCK
    nbp = kmean.shape[0]
    k = k_ref[0]
    q = q_ref[0]
    kmean[...] = jnp.zeros_like(kmean)
    kmean[0:nb, :] = jnp.mean(k.reshape(nb, bs, HEAD_DIM), axis=1)

    gate = lax.dot_general(kmean[...], q, (((1,), (1,)), ((), ())), precision=lax.Precision.HIGHEST,
                           preferred_element_type=F32)
    blk = lax.broadcasted_iota(jnp.int32, (nbp, seq), 0)
    own = lax.broadcasted_iota(jnp.int32, (nbp, seq), 1) // bs
    gate = jnp.where(blk < own, gate, -jnp.inf)
    attend = blk == own
    for _ in range(MOBA_TOPK):
        m = jnp.max(gate, axis=0, keepdims=True)
        idx = jnp.min(jnp.where(gate == m, blk, nbp), axis=0, keepdims=True)
        hit = blk == idx
        attend = attend | (hit & (blk < own))
        gate = jnp.where(hit, -jnp.inf, gate)
    attend_t = jnp.where(attend, 1.0, 0.0)
    attend_r = jnp.concatenate([attend_t, jnp.zeros((LANES - nbp, seq), F32)], axis=0).T

    lane = lax.broadcasted_iota(jnp.int32, (seq, LANES), 1)
    pos_i = lax.broadcasted_iota(jnp.int32, (seq, LANES), 0)
    pos = pos_i.astype(F32)

    khi, kmid, klo = _split3(slope2 * pos)
    ek = jnp.where(lane == pos_i // bs, 1.0, 0.0)
    ek = jnp.where(lane == nb, khi, ek)
    ek = jnp.where(lane == nb + 1, kmid, ek)
    ek = jnp.where(lane == nb + 2, klo, ek)
    ek = jnp.where((lane >= nb + 3) & (lane < nb + 6), 1.0, ek)
    kaug[:, 0:HEAD_DIM] = k.astype(BF16)
    kaug[:, HEAD_DIM:] = ek.astype(BF16)

    qhi, qmid, qlo = _split3(-slope2 * pos)
    eq = jnp.where(lane < nb, jnp.where(attend_r > 0.5, 0.0, MASKED), 0.0)
    eq = jnp.where((lane >= nb) & (lane < nb + 3), 1.0, eq)
    eq = jnp.where(lane == nb + 3, qhi, eq)
    eq = jnp.where(lane == nb + 4, qmid, eq)
    eq = jnp.where(lane == nb + 5, qlo, eq)
    qaug[:, 0:HEAD_DIM] = (q * (HEAD_DIM ** -0.5 * LOG2E)).astype(BF16)
    qaug[:, HEAD_DIM:] = eq.astype(BF16)

    vaug[:, 0:HEAD_DIM] = v_ref[0].astype(BF16)
    vaug[:, HEAD_DIM:] = jnp.ones((seq, HEAD_DIM), BF16)


def _moba_kernel(q_ref, k_ref, v_ref, o_ref, qaug, kaug, vaug, kmean, m_ref, acc_ref, s_ref, *, n_heads, nb):
    h = pl.program_id(1)
    i = pl.program_id(2)
    bs = MOBA_BLOCK
    nt = (((1,), (1,)), ((), ()))

    @pl.when(i == 0)
    def _():
        slope2 = jnp.exp2((h + 1).astype(F32) * (-8.0 / n_heads) + jnp.zeros((1, 1), F32)) * LOG2E
        _moba_setup(q_ref, k_ref, v_ref, qaug, kaug, vaug, kmean, slope2, nb)

    r0 = pl.multiple_of(i * TQ, TQ)
    qa = qaug[pl.ds(r0, TQ), :]

    halves = (slice(0, bs), slice(bs, TQ))

    def scores(g):
        kg = kaug[pl.ds(pl.multiple_of(g * TQ, TQ), TQ), :]
        return jnp.concatenate([lax.dot_general(qa[r], kg, nt, preferred_element_type=F32) for r in halves], axis=0)

    def update(s, g):
        vg = vaug[pl.ds(pl.multiple_of(g * TQ, TQ), TQ), :]
        for r in halves:
            m = m_ref[r, :]
            m_new = jnp.maximum(m, jnp.max(s[r], axis=1, keepdims=True))
            p = jnp.exp2(s[r] - m_new).astype(BF16)
            pv = jnp.dot(p, vg, preferred_element_type=F32)
            acc_ref[r, :] = jnp.exp2(m - m_new) * acc_ref[r, :] + pv
            m_ref[r, :] = m_new

    m_ref[...] = jnp.full(m_ref.shape, MASKED, F32)
    acc_ref[...] = jnp.zeros_like(acc_ref)
    s_ref[0] = scores(0)

    def past_group(g, slot):
        s_ref[1 - slot] = scores(g + 1)
        update(s_ref[slot], g)

    @pl.loop(0, i // 2)
    def _(j):
        past_group(2 * j, 0)
        past_group(2 * j + 1, 1)

    def own_group(slot):
        causal = (lax.broadcasted_iota(jnp.int32, (TQ, TQ), 0) >= lax.broadcasted_iota(jnp.int32, (TQ, TQ), 1))
        update(jnp.where(causal, s_ref[slot], MASKED), i)
        o_ref[0] = acc_ref[:, 0:HEAD_DIM] / acc_ref[:, HEAD_DIM:]

    @pl.when(i % 2 == 0)
    def _():
        own_group(0)

    @pl.when(i % 2 == 1)
    def _():
        past_group(i - 1, 0)
        own_group(1)


def moba_prompt(q, k, v):
    bsz, seq, width = q.shape
    n_heads = width // HEAD_DIM
    assert seq % TQ == 0
    nb = seq // MOBA_BLOCK
    nbp = -(-nb // 8) * 8
    assert nb + 6 <= LANES
    full = pl.BlockSpec((1, seq, HEAD_DIM), lambda b, h, i: (b, 0, h))
    tile = pl.BlockSpec((1, TQ, HEAD_DIM), lambda b, h, i: (b, i, h))
    return pl.pallas_call(
        functools.partial(_moba_kernel, n_heads=n_heads, nb=nb),
        grid=(bsz, n_heads, seq // TQ),
        in_specs=[full, full, full],
        out_specs=tile,
        out_shape=jax.ShapeDtypeStruct((bsz, seq, width), F32),
        scratch_shapes=[pltpu.VMEM((seq, DA), BF16), pltpu.VMEM((seq, DA), BF16), pltpu.VMEM((seq, DA), BF16),
                        pltpu.VMEM((nbp, HEAD_DIM), F32), pltpu.VMEM((TQ, 1), F32), pltpu.VMEM((TQ, DA), F32),
                        pltpu.VMEM((2, TQ, TQ), F32)],
        compiler_params=_params("parallel", "parallel", "arbitrary"),
        name="moba_prompt",
    )(q, k, v)


def _gate_step_kernel(pt_ref, k_ref, q_ref, sel_ref, ksum, *, pages_per_block, n_heads):
    p = pl.program_id(1)
    nblk = ksum.shape[0]

    @pl.when(p == 0)
    def _():
        ksum[...] = jnp.zeros_like(ksum)

    ksum[p // pages_per_block] += jnp.sum(k_ref[...], axis=0)

    @pl.when(p == pl.num_programs(1) - 1)
    def _():
        gate = jnp.sum(ksum[...] * (1.0 / MOBA_BLOCK) * q_ref[...][None], axis=-1)
        blk = lax.broadcasted_iota(jnp.int32, gate.shape, 0)
        sel_ref[...] = jnp.zeros_like(sel_ref)
        for r in range(MOBA_TOPK):
            m = jnp.max(gate, axis=0, keepdims=True)
            idx = jnp.min(jnp.where(gate == m, blk, nblk), axis=0, keepdims=True)
            sel_ref[r:r + 1, 0:n_heads] = idx
            gate = jnp.where(blk == idx, -jnp.inf, gate)


SEL_ROWS = 8


def gate_step(cache_k, layer, page_table, q):
    bd, n_pages = page_table.shape
    n_heads = q.shape[1]
    pages_per_block = MOBA_BLOCK // PAGE_SIZE
    nblk = n_pages // pages_per_block
    assert MOBA_TOPK <= SEL_ROWS and n_heads <= LANES
    grid_spec = pltpu.PrefetchScalarGridSpec(
        num_scalar_prefetch=1,
        grid=(bd, n_pages),
        in_specs=[pl.BlockSpec((None, None, PAGE_SIZE, n_heads, HEAD_DIM),
                               lambda b, p, pt: (layer, pt[b * n_pages + p], 0, 0, 0)),
                  pl.BlockSpec((None, n_heads, HEAD_DIM), lambda b, p, pt: (b, 0, 0))],
        out_specs=pl.BlockSpec((None, SEL_ROWS, LANES), lambda b, p, pt: (b, 0, 0)),
        scratch_shapes=[pltpu.VMEM((nblk, n_heads, HEAD_DIM), F32)],
    )
    sel = pl.pallas_call(
        functools.partial(_gate_step_kernel, pages_per_block=pages_per_block, n_heads=n_heads),
        grid_spec=grid_spec,
        out_shape=jax.ShapeDtypeStruct((bd, SEL_ROWS, LANES), jnp.int32),
        compiler_params=_params("parallel", "arbitrary"),
        name="gate_step",
    )(page_table.reshape(-1), cache_k, q)
    return sel[:, :MOBA_TOPK, :n_heads]


def _attn_step_kernel(pt_ref, sel_ref, k_ref, v_ref, q_ref, kn_ref, vn_ref, o_ref, m_ref, l_ref, acc_ref,
                      *, n_heads, q_pos, pages_per_block):
    b, h, j = pl.program_id(0), pl.program_id(1), pl.program_id(2)
    scale = HEAD_DIM ** -0.5
    q = q_ref[...][None]
    hidx = lax.broadcasted_iota(jnp.int32, (1, n_heads, 1), 1)

    @pl.when((h == 0) & (j == 0))
    def _():
        m_ref[...] = jnp.sum(q * kn_ref[...][None], axis=-1, keepdims=True) * scale
        l_ref[...] = jnp.ones_like(l_ref)
        acc_ref[...] = vn_ref[...][None]

    own = q_pos // MOBA_BLOCK
    n = sel_ref[(b * MOBA_TOPK + j // pages_per_block) * n_heads + h]
    slope = jnp.exp2((hidx + 1).astype(F32) * (-8.0 / n_heads))
    kpos = (n * MOBA_BLOCK + (j % pages_per_block) * PAGE_SIZE
            + lax.broadcasted_iota(jnp.int32, (PAGE_SIZE, 1, 1), 0))
    dist = (q_pos - kpos).astype(F32)
    s = jnp.sum(k_ref[...] * q, axis=-1, keepdims=True) * scale - slope * dist
    s = jnp.where((dist >= 0.0) & (hidx == h) & (n < own), s, MASKED)
    m = m_ref[...]
    m_new = jnp.maximum(m, jnp.max(s, axis=0, keepdims=True))
    alpha = jnp.exp(m - m_new)
    p = jnp.exp(s - m_new)
    l_ref[...] = alpha * l_ref[...] + jnp.sum(p, axis=0, keepdims=True)
    acc_ref[...] = alpha * acc_ref[...] + jnp.sum(p * v_ref[...], axis=0, keepdims=True)
    m_ref[...] = m_new

    @pl.when((h == n_heads - 1) & (j == pl.num_programs(2) - 1))
    def _():
        o_ref[...] = (acc_ref[...] / l_ref[...])[0]


def attn_step(cache_k, cache_v, layer, page_table, sel, q, k_new, v_new):
    bd, n_pages = page_table.shape
    n_heads = q.shape[1]
    pages_per_block = MOBA_BLOCK // PAGE_SIZE
    assert n_pages % pages_per_block == 0

    def page_index(b, h, j, pt, sl):
        blk = sl[(b * MOBA_TOPK + j // pages_per_block) * n_heads + h]
        return (layer, pt[b * n_pages + blk * pages_per_block + j % pages_per_block], 0, 0, 0)

    page = pl.BlockSpec((None, None, PAGE_SIZE, n_heads, HEAD_DIM), page_index)
    vec = pl.BlockSpec((None, n_heads, HEAD_DIM), lambda b, h, j, pt, sl: (b, 0, 0))
    grid_spec = pltpu.PrefetchScalarGridSpec(
        num_scalar_prefetch=2,
        grid=(bd, n_heads, MOBA_TOPK * pages_per_block),
        in_specs=[page, page, vec, vec, vec],
        out_specs=vec,
        scratch_shapes=[pltpu.VMEM((1, n_heads, 1), F32), pltpu.VMEM((1, n_heads, 1), F32),
                        pltpu.VMEM((1, n_heads, HEAD_DIM), F32)],
    )
    return pl.pallas_call(
        functools.partial(_attn_step_kernel, n_heads=n_heads, q_pos=n_pages * PAGE_SIZE,
                          pages_per_block=pages_per_block),
        grid_spec=grid_spec,
        out_shape=jax.ShapeDtypeStruct((bd, n_heads, HEAD_DIM), F32),
        compiler_params=_params("parallel", "arbitrary", "arbitrary"),
        name="attn_step",
    )(page_table.reshape(-1), sel.reshape(-1), cache_k, cache_v, q, k_new, v_new)


P_HALO = 16


def _pool_out_kernel(o_ref, u_ref, x_ref, pw_ref, ps_ref, wo_ref, y_ref, newp_ref, uext, mix):
    l = pl.program_id(1)
    tl = u_ref.shape[1]
    wp = u_ref.shape[2]
    wo_attn = o_ref.shape[2]
    pg = wp // len(POOL_WINDOWS)

    @pl.when(l == 0)
    def _():
        uext[0:P_HALO, :] = jnp.zeros((P_HALO, wp), F32)

    @pl.when(l > 0)
    def _():
        uext[0:P_HALO, :] = uext[tl:tl + P_HALO, :]

    uext[P_HALO:P_HALO + tl, :] = u_ref[0]
    mix[:, 0:wo_attn] = o_ref[0].astype(BF16)
    pos = l * tl + lax.broadcasted_iota(jnp.int32, (tl, pg), 0)
    for g, w in enumerate(POOL_WINDOWS):
        c0 = g * pg
        wsum = uext[P_HALO:P_HALO + tl, c0:c0 + pg]
        for j in range(1, w):
            wsum = wsum + uext[P_HALO - j:P_HALO - j + tl, c0:c0 + pg]
        count = jnp.minimum(pos + 1, w).astype(F32)
        dlt = (wsum / count - uext[P_HALO:P_HALO + tl, c0:c0 + pg]).astype(BF16)
        yp = jnp.dot(dlt, pw_ref[g], preferred_element_type=F32) * ps_ref[:, c0:c0 + pg]
        mix[:, wo_attn + c0:wo_attn + c0 + pg] = yp.astype(BF16)

    y_ref[0] = x_ref[0] + jnp.dot(mix[...], wo_ref[...], preferred_element_type=F32)

    @pl.when(l == pl.num_programs(1) - 1)
    def _():
        newp_ref[0] = uext[P_HALO + tl - (POOL_MAX - 1):P_HALO + tl, :]


def pool_out(o, u, x, pool_w, pool_scale, w_out, tl):
    bsz, seq, d = x.shape
    wp = u.shape[-1]
    tl = min(tl, seq)
    assert seq % tl == 0 and tl >= P_HALO
    const = lambda shape: pl.BlockSpec(shape, lambda b, l: (0,) * len(shape))
    return pl.pallas_call(
        _pool_out_kernel,
        grid=(bsz, seq // tl),
        in_specs=[pl.BlockSpec((1, tl, o.shape[-1]), lambda b, l: (b, l, 0)),
                  pl.BlockSpec((1, tl, wp), lambda b, l: (b, l, 0)),
                  pl.BlockSpec((1, tl, d), lambda b, l: (b, l, 0)),
                  const(pool_w.shape), const((1, wp)), const(w_out.shape)],
        out_specs=[pl.BlockSpec((1, tl, d), lambda b, l: (b, l, 0)),
                   pl.BlockSpec((1, POOL_MAX - 1, wp), lambda b, l: (b, 0, 0))],
        out_shape=[jax.ShapeDtypeStruct((bsz, seq, d), F32),
                   jax.ShapeDtypeStruct((bsz, POOL_MAX - 1, wp), F32)],
        scratch_shapes=[pltpu.VMEM((P_HALO + tl, wp), F32), pltpu.VMEM((tl, o.shape[-1] + wp), BF16)],
        compiler_params=_params("parallel", "arbitrary"),
        name="pool_out",
    )(o, u, x, pool_w, pool_scale.reshape(1, wp), w_out)


def _pool_step_kernel(o_ref, u_ref, x_ref, stp_ref, pw_ref, ps_ref, wo_ref, y_ref, *, start_pos):
    wp = u_ref.shape[1]
    pg = wp // len(POOL_WINDOWS)
    u = u_ref[...]
    parts = [o_ref[...].astype(BF16)]
    for g, w in enumerate(POOL_WINDOWS):
        c0 = g * pg
        wsum = u[:, c0:c0 + pg]
        for j in range(1, w):
            wsum = wsum + stp_ref[POOL_MAX - 1 - j][:, c0:c0 + pg]
        count = float(min(start_pos + 1, w))
        dlt = (wsum / count - u[:, c0:c0 + pg]).astype(BF16)
        yp = jnp.dot(dlt, pw_ref[g], preferred_element_type=F32) * ps_ref[:, c0:c0 + pg]
        parts.append(yp.astype(BF16))
    mix = jnp.concatenate(parts, axis=-1)
    y_ref[...] = x_ref[...] + jnp.dot(mix, wo_ref[...], preferred_element_type=F32)


def pool_step(o, u, x, st_p, pool_w, pool_scale, w_out, start_pos):
    bd, d = x.shape
    return pl.pallas_call(
        functools.partial(_pool_step_kernel, start_pos=start_pos),
        out_shape=jax.ShapeDtypeStruct((bd, d), F32),
        compiler_params=pltpu.CompilerParams(vmem_limit_bytes=VMEM_LIMIT),
        name="pool_step",
    )(o, u, x, st_p, pool_w, pool_scale.reshape(1, -1), w_out)


TM_PROMPT = 512
TF = 512
TL_MIX = 256


def kernel(x_prompt, x_sample, state_conv_a, state_conv_b, cache_k, cache_v, state_pool, page_table, conv_norm_g, conv_w_in, conv_a_dw, conv_a_dw_b, conv_a_ln_g, conv_a_ln_b, conv_b_dw, conv_w_out, attn_norm_g, attn_w_in, pool_w, pool_scale, attn_w_out, ffn_norm_g, ffn_w_gate, ffn_w_up, ffn_w_down, final_norm_g):
    bsz, seq, d = x_prompt.shape
    bd, dec_seq, _ = x_sample.shape
    assert dec_seq == 1
    depth = ffn_norm_g.shape[0]
    n_heads = cache_k.shape[3]
    wattn = n_heads * HEAD_DIM
    past_len = page_table.shape[1] * PAGE_SIZE

    to_bf16 = lambda w: w.astype(BF16)
    conv_w_in_b, conv_w_out_b = to_bf16(conv_w_in), to_bf16(conv_w_out)
    attn_w_in_b, attn_w_out_b, pool_w_b = to_bf16(attn_w_in), to_bf16(attn_w_out), to_bf16(pool_w)
    wg_b, wu_b, wd_b = to_bf16(ffn_w_gate), to_bf16(ffn_w_up), to_bf16(ffn_w_down)

    xp = x_prompt.reshape(bsz * seq, d)
    xs = x_sample.reshape(bd, d)
    pa, pb, pk, pv, pp = [], [], [], [], []
    sa, sb, sk, sv, sp = [], [], [], [], []
    for layer in range(depth):
        i = layer // 2
        if layer % 2 == 0:
            parts = norm_matmul(xp, conv_norm_g[i], conv_w_in_b[i], TM_PROMPT)
            wa = parts[0].shape[-1]
            y, na, nb_ = conv_mix([t.reshape(bsz, seq, wa) for t in parts], xp.reshape(bsz, seq, d),
                                  conv_a_dw[i], conv_a_dw_b[i], conv_a_ln_g[i], conv_a_ln_b[i], conv_b_dw[i],
                                  conv_w_out_b[i], TL_MIX)
            xp = y.reshape(bsz * seq, d)
            pa.append(na)
            pb.append(nb_)

            parts = norm_matmul(xs, conv_norm_g[i], conv_w_in_b[i], TM_PROMPT)
            st_a = jnp.swapaxes(state_conv_a[i], 0, 1)
            st_b = jnp.swapaxes(state_conv_b[i], 0, 1)
            xs, a_new, c_new = conv_step(parts, xs, st_a, st_b, conv_a_dw[i], conv_a_dw_b[i], conv_a_ln_g[i],
                                         conv_a_ln_b[i], conv_b_dw[i], conv_w_out_b[i])
            sa.append(jnp.concatenate([state_conv_a[i][:, 1:], a_new[:, None]], axis=1))
            sb.append(jnp.concatenate([state_conv_b[i][:, 1:], c_new[:, None]], axis=1))
        else:
            q, k, v, u = norm_matmul(xp, attn_norm_g[i], attn_w_in_b[i], TM_PROMPT)
            shp = (bsz, seq, wattn)
            o = moba_prompt(q.reshape(shp), k.reshape(shp), v.reshape(shp))
            y, npool = pool_out(o, u.reshape(bsz, seq, -1), xp.reshape(bsz, seq, d), pool_w_b[i], pool_scale[i],
                                attn_w_out_b[i], TL_MIX)
            xp = y.reshape(bsz * seq, d)
            pk.append(k.reshape(bsz, seq, n_heads, HEAD_DIM))
            pv.append(v.reshape(bsz, seq, n_heads, HEAD_DIM))
            pp.append(npool)

            q, k, v, u = norm_matmul(xs, attn_norm_g[i], attn_w_in_b[i], TM_PROMPT)
            heads = lambda t: t.reshape(bd, n_heads, HEAD_DIM)
            sel = gate_step(cache_k, i, page_table, heads(q))
            o = attn_step(cache_k, cache_v, i, page_table, sel, heads(q), heads(k), heads(v))
            st_p = jnp.swapaxes(state_pool[i], 0, 1)
            xs = pool_step(o.reshape(bd, wattn), u, xs, st_p, pool_w_b[i], pool_scale[i], attn_w_out_b[i], past_len)
            sk.append(k.reshape(bd, 1, n_heads, HEAD_DIM))
            sv.append(v.reshape(bd, 1, n_heads, HEAD_DIM))
            sp.append(jnp.concatenate([state_pool[i][:, 1:], u[:, None]], axis=1))
        last = layer == depth - 1
        xp = ffn(xp, ffn_norm_g[layer], wg_b[layer], wu_b[layer], wd_b[layer], final_norm_g, TM_PROMPT, TF, last)
        xs = ffn(xs, ffn_norm_g[layer], wg_b[layer], wu_b[layer], wd_b[layer], final_norm_g, TM_PROMPT, TF, last)
    return (xp.reshape(bsz, seq, d), xs.reshape(bd, 1, d), jnp.stack(pa), jnp.stack(sa), jnp.stack(pb),
            jnp.stack(sb), jnp.stack(pk), jnp.stack(sk), jnp.stack(pv), jnp.stack(sv), jnp.stack(pp), jnp.stack(sp))
```

```python
import functools

import jax
import jax.numpy as jnp
from jax import lax
from jax.experimental import pallas as pl
from jax.experimental.pallas import tpu as pltpu

EPS = 1e-6
PAGE_SIZE = 128
MOBA_BLOCK = 256
MOBA_TOPK = 3
HEAD_DIM = 128
POOL_WINDOWS = (2, 4, 8, 16)
POOL_MAX = max(POOL_WINDOWS)
MASKED = -1e30

V7X_VMEM_BYTES = 64 * 1024 * 1024
VMEM_LIMIT = V7X_VMEM_BYTES - 8 * 1024 * 1024
LANES = 128
COL_GROUP = 1024

BF16 = jnp.bfloat16
F32 = jnp.float32


def _params(*sem):
    return pltpu.CompilerParams(dimension_semantics=sem, vmem_limit_bytes=VMEM_LIMIT)


def _rms(x, g):
    ms = jnp.mean(x * x, axis=-1, keepdims=True)
    return x * lax.rsqrt(ms + EPS) * g


def _sigmoid(x):
    return 1.0 / (1.0 + jnp.exp(-x))


def _norm_matmul_kernel(x_ref, g_ref, w_ref, *rest, n_out):
    out_refs, h_ref = rest[:n_out], rest[n_out]
    j = pl.program_id(1)

    @pl.when(j == 0)
    def _():
        h_ref[...] = _rms(x_ref[...], g_ref[...]).astype(h_ref.dtype)

    r = jnp.dot(h_ref[...], w_ref[...], preferred_element_type=F32)
    for c in range(n_out):
        @pl.when(j == c)
        def _(c=c):
            out_refs[c][...] = r


def norm_matmul(x, g, w, tm):
    m, d = x.shape
    n_out = w.shape[1] // COL_GROUP
    tm = min(tm, m)
    return pl.pallas_call(
        functools.partial(_norm_matmul_kernel, n_out=n_out),
        grid=(m // tm, n_out),
        in_specs=[pl.BlockSpec((tm, d), lambda i, j: (i, 0)),
                  pl.BlockSpec((1, d), lambda i, j: (0, 0)),
                  pl.BlockSpec((d, COL_GROUP), lambda i, j: (0, j))],
        out_specs=[pl.BlockSpec((tm, COL_GROUP), lambda i, j: (i, 0))] * n_out,
        out_shape=[jax.ShapeDtypeStruct((m, COL_GROUP), F32)] * n_out,
        scratch_shapes=[pltpu.VMEM((tm, d), BF16)],
        compiler_params=_params("parallel", "arbitrary"),
        name="norm_matmul",
    )(x, g.reshape(1, d), w)


def _ffn_kernel(x_ref, g_ref, wg_ref, wu_ref, wd_ref, fg_ref, o_ref, h_ref, acc_ref, *, final_norm):
    f = pl.program_id(1)

    @pl.when(f == 0)
    def _():
        h_ref[...] = _rms(x_ref[...], g_ref[...]).astype(h_ref.dtype)
        acc_ref[...] = jnp.zeros_like(acc_ref)

    h = h_ref[...]
    gate = jnp.dot(h, wg_ref[...], preferred_element_type=F32)
    up = jnp.dot(h, wu_ref[...], preferred_element_type=F32)
    act = (gate * _sigmoid(gate) * up).astype(BF16)
    acc_ref[...] += jnp.dot(act, wd_ref[...], preferred_element_type=F32)

    @pl.when(f == pl.num_programs(1) - 1)
    def _():
        y = x_ref[...] + acc_ref[...]
        if final_norm:
            y = _rms(y, fg_ref[...])
        o_ref[...] = y


def ffn(x, g, wg, wu, wd, final_g, tm, tf, final_norm):
    m, d = x.shape
    dff = wg.shape[1]
    tm = min(tm, m)
    return pl.pallas_call(
        functools.partial(_ffn_kernel, final_norm=final_norm),
        grid=(m // tm, dff // tf),
        in_specs=[pl.BlockSpec((tm, d), lambda i, f: (i, 0)),
                  pl.BlockSpec((1, d), lambda i, f: (0, 0)),
                  pl.BlockSpec((d, tf), lambda i, f: (0, f)),
                  pl.BlockSpec((d, tf), lambda i, f: (0, f)),
                  pl.BlockSpec((tf, d), lambda i, f: (f, 0)),
                  pl.BlockSpec((1, d), lambda i, f: (0, 0))],
        out_specs=pl.BlockSpec((tm, d), lambda i, f: (i, 0)),
        out_shape=jax.ShapeDtypeStruct((m, d), F32),
        scratch_shapes=[pltpu.VMEM((tm, d), BF16), pltpu.VMEM((tm, d), F32)],
        compiler_params=_params("parallel", "arbitrary"),
        name="ffn",
    )(x, g.reshape(1, d), wg, wu, wd, final_g.reshape(1, d))


A_HALO = 32
B_HALO = 8


def _layer_norm_silu(x, g, b):
    mu = jnp.mean(x, axis=-1, keepdims=True)
    xc = x - mu
    y = xc * lax.rsqrt(jnp.mean(xc * xc, axis=-1, keepdims=True) + EPS) * g + b
    return y * _sigmoid(y)


def _conv_mix_kernel(av_ref, ag_ref, bh_ref, bb_ref, bc_ref, x_ref, adw_ref, adwb_ref, lng_ref, lnb_ref,
                     bdw_ref, wo_ref, y_ref, newa_ref, newb_ref, aext, cext, aconv, mix, *, ta, tb, rc):
    l = pl.program_id(1)
    tl = av_ref.shape[1]
    wa = av_ref.shape[2]

    @pl.when(l == 0)
    def _():
        aext[0:A_HALO, :] = jnp.zeros((A_HALO, wa), F32)
        cext[0:B_HALO, :] = jnp.zeros((B_HALO, wa), F32)

    @pl.when(l > 0)
    def _():
        aext[0:A_HALO, :] = aext[tl:tl + A_HALO, :]
        cext[0:B_HALO, :] = cext[tl:tl + B_HALO, :]

    aext[A_HALO:A_HALO + tl, :] = av_ref[0] * _sigmoid(ag_ref[0])
    cext[B_HALO:B_HALO + tl, :] = bc_ref[0] * bh_ref[0]

    for r0 in range(0, tl, rc):
        for c0 in range(0, wa, LANES):
            acc = jnp.broadcast_to(adwb_ref[:, c0:c0 + LANES], (rc, LANES))
            for k in range(ta):
                off = A_HALO - (ta - 1) + r0 + k
                acc = acc + adw_ref[k:k + 1, c0:c0 + LANES] * aext[off:off + rc, c0:c0 + LANES]
            aconv[r0:r0 + rc, c0:c0 + LANES] = acc
    mix[:, 0:wa] = _layer_norm_silu(aconv[...], lng_ref[...], lnb_ref[...]).astype(BF16)

    bconv = jnp.zeros((tl, wa), F32)
    for k in range(tb):
        off = B_HALO - (tb - 1) + k
        bconv = bconv + bdw_ref[k:k + 1, :] * cext[off:off + tl, :]
    mix[:, wa:] = (bb_ref[0] * bconv).astype(BF16)

    y_ref[0] = x_ref[0] + jnp.dot(mix[...], wo_ref[...], preferred_element_type=F32)

    @pl.when(l == pl.num_programs(1) - 1)
    def _():
        newa_ref[0] = aext[A_HALO + tl - (ta - 1):A_HALO + tl, :]
        newb_ref[0] = cext[B_HALO + tl - (tb - 1):B_HALO + tl, :]


def conv_mix(parts, x, a_dw, a_dw_b, ln_g, ln_b, b_dw, w_out, tl):
    bsz, seq, d = x.shape
    wa = parts[0].shape[-1]
    ta, tb = a_dw.shape[0], b_dw.shape[0]
    tl = min(tl, seq)
    assert seq % tl == 0 and tl >= A_HALO and ta - 1 <= A_HALO and tb - 1 <= B_HALO
    part_spec = pl.BlockSpec((1, tl, wa), lambda b, l: (b, l, 0))
    const = lambda shape: pl.BlockSpec(shape, lambda b, l: (0,) * len(shape))
    return pl.pallas_call(
        functools.partial(_conv_mix_kernel, ta=ta, tb=tb, rc=min(128, tl)),
        grid=(bsz, seq // tl),
        in_specs=[part_spec] * 5 + [pl.BlockSpec((1, tl, d), lambda b, l: (b, l, 0)),
                                    const((ta, wa)), const((1, wa)), const((1, wa)), const((1, wa)),
                                    const((tb, wa)), const(w_out.shape)],
        out_specs=[pl.BlockSpec((1, tl, d), lambda b, l: (b, l, 0)),
                   pl.BlockSpec((1, ta - 1, wa), lambda b, l: (b, 0, 0)),
                   pl.BlockSpec((1, tb - 1, wa), lambda b, l: (b, 0, 0))],
        out_shape=[jax.ShapeDtypeStruct((bsz, seq, d), F32),
                   jax.ShapeDtypeStruct((bsz, ta - 1, wa), F32),
                   jax.ShapeDtypeStruct((bsz, tb - 1, wa), F32)],
        scratch_shapes=[pltpu.VMEM((A_HALO + tl, wa), F32), pltpu.VMEM((B_HALO + tl, wa), F32),
                        pltpu.VMEM((tl, wa), F32), pltpu.VMEM((tl, 2 * wa), BF16)],
        compiler_params=_params("parallel", "arbitrary"),
        name="conv_mix",
    )(*parts, x, a_dw, a_dw_b.reshape(1, wa), ln_g.reshape(1, wa), ln_b.reshape(1, wa), b_dw, w_out)


def _conv_step_kernel(av_ref, ag_ref, bh_ref, bb_ref, bc_ref, x_ref, sta_ref, stb_ref, adw_ref, adwb_ref,
                      lng_ref, lnb_ref, bdw_ref, wo_ref, y_ref, anew_ref, cnew_ref, *, ta, tb):
    a = av_ref[...] * _sigmoid(ag_ref[...])
    c = bc_ref[...] * bh_ref[...]
    anew_ref[...] = a
    cnew_ref[...] = c
    acc = adwb_ref[...] + adw_ref[ta - 1:ta, :] * a
    for k in range(ta - 1):
        acc = acc + adw_ref[k:k + 1, :] * sta_ref[k]
    a_out = _layer_norm_silu(acc, lng_ref[...], lnb_ref[...])
    bconv = bdw_ref[tb - 1:tb, :] * c
    for k in range(tb - 1):
        bconv = bconv + bdw_ref[k:k + 1, :] * stb_ref[k]
    mix = jnp.concatenate([a_out, bb_ref[...] * bconv], axis=-1).astype(BF16)
    y_ref[...] = x_ref[...] + jnp.dot(mix, wo_ref[...], preferred_element_type=F32)


def conv_step(parts, x, st_a, st_b, a_dw, a_dw_b, ln_g, ln_b, b_dw, w_out):
    bd, d = x.shape
    wa = parts[0].shape[-1]
    ta, tb = a_dw.shape[0], b_dw.shape[0]
    return pl.pallas_call(
        functools.partial(_conv_step_kernel, ta=ta, tb=tb),
        out_shape=[jax.ShapeDtypeStruct((bd, d), F32), jax.ShapeDtypeStruct((bd, wa), F32),
                   jax.ShapeDtypeStruct((bd, wa), F32)],
        compiler_params=pltpu.CompilerParams(vmem_limit_bytes=VMEM_LIMIT),
        name="conv_step",
    )(*parts, x, st_a, st_b, a_dw, a_dw_b.reshape(1, wa), ln_g.reshape(1, wa), ln_b.reshape(1, wa), b_dw, w_out)


LOG2E = 1.4426950408889634
TQ = 2 * MOBA_BLOCK
DA = 2 * HEAD_DIM


def _split3(x):
    hi = x.astype(BF16).astype(F32)
    mid = (x - hi).astype(BF16).astype(F32)
    lo = (x - hi - mid).astype(BF16).astype(F32)
    return hi, mid, lo


def _moba_setup(q_ref, k_ref, v_ref, qaug, kaug, vaug, kmean, slope2, nb):
    seq = k_ref.shape[1]
    bs = MOBA_BLOCK
    nbp = kmean.shape[0]
    k = k_ref[0]
    q = q_ref[0]
    kmean[...] = jnp.zeros_like(kmean)
    kmean[0:nb, :] = jnp.mean(k.reshape(nb, bs, HEAD_DIM), axis=1)

    gate = lax.dot_general(kmean[...], q, (((1,), (1,)), ((), ())), precision=lax.Precision.HIGHEST,
                           preferred_element_type=F32)
    blk = lax.broadcasted_iota(jnp.int32, (nbp, seq), 0)
    own = lax.broadcasted_iota(jnp.int32, (nbp, seq), 1) // bs
    gate = jnp.where(blk < own, gate, -jnp.inf)
    attend = blk == own
    for _ in range(MOBA_TOPK):
        m = jnp.max(gate, axis=0, keepdims=True)
        idx = jnp.min(jnp.where(gate == m, blk, nbp), axis=0, keepdims=True)
        hit = blk == idx
        attend = attend | (hit & (blk < own))
        gate = jnp.where(hit, -jnp.inf, gate)
    attend_t = jnp.where(attend, 1.0, 0.0)
    attend_r = jnp.concatenate([attend_t, jnp.zeros((LANES - nbp, seq), F32)], axis=0).T

    lane = lax.broadcasted_iota(jnp.int32, (seq, LANES), 1)
    pos_i = lax.broadcasted_iota(jnp.int32, (seq, LANES), 0)
    pos = pos_i.astype(F32)

    khi, kmid, klo = _split3(slope2 * pos)
    ek = jnp.where(lane == pos_i // bs, 1.0, 0.0)
    ek = jnp.where(lane == nb, khi, ek)
    ek = jnp.where(lane == nb + 1, kmid, ek)
    ek = jnp.where(lane == nb + 2, klo, ek)
    ek = jnp.where((lane >= nb + 3) & (lane < nb + 6), 1.0, ek)
    kaug[:, 0:HEAD_DIM] = k.astype(BF16)
    kaug[:, HEAD_DIM:] = ek.astype(BF16)

    qhi, qmid, qlo = _split3(-slope2 * pos)
    eq = jnp.where(lane < nb, jnp.where(attend_r > 0.5, 0.0, MASKED), 0.0)
    eq = jnp.where((lane >= nb) & (lane < nb + 3), 1.0, eq)
    eq = jnp.where(lane == nb + 3, qhi, eq)
    eq = jnp.where(lane == nb + 4, qmid, eq)
    eq = jnp.where(lane == nb + 5, qlo, eq)
    qaug[:, 0:HEAD_DIM] = (q * (HEAD_DIM ** -0.5 * LOG2E)).astype(BF16)
    qaug[:, HEAD_DIM:] = eq.astype(BF16)

    vaug[:, 0:HEAD_DIM] = v_ref[0].astype(BF16)
    vaug[:, HEAD_DIM:] = jnp.ones((seq, HEAD_DIM), BF16)


def _moba_kernel(q_ref, k_ref, v_ref, o_ref, qaug, kaug, vaug, kmean, m_ref, acc_ref, s_ref, *, n_heads, nb):
    h = pl.program_id(1)
    i = pl.program_id(2)
    bs = MOBA_BLOCK
    nt = (((1,), (1,)), ((), ()))

    @pl.when(i == 0)
    def _():
        slope2 = jnp.exp2((h + 1).astype(F32) * (-8.0 / n_heads) + jnp.zeros((1, 1), F32)) * LOG2E
        _moba_setup(q_ref, k_ref, v_ref, qaug, kaug, vaug, kmean, slope2, nb)

    r0 = pl.multiple_of(i * TQ, TQ)
    qa = qaug[pl.ds(r0, TQ), :]

    halves = (slice(0, bs), slice(bs, TQ))

    def scores(g):
        kg = kaug[pl.ds(pl.multiple_of(g * TQ, TQ), TQ), :]
        return jnp.concatenate([lax.dot_general(qa[r], kg, nt, preferred_element_type=F32) for r in halves], axis=0)

    def update(s, g):
        vg = vaug[pl.ds(pl.multiple_of(g * TQ, TQ), TQ), :]
        for r in halves:
            m = m_ref[r, :]
            m_new = jnp.maximum(m, jnp.max(s[r], axis=1, keepdims=True))
            p = jnp.exp2(s[r] - m_new).astype(BF16)
            pv = jnp.dot(p, vg, preferred_element_type=F32)
            acc_ref[r, :] = jnp.exp2(m - m_new) * acc_ref[r, :] + pv
            m_ref[r, :] = m_new

    m_ref[...] = jnp.full(m_ref.shape, MASKED, F32)
    acc_ref[...] = jnp.zeros_like(acc_ref)
    s_ref[0] = scores(0)

    def past_group(g, slot):
        s_ref[1 - slot] = scores(g + 1)
        update(s_ref[slot], g)

    @pl.loop(0, i // 2)
    def _(j):
        past_group(2 * j, 0)
        past_group(2 * j + 1, 1)

    def own_group(slot):
        causal = (lax.broadcasted_iota(jnp.int32, (TQ, TQ), 0) >= lax.broadcasted_iota(jnp.int32, (TQ, TQ), 1))
        update(jnp.where(causal, s_ref[slot], MASKED), i)
        o_ref[0] = acc_ref[:, 0:HEAD_DIM] / acc_ref[:, HEAD_DIM:]

    @pl.when(i % 2 == 0)
    def _():
        own_group(0)

    @pl.when(i % 2 == 1)
    def _():
        past_group(i - 1, 0)
        own_group(1)


def moba_prompt(q, k, v):
    bsz, seq, width = q.shape
    n_heads = width // HEAD_DIM
    assert seq % TQ == 0
    nb = seq // MOBA_BLOCK
    nbp = -(-nb // 8) * 8
    assert nb + 6 <= LANES
    full = pl.BlockSpec((1, seq, HEAD_DIM), lambda b, h, i: (b, 0, h))
    tile = pl.BlockSpec((1, TQ, HEAD_DIM), lambda b, h, i: (b, i, h))
    return pl.pallas_call(
        functools.partial(_moba_kernel, n_heads=n_heads, nb=nb),
        grid=(bsz, n_heads, seq // TQ),
        in_specs=[full, full, full],
        out_specs=tile,
        out_shape=jax.ShapeDtypeStruct((bsz, seq, width), F32),
        scratch_shapes=[pltpu.VMEM((seq, DA), BF16), pltpu.VMEM((seq, DA), BF16), pltpu.VMEM((seq, DA), BF16),
                        pltpu.VMEM((nbp, HEAD_DIM), F32), pltpu.VMEM((TQ, 1), F32), pltpu.VMEM((TQ, DA), F32),
                        pltpu.VMEM((2, TQ, TQ), F32)],
        compiler_params=_params("parallel", "parallel", "arbitrary"),
        name="moba_prompt",
    )(q, k, v)


PAGES_IN_FLIGHT = 8
SUM_CHAINS = 4


def _gate_step_kernel(pt_ref, q_ref, k_hbm, sel_ref, buf, sem, ksum, *, layer, pages_per_block, n_heads):
    total = pt_ref.shape[0]
    bd = q_ref.shape[0]
    nblk = ksum.shape[0] // bd
    depth = buf.shape[0]

    def page_copy(t, slot):
        return pltpu.make_async_copy(k_hbm.at[layer, pt_ref[t]], buf.at[slot], sem.at[slot])

    for t in range(min(depth, total)):
        page_copy(t, t).start()
    ksum[...] = jnp.zeros_like(ksum)

    @pl.loop(0, total)
    def _(t):
        slot = t % depth
        page_copy(t, slot).wait()
        page = buf[slot].reshape(SUM_CHAINS, PAGE_SIZE // SUM_CHAINS, n_heads, HEAD_DIM)
        ksum[t // pages_per_block] += jnp.sum(jnp.sum(page, axis=1), axis=0)

        @pl.when(t + depth < total)
        def _():
            page_copy(t + depth, slot).start()

    sel_ref[...] = jnp.zeros_like(sel_ref)
    for b in range(bd):
        kmean = ksum[b * nblk:(b + 1) * nblk] * (1.0 / MOBA_BLOCK)
        gate = jnp.sum(kmean * q_ref[b][None], axis=-1)
        blk = lax.broadcasted_iota(jnp.int32, gate.shape, 0)
        for r in range(MOBA_TOPK):
            m = jnp.max(gate, axis=0, keepdims=True)
            idx = jnp.min(jnp.where(gate == m, blk, nblk), axis=0, keepdims=True)
            sel_ref[b, r:r + 1, 0:n_heads] = idx
            gate = jnp.where(blk == idx, -jnp.inf, gate)


SEL_ROWS = 8


def gate_step(cache_k, layer, page_table, q):
    bd, n_pages = page_table.shape
    n_heads = q.shape[1]
    pages_per_block = MOBA_BLOCK // PAGE_SIZE
    assert n_pages % pages_per_block == 0 and MOBA_TOPK <= SEL_ROWS and n_heads <= LANES
    nblk = n_pages // pages_per_block
    sel = pl.pallas_call(
        functools.partial(_gate_step_kernel, layer=layer, pages_per_block=pages_per_block, n_heads=n_heads),
        in_specs=[pl.BlockSpec(memory_space=pltpu.SMEM), pl.BlockSpec(memory_space=pltpu.VMEM),
                  pl.BlockSpec(memory_space=pl.ANY)],
        out_specs=pl.BlockSpec(memory_space=pltpu.VMEM),
        out_shape=jax.ShapeDtypeStruct((bd, SEL_ROWS, LANES), jnp.int32),
        scratch_shapes=[pltpu.VMEM((PAGES_IN_FLIGHT, PAGE_SIZE, n_heads, HEAD_DIM), F32),
                        pltpu.SemaphoreType.DMA((PAGES_IN_FLIGHT,)),
                        pltpu.VMEM((bd * nblk, n_heads, HEAD_DIM), F32)],
        compiler_params=pltpu.CompilerParams(vmem_limit_bytes=VMEM_LIMIT),
        name="gate_step",
    )(page_table.reshape(-1), q, cache_k)
    return sel[:, :MOBA_TOPK, :n_heads]


def _attn_step_kernel(pt_ref, sel_ref, q_ref, kn_ref, vn_ref, k_hbm, v_hbm, o_ref, kbuf, vbuf, ksem, vsem,
                      m_ref, l_ref, acc_ref, *, layer, n_pages, q_pos, pages_per_block):
    bd, n_heads, _ = q_ref.shape
    per_head = MOBA_TOPK * pages_per_block
    total = bd * n_heads * per_head
    depth = kbuf.shape[0]
    scale = HEAD_DIM ** -0.5
    own = q_pos // MOBA_BLOCK

    def coords(t):
        bh, j = t // per_head, t % per_head
        b, h = bh // n_heads, bh % n_heads
        return b, h, j, sel_ref[(b * MOBA_TOPK + j // pages_per_block) * n_heads + h]

    def page_copies(t, slot):
        b, _, j, n = coords(t)
        page = pt_ref[b * n_pages + n * pages_per_block + j % pages_per_block]
        return (pltpu.make_async_copy(k_hbm.at[layer, page], kbuf.at[slot], ksem.at[slot]),
                pltpu.make_async_copy(v_hbm.at[layer, page], vbuf.at[slot], vsem.at[slot]))

    for t in range(min(depth, total)):
        for c in page_copies(t, t):
            c.start()

    @pl.loop(0, total)
    def _(t):
        slot = t % depth
        b, h, j, n = coords(t)
        for c in page_copies(t, slot):
            c.wait()
        q = q_ref[b, pl.ds(h, 1), :]

        @pl.when(j == 0)
        def _():
            m_ref[...] = jnp.sum(q * kn_ref[b, pl.ds(h, 1), :], axis=1, keepdims=True) * scale
            l_ref[...] = jnp.ones_like(l_ref)
            acc_ref[...] = vn_ref[b, pl.ds(h, 1), :]

        k = kbuf[slot, pl.ds(h, PAGE_SIZE, stride=n_heads), :]
        v = vbuf[slot, pl.ds(h, PAGE_SIZE, stride=n_heads), :]
        slope = jnp.exp2(jnp.asarray(h + 1, F32) * (-8.0 / n_heads) + jnp.zeros((1, 1), F32))
        kpos = (n * MOBA_BLOCK + (j % pages_per_block) * PAGE_SIZE
                + lax.broadcasted_iota(jnp.int32, (PAGE_SIZE, 1), 0))
        dist = (q_pos - kpos).astype(F32)
        s = jnp.sum(k * q, axis=1, keepdims=True) * scale - slope * dist
        s = jnp.where((dist >= 0.0) & (n < own), s, MASKED)
        m = m_ref[...]
        m_new = jnp.maximum(m, jnp.max(s, axis=0, keepdims=True))
        alpha = jnp.exp(m - m_new)
        p = jnp.exp(s - m_new)
        l_ref[...] = alpha * l_ref[...] + jnp.sum(p, axis=0, keepdims=True)
        acc_ref[...] = alpha * acc_ref[...] + jnp.sum(p * v, axis=0, keepdims=True)
        m_ref[...] = m_new

        @pl.when(j == per_head - 1)
        def _():
            o_ref[b, pl.ds(h, 1), :] = acc_ref[...] / l_ref[...]

        @pl.when(t + depth < total)
        def _():
            for c in page_copies(t + depth, slot):
                c.start()


def attn_step(cache_k, cache_v, layer, page_table, sel, q, k_new, v_new):
    bd, n_pages = page_table.shape
    n_heads = q.shape[1]
    pages_per_block = MOBA_BLOCK // PAGE_SIZE
    assert n_pages % pages_per_block == 0
    rows = PAGE_SIZE * n_heads
    as_rows = lambda c: c.reshape(c.shape[0], c.shape[1], rows, HEAD_DIM)
    vmem, smem = pl.BlockSpec(memory_space=pltpu.VMEM), pl.BlockSpec(memory_space=pltpu.SMEM)
    hbm = pl.BlockSpec(memory_space=pl.ANY)
    return pl.pallas_call(
        functools.partial(_attn_step_kernel, layer=layer, n_pages=n_pages, q_pos=n_pages * PAGE_SIZE,
                          pages_per_block=pages_per_block),
        in_specs=[smem, smem, vmem, vmem, vmem, hbm, hbm],
        out_specs=vmem,
        out_shape=jax.ShapeDtypeStruct((bd, n_heads, HEAD_DIM), F32),
        scratch_shapes=[pltpu.VMEM((PAGES_IN_FLIGHT, rows, HEAD_DIM), F32),
                        pltpu.VMEM((PAGES_IN_FLIGHT, rows, HEAD_DIM), F32),
                        pltpu.SemaphoreType.DMA((PAGES_IN_FLIGHT,)), pltpu.SemaphoreType.DMA((PAGES_IN_FLIGHT,)),
                        pltpu.VMEM((1, 1), F32), pltpu.VMEM((1, 1), F32), pltpu.VMEM((1, HEAD_DIM), F32)],
        compiler_params=pltpu.CompilerParams(vmem_limit_bytes=VMEM_LIMIT),
        name="attn_step",
    )(page_table.reshape(-1), sel.reshape(-1), q, k_new, v_new, as_rows(cache_k), as_rows(cache_v))


P_HALO = 16


def _pool_out_kernel(o_ref, u_ref, x_ref, pw_ref, ps_ref, wo_ref, y_ref, newp_ref, uext, mix):
    l = pl.program_id(1)
    tl = u_ref.shape[1]
    wp = u_ref.shape[2]
    wo_attn = o_ref.shape[2]
    pg = wp // len(POOL_WINDOWS)

    @pl.when(l == 0)
    def _():
        uext[0:P_HALO, :] = jnp.zeros((P_HALO, wp), F32)

    @pl.when(l > 0)
    def _():
        uext[0:P_HALO, :] = uext[tl:tl + P_HALO, :]

    uext[P_HALO:P_HALO + tl, :] = u_ref[0]
    mix[:, 0:wo_attn] = o_ref[0].astype(BF16)
    pos = l * tl + lax.broadcasted_iota(jnp.int32, (tl, pg), 0)
    for g, w in enumerate(POOL_WINDOWS):
        c0 = g * pg
        wsum = uext[P_HALO:P_HALO + tl, c0:c0 + pg]
        for j in range(1, w):
            wsum = wsum + uext[P_HALO - j:P_HALO - j + tl, c0:c0 + pg]
        count = jnp.minimum(pos + 1, w).astype(F32)
        dlt = (wsum / count - uext[P_HALO:P_HALO + tl, c0:c0 + pg]).astype(BF16)
        yp = jnp.dot(dlt, pw_ref[g], preferred_element_type=F32) * ps_ref[:, c0:c0 + pg]
        mix[:, wo_attn + c0:wo_attn + c0 + pg] = yp.astype(BF16)

    y_ref[0] = x_ref[0] + jnp.dot(mix[...], wo_ref[...], preferred_element_type=F32)

    @pl.when(l == pl.num_programs(1) - 1)
    def _():
        newp_ref[0] = uext[P_HALO + tl - (POOL_MAX - 1):P_HALO + tl, :]


def pool_out(o, u, x, pool_w, pool_scale, w_out, tl):
    bsz, seq, d = x.shape
    wp = u.shape[-1]
    tl = min(tl, seq)
    assert seq % tl == 0 and tl >= P_HALO
    const = lambda shape: pl.BlockSpec(shape, lambda b, l: (0,) * len(shape))
    return pl.pallas_call(
        _pool_out_kernel,
        grid=(bsz, seq // tl),
        in_specs=[pl.BlockSpec((1, tl, o.shape[-1]), lambda b, l: (b, l, 0)),
                  pl.BlockSpec((1, tl, wp), lambda b, l: (b, l, 0)),
                  pl.BlockSpec((1, tl, d), lambda b, l: (b, l, 0)),
                  const(pool_w.shape), const((1, wp)), const(w_out.shape)],
        out_specs=[pl.BlockSpec((1, tl, d), lambda b, l: (b, l, 0)),
                   pl.BlockSpec((1, POOL_MAX - 1, wp), lambda b, l: (b, 0, 0))],
        out_shape=[jax.ShapeDtypeStruct((bsz, seq, d), F32),
                   jax.ShapeDtypeStruct((bsz, POOL_MAX - 1, wp), F32)],
        scratch_shapes=[pltpu.VMEM((P_HALO + tl, wp), F32), pltpu.VMEM((tl, o.shape[-1] + wp), BF16)],
        compiler_params=_params("parallel", "arbitrary"),
        name="pool_out",
    )(o, u, x, pool_w, pool_scale.reshape(1, wp), w_out)


def _pool_step_kernel(o_ref, u_ref, x_ref, stp_ref, pw_ref, ps_ref, wo_ref, y_ref, *, start_pos):
    wp = u_ref.shape[1]
    pg = wp // len(POOL_WINDOWS)
    u = u_ref[...]
    parts = [o_ref[...].astype(BF16)]
    for g, w in enumerate(POOL_WINDOWS):
        c0 = g * pg
        wsum = u[:, c0:c0 + pg]
        for j in range(1, w):
            wsum = wsum + stp_ref[POOL_MAX - 1 - j][:, c0:c0 + pg]
        count = float(min(start_pos + 1, w))
        dlt = (wsum / count - u[:, c0:c0 + pg]).astype(BF16)
        yp = jnp.dot(dlt, pw_ref[g], preferred_element_type=F32) * ps_ref[:, c0:c0 + pg]
        parts.append(yp.astype(BF16))
    mix = jnp.concatenate(parts, axis=-1)
    y_ref[...] = x_ref[...] + jnp.dot(mix, wo_ref[...], preferred_element_type=F32)


def pool_step(o, u, x, st_p, pool_w, pool_scale, w_out, start_pos):
    bd, d = x.shape
    return pl.pallas_call(
        functools.partial(_pool_step_kernel, start_pos=start_pos),
        out_shape=jax.ShapeDtypeStruct((bd, d), F32),
        compiler_params=pltpu.CompilerParams(vmem_limit_bytes=VMEM_LIMIT),
        name="pool_step",
    )(o, u, x, st_p, pool_w, pool_scale.reshape(1, -1), w_out)


TM_PROMPT = 512
TF = 512
TL_MIX = 256


def kernel(x_prompt, x_sample, state_conv_a, state_conv_b, cache_k, cache_v, state_pool, page_table, conv_norm_g, conv_w_in, conv_a_dw, conv_a_dw_b, conv_a_ln_g, conv_a_ln_b, conv_b_dw, conv_w_out, attn_norm_g, attn_w_in, pool_w, pool_scale, attn_w_out, ffn_norm_g, ffn_w_gate, ffn_w_up, ffn_w_down, final_norm_g):
    bsz, seq, d = x_prompt.shape
    bd, dec_seq, _ = x_sample.shape
    assert dec_seq == 1
    depth = ffn_norm_g.shape[0]
    n_heads = cache_k.shape[3]
    wattn = n_heads * HEAD_DIM
    past_len = page_table.shape[1] * PAGE_SIZE

    to_bf16 = lambda w: w.astype(BF16)
    conv_w_in_b, conv_w_out_b = to_bf16(conv_w_in), to_bf16(conv_w_out)
    attn_w_in_b, attn_w_out_b, pool_w_b = to_bf16(attn_w_in), to_bf16(attn_w_out), to_bf16(pool_w)
    wg_b, wu_b, wd_b = to_bf16(ffn_w_gate), to_bf16(ffn_w_up), to_bf16(ffn_w_down)

    xp = x_prompt.reshape(bsz * seq, d)
    xs = x_sample.reshape(bd, d)
    pa, pb, pk, pv, pp = [], [], [], [], []
    sa, sb, sk, sv, sp = [], [], [], [], []
    for layer in range(depth):
        i = layer // 2
        if layer % 2 == 0:
            parts = norm_matmul(xp, conv_norm_g[i], conv_w_in_b[i], TM_PROMPT)
            wa = parts[0].shape[-1]
            y, na, nb_ = conv_mix([t.reshape(bsz, seq, wa) for t in parts], xp.reshape(bsz, seq, d),
                                  conv_a_dw[i], conv_a_dw_b[i], conv_a_ln_g[i], conv_a_ln_b[i], conv_b_dw[i],
                                  conv_w_out_b[i], TL_MIX)
            xp = y.reshape(bsz * seq, d)
            pa.append(na)
            pb.append(nb_)

            parts = norm_matmul(xs, conv_norm_g[i], conv_w_in_b[i], TM_PROMPT)
            st_a = jnp.swapaxes(state_conv_a[i], 0, 1)
            st_b = jnp.swapaxes(state_conv_b[i], 0, 1)
            xs, a_new, c_new = conv_step(parts, xs, st_a, st_b, conv_a_dw[i], conv_a_dw_b[i], conv_a_ln_g[i],
                                         conv_a_ln_b[i], conv_b_dw[i], conv_w_out_b[i])
            sa.append(jnp.concatenate([state_conv_a[i][:, 1:], a_new[:, None]], axis=1))
            sb.append(jnp.concatenate([state_conv_b[i][:, 1:], c_new[:, None]], axis=1))
        else:
            q, k, v, u = norm_matmul(xp, attn_norm_g[i], attn_w_in_b[i], TM_PROMPT)
            shp = (bsz, seq, wattn)
            o = moba_prompt(q.reshape(shp), k.reshape(shp), v.reshape(shp))
            y, npool = pool_out(o, u.reshape(bsz, seq, -1), xp.reshape(bsz, seq, d), pool_w_b[i], pool_scale[i],
                                attn_w_out_b[i], TL_MIX)
            xp = y.reshape(bsz * seq, d)
            pk.append(k.reshape(bsz, seq, n_heads, HEAD_DIM))
            pv.append(v.reshape(bsz, seq, n_heads, HEAD_DIM))
            pp.append(npool)

            q, k, v, u = norm_matmul(xs, attn_norm_g[i], attn_w_in_b[i], TM_PROMPT)
            heads = lambda t: t.reshape(bd, n_heads, HEAD_DIM)
            sel = gate_step(cache_k, i, page_table, heads(q))
            o = attn_step(cache_k, cache_v, i, page_table, sel, heads(q), heads(k), heads(v))
            st_p = jnp.swapaxes(state_pool[i], 0, 1)
            xs = pool_step(o.reshape(bd, wattn), u, xs, st_p, pool_w_b[i], pool_scale[i], attn_w_out_b[i], past_len)
            sk.append(k.reshape(bd, 1, n_heads, HEAD_DIM))
            sv.append(v.reshape(bd, 1, n_heads, HEAD_DIM))
            sp.append(jnp.concatenate([state_pool[i][:, 1:], u[:, None]], axis=1))
        last = layer == depth - 1
        xp = ffn(xp, ffn_norm_g[layer], wg_b[layer], wu_b[layer], wd_b[layer], final_norm_g, TM_PROMPT, TF, last)
        xs = ffn(xs, ffn_norm_g[layer], wg_b[layer], wu_b[layer], wd_b[layer], final_norm_g, TM_PROMPT, TF, last)
    return (xp.reshape(bsz, seq, d), xs.reshape(bd, 1, d), jnp.stack(pa), jnp.stack(sa), jnp.stack(pb),
            jnp.stack(sb), jnp.stack(pk), jnp.stack(sk), jnp.stack(pv), jnp.stack(sv), jnp.stack(pp), jnp.stack(sp))
```

```python
import functools

import jax
import jax.numpy as jnp
from jax import lax
from jax.experimental import pallas as pl
from jax.experimental.pallas import tpu as pltpu

EPS = 1e-6
PAGE_SIZE = 128
MOBA_BLOCK = 256
MOBA_TOPK = 3
HEAD_DIM = 128
POOL_WINDOWS = (2, 4, 8, 16)
POOL_MAX = max(POOL_WINDOWS)
MASKED = -1e30

V7X_VMEM_BYTES = 64 * 1024 * 1024
VMEM_LIMIT = V7X_VMEM_BYTES - 8 * 1024 * 1024
LANES = 128
SUBLANES = 8
COL_GROUP = 1024

BF16 = jnp.bfloat16
F32 = jnp.float32


def _params(*sem):
    return pltpu.CompilerParams(dimension_semantics=sem, vmem_limit_bytes=VMEM_LIMIT)


def _rms(x, g):
    ms = jnp.mean(x * x, axis=-1, keepdims=True)
    return x * lax.rsqrt(ms + EPS) * g


def _sigmoid(x):
    return 1.0 / (1.0 + jnp.exp(-x))


def _resident(shape):
    return pl.BlockSpec(shape, lambda *_: (0,) * len(shape), pipeline_mode=pl.Buffered(1))


def _norm_matmul_kernel(x_ref, g_ref, w_ref, *out_refs):
    h = _rms(x_ref[...], g_ref[...]).astype(BF16)
    for c, o_ref in enumerate(out_refs):
        o_ref[...] = jnp.dot(h, w_ref[:, c * COL_GROUP:(c + 1) * COL_GROUP], preferred_element_type=F32)


def norm_matmul(x, g, w, tm):
    m, d = x.shape
    n_out = w.shape[1] // COL_GROUP
    tm = min(tm, m)
    return pl.pallas_call(
        _norm_matmul_kernel,
        grid=(m // tm,),
        in_specs=[pl.BlockSpec((tm, d), lambda i: (i, 0)), _resident((1, d)), _resident(w.shape)],
        out_specs=[pl.BlockSpec((tm, COL_GROUP), lambda i: (i, 0))] * n_out,
        out_shape=[jax.ShapeDtypeStruct((m, COL_GROUP), F32)] * n_out,
        compiler_params=_params("parallel"),
        name="norm_matmul",
    )(x, g.reshape(1, d), w)


def _ffn_kernel(x_ref, g_ref, wg_ref, wu_ref, wd_ref, fg_ref, o_ref, h_ref, acc_ref, *, final_norm):
    f = pl.program_id(1)

    @pl.when(f == 0)
    def _():
        h_ref[...] = _rms(x_ref[...], g_ref[...]).astype(h_ref.dtype)
        acc_ref[...] = jnp.zeros_like(acc_ref)

    h = h_ref[...]
    gate = jnp.dot(h, wg_ref[...], preferred_element_type=F32)
    up = jnp.dot(h, wu_ref[...], preferred_element_type=F32)
    act = (gate * _sigmoid(gate) * up).astype(BF16)
    acc_ref[...] += jnp.dot(act, wd_ref[...], preferred_element_type=F32)

    @pl.when(f == pl.num_programs(1) - 1)
    def _():
        y = x_ref[...] + acc_ref[...]
        if final_norm:
            y = _rms(y, fg_ref[...])
        o_ref[...] = y


def ffn(x, g, wg, wu, wd, final_g, tm, tf, final_norm):
    m, d = x.shape
    dff = wg.shape[1]
    tm = min(tm, m)
    return pl.pallas_call(
        functools.partial(_ffn_kernel, final_norm=final_norm),
        grid=(m // tm, dff // tf),
        in_specs=[pl.BlockSpec((tm, d), lambda i, f: (i, 0)),
                  pl.BlockSpec((1, d), lambda i, f: (0, 0)),
                  pl.BlockSpec((d, tf), lambda i, f: (0, f)),
                  pl.BlockSpec((d, tf), lambda i, f: (0, f)),
                  pl.BlockSpec((tf, d), lambda i, f: (f, 0)),
                  pl.BlockSpec((1, d), lambda i, f: (0, 0))],
        out_specs=pl.BlockSpec((tm, d), lambda i, f: (i, 0)),
        out_shape=jax.ShapeDtypeStruct((m, d), F32),
        scratch_shapes=[pltpu.VMEM((tm, d), BF16), pltpu.VMEM((tm, d), F32)],
        compiler_params=_params("parallel", "arbitrary"),
        name="ffn",
    )(x, g.reshape(1, d), wg, wu, wd, final_g.reshape(1, d))


A_HALO = 32
B_HALO = 8


def _layer_norm_silu(x, g, b):
    mu = jnp.mean(x, axis=-1, keepdims=True)
    xc = x - mu
    y = xc * lax.rsqrt(jnp.mean(xc * xc, axis=-1, keepdims=True) + EPS) * g + b
    return y * _sigmoid(y)


def _conv_mix_kernel(av_ref, ag_ref, bh_ref, bb_ref, bc_ref, x_ref, adw_ref, adwb_ref, lng_ref, lnb_ref,
                     bdw_ref, wo_ref, y_ref, newa_ref, newb_ref, aext, cext, ashift, aconv, mix, *, ta, tb, rc):
    l = pl.program_id(1)
    tl = av_ref.shape[1]
    wa = av_ref.shape[2]

    @pl.when(l == 0)
    def _():
        aext[0:A_HALO, :] = jnp.zeros((A_HALO, wa), F32)
        cext[0:B_HALO, :] = jnp.zeros((B_HALO, wa), F32)

    @pl.when(l > 0)
    def _():
        aext[0:A_HALO, :] = aext[tl:tl + A_HALO, :]
        cext[0:B_HALO, :] = cext[tl:tl + B_HALO, :]

    aext[A_HALO:A_HALO + tl, :] = av_ref[0] * _sigmoid(ag_ref[0])
    cext[B_HALO:B_HALO + tl, :] = bc_ref[0] * bh_ref[0]

    n_shift = ashift.shape[1]
    for s in range(1, SUBLANES):
        ashift[s - 1] = aext[s:s + n_shift, :]

    for r0 in range(0, tl, rc):
        for c0 in range(0, wa, LANES):
            acc = jnp.broadcast_to(adwb_ref[:, c0:c0 + LANES], (rc, LANES))
            for k in range(ta):
                off = A_HALO - (ta - 1) + r0 + k
                s = off % SUBLANES
                src = aext if s == 0 else ashift.at[s - 1]
                acc = acc + adw_ref[k:k + 1, c0:c0 + LANES] * src[off - s:off - s + rc, c0:c0 + LANES]
            aconv[r0:r0 + rc, c0:c0 + LANES] = acc
    mix[:, 0:wa] = _layer_norm_silu(aconv[...], lng_ref[...], lnb_ref[...]).astype(BF16)

    bconv = jnp.zeros((tl, wa), F32)
    for k in range(tb):
        off = B_HALO - (tb - 1) + k
        bconv = bconv + bdw_ref[k:k + 1, :] * cext[off:off + tl, :]
    mix[:, wa:] = (bb_ref[0] * bconv).astype(BF16)

    y_ref[0] = x_ref[0] + jnp.dot(mix[...], wo_ref[...], preferred_element_type=F32)

    @pl.when(l == pl.num_programs(1) - 1)
    def _():
        newa_ref[0] = aext[A_HALO + tl - (ta - 1):A_HALO + tl, :]
        newb_ref[0] = cext[B_HALO + tl - (tb - 1):B_HALO + tl, :]


def conv_mix(parts, x, a_dw, a_dw_b, ln_g, ln_b, b_dw, w_out, tl):
    bsz, seq, d = x.shape
    wa = parts[0].shape[-1]
    ta, tb = a_dw.shape[0], b_dw.shape[0]
    tl = min(tl, seq)
    assert seq % tl == 0 and tl >= A_HALO and ta - 1 <= A_HALO and tb - 1 <= B_HALO
    part_spec = pl.BlockSpec((1, tl, wa), lambda b, l: (b, l, 0))
    const = _resident
    return pl.pallas_call(
        functools.partial(_conv_mix_kernel, ta=ta, tb=tb, rc=min(128, tl)),
        grid=(bsz, seq // tl),
        in_specs=[part_spec] * 5 + [pl.BlockSpec((1, tl, d), lambda b, l: (b, l, 0)),
                                    const((ta, wa)), const((1, wa)), const((1, wa)), const((1, wa)),
                                    const((tb, wa)), const(w_out.shape)],
        out_specs=[pl.BlockSpec((1, tl, d), lambda b, l: (b, l, 0)),
                   pl.BlockSpec((1, ta - 1, wa), lambda b, l: (b, 0, 0)),
                   pl.BlockSpec((1, tb - 1, wa), lambda b, l: (b, 0, 0))],
        out_shape=[jax.ShapeDtypeStruct((bsz, seq, d), F32),
                   jax.ShapeDtypeStruct((bsz, ta - 1, wa), F32),
                   jax.ShapeDtypeStruct((bsz, tb - 1, wa), F32)],
        scratch_shapes=[pltpu.VMEM((A_HALO + tl, wa), F32), pltpu.VMEM((B_HALO + tl, wa), F32),
                        pltpu.VMEM((SUBLANES - 1, A_HALO + tl - SUBLANES, wa), F32),
                        pltpu.VMEM((tl, wa), F32), pltpu.VMEM((tl, 2 * wa), BF16)],
        compiler_params=_params("parallel", "arbitrary"),
        name="conv_mix",
    )(*parts, x, a_dw, a_dw_b.reshape(1, wa), ln_g.reshape(1, wa), ln_b.reshape(1, wa), b_dw, w_out)


def _conv_step_kernel(av_ref, ag_ref, bh_ref, bb_ref, bc_ref, x_ref, sta_ref, stb_ref, adw_ref, adwb_ref,
                      lng_ref, lnb_ref, bdw_ref, wo_ref, y_ref, anew_ref, cnew_ref, *, ta, tb):
    a = av_ref[...] * _sigmoid(ag_ref[...])
    c = bc_ref[...] * bh_ref[...]
    anew_ref[...] = a
    cnew_ref[...] = c
    acc = adwb_ref[...] + adw_ref[ta - 1:ta, :] * a
    for k in range(ta - 1):
        acc = acc + adw_ref[k:k + 1, :] * sta_ref[k]
    a_out = _layer_norm_silu(acc, lng_ref[...], lnb_ref[...])
    bconv = bdw_ref[tb - 1:tb, :] * c
    for k in range(tb - 1):
        bconv = bconv + bdw_ref[k:k + 1, :] * stb_ref[k]
    mix = jnp.concatenate([a_out, bb_ref[...] * bconv], axis=-1).astype(BF16)
    y_ref[...] = x_ref[...] + jnp.dot(mix, wo_ref[...], preferred_element_type=F32)


def conv_step(parts, x, st_a, st_b, a_dw, a_dw_b, ln_g, ln_b, b_dw, w_out):
    bd, d = x.shape
    wa = parts[0].shape[-1]
    ta, tb = a_dw.shape[0], b_dw.shape[0]
    return pl.pallas_call(
        functools.partial(_conv_step_kernel, ta=ta, tb=tb),
        out_shape=[jax.ShapeDtypeStruct((bd, d), F32), jax.ShapeDtypeStruct((bd, wa), F32),
                   jax.ShapeDtypeStruct((bd, wa), F32)],
        compiler_params=pltpu.CompilerParams(vmem_limit_bytes=VMEM_LIMIT),
        name="conv_step",
    )(*parts, x, st_a, st_b, a_dw, a_dw_b.reshape(1, wa), ln_g.reshape(1, wa), ln_b.reshape(1, wa), b_dw, w_out)


LOG2E = 1.4426950408889634
TQ = 2 * MOBA_BLOCK
DA = 2 * HEAD_DIM


def _split3(x):
    hi = x.astype(BF16).astype(F32)
    mid = (x - hi).astype(BF16).astype(F32)
    lo = (x - hi - mid).astype(BF16).astype(F32)
    return hi, mid, lo


def _moba_setup(q_ref, k_ref, v_ref, qaug, kaug, vaug, kmean, slope2, nb):
    seq = k_ref.shape[1]
    bs = MOBA_BLOCK
    nbp = kmean.shape[0]
    k = k_ref[0]
    q = q_ref[0]
    kmean[...] = jnp.zeros_like(kmean)
    kmean[0:nb, :] = jnp.mean(k.reshape(nb, bs, HEAD_DIM), axis=1)

    gate = lax.dot_general(kmean[...], q, (((1,), (1,)), ((), ())), precision=lax.Precision.HIGHEST,
                           preferred_element_type=F32)
    blk = lax.broadcasted_iota(jnp.int32, (nbp, seq), 0)
    own = lax.broadcasted_iota(jnp.int32, (nbp, seq), 1) // bs
    gate = jnp.where(blk < own, gate, -jnp.inf)
    attend = blk == own
    for _ in range(MOBA_TOPK):
        m = jnp.max(gate, axis=0, keepdims=True)
        idx = jnp.min(jnp.where(gate == m, blk, nbp), axis=0, keepdims=True)
        hit = blk == idx
        attend = attend | (hit & (blk < own))
        gate = jnp.where(hit, -jnp.inf, gate)
    attend_t = jnp.where(attend, 1.0, 0.0)
    attend_r = jnp.concatenate([attend_t, jnp.zeros((LANES - nbp, seq), F32)], axis=0).T

    lane = lax.broadcasted_iota(jnp.int32, (seq, LANES), 1)
    pos_i = lax.broadcasted_iota(jnp.int32, (seq, LANES), 0)
    pos = pos_i.astype(F32)

    khi, kmid, klo = _split3(slope2 * pos)
    ek = jnp.where(lane == pos_i // bs, 1.0, 0.0)
    ek = jnp.where(lane == nb, khi, ek)
    ek = jnp.where(lane == nb + 1, kmid, ek)
    ek = jnp.where(lane == nb + 2, klo, ek)
    ek = jnp.where((lane >= nb + 3) & (lane < nb + 6), 1.0, ek)
    kaug[:, 0:HEAD_DIM] = k.astype(BF16)
    kaug[:, HEAD_DIM:] = ek.astype(BF16)

    qhi, qmid, qlo = _split3(-slope2 * pos)
    eq = jnp.where(lane < nb, jnp.where(attend_r > 0.5, 0.0, MASKED), 0.0)
    eq = jnp.where((lane >= nb) & (lane < nb + 3), 1.0, eq)
    eq = jnp.where(lane == nb + 3, qhi, eq)
    eq = jnp.where(lane == nb + 4, qmid, eq)
    eq = jnp.where(lane == nb + 5, qlo, eq)
    qaug[:, 0:HEAD_DIM] = (q * (HEAD_DIM ** -0.5 * LOG2E)).astype(BF16)
    qaug[:, HEAD_DIM:] = eq.astype(BF16)

    vaug[:, 0:HEAD_DIM] = v_ref[0].astype(BF16)
    vaug[:, HEAD_DIM:] = jnp.ones((seq, HEAD_DIM), BF16)


def _moba_kernel(q_ref, k_ref, v_ref, o_ref, qaug, kaug, vaug, kmean, m_ref, acc_ref, s_ref, *, n_heads, nb):
    h = pl.program_id(1)
    i = pl.program_id(2)
    bs = MOBA_BLOCK
    nt = (((1,), (1,)), ((), ()))

    @pl.when(i == 0)
    def _():
        slope2 = jnp.exp2((h + 1).astype(F32) * (-8.0 / n_heads) + jnp.zeros((1, 1), F32)) * LOG2E
        _moba_setup(q_ref, k_ref, v_ref, qaug, kaug, vaug, kmean, slope2, nb)

    r0 = pl.multiple_of(i * TQ, TQ)
    qa = qaug[pl.ds(r0, TQ), :]

    halves = (slice(0, bs), slice(bs, TQ))

    def scores(g):
        kg = kaug[pl.ds(pl.multiple_of(g * TQ, TQ), TQ), :]
        return jnp.concatenate([lax.dot_general(qa[r], kg, nt, preferred_element_type=F32) for r in halves], axis=0)

    def update(s, g):
        vg = vaug[pl.ds(pl.multiple_of(g * TQ, TQ), TQ), :]
        for r in halves:
            m = m_ref[r, :]
            m_new = jnp.maximum(m, jnp.max(s[r], axis=1, keepdims=True))
            p = jnp.exp2(s[r] - m_new).astype(BF16)
            pv = jnp.dot(p, vg, preferred_element_type=F32)
            acc_ref[r, :] = jnp.exp2(m - m_new) * acc_ref[r, :] + pv
            m_ref[r, :] = m_new

    m_ref[...] = jnp.full(m_ref.shape, MASKED, F32)
    acc_ref[...] = jnp.zeros_like(acc_ref)
    s_ref[0] = scores(0)

    def past_group(g, slot):
        s_ref[1 - slot] = scores(g + 1)
        update(s_ref[slot], g)

    @pl.loop(0, i // 2)
    def _(j):
        past_group(2 * j, 0)
        past_group(2 * j + 1, 1)

    def own_group(slot):
        causal = (lax.broadcasted_iota(jnp.int32, (TQ, TQ), 0) >= lax.broadcasted_iota(jnp.int32, (TQ, TQ), 1))
        update(jnp.where(causal, s_ref[slot], MASKED), i)
        o_ref[0] = acc_ref[:, 0:HEAD_DIM] / acc_ref[:, HEAD_DIM:]

    @pl.when(i % 2 == 0)
    def _():
        own_group(0)

    @pl.when(i % 2 == 1)
    def _():
        past_group(i - 1, 0)
        own_group(1)


def moba_prompt(q, k, v):
    bsz, seq, width = q.shape
    n_heads = width // HEAD_DIM
    assert seq % TQ == 0
    nb = seq // MOBA_BLOCK
    nbp = -(-nb // 8) * 8
    assert nb + 6 <= LANES
    full = pl.BlockSpec((1, seq, HEAD_DIM), lambda b, h, i: (b, 0, h))
    tile = pl.BlockSpec((1, TQ, HEAD_DIM), lambda b, h, i: (b, i, h))
    return pl.pallas_call(
        functools.partial(_moba_kernel, n_heads=n_heads, nb=nb),
        grid=(bsz, n_heads, seq // TQ),
        in_specs=[full, full, full],
        out_specs=tile,
        out_shape=jax.ShapeDtypeStruct((bsz, seq, width), F32),
        scratch_shapes=[pltpu.VMEM((seq, DA), BF16), pltpu.VMEM((seq, DA), BF16), pltpu.VMEM((seq, DA), BF16),
                        pltpu.VMEM((nbp, HEAD_DIM), F32), pltpu.VMEM((TQ, 1), F32), pltpu.VMEM((TQ, DA), F32),
                        pltpu.VMEM((2, TQ, TQ), F32)],
        compiler_params=_params("parallel", "parallel", "arbitrary"),
        name="moba_prompt",
    )(q, k, v)


PAGES_IN_FLIGHT = 8
SUM_CHAINS = 4


def _gate_step_kernel(pt_ref, q_ref, k_hbm, sel_ref, buf, sem, ksum, *, layer, pages_per_block, n_heads):
    total = pt_ref.shape[0]
    bd = q_ref.shape[0]
    nblk = ksum.shape[0] // bd
    depth = buf.shape[0]

    def page_copy(t, slot):
        return pltpu.make_async_copy(k_hbm.at[layer, pt_ref[t]], buf.at[slot], sem.at[slot])

    for t in range(min(depth, total)):
        page_copy(t, t).start()
    ksum[...] = jnp.zeros_like(ksum)

    @pl.loop(0, total)
    def _(t):
        slot = t % depth
        page_copy(t, slot).wait()
        page = buf[slot].reshape(SUM_CHAINS, PAGE_SIZE // SUM_CHAINS, n_heads, HEAD_DIM)
        ksum[t // pages_per_block] += jnp.sum(jnp.sum(page, axis=1), axis=0)

        @pl.when(t + depth < total)
        def _():
            page_copy(t + depth, slot).start()

    sel_ref[...] = jnp.zeros_like(sel_ref)
    for b in range(bd):
        kmean = ksum[b * nblk:(b + 1) * nblk] * (1.0 / MOBA_BLOCK)
        gate = jnp.sum(kmean * q_ref[b][None], axis=-1)
        blk = lax.broadcasted_iota(jnp.int32, gate.shape, 0)
        for r in range(MOBA_TOPK):
            m = jnp.max(gate, axis=0, keepdims=True)
            idx = jnp.min(jnp.where(gate == m, blk, nblk), axis=0, keepdims=True)
            sel_ref[b, r:r + 1, 0:n_heads] = idx
            gate = jnp.where(blk == idx, -jnp.inf, gate)


SEL_ROWS = 8


def gate_step(cache_k, layer, page_table, q):
    bd, n_pages = page_table.shape
    n_heads = q.shape[1]
    pages_per_block = MOBA_BLOCK // PAGE_SIZE
    assert n_pages % pages_per_block == 0 and MOBA_TOPK <= SEL_ROWS and n_heads <= LANES
    nblk = n_pages // pages_per_block
    sel = pl.pallas_call(
        functools.partial(_gate_step_kernel, layer=layer, pages_per_block=pages_per_block, n_heads=n_heads),
        in_specs=[pl.BlockSpec(memory_space=pltpu.SMEM), pl.BlockSpec(memory_space=pltpu.VMEM),
                  pl.BlockSpec(memory_space=pl.ANY)],
        out_specs=pl.BlockSpec(memory_space=pltpu.VMEM),
        out_shape=jax.ShapeDtypeStruct((bd, SEL_ROWS, LANES), jnp.int32),
        scratch_shapes=[pltpu.VMEM((PAGES_IN_FLIGHT, PAGE_SIZE, n_heads, HEAD_DIM), F32),
                        pltpu.SemaphoreType.DMA((PAGES_IN_FLIGHT,)),
                        pltpu.VMEM((bd * nblk, n_heads, HEAD_DIM), F32)],
        compiler_params=pltpu.CompilerParams(vmem_limit_bytes=VMEM_LIMIT),
        name="gate_step",
    )(page_table.reshape(-1), q, cache_k)
    return sel[:, :MOBA_TOPK, :n_heads]


def _attn_step_kernel(pt_ref, sel_ref, q_ref, kn_ref, vn_ref, k_hbm, v_hbm, o_ref, kbuf, vbuf, ksem, vsem,
                      m_ref, l_ref, acc_ref, *, layer, n_pages, q_pos, pages_per_block):
    bd, n_heads, _ = q_ref.shape
    per_head = MOBA_TOPK * pages_per_block
    total = bd * n_heads * per_head
    depth = kbuf.shape[0]
    scale = HEAD_DIM ** -0.5
    own = q_pos // MOBA_BLOCK

    def coords(t):
        bh, j = t // per_head, t % per_head
        b, h = bh // n_heads, bh % n_heads
        return b, h, j, sel_ref[(b * MOBA_TOPK + j // pages_per_block) * n_heads + h]

    def page_copies(t, slot):
        b, _, j, n = coords(t)
        page = pt_ref[b * n_pages + n * pages_per_block + j % pages_per_block]
        return (pltpu.make_async_copy(k_hbm.at[layer, page], kbuf.at[slot], ksem.at[slot]),
                pltpu.make_async_copy(v_hbm.at[layer, page], vbuf.at[slot], vsem.at[slot]))

    for t in range(min(depth, total)):
        for c in page_copies(t, t):
            c.start()

    @pl.loop(0, total)
    def _(t):
        slot = t % depth
        b, h, j, n = coords(t)
        for c in page_copies(t, slot):
            c.wait()
        q = q_ref[b, pl.ds(h, 1), :]

        @pl.when(j == 0)
        def _():
            m_ref[...] = jnp.sum(q * kn_ref[b, pl.ds(h, 1), :], axis=1, keepdims=True) * scale
            l_ref[...] = jnp.ones_like(l_ref)
            acc_ref[...] = vn_ref[b, pl.ds(h, 1), :]

        k = kbuf[slot, pl.ds(h, PAGE_SIZE, stride=n_heads), :]
        v = vbuf[slot, pl.ds(h, PAGE_SIZE, stride=n_heads), :]
        slope = jnp.exp2(jnp.asarray(h + 1, F32) * (-8.0 / n_heads) + jnp.zeros((1, 1), F32))
        kpos = (n * MOBA_BLOCK + (j % pages_per_block) * PAGE_SIZE
                + lax.broadcasted_iota(jnp.int32, (PAGE_SIZE, 1), 0))
        dist = (q_pos - kpos).astype(F32)
        s = jnp.sum(k * q, axis=1, keepdims=True) * scale - slope * dist
        s = jnp.where((dist >= 0.0) & (n < own), s, MASKED)
        m = m_ref[...]
        m_new = jnp.maximum(m, jnp.max(s, axis=0, keepdims=True))
        alpha = jnp.exp(m - m_new)
        p = jnp.exp(s - m_new)
        l_ref[...] = alpha * l_ref[...] + jnp.sum(p, axis=0, keepdims=True)
        acc_ref[...] = alpha * acc_ref[...] + jnp.sum(p * v, axis=0, keepdims=True)
        m_ref[...] = m_new

        @pl.when(j == per_head - 1)
        def _():
            o_ref[b, pl.ds(h, 1), :] = acc_ref[...] / l_ref[...]

        @pl.when(t + depth < total)
        def _():
            for c in page_copies(t + depth, slot):
                c.start()


def attn_step(cache_k, cache_v, layer, page_table, sel, q, k_new, v_new):
    bd, n_pages = page_table.shape
    n_heads = q.shape[1]
    pages_per_block = MOBA_BLOCK // PAGE_SIZE
    assert n_pages % pages_per_block == 0
    rows = PAGE_SIZE * n_heads
    as_rows = lambda c: c.reshape(c.shape[0], c.shape[1], rows, HEAD_DIM)
    vmem, smem = pl.BlockSpec(memory_space=pltpu.VMEM), pl.BlockSpec(memory_space=pltpu.SMEM)
    hbm = pl.BlockSpec(memory_space=pl.ANY)
    return pl.pallas_call(
        functools.partial(_attn_step_kernel, layer=layer, n_pages=n_pages, q_pos=n_pages * PAGE_SIZE,
                          pages_per_block=pages_per_block),
        in_specs=[smem, smem, vmem, vmem, vmem, hbm, hbm],
        out_specs=vmem,
        out_shape=jax.ShapeDtypeStruct((bd, n_heads, HEAD_DIM), F32),
        scratch_shapes=[pltpu.VMEM((PAGES_IN_FLIGHT, rows, HEAD_DIM), F32),
                        pltpu.VMEM((PAGES_IN_FLIGHT, rows, HEAD_DIM), F32),
                        pltpu.SemaphoreType.DMA((PAGES_IN_FLIGHT,)), pltpu.SemaphoreType.DMA((PAGES_IN_FLIGHT,)),
                        pltpu.VMEM((1, 1), F32), pltpu.VMEM((1, 1), F32), pltpu.VMEM((1, HEAD_DIM), F32)],
        compiler_params=pltpu.CompilerParams(vmem_limit_bytes=VMEM_LIMIT),
        name="attn_step",
    )(page_table.reshape(-1), sel.reshape(-1), q, k_new, v_new, as_rows(cache_k), as_rows(cache_v))


P_HALO = 16


def _pool_out_kernel(o_ref, u_ref, x_ref, pw_ref, ps_ref, wo_ref, y_ref, newp_ref, uext, mix):
    l = pl.program_id(1)
    tl = u_ref.shape[1]
    wp = u_ref.shape[2]
    wo_attn = o_ref.shape[2]
    pg = wp // len(POOL_WINDOWS)

    @pl.when(l == 0)
    def _():
        uext[0:P_HALO, :] = jnp.zeros((P_HALO, wp), F32)

    @pl.when(l > 0)
    def _():
        uext[0:P_HALO, :] = uext[tl:tl + P_HALO, :]

    uext[P_HALO:P_HALO + tl, :] = u_ref[0]
    mix[:, 0:wo_attn] = o_ref[0].astype(BF16)
    pos = l * tl + lax.broadcasted_iota(jnp.int32, (tl, pg), 0)
    for g, w in enumerate(POOL_WINDOWS):
        c0 = g * pg
        wsum = uext[P_HALO:P_HALO + tl, c0:c0 + pg]
        for j in range(1, w):
            wsum = wsum + uext[P_HALO - j:P_HALO - j + tl, c0:c0 + pg]
        count = jnp.minimum(pos + 1, w).astype(F32)
        dlt = (wsum / count - uext[P_HALO:P_HALO + tl, c0:c0 + pg]).astype(BF16)
        yp = jnp.dot(dlt, pw_ref[g], preferred_element_type=F32) * ps_ref[:, c0:c0 + pg]
        mix[:, wo_attn + c0:wo_attn + c0 + pg] = yp.astype(BF16)

    y_ref[0] = x_ref[0] + jnp.dot(mix[...], wo_ref[...], preferred_element_type=F32)

    @pl.when(l == pl.num_programs(1) - 1)
    def _():
        newp_ref[0] = uext[P_HALO + tl - (POOL_MAX - 1):P_HALO + tl, :]


def pool_out(o, u, x, pool_w, pool_scale, w_out, tl):
    bsz, seq, d = x.shape
    wp = u.shape[-1]
    tl = min(tl, seq)
    assert seq % tl == 0 and tl >= P_HALO
    const = _resident
    return pl.pallas_call(
        _pool_out_kernel,
        grid=(bsz, seq // tl),
        in_specs=[pl.BlockSpec((1, tl, o.shape[-1]), lambda b, l: (b, l, 0)),
                  pl.BlockSpec((1, tl, wp), lambda b, l: (b, l, 0)),
                  pl.BlockSpec((1, tl, d), lambda b, l: (b, l, 0)),
                  const(pool_w.shape), const((1, wp)), const(w_out.shape)],
        out_specs=[pl.BlockSpec((1, tl, d), lambda b, l: (b, l, 0)),
                   pl.BlockSpec((1, POOL_MAX - 1, wp), lambda b, l: (b, 0, 0))],
        out_shape=[jax.ShapeDtypeStruct((bsz, seq, d), F32),
                   jax.ShapeDtypeStruct((bsz, POOL_MAX - 1, wp), F32)],
        scratch_shapes=[pltpu.VMEM((P_HALO + tl, wp), F32), pltpu.VMEM((tl, o.shape[-1] + wp), BF16)],
        compiler_params=_params("parallel", "arbitrary"),
        name="pool_out",
    )(o, u, x, pool_w, pool_scale.reshape(1, wp), w_out)


def _pool_step_kernel(o_ref, u_ref, x_ref, stp_ref, pw_ref, ps_ref, wo_ref, y_ref, *, start_pos):
    wp = u_ref.shape[1]
    pg = wp // len(POOL_WINDOWS)
    u = u_ref[...]
    parts = [o_ref[...].astype(BF16)]
    for g, w in enumerate(POOL_WINDOWS):
        c0 = g * pg
        wsum = u[:, c0:c0 + pg]
        for j in range(1, w):
            wsum = wsum + stp_ref[POOL_MAX - 1 - j][:, c0:c0 + pg]
        count = float(min(start_pos + 1, w))
        dlt = (wsum / count - u[:, c0:c0 + pg]).astype(BF16)
        yp = jnp.dot(dlt, pw_ref[g], preferred_element_type=F32) * ps_ref[:, c0:c0 + pg]
        parts.append(yp.astype(BF16))
    mix = jnp.concatenate(parts, axis=-1)
    y_ref[...] = x_ref[...] + jnp.dot(mix, wo_ref[...], preferred_element_type=F32)


def pool_step(o, u, x, st_p, pool_w, pool_scale, w_out, start_pos):
    bd, d = x.shape
    return pl.pallas_call(
        functools.partial(_pool_step_kernel, start_pos=start_pos),
        out_shape=jax.ShapeDtypeStruct((bd, d), F32),
        compiler_params=pltpu.CompilerParams(vmem_limit_bytes=VMEM_LIMIT),
        name="pool_step",
    )(o, u, x, st_p, pool_w, pool_scale.reshape(1, -1), w_out)


TM_PROMPT = 512
TF = 512
TL_MIX = 256


def kernel(x_prompt, x_sample, state_conv_a, state_conv_b, cache_k, cache_v, state_pool, page_table, conv_norm_g, conv_w_in, conv_a_dw, conv_a_dw_b, conv_a_ln_g, conv_a_ln_b, conv_b_dw, conv_w_out, attn_norm_g, attn_w_in, pool_w, pool_scale, attn_w_out, ffn_norm_g, ffn_w_gate, ffn_w_up, ffn_w_down, final_norm_g):
    bsz, seq, d = x_prompt.shape
    bd, dec_seq, _ = x_sample.shape
    assert dec_seq == 1
    depth = ffn_norm_g.shape[0]
    n_heads = cache_k.shape[3]
    wattn = n_heads * HEAD_DIM
    past_len = page_table.shape[1] * PAGE_SIZE

    to_bf16 = lambda w: w.astype(BF16)
    conv_w_in_b, conv_w_out_b = to_bf16(conv_w_in), to_bf16(conv_w_out)
    attn_w_in_b, attn_w_out_b, pool_w_b = to_bf16(attn_w_in), to_bf16(attn_w_out), to_bf16(pool_w)
    wg_b, wu_b, wd_b = to_bf16(ffn_w_gate), to_bf16(ffn_w_up), to_bf16(ffn_w_down)

    xp = x_prompt.reshape(bsz * seq, d)
    xs = x_sample.reshape(bd, d)
    pa, pb, pk, pv, pp = [], [], [], [], []
    sa, sb, sk, sv, sp = [], [], [], [], []
    for layer in range(depth):
        i = layer // 2
        if layer % 2 == 0:
            parts = norm_matmul(xp, conv_norm_g[i], conv_w_in_b[i], TM_PROMPT)
            wa = parts[0].shape[-1]
            y, na, nb_ = conv_mix([t.reshape(bsz, seq, wa) for t in parts], xp.reshape(bsz, seq, d),
                                  conv_a_dw[i], conv_a_dw_b[i], conv_a_ln_g[i], conv_a_ln_b[i], conv_b_dw[i],
                                  conv_w_out_b[i], TL_MIX)
            xp = y.reshape(bsz * seq, d)
            pa.append(na)
            pb.append(nb_)

            parts = norm_matmul(xs, conv_norm_g[i], conv_w_in_b[i], TM_PROMPT)
            st_a = jnp.swapaxes(state_conv_a[i], 0, 1)
            st_b = jnp.swapaxes(state_conv_b[i], 0, 1)
            xs, a_new, c_new = conv_step(parts, xs, st_a, st_b, conv_a_dw[i], conv_a_dw_b[i], conv_a_ln_g[i],
                                         conv_a_ln_b[i], conv_b_dw[i], conv_w_out_b[i])
            sa.append(jnp.concatenate([state_conv_a[i][:, 1:], a_new[:, None]], axis=1))
            sb.append(jnp.concatenate([state_conv_b[i][:, 1:], c_new[:, None]], axis=1))
        else:
            q, k, v, u = norm_matmul(xp, attn_norm_g[i], attn_w_in_b[i], TM_PROMPT)
            shp = (bsz, seq, wattn)
            o = moba_prompt(q.reshape(shp), k.reshape(shp), v.reshape(shp))
            y, npool = pool_out(o, u.reshape(bsz, seq, -1), xp.reshape(bsz, seq, d), pool_w_b[i], pool_scale[i],
                                attn_w_out_b[i], TL_MIX)
            xp = y.reshape(bsz * seq, d)
            pk.append(k.reshape(bsz, seq, n_heads, HEAD_DIM))
            pv.append(v.reshape(bsz, seq, n_heads, HEAD_DIM))
            pp.append(npool)

            q, k, v, u = norm_matmul(xs, attn_norm_g[i], attn_w_in_b[i], TM_PROMPT)
            heads = lambda t: t.reshape(bd, n_heads, HEAD_DIM)
            sel = gate_step(cache_k, i, page_table, heads(q))
            o = attn_step(cache_k, cache_v, i, page_table, sel, heads(q), heads(k), heads(v))
            st_p = jnp.swapaxes(state_pool[i], 0, 1)
            xs = pool_step(o.reshape(bd, wattn), u, xs, st_p, pool_w_b[i], pool_scale[i], attn_w_out_b[i], past_len)
            sk.append(k.reshape(bd, 1, n_heads, HEAD_DIM))
            sv.append(v.reshape(bd, 1, n_heads, HEAD_DIM))
            sp.append(jnp.concatenate([state_pool[i][:, 1:], u[:, None]], axis=1))
        last = layer == depth - 1
        xp = ffn(xp, ffn_norm_g[layer], wg_b[layer], wu_b[layer], wd_b[layer], final_norm_g, TM_PROMPT, TF, last)
        xs = ffn(xs, ffn_norm_g[layer], wg_b[layer], wu_b[layer], wd_b[layer], final_norm_g, TM_PROMPT, TF, last)
    return (xp.reshape(bsz, seq, d), xs.reshape(bd, 1, d), jnp.stack(pa), jnp.stack(sa), jnp.stack(pb),
            jnp.stack(sb), jnp.stack(pk), jnp.stack(sk), jnp.stack(pv), jnp.stack(sv), jnp.stack(pp), jnp.stack(sp))
```

```python
import functools

import jax
import jax.numpy as jnp
from jax import lax
from jax.experimental import pallas as pl
from jax.experimental.pallas import tpu as pltpu

EPS = 1e-6
PAGE_SIZE = 128
MOBA_BLOCK = 256
MOBA_TOPK = 3
HEAD_DIM = 128
POOL_WINDOWS = (2, 4, 8, 16)
POOL_MAX = max(POOL_WINDOWS)
MASKED = -1e30

V7X_VMEM_BYTES = 64 * 1024 * 1024
VMEM_LIMIT = V7X_VMEM_BYTES - 8 * 1024 * 1024
LANES = 128
SUBLANES = 8
COL_GROUP = 1024

BF16 = jnp.bfloat16
F32 = jnp.float32


def _params(*sem):
    return pltpu.CompilerParams(dimension_semantics=sem, vmem_limit_bytes=VMEM_LIMIT)


def _rms(x, g):
    ms = jnp.mean(x * x, axis=-1, keepdims=True)
    return x * lax.rsqrt(ms + EPS) * g


def _sigmoid(x):
    return 1.0 / (1.0 + jnp.exp(-x))


def _resident(shape):
    return pl.BlockSpec(shape, lambda *_: (0,) * len(shape), pipeline_mode=pl.Buffered(1))


def _norm_matmul_kernel(x_ref, g_ref, w_ref, *out_refs):
    h = _rms(x_ref[...], g_ref[...]).astype(BF16)
    for c, o_ref in enumerate(out_refs):
        o_ref[...] = jnp.dot(h, w_ref[:, c * COL_GROUP:(c + 1) * COL_GROUP], preferred_element_type=F32)


def norm_matmul(x, g, w, tm):
    m, d = x.shape
    n_out = w.shape[1] // COL_GROUP
    tm = min(tm, m)
    return pl.pallas_call(
        _norm_matmul_kernel,
        grid=(m // tm,),
        in_specs=[pl.BlockSpec((tm, d), lambda i: (i, 0)), _resident((1, d)), _resident(w.shape)],
        out_specs=[pl.BlockSpec((tm, COL_GROUP), lambda i: (i, 0))] * n_out,
        out_shape=[jax.ShapeDtypeStruct((m, COL_GROUP), F32)] * n_out,
        compiler_params=_params("parallel"),
        name="norm_matmul",
    )(x, g.reshape(1, d), w)


def _ffn_kernel(x_ref, g_ref, wg_ref, wu_ref, wd_ref, fg_ref, o_ref, h_ref, acc_ref, *, final_norm):
    f = pl.program_id(1)

    @pl.when(f == 0)
    def _():
        h_ref[...] = _rms(x_ref[...], g_ref[...]).astype(h_ref.dtype)
        acc_ref[...] = jnp.zeros_like(acc_ref)

    h = h_ref[...]
    gate = jnp.dot(h, wg_ref[...], preferred_element_type=F32)
    up = jnp.dot(h, wu_ref[...], preferred_element_type=F32)
    act = (gate * _sigmoid(gate) * up).astype(BF16)
    acc_ref[...] += jnp.dot(act, wd_ref[...], preferred_element_type=F32)

    @pl.when(f == pl.num_programs(1) - 1)
    def _():
        y = x_ref[...] + acc_ref[...]
        if final_norm:
            y = _rms(y, fg_ref[...])
        o_ref[...] = y


def ffn(x, g, wg, wu, wd, final_g, tm, tf, final_norm):
    m, d = x.shape
    dff = wg.shape[1]
    tm = min(tm, m)
    return pl.pallas_call(
        functools.partial(_ffn_kernel, final_norm=final_norm),
        grid=(m // tm, dff // tf),
        in_specs=[pl.BlockSpec((tm, d), lambda i, f: (i, 0)),
                  pl.BlockSpec((1, d), lambda i, f: (0, 0)),
                  pl.BlockSpec((d, tf), lambda i, f: (0, f)),
                  pl.BlockSpec((d, tf), lambda i, f: (0, f)),
                  pl.BlockSpec((tf, d), lambda i, f: (f, 0)),
                  pl.BlockSpec((1, d), lambda i, f: (0, 0))],
        out_specs=pl.BlockSpec((tm, d), lambda i, f: (i, 0)),
        out_shape=jax.ShapeDtypeStruct((m, d), F32),
        scratch_shapes=[pltpu.VMEM((tm, d), BF16), pltpu.VMEM((tm, d), F32)],
        compiler_params=_params("parallel", "arbitrary"),
        name="ffn",
    )(x, g.reshape(1, d), wg, wu, wd, final_g.reshape(1, d))


A_HALO = 32
B_HALO = 8


def _layer_norm_silu(x, g, b):
    mu = jnp.mean(x, axis=-1, keepdims=True)
    xc = x - mu
    y = xc * lax.rsqrt(jnp.mean(xc * xc, axis=-1, keepdims=True) + EPS) * g + b
    return y * _sigmoid(y)


def _conv_mix_kernel(av_ref, ag_ref, bh_ref, bb_ref, bc_ref, x_ref, adw_ref, adwb_ref, lng_ref, lnb_ref,
                     bdw_ref, wo_ref, y_ref, newa_ref, newb_ref, aext, cext, ashift, aconv, mix, *, ta, tb, rc):
    l = pl.program_id(1)
    tl = av_ref.shape[1]
    wa = av_ref.shape[2]

    @pl.when(l == 0)
    def _():
        aext[0:A_HALO, :] = jnp.zeros((A_HALO, wa), F32)
        cext[0:B_HALO, :] = jnp.zeros((B_HALO, wa), F32)

    @pl.when(l > 0)
    def _():
        aext[0:A_HALO, :] = aext[tl:tl + A_HALO, :]
        cext[0:B_HALO, :] = cext[tl:tl + B_HALO, :]

    aext[A_HALO:A_HALO + tl, :] = av_ref[0] * _sigmoid(ag_ref[0])
    cext[B_HALO:B_HALO + tl, :] = bc_ref[0] * bh_ref[0]

    n_shift = ashift.shape[1]
    for s in range(1, SUBLANES):
        ashift[s - 1] = aext[s:s + n_shift, :]

    for r0 in range(0, tl, rc):
        for c0 in range(0, wa, LANES):
            acc = jnp.broadcast_to(adwb_ref[:, c0:c0 + LANES], (rc, LANES))
            for k in range(ta):
                off = A_HALO - (ta - 1) + r0 + k
                s = off % SUBLANES
                src = aext if s == 0 else ashift.at[s - 1]
                acc = acc + adw_ref[k:k + 1, c0:c0 + LANES] * src[off - s:off - s + rc, c0:c0 + LANES]
            aconv[r0:r0 + rc, c0:c0 + LANES] = acc
    mix[:, 0:wa] = _layer_norm_silu(aconv[...], lng_ref[...], lnb_ref[...]).astype(BF16)

    bconv = jnp.zeros((tl, wa), F32)
    for k in range(tb):
        off = B_HALO - (tb - 1) + k
        bconv = bconv + bdw_ref[k:k + 1, :] * cext[off:off + tl, :]
    mix[:, wa:] = (bb_ref[0] * bconv).astype(BF16)

    y_ref[0] = x_ref[0] + jnp.dot(mix[...], wo_ref[...], preferred_element_type=F32)

    @pl.when(l == pl.num_programs(1) - 1)
    def _():
        newa_ref[0] = aext[A_HALO + tl - (ta - 1):A_HALO + tl, :]
        newb_ref[0] = cext[B_HALO + tl - (tb - 1):B_HALO + tl, :]


def conv_mix(parts, x, a_dw, a_dw_b, ln_g, ln_b, b_dw, w_out, tl):
    bsz, seq, d = x.shape
    wa = parts[0].shape[-1]
    ta, tb = a_dw.shape[0], b_dw.shape[0]
    tl = min(tl, seq)
    assert seq % tl == 0 and tl >= A_HALO and ta - 1 <= A_HALO and tb - 1 <= B_HALO
    part_spec = pl.BlockSpec((1, tl, wa), lambda b, l: (b, l, 0))
    const = _resident
    return pl.pallas_call(
        functools.partial(_conv_mix_kernel, ta=ta, tb=tb, rc=min(128, tl)),
        grid=(bsz, seq // tl),
        in_specs=[part_spec] * 5 + [pl.BlockSpec((1, tl, d), lambda b, l: (b, l, 0)),
                                    const((ta, wa)), const((1, wa)), const((1, wa)), const((1, wa)),
                                    const((tb, wa)), const(w_out.shape)],
        out_specs=[pl.BlockSpec((1, tl, d), lambda b, l: (b, l, 0)),
                   pl.BlockSpec((1, ta - 1, wa), lambda b, l: (b, 0, 0)),
                   pl.BlockSpec((1, tb - 1, wa), lambda b, l: (b, 0, 0))],
        out_shape=[jax.ShapeDtypeStruct((bsz, seq, d), F32),
                   jax.ShapeDtypeStruct((bsz, ta - 1, wa), F32),
                   jax.ShapeDtypeStruct((bsz, tb - 1, wa), F32)],
        scratch_shapes=[pltpu.VMEM((A_HALO + tl, wa), F32), pltpu.VMEM((B_HALO + tl, wa), F32),
                        pltpu.VMEM((SUBLANES - 1, A_HALO + tl - SUBLANES, wa), F32),
                        pltpu.VMEM((tl, wa), F32), pltpu.VMEM((tl, 2 * wa), BF16)],
        compiler_params=_params("parallel", "arbitrary"),
        name="conv_mix",
    )(*parts, x, a_dw, a_dw_b.reshape(1, wa), ln_g.reshape(1, wa), ln_b.reshape(1, wa), b_dw, w_out)


def _conv_step_kernel(av_ref, ag_ref, bh_ref, bb_ref, bc_ref, x_ref, sta_ref, stb_ref, adw_ref, adwb_ref,
                      lng_ref, lnb_ref, bdw_ref, wo_ref, y_ref, anew_ref, cnew_ref, *, ta, tb):
    a = av_ref[...] * _sigmoid(ag_ref[...])
    c = bc_ref[...] * bh_ref[...]
    anew_ref[...] = a
    cnew_ref[...] = c
    acc = adwb_ref[...] + adw_ref[ta - 1:ta, :] * a
    for k in range(ta - 1):
        acc = acc + adw_ref[k:k + 1, :] * sta_ref[k]
    a_out = _layer_norm_silu(acc, lng_ref[...], lnb_ref[...])
    bconv = bdw_ref[tb - 1:tb, :] * c
    for k in range(tb - 1):
        bconv = bconv + bdw_ref[k:k + 1, :] * stb_ref[k]
    mix = jnp.concatenate([a_out, bb_ref[...] * bconv], axis=-1).astype(BF16)
    y_ref[...] = x_ref[...] + jnp.dot(mix, wo_ref[...], preferred_element_type=F32)


def conv_step(parts, x, st_a, st_b, a_dw, a_dw_b, ln_g, ln_b, b_dw, w_out):
    bd, d = x.shape
    wa = parts[0].shape[-1]
    ta, tb = a_dw.shape[0], b_dw.shape[0]
    return pl.pallas_call(
        functools.partial(_conv_step_kernel, ta=ta, tb=tb),
        out_shape=[jax.ShapeDtypeStruct((bd, d), F32), jax.ShapeDtypeStruct((bd, wa), F32),
                   jax.ShapeDtypeStruct((bd, wa), F32)],
        compiler_params=pltpu.CompilerParams(vmem_limit_bytes=VMEM_LIMIT),
        name="conv_step",
    )(*parts, x, st_a, st_b, a_dw, a_dw_b.reshape(1, wa), ln_g.reshape(1, wa), ln_b.reshape(1, wa), b_dw, w_out)


LOG2E = 1.4426950408889634
TQ = 2 * MOBA_BLOCK
DA = 2 * HEAD_DIM


def _split3(x):
    hi = x.astype(BF16).astype(F32)
    mid = (x - hi).astype(BF16).astype(F32)
    lo = (x - hi - mid).astype(BF16).astype(F32)
    return hi, mid, lo


def _moba_setup(q_ref, k_ref, v_ref, qaug, kaug, vaug, kmean, slope2, nb):
    seq = k_ref.shape[1]
    bs = MOBA_BLOCK
    nbp = kmean.shape[0]
    k = k_ref[0]
    q = q_ref[0]
    kmean[...] = jnp.zeros_like(kmean)
    kmean[0:nb, :] = jnp.mean(k.reshape(nb, bs, HEAD_DIM), axis=1)

    gate = lax.dot_general(kmean[...], q, (((1,), (1,)), ((), ())), precision=lax.Precision.HIGHEST,
                           preferred_element_type=F32)
    blk = lax.broadcasted_iota(jnp.int32, (nbp, seq), 0)
    own = lax.broadcasted_iota(jnp.int32, (nbp, seq), 1) // bs
    gate = jnp.where(blk < own, gate, -jnp.inf)
    attend = blk == own
    for _ in range(MOBA_TOPK):
        m = jnp.max(gate, axis=0, keepdims=True)
        idx = jnp.min(jnp.where(gate == m, blk, nbp), axis=0, keepdims=True)
        hit = blk == idx
        attend = attend | (hit & (blk < own))
        gate = jnp.where(hit, -jnp.inf, gate)
    attend_t = jnp.where(attend, 1.0, 0.0)
    attend_r = jnp.concatenate([attend_t, jnp.zeros((LANES - nbp, seq), F32)], axis=0).T

    lane = lax.broadcasted_iota(jnp.int32, (seq, LANES), 1)
    pos_i = lax.broadcasted_iota(jnp.int32, (seq, LANES), 0)
    pos = pos_i.astype(F32)

    khi, kmid, klo = _split3(slope2 * pos)
    ek = jnp.where(lane == pos_i // bs, 1.0, 0.0)
    ek = jnp.where(lane == nb, khi, ek)
    ek = jnp.where(lane == nb + 1, kmid, ek)
    ek = jnp.where(lane == nb + 2, klo, ek)
    ek = jnp.where((lane >= nb + 3) & (lane < nb + 6), 1.0, ek)
    kaug[:, 0:HEAD_DIM] = k.astype(BF16)
    kaug[:, HEAD_DIM:] = ek.astype(BF16)

    qhi, qmid, qlo = _split3(-slope2 * pos)
    eq = jnp.where(lane < nb, jnp.where(attend_r > 0.5, 0.0, MASKED), 0.0)
    eq = jnp.where((lane >= nb) & (lane < nb + 3), 1.0, eq)
    eq = jnp.where(lane == nb + 3, qhi, eq)
    eq = jnp.where(lane == nb + 4, qmid, eq)
    eq = jnp.where(lane == nb + 5, qlo, eq)
    qaug[:, 0:HEAD_DIM] = (q * (HEAD_DIM ** -0.5 * LOG2E)).astype(BF16)
    qaug[:, HEAD_DIM:] = eq.astype(BF16)

    vaug[:, 0:HEAD_DIM] = v_ref[0].astype(BF16)
    vaug[:, HEAD_DIM:] = jnp.ones((seq, HEAD_DIM), BF16)


def _moba_kernel(q_ref, k_ref, v_ref, o_ref, qaug, kaug, vaug, kmean, m_ref, acc_ref, s_ref, *, n_heads, nb):
    h = pl.program_id(1)
    i = pl.program_id(2)
    bs = MOBA_BLOCK
    nt = (((1,), (1,)), ((), ()))

    @pl.when(i == 0)
    def _():
        slope2 = jnp.exp2((h + 1).astype(F32) * (-8.0 / n_heads) + jnp.zeros((1, 1), F32)) * LOG2E
        _moba_setup(q_ref, k_ref, v_ref, qaug, kaug, vaug, kmean, slope2, nb)

    r0 = pl.multiple_of(i * TQ, TQ)
    qa = qaug[pl.ds(r0, TQ), :]

    halves = (slice(0, bs), slice(bs, TQ))

    def scores(g):
        kg = kaug[pl.ds(pl.multiple_of(g * TQ, TQ), TQ), :]
        return jnp.concatenate([lax.dot_general(qa[r], kg, nt, preferred_element_type=F32) for r in halves], axis=0)

    def update(s, g):
        vg = vaug[pl.ds(pl.multiple_of(g * TQ, TQ), TQ), :]
        for r in halves:
            m = m_ref[r, :]
            m_new = jnp.maximum(m, jnp.max(s[r], axis=1, keepdims=True))
            p = jnp.exp2(s[r] - m_new).astype(BF16)
            pv = jnp.dot(p, vg, preferred_element_type=F32)
            acc_ref[r, :] = jnp.exp2(m - m_new) * acc_ref[r, :] + pv
            m_ref[r, :] = m_new

    m_ref[...] = jnp.full(m_ref.shape, MASKED, F32)
    acc_ref[...] = jnp.zeros_like(acc_ref)
    s_ref[0] = scores(0)

    def past_group(g, slot):
        s_ref[1 - slot] = scores(g + 1)
        update(s_ref[slot], g)

    @pl.loop(0, i // 2)
    def _(j):
        past_group(2 * j, 0)
        past_group(2 * j + 1, 1)

    def own_group(slot):
        causal = (lax.broadcasted_iota(jnp.int32, (TQ, TQ), 0) >= lax.broadcasted_iota(jnp.int32, (TQ, TQ), 1))
        update(jnp.where(causal, s_ref[slot], MASKED), i)
        o_ref[0] = acc_ref[:, 0:HEAD_DIM] / acc_ref[:, HEAD_DIM:]

    @pl.when(i % 2 == 0)
    def _():
        own_group(0)

    @pl.when(i % 2 == 1)
    def _():
        past_group(i - 1, 0)
        own_group(1)


def moba_prompt(q, k, v):
    bsz, seq, width = q.shape
    n_heads = width // HEAD_DIM
    assert seq % TQ == 0
    nb = seq // MOBA_BLOCK
    nbp = -(-nb // 8) * 8
    assert nb + 6 <= LANES
    full = pl.BlockSpec((1, seq, HEAD_DIM), lambda b, h, i: (b, 0, h))
    tile = pl.BlockSpec((1, TQ, HEAD_DIM), lambda b, h, i: (b, i, h))
    return pl.pallas_call(
        functools.partial(_moba_kernel, n_heads=n_heads, nb=nb),
        grid=(bsz, n_heads, seq // TQ),
        in_specs=[full, full, full],
        out_specs=tile,
        out_shape=jax.ShapeDtypeStruct((bsz, seq, width), F32),
        scratch_shapes=[pltpu.VMEM((seq, DA), BF16), pltpu.VMEM((seq, DA), BF16), pltpu.VMEM((seq, DA), BF16),
                        pltpu.VMEM((nbp, HEAD_DIM), F32), pltpu.VMEM((TQ, 1), F32), pltpu.VMEM((TQ, DA), F32),
                        pltpu.VMEM((2, TQ, TQ), F32)],
        compiler_params=_params("parallel", "parallel", "arbitrary"),
        name="moba_prompt",
    )(q, k, v)


PAGES_IN_FLIGHT = 8
SUM_CHAINS = 4


def _gate_step_kernel(pt_ref, q_ref, k_hbm, sel_ref, buf, sem, ksum, *, layer, pages_per_block, n_heads):
    total = pt_ref.shape[0]
    bd = q_ref.shape[0]
    nblk = ksum.shape[0] // bd
    depth = buf.shape[0]

    def page_copy(t, slot):
        return pltpu.make_async_copy(k_hbm.at[layer, pt_ref[t]], buf.at[slot], sem.at[slot])

    for t in range(min(depth, total)):
        page_copy(t, t).start()
    ksum[...] = jnp.zeros_like(ksum)

    @pl.loop(0, total)
    def _(t):
        slot = t % depth
        page_copy(t, slot).wait()
        page = buf[slot].reshape(SUM_CHAINS, PAGE_SIZE // SUM_CHAINS, n_heads, HEAD_DIM)
        ksum[t // pages_per_block] += jnp.sum(jnp.sum(page, axis=1), axis=0)

        @pl.when(t + depth < total)
        def _():
            page_copy(t + depth, slot).start()

    sel_ref[...] = jnp.zeros_like(sel_ref)
    for b in range(bd):
        kmean = ksum[b * nblk:(b + 1) * nblk] * (1.0 / MOBA_BLOCK)
        gate = jnp.sum(kmean * q_ref[b][None], axis=-1)
        blk = lax.broadcasted_iota(jnp.int32, gate.shape, 0)
        for r in range(MOBA_TOPK):
            m = jnp.max(gate, axis=0, keepdims=True)
            idx = jnp.min(jnp.where(gate == m, blk, nblk), axis=0, keepdims=True)
            sel_ref[b, r:r + 1, 0:n_heads] = idx
            gate = jnp.where(blk == idx, -jnp.inf, gate)


SEL_ROWS = 8


def gate_step(cache_k, layer, page_table, q):
    bd, n_pages = page_table.shape
    n_heads = q.shape[1]
    pages_per_block = MOBA_BLOCK // PAGE_SIZE
    assert n_pages % pages_per_block == 0 and MOBA_TOPK <= SEL_ROWS and n_heads <= LANES
    nblk = n_pages // pages_per_block
    sel = pl.pallas_call(
        functools.partial(_gate_step_kernel, layer=layer, pages_per_block=pages_per_block, n_heads=n_heads),
        in_specs=[pl.BlockSpec(memory_space=pltpu.SMEM), pl.BlockSpec(memory_space=pltpu.VMEM),
                  pl.BlockSpec(memory_space=pl.ANY)],
        out_specs=pl.BlockSpec(memory_space=pltpu.VMEM),
        out_shape=jax.ShapeDtypeStruct((bd, SEL_ROWS, LANES), jnp.int32),
        scratch_shapes=[pltpu.VMEM((PAGES_IN_FLIGHT, PAGE_SIZE, n_heads, HEAD_DIM), F32),
                        pltpu.SemaphoreType.DMA((PAGES_IN_FLIGHT,)),
                        pltpu.VMEM((bd * nblk, n_heads, HEAD_DIM), F32)],
        compiler_params=pltpu.CompilerParams(vmem_limit_bytes=VMEM_LIMIT),
        name="gate_step",
    )(page_table.reshape(-1), q, cache_k)
    return sel[:, :MOBA_TOPK, :n_heads]


def _attn_step_kernel(pt_ref, sel_ref, q_ref, kn_ref, vn_ref, k_hbm, v_hbm, o_ref, kbuf, vbuf, ksem, vsem,
                      m_ref, l_ref, acc_ref, *, layer, n_pages, q_pos, pages_per_block):
    bd, n_heads, _ = q_ref.shape
    per_head = MOBA_TOPK * pages_per_block
    total = bd * n_heads * per_head
    depth = kbuf.shape[0]
    scale = HEAD_DIM ** -0.5
    own = q_pos // MOBA_BLOCK

    def coords(t):
        bh, j = t // per_head, t % per_head
        b, h = bh // n_heads, bh % n_heads
        return b, h, j, sel_ref[(b * MOBA_TOPK + j // pages_per_block) * n_heads + h]

    def page_copies(t, slot):
        b, h, j, n = coords(t)
        page = pt_ref[b * n_pages + n * pages_per_block + j % pages_per_block]
        head = pl.ds(h, 1)
        return (pltpu.make_async_copy(k_hbm.at[layer, page, :, head, :], kbuf.at[slot], ksem.at[slot]),
                pltpu.make_async_copy(v_hbm.at[layer, page, :, head, :], vbuf.at[slot], vsem.at[slot]))

    for t in range(min(depth, total)):
        for c in page_copies(t, t):
            c.start()

    @pl.loop(0, total)
    def _(t):
        slot = t % depth
        b, h, j, n = coords(t)
        for c in page_copies(t, slot):
            c.wait()
        q = q_ref[b, pl.ds(h, 1), :]

        @pl.when(j == 0)
        def _():
            m_ref[...] = jnp.sum(q * kn_ref[b, pl.ds(h, 1), :], axis=1, keepdims=True) * scale
            l_ref[...] = jnp.ones_like(l_ref)
            acc_ref[...] = vn_ref[b, pl.ds(h, 1), :]

        k = kbuf[slot][:, 0, :]
        v = vbuf[slot][:, 0, :]
        slope = jnp.exp2(jnp.asarray(h + 1, F32) * (-8.0 / n_heads) + jnp.zeros((1, 1), F32))
        kpos = (n * MOBA_BLOCK + (j % pages_per_block) * PAGE_SIZE
                + lax.broadcasted_iota(jnp.int32, (PAGE_SIZE, 1), 0))
        dist = (q_pos - kpos).astype(F32)
        s = jnp.sum(k * q, axis=1, keepdims=True) * scale - slope * dist
        s = jnp.where((dist >= 0.0) & (n < own), s, MASKED)
        m = m_ref[...]
        m_new = jnp.maximum(m, jnp.max(s, axis=0, keepdims=True))
        alpha = jnp.exp(m - m_new)
        p = jnp.exp(s - m_new)
        l_ref[...] = alpha * l_ref[...] + jnp.sum(p, axis=0, keepdims=True)
        acc_ref[...] = alpha * acc_ref[...] + jnp.sum(p * v, axis=0, keepdims=True)
        m_ref[...] = m_new

        @pl.when(j == per_head - 1)
        def _():
            o_ref[b, pl.ds(h, 1), :] = acc_ref[...] / l_ref[...]

        @pl.when(t + depth < total)
        def _():
            for c in page_copies(t + depth, slot):
                c.start()


def attn_step(cache_k, cache_v, layer, page_table, sel, q, k_new, v_new):
    bd, n_pages = page_table.shape
    n_heads = q.shape[1]
    pages_per_block = MOBA_BLOCK // PAGE_SIZE
    assert n_pages % pages_per_block == 0
    vmem, smem = pl.BlockSpec(memory_space=pltpu.VMEM), pl.BlockSpec(memory_space=pltpu.SMEM)
    hbm = pl.BlockSpec(memory_space=pl.ANY)
    return pl.pallas_call(
        functools.partial(_attn_step_kernel, layer=layer, n_pages=n_pages, q_pos=n_pages * PAGE_SIZE,
                          pages_per_block=pages_per_block),
        in_specs=[smem, smem, vmem, vmem, vmem, hbm, hbm],
        out_specs=vmem,
        out_shape=jax.ShapeDtypeStruct((bd, n_heads, HEAD_DIM), F32),
        scratch_shapes=[pltpu.VMEM((PAGES_IN_FLIGHT, PAGE_SIZE, 1, HEAD_DIM), F32),
                        pltpu.VMEM((PAGES_IN_FLIGHT, PAGE_SIZE, 1, HEAD_DIM), F32),
                        pltpu.SemaphoreType.DMA((PAGES_IN_FLIGHT,)), pltpu.SemaphoreType.DMA((PAGES_IN_FLIGHT,)),
                        pltpu.VMEM((1, 1), F32), pltpu.VMEM((1, 1), F32), pltpu.VMEM((1, HEAD_DIM), F32)],
        compiler_params=pltpu.CompilerParams(vmem_limit_bytes=VMEM_LIMIT),
        name="attn_step",
    )(page_table.reshape(-1), sel.reshape(-1), q, k_new, v_new, cache_k, cache_v)


P_HALO = 16


def _pool_out_kernel(o_ref, u_ref, x_ref, pw_ref, ps_ref, wo_ref, y_ref, newp_ref, uext, mix):
    l = pl.program_id(1)
    tl = u_ref.shape[1]
    wp = u_ref.shape[2]
    wo_attn = o_ref.shape[2]
    pg = wp // len(POOL_WINDOWS)

    @pl.when(l == 0)
    def _():
        uext[0:P_HALO, :] = jnp.zeros((P_HALO, wp), F32)

    @pl.when(l > 0)
    def _():
        uext[0:P_HALO, :] = uext[tl:tl + P_HALO, :]

    uext[P_HALO:P_HALO + tl, :] = u_ref[0]
    mix[:, 0:wo_attn] = o_ref[0].astype(BF16)
    pos = l * tl + lax.broadcasted_iota(jnp.int32, (tl, pg), 0)
    for g, w in enumerate(POOL_WINDOWS):
        c0 = g * pg
        wsum = uext[P_HALO:P_HALO + tl, c0:c0 + pg]
        for j in range(1, w):
            wsum = wsum + uext[P_HALO - j:P_HALO - j + tl, c0:c0 + pg]
        count = jnp.minimum(pos + 1, w).astype(F32)
        dlt = (wsum / count - uext[P_HALO:P_HALO + tl, c0:c0 + pg]).astype(BF16)
        yp = jnp.dot(dlt, pw_ref[g], preferred_element_type=F32) * ps_ref[:, c0:c0 + pg]
        mix[:, wo_attn + c0:wo_attn + c0 + pg] = yp.astype(BF16)

    y_ref[0] = x_ref[0] + jnp.dot(mix[...], wo_ref[...], preferred_element_type=F32)

    @pl.when(l == pl.num_programs(1) - 1)
    def _():
        newp_ref[0] = uext[P_HALO + tl - (POOL_MAX - 1):P_HALO + tl, :]


def pool_out(o, u, x, pool_w, pool_scale, w_out, tl):
    bsz, seq, d = x.shape
    wp = u.shape[-1]
    tl = min(tl, seq)
    assert seq % tl == 0 and tl >= P_HALO
    const = _resident
    return pl.pallas_call(
        _pool_out_kernel,
        grid=(bsz, seq // tl),
        in_specs=[pl.BlockSpec((1, tl, o.shape[-1]), lambda b, l: (b, l, 0)),
                  pl.BlockSpec((1, tl, wp), lambda b, l: (b, l, 0)),
                  pl.BlockSpec((1, tl, d), lambda b, l: (b, l, 0)),
                  const(pool_w.shape), const((1, wp)), const(w_out.shape)],
        out_specs=[pl.BlockSpec((1, tl, d), lambda b, l: (b, l, 0)),
                   pl.BlockSpec((1, POOL_MAX - 1, wp), lambda b, l: (b, 0, 0))],
        out_shape=[jax.ShapeDtypeStruct((bsz, seq, d), F32),
                   jax.ShapeDtypeStruct((bsz, POOL_MAX - 1, wp), F32)],
        scratch_shapes=[pltpu.VMEM((P_HALO + tl, wp), F32), pltpu.VMEM((tl, o.shape[-1] + wp), BF16)],
        compiler_params=_params("parallel", "arbitrary"),
        name="pool_out",
    )(o, u, x, pool_w, pool_scale.reshape(1, wp), w_out)


def _pool_step_kernel(o_ref, u_ref, x_ref, stp_ref, pw_ref, ps_ref, wo_ref, y_ref, *, start_pos):
    wp = u_ref.shape[1]
    pg = wp // len(POOL_WINDOWS)
    u = u_ref[...]
    parts = [o_ref[...].astype(BF16)]
    for g, w in enumerate(POOL_WINDOWS):
        c0 = g * pg
        wsum = u[:, c0:c0 + pg]
        for j in range(1, w):
            wsum = wsum + stp_ref[POOL_MAX - 1 - j][:, c0:c0 + pg]
        count = float(min(start_pos + 1, w))
        dlt = (wsum / count - u[:, c0:c0 + pg]).astype(BF16)
        yp = jnp.dot(dlt, pw_ref[g], preferred_element_type=F32) * ps_ref[:, c0:c0 + pg]
        parts.append(yp.astype(BF16))
    mix = jnp.concatenate(parts, axis=-1)
    y_ref[...] = x_ref[...] + jnp.dot(mix, wo_ref[...], preferred_element_type=F32)


def pool_step(o, u, x, st_p, pool_w, pool_scale, w_out, start_pos):
    bd, d = x.shape
    return pl.pallas_call(
        functools.partial(_pool_step_kernel, start_pos=start_pos),
        out_shape=jax.ShapeDtypeStruct((bd, d), F32),
        compiler_params=pltpu.CompilerParams(vmem_limit_bytes=VMEM_LIMIT),
        name="pool_step",
    )(o, u, x, st_p, pool_w, pool_scale.reshape(1, -1), w_out)


TM_PROMPT = 512
TF = 512
TL_MIX = 256


def kernel(x_prompt, x_sample, state_conv_a, state_conv_b, cache_k, cache_v, state_pool, page_table, conv_norm_g, conv_w_in, conv_a_dw, conv_a_dw_b, conv_a_ln_g, conv_a_ln_b, conv_b_dw, conv_w_out, attn_norm_g, attn_w_in, pool_w, pool_scale, attn_w_out, ffn_norm_g, ffn_w_gate, ffn_w_up, ffn_w_down, final_norm_g):
    bsz, seq, d = x_prompt.shape
    bd, dec_seq, _ = x_sample.shape
    assert dec_seq == 1
    depth = ffn_norm_g.shape[0]
    n_heads = cache_k.shape[3]
    wattn = n_heads * HEAD_DIM
    past_len = page_table.shape[1] * PAGE_SIZE

    layer_bf16 = lambda w, n: w[n].astype(BF16)

    xp = x_prompt.reshape(bsz * seq, d)
    xs = x_sample.reshape(bd, d)
    pa, pb, pk, pv, pp = [], [], [], [], []
    sa, sb, sk, sv, sp = [], [], [], [], []
    for layer in range(depth):
        i = layer // 2
        if layer % 2 == 0:
            w_in, w_out = layer_bf16(conv_w_in, i), layer_bf16(conv_w_out, i)
            parts = norm_matmul(xp, conv_norm_g[i], w_in, TM_PROMPT)
            wa = parts[0].shape[-1]
            y, na, nb_ = conv_mix([t.reshape(bsz, seq, wa) for t in parts], xp.reshape(bsz, seq, d),
                                  conv_a_dw[i], conv_a_dw_b[i], conv_a_ln_g[i], conv_a_ln_b[i], conv_b_dw[i],
                                  w_out, TL_MIX)
            xp = y.reshape(bsz * seq, d)
            pa.append(na)
            pb.append(nb_)

            parts = norm_matmul(xs, conv_norm_g[i], w_in, TM_PROMPT)
            st_a = jnp.swapaxes(state_conv_a[i], 0, 1)
            st_b = jnp.swapaxes(state_conv_b[i], 0, 1)
            xs, a_new, c_new = conv_step(parts, xs, st_a, st_b, conv_a_dw[i], conv_a_dw_b[i], conv_a_ln_g[i],
                                         conv_a_ln_b[i], conv_b_dw[i], w_out)
            sa.append(jnp.concatenate([state_conv_a[i][:, 1:], a_new[:, None]], axis=1))
            sb.append(jnp.concatenate([state_conv_b[i][:, 1:], c_new[:, None]], axis=1))
        else:
            w_in, w_out, w_pool = layer_bf16(attn_w_in, i), layer_bf16(attn_w_out, i), layer_bf16(pool_w, i)
            q, k, v, u = norm_matmul(xp, attn_norm_g[i], w_in, TM_PROMPT)
            shp = (bsz, seq, wattn)
            o = moba_prompt(q.reshape(shp), k.reshape(shp), v.reshape(shp))
            y, npool = pool_out(o, u.reshape(bsz, seq, -1), xp.reshape(bsz, seq, d), w_pool, pool_scale[i],
                                w_out, TL_MIX)
            xp = y.reshape(bsz * seq, d)
            pk.append(k.reshape(bsz, seq, n_heads, HEAD_DIM))
            pv.append(v.reshape(bsz, seq, n_heads, HEAD_DIM))
            pp.append(npool)

            q, k, v, u = norm_matmul(xs, attn_norm_g[i], w_in, TM_PROMPT)
            heads = lambda t: t.reshape(bd, n_heads, HEAD_DIM)
            sel = gate_step(cache_k, i, page_table, heads(q))
            o = attn_step(cache_k, cache_v, i, page_table, sel, heads(q), heads(k), heads(v))
            st_p = jnp.swapaxes(state_pool[i], 0, 1)
            xs = pool_step(o.reshape(bd, wattn), u, xs, st_p, w_pool, pool_scale[i], w_out, past_len)
            sk.append(k.reshape(bd, 1, n_heads, HEAD_DIM))
            sv.append(v.reshape(bd, 1, n_heads, HEAD_DIM))
            sp.append(jnp.concatenate([state_pool[i][:, 1:], u[:, None]], axis=1))
        last = layer == depth - 1
        wg, wu, wd = layer_bf16(ffn_w_gate, layer), layer_bf16(ffn_w_up, layer), layer_bf16(ffn_w_down, layer)
        xp = ffn(xp, ffn_norm_g[layer], wg, wu, wd, final_norm_g, TM_PROMPT, TF, last)
        xs = ffn(xs, ffn_norm_g[layer], wg, wu, wd, final_norm_g, TM_PROMPT, TF, last)
    return (xp.reshape(bsz, seq, d), xs.reshape(bd, 1, d), jnp.stack(pa), jnp.stack(sa), jnp.stack(pb),
            jnp.stack(sb), jnp.stack(pk), jnp.stack(sk), jnp.stack(pv), jnp.stack(sv), jnp.stack(pp), jnp.stack(sp))
```

```python
import functools

import jax
import jax.numpy as jnp
from jax import lax
from jax.experimental import pallas as pl
from jax.experimental.pallas import tpu as pltpu

EPS = 1e-6
PAGE_SIZE = 128
MOBA_BLOCK = 256
MOBA_TOPK = 3
HEAD_DIM = 128
POOL_WINDOWS = (2, 4, 8, 16)
POOL_MAX = max(POOL_WINDOWS)
MASKED = -1e30

V7X_VMEM_BYTES = 64 * 1024 * 1024
VMEM_LIMIT = V7X_VMEM_BYTES - 8 * 1024 * 1024
LANES = 128
SUBLANES = 8
COL_GROUP = 1024

BF16 = jnp.bfloat16
F32 = jnp.float32


def _params(*sem):
    return pltpu.CompilerParams(dimension_semantics=sem, vmem_limit_bytes=VMEM_LIMIT)


def _rms(x, g):
    ms = jnp.mean(x * x, axis=-1, keepdims=True)
    return x * lax.rsqrt(ms + EPS) * g


def _sigmoid(x):
    return 1.0 / (1.0 + jnp.exp(-x))


def _resident(shape):
    return pl.BlockSpec(shape, lambda *_: (0,) * len(shape), pipeline_mode=pl.Buffered(1))


def _norm_matmul_kernel(x_ref, g_ref, w_ref, *out_refs):
    h = _rms(x_ref[...], g_ref[...]).astype(BF16)
    for c, o_ref in enumerate(out_refs):
        o_ref[...] = jnp.dot(h, w_ref[:, c * COL_GROUP:(c + 1) * COL_GROUP], preferred_element_type=F32)


def norm_matmul(x, g, w, tm):
    m, d = x.shape
    n_out = w.shape[1] // COL_GROUP
    tm = min(tm, m)
    return pl.pallas_call(
        _norm_matmul_kernel,
        grid=(m // tm,),
        in_specs=[pl.BlockSpec((tm, d), lambda i: (i, 0)), _resident((1, d)), _resident(w.shape)],
        out_specs=[pl.BlockSpec((tm, COL_GROUP), lambda i: (i, 0))] * n_out,
        out_shape=[jax.ShapeDtypeStruct((m, COL_GROUP), F32)] * n_out,
        compiler_params=_params("parallel"),
        name="norm_matmul",
    )(x, g.reshape(1, d), w)


def _ffn_kernel(g_ref, fg_ref, wg_ref, wu_ref, wd_ref, x_hbm, o_ref, xbuf, h_ref, sem, *, final_norm):
    i, f = pl.program_id(0), pl.program_id(1)
    tm = o_ref.shape[0]

    def x_copy(tile):
        return pltpu.make_async_copy(x_hbm.at[pl.ds(pl.multiple_of(tile * tm, tm), tm), :], xbuf, sem)

    @pl.when((i == 0) & (f == 0))
    def _():
        x_copy(0).start()

    @pl.when(f == 0)
    def _():
        x_copy(i).wait()
        x = xbuf[...]
        h_ref[...] = _rms(x, g_ref[...]).astype(h_ref.dtype)
        o_ref[...] = x

    @pl.when((f == 1) & (i + 1 < pl.num_programs(0)))
    def _():
        x_copy(i + 1).start()

    h = h_ref[...]
    gate = jnp.dot(h, wg_ref[...].astype(BF16), preferred_element_type=F32)
    up = jnp.dot(h, wu_ref[...].astype(BF16), preferred_element_type=F32)
    act = (gate * _sigmoid(gate) * up).astype(BF16)
    o_ref[...] += jnp.dot(act, wd_ref[...].astype(BF16), preferred_element_type=F32)

    if final_norm:
        @pl.when(f == pl.num_programs(1) - 1)
        def _():
            o_ref[...] = _rms(o_ref[...], fg_ref[...])


def ffn(x, g, wg, wu, wd, layer, final_g, tm, tf, final_norm):
    m, d = x.shape
    dff = wg.shape[2]
    tm = min(tm, m)
    assert m % tm == 0 and dff % tf == 0 and dff // tf >= 2
    return pl.pallas_call(
        functools.partial(_ffn_kernel, final_norm=final_norm),
        grid=(m // tm, dff // tf),
        in_specs=[_resident((1, d)), _resident((1, d)),
                  pl.BlockSpec((None, d, tf), lambda i, f: (layer, 0, f)),
                  pl.BlockSpec((None, d, tf), lambda i, f: (layer, 0, f)),
                  pl.BlockSpec((None, tf, d), lambda i, f: (layer, f, 0)),
                  pl.BlockSpec(memory_space=pl.ANY)],
        out_specs=pl.BlockSpec((tm, d), lambda i, f: (i, 0)),
        out_shape=jax.ShapeDtypeStruct((m, d), F32),
        scratch_shapes=[pltpu.VMEM((tm, d), F32), pltpu.VMEM((tm, d), BF16), pltpu.SemaphoreType.DMA(())],
        compiler_params=_params("arbitrary", "arbitrary"),
        name="ffn",
    )(g.reshape(1, d), final_g.reshape(1, d), wg, wu, wd, x)


A_HALO = 32
B_HALO = 8


def _layer_norm_silu(x, g, b):
    mu = jnp.mean(x, axis=-1, keepdims=True)
    xc = x - mu
    y = xc * lax.rsqrt(jnp.mean(xc * xc, axis=-1, keepdims=True) + EPS) * g + b
    return y * _sigmoid(y)


def _conv_mix_kernel(av_ref, ag_ref, bh_ref, bb_ref, bc_ref, x_ref, adw_ref, adwb_ref, lng_ref, lnb_ref,
                     bdw_ref, wo_ref, y_ref, newa_ref, newb_ref, aext, cext, ashift, aconv, mix, *, ta, tb, rc):
    l = pl.program_id(1)
    tl = av_ref.shape[1]
    wa = av_ref.shape[2]

    @pl.when(l == 0)
    def _():
        aext[0:A_HALO, :] = jnp.zeros((A_HALO, wa), F32)
        cext[0:B_HALO, :] = jnp.zeros((B_HALO, wa), F32)

    @pl.when(l > 0)
    def _():
        aext[0:A_HALO, :] = aext[tl:tl + A_HALO, :]
        cext[0:B_HALO, :] = cext[tl:tl + B_HALO, :]

    aext[A_HALO:A_HALO + tl, :] = av_ref[0] * _sigmoid(ag_ref[0])
    cext[B_HALO:B_HALO + tl, :] = bc_ref[0] * bh_ref[0]

    n_shift = ashift.shape[1]
    for s in range(1, SUBLANES):
        ashift[s - 1] = aext[s:s + n_shift, :]

    for r0 in range(0, tl, rc):
        for c0 in range(0, wa, LANES):
            acc = jnp.broadcast_to(adwb_ref[:, c0:c0 + LANES], (rc, LANES))
            for k in range(ta):
                off = A_HALO - (ta - 1) + r0 + k
                s = off % SUBLANES
                src = aext if s == 0 else ashift.at[s - 1]
                acc = acc + adw_ref[k:k + 1, c0:c0 + LANES] * src[off - s:off - s + rc, c0:c0 + LANES]
            aconv[r0:r0 + rc, c0:c0 + LANES] = acc
    mix[:, 0:wa] = _layer_norm_silu(aconv[...], lng_ref[...], lnb_ref[...]).astype(BF16)

    bconv = jnp.zeros((tl, wa), F32)
    for k in range(tb):
        off = B_HALO - (tb - 1) + k
        bconv = bconv + bdw_ref[k:k + 1, :] * cext[off:off + tl, :]
    mix[:, wa:] = (bb_ref[0] * bconv).astype(BF16)

    y_ref[0] = x_ref[0] + jnp.dot(mix[...], wo_ref[...], preferred_element_type=F32)

    @pl.when(l == pl.num_programs(1) - 1)
    def _():
        newa_ref[0] = aext[A_HALO + tl - (ta - 1):A_HALO + tl, :]
        newb_ref[0] = cext[B_HALO + tl - (tb - 1):B_HALO + tl, :]


def conv_mix(parts, x, a_dw, a_dw_b, ln_g, ln_b, b_dw, w_out, tl):
    bsz, seq, d = x.shape
    wa = parts[0].shape[-1]
    ta, tb = a_dw.shape[0], b_dw.shape[0]
    tl = min(tl, seq)
    assert seq % tl == 0 and tl >= A_HALO and ta - 1 <= A_HALO and tb - 1 <= B_HALO
    part_spec = pl.BlockSpec((1, tl, wa), lambda b, l: (b, l, 0))
    const = _resident
    return pl.pallas_call(
        functools.partial(_conv_mix_kernel, ta=ta, tb=tb, rc=min(128, tl)),
        grid=(bsz, seq // tl),
        in_specs=[part_spec] * 5 + [pl.BlockSpec((1, tl, d), lambda b, l: (b, l, 0)),
                                    const((ta, wa)), const((1, wa)), const((1, wa)), const((1, wa)),
                                    const((tb, wa)), const(w_out.shape)],
        out_specs=[pl.BlockSpec((1, tl, d), lambda b, l: (b, l, 0)),
                   pl.BlockSpec((1, ta - 1, wa), lambda b, l: (b, 0, 0)),
                   pl.BlockSpec((1, tb - 1, wa), lambda b, l: (b, 0, 0))],
        out_shape=[jax.ShapeDtypeStruct((bsz, seq, d), F32),
                   jax.ShapeDtypeStruct((bsz, ta - 1, wa), F32),
                   jax.ShapeDtypeStruct((bsz, tb - 1, wa), F32)],
        scratch_shapes=[pltpu.VMEM((A_HALO + tl, wa), F32), pltpu.VMEM((B_HALO + tl, wa), F32),
                        pltpu.VMEM((SUBLANES - 1, A_HALO + tl - SUBLANES, wa), F32),
                        pltpu.VMEM((tl, wa), F32), pltpu.VMEM((tl, 2 * wa), BF16)],
        compiler_params=_params("parallel", "arbitrary"),
        name="conv_mix",
    )(*parts, x, a_dw, a_dw_b.reshape(1, wa), ln_g.reshape(1, wa), ln_b.reshape(1, wa), b_dw, w_out)


def _conv_step_kernel(av_ref, ag_ref, bh_ref, bb_ref, bc_ref, x_ref, sta_ref, stb_ref, adw_ref, adwb_ref,
                      lng_ref, lnb_ref, bdw_ref, wo_ref, y_ref, anew_ref, cnew_ref, *, ta, tb):
    a = av_ref[...] * _sigmoid(ag_ref[...])
    c = bc_ref[...] * bh_ref[...]
    anew_ref[...] = a
    cnew_ref[...] = c
    acc = adwb_ref[...] + adw_ref[ta - 1:ta, :] * a
    for k in range(ta - 1):
        acc = acc + adw_ref[k:k + 1, :] * sta_ref[k]
    a_out = _layer_norm_silu(acc, lng_ref[...], lnb_ref[...])
    bconv = bdw_ref[tb - 1:tb, :] * c
    for k in range(tb - 1):
        bconv = bconv + bdw_ref[k:k + 1, :] * stb_ref[k]
    mix = jnp.concatenate([a_out, bb_ref[...] * bconv], axis=-1).astype(BF16)
    y_ref[...] = x_ref[...] + jnp.dot(mix, wo_ref[...], preferred_element_type=F32)


def conv_step(parts, x, st_a, st_b, a_dw, a_dw_b, ln_g, ln_b, b_dw, w_out):
    bd, d = x.shape
    wa = parts[0].shape[-1]
    ta, tb = a_dw.shape[0], b_dw.shape[0]
    return pl.pallas_call(
        functools.partial(_conv_step_kernel, ta=ta, tb=tb),
        out_shape=[jax.ShapeDtypeStruct((bd, d), F32), jax.ShapeDtypeStruct((bd, wa), F32),
                   jax.ShapeDtypeStruct((bd, wa), F32)],
        compiler_params=pltpu.CompilerParams(vmem_limit_bytes=VMEM_LIMIT),
        name="conv_step",
    )(*parts, x, st_a, st_b, a_dw, a_dw_b.reshape(1, wa), ln_g.reshape(1, wa), ln_b.reshape(1, wa), b_dw, w_out)


LOG2E = 1.4426950408889634
TQ = 2 * MOBA_BLOCK
DA = 2 * HEAD_DIM


def _split3(x):
    hi = x.astype(BF16).astype(F32)
    mid = (x - hi).astype(BF16).astype(F32)
    lo = (x - hi - mid).astype(BF16).astype(F32)
    return hi, mid, lo


def _moba_setup(q_ref, k_ref, v_ref, qaug, kaug, vaug, kmean, slope2, nb):
    seq = k_ref.shape[1]
    bs = MOBA_BLOCK
    nbp = kmean.shape[0]
    k = k_ref[0]
    q = q_ref[0]
    kmean[...] = jnp.zeros_like(kmean)
    kmean[0:nb, :] = jnp.mean(k.reshape(nb, bs, HEAD_DIM), axis=1)

    gate = lax.dot_general(kmean[...], q, (((1,), (1,)), ((), ())), precision=lax.Precision.HIGHEST,
                           preferred_element_type=F32)
    blk = lax.broadcasted_iota(jnp.int32, (nbp, seq), 0)
    own = lax.broadcasted_iota(jnp.int32, (nbp, seq), 1) // bs
    gate = jnp.where(blk < own, gate, -jnp.inf)
    attend = blk == own
    for _ in range(MOBA_TOPK):
        m = jnp.max(gate, axis=0, keepdims=True)
        idx = jnp.min(jnp.where(gate == m, blk, nbp), axis=0, keepdims=True)
        hit = blk == idx
        attend = attend | (hit & (blk < own))
        gate = jnp.where(hit, -jnp.inf, gate)
    attend_t = jnp.where(attend, 1.0, 0.0)
    attend_r = jnp.concatenate([attend_t, jnp.zeros((LANES - nbp, seq), F32)], axis=0).T

    lane = lax.broadcasted_iota(jnp.int32, (seq, LANES), 1)
    pos_i = lax.broadcasted_iota(jnp.int32, (seq, LANES), 0)
    pos = pos_i.astype(F32)

    khi, kmid, klo = _split3(slope2 * pos)
    ek = jnp.where(lane == pos_i // bs, 1.0, 0.0)
    ek = jnp.where(lane == nb, khi, ek)
    ek = jnp.where(lane == nb + 1, kmid, ek)
    ek = jnp.where(lane == nb + 2, klo, ek)
    ek = jnp.where((lane >= nb + 3) & (lane < nb + 6), 1.0, ek)
    kaug[:, 0:HEAD_DIM] = k.astype(BF16)
    kaug[:, HEAD_DIM:] = ek.astype(BF16)

    qhi, qmid, qlo = _split3(-slope2 * pos)
    eq = jnp.where(lane < nb, jnp.where(attend_r > 0.5, 0.0, MASKED), 0.0)
    eq = jnp.where((lane >= nb) & (lane < nb + 3), 1.0, eq)
    eq = jnp.where(lane == nb + 3, qhi, eq)
    eq = jnp.where(lane == nb + 4, qmid, eq)
    eq = jnp.where(lane == nb + 5, qlo, eq)
    qaug[:, 0:HEAD_DIM] = (q * (HEAD_DIM ** -0.5 * LOG2E)).astype(BF16)
    qaug[:, HEAD_DIM:] = eq.astype(BF16)

    vaug[:, 0:HEAD_DIM] = v_ref[0].astype(BF16)
    vaug[:, HEAD_DIM:] = jnp.ones((seq, HEAD_DIM), BF16)


def _moba_kernel(q_ref, k_ref, v_ref, o_ref, qaug, kaug, vaug, kmean, m_ref, acc_ref, s_ref, *, n_heads, nb):
    h = pl.program_id(1)
    i = pl.program_id(2)
    bs = MOBA_BLOCK
    nt = (((1,), (1,)), ((), ()))

    @pl.when(i == 0)
    def _():
        slope2 = jnp.exp2((h + 1).astype(F32) * (-8.0 / n_heads) + jnp.zeros((1, 1), F32)) * LOG2E
        _moba_setup(q_ref, k_ref, v_ref, qaug, kaug, vaug, kmean, slope2, nb)

    r0 = pl.multiple_of(i * TQ, TQ)
    qa = qaug[pl.ds(r0, TQ), :]

    halves = (slice(0, bs), slice(bs, TQ))

    def scores(g):
        kg = kaug[pl.ds(pl.multiple_of(g * TQ, TQ), TQ), :]
        return jnp.concatenate([lax.dot_general(qa[r], kg, nt, preferred_element_type=F32) for r in halves], axis=0)

    def update(s, g):
        vg = vaug[pl.ds(pl.multiple_of(g * TQ, TQ), TQ), :]
        for r in halves:
            m = m_ref[r, :]
            m_new = jnp.maximum(m, jnp.max(s[r], axis=1, keepdims=True))
            p = jnp.exp2(s[r] - m_new).astype(BF16)
            pv = jnp.dot(p, vg, preferred_element_type=F32)
            acc_ref[r, :] = jnp.exp2(m - m_new) * acc_ref[r, :] + pv
            m_ref[r, :] = m_new

    m_ref[...] = jnp.full(m_ref.shape, MASKED, F32)
    acc_ref[...] = jnp.zeros_like(acc_ref)
    s_ref[0] = scores(0)

    def past_group(g, slot):
        s_ref[1 - slot] = scores(g + 1)
        update(s_ref[slot], g)

    @pl.loop(0, i // 2)
    def _(j):
        past_group(2 * j, 0)
        past_group(2 * j + 1, 1)

    def own_group(slot):
        causal = (lax.broadcasted_iota(jnp.int32, (TQ, TQ), 0) >= lax.broadcasted_iota(jnp.int32, (TQ, TQ), 1))
        update(jnp.where(causal, s_ref[slot], MASKED), i)
        o_ref[0] = acc_ref[:, 0:HEAD_DIM] / acc_ref[:, HEAD_DIM:]

    @pl.when(i % 2 == 0)
    def _():
        own_group(0)

    @pl.when(i % 2 == 1)
    def _():
        past_group(i - 1, 0)
        own_group(1)


def moba_prompt(q, k, v):
    bsz, seq, width = q.shape
    n_heads = width // HEAD_DIM
    assert seq % TQ == 0
    nb = seq // MOBA_BLOCK
    nbp = -(-nb // 8) * 8
    assert nb + 6 <= LANES
    full = pl.BlockSpec((1, seq, HEAD_DIM), lambda b, h, i: (b, 0, h))
    tile = pl.BlockSpec((1, TQ, HEAD_DIM), lambda b, h, i: (b, i, h))
    return pl.pallas_call(
        functools.partial(_moba_kernel, n_heads=n_heads, nb=nb),
        grid=(bsz, n_heads, seq // TQ),
        in_specs=[full, full, full],
        out_specs=tile,
        out_shape=jax.ShapeDtypeStruct((bsz, seq, width), F32),
        scratch_shapes=[pltpu.VMEM((seq, DA), BF16), pltpu.VMEM((seq, DA), BF16), pltpu.VMEM((seq, DA), BF16),
                        pltpu.VMEM((nbp, HEAD_DIM), F32), pltpu.VMEM((TQ, 1), F32), pltpu.VMEM((TQ, DA), F32),
                        pltpu.VMEM((2, TQ, TQ), F32)],
        compiler_params=_params("parallel", "parallel", "arbitrary"),
        name="moba_prompt",
    )(q, k, v)


PAGES_IN_FLIGHT = 8
SUM_CHAINS = 4


def _gate_step_kernel(pt_ref, q_ref, k_hbm, sel_ref, buf, sem, ksum, *, layer, pages_per_block, n_heads):
    total = pt_ref.shape[0]
    bd = q_ref.shape[0]
    nblk = ksum.shape[0] // bd
    depth = buf.shape[0]

    def page_copy(t, slot):
        return pltpu.make_async_copy(k_hbm.at[layer, pt_ref[t]], buf.at[slot], sem.at[slot])

    for t in range(min(depth, total)):
        page_copy(t, t).start()
    ksum[...] = jnp.zeros_like(ksum)

    @pl.loop(0, total)
    def _(t):
        slot = t % depth
        page_copy(t, slot).wait()
        page = buf[slot].reshape(SUM_CHAINS, PAGE_SIZE // SUM_CHAINS, n_heads, HEAD_DIM)
        ksum[t // pages_per_block] += jnp.sum(jnp.sum(page, axis=1), axis=0)

        @pl.when(t + depth < total)
        def _():
            page_copy(t + depth, slot).start()

    sel_ref[...] = jnp.zeros_like(sel_ref)
    for b in range(bd):
        kmean = ksum[b * nblk:(b + 1) * nblk] * (1.0 / MOBA_BLOCK)
        gate = jnp.sum(kmean * q_ref[b][None], axis=-1)
        blk = lax.broadcasted_iota(jnp.int32, gate.shape, 0)
        for r in range(MOBA_TOPK):
            m = jnp.max(gate, axis=0, keepdims=True)
            idx = jnp.min(jnp.where(gate == m, blk, nblk), axis=0, keepdims=True)
            sel_ref[b, r:r + 1, 0:n_heads] = idx
            gate = jnp.where(blk == idx, -jnp.inf, gate)


SEL_ROWS = 8


def gate_step(cache_k, layer, page_table, q):
    bd, n_pages = page_table.shape
    n_heads = q.shape[1]
    pages_per_block = MOBA_BLOCK // PAGE_SIZE
    assert n_pages % pages_per_block == 0 and MOBA_TOPK <= SEL_ROWS and n_heads <= LANES
    nblk = n_pages // pages_per_block
    sel = pl.pallas_call(
        functools.partial(_gate_step_kernel, layer=layer, pages_per_block=pages_per_block, n_heads=n_heads),
        in_specs=[pl.BlockSpec(memory_space=pltpu.SMEM), pl.BlockSpec(memory_space=pltpu.VMEM),
                  pl.BlockSpec(memory_space=pl.ANY)],
        out_specs=pl.BlockSpec(memory_space=pltpu.VMEM),
        out_shape=jax.ShapeDtypeStruct((bd, SEL_ROWS, LANES), jnp.int32),
        scratch_shapes=[pltpu.VMEM((PAGES_IN_FLIGHT, PAGE_SIZE, n_heads, HEAD_DIM), F32),
                        pltpu.SemaphoreType.DMA((PAGES_IN_FLIGHT,)),
                        pltpu.VMEM((bd * nblk, n_heads, HEAD_DIM), F32)],
        compiler_params=pltpu.CompilerParams(vmem_limit_bytes=VMEM_LIMIT),
        name="gate_step",
    )(page_table.reshape(-1), q, cache_k)
    return sel[:, :MOBA_TOPK, :n_heads]


def _attn_step_kernel(pt_ref, sel_ref, q_ref, kn_ref, vn_ref, k_hbm, v_hbm, o_ref, kbuf, vbuf, ksem, vsem,
                      m_ref, l_ref, acc_ref, *, layer, n_pages, q_pos, pages_per_block):
    bd, n_heads, _ = q_ref.shape
    per_head = MOBA_TOPK * pages_per_block
    total = bd * n_heads * per_head
    depth = kbuf.shape[0]
    scale = HEAD_DIM ** -0.5
    own = q_pos // MOBA_BLOCK

    def coords(t):
        bh, j = t // per_head, t % per_head
        b, h = bh // n_heads, bh % n_heads
        return b, h, j, sel_ref[(b * MOBA_TOPK + j // pages_per_block) * n_heads + h]

    def page_copies(t, slot):
        b, _, j, n = coords(t)
        page = pt_ref[b * n_pages + n * pages_per_block + j % pages_per_block]
        return (pltpu.make_async_copy(k_hbm.at[layer, page], kbuf.at[slot], ksem.at[slot]),
                pltpu.make_async_copy(v_hbm.at[layer, page], vbuf.at[slot], vsem.at[slot]))

    for t in range(min(depth, total)):
        for c in page_copies(t, t):
            c.start()

    @pl.loop(0, total)
    def _(t):
        slot = t % depth
        b, h, j, n = coords(t)
        for c in page_copies(t, slot):
            c.wait()
        q = q_ref[b, pl.ds(h, 1), :]

        @pl.when(j == 0)
        def _():
            m_ref[...] = jnp.sum(q * kn_ref[b, pl.ds(h, 1), :], axis=1, keepdims=True) * scale
            l_ref[...] = jnp.ones_like(l_ref)
            acc_ref[...] = vn_ref[b, pl.ds(h, 1), :]

        k = kbuf[slot, pl.ds(h, PAGE_SIZE, stride=n_heads), :]
        v = vbuf[slot, pl.ds(h, PAGE_SIZE, stride=n_heads), :]
        slope = jnp.exp2(jnp.asarray(h + 1, F32) * (-8.0 / n_heads) + jnp.zeros((1, 1), F32))
        kpos = (n * MOBA_BLOCK + (j % pages_per_block) * PAGE_SIZE
                + lax.broadcasted_iota(jnp.int32, (PAGE_SIZE, 1), 0))
        dist = (q_pos - kpos).astype(F32)
        s = jnp.sum(k * q, axis=1, keepdims=True) * scale - slope * dist
        s = jnp.where((dist >= 0.0) & (n < own), s, MASKED)
        m = m_ref[...]
        m_new = jnp.maximum(m, jnp.max(s, axis=0, keepdims=True))
        alpha = jnp.exp(m - m_new)
        p = jnp.exp(s - m_new)
        l_ref[...] = alpha * l_ref[...] + jnp.sum(p, axis=0, keepdims=True)
        acc_ref[...] = alpha * acc_ref[...] + jnp.sum(p * v, axis=0, keepdims=True)
        m_ref[...] = m_new

        @pl.when(j == per_head - 1)
        def _():
            o_ref[b, pl.ds(h, 1), :] = acc_ref[...] / l_ref[...]

        @pl.when(t + depth < total)
        def _():
            for c in page_copies(t + depth, slot):
                c.start()


def attn_step(cache_k, cache_v, layer, page_table, sel, q, k_new, v_new):
    bd, n_pages = page_table.shape
    n_heads = q.shape[1]
    pages_per_block = MOBA_BLOCK // PAGE_SIZE
    assert n_pages % pages_per_block == 0
    rows = PAGE_SIZE * n_heads
    as_rows = lambda c: c.reshape(c.shape[0], c.shape[1], rows, HEAD_DIM)
    vmem, smem = pl.BlockSpec(memory_space=pltpu.VMEM), pl.BlockSpec(memory_space=pltpu.SMEM)
    hbm = pl.BlockSpec(memory_space=pl.ANY)
    return pl.pallas_call(
        functools.partial(_attn_step_kernel, layer=layer, n_pages=n_pages, q_pos=n_pages * PAGE_SIZE,
                          pages_per_block=pages_per_block),
        in_specs=[smem, smem, vmem, vmem, vmem, hbm, hbm],
        out_specs=vmem,
        out_shape=jax.ShapeDtypeStruct((bd, n_heads, HEAD_DIM), F32),
        scratch_shapes=[pltpu.VMEM((PAGES_IN_FLIGHT, rows, HEAD_DIM), F32),
                        pltpu.VMEM((PAGES_IN_FLIGHT, rows, HEAD_DIM), F32),
                        pltpu.SemaphoreType.DMA((PAGES_IN_FLIGHT,)), pltpu.SemaphoreType.DMA((PAGES_IN_FLIGHT,)),
                        pltpu.VMEM((1, 1), F32), pltpu.VMEM((1, 1), F32), pltpu.VMEM((1, HEAD_DIM), F32)],
        compiler_params=pltpu.CompilerParams(vmem_limit_bytes=VMEM_LIMIT),
        name="attn_step",
    )(page_table.reshape(-1), sel.reshape(-1), q, k_new, v_new, as_rows(cache_k), as_rows(cache_v))


P_HALO = 16


def _pool_out_kernel(o_ref, u_ref, x_ref, pw_ref, ps_ref, wo_ref, y_ref, newp_ref, uext, mix):
    l = pl.program_id(1)
    tl = u_ref.shape[1]
    wp = u_ref.shape[2]
    wo_attn = o_ref.shape[2]
    pg = wp // len(POOL_WINDOWS)

    @pl.when(l == 0)
    def _():
        uext[0:P_HALO, :] = jnp.zeros((P_HALO, wp), F32)

    @pl.when(l > 0)
    def _():
        uext[0:P_HALO, :] = uext[tl:tl + P_HALO, :]

    uext[P_HALO:P_HALO + tl, :] = u_ref[0]
    mix[:, 0:wo_attn] = o_ref[0].astype(BF16)
    pos = l * tl + lax.broadcasted_iota(jnp.int32, (tl, pg), 0)
    for g, w in enumerate(POOL_WINDOWS):
        c0 = g * pg
        wsum = uext[P_HALO:P_HALO + tl, c0:c0 + pg]
        for j in range(1, w):
            wsum = wsum + uext[P_HALO - j:P_HALO - j + tl, c0:c0 + pg]
        count = jnp.minimum(pos + 1, w).astype(F32)
        dlt = (wsum / count - uext[P_HALO:P_HALO + tl, c0:c0 + pg]).astype(BF16)
        yp = jnp.dot(dlt, pw_ref[g], preferred_element_type=F32) * ps_ref[:, c0:c0 + pg]
        mix[:, wo_attn + c0:wo_attn + c0 + pg] = yp.astype(BF16)

    y_ref[0] = x_ref[0] + jnp.dot(mix[...], wo_ref[...], preferred_element_type=F32)

    @pl.when(l == pl.num_programs(1) - 1)
    def _():
        newp_ref[0] = uext[P_HALO + tl - (POOL_MAX - 1):P_HALO + tl, :]


def pool_out(o, u, x, pool_w, pool_scale, w_out, tl):
    bsz, seq, d = x.shape
    wp = u.shape[-1]
    tl = min(tl, seq)
    assert seq % tl == 0 and tl >= P_HALO
    const = _resident
    return pl.pallas_call(
        _pool_out_kernel,
        grid=(bsz, seq // tl),
        in_specs=[pl.BlockSpec((1, tl, o.shape[-1]), lambda b, l: (b, l, 0)),
                  pl.BlockSpec((1, tl, wp), lambda b, l: (b, l, 0)),
                  pl.BlockSpec((1, tl, d), lambda b, l: (b, l, 0)),
                  const(pool_w.shape), const((1, wp)), const(w_out.shape)],
        out_specs=[pl.BlockSpec((1, tl, d), lambda b, l: (b, l, 0)),
                   pl.BlockSpec((1, POOL_MAX - 1, wp), lambda b, l: (b, 0, 0))],
        out_shape=[jax.ShapeDtypeStruct((bsz, seq, d), F32),
                   jax.ShapeDtypeStruct((bsz, POOL_MAX - 1, wp), F32)],
        scratch_shapes=[pltpu.VMEM((P_HALO + tl, wp), F32), pltpu.VMEM((tl, o.shape[-1] + wp), BF16)],
        compiler_params=_params("parallel", "arbitrary"),
        name="pool_out",
    )(o, u, x, pool_w, pool_scale.reshape(1, wp), w_out)


def _pool_step_kernel(o_ref, u_ref, x_ref, stp_ref, pw_ref, ps_ref, wo_ref, y_ref, *, start_pos):
    wp = u_ref.shape[1]
    pg = wp // len(POOL_WINDOWS)
    u = u_ref[...]
    parts = [o_ref[...].astype(BF16)]
    for g, w in enumerate(POOL_WINDOWS):
        c0 = g * pg
        wsum = u[:, c0:c0 + pg]
        for j in range(1, w):
            wsum = wsum + stp_ref[POOL_MAX - 1 - j][:, c0:c0 + pg]
        count = float(min(start_pos + 1, w))
        dlt = (wsum / count - u[:, c0:c0 + pg]).astype(BF16)
        yp = jnp.dot(dlt, pw_ref[g], preferred_element_type=F32) * ps_ref[:, c0:c0 + pg]
        parts.append(yp.astype(BF16))
    mix = jnp.concatenate(parts, axis=-1)
    y_ref[...] = x_ref[...] + jnp.dot(mix, wo_ref[...], preferred_element_type=F32)


def pool_step(o, u, x, st_p, pool_w, pool_scale, w_out, start_pos):
    bd, d = x.shape
    return pl.pallas_call(
        functools.partial(_pool_step_kernel, start_pos=start_pos),
        out_shape=jax.ShapeDtypeStruct((bd, d), F32),
        compiler_params=pltpu.CompilerParams(vmem_limit_bytes=VMEM_LIMIT),
        name="pool_step",
    )(o, u, x, st_p, pool_w, pool_scale.reshape(1, -1), w_out)


TM_PROMPT = 512
TM_FFN = 1024
TF = 256
TL_MIX = 256


def kernel(x_prompt, x_sample, state_conv_a, state_conv_b, cache_k, cache_v, state_pool, page_table, conv_norm_g, conv_w_in, conv_a_dw, conv_a_dw_b, conv_a_ln_g, conv_a_ln_b, conv_b_dw, conv_w_out, attn_norm_g, attn_w_in, pool_w, pool_scale, attn_w_out, ffn_norm_g, ffn_w_gate, ffn_w_up, ffn_w_down, final_norm_g):
    bsz, seq, d = x_prompt.shape
    bd, dec_seq, _ = x_sample.shape
    assert dec_seq == 1
    depth = ffn_norm_g.shape[0]
    n_heads = cache_k.shape[3]
    wattn = n_heads * HEAD_DIM
    past_len = page_table.shape[1] * PAGE_SIZE

    layer_bf16 = lambda w, n: w[n].astype(BF16)

    xp = x_prompt.reshape(bsz * seq, d)
    xs = x_sample.reshape(bd, d)
    pa, pb, pk, pv, pp = [], [], [], [], []
    sa, sb, sk, sv, sp = [], [], [], [], []
    for layer in range(depth):
        i = layer // 2
        if layer % 2 == 0:
            w_in, w_out = layer_bf16(conv_w_in, i), layer_bf16(conv_w_out, i)
            parts = norm_matmul(xp, conv_norm_g[i], w_in, TM_PROMPT)
            wa = parts[0].shape[-1]
            y, na, nb_ = conv_mix([t.reshape(bsz, seq, wa) for t in parts], xp.reshape(bsz, seq, d),
                                  conv_a_dw[i], conv_a_dw_b[i], conv_a_ln_g[i], conv_a_ln_b[i], conv_b_dw[i],
                                  w_out, TL_MIX)
            xp = y.reshape(bsz * seq, d)
            pa.append(na)
            pb.append(nb_)

            parts = norm_matmul(xs, conv_norm_g[i], w_in, TM_PROMPT)
            st_a = jnp.swapaxes(state_conv_a[i], 0, 1)
            st_b = jnp.swapaxes(state_conv_b[i], 0, 1)
            xs, a_new, c_new = conv_step(parts, xs, st_a, st_b, conv_a_dw[i], conv_a_dw_b[i], conv_a_ln_g[i],
                                         conv_a_ln_b[i], conv_b_dw[i], w_out)
            sa.append(jnp.concatenate([state_conv_a[i][:, 1:], a_new[:, None]], axis=1))
            sb.append(jnp.concatenate([state_conv_b[i][:, 1:], c_new[:, None]], axis=1))
        else:
            w_in, w_out, w_pool = layer_bf16(attn_w_in, i), layer_bf16(attn_w_out, i), layer_bf16(pool_w, i)
            q, k, v, u = norm_matmul(xp, attn_norm_g[i], w_in, TM_PROMPT)
            shp = (bsz, seq, wattn)
            o = moba_prompt(q.reshape(shp), k.reshape(shp), v.reshape(shp))
            y, npool = pool_out(o, u.reshape(bsz, seq, -1), xp.reshape(bsz, seq, d), w_pool, pool_scale[i],
                                w_out, TL_MIX)
            xp = y.reshape(bsz * seq, d)
            pk.append(k.reshape(bsz, seq, n_heads, HEAD_DIM))
            pv.append(v.reshape(bsz, seq, n_heads, HEAD_DIM))
            pp.append(npool)

            q, k, v, u = norm_matmul(xs, attn_norm_g[i], w_in, TM_PROMPT)
            heads = lambda t: t.reshape(bd, n_heads, HEAD_DIM)
            sel = gate_step(cache_k, i, page_table, heads(q))
            o = attn_step(cache_k, cache_v, i, page_table, sel, heads(q), heads(k), heads(v))
            st_p = jnp.swapaxes(state_pool[i], 0, 1)
            xs = pool_step(o.reshape(bd, wattn), u, xs, st_p, w_pool, pool_scale[i], w_out, past_len)
            sk.append(k.reshape(bd, 1, n_heads, HEAD_DIM))
            sv.append(v.reshape(bd, 1, n_heads, HEAD_DIM))
            sp.append(jnp.concatenate([state_pool[i][:, 1:], u[:, None]], axis=1))
        last = layer == depth - 1
        ffn_w = (ffn_w_gate, ffn_w_up, ffn_w_down, layer)
        xp = ffn(xp, ffn_norm_g[layer], *ffn_w, final_norm_g, TM_FFN, TF, last)
        xs = ffn(xs, ffn_norm_g[layer], *ffn_w, final_norm_g, TM_FFN, TF, last)
    return (xp.reshape(bsz, seq, d), xs.reshape(bd, 1, d), jnp.stack(pa), jnp.stack(sa), jnp.stack(pb),
            jnp.stack(sb), jnp.stack(pk), jnp.stack(sk), jnp.stack(pv), jnp.stack(sv), jnp.stack(pp), jnp.stack(sp))
```

```python
import functools

import jax
import jax.numpy as jnp
from jax import lax
from jax.experimental import pallas as pl
from jax.experimental.pallas import tpu as pltpu

EPS = 1e-6
PAGE_SIZE = 128
MOBA_BLOCK = 256
MOBA_TOPK = 3
HEAD_DIM = 128
POOL_WINDOWS = (2, 4, 8, 16)
POOL_MAX = max(POOL_WINDOWS)
MASKED = -1e30

V7X_VMEM_BYTES = 64 * 1024 * 1024
VMEM_LIMIT = V7X_VMEM_BYTES - 8 * 1024 * 1024
LANES = 128
SUBLANES = 8
COL_GROUP = 1024

BF16 = jnp.bfloat16
F32 = jnp.float32


def _params(*sem):
    return pltpu.CompilerParams(dimension_semantics=sem, vmem_limit_bytes=VMEM_LIMIT)


def _rms(x, g):
    ms = jnp.mean(x * x, axis=-1, keepdims=True)
    return x * lax.rsqrt(ms + EPS) * g


def _sigmoid(x):
    return 1.0 / (1.0 + jnp.exp(-x))


def _resident(shape):
    return pl.BlockSpec(shape, lambda *_: (0,) * len(shape), pipeline_mode=pl.Buffered(1))


def _norm_matmul_kernel(x_ref, g_ref, w_ref, *out_refs):
    h = _rms(x_ref[...], g_ref[...]).astype(BF16)
    for c, o_ref in enumerate(out_refs):
        o_ref[...] = jnp.dot(h, w_ref[:, c * COL_GROUP:(c + 1) * COL_GROUP], preferred_element_type=F32)


def norm_matmul(x, g, w, tm):
    m, d = x.shape
    n_out = w.shape[1] // COL_GROUP
    tm = min(tm, m)
    return pl.pallas_call(
        _norm_matmul_kernel,
        grid=(m // tm,),
        in_specs=[pl.BlockSpec((tm, d), lambda i: (i, 0)), _resident((1, d)), _resident(w.shape)],
        out_specs=[pl.BlockSpec((tm, COL_GROUP), lambda i: (i, 0))] * n_out,
        out_shape=[jax.ShapeDtypeStruct((m, COL_GROUP), F32)] * n_out,
        compiler_params=_params("parallel"),
        name="norm_matmul",
    )(x, g.reshape(1, d), w)


def _ffn_kernel(g_ref, fg_ref, wg_ref, wu_ref, wd_ref, x_hbm, o_ref, xbuf, h_ref, sem, *, final_norm):
    i, f = pl.program_id(0), pl.program_id(1)
    tm = o_ref.shape[0]

    def x_copy(tile):
        return pltpu.make_async_copy(x_hbm.at[pl.ds(pl.multiple_of(tile * tm, tm), tm), :], xbuf, sem)

    @pl.when((i == 0) & (f == 0))
    def _():
        x_copy(0).start()

    @pl.when(f == 0)
    def _():
        x_copy(i).wait()
        x = xbuf[...]
        h_ref[...] = _rms(x, g_ref[...]).astype(h_ref.dtype)
        o_ref[...] = x

    @pl.when((f == 1) & (i + 1 < pl.num_programs(0)))
    def _():
        x_copy(i + 1).start()

    h = h_ref[...]
    gate = jnp.dot(h, wg_ref[...].astype(BF16), preferred_element_type=F32)
    up = jnp.dot(h, wu_ref[...].astype(BF16), preferred_element_type=F32)
    act = (gate * _sigmoid(gate) * up).astype(BF16)
    o_ref[...] += jnp.dot(act, wd_ref[...].astype(BF16), preferred_element_type=F32)

    if final_norm:
        @pl.when(f == pl.num_programs(1) - 1)
        def _():
            o_ref[...] = _rms(o_ref[...], fg_ref[...])


def ffn(x, g, wg, wu, wd, layer, final_g, tm, tf, final_norm):
    m, d = x.shape
    dff = wg.shape[2]
    tm = min(tm, m)
    assert m % tm == 0 and dff % tf == 0 and dff // tf >= 2
    return pl.pallas_call(
        functools.partial(_ffn_kernel, final_norm=final_norm),
        grid=(m // tm, dff // tf),
        in_specs=[_resident((1, d)), _resident((1, d)),
                  pl.BlockSpec((None, d, tf), lambda i, f: (layer, 0, f)),
                  pl.BlockSpec((None, d, tf), lambda i, f: (layer, 0, f)),
                  pl.BlockSpec((None, tf, d), lambda i, f: (layer, f, 0)),
                  pl.BlockSpec(memory_space=pl.ANY)],
        out_specs=pl.BlockSpec((tm, d), lambda i, f: (i, 0)),
        out_shape=jax.ShapeDtypeStruct((m, d), F32),
        scratch_shapes=[pltpu.VMEM((tm, d), F32), pltpu.VMEM((tm, d), BF16), pltpu.SemaphoreType.DMA(())],
        compiler_params=_params("arbitrary", "arbitrary"),
        name="ffn",
    )(g.reshape(1, d), final_g.reshape(1, d), wg, wu, wd, x)


A_HALO = 32
B_HALO = 8


def _layer_norm_silu(x, g, b):
    mu = jnp.mean(x, axis=-1, keepdims=True)
    xc = x - mu
    y = xc * lax.rsqrt(jnp.mean(xc * xc, axis=-1, keepdims=True) + EPS) * g + b
    return y * _sigmoid(y)


def _conv_mix_kernel(av_ref, ag_ref, bh_ref, bb_ref, bc_ref, x_ref, adw_ref, adwb_ref, lng_ref, lnb_ref,
                     bdw_ref, wo_ref, y_ref, newa_ref, newb_ref, aext, cext, ashift, aconv, mix, *, ta, tb, rc):
    l = pl.program_id(1)
    tl = av_ref.shape[1]
    wa = av_ref.shape[2]

    @pl.when(l == 0)
    def _():
        aext[0:A_HALO, :] = jnp.zeros((A_HALO, wa), F32)
        cext[0:B_HALO, :] = jnp.zeros((B_HALO, wa), F32)

    @pl.when(l > 0)
    def _():
        aext[0:A_HALO, :] = aext[tl:tl + A_HALO, :]
        cext[0:B_HALO, :] = cext[tl:tl + B_HALO, :]

    aext[A_HALO:A_HALO + tl, :] = av_ref[0] * _sigmoid(ag_ref[0])
    cext[B_HALO:B_HALO + tl, :] = bc_ref[0] * bh_ref[0]

    n_shift = ashift.shape[1]
    for s in range(1, SUBLANES):
        ashift[s - 1] = aext[s:s + n_shift, :]

    for r0 in range(0, tl, rc):
        for c0 in range(0, wa, LANES):
            acc = jnp.broadcast_to(adwb_ref[:, c0:c0 + LANES], (rc, LANES))
            for k in range(ta):
                off = A_HALO - (ta - 1) + r0 + k
                s = off % SUBLANES
                src = aext if s == 0 else ashift.at[s - 1]
                acc = acc + adw_ref[k:k + 1, c0:c0 + LANES] * src[off - s:off - s + rc, c0:c0 + LANES]
            aconv[r0:r0 + rc, c0:c0 + LANES] = acc
    mix[:, 0:wa] = _layer_norm_silu(aconv[...], lng_ref[...], lnb_ref[...]).astype(BF16)

    bconv = jnp.zeros((tl, wa), F32)
    for k in range(tb):
        off = B_HALO - (tb - 1) + k
        bconv = bconv + bdw_ref[k:k + 1, :] * cext[off:off + tl, :]
    mix[:, wa:] = (bb_ref[0] * bconv).astype(BF16)

    y_ref[0] = x_ref[0] + jnp.dot(mix[...], wo_ref[...], preferred_element_type=F32)

    @pl.when(l == pl.num_programs(1) - 1)
    def _():
        newa_ref[0] = aext[A_HALO + tl - (ta - 1):A_HALO + tl, :]
        newb_ref[0] = cext[B_HALO + tl - (tb - 1):B_HALO + tl, :]


def conv_mix(parts, x, a_dw, a_dw_b, ln_g, ln_b, b_dw, w_out, tl):
    bsz, seq, d = x.shape
    wa = parts[0].shape[-1]
    ta, tb = a_dw.shape[0], b_dw.shape[0]
    tl = min(tl, seq)
    assert seq % tl == 0 and tl >= A_HALO and ta - 1 <= A_HALO and tb - 1 <= B_HALO
    part_spec = pl.BlockSpec((1, tl, wa), lambda b, l: (b, l, 0))
    const = _resident
    return pl.pallas_call(
        functools.partial(_conv_mix_kernel, ta=ta, tb=tb, rc=min(128, tl)),
        grid=(bsz, seq // tl),
        in_specs=[part_spec] * 5 + [pl.BlockSpec((1, tl, d), lambda b, l: (b, l, 0)),
                                    const((ta, wa)), const((1, wa)), const((1, wa)), const((1, wa)),
                                    const((tb, wa)), const(w_out.shape)],
        out_specs=[pl.BlockSpec((1, tl, d), lambda b, l: (b, l, 0)),
                   pl.BlockSpec((1, ta - 1, wa), lambda b, l: (b, 0, 0)),
                   pl.BlockSpec((1, tb - 1, wa), lambda b, l: (b, 0, 0))],
        out_shape=[jax.ShapeDtypeStruct((bsz, seq, d), F32),
                   jax.ShapeDtypeStruct((bsz, ta - 1, wa), F32),
                   jax.ShapeDtypeStruct((bsz, tb - 1, wa), F32)],
        scratch_shapes=[pltpu.VMEM((A_HALO + tl, wa), F32), pltpu.VMEM((B_HALO + tl, wa), F32),
                        pltpu.VMEM((SUBLANES - 1, A_HALO + tl - SUBLANES, wa), F32),
                        pltpu.VMEM((tl, wa), F32), pltpu.VMEM((tl, 2 * wa), BF16)],
        compiler_params=_params("parallel", "arbitrary"),
        name="conv_mix",
    )(*parts, x, a_dw, a_dw_b.reshape(1, wa), ln_g.reshape(1, wa), ln_b.reshape(1, wa), b_dw, w_out)


def _conv_step_kernel(av_ref, ag_ref, bh_ref, bb_ref, bc_ref, x_ref, sta_ref, stb_ref, adw_ref, adwb_ref,
                      lng_ref, lnb_ref, bdw_ref, wo_ref, y_ref, anew_ref, cnew_ref, *, ta, tb):
    a = av_ref[...] * _sigmoid(ag_ref[...])
    c = bc_ref[...] * bh_ref[...]
    anew_ref[...] = a
    cnew_ref[...] = c
    acc = adwb_ref[...] + adw_ref[ta - 1:ta, :] * a
    for k in range(ta - 1):
        acc = acc + adw_ref[k:k + 1, :] * sta_ref[k]
    a_out = _layer_norm_silu(acc, lng_ref[...], lnb_ref[...])
    bconv = bdw_ref[tb - 1:tb, :] * c
    for k in range(tb - 1):
        bconv = bconv + bdw_ref[k:k + 1, :] * stb_ref[k]
    mix = jnp.concatenate([a_out, bb_ref[...] * bconv], axis=-1).astype(BF16)
    y_ref[...] = x_ref[...] + jnp.dot(mix, wo_ref[...], preferred_element_type=F32)


def conv_step(parts, x, st_a, st_b, a_dw, a_dw_b, ln_g, ln_b, b_dw, w_out):
    bd, d = x.shape
    wa = parts[0].shape[-1]
    ta, tb = a_dw.shape[0], b_dw.shape[0]
    return pl.pallas_call(
        functools.partial(_conv_step_kernel, ta=ta, tb=tb),
        out_shape=[jax.ShapeDtypeStruct((bd, d), F32), jax.ShapeDtypeStruct((bd, wa), F32),
                   jax.ShapeDtypeStruct((bd, wa), F32)],
        compiler_params=pltpu.CompilerParams(vmem_limit_bytes=VMEM_LIMIT),
        name="conv_step",
    )(*parts, x, st_a, st_b, a_dw, a_dw_b.reshape(1, wa), ln_g.reshape(1, wa), ln_b.reshape(1, wa), b_dw, w_out)


LOG2E = 1.4426950408889634
TQ = 2 * MOBA_BLOCK
DA = 2 * HEAD_DIM


def _split3(x):
    hi = x.astype(BF16).astype(F32)
    mid = (x - hi).astype(BF16).astype(F32)
    lo = (x - hi - mid).astype(BF16).astype(F32)
    return hi, mid, lo


def _moba_setup(q_ref, k_ref, v_ref, qaug, kaug, vaug, kmean, slope2, nb):
    seq = k_ref.shape[1]
    bs = MOBA_BLOCK
    nbp = kmean.shape[0]
    k = k_ref[0]
    q = q_ref[0]
    kmean[...] = jnp.zeros_like(kmean)
    kmean[0:nb, :] = jnp.mean(k.reshape(nb, bs, HEAD_DIM), axis=1)

    gate = lax.dot_general(kmean[...], q, (((1,), (1,)), ((), ())), precision=lax.Precision.HIGHEST,
                           preferred_element_type=F32)
    blk = lax.broadcasted_iota(jnp.int32, (nbp, seq), 0)
    own = lax.broadcasted_iota(jnp.int32, (nbp, seq), 1) // bs
    gate = jnp.where(blk < own, gate, -jnp.inf)
    attend = blk == own
    for _ in range(MOBA_TOPK):
        m = jnp.max(gate, axis=0, keepdims=True)
        idx = jnp.min(jnp.where(gate == m, blk, nbp), axis=0, keepdims=True)
        hit = blk == idx
        attend = attend | (hit & (blk < own))
        gate = jnp.where(hit, -jnp.inf, gate)
    attend_t = jnp.where(attend, 1.0, 0.0)
    attend_r = jnp.concatenate([attend_t, jnp.zeros((LANES - nbp, seq), F32)], axis=0).T

    lane = lax.broadcasted_iota(jnp.int32, (seq, LANES), 1)
    pos_i = lax.broadcasted_iota(jnp.int32, (seq, LANES), 0)
    pos = pos_i.astype(F32)

    khi, kmid, klo = _split3(slope2 * pos)
    ek = jnp.where(lane == pos_i // bs, 1.0, 0.0)
    ek = jnp.where(lane == nb, khi, ek)
    ek = jnp.where(lane == nb + 1, kmid, ek)
    ek = jnp.where(lane == nb + 2, klo, ek)
    ek = jnp.where((lane >= nb + 3) & (lane < nb + 6), 1.0, ek)
    kaug[:, 0:HEAD_DIM] = k.astype(BF16)
    kaug[:, HEAD_DIM:] = ek.astype(BF16)

    qhi, qmid, qlo = _split3(-slope2 * pos)
    eq = jnp.where(lane < nb, jnp.where(attend_r > 0.5, 0.0, MASKED), 0.0)
    eq = jnp.where((lane >= nb) & (lane < nb + 3), 1.0, eq)
    eq = jnp.where(lane == nb + 3, qhi, eq)
    eq = jnp.where(lane == nb + 4, qmid, eq)
    eq = jnp.where(lane == nb + 5, qlo, eq)
    qaug[:, 0:HEAD_DIM] = (q * (HEAD_DIM ** -0.5 * LOG2E)).astype(BF16)
    qaug[:, HEAD_DIM:] = eq.astype(BF16)

    vaug[:, 0:HEAD_DIM] = v_ref[0].astype(BF16)
    vaug[:, HEAD_DIM:] = jnp.ones((seq, HEAD_DIM), BF16)


SUM_CHAINS = 4


def _page_sums_step(step, pt_ref, kc_hbm, ksum_ref, ring, sem, *, layer, pages_per_block):
    total = pt_ref.shape[0]
    depth, page_rows, n_heads, _ = ring.shape

    def page_copy(t, slot):
        return pltpu.make_async_copy(kc_hbm.at[layer, pt_ref[t]], ring.at[slot], sem.at[slot])

    @pl.when(step == 0)
    def _():
        ksum_ref[...] = jnp.zeros_like(ksum_ref)
        for slot in range(min(depth, total)):
            page_copy(slot, slot).start()

    for slot in range(depth):
        t = step * depth + slot

        @pl.when(t < total)
        def _(t=t, slot=slot):
            page_copy(t, slot).wait()
            page = ring[slot].reshape(SUM_CHAINS, page_rows // SUM_CHAINS, n_heads, HEAD_DIM)
            ksum_ref[t // pages_per_block] += jnp.sum(jnp.sum(page, axis=1), axis=0)

            @pl.when(t + depth < total)
            def _():
                page_copy(t + depth, slot).start()


def _moba_kernel(pt_ref, q_ref, k_ref, v_ref, kc_hbm, o_ref, ksum_ref, qaug, kaug, vaug, kmean, m_ref, acc_ref,
                 s_ref, ring, psem, *, n_heads, nb, layer):
    h = pl.program_id(1)
    i = pl.program_id(2)
    bs = MOBA_BLOCK
    nt = (((1,), (1,)), ((), ()))
    step = (pl.program_id(0) * pl.num_programs(1) + h) * pl.num_programs(2) + i
    _page_sums_step(step, pt_ref, kc_hbm, ksum_ref, ring, psem, layer=layer,
                    pages_per_block=MOBA_BLOCK // PAGE_SIZE)

    @pl.when(i == 0)
    def _():
        slope2 = jnp.exp2((h + 1).astype(F32) * (-8.0 / n_heads) + jnp.zeros((1, 1), F32)) * LOG2E
        _moba_setup(q_ref, k_ref, v_ref, qaug, kaug, vaug, kmean, slope2, nb)

    r0 = pl.multiple_of(i * TQ, TQ)
    qa = qaug[pl.ds(r0, TQ), :]

    halves = (slice(0, bs), slice(bs, TQ))

    def scores(g):
        kg = kaug[pl.ds(pl.multiple_of(g * TQ, TQ), TQ), :]
        return jnp.concatenate([lax.dot_general(qa[r], kg, nt, preferred_element_type=F32) for r in halves], axis=0)

    def update(s, g):
        vg = vaug[pl.ds(pl.multiple_of(g * TQ, TQ), TQ), :]
        for r in halves:
            m = m_ref[r, :]
            m_new = jnp.maximum(m, jnp.max(s[r], axis=1, keepdims=True))
            p = jnp.exp2(s[r] - m_new).astype(BF16)
            pv = jnp.dot(p, vg, preferred_element_type=F32)
            acc_ref[r, :] = jnp.exp2(m - m_new) * acc_ref[r, :] + pv
            m_ref[r, :] = m_new

    m_ref[...] = jnp.full(m_ref.shape, MASKED, F32)
    acc_ref[...] = jnp.zeros_like(acc_ref)
    s_ref[0] = scores(0)

    def past_group(g, slot):
        s_ref[1 - slot] = scores(g + 1)
        update(s_ref[slot], g)

    @pl.loop(0, i // 2)
    def _(j):
        past_group(2 * j, 0)
        past_group(2 * j + 1, 1)

    def own_group(slot):
        causal = (lax.broadcasted_iota(jnp.int32, (TQ, TQ), 0) >= lax.broadcasted_iota(jnp.int32, (TQ, TQ), 1))
        update(jnp.where(causal, s_ref[slot], MASKED), i)
        o_ref[0] = acc_ref[:, 0:HEAD_DIM] / acc_ref[:, HEAD_DIM:]

    @pl.when(i % 2 == 0)
    def _():
        own_group(0)

    @pl.when(i % 2 == 1)
    def _():
        past_group(i - 1, 0)
        own_group(1)


def moba_prompt(q, k, v, cache_k, layer, page_table):
    bsz, seq, width = q.shape
    n_heads = width // HEAD_DIM
    assert seq % TQ == 0
    nb = seq // MOBA_BLOCK
    nbp = -(-nb // 8) * 8
    assert nb + 6 <= LANES
    bd, n_pages = page_table.shape
    pages_per_block = MOBA_BLOCK // PAGE_SIZE
    assert n_pages % pages_per_block == 0
    grid = (bsz, n_heads, seq // TQ)
    pages_per_step = -(-bd * n_pages // (grid[0] * grid[1] * grid[2]))
    sums_shape = (bd * n_pages // pages_per_block, n_heads, HEAD_DIM)
    full = pl.BlockSpec((1, seq, HEAD_DIM), lambda b, h, i: (b, 0, h))
    tile = pl.BlockSpec((1, TQ, HEAD_DIM), lambda b, h, i: (b, i, h))
    return pl.pallas_call(
        functools.partial(_moba_kernel, n_heads=n_heads, nb=nb, layer=layer),
        grid=grid,
        in_specs=[pl.BlockSpec(memory_space=pltpu.SMEM), full, full, full, pl.BlockSpec(memory_space=pl.ANY)],
        out_specs=[tile, pl.BlockSpec(sums_shape, lambda b, h, i: (0, 0, 0))],
        out_shape=[jax.ShapeDtypeStruct((bsz, seq, width), F32), jax.ShapeDtypeStruct(sums_shape, F32)],
        scratch_shapes=[pltpu.VMEM((seq, DA), BF16), pltpu.VMEM((seq, DA), BF16), pltpu.VMEM((seq, DA), BF16),
                        pltpu.VMEM((nbp, HEAD_DIM), F32), pltpu.VMEM((TQ, 1), F32), pltpu.VMEM((TQ, DA), F32),
                        pltpu.VMEM((2, TQ, TQ), F32),
                        pltpu.VMEM((pages_per_step, PAGE_SIZE, n_heads, HEAD_DIM), F32),
                        pltpu.SemaphoreType.DMA((pages_per_step,))],
        compiler_params=_params("arbitrary", "arbitrary", "arbitrary"),
        name="moba_prompt",
    )(page_table.reshape(-1), q, k, v, cache_k)


PAGES_IN_FLIGHT = 8


def _gate_step_kernel(q_ref, ksum_ref, sel_ref, *, n_heads):
    bd = q_ref.shape[0]
    nblk = ksum_ref.shape[0] // bd
    sel_ref[...] = jnp.zeros_like(sel_ref)
    for b in range(bd):
        kmean = ksum_ref[b * nblk:(b + 1) * nblk] * (1.0 / MOBA_BLOCK)
        gate = jnp.sum(kmean * q_ref[b][None], axis=-1)
        blk = lax.broadcasted_iota(jnp.int32, gate.shape, 0)
        for r in range(MOBA_TOPK):
            m = jnp.max(gate, axis=0, keepdims=True)
            idx = jnp.min(jnp.where(gate == m, blk, nblk), axis=0, keepdims=True)
            sel_ref[b, r:r + 1, 0:n_heads] = idx
            gate = jnp.where(blk == idx, -jnp.inf, gate)


SEL_ROWS = 8


def gate_step(block_sums, q):
    bd, n_heads, _ = q.shape
    assert MOBA_TOPK <= SEL_ROWS and n_heads <= LANES
    sel = pl.pallas_call(
        functools.partial(_gate_step_kernel, n_heads=n_heads),
        out_shape=jax.ShapeDtypeStruct((bd, SEL_ROWS, LANES), jnp.int32),
        compiler_params=pltpu.CompilerParams(vmem_limit_bytes=VMEM_LIMIT),
        name="gate_step",
    )(q, block_sums)
    return sel[:, :MOBA_TOPK, :n_heads]


def _attn_step_kernel(pt_ref, sel_ref, q_ref, kn_ref, vn_ref, k_hbm, v_hbm, o_ref, kbuf, vbuf, ksem, vsem,
                      m_ref, l_ref, acc_ref, *, layer, n_pages, q_pos, pages_per_block):
    bd, n_heads, _ = q_ref.shape
    per_head = MOBA_TOPK * pages_per_block
    total = bd * n_heads * per_head
    depth = kbuf.shape[0]
    scale = HEAD_DIM ** -0.5
    own = q_pos // MOBA_BLOCK

    def coords(t):
        bh, j = t // per_head, t % per_head
        b, h = bh // n_heads, bh % n_heads
        return b, h, j, sel_ref[(b * MOBA_TOPK + j // pages_per_block) * n_heads + h]

    def page_copies(t, slot):
        b, _, j, n = coords(t)
        page = pt_ref[b * n_pages + n * pages_per_block + j % pages_per_block]
        return (pltpu.make_async_copy(k_hbm.at[layer, page], kbuf.at[slot], ksem.at[slot]),
                pltpu.make_async_copy(v_hbm.at[layer, page], vbuf.at[slot], vsem.at[slot]))

    for t in range(min(depth, total)):
        for c in page_copies(t, t):
            c.start()

    @pl.loop(0, total)
    def _(t):
        slot = t % depth
        b, h, j, n = coords(t)
        for c in page_copies(t, slot):
            c.wait()
        q = q_ref[b, pl.ds(h, 1), :]

        @pl.when(j == 0)
        def _():
            m_ref[...] = jnp.sum(q * kn_ref[b, pl.ds(h, 1), :], axis=1, keepdims=True) * scale
            l_ref[...] = jnp.ones_like(l_ref)
            acc_ref[...] = vn_ref[b, pl.ds(h, 1), :]

        k = kbuf[slot, pl.ds(h, PAGE_SIZE, stride=n_heads), :]
        v = vbuf[slot, pl.ds(h, PAGE_SIZE, stride=n_heads), :]
        slope = jnp.exp2(jnp.asarray(h + 1, F32) * (-8.0 / n_heads) + jnp.zeros((1, 1), F32))
        kpos = (n * MOBA_BLOCK + (j % pages_per_block) * PAGE_SIZE
                + lax.broadcasted_iota(jnp.int32, (PAGE_SIZE, 1), 0))
        dist = (q_pos - kpos).astype(F32)
        s = jnp.sum(k * q, axis=1, keepdims=True) * scale - slope * dist
        s = jnp.where((dist >= 0.0) & (n < own), s, MASKED)
        m = m_ref[...]
        m_new = jnp.maximum(m, jnp.max(s, axis=0, keepdims=True))
        alpha = jnp.exp(m - m_new)
        p = jnp.exp(s - m_new)
        l_ref[...] = alpha * l_ref[...] + jnp.sum(p, axis=0, keepdims=True)
        acc_ref[...] = alpha * acc_ref[...] + jnp.sum(p * v, axis=0, keepdims=True)
        m_ref[...] = m_new

        @pl.when(j == per_head - 1)
        def _():
            o_ref[b, pl.ds(h, 1), :] = acc_ref[...] / l_ref[...]

        @pl.when(t + depth < total)
        def _():
            for c in page_copies(t + depth, slot):
                c.start()


def attn_step(cache_k, cache_v, layer, page_table, sel, q, k_new, v_new):
    bd, n_pages = page_table.shape
    n_heads = q.shape[1]
    pages_per_block = MOBA_BLOCK // PAGE_SIZE
    assert n_pages % pages_per_block == 0
    rows = PAGE_SIZE * n_heads
    as_rows = lambda c: c.reshape(c.shape[0], c.shape[1], rows, HEAD_DIM)
    vmem, smem = pl.BlockSpec(memory_space=pltpu.VMEM), pl.BlockSpec(memory_space=pltpu.SMEM)
    hbm = pl.BlockSpec(memory_space=pl.ANY)
    return pl.pallas_call(
        functools.partial(_attn_step_kernel, layer=layer, n_pages=n_pages, q_pos=n_pages * PAGE_SIZE,
                          pages_per_block=pages_per_block),
        in_specs=[smem, smem, vmem, vmem, vmem, hbm, hbm],
        out_specs=vmem,
        out_shape=jax.ShapeDtypeStruct((bd, n_heads, HEAD_DIM), F32),
        scratch_shapes=[pltpu.VMEM((PAGES_IN_FLIGHT, rows, HEAD_DIM), F32),
                        pltpu.VMEM((PAGES_IN_FLIGHT, rows, HEAD_DIM), F32),
                        pltpu.SemaphoreType.DMA((PAGES_IN_FLIGHT,)), pltpu.SemaphoreType.DMA((PAGES_IN_FLIGHT,)),
                        pltpu.VMEM((1, 1), F32), pltpu.VMEM((1, 1), F32), pltpu.VMEM((1, HEAD_DIM), F32)],
        compiler_params=pltpu.CompilerParams(vmem_limit_bytes=VMEM_LIMIT),
        name="attn_step",
    )(page_table.reshape(-1), sel.reshape(-1), q, k_new, v_new, as_rows(cache_k), as_rows(cache_v))


P_HALO = 16


def _pool_out_kernel(o_ref, u_ref, x_ref, pw_ref, ps_ref, wo_ref, y_ref, newp_ref, uext, mix):
    l = pl.program_id(1)
    tl = u_ref.shape[1]
    wp = u_ref.shape[2]
    wo_attn = o_ref.shape[2]
    pg = wp // len(POOL_WINDOWS)

    @pl.when(l == 0)
    def _():
        uext[0:P_HALO, :] = jnp.zeros((P_HALO, wp), F32)

    @pl.when(l > 0)
    def _():
        uext[0:P_HALO, :] = uext[tl:tl + P_HALO, :]

    uext[P_HALO:P_HALO + tl, :] = u_ref[0]
    mix[:, 0:wo_attn] = o_ref[0].astype(BF16)
    pos = l * tl + lax.broadcasted_iota(jnp.int32, (tl, pg), 0)
    for g, w in enumerate(POOL_WINDOWS):
        c0 = g * pg
        wsum = uext[P_HALO:P_HALO + tl, c0:c0 + pg]
        for j in range(1, w):
            wsum = wsum + uext[P_HALO - j:P_HALO - j + tl, c0:c0 + pg]
        count = jnp.minimum(pos + 1, w).astype(F32)
        dlt = (wsum / count - uext[P_HALO:P_HALO + tl, c0:c0 + pg]).astype(BF16)
        yp = jnp.dot(dlt, pw_ref[g], preferred_element_type=F32) * ps_ref[:, c0:c0 + pg]
        mix[:, wo_attn + c0:wo_attn + c0 + pg] = yp.astype(BF16)

    y_ref[0] = x_ref[0] + jnp.dot(mix[...], wo_ref[...], preferred_element_type=F32)

    @pl.when(l == pl.num_programs(1) - 1)
    def _():
        newp_ref[0] = uext[P_HALO + tl - (POOL_MAX - 1):P_HALO + tl, :]


def pool_out(o, u, x, pool_w, pool_scale, w_out, tl):
    bsz, seq, d = x.shape
    wp = u.shape[-1]
    tl = min(tl, seq)
    assert seq % tl == 0 and tl >= P_HALO
    const = _resident
    return pl.pallas_call(
        _pool_out_kernel,
        grid=(bsz, seq // tl),
        in_specs=[pl.BlockSpec((1, tl, o.shape[-1]), lambda b, l: (b, l, 0)),
                  pl.BlockSpec((1, tl, wp), lambda b, l: (b, l, 0)),
                  pl.BlockSpec((1, tl, d), lambda b, l: (b, l, 0)),
                  const(pool_w.shape), const((1, wp)), const(w_out.shape)],
        out_specs=[pl.BlockSpec((1, tl, d), lambda b, l: (b, l, 0)),
                   pl.BlockSpec((1, POOL_MAX - 1, wp), lambda b, l: (b, 0, 0))],
        out_shape=[jax.ShapeDtypeStruct((bsz, seq, d), F32),
                   jax.ShapeDtypeStruct((bsz, POOL_MAX - 1, wp), F32)],
        scratch_shapes=[pltpu.VMEM((P_HALO + tl, wp), F32), pltpu.VMEM((tl, o.shape[-1] + wp), BF16)],
        compiler_params=_params("parallel", "arbitrary"),
        name="pool_out",
    )(o, u, x, pool_w, pool_scale.reshape(1, wp), w_out)


def _pool_step_kernel(o_ref, u_ref, x_ref, stp_ref, pw_ref, ps_ref, wo_ref, y_ref, *, start_pos):
    wp = u_ref.shape[1]
    pg = wp // len(POOL_WINDOWS)
    u = u_ref[...]
    parts = [o_ref[...].astype(BF16)]
    for g, w in enumerate(POOL_WINDOWS):
        c0 = g * pg
        wsum = u[:, c0:c0 + pg]
        for j in range(1, w):
            wsum = wsum + stp_ref[POOL_MAX - 1 - j][:, c0:c0 + pg]
        count = float(min(start_pos + 1, w))
        dlt = (wsum / count - u[:, c0:c0 + pg]).astype(BF16)
        yp = jnp.dot(dlt, pw_ref[g], preferred_element_type=F32) * ps_ref[:, c0:c0 + pg]
        parts.append(yp.astype(BF16))
    mix = jnp.concatenate(parts, axis=-1)
    y_ref[...] = x_ref[...] + jnp.dot(mix, wo_ref[...], preferred_element_type=F32)


def pool_step(o, u, x, st_p, pool_w, pool_scale, w_out, start_pos):
    bd, d = x.shape
    return pl.pallas_call(
        functools.partial(_pool_step_kernel, start_pos=start_pos),
        out_shape=jax.ShapeDtypeStruct((bd, d), F32),
        compiler_params=pltpu.CompilerParams(vmem_limit_bytes=VMEM_LIMIT),
        name="pool_step",
    )(o, u, x, st_p, pool_w, pool_scale.reshape(1, -1), w_out)


TM_PROMPT = 512
TM_FFN = 1024
TF = 256
TL_MIX = 256


def kernel(x_prompt, x_sample, state_conv_a, state_conv_b, cache_k, cache_v, state_pool, page_table, conv_norm_g, conv_w_in, conv_a_dw, conv_a_dw_b, conv_a_ln_g, conv_a_ln_b, conv_b_dw, conv_w_out, attn_norm_g, attn_w_in, pool_w, pool_scale, attn_w_out, ffn_norm_g, ffn_w_gate, ffn_w_up, ffn_w_down, final_norm_g):
    bsz, seq, d = x_prompt.shape
    bd, dec_seq, _ = x_sample.shape
    assert dec_seq == 1
    depth = ffn_norm_g.shape[0]
    n_heads = cache_k.shape[3]
    wattn = n_heads * HEAD_DIM
    past_len = page_table.shape[1] * PAGE_SIZE

    layer_bf16 = lambda w, n: w[n].astype(BF16)

    xp = x_prompt.reshape(bsz * seq, d)
    xs = x_sample.reshape(bd, d)
    pa, pb, pk, pv, pp = [], [], [], [], []
    sa, sb, sk, sv, sp = [], [], [], [], []
    for layer in range(depth):
        i = layer // 2
        if layer % 2 == 0:
            w_in, w_out = layer_bf16(conv_w_in, i), layer_bf16(conv_w_out, i)
            parts = norm_matmul(xp, conv_norm_g[i], w_in, TM_PROMPT)
            wa = parts[0].shape[-1]
            y, na, nb_ = conv_mix([t.reshape(bsz, seq, wa) for t in parts], xp.reshape(bsz, seq, d),
                                  conv_a_dw[i], conv_a_dw_b[i], conv_a_ln_g[i], conv_a_ln_b[i], conv_b_dw[i],
                                  w_out, TL_MIX)
            xp = y.reshape(bsz * seq, d)
            pa.append(na)
            pb.append(nb_)

            parts = norm_matmul(xs, conv_norm_g[i], w_in, TM_PROMPT)
            st_a = jnp.swapaxes(state_conv_a[i], 0, 1)
            st_b = jnp.swapaxes(state_conv_b[i], 0, 1)
            xs, a_new, c_new = conv_step(parts, xs, st_a, st_b, conv_a_dw[i], conv_a_dw_b[i], conv_a_ln_g[i],
                                         conv_a_ln_b[i], conv_b_dw[i], w_out)
            sa.append(jnp.concatenate([state_conv_a[i][:, 1:], a_new[:, None]], axis=1))
            sb.append(jnp.concatenate([state_conv_b[i][:, 1:], c_new[:, None]], axis=1))
        else:
            w_in, w_out, w_pool = layer_bf16(attn_w_in, i), layer_bf16(attn_w_out, i), layer_bf16(pool_w, i)
            q, k, v, u = norm_matmul(xp, attn_norm_g[i], w_in, TM_PROMPT)
            shp = (bsz, seq, wattn)
            o, block_sums = moba_prompt(q.reshape(shp), k.reshape(shp), v.reshape(shp), cache_k, i, page_table)
            y, npool = pool_out(o, u.reshape(bsz, seq, -1), xp.reshape(bsz, seq, d), w_pool, pool_scale[i],
                                w_out, TL_MIX)
            xp = y.reshape(bsz * seq, d)
            pk.append(k.reshape(bsz, seq, n_heads, HEAD_DIM))
            pv.append(v.reshape(bsz, seq, n_heads, HEAD_DIM))
            pp.append(npool)

            q, k, v, u = norm_matmul(xs, attn_norm_g[i], w_in, TM_PROMPT)
            heads = lambda t: t.reshape(bd, n_heads, HEAD_DIM)
            sel = gate_step(block_sums, heads(q))
            o = attn_step(cache_k, cache_v, i, page_table, sel, heads(q), heads(k), heads(v))
            st_p = jnp.swapaxes(state_pool[i], 0, 1)
            xs = pool_step(o.reshape(bd, wattn), u, xs, st_p, w_pool, pool_scale[i], w_out, past_len)
            sk.append(k.reshape(bd, 1, n_heads, HEAD_DIM))
            sv.append(v.reshape(bd, 1, n_heads, HEAD_DIM))
            sp.append(jnp.concatenate([state_pool[i][:, 1:], u[:, None]], axis=1))
        last = layer == depth - 1
        ffn_w = (ffn_w_gate, ffn_w_up, ffn_w_down, layer)
        xp = ffn(xp, ffn_norm_g[layer], *ffn_w, final_norm_g, TM_FFN, TF, last)
        xs = ffn(xs, ffn_norm_g[layer], *ffn_w, final_norm_g, TM_FFN, TF, last)
    return (xp.reshape(bsz, seq, d), xs.reshape(bd, 1, d), jnp.stack(pa), jnp.stack(sa), jnp.stack(pb),
            jnp.stack(sb), jnp.stack(pk), jnp.stack(sk), jnp.stack(pv), jnp.stack(sv), jnp.stack(pp), jnp.stack(sp))
```

```python
import functools

import jax
import jax.numpy as jnp
from jax import lax
from jax.experimental import pallas as pl
from jax.experimental.pallas import tpu as pltpu

EPS = 1e-6
PAGE_SIZE = 128
MOBA_BLOCK = 256
MOBA_TOPK = 3
HEAD_DIM = 128
POOL_WINDOWS = (2, 4, 8, 16)
POOL_MAX = max(POOL_WINDOWS)
MASKED = -1e30

V7X_VMEM_BYTES = 64 * 1024 * 1024
VMEM_LIMIT = V7X_VMEM_BYTES - 8 * 1024 * 1024
LANES = 128
SUBLANES = 8
COL_GROUP = 1024

BF16 = jnp.bfloat16
F32 = jnp.float32


def _params(*sem):
    return pltpu.CompilerParams(dimension_semantics=sem, vmem_limit_bytes=VMEM_LIMIT)


def _rms(x, g):
    ms = jnp.mean(x * x, axis=-1, keepdims=True)
    return x * lax.rsqrt(ms + EPS) * g


def _sigmoid(x):
    return 1.0 / (1.0 + jnp.exp(-x))


def _resident(shape):
    return pl.BlockSpec(shape, lambda *_: (0,) * len(shape), pipeline_mode=pl.Buffered(1))


def _norm_matmul_kernel(x_ref, g_ref, w_ref, *out_refs):
    h = _rms(x_ref[...], g_ref[...]).astype(BF16)
    for c, o_ref in enumerate(out_refs):
        o_ref[...] = jnp.dot(h, w_ref[:, c * COL_GROUP:(c + 1) * COL_GROUP], preferred_element_type=F32)


def norm_matmul(x, g, w, tm):
    m, d = x.shape
    n_out = w.shape[1] // COL_GROUP
    tm = min(tm, m)
    return pl.pallas_call(
        _norm_matmul_kernel,
        grid=(m // tm,),
        in_specs=[pl.BlockSpec((tm, d), lambda i: (i, 0)), _resident((1, d)), _resident(w.shape)],
        out_specs=[pl.BlockSpec((tm, COL_GROUP), lambda i: (i, 0))] * n_out,
        out_shape=[jax.ShapeDtypeStruct((m, COL_GROUP), F32)] * n_out,
        compiler_params=_params("parallel"),
        name="norm_matmul",
    )(x, g.reshape(1, d), w)


def _ffn_kernel(g_ref, fg_ref, xs_ref, wg_ref, wu_ref, wd_ref, x_hbm, o_ref, os_ref, xbuf, h_ref, sem, *, final_norm):
    i, f = pl.program_id(0), pl.program_id(1)
    tm = o_ref.shape[0]
    ns = xs_ref.shape[0]
    last_f = pl.num_programs(1) - 1

    def x_copy(tile):
        return pltpu.make_async_copy(x_hbm.at[pl.ds(pl.multiple_of(tile * tm, tm), tm), :], xbuf, sem)

    @pl.when((i == 0) & (f == 0))
    def _():
        x_copy(0).start()
        xs = xs_ref[...]
        h_ref[tm:, :] = jnp.zeros((h_ref.shape[0] - tm, h_ref.shape[1]), h_ref.dtype)
        h_ref[tm:tm + ns, :] = _rms(xs, g_ref[...]).astype(h_ref.dtype)
        os_ref[...] = xs

    @pl.when(f == 0)
    def _():
        x_copy(i).wait()
        x = xbuf[...]
        h_ref[0:tm, :] = _rms(x, g_ref[...]).astype(h_ref.dtype)
        o_ref[...] = x

    @pl.when((f == 1) & (i + 1 < pl.num_programs(0)))
    def _():
        x_copy(i + 1).start()

    h = h_ref[...]
    gate = jnp.dot(h, wg_ref[...].astype(BF16), preferred_element_type=F32)
    up = jnp.dot(h, wu_ref[...].astype(BF16), preferred_element_type=F32)
    act = (gate * _sigmoid(gate) * up).astype(BF16)
    down = jnp.dot(act, wd_ref[...].astype(BF16), preferred_element_type=F32)
    o_ref[...] += down[0:tm]

    @pl.when(i == 0)
    def _():
        os_ref[...] += down[tm:tm + ns]

    if final_norm:
        @pl.when(f == last_f)
        def _():
            o_ref[...] = _rms(o_ref[...], fg_ref[...])

        @pl.when((i == 0) & (f == last_f))
        def _():
            os_ref[...] = _rms(os_ref[...], fg_ref[...])


BF16_ROWS = 16


def ffn(x, xs, g, wg, wu, wd, layer, final_g, tm, tf, final_norm):
    m, d = x.shape
    ns = xs.shape[0]
    dff = wg.shape[2]
    tm = min(tm, m)
    assert m % tm == 0 and dff % tf == 0 and dff // tf >= 2 and ns <= BF16_ROWS and tm % BF16_ROWS == 0
    return pl.pallas_call(
        functools.partial(_ffn_kernel, final_norm=final_norm),
        grid=(m // tm, dff // tf),
        in_specs=[_resident((1, d)), _resident((1, d)), _resident((ns, d)),
                  pl.BlockSpec((None, d, tf), lambda i, f: (layer, 0, f)),
                  pl.BlockSpec((None, d, tf), lambda i, f: (layer, 0, f)),
                  pl.BlockSpec((None, tf, d), lambda i, f: (layer, f, 0)),
                  pl.BlockSpec(memory_space=pl.ANY)],
        out_specs=[pl.BlockSpec((tm, d), lambda i, f: (i, 0)), pl.BlockSpec((ns, d), lambda i, f: (0, 0))],
        out_shape=[jax.ShapeDtypeStruct((m, d), F32), jax.ShapeDtypeStruct((ns, d), F32)],
        scratch_shapes=[pltpu.VMEM((tm, d), F32), pltpu.VMEM((tm + BF16_ROWS, d), BF16),
                        pltpu.SemaphoreType.DMA(())],
        compiler_params=_params("arbitrary", "arbitrary"),
        name="ffn",
    )(g.reshape(1, d), final_g.reshape(1, d), xs, wg, wu, wd, x)


A_HALO = 32
B_HALO = 8


def _layer_norm_silu(x, g, b):
    mu = jnp.mean(x, axis=-1, keepdims=True)
    xc = x - mu
    y = xc * lax.rsqrt(jnp.mean(xc * xc, axis=-1, keepdims=True) + EPS) * g + b
    return y * _sigmoid(y)


def _conv_mix_kernel(av_ref, ag_ref, bh_ref, bb_ref, bc_ref, x_ref, adw_ref, adwb_ref, lng_ref, lnb_ref,
                     bdw_ref, wo_ref, y_ref, newa_ref, newb_ref, aext, cext, ashift, aconv, mix, *, ta, tb, rc):
    l = pl.program_id(1)
    tl = av_ref.shape[1]
    wa = av_ref.shape[2]

    @pl.when(l == 0)
    def _():
        aext[0:A_HALO, :] = jnp.zeros((A_HALO, wa), F32)
        cext[0:B_HALO, :] = jnp.zeros((B_HALO, wa), F32)

    @pl.when(l > 0)
    def _():
        aext[0:A_HALO, :] = aext[tl:tl + A_HALO, :]
        cext[0:B_HALO, :] = cext[tl:tl + B_HALO, :]

    aext[A_HALO:A_HALO + tl, :] = av_ref[0] * _sigmoid(ag_ref[0])
    cext[B_HALO:B_HALO + tl, :] = bc_ref[0] * bh_ref[0]

    n_shift = ashift.shape[1]
    for s in range(1, SUBLANES):
        ashift[s - 1] = aext[s:s + n_shift, :]

    for r0 in range(0, tl, rc):
        for c0 in range(0, wa, LANES):
            acc = jnp.broadcast_to(adwb_ref[:, c0:c0 + LANES], (rc, LANES))
            for k in range(ta):
                off = A_HALO - (ta - 1) + r0 + k
                s = off % SUBLANES
                src = aext if s == 0 else ashift.at[s - 1]
                acc = acc + adw_ref[k:k + 1, c0:c0 + LANES] * src[off - s:off - s + rc, c0:c0 + LANES]
            aconv[r0:r0 + rc, c0:c0 + LANES] = acc
    mix[:, 0:wa] = _layer_norm_silu(aconv[...], lng_ref[...], lnb_ref[...]).astype(BF16)

    bconv = jnp.zeros((tl, wa), F32)
    for k in range(tb):
        off = B_HALO - (tb - 1) + k
        bconv = bconv + bdw_ref[k:k + 1, :] * cext[off:off + tl, :]
    mix[:, wa:] = (bb_ref[0] * bconv).astype(BF16)

    y_ref[0] = x_ref[0] + jnp.dot(mix[...], wo_ref[...], preferred_element_type=F32)

    @pl.when(l == pl.num_programs(1) - 1)
    def _():
        newa_ref[0] = aext[A_HALO + tl - (ta - 1):A_HALO + tl, :]
        newb_ref[0] = cext[B_HALO + tl - (tb - 1):B_HALO + tl, :]


def conv_mix(parts, x, a_dw, a_dw_b, ln_g, ln_b, b_dw, w_out, tl):
    bsz, seq, d = x.shape
    wa = parts[0].shape[-1]
    ta, tb = a_dw.shape[0], b_dw.shape[0]
    tl = min(tl, seq)
    assert seq % tl == 0 and tl >= A_HALO and ta - 1 <= A_HALO and tb - 1 <= B_HALO
    part_spec = pl.BlockSpec((1, tl, wa), lambda b, l: (b, l, 0))
    const = _resident
    return pl.pallas_call(
        functools.partial(_conv_mix_kernel, ta=ta, tb=tb, rc=min(128, tl)),
        grid=(bsz, seq // tl),
        in_specs=[part_spec] * 5 + [pl.BlockSpec((1, tl, d), lambda b, l: (b, l, 0)),
                                    const((ta, wa)), const((1, wa)), const((1, wa)), const((1, wa)),
                                    const((tb, wa)), const(w_out.shape)],
        out_specs=[pl.BlockSpec((1, tl, d), lambda b, l: (b, l, 0)),
                   pl.BlockSpec((1, ta - 1, wa), lambda b, l: (b, 0, 0)),
                   pl.BlockSpec((1, tb - 1, wa), lambda b, l: (b, 0, 0))],
        out_shape=[jax.ShapeDtypeStruct((bsz, seq, d), F32),
                   jax.ShapeDtypeStruct((bsz, ta - 1, wa), F32),
                   jax.ShapeDtypeStruct((bsz, tb - 1, wa), F32)],
        scratch_shapes=[pltpu.VMEM((A_HALO + tl, wa), F32), pltpu.VMEM((B_HALO + tl, wa), F32),
                        pltpu.VMEM((SUBLANES - 1, A_HALO + tl - SUBLANES, wa), F32),
                        pltpu.VMEM((tl, wa), F32), pltpu.VMEM((tl, 2 * wa), BF16)],
        compiler_params=_params("parallel", "arbitrary"),
        name="conv_mix",
    )(*parts, x, a_dw, a_dw_b.reshape(1, wa), ln_g.reshape(1, wa), ln_b.reshape(1, wa), b_dw, w_out)


def _conv_step_kernel(av_ref, ag_ref, bh_ref, bb_ref, bc_ref, x_ref, sta_ref, stb_ref, adw_ref, adwb_ref,
                      lng_ref, lnb_ref, bdw_ref, wo_ref, y_ref, anew_ref, cnew_ref, *, ta, tb):
    a = av_ref[...] * _sigmoid(ag_ref[...])
    c = bc_ref[...] * bh_ref[...]
    anew_ref[...] = a
    cnew_ref[...] = c
    acc = adwb_ref[...] + adw_ref[ta - 1:ta, :] * a
    for k in range(ta - 1):
        acc = acc + adw_ref[k:k + 1, :] * sta_ref[k]
    a_out = _layer_norm_silu(acc, lng_ref[...], lnb_ref[...])
    bconv = bdw_ref[tb - 1:tb, :] * c
    for k in range(tb - 1):
        bconv = bconv + bdw_ref[k:k + 1, :] * stb_ref[k]
    mix = jnp.concatenate([a_out, bb_ref[...] * bconv], axis=-1).astype(BF16)
    y_ref[...] = x_ref[...] + jnp.dot(mix, wo_ref[...], preferred_element_type=F32)


def conv_step(parts, x, st_a, st_b, a_dw, a_dw_b, ln_g, ln_b, b_dw, w_out):
    bd, d = x.shape
    wa = parts[0].shape[-1]
    ta, tb = a_dw.shape[0], b_dw.shape[0]
    return pl.pallas_call(
        functools.partial(_conv_step_kernel, ta=ta, tb=tb),
        out_shape=[jax.ShapeDtypeStruct((bd, d), F32), jax.ShapeDtypeStruct((bd, wa), F32),
                   jax.ShapeDtypeStruct((bd, wa), F32)],
        compiler_params=pltpu.CompilerParams(vmem_limit_bytes=VMEM_LIMIT),
        name="conv_step",
    )(*parts, x, st_a, st_b, a_dw, a_dw_b.reshape(1, wa), ln_g.reshape(1, wa), ln_b.reshape(1, wa), b_dw, w_out)


LOG2E = 1.4426950408889634
TQ = 2 * MOBA_BLOCK
DA = 2 * HEAD_DIM


def _split3(x):
    hi = x.astype(BF16).astype(F32)
    mid = (x - hi).astype(BF16).astype(F32)
    lo = (x - hi - mid).astype(BF16).astype(F32)
    return hi, mid, lo


def _moba_setup(q_ref, k_ref, v_ref, qaug, kaug, vaug, kmean, slope2, nb):
    seq = k_ref.shape[1]
    bs = MOBA_BLOCK
    nbp = kmean.shape[0]
    k = k_ref[0]
    q = q_ref[0]
    kmean[...] = jnp.zeros_like(kmean)
    kmean[0:nb, :] = jnp.mean(k.reshape(nb, bs, HEAD_DIM), axis=1)

    gate = lax.dot_general(kmean[...], q, (((1,), (1,)), ((), ())), precision=lax.Precision.HIGHEST,
                           preferred_element_type=F32)
    blk = lax.broadcasted_iota(jnp.int32, (nbp, seq), 0)
    own = lax.broadcasted_iota(jnp.int32, (nbp, seq), 1) // bs
    gate = jnp.where(blk < own, gate, -jnp.inf)
    attend = blk == own
    for _ in range(MOBA_TOPK):
        m = jnp.max(gate, axis=0, keepdims=True)
        idx = jnp.min(jnp.where(gate == m, blk, nbp), axis=0, keepdims=True)
        hit = blk == idx
        attend = attend | (hit & (blk < own))
        gate = jnp.where(hit, -jnp.inf, gate)
    attend_t = jnp.where(attend, 1.0, 0.0)
    attend_r = jnp.concatenate([attend_t, jnp.zeros((LANES - nbp, seq), F32)], axis=0).T

    lane = lax.broadcasted_iota(jnp.int32, (seq, LANES), 1)
    pos_i = lax.broadcasted_iota(jnp.int32, (seq, LANES), 0)
    pos = pos_i.astype(F32)

    khi, kmid, klo = _split3(slope2 * pos)
    ek = jnp.where(lane == pos_i // bs, 1.0, 0.0)
    ek = jnp.where(lane == nb, khi, ek)
    ek = jnp.where(lane == nb + 1, kmid, ek)
    ek = jnp.where(lane == nb + 2, klo, ek)
    ek = jnp.where((lane >= nb + 3) & (lane < nb + 6), 1.0, ek)
    kaug[:, 0:HEAD_DIM] = k.astype(BF16)
    kaug[:, HEAD_DIM:] = ek.astype(BF16)

    qhi, qmid, qlo = _split3(-slope2 * pos)
    eq = jnp.where(lane < nb, jnp.where(attend_r > 0.5, 0.0, MASKED), 0.0)
    eq = jnp.where((lane >= nb) & (lane < nb + 3), 1.0, eq)
    eq = jnp.where(lane == nb + 3, qhi, eq)
    eq = jnp.where(lane == nb + 4, qmid, eq)
    eq = jnp.where(lane == nb + 5, qlo, eq)
    qaug[:, 0:HEAD_DIM] = (q * (HEAD_DIM ** -0.5 * LOG2E)).astype(BF16)
    qaug[:, HEAD_DIM:] = eq.astype(BF16)

    vaug[:, 0:HEAD_DIM] = v_ref[0].astype(BF16)
    vaug[:, HEAD_DIM:] = jnp.ones((seq, HEAD_DIM), BF16)


SUM_CHAINS = 4


def _page_sums_step(step, pt_ref, kc_hbm, ksum_ref, ring, sem, *, layer, pages_per_block):
    total = pt_ref.shape[0]
    depth, page_rows, n_heads, _ = ring.shape

    def page_copy(t, slot):
        return pltpu.make_async_copy(kc_hbm.at[layer, pt_ref[t]], ring.at[slot], sem.at[slot])

    @pl.when(step == 0)
    def _():
        ksum_ref[...] = jnp.zeros_like(ksum_ref)
        for slot in range(min(depth, total)):
            page_copy(slot, slot).start()

    for slot in range(depth):
        t = step * depth + slot

        @pl.when(t < total)
        def _(t=t, slot=slot):
            page_copy(t, slot).wait()
            page = ring[slot].reshape(SUM_CHAINS, page_rows // SUM_CHAINS, n_heads, HEAD_DIM)
            ksum_ref[t // pages_per_block] += jnp.sum(jnp.sum(page, axis=1), axis=0)

            @pl.when(t + depth < total)
            def _():
                page_copy(t + depth, slot).start()


def _moba_kernel(pt_ref, q_ref, k_ref, v_ref, kc_hbm, o_ref, ksum_ref, qaug, kaug, vaug, kmean, m_ref, acc_ref,
                 s_ref, ring, psem, *, n_heads, nb, layer):
    h = pl.program_id(1)
    i = pl.program_id(2)
    bs = MOBA_BLOCK
    nt = (((1,), (1,)), ((), ()))
    step = (pl.program_id(0) * pl.num_programs(1) + h) * pl.num_programs(2) + i
    _page_sums_step(step, pt_ref, kc_hbm, ksum_ref, ring, psem, layer=layer,
                    pages_per_block=MOBA_BLOCK // PAGE_SIZE)

    @pl.when(i == 0)
    def _():
        slope2 = jnp.exp2((h + 1).astype(F32) * (-8.0 / n_heads) + jnp.zeros((1, 1), F32)) * LOG2E
        _moba_setup(q_ref, k_ref, v_ref, qaug, kaug, vaug, kmean, slope2, nb)

    r0 = pl.multiple_of(i * TQ, TQ)
    qa = qaug[pl.ds(r0, TQ), :]

    halves = (slice(0, bs), slice(bs, TQ))

    def scores(g):
        kg = kaug[pl.ds(pl.multiple_of(g * TQ, TQ), TQ), :]
        return jnp.concatenate([lax.dot_general(qa[r], kg, nt, preferred_element_type=F32) for r in halves], axis=0)

    def update(s, g):
        vg = vaug[pl.ds(pl.multiple_of(g * TQ, TQ), TQ), :]
        for r in halves:
            m = m_ref[r, :]
            m_new = jnp.maximum(m, jnp.max(s[r], axis=1, keepdims=True))
            p = jnp.exp2(s[r] - m_new).astype(BF16)
            pv = jnp.dot(p, vg, preferred_element_type=F32)
            acc_ref[r, :] = jnp.exp2(m - m_new) * acc_ref[r, :] + pv
            m_ref[r, :] = m_new

    m_ref[...] = jnp.full(m_ref.shape, MASKED, F32)
    acc_ref[...] = jnp.zeros_like(acc_ref)
    s_ref[0] = scores(0)

    def past_group(g, slot):
        s_ref[1 - slot] = scores(g + 1)
        update(s_ref[slot], g)

    @pl.loop(0, i // 2)
    def _(j):
        past_group(2 * j, 0)
        past_group(2 * j + 1, 1)

    def own_group(slot):
        causal = (lax.broadcasted_iota(jnp.int32, (TQ, TQ), 0) >= lax.broadcasted_iota(jnp.int32, (TQ, TQ), 1))
        update(jnp.where(causal, s_ref[slot], MASKED), i)
        o_ref[0] = acc_ref[:, 0:HEAD_DIM] / acc_ref[:, HEAD_DIM:]

    @pl.when(i % 2 == 0)
    def _():
        own_group(0)

    @pl.when(i % 2 == 1)
    def _():
        past_group(i - 1, 0)
        own_group(1)


def moba_prompt(q, k, v, cache_k, layer, page_table):
    bsz, seq, width = q.shape
    n_heads = width // HEAD_DIM
    assert seq % TQ == 0
    nb = seq // MOBA_BLOCK
    nbp = -(-nb // 8) * 8
    assert nb + 6 <= LANES
    bd, n_pages = page_table.shape
    pages_per_block = MOBA_BLOCK // PAGE_SIZE
    assert n_pages % pages_per_block == 0
    grid = (bsz, n_heads, seq // TQ)
    pages_per_step = -(-bd * n_pages // (grid[0] * grid[1] * grid[2]))
    sums_shape = (bd * n_pages // pages_per_block, n_heads, HEAD_DIM)
    full = pl.BlockSpec((1, seq, HEAD_DIM), lambda b, h, i: (b, 0, h))
    tile = pl.BlockSpec((1, TQ, HEAD_DIM), lambda b, h, i: (b, i, h))
    return pl.pallas_call(
        functools.partial(_moba_kernel, n_heads=n_heads, nb=nb, layer=layer),
        grid=grid,
        in_specs=[pl.BlockSpec(memory_space=pltpu.SMEM), full, full, full, pl.BlockSpec(memory_space=pl.ANY)],
        out_specs=[tile, pl.BlockSpec(sums_shape, lambda b, h, i: (0, 0, 0))],
        out_shape=[jax.ShapeDtypeStruct((bsz, seq, width), F32), jax.ShapeDtypeStruct(sums_shape, F32)],
        scratch_shapes=[pltpu.VMEM((seq, DA), BF16), pltpu.VMEM((seq, DA), BF16), pltpu.VMEM((seq, DA), BF16),
                        pltpu.VMEM((nbp, HEAD_DIM), F32), pltpu.VMEM((TQ, 1), F32), pltpu.VMEM((TQ, DA), F32),
                        pltpu.VMEM((2, TQ, TQ), F32),
                        pltpu.VMEM((pages_per_step, PAGE_SIZE, n_heads, HEAD_DIM), F32),
                        pltpu.SemaphoreType.DMA((pages_per_step,))],
        compiler_params=_params("arbitrary", "arbitrary", "arbitrary"),
        name="moba_prompt",
    )(page_table.reshape(-1), q, k, v, cache_k)


PAGES_IN_FLIGHT = 8


def _gate_step_kernel(q_ref, ksum_ref, sel_ref, *, n_heads):
    bd = q_ref.shape[0]
    nblk = ksum_ref.shape[0] // bd
    sel_ref[...] = jnp.zeros_like(sel_ref)
    for b in range(bd):
        kmean = ksum_ref[b * nblk:(b + 1) * nblk] * (1.0 / MOBA_BLOCK)
        gate = jnp.sum(kmean * q_ref[b][None], axis=-1)
        blk = lax.broadcasted_iota(jnp.int32, gate.shape, 0)
        for r in range(MOBA_TOPK):
            m = jnp.max(gate, axis=0, keepdims=True)
            idx = jnp.min(jnp.where(gate == m, blk, nblk), axis=0, keepdims=True)
            sel_ref[b, r:r + 1, 0:n_heads] = idx
            gate = jnp.where(blk == idx, -jnp.inf, gate)


SEL_ROWS = 8


def gate_step(block_sums, q):
    bd, n_heads, _ = q.shape
    assert MOBA_TOPK <= SEL_ROWS and n_heads <= LANES
    sel = pl.pallas_call(
        functools.partial(_gate_step_kernel, n_heads=n_heads),
        out_shape=jax.ShapeDtypeStruct((bd, SEL_ROWS, LANES), jnp.int32),
        compiler_params=pltpu.CompilerParams(vmem_limit_bytes=VMEM_LIMIT),
        name="gate_step",
    )(q, block_sums)
    return sel[:, :MOBA_TOPK, :n_heads]


def _attn_step_kernel(pt_ref, sel_ref, q_ref, kn_ref, vn_ref, k_hbm, v_hbm, o_ref, kbuf, vbuf, ksem, vsem,
                      m_ref, l_ref, acc_ref, *, layer, n_pages, q_pos, pages_per_block):
    bd, n_heads, _ = q_ref.shape
    per_head = MOBA_TOPK * pages_per_block
    total = bd * n_heads * per_head
    depth = kbuf.shape[0]
    scale = HEAD_DIM ** -0.5
    own = q_pos // MOBA_BLOCK

    def coords(t):
        bh, j = t // per_head, t % per_head
        b, h = bh // n_heads, bh % n_heads
        return b, h, j, sel_ref[(b * MOBA_TOPK + j // pages_per_block) * n_heads + h]

    def page_copies(t, slot):
        b, _, j, n = coords(t)
        page = pt_ref[b * n_pages + n * pages_per_block + j % pages_per_block]
        return (pltpu.make_async_copy(k_hbm.at[layer, page], kbuf.at[slot], ksem.at[slot]),
                pltpu.make_async_copy(v_hbm.at[layer, page], vbuf.at[slot], vsem.at[slot]))

    for t in range(min(depth, total)):
        for c in page_copies(t, t):
            c.start()

    @pl.loop(0, total)
    def _(t):
        slot = t % depth
        b, h, j, n = coords(t)
        for c in page_copies(t, slot):
            c.wait()
        q = q_ref[b, pl.ds(h, 1), :]

        @pl.when(j == 0)
        def _():
            m_ref[...] = jnp.sum(q * kn_ref[b, pl.ds(h, 1), :], axis=1, keepdims=True) * scale
            l_ref[...] = jnp.ones_like(l_ref)
            acc_ref[...] = vn_ref[b, pl.ds(h, 1), :]

        k = kbuf[slot, pl.ds(h, PAGE_SIZE, stride=n_heads), :]
        v = vbuf[slot, pl.ds(h, PAGE_SIZE, stride=n_heads), :]
        slope = jnp.exp2(jnp.asarray(h + 1, F32) * (-8.0 / n_heads) + jnp.zeros((1, 1), F32))
        kpos = (n * MOBA_BLOCK + (j % pages_per_block) * PAGE_SIZE
                + lax.broadcasted_iota(jnp.int32, (PAGE_SIZE, 1), 0))
        dist = (q_pos - kpos).astype(F32)
        s = jnp.sum(k * q, axis=1, keepdims=True) * scale - slope * dist
        s = jnp.where((dist >= 0.0) & (n < own), s, MASKED)
        m = m_ref[...]
        m_new = jnp.maximum(m, jnp.max(s, axis=0, keepdims=True))
        alpha = jnp.exp(m - m_new)
        p = jnp.exp(s - m_new)
        l_ref[...] = alpha * l_ref[...] + jnp.sum(p, axis=0, keepdims=True)
        acc_ref[...] = alpha * acc_ref[...] + jnp.sum(p * v, axis=0, keepdims=True)
        m_ref[...] = m_new

        @pl.when(j == per_head - 1)
        def _():
            o_ref[b, pl.ds(h, 1), :] = acc_ref[...] / l_ref[...]

        @pl.when(t + depth < total)
        def _():
            for c in page_copies(t + depth, slot):
                c.start()


def attn_step(cache_k, cache_v, layer, page_table, sel, q, k_new, v_new):
    bd, n_pages = page_table.shape
    n_heads = q.shape[1]
    pages_per_block = MOBA_BLOCK // PAGE_SIZE
    assert n_pages % pages_per_block == 0
    rows = PAGE_SIZE * n_heads
    as_rows = lambda c: c.reshape(c.shape[0], c.shape[1], rows, HEAD_DIM)
    vmem, smem = pl.BlockSpec(memory_space=pltpu.VMEM), pl.BlockSpec(memory_space=pltpu.SMEM)
    hbm = pl.BlockSpec(memory_space=pl.ANY)
    return pl.pallas_call(
        functools.partial(_attn_step_kernel, layer=layer, n_pages=n_pages, q_pos=n_pages * PAGE_SIZE,
                          pages_per_block=pages_per_block),
        in_specs=[smem, smem, vmem, vmem, vmem, hbm, hbm],
        out_specs=vmem,
        out_shape=jax.ShapeDtypeStruct((bd, n_heads, HEAD_DIM), F32),
        scratch_shapes=[pltpu.VMEM((PAGES_IN_FLIGHT, rows, HEAD_DIM), F32),
                        pltpu.VMEM((PAGES_IN_FLIGHT, rows, HEAD_DIM), F32),
                        pltpu.SemaphoreType.DMA((PAGES_IN_FLIGHT,)), pltpu.SemaphoreType.DMA((PAGES_IN_FLIGHT,)),
                        pltpu.VMEM((1, 1), F32), pltpu.VMEM((1, 1), F32), pltpu.VMEM((1, HEAD_DIM), F32)],
        compiler_params=pltpu.CompilerParams(vmem_limit_bytes=VMEM_LIMIT),
        name="attn_step",
    )(page_table.reshape(-1), sel.reshape(-1), q, k_new, v_new, as_rows(cache_k), as_rows(cache_v))


P_HALO = 16


def _pool_out_kernel(o_ref, u_ref, x_ref, pw_ref, ps_ref, wo_ref, y_ref, newp_ref, uext, mix):
    l = pl.program_id(1)
    tl = u_ref.shape[1]
    wp = u_ref.shape[2]
    wo_attn = o_ref.shape[2]
    pg = wp // len(POOL_WINDOWS)

    @pl.when(l == 0)
    def _():
        uext[0:P_HALO, :] = jnp.zeros((P_HALO, wp), F32)

    @pl.when(l > 0)
    def _():
        uext[0:P_HALO, :] = uext[tl:tl + P_HALO, :]

    uext[P_HALO:P_HALO + tl, :] = u_ref[0]
    mix[:, 0:wo_attn] = o_ref[0].astype(BF16)
    pos = l * tl + lax.broadcasted_iota(jnp.int32, (tl, pg), 0)
    for g, w in enumerate(POOL_WINDOWS):
        c0 = g * pg
        wsum = uext[P_HALO:P_HALO + tl, c0:c0 + pg]
        for j in range(1, w):
            wsum = wsum + uext[P_HALO - j:P_HALO - j + tl, c0:c0 + pg]
        count = jnp.minimum(pos + 1, w).astype(F32)
        dlt = (wsum / count - uext[P_HALO:P_HALO + tl, c0:c0 + pg]).astype(BF16)
        yp = jnp.dot(dlt, pw_ref[g], preferred_element_type=F32) * ps_ref[:, c0:c0 + pg]
        mix[:, wo_attn + c0:wo_attn + c0 + pg] = yp.astype(BF16)

    y_ref[0] = x_ref[0] + jnp.dot(mix[...], wo_ref[...], preferred_element_type=F32)

    @pl.when(l == pl.num_programs(1) - 1)
    def _():
        newp_ref[0] = uext[P_HALO + tl - (POOL_MAX - 1):P_HALO + tl, :]


def pool_out(o, u, x, pool_w, pool_scale, w_out, tl):
    bsz, seq, d = x.shape
    wp = u.shape[-1]
    tl = min(tl, seq)
    assert seq % tl == 0 and tl >= P_HALO
    const = _resident
    return pl.pallas_call(
        _pool_out_kernel,
        grid=(bsz, seq // tl),
        in_specs=[pl.BlockSpec((1, tl, o.shape[-1]), lambda b, l: (b, l, 0)),
                  pl.BlockSpec((1, tl, wp), lambda b, l: (b, l, 0)),
                  pl.BlockSpec((1, tl, d), lambda b, l: (b, l, 0)),
                  const(pool_w.shape), const((1, wp)), const(w_out.shape)],
        out_specs=[pl.BlockSpec((1, tl, d), lambda b, l: (b, l, 0)),
                   pl.BlockSpec((1, POOL_MAX - 1, wp), lambda b, l: (b, 0, 0))],
        out_shape=[jax.ShapeDtypeStruct((bsz, seq, d), F32),
                   jax.ShapeDtypeStruct((bsz, POOL_MAX - 1, wp), F32)],
        scratch_shapes=[pltpu.VMEM((P_HALO + tl, wp), F32), pltpu.VMEM((tl, o.shape[-1] + wp), BF16)],
        compiler_params=_params("parallel", "arbitrary"),
        name="pool_out",
    )(o, u, x, pool_w, pool_scale.reshape(1, wp), w_out)


def _pool_step_kernel(o_ref, u_ref, x_ref, stp_ref, pw_ref, ps_ref, wo_ref, y_ref, *, start_pos):
    wp = u_ref.shape[1]
    pg = wp // len(POOL_WINDOWS)
    u = u_ref[...]
    parts = [o_ref[...].astype(BF16)]
    for g, w in enumerate(POOL_WINDOWS):
        c0 = g * pg
        wsum = u[:, c0:c0 + pg]
        for j in range(1, w):
            wsum = wsum + stp_ref[POOL_MAX - 1 - j][:, c0:c0 + pg]
        count = float(min(start_pos + 1, w))
        dlt = (wsum / count - u[:, c0:c0 + pg]).astype(BF16)
        yp = jnp.dot(dlt, pw_ref[g], preferred_element_type=F32) * ps_ref[:, c0:c0 + pg]
        parts.append(yp.astype(BF16))
    mix = jnp.concatenate(parts, axis=-1)
    y_ref[...] = x_ref[...] + jnp.dot(mix, wo_ref[...], preferred_element_type=F32)


def pool_step(o, u, x, st_p, pool_w, pool_scale, w_out, start_pos):
    bd, d = x.shape
    return pl.pallas_call(
        functools.partial(_pool_step_kernel, start_pos=start_pos),
        out_shape=jax.ShapeDtypeStruct((bd, d), F32),
        compiler_params=pltpu.CompilerParams(vmem_limit_bytes=VMEM_LIMIT),
        name="pool_step",
    )(o, u, x, st_p, pool_w, pool_scale.reshape(1, -1), w_out)


TM_PROMPT = 512
TM_FFN = 1024
TF = 256
TL_MIX = 256


def kernel(x_prompt, x_sample, state_conv_a, state_conv_b, cache_k, cache_v, state_pool, page_table, conv_norm_g, conv_w_in, conv_a_dw, conv_a_dw_b, conv_a_ln_g, conv_a_ln_b, conv_b_dw, conv_w_out, attn_norm_g, attn_w_in, pool_w, pool_scale, attn_w_out, ffn_norm_g, ffn_w_gate, ffn_w_up, ffn_w_down, final_norm_g):
    bsz, seq, d = x_prompt.shape
    bd, dec_seq, _ = x_sample.shape
    assert dec_seq == 1
    depth = ffn_norm_g.shape[0]
    n_heads = cache_k.shape[3]
    wattn = n_heads * HEAD_DIM
    past_len = page_table.shape[1] * PAGE_SIZE

    layer_bf16 = lambda w, n: w[n].astype(BF16)

    xp = x_prompt.reshape(bsz * seq, d)
    xs = x_sample.reshape(bd, d)
    pa, pb, pk, pv, pp = [], [], [], [], []
    sa, sb, sk, sv, sp = [], [], [], [], []
    for layer in range(depth):
        i = layer // 2
        if layer % 2 == 0:
            w_in, w_out = layer_bf16(conv_w_in, i), layer_bf16(conv_w_out, i)
            parts = norm_matmul(xp, conv_norm_g[i], w_in, TM_PROMPT)
            wa = parts[0].shape[-1]
            y, na, nb_ = conv_mix([t.reshape(bsz, seq, wa) for t in parts], xp.reshape(bsz, seq, d),
                                  conv_a_dw[i], conv_a_dw_b[i], conv_a_ln_g[i], conv_a_ln_b[i], conv_b_dw[i],
                                  w_out, TL_MIX)
            xp = y.reshape(bsz * seq, d)
            pa.append(na)
            pb.append(nb_)

            parts = norm_matmul(xs, conv_norm_g[i], w_in, TM_PROMPT)
            st_a = jnp.swapaxes(state_conv_a[i], 0, 1)
            st_b = jnp.swapaxes(state_conv_b[i], 0, 1)
            xs, a_new, c_new = conv_step(parts, xs, st_a, st_b, conv_a_dw[i], conv_a_dw_b[i], conv_a_ln_g[i],
                                         conv_a_ln_b[i], conv_b_dw[i], w_out)
            sa.append(jnp.concatenate([state_conv_a[i][:, 1:], a_new[:, None]], axis=1))
            sb.append(jnp.concatenate([state_conv_b[i][:, 1:], c_new[:, None]], axis=1))
        else:
            w_in, w_out, w_pool = layer_bf16(attn_w_in, i), layer_bf16(attn_w_out, i), layer_bf16(pool_w, i)
            q, k, v, u = norm_matmul(xp, attn_norm_g[i], w_in, TM_PROMPT)
            shp = (bsz, seq, wattn)
            o, block_sums = moba_prompt(q.reshape(shp), k.reshape(shp), v.reshape(shp), cache_k, i, page_table)
            y, npool = pool_out(o, u.reshape(bsz, seq, -1), xp.reshape(bsz, seq, d), w_pool, pool_scale[i],
                                w_out, TL_MIX)
            xp = y.reshape(bsz * seq, d)
            pk.append(k.reshape(bsz, seq, n_heads, HEAD_DIM))
            pv.append(v.reshape(bsz, seq, n_heads, HEAD_DIM))
            pp.append(npool)

            q, k, v, u = norm_matmul(xs, attn_norm_g[i], w_in, TM_PROMPT)
            heads = lambda t: t.reshape(bd, n_heads, HEAD_DIM)
            sel = gate_step(block_sums, heads(q))
            o = attn_step(cache_k, cache_v, i, page_table, sel, heads(q), heads(k), heads(v))
            st_p = jnp.swapaxes(state_pool[i], 0, 1)
            xs = pool_step(o.reshape(bd, wattn), u, xs, st_p, w_pool, pool_scale[i], w_out, past_len)
            sk.append(k.reshape(bd, 1, n_heads, HEAD_DIM))
            sv.append(v.reshape(bd, 1, n_heads, HEAD_DIM))
            sp.append(jnp.concatenate([state_pool[i][:, 1:], u[:, None]], axis=1))
        last = layer == depth - 1
        ffn_w = (ffn_w_gate, ffn_w_up, ffn_w_down, layer)
        xp, xs = ffn(xp, xs, ffn_norm_g[layer], *ffn_w, final_norm_g, TM_FFN, TF, last)
    return (xp.reshape(bsz, seq, d), xs.reshape(bd, 1, d), jnp.stack(pa), jnp.stack(sa), jnp.stack(pb),
            jnp.stack(sb), jnp.stack(pk), jnp.stack(sk), jnp.stack(pv), jnp.stack(sv), jnp.stack(pp), jnp.stack(sp))
```

```python
import functools

import jax
import jax.numpy as jnp
from jax import lax
from jax.experimental import pallas as pl
from jax.experimental.pallas import tpu as pltpu

EPS = 1e-6
PAGE_SIZE = 128
MOBA_BLOCK = 256
MOBA_TOPK = 3
HEAD_DIM = 128
POOL_WINDOWS = (2, 4, 8, 16)
POOL_MAX = max(POOL_WINDOWS)
MASKED = -1e30

V7X_VMEM_BYTES = 64 * 1024 * 1024
VMEM_LIMIT = V7X_VMEM_BYTES - 8 * 1024 * 1024
LANES = 128
SUBLANES = 8
COL_GROUP = 1024

BF16 = jnp.bfloat16
F32 = jnp.float32


def _params(*sem):
    return pltpu.CompilerParams(dimension_semantics=sem, vmem_limit_bytes=VMEM_LIMIT)


def _rms(x, g):
    ms = jnp.mean(x * x, axis=-1, keepdims=True)
    return x * lax.rsqrt(ms + EPS) * g


def _sigmoid(x):
    return 1.0 / (1.0 + jnp.exp(-x))


def _resident(shape):
    return pl.BlockSpec(shape, lambda *_: (0,) * len(shape), pipeline_mode=pl.Buffered(1))


def _norm_matmul_kernel(x_ref, g_ref, w_ref, *out_refs):
    h = _rms(x_ref[...], g_ref[...]).astype(BF16)
    for c, o_ref in enumerate(out_refs):
        o_ref[...] = jnp.dot(h, w_ref[:, c * COL_GROUP:(c + 1) * COL_GROUP], preferred_element_type=F32)


def norm_matmul(x, g, w, tm):
    m, d = x.shape
    n_out = w.shape[1] // COL_GROUP
    tm = min(tm, m)
    return pl.pallas_call(
        _norm_matmul_kernel,
        grid=(m // tm,),
        in_specs=[pl.BlockSpec((tm, d), lambda i: (i, 0)), _resident((1, d)), _resident(w.shape)],
        out_specs=[pl.BlockSpec((tm, COL_GROUP), lambda i: (i, 0))] * n_out,
        out_shape=[jax.ShapeDtypeStruct((m, COL_GROUP), F32)] * n_out,
        compiler_params=_params("parallel"),
        name="norm_matmul",
    )(x, g.reshape(1, d), w)


def _ffn_kernel(g_ref, fg_ref, xs_ref, wg_ref, wu_ref, wd_ref, x_hbm, o_ref, os_ref, xbuf, h_ref, sem, *, final_norm):
    i, f = pl.program_id(0), pl.program_id(1)
    tm = o_ref.shape[0]
    ns = xs_ref.shape[0]
    last_f = pl.num_programs(1) - 1

    def x_copy(tile):
        return pltpu.make_async_copy(x_hbm.at[pl.ds(pl.multiple_of(tile * tm, tm), tm), :], xbuf, sem)

    @pl.when((i == 0) & (f == 0))
    def _():
        x_copy(0).start()
        xs = xs_ref[...]
        h_ref[tm:, :] = jnp.zeros((h_ref.shape[0] - tm, h_ref.shape[1]), h_ref.dtype)
        h_ref[tm:tm + ns, :] = _rms(xs, g_ref[...]).astype(h_ref.dtype)
        os_ref[...] = xs

    @pl.when(f == 0)
    def _():
        x_copy(i).wait()
        x = xbuf[...]
        h_ref[0:tm, :] = _rms(x, g_ref[...]).astype(h_ref.dtype)
        o_ref[...] = x

    @pl.when((f == 1) & (i + 1 < pl.num_programs(0)))
    def _():
        x_copy(i + 1).start()

    h = h_ref[...]
    gate = jnp.dot(h, wg_ref[...].astype(BF16), preferred_element_type=F32)
    up = jnp.dot(h, wu_ref[...].astype(BF16), preferred_element_type=F32)
    act = (gate * _sigmoid(gate) * up).astype(BF16)
    down = jnp.dot(act, wd_ref[...].astype(BF16), preferred_element_type=F32)
    o_ref[...] += down[0:tm]

    @pl.when(i == 0)
    def _():
        os_ref[...] += down[tm:tm + ns]

    if final_norm:
        @pl.when(f == last_f)
        def _():
            o_ref[...] = _rms(o_ref[...], fg_ref[...])

        @pl.when((i == 0) & (f == last_f))
        def _():
            os_ref[...] = _rms(os_ref[...], fg_ref[...])


BF16_ROWS = 16


def ffn(x, xs, g, wg, wu, wd, layer, final_g, tm, tf, final_norm):
    m, d = x.shape
    ns = xs.shape[0]
    dff = wg.shape[2]
    tm = min(tm, m)
    assert m % tm == 0 and dff % tf == 0 and dff // tf >= 2 and ns <= BF16_ROWS and tm % BF16_ROWS == 0
    return pl.pallas_call(
        functools.partial(_ffn_kernel, final_norm=final_norm),
        grid=(m // tm, dff // tf),
        in_specs=[_resident((1, d)), _resident((1, d)), _resident((ns, d)),
                  pl.BlockSpec((None, d, tf), lambda i, f: (layer, 0, f)),
                  pl.BlockSpec((None, d, tf), lambda i, f: (layer, 0, f)),
                  pl.BlockSpec((None, tf, d), lambda i, f: (layer, f, 0)),
                  pl.BlockSpec(memory_space=pl.ANY)],
        out_specs=[pl.BlockSpec((tm, d), lambda i, f: (i, 0)), pl.BlockSpec((ns, d), lambda i, f: (0, 0))],
        out_shape=[jax.ShapeDtypeStruct((m, d), F32), jax.ShapeDtypeStruct((ns, d), F32)],
        scratch_shapes=[pltpu.VMEM((tm, d), F32), pltpu.VMEM((tm + BF16_ROWS, d), BF16),
                        pltpu.SemaphoreType.DMA(())],
        compiler_params=_params("arbitrary", "arbitrary"),
        name="ffn",
    )(g.reshape(1, d), final_g.reshape(1, d), xs, wg, wu, wd, x)


A_HALO = 32
B_HALO = 8


def _layer_norm_silu(x, g, b):
    mu = jnp.mean(x, axis=-1, keepdims=True)
    xc = x - mu
    y = xc * lax.rsqrt(jnp.mean(xc * xc, axis=-1, keepdims=True) + EPS) * g + b
    return y * _sigmoid(y)


def _conv_mix_kernel(av_ref, ag_ref, bh_ref, bb_ref, bc_ref, x_ref, adw_ref, adwb_ref, lng_ref, lnb_ref,
                     bdw_ref, wo_ref, y_ref, newa_ref, newb_ref, aext, cext, ashift, aconv, mix, *, ta, tb, rc):
    l = pl.program_id(1)
    tl = av_ref.shape[1]
    wa = av_ref.shape[2]

    @pl.when(l == 0)
    def _():
        aext[0:A_HALO, :] = jnp.zeros((A_HALO, wa), F32)
        cext[0:B_HALO, :] = jnp.zeros((B_HALO, wa), F32)

    @pl.when(l > 0)
    def _():
        aext[0:A_HALO, :] = aext[tl:tl + A_HALO, :]
        cext[0:B_HALO, :] = cext[tl:tl + B_HALO, :]

    aext[A_HALO:A_HALO + tl, :] = av_ref[0] * _sigmoid(ag_ref[0])
    cext[B_HALO:B_HALO + tl, :] = bc_ref[0] * bh_ref[0]

    n_shift = ashift.shape[1]
    for s in range(1, SUBLANES):
        ashift[s - 1] = aext[s:s + n_shift, :]

    for r0 in range(0, tl, rc):
        for c0 in range(0, wa, LANES):
            acc = jnp.broadcast_to(adwb_ref[:, c0:c0 + LANES], (rc, LANES))
            for k in range(ta):
                off = A_HALO - (ta - 1) + r0 + k
                s = off % SUBLANES
                src = aext if s == 0 else ashift.at[s - 1]
                acc = acc + adw_ref[k:k + 1, c0:c0 + LANES] * src[off - s:off - s + rc, c0:c0 + LANES]
            aconv[r0:r0 + rc, c0:c0 + LANES] = acc
    mix[:, 0:wa] = _layer_norm_silu(aconv[...], lng_ref[...], lnb_ref[...]).astype(BF16)

    bconv = jnp.zeros((tl, wa), F32)
    for k in range(tb):
        off = B_HALO - (tb - 1) + k
        bconv = bconv + bdw_ref[k:k + 1, :] * cext[off:off + tl, :]
    mix[:, wa:] = (bb_ref[0] * bconv).astype(BF16)

    y_ref[0] = x_ref[0] + jnp.dot(mix[...], wo_ref[...], preferred_element_type=F32)

    @pl.when(l == pl.num_programs(1) - 1)
    def _():
        newa_ref[0] = aext[A_HALO + tl - (ta - 1):A_HALO + tl, :]
        newb_ref[0] = cext[B_HALO + tl - (tb - 1):B_HALO + tl, :]


def conv_mix(parts, x, a_dw, a_dw_b, ln_g, ln_b, b_dw, w_out, tl):
    bsz, seq, d = x.shape
    wa = parts[0].shape[-1]
    ta, tb = a_dw.shape[0], b_dw.shape[0]
    tl = min(tl, seq)
    assert seq % tl == 0 and tl >= A_HALO and ta - 1 <= A_HALO and tb - 1 <= B_HALO
    part_spec = pl.BlockSpec((1, tl, wa), lambda b, l: (b, l, 0))
    const = _resident
    return pl.pallas_call(
        functools.partial(_conv_mix_kernel, ta=ta, tb=tb, rc=min(128, tl)),
        grid=(bsz, seq // tl),
        in_specs=[part_spec] * 5 + [pl.BlockSpec((1, tl, d), lambda b, l: (b, l, 0)),
                                    const((ta, wa)), const((1, wa)), const((1, wa)), const((1, wa)),
                                    const((tb, wa)), const(w_out.shape)],
        out_specs=[pl.BlockSpec((1, tl, d), lambda b, l: (b, l, 0)),
                   pl.BlockSpec((1, ta - 1, wa), lambda b, l: (b, 0, 0)),
                   pl.BlockSpec((1, tb - 1, wa), lambda b, l: (b, 0, 0))],
        out_shape=[jax.ShapeDtypeStruct((bsz, seq, d), F32),
                   jax.ShapeDtypeStruct((bsz, ta - 1, wa), F32),
                   jax.ShapeDtypeStruct((bsz, tb - 1, wa), F32)],
        scratch_shapes=[pltpu.VMEM((A_HALO + tl, wa), F32), pltpu.VMEM((B_HALO + tl, wa), F32),
                        pltpu.VMEM((SUBLANES - 1, A_HALO + tl - SUBLANES, wa), F32),
                        pltpu.VMEM((tl, wa), F32), pltpu.VMEM((tl, 2 * wa), BF16)],
        compiler_params=_params("parallel", "arbitrary"),
        name="conv_mix",
    )(*parts, x, a_dw, a_dw_b.reshape(1, wa), ln_g.reshape(1, wa), ln_b.reshape(1, wa), b_dw, w_out)


def _conv_step_kernel(av_ref, ag_ref, bh_ref, bb_ref, bc_ref, x_ref, sta_ref, stb_ref, adw_ref, adwb_ref,
                      lng_ref, lnb_ref, bdw_ref, wo_ref, y_ref, anew_ref, cnew_ref, *, ta, tb):
    a = av_ref[...] * _sigmoid(ag_ref[...])
    c = bc_ref[...] * bh_ref[...]
    anew_ref[...] = a
    cnew_ref[...] = c
    acc = adwb_ref[...] + adw_ref[ta - 1:ta, :] * a
    for k in range(ta - 1):
        acc = acc + adw_ref[k:k + 1, :] * sta_ref[k]
    a_out = _layer_norm_silu(acc, lng_ref[...], lnb_ref[...])
    bconv = bdw_ref[tb - 1:tb, :] * c
    for k in range(tb - 1):
        bconv = bconv + bdw_ref[k:k + 1, :] * stb_ref[k]
    mix = jnp.concatenate([a_out, bb_ref[...] * bconv], axis=-1).astype(BF16)
    y_ref[...] = x_ref[...] + jnp.dot(mix, wo_ref[...], preferred_element_type=F32)


def conv_step(parts, x, st_a, st_b, a_dw, a_dw_b, ln_g, ln_b, b_dw, w_out):
    bd, d = x.shape
    wa = parts[0].shape[-1]
    ta, tb = a_dw.shape[0], b_dw.shape[0]
    return pl.pallas_call(
        functools.partial(_conv_step_kernel, ta=ta, tb=tb),
        out_shape=[jax.ShapeDtypeStruct((bd, d), F32), jax.ShapeDtypeStruct((bd, wa), F32),
                   jax.ShapeDtypeStruct((bd, wa), F32)],
        compiler_params=pltpu.CompilerParams(vmem_limit_bytes=VMEM_LIMIT),
        name="conv_step",
    )(*parts, x, st_a, st_b, a_dw, a_dw_b.reshape(1, wa), ln_g.reshape(1, wa), ln_b.reshape(1, wa), b_dw, w_out)


LOG2E = 1.4426950408889634
TQ = 2 * MOBA_BLOCK
DA = 2 * HEAD_DIM


def _split3(x):
    def top_bits(v):
        return lax.bitcast_convert_type(lax.bitcast_convert_type(v, jnp.uint32) & jnp.uint32(0xFFFF0000), F32)

    hi = top_bits(x)
    mid = top_bits(x - hi)
    return hi, mid, x - hi - mid


def _moba_setup(q_ref, k_ref, v_ref, qaug, kaug, vaug, kmean, slope2, nb):
    seq = k_ref.shape[1]
    bs = MOBA_BLOCK
    nbp = kmean.shape[0]
    k = k_ref[0]
    q = q_ref[0]
    kmean[...] = jnp.zeros_like(kmean)
    kmean[0:nb, :] = jnp.mean(k.reshape(nb, bs, HEAD_DIM), axis=1)

    gate = lax.dot_general(kmean[...], q, (((1,), (1,)), ((), ())), precision=lax.Precision.HIGHEST,
                           preferred_element_type=F32)
    blk = lax.broadcasted_iota(jnp.int32, (nbp, seq), 0)
    own = lax.broadcasted_iota(jnp.int32, (nbp, seq), 1) // bs
    gate = jnp.where(blk < own, gate, -jnp.inf)
    attend = blk == own
    for _ in range(MOBA_TOPK):
        m = jnp.max(gate, axis=0, keepdims=True)
        idx = jnp.min(jnp.where(gate == m, blk, nbp), axis=0, keepdims=True)
        hit = blk == idx
        attend = attend | (hit & (blk < own))
        gate = jnp.where(hit, -jnp.inf, gate)
    attend_t = jnp.where(attend, 1.0, 0.0)
    attend_r = jnp.concatenate([attend_t, jnp.zeros((LANES - nbp, seq), F32)], axis=0).T

    lane = lax.broadcasted_iota(jnp.int32, (seq, LANES), 1)
    pos_i = lax.broadcasted_iota(jnp.int32, (seq, LANES), 0)
    pos = pos_i.astype(F32)

    khi, kmid, klo = _split3(slope2 * pos)
    ek = jnp.where(lane == pos_i // bs, 1.0, 0.0)
    ek = jnp.where(lane == nb, khi, ek)
    ek = jnp.where(lane == nb + 1, kmid, ek)
    ek = jnp.where(lane == nb + 2, klo, ek)
    ek = jnp.where((lane >= nb + 3) & (lane < nb + 6), 1.0, ek)
    kaug[:, 0:HEAD_DIM] = k.astype(BF16)
    kaug[:, HEAD_DIM:] = ek.astype(BF16)

    qhi, qmid, qlo = _split3(-slope2 * pos)
    eq = jnp.where(lane < nb, jnp.where(attend_r > 0.5, 0.0, MASKED), 0.0)
    eq = jnp.where((lane >= nb) & (lane < nb + 3), 1.0, eq)
    eq = jnp.where(lane == nb + 3, qhi, eq)
    eq = jnp.where(lane == nb + 4, qmid, eq)
    eq = jnp.where(lane == nb + 5, qlo, eq)
    qaug[:, 0:HEAD_DIM] = (q * (HEAD_DIM ** -0.5 * LOG2E)).astype(BF16)
    qaug[:, HEAD_DIM:] = eq.astype(BF16)

    vaug[:, 0:HEAD_DIM] = v_ref[0].astype(BF16)
    vaug[:, HEAD_DIM:] = jnp.ones((seq, HEAD_DIM), BF16)


SUM_CHAINS = 16


def _page_sums_step(step, n_steps, pt_ref, kc_hbm, ksum_ref, ring, sem, *, layer, pages_per_block):
    total = pt_ref.shape[0]
    depth, page_rows, n_heads, _ = ring.shape

    def page_copy(t, slot):
        return pltpu.make_async_copy(kc_hbm.at[layer, pt_ref[t]], ring.at[slot], sem.at[slot])

    @pl.when(step == 0)
    def _():
        ksum_ref[...] = jnp.zeros_like(ksum_ref)
        for slot in range(min(depth, total)):
            page_copy(slot, slot).start()

    def fold(t, slot):
        page_copy(t, slot).wait()
        page = ring[slot].reshape(SUM_CHAINS, page_rows // SUM_CHAINS, n_heads, HEAD_DIM)
        ksum_ref[t // pages_per_block] += jnp.sum(jnp.sum(page, axis=1), axis=0)

    if n_steps * depth == total:
        for slot in range(depth):
            fold(step * depth + slot, slot)

        @pl.when(step + 1 < n_steps)
        def _():
            for slot in range(depth):
                page_copy((step + 1) * depth + slot, slot).start()
    else:
        for slot in range(depth):
            t = step * depth + slot

            @pl.when(t < total)
            def _(t=t, slot=slot):
                fold(t, slot)

                @pl.when(t + depth < total)
                def _():
                    page_copy(t + depth, slot).start()


def _moba_kernel(pt_ref, q_ref, k_ref, v_ref, kc_hbm, o_ref, ksum_ref, qaug, kaug, vaug, kmean, m_ref, acc_ref,
                 s_ref, ring, psem, *, n_heads, nb, layer, n_steps):
    h = pl.program_id(1)
    i = pl.program_id(2)
    bs = MOBA_BLOCK
    nt = (((1,), (1,)), ((), ()))
    step = (pl.program_id(0) * pl.num_programs(1) + h) * pl.num_programs(2) + i
    _page_sums_step(step, n_steps, pt_ref, kc_hbm, ksum_ref, ring, psem, layer=layer,
                    pages_per_block=MOBA_BLOCK // PAGE_SIZE)

    @pl.when(i == 0)
    def _():
        slope2 = jnp.exp2((h + 1).astype(F32) * (-8.0 / n_heads) + jnp.zeros((1, 1), F32)) * LOG2E
        _moba_setup(q_ref, k_ref, v_ref, qaug, kaug, vaug, kmean, slope2, nb)

    r0 = pl.multiple_of(i * TQ, TQ)
    qa = qaug[pl.ds(r0, TQ), :]

    halves = (slice(0, bs), slice(bs, TQ))

    def scores(g):
        kg = kaug[pl.ds(pl.multiple_of(g * TQ, TQ), TQ), :]
        return jnp.concatenate([lax.dot_general(qa[r], kg, nt, preferred_element_type=F32) for r in halves], axis=0)

    def update(s, g):
        vg = vaug[pl.ds(pl.multiple_of(g * TQ, TQ), TQ), :]
        for r in halves:
            m = m_ref[r, :]
            m_new = jnp.maximum(m, jnp.max(s[r], axis=1, keepdims=True))
            p = jnp.exp2(s[r] - m_new).astype(BF16)
            pv = jnp.dot(p, vg, preferred_element_type=F32)
            acc_ref[r, :] = jnp.exp2(m - m_new) * acc_ref[r, :] + pv
            m_ref[r, :] = m_new

    m_ref[...] = jnp.full(m_ref.shape, MASKED, F32)
    acc_ref[...] = jnp.zeros_like(acc_ref)
    s_ref[0] = scores(0)

    def past_group(g, slot):
        s_ref[1 - slot] = scores(g + 1)
        update(s_ref[slot], g)

    @pl.loop(0, i // 2)
    def _(j):
        past_group(2 * j, 0)
        past_group(2 * j + 1, 1)

    def own_group(slot):
        causal = (lax.broadcasted_iota(jnp.int32, (TQ, TQ), 0) >= lax.broadcasted_iota(jnp.int32, (TQ, TQ), 1))
        update(jnp.where(causal, s_ref[slot], MASKED), i)
        o_ref[0] = acc_ref[:, 0:HEAD_DIM] / acc_ref[:, HEAD_DIM:]

    @pl.when(i % 2 == 0)
    def _():
        own_group(0)

    @pl.when(i % 2 == 1)
    def _():
        past_group(i - 1, 0)
        own_group(1)


def moba_prompt(q, k, v, cache_k, layer, page_table):
    bsz, seq, width = q.shape
    n_heads = width // HEAD_DIM
    assert seq % TQ == 0
    nb = seq // MOBA_BLOCK
    nbp = -(-nb // 8) * 8
    assert nb + 6 <= LANES
    bd, n_pages = page_table.shape
    pages_per_block = MOBA_BLOCK // PAGE_SIZE
    assert n_pages % pages_per_block == 0
    grid = (bsz, n_heads, seq // TQ)
    pages_per_step = -(-bd * n_pages // (grid[0] * grid[1] * grid[2]))
    sums_shape = (bd * n_pages // pages_per_block, n_heads, HEAD_DIM)
    full = pl.BlockSpec((1, seq, HEAD_DIM), lambda b, h, i: (b, 0, h))
    tile = pl.BlockSpec((1, TQ, HEAD_DIM), lambda b, h, i: (b, i, h))
    return pl.pallas_call(
        functools.partial(_moba_kernel, n_heads=n_heads, nb=nb, layer=layer, n_steps=grid[0] * grid[1] * grid[2]),
        grid=grid,
        in_specs=[pl.BlockSpec(memory_space=pltpu.SMEM), full, full, full, pl.BlockSpec(memory_space=pl.ANY)],
        out_specs=[tile, pl.BlockSpec(sums_shape, lambda b, h, i: (0, 0, 0))],
        out_shape=[jax.ShapeDtypeStruct((bsz, seq, width), F32), jax.ShapeDtypeStruct(sums_shape, F32)],
        scratch_shapes=[pltpu.VMEM((seq, DA), BF16), pltpu.VMEM((seq, DA), BF16), pltpu.VMEM((seq, DA), BF16),
                        pltpu.VMEM((nbp, HEAD_DIM), F32), pltpu.VMEM((TQ, 1), F32), pltpu.VMEM((TQ, DA), F32),
                        pltpu.VMEM((2, TQ, TQ), F32),
                        pltpu.VMEM((pages_per_step, PAGE_SIZE, n_heads, HEAD_DIM), F32),
                        pltpu.SemaphoreType.DMA((pages_per_step,))],
        compiler_params=_params("arbitrary", "arbitrary", "arbitrary"),
        name="moba_prompt",
    )(page_table.reshape(-1), q, k, v, cache_k)


PAGES_IN_FLIGHT = 8


def _gate_step_kernel(q_ref, ksum_ref, sel_ref, *, n_heads):
    bd = q_ref.shape[0]
    nblk = ksum_ref.shape[0] // bd
    sel_ref[...] = jnp.zeros_like(sel_ref)
    for b in range(bd):
        kmean = ksum_ref[b * nblk:(b + 1) * nblk] * (1.0 / MOBA_BLOCK)
        gate = jnp.sum(kmean * q_ref[b][None], axis=-1)
        blk = lax.broadcasted_iota(jnp.int32, gate.shape, 0)
        for r in range(MOBA_TOPK):
            m = jnp.max(gate, axis=0, keepdims=True)
            idx = jnp.min(jnp.where(gate == m, blk, nblk), axis=0, keepdims=True)
            sel_ref[b, r:r + 1, 0:n_heads] = idx
            gate = jnp.where(blk == idx, -jnp.inf, gate)


SEL_ROWS = 8


def gate_step(block_sums, q):
    bd, n_heads, _ = q.shape
    assert MOBA_TOPK <= SEL_ROWS and n_heads <= LANES
    sel = pl.pallas_call(
        functools.partial(_gate_step_kernel, n_heads=n_heads),
        out_shape=jax.ShapeDtypeStruct((bd, SEL_ROWS, LANES), jnp.int32),
        compiler_params=pltpu.CompilerParams(vmem_limit_bytes=VMEM_LIMIT),
        name="gate_step",
    )(q, block_sums)
    return sel[:, :MOBA_TOPK, :n_heads]


def _attn_step_kernel(pt_ref, sel_ref, q_ref, kn_ref, vn_ref, k_hbm, v_hbm, o_ref, kbuf, vbuf, ksem, vsem,
                      m_ref, l_ref, acc_ref, *, layer, n_pages, q_pos, pages_per_block):
    bd, n_heads, _ = q_ref.shape
    per_head = MOBA_TOPK * pages_per_block
    total = bd * n_heads * per_head
    depth = kbuf.shape[0]
    scale = HEAD_DIM ** -0.5
    own = q_pos // MOBA_BLOCK

    def coords(t):
        bh, j = t // per_head, t % per_head
        b, h = bh // n_heads, bh % n_heads
        return b, h, j, sel_ref[(b * MOBA_TOPK + j // pages_per_block) * n_heads + h]

    def page_copies(t, slot):
        b, _, j, n = coords(t)
        page = pt_ref[b * n_pages + n * pages_per_block + j % pages_per_block]
        return (pltpu.make_async_copy(k_hbm.at[layer, page], kbuf.at[slot], ksem.at[slot]),
                pltpu.make_async_copy(v_hbm.at[layer, page], vbuf.at[slot], vsem.at[slot]))

    for t in range(min(depth, total)):
        for c in page_copies(t, t):
            c.start()

    @pl.loop(0, total)
    def _(t):
        slot = t % depth
        b, h, j, n = coords(t)
        for c in page_copies(t, slot):
            c.wait()
        q = q_ref[b, pl.ds(h, 1), :]

        @pl.when(j == 0)
        def _():
            m_ref[...] = jnp.sum(q * kn_ref[b, pl.ds(h, 1), :], axis=1, keepdims=True) * scale
            l_ref[...] = jnp.ones_like(l_ref)
            acc_ref[...] = vn_ref[b, pl.ds(h, 1), :]

        k = kbuf[slot, pl.ds(h, PAGE_SIZE, stride=n_heads), :]
        v = vbuf[slot, pl.ds(h, PAGE_SIZE, stride=n_heads), :]
        slope = jnp.exp2(jnp.asarray(h + 1, F32) * (-8.0 / n_heads) + jnp.zeros((1, 1), F32))
        kpos = (n * MOBA_BLOCK + (j % pages_per_block) * PAGE_SIZE
                + lax.broadcasted_iota(jnp.int32, (PAGE_SIZE, 1), 0))
        dist = (q_pos - kpos).astype(F32)
        s = jnp.sum(k * q, axis=1, keepdims=True) * scale - slope * dist
        s = jnp.where((dist >= 0.0) & (n < own), s, MASKED)
        m = m_ref[...]
        m_new = jnp.maximum(m, jnp.max(s, axis=0, keepdims=True))
        alpha = jnp.exp(m - m_new)
        p = jnp.exp(s - m_new)
        l_ref[...] = alpha * l_ref[...] + jnp.sum(p, axis=0, keepdims=True)
        acc_ref[...] = alpha * acc_ref[...] + jnp.sum(p * v, axis=0, keepdims=True)
        m_ref[...] = m_new

        @pl.when(j == per_head - 1)
        def _():
            o_ref[b, pl.ds(h, 1), :] = acc_ref[...] / l_ref[...]

        @pl.when(t + depth < total)
        def _():
            for c in page_copies(t + depth, slot):
                c.start()


def attn_step(cache_k, cache_v, layer, page_table, sel, q, k_new, v_new):
    bd, n_pages = page_table.shape
    n_heads = q.shape[1]
    pages_per_block = MOBA_BLOCK // PAGE_SIZE
    assert n_pages % pages_per_block == 0
    rows = PAGE_SIZE * n_heads
    as_rows = lambda c: c.reshape(c.shape[0], c.shape[1], rows, HEAD_DIM)
    vmem, smem = pl.BlockSpec(memory_space=pltpu.VMEM), pl.BlockSpec(memory_space=pltpu.SMEM)
    hbm = pl.BlockSpec(memory_space=pl.ANY)
    return pl.pallas_call(
        functools.partial(_attn_step_kernel, layer=layer, n_pages=n_pages, q_pos=n_pages * PAGE_SIZE,
                          pages_per_block=pages_per_block),
        in_specs=[smem, smem, vmem, vmem, vmem, hbm, hbm],
        out_specs=vmem,
        out_shape=jax.ShapeDtypeStruct((bd, n_heads, HEAD_DIM), F32),
        scratch_shapes=[pltpu.VMEM((PAGES_IN_FLIGHT, rows, HEAD_DIM), F32),
                        pltpu.VMEM((PAGES_IN_FLIGHT, rows, HEAD_DIM), F32),
                        pltpu.SemaphoreType.DMA((PAGES_IN_FLIGHT,)), pltpu.SemaphoreType.DMA((PAGES_IN_FLIGHT,)),
                        pltpu.VMEM((1, 1), F32), pltpu.VMEM((1, 1), F32), pltpu.VMEM((1, HEAD_DIM), F32)],
        compiler_params=pltpu.CompilerParams(vmem_limit_bytes=VMEM_LIMIT),
        name="attn_step",
    )(page_table.reshape(-1), sel.reshape(-1), q, k_new, v_new, as_rows(cache_k), as_rows(cache_v))


P_HALO = 16


def _pool_out_kernel(o_ref, u_ref, x_ref, pw_ref, ps_ref, wo_ref, y_ref, newp_ref, uext, mix):
    l = pl.program_id(1)
    tl = u_ref.shape[1]
    wp = u_ref.shape[2]
    wo_attn = o_ref.shape[2]
    pg = wp // len(POOL_WINDOWS)

    @pl.when(l == 0)
    def _():
        uext[0:P_HALO, :] = jnp.zeros((P_HALO, wp), F32)

    @pl.when(l > 0)
    def _():
        uext[0:P_HALO, :] = uext[tl:tl + P_HALO, :]

    uext[P_HALO:P_HALO + tl, :] = u_ref[0]
    mix[:, 0:wo_attn] = o_ref[0].astype(BF16)
    pos = l * tl + lax.broadcasted_iota(jnp.int32, (tl, pg), 0)
    for g, w in enumerate(POOL_WINDOWS):
        c0 = g * pg
        wsum = uext[P_HALO:P_HALO + tl, c0:c0 + pg]
        for j in range(1, w):
            wsum = wsum + uext[P_HALO - j:P_HALO - j + tl, c0:c0 + pg]
        count = jnp.minimum(pos + 1, w).astype(F32)
        dlt = (wsum / count - uext[P_HALO:P_HALO + tl, c0:c0 + pg]).astype(BF16)
        yp = jnp.dot(dlt, pw_ref[g], preferred_element_type=F32) * ps_ref[:, c0:c0 + pg]
        mix[:, wo_attn + c0:wo_attn + c0 + pg] = yp.astype(BF16)

    y_ref[0] = x_ref[0] + jnp.dot(mix[...], wo_ref[...], preferred_element_type=F32)

    @pl.when(l == pl.num_programs(1) - 1)
    def _():
        newp_ref[0] = uext[P_HALO + tl - (POOL_MAX - 1):P_HALO + tl, :]


def pool_out(o, u, x, pool_w, pool_scale, w_out, tl):
    bsz, seq, d = x.shape
    wp = u.shape[-1]
    tl = min(tl, seq)
    assert seq % tl == 0 and tl >= P_HALO
    const = _resident
    return pl.pallas_call(
        _pool_out_kernel,
        grid=(bsz, seq // tl),
        in_specs=[pl.BlockSpec((1, tl, o.shape[-1]), lambda b, l: (b, l, 0)),
                  pl.BlockSpec((1, tl, wp), lambda b, l: (b, l, 0)),
                  pl.BlockSpec((1, tl, d), lambda b, l: (b, l, 0)),
                  const(pool_w.shape), const((1, wp)), const(w_out.shape)],
        out_specs=[pl.BlockSpec((1, tl, d), lambda b, l: (b, l, 0)),
                   pl.BlockSpec((1, POOL_MAX - 1, wp), lambda b, l: (b, 0, 0))],
        out_shape=[jax.ShapeDtypeStruct((bsz, seq, d), F32),
                   jax.ShapeDtypeStruct((bsz, POOL_MAX - 1, wp), F32)],
        scratch_shapes=[pltpu.VMEM((P_HALO + tl, wp), F32), pltpu.VMEM((tl, o.shape[-1] + wp), BF16)],
        compiler_params=_params("parallel", "arbitrary"),
        name="pool_out",
    )(o, u, x, pool_w, pool_scale.reshape(1, wp), w_out)


def _pool_step_kernel(o_ref, u_ref, x_ref, stp_ref, pw_ref, ps_ref, wo_ref, y_ref, *, start_pos):
    wp = u_ref.shape[1]
    pg = wp // len(POOL_WINDOWS)
    u = u_ref[...]
    parts = [o_ref[...].astype(BF16)]
    for g, w in enumerate(POOL_WINDOWS):
        c0 = g * pg
        wsum = u[:, c0:c0 + pg]
        for j in range(1, w):
            wsum = wsum + stp_ref[POOL_MAX - 1 - j][:, c0:c0 + pg]
        count = float(min(start_pos + 1, w))
        dlt = (wsum / count - u[:, c0:c0 + pg]).astype(BF16)
        yp = jnp.dot(dlt, pw_ref[g], preferred_element_type=F32) * ps_ref[:, c0:c0 + pg]
        parts.append(yp.astype(BF16))
    mix = jnp.concatenate(parts, axis=-1)
    y_ref[...] = x_ref[...] + jnp.dot(mix, wo_ref[...], preferred_element_type=F32)


def pool_step(o, u, x, st_p, pool_w, pool_scale, w_out, start_pos):
    bd, d = x.shape
    return pl.pallas_call(
        functools.partial(_pool_step_kernel, start_pos=start_pos),
        out_shape=jax.ShapeDtypeStruct((bd, d), F32),
        compiler_params=pltpu.CompilerParams(vmem_limit_bytes=VMEM_LIMIT),
        name="pool_step",
    )(o, u, x, st_p, pool_w, pool_scale.reshape(1, -1), w_out)


TM_PROMPT = 512
TM_FFN = 1024
TF = 256
TL_MIX = 256


def kernel(x_prompt, x_sample, state_conv_a, state_conv_b, cache_k, cache_v, state_pool, page_table, conv_norm_g, conv_w_in, conv_a_dw, conv_a_dw_b, conv_a_ln_g, conv_a_ln_b, conv_b_dw, conv_w_out, attn_norm_g, attn_w_in, pool_w, pool_scale, attn_w_out, ffn_norm_g, ffn_w_gate, ffn_w_up, ffn_w_down, final_norm_g):
    bsz, seq, d = x_prompt.shape
    bd, dec_seq, _ = x_sample.shape
    assert dec_seq == 1
    depth = ffn_norm_g.shape[0]
    n_heads = cache_k.shape[3]
    wattn = n_heads * HEAD_DIM
    past_len = page_table.shape[1] * PAGE_SIZE

    layer_bf16 = lambda w, n: w[n].astype(BF16)

    xp = x_prompt.reshape(bsz * seq, d)
    xs = x_sample.reshape(bd, d)
    pa, pb, pk, pv, pp = [], [], [], [], []
    sa, sb, sk, sv, sp = [], [], [], [], []
    for layer in range(depth):
        i = layer // 2
        if layer % 2 == 0:
            w_in, w_out = layer_bf16(conv_w_in, i), layer_bf16(conv_w_out, i)
            parts = norm_matmul(xp, conv_norm_g[i], w_in, TM_PROMPT)
            wa = parts[0].shape[-1]
            y, na, nb_ = conv_mix([t.reshape(bsz, seq, wa) for t in parts], xp.reshape(bsz, seq, d),
                                  conv_a_dw[i], conv_a_dw_b[i], conv_a_ln_g[i], conv_a_ln_b[i], conv_b_dw[i],
                                  w_out, TL_MIX)
            xp = y.reshape(bsz * seq, d)
            pa.append(na)
            pb.append(nb_)

            parts = norm_matmul(xs, conv_norm_g[i], w_in, TM_PROMPT)
            st_a = jnp.swapaxes(state_conv_a[i], 0, 1)
            st_b = jnp.swapaxes(state_conv_b[i], 0, 1)
            xs, a_new, c_new = conv_step(parts, xs, st_a, st_b, conv_a_dw[i], conv_a_dw_b[i], conv_a_ln_g[i],
                                         conv_a_ln_b[i], conv_b_dw[i], w_out)
            sa.append(jnp.concatenate([state_conv_a[i][:, 1:], a_new[:, None]], axis=1))
            sb.append(jnp.concatenate([state_conv_b[i][:, 1:], c_new[:, None]], axis=1))
        else:
            w_in, w_out, w_pool = layer_bf16(attn_w_in, i), layer_bf16(attn_w_out, i), layer_bf16(pool_w, i)
            q, k, v, u = norm_matmul(xp, attn_norm_g[i], w_in, TM_PROMPT)
            shp = (bsz, seq, wattn)
            o, block_sums = moba_prompt(q.reshape(shp), k.reshape(shp), v.reshape(shp), cache_k, i, page_table)
            y, npool = pool_out(o, u.reshape(bsz, seq, -1), xp.reshape(bsz, seq, d), w_pool, pool_scale[i],
                                w_out, TL_MIX)
            xp = y.reshape(bsz * seq, d)
            pk.append(k.reshape(bsz, seq, n_heads, HEAD_DIM))
            pv.append(v.reshape(bsz, seq, n_heads, HEAD_DIM))
            pp.append(npool)

            q, k, v, u = norm_matmul(xs, attn_norm_g[i], w_in, TM_PROMPT)
            heads = lambda t: t.reshape(bd, n_heads, HEAD_DIM)
            sel = gate_step(block_sums, heads(q))
            o = attn_step(cache_k, cache_v, i, page_table, sel, heads(q), heads(k), heads(v))
            st_p = jnp.swapaxes(state_pool[i], 0, 1)
            xs = pool_step(o.reshape(bd, wattn), u, xs, st_p, w_pool, pool_scale[i], w_out, past_len)
            sk.append(k.reshape(bd, 1, n_heads, HEAD_DIM))
            sv.append(v.reshape(bd, 1, n_heads, HEAD_DIM))
            sp.append(jnp.concatenate([state_pool[i][:, 1:], u[:, None]], axis=1))
        last = layer == depth - 1
        ffn_w = (ffn_w_gate, ffn_w_up, ffn_w_down, layer)
        xp, xs = ffn(xp, xs, ffn_norm_g[layer], *ffn_w, final_norm_g, TM_FFN, TF, last)
    return (xp.reshape(bsz, seq, d), xs.reshape(bd, 1, d), jnp.stack(pa), jnp.stack(sa), jnp.stack(pb),
            jnp.stack(sb), jnp.stack(pk), jnp.stack(sk), jnp.stack(pv), jnp.stack(sv), jnp.stack(pp), jnp.stack(sp))
```

```python
import functools

import jax
import jax.numpy as jnp
from jax import lax
from jax.experimental import pallas as pl
from jax.experimental.pallas import tpu as pltpu

EPS = 1e-6
PAGE_SIZE = 128
MOBA_BLOCK = 256
MOBA_TOPK = 3
HEAD_DIM = 128
POOL_WINDOWS = (2, 4, 8, 16)
POOL_MAX = max(POOL_WINDOWS)
MASKED = -1e30

V7X_VMEM_BYTES = 64 * 1024 * 1024
VMEM_LIMIT = V7X_VMEM_BYTES - 8 * 1024 * 1024
LANES = 128
SUBLANES = 8
COL_GROUP = 1024

BF16 = jnp.bfloat16
F32 = jnp.float32


def _params(*sem):
    return pltpu.CompilerParams(dimension_semantics=sem, vmem_limit_bytes=VMEM_LIMIT)


def _rms(x, g):
    ms = jnp.mean(x * x, axis=-1, keepdims=True)
    return x * lax.rsqrt(ms + EPS) * g


def _sigmoid(x):
    return 1.0 / (1.0 + jnp.exp(-x))


def _resident(shape):
    return pl.BlockSpec(shape, lambda *_: (0,) * len(shape), pipeline_mode=pl.Buffered(1))


def _norm_matmul_kernel(x_ref, g_ref, w_ref, *out_refs, head_major):
    h = _rms(x_ref[...], g_ref[...]).astype(BF16)
    for c, o_ref in enumerate(out_refs):
        r = jnp.dot(h, w_ref[:, c * COL_GROUP:(c + 1) * COL_GROUP], preferred_element_type=F32)
        if c < head_major:
            for hh in range(COL_GROUP // HEAD_DIM):
                o_ref[hh] = r[:, hh * HEAD_DIM:(hh + 1) * HEAD_DIM]
        else:
            o_ref[...] = r


def norm_matmul(x, g, w, tm, head_major=0):
    m, d = x.shape
    n_out = w.shape[1] // COL_GROUP
    tm = min(tm, m)
    heads = COL_GROUP // HEAD_DIM
    flat_spec = pl.BlockSpec((tm, COL_GROUP), lambda i: (i, 0))
    head_spec = pl.BlockSpec((heads, tm, HEAD_DIM), lambda i: (0, i, 0))
    flat_shape = jax.ShapeDtypeStruct((m, COL_GROUP), F32)
    head_shape = jax.ShapeDtypeStruct((heads, m, HEAD_DIM), F32)
    return pl.pallas_call(
        functools.partial(_norm_matmul_kernel, head_major=head_major),
        grid=(m // tm,),
        in_specs=[pl.BlockSpec((tm, d), lambda i: (i, 0)), _resident((1, d)), _resident(w.shape)],
        out_specs=[head_spec] * head_major + [flat_spec] * (n_out - head_major),
        out_shape=[head_shape] * head_major + [flat_shape] * (n_out - head_major),
        compiler_params=_params("parallel"),
        name="norm_matmul",
    )(x, g.reshape(1, d), w)


def _ffn_kernel(g_ref, fg_ref, xs_ref, wg_ref, wu_ref, wd_ref, x_hbm, o_ref, os_ref, xbuf, h_ref, sem, *, final_norm):
    i, f = pl.program_id(0), pl.program_id(1)
    tm = o_ref.shape[0]
    ns = xs_ref.shape[0]
    last_f = pl.num_programs(1) - 1

    def x_copy(tile):
        return pltpu.make_async_copy(x_hbm.at[pl.ds(pl.multiple_of(tile * tm, tm), tm), :], xbuf, sem)

    @pl.when((i == 0) & (f == 0))
    def _():
        x_copy(0).start()
        xs = xs_ref[...]
        h_ref[tm:, :] = jnp.zeros((h_ref.shape[0] - tm, h_ref.shape[1]), h_ref.dtype)
        h_ref[tm:tm + ns, :] = _rms(xs, g_ref[...]).astype(h_ref.dtype)
        os_ref[...] = xs

    @pl.when(f == 0)
    def _():
        x_copy(i).wait()
        x = xbuf[...]
        h_ref[0:tm, :] = _rms(x, g_ref[...]).astype(h_ref.dtype)
        o_ref[...] = x

    @pl.when((f == 1) & (i + 1 < pl.num_programs(0)))
    def _():
        x_copy(i + 1).start()

    h = h_ref[...]
    gate = jnp.dot(h, wg_ref[...].astype(BF16), preferred_element_type=F32)
    up = jnp.dot(h, wu_ref[...].astype(BF16), preferred_element_type=F32)
    act = (gate * _sigmoid(gate) * up).astype(BF16)
    down = jnp.dot(act, wd_ref[...].astype(BF16), preferred_element_type=F32)
    o_ref[...] += down[0:tm]

    @pl.when(i == 0)
    def _():
        os_ref[...] += down[tm:tm + ns]

    if final_norm:
        @pl.when(f == last_f)
        def _():
            o_ref[...] = _rms(o_ref[...], fg_ref[...])

        @pl.when((i == 0) & (f == last_f))
        def _():
            os_ref[...] = _rms(os_ref[...], fg_ref[...])


BF16_ROWS = 16


def ffn(x, xs, g, wg, wu, wd, layer, final_g, tm, tf, final_norm):
    m, d = x.shape
    ns = xs.shape[0]
    dff = wg.shape[2]
    tm = min(tm, m)
    assert m % tm == 0 and dff % tf == 0 and dff // tf >= 2 and ns <= BF16_ROWS and tm % BF16_ROWS == 0
    return pl.pallas_call(
        functools.partial(_ffn_kernel, final_norm=final_norm),
        grid=(m // tm, dff // tf),
        in_specs=[_resident((1, d)), _resident((1, d)), _resident((ns, d)),
                  pl.BlockSpec((None, d, tf), lambda i, f: (layer, 0, f)),
                  pl.BlockSpec((None, d, tf), lambda i, f: (layer, 0, f)),
                  pl.BlockSpec((None, tf, d), lambda i, f: (layer, f, 0)),
                  pl.BlockSpec(memory_space=pl.ANY)],
        out_specs=[pl.BlockSpec((tm, d), lambda i, f: (i, 0)), pl.BlockSpec((ns, d), lambda i, f: (0, 0))],
        out_shape=[jax.ShapeDtypeStruct((m, d), F32), jax.ShapeDtypeStruct((ns, d), F32)],
        scratch_shapes=[pltpu.VMEM((tm, d), F32), pltpu.VMEM((tm + BF16_ROWS, d), BF16),
                        pltpu.SemaphoreType.DMA(())],
        compiler_params=_params("arbitrary", "arbitrary"),
        name="ffn",
    )(g.reshape(1, d), final_g.reshape(1, d), xs, wg, wu, wd, x)


A_HALO = 32
B_HALO = 8


def _layer_norm_silu(x, g, b):
    mu = jnp.mean(x, axis=-1, keepdims=True)
    xc = x - mu
    y = xc * lax.rsqrt(jnp.mean(xc * xc, axis=-1, keepdims=True) + EPS) * g + b
    return y * _sigmoid(y)


def _conv_mix_kernel(av_ref, ag_ref, bh_ref, bb_ref, bc_ref, x_ref, adw_ref, adwb_ref, lng_ref, lnb_ref,
                     bdw_ref, wo_ref, y_ref, newa_ref, newb_ref, aext, cext, ashift, aconv, mix, *, ta, tb, rc):
    l = pl.program_id(1)
    tl = av_ref.shape[1]
    wa = av_ref.shape[2]

    @pl.when(l == 0)
    def _():
        aext[0:A_HALO, :] = jnp.zeros((A_HALO, wa), F32)
        cext[0:B_HALO, :] = jnp.zeros((B_HALO, wa), F32)

    @pl.when(l > 0)
    def _():
        aext[0:A_HALO, :] = aext[tl:tl + A_HALO, :]
        cext[0:B_HALO, :] = cext[tl:tl + B_HALO, :]

    aext[A_HALO:A_HALO + tl, :] = av_ref[0] * _sigmoid(ag_ref[0])
    cext[B_HALO:B_HALO + tl, :] = bc_ref[0] * bh_ref[0]

    n_shift = ashift.shape[1]
    for s in range(1, SUBLANES):
        ashift[s - 1] = aext[s:s + n_shift, :]

    for r0 in range(0, tl, rc):
        for c0 in range(0, wa, LANES):
            acc = jnp.broadcast_to(adwb_ref[:, c0:c0 + LANES], (rc, LANES))
            for k in range(ta):
                off = A_HALO - (ta - 1) + r0 + k
                s = off % SUBLANES
                src = aext if s == 0 else ashift.at[s - 1]
                acc = acc + adw_ref[k:k + 1, c0:c0 + LANES] * src[off - s:off - s + rc, c0:c0 + LANES]
            aconv[r0:r0 + rc, c0:c0 + LANES] = acc
    mix[:, 0:wa] = _layer_norm_silu(aconv[...], lng_ref[...], lnb_ref[...]).astype(BF16)

    bconv = jnp.zeros((tl, wa), F32)
    for k in range(tb):
        off = B_HALO - (tb - 1) + k
        bconv = bconv + bdw_ref[k:k + 1, :] * cext[off:off + tl, :]
    mix[:, wa:] = (bb_ref[0] * bconv).astype(BF16)

    y_ref[0] = x_ref[0] + jnp.dot(mix[...], wo_ref[...], preferred_element_type=F32)

    @pl.when(l == pl.num_programs(1) - 1)
    def _():
        newa_ref[0] = aext[A_HALO + tl - (ta - 1):A_HALO + tl, :]
        newb_ref[0] = cext[B_HALO + tl - (tb - 1):B_HALO + tl, :]


def conv_mix(parts, x, a_dw, a_dw_b, ln_g, ln_b, b_dw, w_out, tl):
    bsz, seq, d = x.shape
    wa = parts[0].shape[-1]
    ta, tb = a_dw.shape[0], b_dw.shape[0]
    tl = min(tl, seq)
    assert seq % tl == 0 and tl >= A_HALO and ta - 1 <= A_HALO and tb - 1 <= B_HALO
    part_spec = pl.BlockSpec((1, tl, wa), lambda b, l: (b, l, 0))
    const = _resident
    return pl.pallas_call(
        functools.partial(_conv_mix_kernel, ta=ta, tb=tb, rc=min(128, tl)),
        grid=(bsz, seq // tl),
        in_specs=[part_spec] * 5 + [pl.BlockSpec((1, tl, d), lambda b, l: (b, l, 0)),
                                    const((ta, wa)), const((1, wa)), const((1, wa)), const((1, wa)),
                                    const((tb, wa)), const(w_out.shape)],
        out_specs=[pl.BlockSpec((1, tl, d), lambda b, l: (b, l, 0)),
                   pl.BlockSpec((1, ta - 1, wa), lambda b, l: (b, 0, 0)),
                   pl.BlockSpec((1, tb - 1, wa), lambda b, l: (b, 0, 0))],
        out_shape=[jax.ShapeDtypeStruct((bsz, seq, d), F32),
                   jax.ShapeDtypeStruct((bsz, ta - 1, wa), F32),
                   jax.ShapeDtypeStruct((bsz, tb - 1, wa), F32)],
        scratch_shapes=[pltpu.VMEM((A_HALO + tl, wa), F32), pltpu.VMEM((B_HALO + tl, wa), F32),
                        pltpu.VMEM((SUBLANES - 1, A_HALO + tl - SUBLANES, wa), F32),
                        pltpu.VMEM((tl, wa), F32), pltpu.VMEM((tl, 2 * wa), BF16)],
        compiler_params=_params("parallel", "arbitrary"),
        name="conv_mix",
    )(*parts, x, a_dw, a_dw_b.reshape(1, wa), ln_g.reshape(1, wa), ln_b.reshape(1, wa), b_dw, w_out)


def _conv_step_kernel(av_ref, ag_ref, bh_ref, bb_ref, bc_ref, x_ref, sta_ref, stb_ref, adw_ref, adwb_ref,
                      lng_ref, lnb_ref, bdw_ref, wo_ref, y_ref, anew_ref, cnew_ref, *, ta, tb):
    a = av_ref[...] * _sigmoid(ag_ref[...])
    c = bc_ref[...] * bh_ref[...]
    anew_ref[...] = a
    cnew_ref[...] = c
    acc = adwb_ref[...] + adw_ref[ta - 1:ta, :] * a
    for k in range(ta - 1):
        acc = acc + adw_ref[k:k + 1, :] * sta_ref[k]
    a_out = _layer_norm_silu(acc, lng_ref[...], lnb_ref[...])
    bconv = bdw_ref[tb - 1:tb, :] * c
    for k in range(tb - 1):
        bconv = bconv + bdw_ref[k:k + 1, :] * stb_ref[k]
    mix = jnp.concatenate([a_out, bb_ref[...] * bconv], axis=-1).astype(BF16)
    y_ref[...] = x_ref[...] + jnp.dot(mix, wo_ref[...], preferred_element_type=F32)


def conv_step(parts, x, st_a, st_b, a_dw, a_dw_b, ln_g, ln_b, b_dw, w_out):
    bd, d = x.shape
    wa = parts[0].shape[-1]
    ta, tb = a_dw.shape[0], b_dw.shape[0]
    return pl.pallas_call(
        functools.partial(_conv_step_kernel, ta=ta, tb=tb),
        out_shape=[jax.ShapeDtypeStruct((bd, d), F32), jax.ShapeDtypeStruct((bd, wa), F32),
                   jax.ShapeDtypeStruct((bd, wa), F32)],
        compiler_params=pltpu.CompilerParams(vmem_limit_bytes=VMEM_LIMIT),
        name="conv_step",
    )(*parts, x, st_a, st_b, a_dw, a_dw_b.reshape(1, wa), ln_g.reshape(1, wa), ln_b.reshape(1, wa), b_dw, w_out)


LOG2E = 1.4426950408889634
TQ = 2 * MOBA_BLOCK
DA = 2 * HEAD_DIM


def _split3(x):
    def top_bits(v):
        return lax.bitcast_convert_type(lax.bitcast_convert_type(v, jnp.uint32) & jnp.uint32(0xFFFF0000), F32)

    hi = top_bits(x)
    mid = top_bits(x - hi)
    return hi, mid, x - hi - mid


def _moba_setup(q_ref, k_ref, v_ref, qaug, kaug, vaug, kmean, slope2, nb):
    seq = k_ref.shape[1]
    bs = MOBA_BLOCK
    nbp = kmean.shape[0]
    k = k_ref[0]
    q = q_ref[0]
    kmean[...] = jnp.zeros_like(kmean)
    kmean[0:nb, :] = jnp.mean(k.reshape(nb, bs, HEAD_DIM), axis=1)

    gate = lax.dot_general(kmean[...], q, (((1,), (1,)), ((), ())), precision=lax.Precision.HIGHEST,
                           preferred_element_type=F32)
    blk = lax.broadcasted_iota(jnp.int32, (nbp, seq), 0)
    own = lax.broadcasted_iota(jnp.int32, (nbp, seq), 1) // bs
    gate = jnp.where(blk < own, gate, -jnp.inf)
    attend = blk == own
    for _ in range(MOBA_TOPK):
        m = jnp.max(gate, axis=0, keepdims=True)
        idx = jnp.min(jnp.where(gate == m, blk, nbp), axis=0, keepdims=True)
        hit = blk == idx
        attend = attend | (hit & (blk < own))
        gate = jnp.where(hit, -jnp.inf, gate)
    attend_t = jnp.where(attend, 1.0, 0.0)
    attend_r = jnp.concatenate([attend_t, jnp.zeros((LANES - nbp, seq), F32)], axis=0).T

    lane = lax.broadcasted_iota(jnp.int32, (seq, LANES), 1)
    pos_i = lax.broadcasted_iota(jnp.int32, (seq, LANES), 0)
    pos = pos_i.astype(F32)

    khi, kmid, klo = _split3(slope2 * pos)
    ek = jnp.where(lane == pos_i // bs, 1.0, 0.0)
    ek = jnp.where(lane == nb, khi, ek)
    ek = jnp.where(lane == nb + 1, kmid, ek)
    ek = jnp.where(lane == nb + 2, klo, ek)
    ek = jnp.where((lane >= nb + 3) & (lane < nb + 6), 1.0, ek)
    kaug[:, 0:HEAD_DIM] = k.astype(BF16)
    kaug[:, HEAD_DIM:] = ek.astype(BF16)

    qhi, qmid, qlo = _split3(-slope2 * pos)
    eq = jnp.where(lane < nb, jnp.where(attend_r > 0.5, 0.0, MASKED), 0.0)
    eq = jnp.where((lane >= nb) & (lane < nb + 3), 1.0, eq)
    eq = jnp.where(lane == nb + 3, qhi, eq)
    eq = jnp.where(lane == nb + 4, qmid, eq)
    eq = jnp.where(lane == nb + 5, qlo, eq)
    qaug[:, 0:HEAD_DIM] = (q * (HEAD_DIM ** -0.5 * LOG2E)).astype(BF16)
    qaug[:, HEAD_DIM:] = eq.astype(BF16)

    vaug[:, 0:HEAD_DIM] = v_ref[0].astype(BF16)
    vaug[:, HEAD_DIM:] = jnp.ones((seq, HEAD_DIM), BF16)


SUM_CHAINS = 16


def _page_sums_step(step, n_steps, pt_ref, kc_hbm, ksum_ref, ring, sem, *, layer, pages_per_block):
    total = pt_ref.shape[0]
    depth, page_rows, n_heads, _ = ring.shape

    def page_copy(t, slot):
        return pltpu.make_async_copy(kc_hbm.at[layer, pt_ref[t]], ring.at[slot], sem.at[slot])

    @pl.when(step == 0)
    def _():
        ksum_ref[...] = jnp.zeros_like(ksum_ref)
        for slot in range(min(depth, total)):
            page_copy(slot, slot).start()

    def fold(t, slot):
        page_copy(t, slot).wait()
        page = ring[slot].reshape(SUM_CHAINS, page_rows // SUM_CHAINS, n_heads, HEAD_DIM)
        ksum_ref[t // pages_per_block] += jnp.sum(jnp.sum(page, axis=1), axis=0)

    if n_steps * depth == total:
        for slot in range(depth):
            fold(step * depth + slot, slot)

        @pl.when(step + 1 < n_steps)
        def _():
            for slot in range(depth):
                page_copy((step + 1) * depth + slot, slot).start()
    else:
        for slot in range(depth):
            t = step * depth + slot

            @pl.when(t < total)
            def _(t=t, slot=slot):
                fold(t, slot)

                @pl.when(t + depth < total)
                def _():
                    page_copy(t + depth, slot).start()


def _moba_kernel(pt_ref, q_ref, k_ref, v_ref, kc_hbm, o_ref, ksum_ref, qaug, kaug, vaug, kmean, m_ref, acc_ref,
                 s_ref, ring, psem, *, n_heads, nb, layer, n_steps):
    h = pl.program_id(1)
    i = pl.program_id(2)
    bs = MOBA_BLOCK
    nt = (((1,), (1,)), ((), ()))
    step = (pl.program_id(0) * pl.num_programs(1) + h) * pl.num_programs(2) + i
    _page_sums_step(step, n_steps, pt_ref, kc_hbm, ksum_ref, ring, psem, layer=layer,
                    pages_per_block=MOBA_BLOCK // PAGE_SIZE)

    @pl.when(i == 0)
    def _():
        slope2 = jnp.exp2((h + 1).astype(F32) * (-8.0 / n_heads) + jnp.zeros((1, 1), F32)) * LOG2E
        _moba_setup(q_ref, k_ref, v_ref, qaug, kaug, vaug, kmean, slope2, nb)

    r0 = pl.multiple_of(i * TQ, TQ)
    qa = qaug[pl.ds(r0, TQ), :]

    halves = (slice(0, bs), slice(bs, TQ))

    def scores(g):
        kg = kaug[pl.ds(pl.multiple_of(g * TQ, TQ), TQ), :]
        return jnp.concatenate([lax.dot_general(qa[r], kg, nt, preferred_element_type=F32) for r in halves], axis=0)

    def update(s, g):
        vg = vaug[pl.ds(pl.multiple_of(g * TQ, TQ), TQ), :]
        for r in halves:
            m = m_ref[r, :]
            m_new = jnp.maximum(m, jnp.max(s[r], axis=1, keepdims=True))
            p = jnp.exp2(s[r] - m_new).astype(BF16)
            pv = jnp.dot(p, vg, preferred_element_type=F32)
            acc_ref[r, :] = jnp.exp2(m - m_new) * acc_ref[r, :] + pv
            m_ref[r, :] = m_new

    m_ref[...] = jnp.full(m_ref.shape, MASKED, F32)
    acc_ref[...] = jnp.zeros_like(acc_ref)
    s_ref[0] = scores(0)

    def past_group(g, slot):
        s_ref[1 - slot] = scores(g + 1)
        update(s_ref[slot], g)

    @pl.loop(0, i // 2)
    def _(j):
        past_group(2 * j, 0)
        past_group(2 * j + 1, 1)

    def own_group(slot):
        causal = (lax.broadcasted_iota(jnp.int32, (TQ, TQ), 0) >= lax.broadcasted_iota(jnp.int32, (TQ, TQ), 1))
        update(jnp.where(causal, s_ref[slot], MASKED), i)
        o_ref[0] = acc_ref[:, 0:HEAD_DIM] / acc_ref[:, HEAD_DIM:]

    @pl.when(i % 2 == 0)
    def _():
        own_group(0)

    @pl.when(i % 2 == 1)
    def _():
        past_group(i - 1, 0)
        own_group(1)


def moba_prompt(q, k, v, cache_k, layer, page_table):
    n_heads, bsz, seq, _ = q.shape
    width = n_heads * HEAD_DIM
    assert seq % TQ == 0
    nb = seq // MOBA_BLOCK
    nbp = -(-nb // 8) * 8
    assert nb + 6 <= LANES
    bd, n_pages = page_table.shape
    pages_per_block = MOBA_BLOCK // PAGE_SIZE
    assert n_pages % pages_per_block == 0
    grid = (bsz, n_heads, seq // TQ)
    pages_per_step = -(-bd * n_pages // (grid[0] * grid[1] * grid[2]))
    sums_shape = (bd * n_pages // pages_per_block, n_heads, HEAD_DIM)
    full = pl.BlockSpec((None, 1, seq, HEAD_DIM), lambda b, h, i: (h, b, 0, 0))
    tile = pl.BlockSpec((1, TQ, HEAD_DIM), lambda b, h, i: (b, i, h))
    return pl.pallas_call(
        functools.partial(_moba_kernel, n_heads=n_heads, nb=nb, layer=layer, n_steps=grid[0] * grid[1] * grid[2]),
        grid=grid,
        in_specs=[pl.BlockSpec(memory_space=pltpu.SMEM), full, full, full, pl.BlockSpec(memory_space=pl.ANY)],
        out_specs=[tile, pl.BlockSpec(sums_shape, lambda b, h, i: (0, 0, 0))],
        out_shape=[jax.ShapeDtypeStruct((bsz, seq, width), F32), jax.ShapeDtypeStruct(sums_shape, F32)],
        scratch_shapes=[pltpu.VMEM((seq, DA), BF16), pltpu.VMEM((seq, DA), BF16), pltpu.VMEM((seq, DA), BF16),
                        pltpu.VMEM((nbp, HEAD_DIM), F32), pltpu.VMEM((TQ, 1), F32), pltpu.VMEM((TQ, DA), F32),
                        pltpu.VMEM((2, TQ, TQ), F32),
                        pltpu.VMEM((pages_per_step, PAGE_SIZE, n_heads, HEAD_DIM), F32),
                        pltpu.SemaphoreType.DMA((pages_per_step,))],
        compiler_params=_params("arbitrary", "arbitrary", "arbitrary"),
        name="moba_prompt",
    )(page_table.reshape(-1), q, k, v, cache_k)


PAGES_IN_FLIGHT = 8


def _gate_step_kernel(q_ref, ksum_ref, sel_ref, *, n_heads):
    bd = q_ref.shape[0]
    nblk = ksum_ref.shape[0] // bd
    sel_ref[...] = jnp.zeros_like(sel_ref)
    for b in range(bd):
        kmean = ksum_ref[b * nblk:(b + 1) * nblk] * (1.0 / MOBA_BLOCK)
        gate = jnp.sum(kmean * q_ref[b][None], axis=-1)
        blk = lax.broadcasted_iota(jnp.int32, gate.shape, 0)
        for r in range(MOBA_TOPK):
            m = jnp.max(gate, axis=0, keepdims=True)
            idx = jnp.min(jnp.where(gate == m, blk, nblk), axis=0, keepdims=True)
            sel_ref[b, r:r + 1, 0:n_heads] = idx
            gate = jnp.where(blk == idx, -jnp.inf, gate)


SEL_ROWS = 8


def gate_step(block_sums, q):
    bd, n_heads, _ = q.shape
    assert MOBA_TOPK <= SEL_ROWS and n_heads <= LANES
    sel = pl.pallas_call(
        functools.partial(_gate_step_kernel, n_heads=n_heads),
        out_shape=jax.ShapeDtypeStruct((bd, SEL_ROWS, LANES), jnp.int32),
        compiler_params=pltpu.CompilerParams(vmem_limit_bytes=VMEM_LIMIT),
        name="gate_step",
    )(q, block_sums)
    return sel[:, :MOBA_TOPK, :n_heads]


def _attn_step_kernel(pt_ref, sel_ref, q_ref, kn_ref, vn_ref, k_hbm, v_hbm, o_ref, kbuf, vbuf, ksem, vsem,
                      m_ref, l_ref, acc_ref, *, layer, n_pages, q_pos, pages_per_block):
    bd, n_heads, _ = q_ref.shape
    per_head = MOBA_TOPK * pages_per_block
    total = bd * n_heads * per_head
    depth = kbuf.shape[0]
    scale = HEAD_DIM ** -0.5
    own = q_pos // MOBA_BLOCK

    def coords(t):
        bh, j = t // per_head, t % per_head
        b, h = bh // n_heads, bh % n_heads
        return b, h, j, sel_ref[(b * MOBA_TOPK + j // pages_per_block) * n_heads + h]

    def page_copies(t, slot):
        b, _, j, n = coords(t)
        page = pt_ref[b * n_pages + n * pages_per_block + j % pages_per_block]
        return (pltpu.make_async_copy(k_hbm.at[layer, page], kbuf.at[slot], ksem.at[slot]),
                pltpu.make_async_copy(v_hbm.at[layer, page], vbuf.at[slot], vsem.at[slot]))

    for t in range(min(depth, total)):
        for c in page_copies(t, t):
            c.start()

    @pl.loop(0, total)
    def _(t):
        slot = t % depth
        b, h, j, n = coords(t)
        for c in page_copies(t, slot):
            c.wait()
        q = q_ref[b, pl.ds(h, 1), :]

        @pl.when(j == 0)
        def _():
            m_ref[...] = jnp.sum(q * kn_ref[b, pl.ds(h, 1), :], axis=1, keepdims=True) * scale
            l_ref[...] = jnp.ones_like(l_ref)
            acc_ref[...] = vn_ref[b, pl.ds(h, 1), :]

        k = kbuf[slot, pl.ds(h, PAGE_SIZE, stride=n_heads), :]
        v = vbuf[slot, pl.ds(h, PAGE_SIZE, stride=n_heads), :]
        slope = jnp.exp2(jnp.asarray(h + 1, F32) * (-8.0 / n_heads) + jnp.zeros((1, 1), F32))
        kpos = (n * MOBA_BLOCK + (j % pages_per_block) * PAGE_SIZE
                + lax.broadcasted_iota(jnp.int32, (PAGE_SIZE, 1), 0))
        dist = (q_pos - kpos).astype(F32)
        s = jnp.sum(k * q, axis=1, keepdims=True) * scale - slope * dist
        s = jnp.where((dist >= 0.0) & (n < own), s, MASKED)
        m = m_ref[...]
        m_new = jnp.maximum(m, jnp.max(s, axis=0, keepdims=True))
        alpha = jnp.exp(m - m_new)
        p = jnp.exp(s - m_new)
        l_ref[...] = alpha * l_ref[...] + jnp.sum(p, axis=0, keepdims=True)
        acc_ref[...] = alpha * acc_ref[...] + jnp.sum(p * v, axis=0, keepdims=True)
        m_ref[...] = m_new

        @pl.when(j == per_head - 1)
        def _():
            o_ref[b, pl.ds(h, 1), :] = acc_ref[...] / l_ref[...]

        @pl.when(t + depth < total)
        def _():
            for c in page_copies(t + depth, slot):
                c.start()


def attn_step(cache_k, cache_v, layer, page_table, sel, q, k_new, v_new):
    bd, n_pages = page_table.shape
    n_heads = q.shape[1]
    pages_per_block = MOBA_BLOCK // PAGE_SIZE
    assert n_pages % pages_per_block == 0
    rows = PAGE_SIZE * n_heads
    as_rows = lambda c: c.reshape(c.shape[0], c.shape[1], rows, HEAD_DIM)
    vmem, smem = pl.BlockSpec(memory_space=pltpu.VMEM), pl.BlockSpec(memory_space=pltpu.SMEM)
    hbm = pl.BlockSpec(memory_space=pl.ANY)
    return pl.pallas_call(
        functools.partial(_attn_step_kernel, layer=layer, n_pages=n_pages, q_pos=n_pages * PAGE_SIZE,
                          pages_per_block=pages_per_block),
        in_specs=[smem, smem, vmem, vmem, vmem, hbm, hbm],
        out_specs=vmem,
        out_shape=jax.ShapeDtypeStruct((bd, n_heads, HEAD_DIM), F32),
        scratch_shapes=[pltpu.VMEM((PAGES_IN_FLIGHT, rows, HEAD_DIM), F32),
                        pltpu.VMEM((PAGES_IN_FLIGHT, rows, HEAD_DIM), F32),
                        pltpu.SemaphoreType.DMA((PAGES_IN_FLIGHT,)), pltpu.SemaphoreType.DMA((PAGES_IN_FLIGHT,)),
                        pltpu.VMEM((1, 1), F32), pltpu.VMEM((1, 1), F32), pltpu.VMEM((1, HEAD_DIM), F32)],
        compiler_params=pltpu.CompilerParams(vmem_limit_bytes=VMEM_LIMIT),
        name="attn_step",
    )(page_table.reshape(-1), sel.reshape(-1), q, k_new, v_new, as_rows(cache_k), as_rows(cache_v))


P_HALO = 16


def _pool_out_kernel(o_ref, u_ref, x_ref, pw_ref, ps_ref, wo_ref, y_ref, newp_ref, uext, mix):
    l = pl.program_id(1)
    tl = u_ref.shape[1]
    wp = u_ref.shape[2]
    wo_attn = o_ref.shape[2]
    pg = wp // len(POOL_WINDOWS)

    @pl.when(l == 0)
    def _():
        uext[0:P_HALO, :] = jnp.zeros((P_HALO, wp), F32)

    @pl.when(l > 0)
    def _():
        uext[0:P_HALO, :] = uext[tl:tl + P_HALO, :]

    uext[P_HALO:P_HALO + tl, :] = u_ref[0]
    mix[:, 0:wo_attn] = o_ref[0].astype(BF16)
    pos = l * tl + lax.broadcasted_iota(jnp.int32, (tl, pg), 0)
    for g, w in enumerate(POOL_WINDOWS):
        c0 = g * pg
        wsum = uext[P_HALO:P_HALO + tl, c0:c0 + pg]
        for j in range(1, w):
            wsum = wsum + uext[P_HALO - j:P_HALO - j + tl, c0:c0 + pg]
        count = jnp.minimum(pos + 1, w).astype(F32)
        dlt = (wsum / count - uext[P_HALO:P_HALO + tl, c0:c0 + pg]).astype(BF16)
        yp = jnp.dot(dlt, pw_ref[g], preferred_element_type=F32) * ps_ref[:, c0:c0 + pg]
        mix[:, wo_attn + c0:wo_attn + c0 + pg] = yp.astype(BF16)

    y_ref[0] = x_ref[0] + jnp.dot(mix[...], wo_ref[...], preferred_element_type=F32)

    @pl.when(l == pl.num_programs(1) - 1)
    def _():
        newp_ref[0] = uext[P_HALO + tl - (POOL_MAX - 1):P_HALO + tl, :]


def pool_out(o, u, x, pool_w, pool_scale, w_out, tl):
    bsz, seq, d = x.shape
    wp = u.shape[-1]
    tl = min(tl, seq)
    assert seq % tl == 0 and tl >= P_HALO
    const = _resident
    return pl.pallas_call(
        _pool_out_kernel,
        grid=(bsz, seq // tl),
        in_specs=[pl.BlockSpec((1, tl, o.shape[-1]), lambda b, l: (b, l, 0)),
                  pl.BlockSpec((1, tl, wp), lambda b, l: (b, l, 0)),
                  pl.BlockSpec((1, tl, d), lambda b, l: (b, l, 0)),
                  const(pool_w.shape), const((1, wp)), const(w_out.shape)],
        out_specs=[pl.BlockSpec((1, tl, d), lambda b, l: (b, l, 0)),
                   pl.BlockSpec((1, POOL_MAX - 1, wp), lambda b, l: (b, 0, 0))],
        out_shape=[jax.ShapeDtypeStruct((bsz, seq, d), F32),
                   jax.ShapeDtypeStruct((bsz, POOL_MAX - 1, wp), F32)],
        scratch_shapes=[pltpu.VMEM((P_HALO + tl, wp), F32), pltpu.VMEM((tl, o.shape[-1] + wp), BF16)],
        compiler_params=_params("parallel", "arbitrary"),
        name="pool_out",
    )(o, u, x, pool_w, pool_scale.reshape(1, wp), w_out)


def _pool_step_kernel(o_ref, u_ref, x_ref, stp_ref, pw_ref, ps_ref, wo_ref, y_ref, *, start_pos):
    wp = u_ref.shape[1]
    pg = wp // len(POOL_WINDOWS)
    u = u_ref[...]
    parts = [o_ref[...].astype(BF16)]
    for g, w in enumerate(POOL_WINDOWS):
        c0 = g * pg
        wsum = u[:, c0:c0 + pg]
        for j in range(1, w):
            wsum = wsum + stp_ref[POOL_MAX - 1 - j][:, c0:c0 + pg]
        count = float(min(start_pos + 1, w))
        dlt = (wsum / count - u[:, c0:c0 + pg]).astype(BF16)
        yp = jnp.dot(dlt, pw_ref[g], preferred_element_type=F32) * ps_ref[:, c0:c0 + pg]
        parts.append(yp.astype(BF16))
    mix = jnp.concatenate(parts, axis=-1)
    y_ref[...] = x_ref[...] + jnp.dot(mix, wo_ref[...], preferred_element_type=F32)


def pool_step(o, u, x, st_p, pool_w, pool_scale, w_out, start_pos):
    bd, d = x.shape
    return pl.pallas_call(
        functools.partial(_pool_step_kernel, start_pos=start_pos),
        out_shape=jax.ShapeDtypeStruct((bd, d), F32),
        compiler_params=pltpu.CompilerParams(vmem_limit_bytes=VMEM_LIMIT),
        name="pool_step",
    )(o, u, x, st_p, pool_w, pool_scale.reshape(1, -1), w_out)


TM_PROMPT = 512
TM_FFN = 1024
TF = 256
TL_MIX = 256


def kernel(x_prompt, x_sample, state_conv_a, state_conv_b, cache_k, cache_v, state_pool, page_table, conv_norm_g, conv_w_in, conv_a_dw, conv_a_dw_b, conv_a_ln_g, conv_a_ln_b, conv_b_dw, conv_w_out, attn_norm_g, attn_w_in, pool_w, pool_scale, attn_w_out, ffn_norm_g, ffn_w_gate, ffn_w_up, ffn_w_down, final_norm_g):
    bsz, seq, d = x_prompt.shape
    bd, dec_seq, _ = x_sample.shape
    assert dec_seq == 1
    depth = ffn_norm_g.shape[0]
    n_heads = cache_k.shape[3]
    wattn = n_heads * HEAD_DIM
    past_len = page_table.shape[1] * PAGE_SIZE

    layer_bf16 = lambda w, n: w[n].astype(BF16)

    xp = x_prompt.reshape(bsz * seq, d)
    xs = x_sample.reshape(bd, d)
    pa, pb, pk, pv, pp = [], [], [], [], []
    sa, sb, sk, sv, sp = [], [], [], [], []
    for layer in range(depth):
        i = layer // 2
        if layer % 2 == 0:
            w_in, w_out = layer_bf16(conv_w_in, i), layer_bf16(conv_w_out, i)
            parts = norm_matmul(xp, conv_norm_g[i], w_in, TM_PROMPT)
            wa = parts[0].shape[-1]
            y, na, nb_ = conv_mix([t.reshape(bsz, seq, wa) for t in parts], xp.reshape(bsz, seq, d),
                                  conv_a_dw[i], conv_a_dw_b[i], conv_a_ln_g[i], conv_a_ln_b[i], conv_b_dw[i],
                                  w_out, TL_MIX)
            xp = y.reshape(bsz * seq, d)
            pa.append(na)
            pb.append(nb_)

            parts = norm_matmul(xs, conv_norm_g[i], w_in, TM_PROMPT)
            st_a = jnp.swapaxes(state_conv_a[i], 0, 1)
            st_b = jnp.swapaxes(state_conv_b[i], 0, 1)
            xs, a_new, c_new = conv_step(parts, xs, st_a, st_b, conv_a_dw[i], conv_a_dw_b[i], conv_a_ln_g[i],
                                         conv_a_ln_b[i], conv_b_dw[i], w_out)
            sa.append(jnp.concatenate([state_conv_a[i][:, 1:], a_new[:, None]], axis=1))
            sb.append(jnp.concatenate([state_conv_b[i][:, 1:], c_new[:, None]], axis=1))
        else:
            w_in, w_out, w_pool = layer_bf16(attn_w_in, i), layer_bf16(attn_w_out, i), layer_bf16(pool_w, i)
            q, k, v, u = norm_matmul(xp, attn_norm_g[i], w_in, TM_PROMPT, head_major=3)
            shp = (n_heads, bsz, seq, HEAD_DIM)
            o, block_sums = moba_prompt(q.reshape(shp), k.reshape(shp), v.reshape(shp), cache_k, i, page_table)
            y, npool = pool_out(o, u.reshape(bsz, seq, -1), xp.reshape(bsz, seq, d), w_pool, pool_scale[i],
                                w_out, TL_MIX)
            xp = y.reshape(bsz * seq, d)
            pk.append(jnp.transpose(k.reshape(shp), (1, 2, 0, 3)))
            pv.append(jnp.transpose(v.reshape(shp), (1, 2, 0, 3)))
            pp.append(npool)

            q, k, v, u = norm_matmul(xs, attn_norm_g[i], w_in, TM_PROMPT, head_major=3)
            heads = lambda t: jnp.swapaxes(t, 0, 1)
            sel = gate_step(block_sums, heads(q))
            o = attn_step(cache_k, cache_v, i, page_table, sel, heads(q), heads(k), heads(v))
            st_p = jnp.swapaxes(state_pool[i], 0, 1)
            xs = pool_step(o.reshape(bd, wattn), u, xs, st_p, w_pool, pool_scale[i], w_out, past_len)
            sk.append(heads(k).reshape(bd, 1, n_heads, HEAD_DIM))
            sv.append(heads(v).reshape(bd, 1, n_heads, HEAD_DIM))
            sp.append(jnp.concatenate([state_pool[i][:, 1:], u[:, None]], axis=1))
        last = layer == depth - 1
        ffn_w = (ffn_w_gate, ffn_w_up, ffn_w_down, layer)
        xp, xs = ffn(xp, xs, ffn_norm_g[layer], *ffn_w, final_norm_g, TM_FFN, TF, last)
    return (xp.reshape(bsz, seq, d), xs.reshape(bd, 1, d), jnp.stack(pa), jnp.stack(sa), jnp.stack(pb),
            jnp.stack(sb), jnp.stack(pk), jnp.stack(sk), jnp.stack(pv), jnp.stack(sv), jnp.stack(pp), jnp.stack(sp))
```

```python
import functools

import jax
import jax.numpy as jnp
from jax import lax
from jax.experimental import pallas as pl
from jax.experimental.pallas import tpu as pltpu

EPS = 1e-6
PAGE_SIZE = 128
MOBA_BLOCK = 256
MOBA_TOPK = 3
HEAD_DIM = 128
POOL_WINDOWS = (2, 4, 8, 16)
POOL_MAX = max(POOL_WINDOWS)
MASKED = -1e30

V7X_VMEM_BYTES = 64 * 1024 * 1024
VMEM_LIMIT = V7X_VMEM_BYTES - 4 * 1024 * 1024
LANES = 128
SUBLANES = 8
COL_GROUP = 1024

BF16 = jnp.bfloat16
F32 = jnp.float32


def _params(*sem):
    return pltpu.CompilerParams(dimension_semantics=sem, vmem_limit_bytes=VMEM_LIMIT)


def _rms(x, g):
    ms = jnp.mean(x * x, axis=-1, keepdims=True)
    return x * lax.rsqrt(ms + EPS) * g


def _sigmoid(x):
    return 1.0 / (1.0 + jnp.exp(-x))


def _resident(shape):
    return pl.BlockSpec(shape, lambda *_: (0,) * len(shape), pipeline_mode=pl.Buffered(1))


def _norm_matmul_kernel(x_ref, g_ref, w_ref, *out_refs, head_major):
    h = _rms(x_ref[...], g_ref[...]).astype(BF16)
    for c, o_ref in enumerate(out_refs):
        r = jnp.dot(h, w_ref[:, c * COL_GROUP:(c + 1) * COL_GROUP], preferred_element_type=F32)
        if c < head_major:
            for hh in range(COL_GROUP // HEAD_DIM):
                o_ref[hh] = r[:, hh * HEAD_DIM:(hh + 1) * HEAD_DIM]
        else:
            o_ref[...] = r


def norm_matmul(x, g, w, tm, head_major=0):
    m, d = x.shape
    n_out = w.shape[1] // COL_GROUP
    tm = min(tm, m)
    heads = COL_GROUP // HEAD_DIM
    flat_spec = pl.BlockSpec((tm, COL_GROUP), lambda i: (i, 0))
    head_spec = pl.BlockSpec((heads, tm, HEAD_DIM), lambda i: (0, i, 0))
    flat_shape = jax.ShapeDtypeStruct((m, COL_GROUP), F32)
    head_shape = jax.ShapeDtypeStruct((heads, m, HEAD_DIM), F32)
    return pl.pallas_call(
        functools.partial(_norm_matmul_kernel, head_major=head_major),
        grid=(m // tm,),
        in_specs=[pl.BlockSpec((tm, d), lambda i: (i, 0)), _resident((1, d)), _resident(w.shape)],
        out_specs=[head_spec] * head_major + [flat_spec] * (n_out - head_major),
        out_shape=[head_shape] * head_major + [flat_shape] * (n_out - head_major),
        compiler_params=_params("parallel"),
        name="norm_matmul",
    )(x, g.reshape(1, d), w)


def _ffn_kernel(g_ref, fg_ref, xs_ref, wg_ref, wu_ref, wd_ref, x_hbm, o_ref, os_ref, xbuf, h_ref, sem, *, final_norm):
    i, f = pl.program_id(0), pl.program_id(1)
    tm = o_ref.shape[0]
    ns = xs_ref.shape[0]
    last_f = pl.num_programs(1) - 1

    def x_copy(tile):
        return pltpu.make_async_copy(x_hbm.at[pl.ds(pl.multiple_of(tile * tm, tm), tm), :], xbuf, sem)

    @pl.when((i == 0) & (f == 0))
    def _():
        x_copy(0).start()
        xs = xs_ref[...]
        h_ref[tm:, :] = jnp.zeros((h_ref.shape[0] - tm, h_ref.shape[1]), h_ref.dtype)
        h_ref[tm:tm + ns, :] = _rms(xs, g_ref[...]).astype(h_ref.dtype)
        os_ref[...] = xs

    @pl.when(f == 0)
    def _():
        x_copy(i).wait()
        x = xbuf[...]
        h_ref[0:tm, :] = _rms(x, g_ref[...]).astype(h_ref.dtype)
        o_ref[...] = x

    @pl.when((f == 1) & (i + 1 < pl.num_programs(0)))
    def _():
        x_copy(i + 1).start()

    h = h_ref[...]
    gate = jnp.dot(h, wg_ref[...].astype(BF16), preferred_element_type=F32)
    up = jnp.dot(h, wu_ref[...].astype(BF16), preferred_element_type=F32)
    act = (gate * _sigmoid(gate) * up).astype(BF16)
    down = jnp.dot(act, wd_ref[...].astype(BF16), preferred_element_type=F32)
    o_ref[...] += down[0:tm]

    @pl.when(i == 0)
    def _():
        os_ref[...] += down[tm:tm + ns]

    if final_norm:
        @pl.when(f == last_f)
        def _():
            o_ref[...] = _rms(o_ref[...], fg_ref[...])

        @pl.when((i == 0) & (f == last_f))
        def _():
            os_ref[...] = _rms(os_ref[...], fg_ref[...])


BF16_ROWS = 16


def ffn(x, xs, g, wg, wu, wd, layer, final_g, tm, tf, final_norm):
    m, d = x.shape
    ns = xs.shape[0]
    dff = wg.shape[2]
    tm = min(tm, m)
    assert m % tm == 0 and dff % tf == 0 and dff // tf >= 2 and ns <= BF16_ROWS and tm % BF16_ROWS == 0
    return pl.pallas_call(
        functools.partial(_ffn_kernel, final_norm=final_norm),
        grid=(m // tm, dff // tf),
        in_specs=[_resident((1, d)), _resident((1, d)), _resident((ns, d)),
                  pl.BlockSpec((None, d, tf), lambda i, f: (layer, 0, f)),
                  pl.BlockSpec((None, d, tf), lambda i, f: (layer, 0, f)),
                  pl.BlockSpec((None, tf, d), lambda i, f: (layer, f, 0)),
                  pl.BlockSpec(memory_space=pl.ANY)],
        out_specs=[pl.BlockSpec((tm, d), lambda i, f: (i, 0)), pl.BlockSpec((ns, d), lambda i, f: (0, 0))],
        out_shape=[jax.ShapeDtypeStruct((m, d), F32), jax.ShapeDtypeStruct((ns, d), F32)],
        scratch_shapes=[pltpu.VMEM((tm, d), F32), pltpu.VMEM((tm + BF16_ROWS, d), BF16),
                        pltpu.SemaphoreType.DMA(())],
        compiler_params=_params("arbitrary", "arbitrary"),
        name="ffn",
    )(g.reshape(1, d), final_g.reshape(1, d), xs, wg, wu, wd, x)


A_HALO = 32
B_HALO = 8


def _layer_norm_silu(x, g, b):
    mu = jnp.mean(x, axis=-1, keepdims=True)
    xc = x - mu
    y = xc * lax.rsqrt(jnp.mean(xc * xc, axis=-1, keepdims=True) + EPS) * g + b
    return y * _sigmoid(y)


def _conv_mix_kernel(av_ref, ag_ref, bh_ref, bb_ref, bc_ref, x_ref, adw_ref, adwb_ref, lng_ref, lnb_ref,
                     bdw_ref, wo_ref, y_ref, newa_ref, newb_ref, aext, cext, ashift, aconv, mix, *, ta, tb, rc):
    l = pl.program_id(1)
    tl = av_ref.shape[1]
    wa = av_ref.shape[2]

    @pl.when(l == 0)
    def _():
        aext[0:A_HALO, :] = jnp.zeros((A_HALO, wa), F32)
        cext[0:B_HALO, :] = jnp.zeros((B_HALO, wa), F32)

    @pl.when(l > 0)
    def _():
        aext[0:A_HALO, :] = aext[tl:tl + A_HALO, :]
        cext[0:B_HALO, :] = cext[tl:tl + B_HALO, :]

    aext[A_HALO:A_HALO + tl, :] = av_ref[0] * _sigmoid(ag_ref[0])
    cext[B_HALO:B_HALO + tl, :] = bc_ref[0] * bh_ref[0]

    n_shift = ashift.shape[1]
    for s in range(1, SUBLANES):
        ashift[s - 1] = aext[s:s + n_shift, :]

    for r0 in range(0, tl, rc):
        for c0 in range(0, wa, LANES):
            acc = jnp.broadcast_to(adwb_ref[:, c0:c0 + LANES], (rc, LANES))
            for k in range(ta):
                off = A_HALO - (ta - 1) + r0 + k
                s = off % SUBLANES
                src = aext if s == 0 else ashift.at[s - 1]
                acc = acc + adw_ref[k:k + 1, c0:c0 + LANES] * src[off - s:off - s + rc, c0:c0 + LANES]
            aconv[r0:r0 + rc, c0:c0 + LANES] = acc
    mix[:, 0:wa] = _layer_norm_silu(aconv[...], lng_ref[...], lnb_ref[...]).astype(BF16)

    bconv = jnp.zeros((tl, wa), F32)
    for k in range(tb):
        off = B_HALO - (tb - 1) + k
        bconv = bconv + bdw_ref[k:k + 1, :] * cext[off:off + tl, :]
    mix[:, wa:] = (bb_ref[0] * bconv).astype(BF16)

    y_ref[0] = x_ref[0] + jnp.dot(mix[...], wo_ref[...], preferred_element_type=F32)

    @pl.when(l == pl.num_programs(1) - 1)
    def _():
        newa_ref[0] = aext[A_HALO + tl - (ta - 1):A_HALO + tl, :]
        newb_ref[0] = cext[B_HALO + tl - (tb - 1):B_HALO + tl, :]


def conv_mix(parts, x, a_dw, a_dw_b, ln_g, ln_b, b_dw, w_out, tl):
    bsz, seq, d = x.shape
    wa = parts[0].shape[-1]
    ta, tb = a_dw.shape[0], b_dw.shape[0]
    tl = min(tl, seq)
    assert seq % tl == 0 and tl >= A_HALO and ta - 1 <= A_HALO and tb - 1 <= B_HALO
    part_spec = pl.BlockSpec((1, tl, wa), lambda b, l: (b, l, 0))
    const = _resident
    return pl.pallas_call(
        functools.partial(_conv_mix_kernel, ta=ta, tb=tb, rc=min(128, tl)),
        grid=(bsz, seq // tl),
        in_specs=[part_spec] * 5 + [pl.BlockSpec((1, tl, d), lambda b, l: (b, l, 0)),
                                    const((ta, wa)), const((1, wa)), const((1, wa)), const((1, wa)),
                                    const((tb, wa)), const(w_out.shape)],
        out_specs=[pl.BlockSpec((1, tl, d), lambda b, l: (b, l, 0)),
                   pl.BlockSpec((1, ta - 1, wa), lambda b, l: (b, 0, 0)),
                   pl.BlockSpec((1, tb - 1, wa), lambda b, l: (b, 0, 0))],
        out_shape=[jax.ShapeDtypeStruct((bsz, seq, d), F32),
                   jax.ShapeDtypeStruct((bsz, ta - 1, wa), F32),
                   jax.ShapeDtypeStruct((bsz, tb - 1, wa), F32)],
        scratch_shapes=[pltpu.VMEM((A_HALO + tl, wa), F32), pltpu.VMEM((B_HALO + tl, wa), F32),
                        pltpu.VMEM((SUBLANES - 1, A_HALO + tl - SUBLANES, wa), F32),
                        pltpu.VMEM((tl, wa), F32), pltpu.VMEM((tl, 2 * wa), BF16)],
        compiler_params=_params("parallel", "arbitrary"),
        name="conv_mix",
    )(*parts, x, a_dw, a_dw_b.reshape(1, wa), ln_g.reshape(1, wa), ln_b.reshape(1, wa), b_dw, w_out)


def _conv_step_kernel(av_ref, ag_ref, bh_ref, bb_ref, bc_ref, x_ref, sta_ref, stb_ref, adw_ref, adwb_ref,
                      lng_ref, lnb_ref, bdw_ref, wo_ref, y_ref, anew_ref, cnew_ref, *, ta, tb):
    a = av_ref[...] * _sigmoid(ag_ref[...])
    c = bc_ref[...] * bh_ref[...]
    anew_ref[...] = a
    cnew_ref[...] = c
    acc = adwb_ref[...] + adw_ref[ta - 1:ta, :] * a
    for k in range(ta - 1):
        acc = acc + adw_ref[k:k + 1, :] * sta_ref[k]
    a_out = _layer_norm_silu(acc, lng_ref[...], lnb_ref[...])
    bconv = bdw_ref[tb - 1:tb, :] * c
    for k in range(tb - 1):
        bconv = bconv + bdw_ref[k:k + 1, :] * stb_ref[k]
    mix = jnp.concatenate([a_out, bb_ref[...] * bconv], axis=-1).astype(BF16)
    y_ref[...] = x_ref[...] + jnp.dot(mix, wo_ref[...], preferred_element_type=F32)


def conv_step(parts, x, st_a, st_b, a_dw, a_dw_b, ln_g, ln_b, b_dw, w_out):
    bd, d = x.shape
    wa = parts[0].shape[-1]
    ta, tb = a_dw.shape[0], b_dw.shape[0]
    return pl.pallas_call(
        functools.partial(_conv_step_kernel, ta=ta, tb=tb),
        out_shape=[jax.ShapeDtypeStruct((bd, d), F32), jax.ShapeDtypeStruct((bd, wa), F32),
                   jax.ShapeDtypeStruct((bd, wa), F32)],
        compiler_params=pltpu.CompilerParams(vmem_limit_bytes=VMEM_LIMIT),
        name="conv_step",
    )(*parts, x, st_a, st_b, a_dw, a_dw_b.reshape(1, wa), ln_g.reshape(1, wa), ln_b.reshape(1, wa), b_dw, w_out)


LOG2E = 1.4426950408889634
TQ = 2 * MOBA_BLOCK
DA = 2 * HEAD_DIM


def _split3(x):
    def top_bits(v):
        return lax.bitcast_convert_type(lax.bitcast_convert_type(v, jnp.uint32) & jnp.uint32(0xFFFF0000), F32)

    hi = top_bits(x)
    mid = top_bits(x - hi)
    return hi, mid, x - hi - mid


def _moba_setup(q_ref, k_ref, v_ref, qaug, kaug, vaug, kmean, slope2, nb):
    seq = k_ref.shape[1]
    bs = MOBA_BLOCK
    nbp = kmean.shape[0]
    k = k_ref[0]
    q = q_ref[0]
    kmean[...] = jnp.zeros_like(kmean)
    kmean[0:nb, :] = jnp.mean(k.reshape(nb, bs, HEAD_DIM), axis=1)

    gate = lax.dot_general(kmean[...], q, (((1,), (1,)), ((), ())), precision=lax.Precision.HIGHEST,
                           preferred_element_type=F32)
    blk = lax.broadcasted_iota(jnp.int32, (nbp, seq), 0)
    own = lax.broadcasted_iota(jnp.int32, (nbp, seq), 1) // bs
    gate = jnp.where(blk < own, gate, -jnp.inf)
    attend = blk == own
    for _ in range(MOBA_TOPK):
        m = jnp.max(gate, axis=0, keepdims=True)
        idx = jnp.min(jnp.where(gate == m, blk, nbp), axis=0, keepdims=True)
        hit = blk == idx
        attend = attend | (hit & (blk < own))
        gate = jnp.where(hit, -jnp.inf, gate)
    attend_t = jnp.where(attend, 1.0, 0.0)
    attend_r = jnp.concatenate([attend_t, jnp.zeros((LANES - nbp, seq), F32)], axis=0).T

    lane = lax.broadcasted_iota(jnp.int32, (seq, LANES), 1)
    pos_i = lax.broadcasted_iota(jnp.int32, (seq, LANES), 0)
    pos = pos_i.astype(F32)

    khi, kmid, klo = _split3(slope2 * pos)
    ek = jnp.where(lane == pos_i // bs, 1.0, 0.0)
    ek = jnp.where(lane == nb, khi, ek)
    ek = jnp.where(lane == nb + 1, kmid, ek)
    ek = jnp.where(lane == nb + 2, klo, ek)
    ek = jnp.where((lane >= nb + 3) & (lane < nb + 6), 1.0, ek)
    kaug[:, 0:HEAD_DIM] = k.astype(BF16)
    kaug[:, HEAD_DIM:] = ek.astype(BF16)

    qhi, qmid, qlo = _split3(-slope2 * pos)
    eq = jnp.where(lane < nb, jnp.where(attend_r > 0.5, 0.0, MASKED), 0.0)
    eq = jnp.where((lane >= nb) & (lane < nb + 3), 1.0, eq)
    eq = jnp.where(lane == nb + 3, qhi, eq)
    eq = jnp.where(lane == nb + 4, qmid, eq)
    eq = jnp.where(lane == nb + 5, qlo, eq)
    qaug[:, 0:HEAD_DIM] = (q * (HEAD_DIM ** -0.5 * LOG2E)).astype(BF16)
    qaug[:, HEAD_DIM:] = eq.astype(BF16)

    vaug[:, 0:HEAD_DIM] = v_ref[0].astype(BF16)
    vaug[:, HEAD_DIM:] = jnp.ones((seq, HEAD_DIM), BF16)


SUM_CHAINS = 16


def _page_sums_step(step, n_steps, pt_ref, kc_hbm, ksum_ref, ring, sem, *, layer, pages_per_block):
    total = pt_ref.shape[0]
    depth, page_rows, n_heads, _ = ring.shape

    def page_copy(t, slot):
        return pltpu.make_async_copy(kc_hbm.at[layer, pt_ref[t]], ring.at[slot], sem.at[slot])

    @pl.when(step == 0)
    def _():
        ksum_ref[...] = jnp.zeros_like(ksum_ref)
        for slot in range(min(depth, total)):
            page_copy(slot, slot).start()

    def fold(t, slot):
        page_copy(t, slot).wait()
        page = ring[slot].reshape(SUM_CHAINS, page_rows // SUM_CHAINS, n_heads, HEAD_DIM)
        ksum_ref[t // pages_per_block] += jnp.sum(jnp.sum(page, axis=1), axis=0)

    if n_steps * depth == total:
        for slot in range(depth):
            fold(step * depth + slot, slot)

        @pl.when(step + 1 < n_steps)
        def _():
            for slot in range(depth):
                page_copy((step + 1) * depth + slot, slot).start()
    else:
        for slot in range(depth):
            t = step * depth + slot

            @pl.when(t < total)
            def _(t=t, slot=slot):
                fold(t, slot)

                @pl.when(t + depth < total)
                def _():
                    page_copy(t + depth, slot).start()


def _moba_kernel(pt_ref, q_ref, k_ref, v_ref, kc_hbm, o_ref, ksum_ref, qaug, kaug, vaug, kmean, m_ref, acc_ref,
                 s_ref, ring, psem, *, n_heads, nb, layer, n_steps):
    h = pl.program_id(1)
    i = pl.program_id(2)
    bs = MOBA_BLOCK
    nt = (((1,), (1,)), ((), ()))
    step = (pl.program_id(0) * pl.num_programs(1) + h) * pl.num_programs(2) + i
    _page_sums_step(step, n_steps, pt_ref, kc_hbm, ksum_ref, ring, psem, layer=layer,
                    pages_per_block=MOBA_BLOCK // PAGE_SIZE)

    @pl.when(i == 0)
    def _():
        slope2 = jnp.exp2((h + 1).astype(F32) * (-8.0 / n_heads) + jnp.zeros((1, 1), F32)) * LOG2E
        _moba_setup(q_ref, k_ref, v_ref, qaug, kaug, vaug, kmean, slope2, nb)

    r0 = pl.multiple_of(i * TQ, TQ)
    qa = qaug[pl.ds(r0, TQ), :]

    halves = (slice(0, bs), slice(bs, TQ))

    def scores(g):
        kg = kaug[pl.ds(pl.multiple_of(g * TQ, TQ), TQ), :]
        return jnp.concatenate([lax.dot_general(qa[r], kg, nt, preferred_element_type=F32) for r in halves], axis=0)

    def update(s, g):
        vg = vaug[pl.ds(pl.multiple_of(g * TQ, TQ), TQ), :]
        for r in halves:
            m = m_ref[r, :]
            m_new = jnp.maximum(m, jnp.max(s[r], axis=1, keepdims=True))
            p = jnp.exp2(s[r] - m_new).astype(BF16)
            pv = jnp.dot(p, vg, preferred_element_type=F32)
            acc_ref[r, :] = jnp.exp2(m - m_new) * acc_ref[r, :] + pv
            m_ref[r, :] = m_new

    m_ref[...] = jnp.full(m_ref.shape, MASKED, F32)
    acc_ref[...] = jnp.zeros_like(acc_ref)
    s_ref[0] = scores(0)

    def past_group(g, slot):
        s_ref[1 - slot] = scores(g + 1)
        update(s_ref[slot], g)

    @pl.loop(0, i // 2)
    def _(j):
        past_group(2 * j, 0)
        past_group(2 * j + 1, 1)

    def own_group(slot):
        causal = (lax.broadcasted_iota(jnp.int32, (TQ, TQ), 0) >= lax.broadcasted_iota(jnp.int32, (TQ, TQ), 1))
        update(jnp.where(causal, s_ref[slot], MASKED), i)
        o_ref[0] = acc_ref[:, 0:HEAD_DIM] / acc_ref[:, HEAD_DIM:]

    @pl.when(i % 2 == 0)
    def _():
        own_group(0)

    @pl.when(i % 2 == 1)
    def _():
        past_group(i - 1, 0)
        own_group(1)


def moba_prompt(q, k, v, cache_k, layer, page_table):
    n_heads, bsz, seq, _ = q.shape
    width = n_heads * HEAD_DIM
    assert seq % TQ == 0
    nb = seq // MOBA_BLOCK
    nbp = -(-nb // 8) * 8
    assert nb + 6 <= LANES
    bd, n_pages = page_table.shape
    pages_per_block = MOBA_BLOCK // PAGE_SIZE
    assert n_pages % pages_per_block == 0
    grid = (bsz, n_heads, seq // TQ)
    pages_per_step = -(-bd * n_pages // (grid[0] * grid[1] * grid[2]))
    sums_shape = (bd * n_pages // pages_per_block, n_heads, HEAD_DIM)
    full = pl.BlockSpec((None, 1, seq, HEAD_DIM), lambda b, h, i: (h, b, 0, 0))
    tile = pl.BlockSpec((1, TQ, HEAD_DIM), lambda b, h, i: (b, i, h))
    return pl.pallas_call(
        functools.partial(_moba_kernel, n_heads=n_heads, nb=nb, layer=layer, n_steps=grid[0] * grid[1] * grid[2]),
        grid=grid,
        in_specs=[pl.BlockSpec(memory_space=pltpu.SMEM), full, full, full, pl.BlockSpec(memory_space=pl.ANY)],
        out_specs=[tile, pl.BlockSpec(sums_shape, lambda b, h, i: (0, 0, 0))],
        out_shape=[jax.ShapeDtypeStruct((bsz, seq, width), F32), jax.ShapeDtypeStruct(sums_shape, F32)],
        scratch_shapes=[pltpu.VMEM((seq, DA), BF16), pltpu.VMEM((seq, DA), BF16), pltpu.VMEM((seq, DA), BF16),
                        pltpu.VMEM((nbp, HEAD_DIM), F32), pltpu.VMEM((TQ, 1), F32), pltpu.VMEM((TQ, DA), F32),
                        pltpu.VMEM((2, TQ, TQ), F32),
                        pltpu.VMEM((pages_per_step, PAGE_SIZE, n_heads, HEAD_DIM), F32),
                        pltpu.SemaphoreType.DMA((pages_per_step,))],
        compiler_params=_params("arbitrary", "arbitrary", "arbitrary"),
        name="moba_prompt",
    )(page_table.reshape(-1), q, k, v, cache_k)


PAGES_IN_FLIGHT = 8


def _gate_step_kernel(q_ref, ksum_ref, sel_ref, *, n_heads):
    bd = q_ref.shape[0]
    nblk = ksum_ref.shape[0] // bd
    sel_ref[...] = jnp.zeros_like(sel_ref)
    for b in range(bd):
        kmean = ksum_ref[b * nblk:(b + 1) * nblk] * (1.0 / MOBA_BLOCK)
        gate = jnp.sum(kmean * q_ref[b][None], axis=-1)
        blk = lax.broadcasted_iota(jnp.int32, gate.shape, 0)
        for r in range(MOBA_TOPK):
            m = jnp.max(gate, axis=0, keepdims=True)
            idx = jnp.min(jnp.where(gate == m, blk, nblk), axis=0, keepdims=True)
            sel_ref[b, r:r + 1, 0:n_heads] = idx
            gate = jnp.where(blk == idx, -jnp.inf, gate)


SEL_ROWS = 8


def gate_step(block_sums, q):
    bd, n_heads, _ = q.shape
    assert MOBA_TOPK <= SEL_ROWS and n_heads <= LANES
    sel = pl.pallas_call(
        functools.partial(_gate_step_kernel, n_heads=n_heads),
        out_shape=jax.ShapeDtypeStruct((bd, SEL_ROWS, LANES), jnp.int32),
        compiler_params=pltpu.CompilerParams(vmem_limit_bytes=VMEM_LIMIT),
        name="gate_step",
    )(q, block_sums)
    return sel[:, :MOBA_TOPK, :n_heads]


def _attn_step_kernel(pt_ref, sel_ref, q_ref, kn_ref, vn_ref, k_hbm, v_hbm, o_ref, kbuf, vbuf, ksem, vsem,
                      m_ref, l_ref, acc_ref, *, layer, n_pages, q_pos, pages_per_block):
    bd, n_heads, _ = q_ref.shape
    per_head = MOBA_TOPK * pages_per_block
    total = bd * n_heads * per_head
    depth = kbuf.shape[0]
    scale = HEAD_DIM ** -0.5
    own = q_pos // MOBA_BLOCK

    def coords(t):
        bh, j = t // per_head, t % per_head
        b, h = bh // n_heads, bh % n_heads
        return b, h, j, sel_ref[(b * MOBA_TOPK + j // pages_per_block) * n_heads + h]

    def page_copies(t, slot):
        b, _, j, n = coords(t)
        page = pt_ref[b * n_pages + n * pages_per_block + j % pages_per_block]
        return (pltpu.make_async_copy(k_hbm.at[layer, page], kbuf.at[slot], ksem.at[slot]),
                pltpu.make_async_copy(v_hbm.at[layer, page], vbuf.at[slot], vsem.at[slot]))

    for t in range(min(depth, total)):
        for c in page_copies(t, t):
            c.start()

    @pl.loop(0, total)
    def _(t):
        slot = t % depth
        b, h, j, n = coords(t)
        for c in page_copies(t, slot):
            c.wait()
        q = q_ref[b, pl.ds(h, 1), :]

        @pl.when(j == 0)
        def _():
            m_ref[...] = jnp.sum(q * kn_ref[b, pl.ds(h, 1), :], axis=1, keepdims=True) * scale
            l_ref[...] = jnp.ones_like(l_ref)
            acc_ref[...] = vn_ref[b, pl.ds(h, 1), :]

        k = kbuf[slot, pl.ds(h, PAGE_SIZE, stride=n_heads), :]
        v = vbuf[slot, pl.ds(h, PAGE_SIZE, stride=n_heads), :]
        slope = jnp.exp2(jnp.asarray(h + 1, F32) * (-8.0 / n_heads) + jnp.zeros((1, 1), F32))
        kpos = (n * MOBA_BLOCK + (j % pages_per_block) * PAGE_SIZE
                + lax.broadcasted_iota(jnp.int32, (PAGE_SIZE, 1), 0))
        dist = (q_pos - kpos).astype(F32)
        s = jnp.sum(k * q, axis=1, keepdims=True) * scale - slope * dist
        s = jnp.where((dist >= 0.0) & (n < own), s, MASKED)
        m = m_ref[...]
        m_new = jnp.maximum(m, jnp.max(s, axis=0, keepdims=True))
        alpha = jnp.exp(m - m_new)
        p = jnp.exp(s - m_new)
        l_ref[...] = alpha * l_ref[...] + jnp.sum(p, axis=0, keepdims=True)
        acc_ref[...] = alpha * acc_ref[...] + jnp.sum(p * v, axis=0, keepdims=True)
        m_ref[...] = m_new

        @pl.when(j == per_head - 1)
        def _():
            o_ref[b, pl.ds(h, 1), :] = acc_ref[...] / l_ref[...]

        @pl.when(t + depth < total)
        def _():
            for c in page_copies(t + depth, slot):
                c.start()


def attn_step(cache_k, cache_v, layer, page_table, sel, q, k_new, v_new):
    bd, n_pages = page_table.shape
    n_heads = q.shape[1]
    pages_per_block = MOBA_BLOCK // PAGE_SIZE
    assert n_pages % pages_per_block == 0
    rows = PAGE_SIZE * n_heads
    as_rows = lambda c: c.reshape(c.shape[0], c.shape[1], rows, HEAD_DIM)
    vmem, smem = pl.BlockSpec(memory_space=pltpu.VMEM), pl.BlockSpec(memory_space=pltpu.SMEM)
    hbm = pl.BlockSpec(memory_space=pl.ANY)
    return pl.pallas_call(
        functools.partial(_attn_step_kernel, layer=layer, n_pages=n_pages, q_pos=n_pages * PAGE_SIZE,
                          pages_per_block=pages_per_block),
        in_specs=[smem, smem, vmem, vmem, vmem, hbm, hbm],
        out_specs=vmem,
        out_shape=jax.ShapeDtypeStruct((bd, n_heads, HEAD_DIM), F32),
        scratch_shapes=[pltpu.VMEM((PAGES_IN_FLIGHT, rows, HEAD_DIM), F32),
                        pltpu.VMEM((PAGES_IN_FLIGHT, rows, HEAD_DIM), F32),
                        pltpu.SemaphoreType.DMA((PAGES_IN_FLIGHT,)), pltpu.SemaphoreType.DMA((PAGES_IN_FLIGHT,)),
                        pltpu.VMEM((1, 1), F32), pltpu.VMEM((1, 1), F32), pltpu.VMEM((1, HEAD_DIM), F32)],
        compiler_params=pltpu.CompilerParams(vmem_limit_bytes=VMEM_LIMIT),
        name="attn_step",
    )(page_table.reshape(-1), sel.reshape(-1), q, k_new, v_new, as_rows(cache_k), as_rows(cache_v))


P_HALO = 16


def _pool_out_kernel(o_ref, u_ref, x_ref, pw_ref, ps_ref, wo_ref, y_ref, newp_ref, uext, mix):
    l = pl.program_id(1)
    tl = u_ref.shape[1]
    wp = u_ref.shape[2]
    wo_attn = o_ref.shape[2]
    pg = wp // len(POOL_WINDOWS)

    @pl.when(l == 0)
    def _():
        uext[0:P_HALO, :] = jnp.zeros((P_HALO, wp), F32)

    @pl.when(l > 0)
    def _():
        uext[0:P_HALO, :] = uext[tl:tl + P_HALO, :]

    uext[P_HALO:P_HALO + tl, :] = u_ref[0]
    mix[:, 0:wo_attn] = o_ref[0].astype(BF16)
    pos = l * tl + lax.broadcasted_iota(jnp.int32, (tl, pg), 0)
    for g, w in enumerate(POOL_WINDOWS):
        c0 = g * pg
        wsum = uext[P_HALO:P_HALO + tl, c0:c0 + pg]
        for j in range(1, w):
            wsum = wsum + uext[P_HALO - j:P_HALO - j + tl, c0:c0 + pg]
        count = jnp.minimum(pos + 1, w).astype(F32)
        dlt = (wsum / count - uext[P_HALO:P_HALO + tl, c0:c0 + pg]).astype(BF16)
        yp = jnp.dot(dlt, pw_ref[g], preferred_element_type=F32) * ps_ref[:, c0:c0 + pg]
        mix[:, wo_attn + c0:wo_attn + c0 + pg] = yp.astype(BF16)

    y_ref[0] = x_ref[0] + jnp.dot(mix[...], wo_ref[...], preferred_element_type=F32)

    @pl.when(l == pl.num_programs(1) - 1)
    def _():
        newp_ref[0] = uext[P_HALO + tl - (POOL_MAX - 1):P_HALO + tl, :]


def pool_out(o, u, x, pool_w, pool_scale, w_out, tl):
    bsz, seq, d = x.shape
    wp = u.shape[-1]
    tl = min(tl, seq)
    assert seq % tl == 0 and tl >= P_HALO
    const = _resident
    return pl.pallas_call(
        _pool_out_kernel,
        grid=(bsz, seq // tl),
        in_specs=[pl.BlockSpec((1, tl, o.shape[-1]), lambda b, l: (b, l, 0)),
                  pl.BlockSpec((1, tl, wp), lambda b, l: (b, l, 0)),
                  pl.BlockSpec((1, tl, d), lambda b, l: (b, l, 0)),
                  const(pool_w.shape), const((1, wp)), const(w_out.shape)],
        out_specs=[pl.BlockSpec((1, tl, d), lambda b, l: (b, l, 0)),
                   pl.BlockSpec((1, POOL_MAX - 1, wp), lambda b, l: (b, 0, 0))],
        out_shape=[jax.ShapeDtypeStruct((bsz, seq, d), F32),
                   jax.ShapeDtypeStruct((bsz, POOL_MAX - 1, wp), F32)],
        scratch_shapes=[pltpu.VMEM((P_HALO + tl, wp), F32), pltpu.VMEM((tl, o.shape[-1] + wp), BF16)],
        compiler_params=_params("parallel", "arbitrary"),
        name="pool_out",
    )(o, u, x, pool_w, pool_scale.reshape(1, wp), w_out)


def _pool_step_kernel(o_ref, u_ref, x_ref, stp_ref, pw_ref, ps_ref, wo_ref, y_ref, *, start_pos):
    wp = u_ref.shape[1]
    pg = wp // len(POOL_WINDOWS)
    u = u_ref[...]
    parts = [o_ref[...].astype(BF16)]
    for g, w in enumerate(POOL_WINDOWS):
        c0 = g * pg
        wsum = u[:, c0:c0 + pg]
        for j in range(1, w):
            wsum = wsum + stp_ref[POOL_MAX - 1 - j][:, c0:c0 + pg]
        count = float(min(start_pos + 1, w))
        dlt = (wsum / count - u[:, c0:c0 + pg]).astype(BF16)
        yp = jnp.dot(dlt, pw_ref[g], preferred_element_type=F32) * ps_ref[:, c0:c0 + pg]
        parts.append(yp.astype(BF16))
    mix = jnp.concatenate(parts, axis=-1)
    y_ref[...] = x_ref[...] + jnp.dot(mix, wo_ref[...], preferred_element_type=F32)


def pool_step(o, u, x, st_p, pool_w, pool_scale, w_out, start_pos):
    bd, d = x.shape
    return pl.pallas_call(
        functools.partial(_pool_step_kernel, start_pos=start_pos),
        out_shape=jax.ShapeDtypeStruct((bd, d), F32),
        compiler_params=pltpu.CompilerParams(vmem_limit_bytes=VMEM_LIMIT),
        name="pool_step",
    )(o, u, x, st_p, pool_w, pool_scale.reshape(1, -1), w_out)


TM_PROMPT = 512
TM_FFN = 1024
TF = 512
TL_MIX = 256


def kernel(x_prompt, x_sample, state_conv_a, state_conv_b, cache_k, cache_v, state_pool, page_table, conv_norm_g, conv_w_in, conv_a_dw, conv_a_dw_b, conv_a_ln_g, conv_a_ln_b, conv_b_dw, conv_w_out, attn_norm_g, attn_w_in, pool_w, pool_scale, attn_w_out, ffn_norm_g, ffn_w_gate, ffn_w_up, ffn_w_down, final_norm_g):
    bsz, seq, d = x_prompt.shape
    bd, dec_seq, _ = x_sample.shape
    assert dec_seq == 1
    depth = ffn_norm_g.shape[0]
    n_heads = cache_k.shape[3]
    wattn = n_heads * HEAD_DIM
    past_len = page_table.shape[1] * PAGE_SIZE

    layer_bf16 = lambda w, n: w[n].astype(BF16)

    xp = x_prompt.reshape(bsz * seq, d)
    xs = x_sample.reshape(bd, d)
    pa, pb, pk, pv, pp = [], [], [], [], []
    sa, sb, sk, sv, sp = [], [], [], [], []
    for layer in range(depth):
        i = layer // 2
        if layer % 2 == 0:
            w_in, w_out = layer_bf16(conv_w_in, i), layer_bf16(conv_w_out, i)
            parts = norm_matmul(xp, conv_norm_g[i], w_in, TM_PROMPT)
            wa = parts[0].shape[-1]
            y, na, nb_ = conv_mix([t.reshape(bsz, seq, wa) for t in parts], xp.reshape(bsz, seq, d),
                                  conv_a_dw[i], conv_a_dw_b[i], conv_a_ln_g[i], conv_a_ln_b[i], conv_b_dw[i],
                                  w_out, TL_MIX)
            xp = y.reshape(bsz * seq, d)
            pa.append(na)
            pb.append(nb_)

            parts = norm_matmul(xs, conv_norm_g[i], w_in, TM_PROMPT)
            st_a = jnp.swapaxes(state_conv_a[i], 0, 1)
            st_b = jnp.swapaxes(state_conv_b[i], 0, 1)
            xs, a_new, c_new = conv_step(parts, xs, st_a, st_b, conv_a_dw[i], conv_a_dw_b[i], conv_a_ln_g[i],
                                         conv_a_ln_b[i], conv_b_dw[i], w_out)
            sa.append(jnp.concatenate([state_conv_a[i][:, 1:], a_new[:, None]], axis=1))
            sb.append(jnp.concatenate([state_conv_b[i][:, 1:], c_new[:, None]], axis=1))
        else:
            w_in, w_out, w_pool = layer_bf16(attn_w_in, i), layer_bf16(attn_w_out, i), layer_bf16(pool_w, i)
            q, k, v, u = norm_matmul(xp, attn_norm_g[i], w_in, TM_PROMPT, head_major=3)
            shp = (n_heads, bsz, seq, HEAD_DIM)
            o, block_sums = moba_prompt(q.reshape(shp), k.reshape(shp), v.reshape(shp), cache_k, i, page_table)
            y, npool = pool_out(o, u.reshape(bsz, seq, -1), xp.reshape(bsz, seq, d), w_pool, pool_scale[i],
                                w_out, TL_MIX)
            xp = y.reshape(bsz * seq, d)
            pk.append(jnp.transpose(k.reshape(shp), (1, 2, 0, 3)))
            pv.append(jnp.transpose(v.reshape(shp), (1, 2, 0, 3)))
            pp.append(npool)

            q, k, v, u = norm_matmul(xs, attn_norm_g[i], w_in, TM_PROMPT, head_major=3)
            heads = lambda t: jnp.swapaxes(t, 0, 1)
            sel = gate_step(block_sums, heads(q))
            o = attn_step(cache_k, cache_v, i, page_table, sel, heads(q), heads(k), heads(v))
            st_p = jnp.swapaxes(state_pool[i], 0, 1)
            xs = pool_step(o.reshape(bd, wattn), u, xs, st_p, w_pool, pool_scale[i], w_out, past_len)
            sk.append(heads(k).reshape(bd, 1, n_heads, HEAD_DIM))
            sv.append(heads(v).reshape(bd, 1, n_heads, HEAD_DIM))
            sp.append(jnp.concatenate([state_pool[i][:, 1:], u[:, None]], axis=1))
        last = layer == depth - 1
        ffn_w = (ffn_w_gate, ffn_w_up, ffn_w_down, layer)
        xp, xs = ffn(xp, xs, ffn_norm_g[layer], *ffn_w, final_norm_g, TM_FFN, TF, last)
    return (xp.reshape(bsz, seq, d), xs.reshape(bd, 1, d), jnp.stack(pa), jnp.stack(sa), jnp.stack(pb),
            jnp.stack(sb), jnp.stack(pk), jnp.stack(sk), jnp.stack(pv), jnp.stack(sv), jnp.stack(pp), jnp.stack(sp))
```

```python
import functools

import jax
import jax.numpy as jnp
from jax import lax
from jax.experimental import pallas as pl
from jax.experimental.pallas import tpu as pltpu

EPS = 1e-6
PAGE_SIZE = 128
MOBA_BLOCK = 256
MOBA_TOPK = 3
HEAD_DIM = 128
POOL_WINDOWS = (2, 4, 8, 16)
POOL_MAX = max(POOL_WINDOWS)
MASKED = -1e30

V7X_VMEM_BYTES = 64 * 1024 * 1024
VMEM_LIMIT = V7X_VMEM_BYTES - 4 * 1024 * 1024
LANES = 128
SUBLANES = 8
COL_GROUP = 1024

BF16 = jnp.bfloat16
F32 = jnp.float32


def _params(*sem):
    return pltpu.CompilerParams(dimension_semantics=sem, vmem_limit_bytes=VMEM_LIMIT)


def _rms(x, g):
    ms = jnp.mean(x * x, axis=-1, keepdims=True)
    return x * lax.rsqrt(ms + EPS) * g


def _sigmoid(x):
    return 1.0 / (1.0 + jnp.exp(-x))


def _resident(shape):
    return pl.BlockSpec(shape, lambda *_: (0,) * len(shape), pipeline_mode=pl.Buffered(1))


def _norm_matmul_kernel(x_ref, g_ref, w_ref, *out_refs, head_major):
    h = _rms(x_ref[...], g_ref[...]).astype(BF16)
    for c, o_ref in enumerate(out_refs):
        r = jnp.dot(h, w_ref[:, c * COL_GROUP:(c + 1) * COL_GROUP], preferred_element_type=F32)
        if c < head_major:
            for hh in range(COL_GROUP // HEAD_DIM):
                o_ref[hh] = r[:, hh * HEAD_DIM:(hh + 1) * HEAD_DIM]
        else:
            o_ref[...] = r


def norm_matmul(x, g, w, tm, head_major=0):
    m, d = x.shape
    n_out = w.shape[1] // COL_GROUP
    tm = min(tm, m)
    heads = COL_GROUP // HEAD_DIM
    flat_spec = pl.BlockSpec((tm, COL_GROUP), lambda i: (i, 0))
    head_spec = pl.BlockSpec((heads, tm, HEAD_DIM), lambda i: (0, i, 0))
    flat_shape = jax.ShapeDtypeStruct((m, COL_GROUP), F32)
    head_shape = jax.ShapeDtypeStruct((heads, m, HEAD_DIM), F32)
    return pl.pallas_call(
        functools.partial(_norm_matmul_kernel, head_major=head_major),
        grid=(m // tm,),
        in_specs=[pl.BlockSpec((tm, d), lambda i: (i, 0)), _resident((1, d)), _resident(w.shape)],
        out_specs=[head_spec] * head_major + [flat_spec] * (n_out - head_major),
        out_shape=[head_shape] * head_major + [flat_shape] * (n_out - head_major),
        compiler_params=_params("parallel"),
        name="norm_matmul",
    )(x, g.reshape(1, d), w)


def _ffn_kernel(g_ref, fg_ref, xs_ref, wg_ref, wu_ref, wd_ref, x_hbm, o_ref, os_ref, xbuf, h_ref, sem, *, final_norm):
    i, f = pl.program_id(0), pl.program_id(1)
    tm = o_ref.shape[0]
    ns = xs_ref.shape[0]
    last_f = pl.num_programs(1) - 1

    def x_copy(tile):
        return pltpu.make_async_copy(x_hbm.at[pl.ds(pl.multiple_of(tile * tm, tm), tm), :], xbuf, sem)

    @pl.when((i == 0) & (f == 0))
    def _():
        x_copy(0).start()
        xs = xs_ref[...]
        h_ref[tm:, :] = jnp.zeros((h_ref.shape[0] - tm, h_ref.shape[1]), h_ref.dtype)
        h_ref[tm:tm + ns, :] = _rms(xs, g_ref[...]).astype(h_ref.dtype)
        os_ref[...] = xs

    @pl.when(f == 0)
    def _():
        x_copy(i).wait()
        x = xbuf[...]
        h_ref[0:tm, :] = _rms(x, g_ref[...]).astype(h_ref.dtype)
        o_ref[...] = x

    @pl.when((f == 1) & (i + 1 < pl.num_programs(0)))
    def _():
        x_copy(i + 1).start()

    h = h_ref[...]
    gate = jnp.dot(h, wg_ref[...].astype(BF16), preferred_element_type=F32)
    up = jnp.dot(h, wu_ref[...].astype(BF16), preferred_element_type=F32)
    act = (gate * _sigmoid(gate) * up).astype(BF16)
    down = jnp.dot(act, wd_ref[...].astype(BF16), preferred_element_type=F32)
    o_ref[...] += down[0:tm]

    @pl.when(i == 0)
    def _():
        os_ref[...] += down[tm:tm + ns]

    if final_norm:
        @pl.when(f == last_f)
        def _():
            o_ref[...] = _rms(o_ref[...], fg_ref[...])

        @pl.when((i == 0) & (f == last_f))
        def _():
            os_ref[...] = _rms(os_ref[...], fg_ref[...])


BF16_ROWS = 16


def ffn(x, xs, g, wg, wu, wd, layer, final_g, tm, tf, final_norm):
    m, d = x.shape
    ns = xs.shape[0]
    dff = wg.shape[2]
    tm = min(tm, m)
    assert m % tm == 0 and dff % tf == 0 and dff // tf >= 2 and ns <= BF16_ROWS and tm % BF16_ROWS == 0
    return pl.pallas_call(
        functools.partial(_ffn_kernel, final_norm=final_norm),
        grid=(m // tm, dff // tf),
        in_specs=[_resident((1, d)), _resident((1, d)), _resident((ns, d)),
                  pl.BlockSpec((None, d, tf), lambda i, f: (layer, 0, f)),
                  pl.BlockSpec((None, d, tf), lambda i, f: (layer, 0, f)),
                  pl.BlockSpec((None, tf, d), lambda i, f: (layer, f, 0)),
                  pl.BlockSpec(memory_space=pl.ANY)],
        out_specs=[pl.BlockSpec((tm, d), lambda i, f: (i, 0)), pl.BlockSpec((ns, d), lambda i, f: (0, 0))],
        out_shape=[jax.ShapeDtypeStruct((m, d), F32), jax.ShapeDtypeStruct((ns, d), F32)],
        scratch_shapes=[pltpu.VMEM((tm, d), F32), pltpu.VMEM((tm + BF16_ROWS, d), BF16),
                        pltpu.SemaphoreType.DMA(())],
        compiler_params=_params("arbitrary", "arbitrary"),
        name="ffn",
    )(g.reshape(1, d), final_g.reshape(1, d), xs, wg, wu, wd, x)


A_HALO = 32
B_HALO = 8


def _layer_norm_silu(x, g, b):
    mu = jnp.mean(x, axis=-1, keepdims=True)
    xc = x - mu
    y = xc * lax.rsqrt(jnp.mean(xc * xc, axis=-1, keepdims=True) + EPS) * g + b
    return y * _sigmoid(y)


def _conv_mix_kernel(av_ref, ag_ref, bh_ref, bb_ref, bc_ref, x_ref, adw_ref, adwb_ref, lng_ref, lnb_ref,
                     bdw_ref, wo_ref, y_ref, newa_ref, newb_ref, aext, cext, ashift, aconv, *, ta, tb, rc):
    l = pl.program_id(1)
    tl = av_ref.shape[1]
    wa = av_ref.shape[2]

    @pl.when(l == 0)
    def _():
        aext[0:A_HALO, :] = jnp.zeros((A_HALO, wa), F32)
        cext[0:B_HALO, :] = jnp.zeros((B_HALO, wa), F32)

    @pl.when(l > 0)
    def _():
        aext[0:A_HALO, :] = aext[tl:tl + A_HALO, :]
        cext[0:B_HALO, :] = cext[tl:tl + B_HALO, :]

    aext[A_HALO:A_HALO + tl, :] = av_ref[0] * _sigmoid(ag_ref[0])
    cext[B_HALO:B_HALO + tl, :] = bc_ref[0] * bh_ref[0]

    bconv = jnp.zeros((tl, wa), F32)
    for k in range(tb):
        off = B_HALO - (tb - 1) + k
        bconv = bconv + bdw_ref[k:k + 1, :] * cext[off:off + tl, :]
    b_out = (bb_ref[0] * bconv).astype(BF16)
    y_ref[0] = x_ref[0] + jnp.dot(b_out, wo_ref[wa:, :], preferred_element_type=F32)

    n_shift = ashift.shape[1]
    for s in range(1, SUBLANES):
        ashift[s - 1] = aext[s:s + n_shift, :]

    for r0 in range(0, tl, rc):
        for c0 in range(0, wa, LANES):
            acc = jnp.broadcast_to(adwb_ref[:, c0:c0 + LANES], (rc, LANES))
            for k in range(ta):
                off = A_HALO - (ta - 1) + r0 + k
                s = off % SUBLANES
                src = aext if s == 0 else ashift.at[s - 1]
                acc = acc + adw_ref[k:k + 1, c0:c0 + LANES] * src[off - s:off - s + rc, c0:c0 + LANES]
            aconv[r0:r0 + rc, c0:c0 + LANES] = acc
    a_out = _layer_norm_silu(aconv[...], lng_ref[...], lnb_ref[...]).astype(BF16)
    y_ref[0] += jnp.dot(a_out, wo_ref[0:wa, :], preferred_element_type=F32)

    @pl.when(l == pl.num_programs(1) - 1)
    def _():
        newa_ref[0] = aext[A_HALO + tl - (ta - 1):A_HALO + tl, :]
        newb_ref[0] = cext[B_HALO + tl - (tb - 1):B_HALO + tl, :]


def conv_mix(parts, x, a_dw, a_dw_b, ln_g, ln_b, b_dw, w_out, tl):
    bsz, seq, d = x.shape
    wa = parts[0].shape[-1]
    ta, tb = a_dw.shape[0], b_dw.shape[0]
    tl = min(tl, seq)
    assert seq % tl == 0 and tl >= A_HALO and ta - 1 <= A_HALO and tb - 1 <= B_HALO
    part_spec = pl.BlockSpec((1, tl, wa), lambda b, l: (b, l, 0))
    const = _resident
    return pl.pallas_call(
        functools.partial(_conv_mix_kernel, ta=ta, tb=tb, rc=min(128, tl)),
        grid=(bsz, seq // tl),
        in_specs=[part_spec] * 5 + [pl.BlockSpec((1, tl, d), lambda b, l: (b, l, 0)),
                                    const((ta, wa)), const((1, wa)), const((1, wa)), const((1, wa)),
                                    const((tb, wa)), const(w_out.shape)],
        out_specs=[pl.BlockSpec((1, tl, d), lambda b, l: (b, l, 0)),
                   pl.BlockSpec((1, ta - 1, wa), lambda b, l: (b, 0, 0)),
                   pl.BlockSpec((1, tb - 1, wa), lambda b, l: (b, 0, 0))],
        out_shape=[jax.ShapeDtypeStruct((bsz, seq, d), F32),
                   jax.ShapeDtypeStruct((bsz, ta - 1, wa), F32),
                   jax.ShapeDtypeStruct((bsz, tb - 1, wa), F32)],
        scratch_shapes=[pltpu.VMEM((A_HALO + tl, wa), F32), pltpu.VMEM((B_HALO + tl, wa), F32),
                        pltpu.VMEM((SUBLANES - 1, A_HALO + tl - SUBLANES, wa), F32),
                        pltpu.VMEM((tl, wa), F32)],
        compiler_params=_params("parallel", "arbitrary"),
        name="conv_mix",
    )(*parts, x, a_dw, a_dw_b.reshape(1, wa), ln_g.reshape(1, wa), ln_b.reshape(1, wa), b_dw, w_out)


def _conv_step_kernel(av_ref, ag_ref, bh_ref, bb_ref, bc_ref, x_ref, sta_ref, stb_ref, adw_ref, adwb_ref,
                      lng_ref, lnb_ref, bdw_ref, wo_ref, y_ref, anew_ref, cnew_ref, *, ta, tb):
    a = av_ref[...] * _sigmoid(ag_ref[...])
    c = bc_ref[...] * bh_ref[...]
    anew_ref[...] = a
    cnew_ref[...] = c
    acc = adwb_ref[...] + adw_ref[ta - 1:ta, :] * a
    for k in range(ta - 1):
        acc = acc + adw_ref[k:k + 1, :] * sta_ref[k]
    a_out = _layer_norm_silu(acc, lng_ref[...], lnb_ref[...])
    bconv = bdw_ref[tb - 1:tb, :] * c
    for k in range(tb - 1):
        bconv = bconv + bdw_ref[k:k + 1, :] * stb_ref[k]
    mix = jnp.concatenate([a_out, bb_ref[...] * bconv], axis=-1).astype(BF16)
    y_ref[...] = x_ref[...] + jnp.dot(mix, wo_ref[...], preferred_element_type=F32)


def conv_step(parts, x, st_a, st_b, a_dw, a_dw_b, ln_g, ln_b, b_dw, w_out):
    bd, d = x.shape
    wa = parts[0].shape[-1]
    ta, tb = a_dw.shape[0], b_dw.shape[0]
    return pl.pallas_call(
        functools.partial(_conv_step_kernel, ta=ta, tb=tb),
        out_shape=[jax.ShapeDtypeStruct((bd, d), F32), jax.ShapeDtypeStruct((bd, wa), F32),
                   jax.ShapeDtypeStruct((bd, wa), F32)],
        compiler_params=pltpu.CompilerParams(vmem_limit_bytes=VMEM_LIMIT),
        name="conv_step",
    )(*parts, x, st_a, st_b, a_dw, a_dw_b.reshape(1, wa), ln_g.reshape(1, wa), ln_b.reshape(1, wa), b_dw, w_out)


LOG2E = 1.4426950408889634
TQ = 2 * MOBA_BLOCK
DA = 2 * HEAD_DIM


def _split3(x):
    def top_bits(v):
        return lax.bitcast_convert_type(lax.bitcast_convert_type(v, jnp.uint32) & jnp.uint32(0xFFFF0000), F32)

    hi = top_bits(x)
    mid = top_bits(x - hi)
    return hi, mid, x - hi - mid


def _moba_setup(q_ref, k_ref, v_ref, qaug, kaug, vaug, kmean, slope2, nb):
    seq = k_ref.shape[1]
    bs = MOBA_BLOCK
    nbp = kmean.shape[0]
    k = k_ref[0]
    q = q_ref[0]
    kmean[...] = jnp.zeros_like(kmean)
    kmean[0:nb, :] = jnp.mean(k.reshape(nb, bs, HEAD_DIM), axis=1)

    gate = lax.dot_general(kmean[...], q, (((1,), (1,)), ((), ())), precision=lax.Precision.HIGHEST,
                           preferred_element_type=F32)
    blk = lax.broadcasted_iota(jnp.int32, (nbp, seq), 0)
    own = lax.broadcasted_iota(jnp.int32, (nbp, seq), 1) // bs
    gate = jnp.where(blk < own, gate, -jnp.inf)
    attend = blk == own
    for _ in range(MOBA_TOPK):
        m = jnp.max(gate, axis=0, keepdims=True)
        idx = jnp.min(jnp.where(gate == m, blk, nbp), axis=0, keepdims=True)
        hit = blk == idx
        attend = attend | (hit & (blk < own))
        gate = jnp.where(hit, -jnp.inf, gate)
    attend_t = jnp.where(attend, 1.0, 0.0)
    attend_r = jnp.concatenate([attend_t, jnp.zeros((LANES - nbp, seq), F32)], axis=0).T

    lane = lax.broadcasted_iota(jnp.int32, (seq, LANES), 1)
    pos_i = lax.broadcasted_iota(jnp.int32, (seq, LANES), 0)
    pos = pos_i.astype(F32)

    khi, kmid, klo = _split3(slope2 * pos)
    ek = jnp.where(lane == pos_i // bs, 1.0, 0.0)
    ek = jnp.where(lane == nb, khi, ek)
    ek = jnp.where(lane == nb + 1, kmid, ek)
    ek = jnp.where(lane == nb + 2, klo, ek)
    ek = jnp.where((lane >= nb + 3) & (lane < nb + 6), 1.0, ek)
    kaug[:, 0:HEAD_DIM] = k.astype(BF16)
    kaug[:, HEAD_DIM:] = ek.astype(BF16)

    qhi, qmid, qlo = _split3(-slope2 * pos)
    eq = jnp.where(lane < nb, jnp.where(attend_r > 0.5, 0.0, MASKED), 0.0)
    eq = jnp.where((lane >= nb) & (lane < nb + 3), 1.0, eq)
    eq = jnp.where(lane == nb + 3, qhi, eq)
    eq = jnp.where(lane == nb + 4, qmid, eq)
    eq = jnp.where(lane == nb + 5, qlo, eq)
    qaug[:, 0:HEAD_DIM] = (q * (HEAD_DIM ** -0.5 * LOG2E)).astype(BF16)
    qaug[:, HEAD_DIM:] = eq.astype(BF16)

    vaug[:, 0:HEAD_DIM] = v_ref[0].astype(BF16)
    vaug[:, HEAD_DIM:] = jnp.ones((seq, HEAD_DIM), BF16)


SUM_CHAINS = 16


def _page_sums_step(step, n_steps, pt_ref, kc_hbm, ksum_ref, ring, sem, *, layer, pages_per_block):
    total = pt_ref.shape[0]
    depth, page_rows, n_heads, _ = ring.shape

    def page_copy(t, slot):
        return pltpu.make_async_copy(kc_hbm.at[layer, pt_ref[t]], ring.at[slot], sem.at[slot])

    @pl.when(step == 0)
    def _():
        ksum_ref[...] = jnp.zeros_like(ksum_ref)
        for slot in range(min(depth, total)):
            page_copy(slot, slot).start()

    def fold(t, slot):
        page_copy(t, slot).wait()
        page = ring[slot].reshape(SUM_CHAINS, page_rows // SUM_CHAINS, n_heads, HEAD_DIM)
        ksum_ref[t // pages_per_block] += jnp.sum(jnp.sum(page, axis=1), axis=0)

    if n_steps * depth == total:
        for slot in range(depth):
            fold(step * depth + slot, slot)

        @pl.when(step + 1 < n_steps)
        def _():
            for slot in range(depth):
                page_copy((step + 1) * depth + slot, slot).start()
    else:
        for slot in range(depth):
            t = step * depth + slot

            @pl.when(t < total)
            def _(t=t, slot=slot):
                fold(t, slot)

                @pl.when(t + depth < total)
                def _():
                    page_copy(t + depth, slot).start()


def _moba_kernel(pt_ref, q_ref, k_ref, v_ref, kc_hbm, o_ref, ksum_ref, qaug, kaug, vaug, kmean, m_ref, acc_ref,
                 s_ref, ring, psem, *, n_heads, nb, layer, n_steps):
    h = pl.program_id(1)
    i = pl.program_id(2)
    bs = MOBA_BLOCK
    nt = (((1,), (1,)), ((), ()))
    step = (pl.program_id(0) * pl.num_programs(1) + h) * pl.num_programs(2) + i
    _page_sums_step(step, n_steps, pt_ref, kc_hbm, ksum_ref, ring, psem, layer=layer,
                    pages_per_block=MOBA_BLOCK // PAGE_SIZE)

    @pl.when(i == 0)
    def _():
        slope2 = jnp.exp2((h + 1).astype(F32) * (-8.0 / n_heads) + jnp.zeros((1, 1), F32)) * LOG2E
        _moba_setup(q_ref, k_ref, v_ref, qaug, kaug, vaug, kmean, slope2, nb)

    r0 = pl.multiple_of(i * TQ, TQ)
    qa = qaug[pl.ds(r0, TQ), :]

    halves = (slice(0, bs), slice(bs, TQ))

    def scores(g):
        kg = kaug[pl.ds(pl.multiple_of(g * TQ, TQ), TQ), :]
        return jnp.concatenate([lax.dot_general(qa[r], kg, nt, preferred_element_type=F32) for r in halves], axis=0)

    def update(s, g):
        vg = vaug[pl.ds(pl.multiple_of(g * TQ, TQ), TQ), :]
        for r in halves:
            m = m_ref[r, :]
            m_new = jnp.maximum(m, jnp.max(s[r], axis=1, keepdims=True))
            p = jnp.exp2(s[r] - m_new).astype(BF16)
            pv = jnp.dot(p, vg, preferred_element_type=F32)
            acc_ref[r, :] = jnp.exp2(m - m_new) * acc_ref[r, :] + pv
            m_ref[r, :] = m_new

    m_ref[...] = jnp.full(m_ref.shape, MASKED, F32)
    acc_ref[...] = jnp.zeros_like(acc_ref)
    s_ref[0] = scores(0)

    def past_group(g, slot):
        s_ref[1 - slot] = scores(g + 1)
        update(s_ref[slot], g)

    @pl.loop(0, i // 2)
    def _(j):
        past_group(2 * j, 0)
        past_group(2 * j + 1, 1)

    def own_group(slot):
        causal = (lax.broadcasted_iota(jnp.int32, (TQ, TQ), 0) >= lax.broadcasted_iota(jnp.int32, (TQ, TQ), 1))
        update(jnp.where(causal, s_ref[slot], MASKED), i)
        o_ref[0] = acc_ref[:, 0:HEAD_DIM] / acc_ref[:, HEAD_DIM:]

    @pl.when(i % 2 == 0)
    def _():
        own_group(0)

    @pl.when(i % 2 == 1)
    def _():
        past_group(i - 1, 0)
        own_group(1)


def moba_prompt(q, k, v, cache_k, layer, page_table):
    n_heads, bsz, seq, _ = q.shape
    width = n_heads * HEAD_DIM
    assert seq % TQ == 0
    nb = seq // MOBA_BLOCK
    nbp = -(-nb // 8) * 8
    assert nb + 6 <= LANES
    bd, n_pages = page_table.shape
    pages_per_block = MOBA_BLOCK // PAGE_SIZE
    assert n_pages % pages_per_block == 0
    grid = (bsz, n_heads, seq // TQ)
    pages_per_step = -(-bd * n_pages // (grid[0] * grid[1] * grid[2]))
    sums_shape = (bd * n_pages // pages_per_block, n_heads, HEAD_DIM)
    full = pl.BlockSpec((None, 1, seq, HEAD_DIM), lambda b, h, i: (h, b, 0, 0))
    tile = pl.BlockSpec((1, TQ, HEAD_DIM), lambda b, h, i: (b, i, h))
    return pl.pallas_call(
        functools.partial(_moba_kernel, n_heads=n_heads, nb=nb, layer=layer, n_steps=grid[0] * grid[1] * grid[2]),
        grid=grid,
        in_specs=[pl.BlockSpec(memory_space=pltpu.SMEM), full, full, full, pl.BlockSpec(memory_space=pl.ANY)],
        out_specs=[tile, pl.BlockSpec(sums_shape, lambda b, h, i: (0, 0, 0))],
        out_shape=[jax.ShapeDtypeStruct((bsz, seq, width), F32), jax.ShapeDtypeStruct(sums_shape, F32)],
        scratch_shapes=[pltpu.VMEM((seq, DA), BF16), pltpu.VMEM((seq, DA), BF16), pltpu.VMEM((seq, DA), BF16),
                        pltpu.VMEM((nbp, HEAD_DIM), F32), pltpu.VMEM((TQ, 1), F32), pltpu.VMEM((TQ, DA), F32),
                        pltpu.VMEM((2, TQ, TQ), F32),
                        pltpu.VMEM((pages_per_step, PAGE_SIZE, n_heads, HEAD_DIM), F32),
                        pltpu.SemaphoreType.DMA((pages_per_step,))],
        compiler_params=_params("arbitrary", "arbitrary", "arbitrary"),
        name="moba_prompt",
    )(page_table.reshape(-1), q, k, v, cache_k)


PAGES_IN_FLIGHT = 8


def _gate_step_kernel(q_ref, ksum_ref, sel_ref, *, n_heads):
    bd = q_ref.shape[0]
    nblk = ksum_ref.shape[0] // bd
    sel_ref[...] = jnp.zeros_like(sel_ref)
    for b in range(bd):
        kmean = ksum_ref[b * nblk:(b + 1) * nblk] * (1.0 / MOBA_BLOCK)
        gate = jnp.sum(kmean * q_ref[b][None], axis=-1)
        blk = lax.broadcasted_iota(jnp.int32, gate.shape, 0)
        for r in range(MOBA_TOPK):
            m = jnp.max(gate, axis=0, keepdims=True)
            idx = jnp.min(jnp.where(gate == m, blk, nblk), axis=0, keepdims=True)
            sel_ref[b, r:r + 1, 0:n_heads] = idx
            gate = jnp.where(blk == idx, -jnp.inf, gate)


SEL_ROWS = 8


def gate_step(block_sums, q):
    bd, n_heads, _ = q.shape
    assert MOBA_TOPK <= SEL_ROWS and n_heads <= LANES
    sel = pl.pallas_call(
        functools.partial(_gate_step_kernel, n_heads=n_heads),
        out_shape=jax.ShapeDtypeStruct((bd, SEL_ROWS, LANES), jnp.int32),
        compiler_params=pltpu.CompilerParams(vmem_limit_bytes=VMEM_LIMIT),
        name="gate_step",
    )(q, block_sums)
    return sel[:, :MOBA_TOPK, :n_heads]


def _attn_step_kernel(pt_ref, sel_ref, q_ref, kn_ref, vn_ref, k_hbm, v_hbm, o_ref, kbuf, vbuf, ksem, vsem,
                      m_ref, l_ref, acc_ref, *, layer, n_pages, q_pos, pages_per_block):
    bd, n_heads, _ = q_ref.shape
    per_head = MOBA_TOPK * pages_per_block
    total = bd * n_heads * per_head
    depth = kbuf.shape[0]
    scale = HEAD_DIM ** -0.5
    own = q_pos // MOBA_BLOCK

    def coords(t):
        bh, j = t // per_head, t % per_head
        b, h = bh // n_heads, bh % n_heads
        return b, h, j, sel_ref[(b * MOBA_TOPK + j // pages_per_block) * n_heads + h]

    def page_copies(t, slot):
        b, _, j, n = coords(t)
        page = pt_ref[b * n_pages + n * pages_per_block + j % pages_per_block]
        return (pltpu.make_async_copy(k_hbm.at[layer, page], kbuf.at[slot], ksem.at[slot]),
                pltpu.make_async_copy(v_hbm.at[layer, page], vbuf.at[slot], vsem.at[slot]))

    for t in range(min(depth, total)):
        for c in page_copies(t, t):
            c.start()

    @pl.loop(0, total)
    def _(t):
        slot = t % depth
        b, h, j, n = coords(t)
        for c in page_copies(t, slot):
            c.wait()
        q = q_ref[b, pl.ds(h, 1), :]

        @pl.when(j == 0)
        def _():
            m_ref[...] = jnp.sum(q * kn_ref[b, pl.ds(h, 1), :], axis=1, keepdims=True) * scale
            l_ref[...] = jnp.ones_like(l_ref)
            acc_ref[...] = vn_ref[b, pl.ds(h, 1), :]

        k = kbuf[slot, pl.ds(h, PAGE_SIZE, stride=n_heads), :]
        v = vbuf[slot, pl.ds(h, PAGE_SIZE, stride=n_heads), :]
        slope = jnp.exp2(jnp.asarray(h + 1, F32) * (-8.0 / n_heads) + jnp.zeros((1, 1), F32))
        kpos = (n * MOBA_BLOCK + (j % pages_per_block) * PAGE_SIZE
                + lax.broadcasted_iota(jnp.int32, (PAGE_SIZE, 1), 0))
        dist = (q_pos - kpos).astype(F32)
        s = jnp.sum(k * q, axis=1, keepdims=True) * scale - slope * dist
        s = jnp.where((dist >= 0.0) & (n < own), s, MASKED)
        m = m_ref[...]
        m_new = jnp.maximum(m, jnp.max(s, axis=0, keepdims=True))
        alpha = jnp.exp(m - m_new)
        p = jnp.exp(s - m_new)
        l_ref[...] = alpha * l_ref[...] + jnp.sum(p, axis=0, keepdims=True)
        acc_ref[...] = alpha * acc_ref[...] + jnp.sum(p * v, axis=0, keepdims=True)
        m_ref[...] = m_new

        @pl.when(j == per_head - 1)
        def _():
            o_ref[b, pl.ds(h, 1), :] = acc_ref[...] / l_ref[...]

        @pl.when(t + depth < total)
        def _():
            for c in page_copies(t + depth, slot):
                c.start()


def attn_step(cache_k, cache_v, layer, page_table, sel, q, k_new, v_new):
    bd, n_pages = page_table.shape
    n_heads = q.shape[1]
    pages_per_block = MOBA_BLOCK // PAGE_SIZE
    assert n_pages % pages_per_block == 0
    rows = PAGE_SIZE * n_heads
    as_rows = lambda c: c.reshape(c.shape[0], c.shape[1], rows, HEAD_DIM)
    vmem, smem = pl.BlockSpec(memory_space=pltpu.VMEM), pl.BlockSpec(memory_space=pltpu.SMEM)
    hbm = pl.BlockSpec(memory_space=pl.ANY)
    return pl.pallas_call(
        functools.partial(_attn_step_kernel, layer=layer, n_pages=n_pages, q_pos=n_pages * PAGE_SIZE,
                          pages_per_block=pages_per_block),
        in_specs=[smem, smem, vmem, vmem, vmem, hbm, hbm],
        out_specs=vmem,
        out_shape=jax.ShapeDtypeStruct((bd, n_heads, HEAD_DIM), F32),
        scratch_shapes=[pltpu.VMEM((PAGES_IN_FLIGHT, rows, HEAD_DIM), F32),
                        pltpu.VMEM((PAGES_IN_FLIGHT, rows, HEAD_DIM), F32),
                        pltpu.SemaphoreType.DMA((PAGES_IN_FLIGHT,)), pltpu.SemaphoreType.DMA((PAGES_IN_FLIGHT,)),
                        pltpu.VMEM((1, 1), F32), pltpu.VMEM((1, 1), F32), pltpu.VMEM((1, HEAD_DIM), F32)],
        compiler_params=pltpu.CompilerParams(vmem_limit_bytes=VMEM_LIMIT),
        name="attn_step",
    )(page_table.reshape(-1), sel.reshape(-1), q, k_new, v_new, as_rows(cache_k), as_rows(cache_v))


P_HALO = 16


def _pool_out_kernel(o_ref, u_ref, x_ref, pw_ref, ps_ref, wo_ref, y_ref, newp_ref, uext, mix):
    l = pl.program_id(1)
    tl = u_ref.shape[1]
    wp = u_ref.shape[2]
    wo_attn = o_ref.shape[2]
    pg = wp // len(POOL_WINDOWS)

    @pl.when(l == 0)
    def _():
        uext[0:P_HALO, :] = jnp.zeros((P_HALO, wp), F32)

    @pl.when(l > 0)
    def _():
        uext[0:P_HALO, :] = uext[tl:tl + P_HALO, :]

    uext[P_HALO:P_HALO + tl, :] = u_ref[0]
    o_bf16 = o_ref[0].astype(BF16)
    n_chunk = y_ref.shape[2] // len(POOL_WINDOWS)
    pos = l * tl + lax.broadcasted_iota(jnp.int32, (tl, pg), 0)
    for g, w in enumerate(POOL_WINDOWS):
        n0 = g * n_chunk
        y_ref[0, :, n0:n0 + n_chunk] = x_ref[0, :, n0:n0 + n_chunk] + jnp.dot(
            o_bf16, wo_ref[0:wo_attn, n0:n0 + n_chunk], preferred_element_type=F32)
        c0 = g * pg
        wsum = uext[P_HALO:P_HALO + tl, c0:c0 + pg]
        for j in range(1, w):
            wsum = wsum + uext[P_HALO - j:P_HALO - j + tl, c0:c0 + pg]
        count = jnp.minimum(pos + 1, w).astype(F32)
        dlt = (wsum / count - uext[P_HALO:P_HALO + tl, c0:c0 + pg]).astype(BF16)
        yp = jnp.dot(dlt, pw_ref[g], preferred_element_type=F32) * ps_ref[:, c0:c0 + pg]
        mix[:, c0:c0 + pg] = yp.astype(BF16)

    y_ref[0] += jnp.dot(mix[...], wo_ref[wo_attn:, :], preferred_element_type=F32)

    @pl.when(l == pl.num_programs(1) - 1)
    def _():
        newp_ref[0] = uext[P_HALO + tl - (POOL_MAX - 1):P_HALO + tl, :]


def pool_out(o, u, x, pool_w, pool_scale, w_out, tl):
    bsz, seq, d = x.shape
    wp = u.shape[-1]
    tl = min(tl, seq)
    assert seq % tl == 0 and tl >= P_HALO
    const = _resident
    return pl.pallas_call(
        _pool_out_kernel,
        grid=(bsz, seq // tl),
        in_specs=[pl.BlockSpec((1, tl, o.shape[-1]), lambda b, l: (b, l, 0)),
                  pl.BlockSpec((1, tl, wp), lambda b, l: (b, l, 0)),
                  pl.BlockSpec((1, tl, d), lambda b, l: (b, l, 0)),
                  const(pool_w.shape), const((1, wp)), const(w_out.shape)],
        out_specs=[pl.BlockSpec((1, tl, d), lambda b, l: (b, l, 0)),
                   pl.BlockSpec((1, POOL_MAX - 1, wp), lambda b, l: (b, 0, 0))],
        out_shape=[jax.ShapeDtypeStruct((bsz, seq, d), F32),
                   jax.ShapeDtypeStruct((bsz, POOL_MAX - 1, wp), F32)],
        scratch_shapes=[pltpu.VMEM((P_HALO + tl, wp), F32), pltpu.VMEM((tl, wp), BF16)],
        compiler_params=_params("parallel", "arbitrary"),
        name="pool_out",
    )(o, u, x, pool_w, pool_scale.reshape(1, wp), w_out)


def _pool_step_kernel(o_ref, u_ref, x_ref, stp_ref, pw_ref, ps_ref, wo_ref, y_ref, *, start_pos):
    wp = u_ref.shape[1]
    pg = wp // len(POOL_WINDOWS)
    u = u_ref[...]
    parts = [o_ref[...].astype(BF16)]
    for g, w in enumerate(POOL_WINDOWS):
        c0 = g * pg
        wsum = u[:, c0:c0 + pg]
        for j in range(1, w):
            wsum = wsum + stp_ref[POOL_MAX - 1 - j][:, c0:c0 + pg]
        count = float(min(start_pos + 1, w))
        dlt = (wsum / count - u[:, c0:c0 + pg]).astype(BF16)
        yp = jnp.dot(dlt, pw_ref[g], preferred_element_type=F32) * ps_ref[:, c0:c0 + pg]
        parts.append(yp.astype(BF16))
    mix = jnp.concatenate(parts, axis=-1)
    y_ref[...] = x_ref[...] + jnp.dot(mix, wo_ref[...], preferred_element_type=F32)


def pool_step(o, u, x, st_p, pool_w, pool_scale, w_out, start_pos):
    bd, d = x.shape
    return pl.pallas_call(
        functools.partial(_pool_step_kernel, start_pos=start_pos),
        out_shape=jax.ShapeDtypeStruct((bd, d), F32),
        compiler_params=pltpu.CompilerParams(vmem_limit_bytes=VMEM_LIMIT),
        name="pool_step",
    )(o, u, x, st_p, pool_w, pool_scale.reshape(1, -1), w_out)


TM_PROMPT = 512
TM_FFN = 1024
TF = 512
TL_MIX = 256


def kernel(x_prompt, x_sample, state_conv_a, state_conv_b, cache_k, cache_v, state_pool, page_table, conv_norm_g, conv_w_in, conv_a_dw, conv_a_dw_b, conv_a_ln_g, conv_a_ln_b, conv_b_dw, conv_w_out, attn_norm_g, attn_w_in, pool_w, pool_scale, attn_w_out, ffn_norm_g, ffn_w_gate, ffn_w_up, ffn_w_down, final_norm_g):
    bsz, seq, d = x_prompt.shape
    bd, dec_seq, _ = x_sample.shape
    assert dec_seq == 1
    depth = ffn_norm_g.shape[0]
    n_heads = cache_k.shape[3]
    wattn = n_heads * HEAD_DIM
    past_len = page_table.shape[1] * PAGE_SIZE

    layer_bf16 = lambda w, n: w[n].astype(BF16)

    xp = x_prompt.reshape(bsz * seq, d)
    xs = x_sample.reshape(bd, d)
    pa, pb, pk, pv, pp = [], [], [], [], []
    sa, sb, sk, sv, sp = [], [], [], [], []
    for layer in range(depth):
        i = layer // 2
        if layer % 2 == 0:
            w_in, w_out = layer_bf16(conv_w_in, i), layer_bf16(conv_w_out, i)
            parts = norm_matmul(xp, conv_norm_g[i], w_in, TM_PROMPT)
            wa = parts[0].shape[-1]
            y, na, nb_ = conv_mix([t.reshape(bsz, seq, wa) for t in parts], xp.reshape(bsz, seq, d),
                                  conv_a_dw[i], conv_a_dw_b[i], conv_a_ln_g[i], conv_a_ln_b[i], conv_b_dw[i],
                                  w_out, TL_MIX)
            xp = y.reshape(bsz * seq, d)
            pa.append(na)
            pb.append(nb_)

            parts = norm_matmul(xs, conv_norm_g[i], w_in, TM_PROMPT)
            st_a = jnp.swapaxes(state_conv_a[i], 0, 1)
            st_b = jnp.swapaxes(state_conv_b[i], 0, 1)
            xs, a_new, c_new = conv_step(parts, xs, st_a, st_b, conv_a_dw[i], conv_a_dw_b[i], conv_a_ln_g[i],
                                         conv_a_ln_b[i], conv_b_dw[i], w_out)
            sa.append(jnp.concatenate([state_conv_a[i][:, 1:], a_new[:, None]], axis=1))
            sb.append(jnp.concatenate([state_conv_b[i][:, 1:], c_new[:, None]], axis=1))
        else:
            w_in, w_out, w_pool = layer_bf16(attn_w_in, i), layer_bf16(attn_w_out, i), layer_bf16(pool_w, i)
            q, k, v, u = norm_matmul(xp, attn_norm_g[i], w_in, TM_PROMPT, head_major=3)
            shp = (n_heads, bsz, seq, HEAD_DIM)
            o, block_sums = moba_prompt(q.reshape(shp), k.reshape(shp), v.reshape(shp), cache_k, i, page_table)
            y, npool = pool_out(o, u.reshape(bsz, seq, -1), xp.reshape(bsz, seq, d), w_pool, pool_scale[i],
                                w_out, TL_MIX)
            xp = y.reshape(bsz * seq, d)
            pk.append(jnp.transpose(k.reshape(shp), (1, 2, 0, 3)))
            pv.append(jnp.transpose(v.reshape(shp), (1, 2, 0, 3)))
            pp.append(npool)

            q, k, v, u = norm_matmul(xs, attn_norm_g[i], w_in, TM_PROMPT, head_major=3)
            heads = lambda t: jnp.swapaxes(t, 0, 1)
            sel = gate_step(block_sums, heads(q))
            o = attn_step(cache_k, cache_v, i, page_table, sel, heads(q), heads(k), heads(v))
            st_p = jnp.swapaxes(state_pool[i], 0, 1)
            xs = pool_step(o.reshape(bd, wattn), u, xs, st_p, w_pool, pool_scale[i], w_out, past_len)
            sk.append(heads(k).reshape(bd, 1, n_heads, HEAD_DIM))
            sv.append(heads(v).reshape(bd, 1, n_heads, HEAD_DIM))
            sp.append(jnp.concatenate([state_pool[i][:, 1:], u[:, None]], axis=1))
        last = layer == depth - 1
        ffn_w = (ffn_w_gate, ffn_w_up, ffn_w_down, layer)
        xp, xs = ffn(xp, xs, ffn_norm_g[layer], *ffn_w, final_norm_g, TM_FFN, TF, last)
    return (xp.reshape(bsz, seq, d), xs.reshape(bd, 1, d), jnp.stack(pa), jnp.stack(sa), jnp.stack(pb),
            jnp.stack(sb), jnp.stack(pk), jnp.stack(sk), jnp.stack(pv), jnp.stack(sv), jnp.stack(pp), jnp.stack(sp))
```

```python
import functools

import jax
import jax.numpy as jnp
from jax import lax
from jax.experimental import pallas as pl
from jax.experimental.pallas import tpu as pltpu

EPS = 1e-6
PAGE_SIZE = 128
MOBA_BLOCK = 256
MOBA_TOPK = 3
HEAD_DIM = 128
POOL_WINDOWS = (2, 4, 8, 16)
POOL_MAX = max(POOL_WINDOWS)
MASKED = -1e30

V7X_VMEM_BYTES = 64 * 1024 * 1024
VMEM_LIMIT = V7X_VMEM_BYTES - 4 * 1024 * 1024
LANES = 128
SUBLANES = 8
COL_GROUP = 1024

BF16 = jnp.bfloat16
F32 = jnp.float32


def _params(*sem):
    return pltpu.CompilerParams(dimension_semantics=sem, vmem_limit_bytes=VMEM_LIMIT)


def _rms(x, g):
    ms = jnp.mean(x * x, axis=-1, keepdims=True)
    return x * lax.rsqrt(ms + EPS) * g


def _sigmoid(x):
    return 1.0 / (1.0 + jnp.exp(-x))


def _resident(shape):
    return pl.BlockSpec(shape, lambda *_: (0,) * len(shape), pipeline_mode=pl.Buffered(1))


def _norm_matmul_kernel(x_ref, g_ref, w_ref, *out_refs, head_major):
    h = _rms(x_ref[...], g_ref[...]).astype(BF16)
    for c, o_ref in enumerate(out_refs):
        r = jnp.dot(h, w_ref[:, c * COL_GROUP:(c + 1) * COL_GROUP], preferred_element_type=F32)
        if c < head_major:
            for hh in range(COL_GROUP // HEAD_DIM):
                o_ref[hh] = r[:, hh * HEAD_DIM:(hh + 1) * HEAD_DIM]
        else:
            o_ref[...] = r


def norm_matmul(x, g, w, tm, head_major=0):
    m, d = x.shape
    n_out = w.shape[1] // COL_GROUP
    tm = min(tm, m)
    heads = COL_GROUP // HEAD_DIM
    flat_spec = pl.BlockSpec((tm, COL_GROUP), lambda i: (i, 0))
    head_spec = pl.BlockSpec((heads, tm, HEAD_DIM), lambda i: (0, i, 0))
    flat_shape = jax.ShapeDtypeStruct((m, COL_GROUP), F32)
    head_shape = jax.ShapeDtypeStruct((heads, m, HEAD_DIM), F32)
    return pl.pallas_call(
        functools.partial(_norm_matmul_kernel, head_major=head_major),
        grid=(m // tm,),
        in_specs=[pl.BlockSpec((tm, d), lambda i: (i, 0)), _resident((1, d)), _resident(w.shape)],
        out_specs=[head_spec] * head_major + [flat_spec] * (n_out - head_major),
        out_shape=[head_shape] * head_major + [flat_shape] * (n_out - head_major),
        compiler_params=_params("parallel"),
        name="norm_matmul",
    )(x, g.reshape(1, d), w)


def _ffn_kernel(g_ref, fg_ref, xs_ref, wg_ref, wu_ref, wd_ref, x_hbm, o_ref, os_ref, xbuf, h_ref, sem, *, final_norm):
    i, f = pl.program_id(0), pl.program_id(1)
    tm = o_ref.shape[0]
    ns = xs_ref.shape[0]
    last_f = pl.num_programs(1) - 1

    def x_copy(tile):
        return pltpu.make_async_copy(x_hbm.at[pl.ds(pl.multiple_of(tile * tm, tm), tm), :], xbuf, sem)

    @pl.when((i == 0) & (f == 0))
    def _():
        x_copy(0).start()
        xs = xs_ref[...]
        h_ref[tm:, :] = jnp.zeros((h_ref.shape[0] - tm, h_ref.shape[1]), h_ref.dtype)
        h_ref[tm:tm + ns, :] = _rms(xs, g_ref[...]).astype(h_ref.dtype)
        os_ref[...] = xs

    @pl.when(f == 0)
    def _():
        x_copy(i).wait()
        x = xbuf[...]
        h_ref[0:tm, :] = _rms(x, g_ref[...]).astype(h_ref.dtype)
        o_ref[...] = x

    @pl.when((f == 1) & (i + 1 < pl.num_programs(0)))
    def _():
        x_copy(i + 1).start()

    h = h_ref[...]
    gate = jnp.dot(h, wg_ref[...].astype(BF16), preferred_element_type=F32)
    up = jnp.dot(h, wu_ref[...].astype(BF16), preferred_element_type=F32)
    act = (gate * _sigmoid(gate) * up).astype(BF16)
    down = jnp.dot(act, wd_ref[...].astype(BF16), preferred_element_type=F32)
    o_ref[...] += down[0:tm]

    @pl.when(i == 0)
    def _():
        os_ref[...] += down[tm:tm + ns]

    if final_norm:
        @pl.when(f == last_f)
        def _():
            o_ref[...] = _rms(o_ref[...], fg_ref[...])

        @pl.when((i == 0) & (f == last_f))
        def _():
            os_ref[...] = _rms(os_ref[...], fg_ref[...])


BF16_ROWS = 16


def ffn(x, xs, g, wg, wu, wd, layer, final_g, tm, tf, final_norm):
    m, d = x.shape
    ns = xs.shape[0]
    dff = wg.shape[2]
    tm = min(tm, m)
    assert m % tm == 0 and dff % tf == 0 and dff // tf >= 2 and ns <= BF16_ROWS and tm % BF16_ROWS == 0
    return pl.pallas_call(
        functools.partial(_ffn_kernel, final_norm=final_norm),
        grid=(m // tm, dff // tf),
        in_specs=[_resident((1, d)), _resident((1, d)), _resident((ns, d)),
                  pl.BlockSpec((None, d, tf), lambda i, f: (layer, 0, f)),
                  pl.BlockSpec((None, d, tf), lambda i, f: (layer, 0, f)),
                  pl.BlockSpec((None, tf, d), lambda i, f: (layer, f, 0)),
                  pl.BlockSpec(memory_space=pl.ANY)],
        out_specs=[pl.BlockSpec((tm, d), lambda i, f: (i, 0)), pl.BlockSpec((ns, d), lambda i, f: (0, 0))],
        out_shape=[jax.ShapeDtypeStruct((m, d), F32), jax.ShapeDtypeStruct((ns, d), F32)],
        scratch_shapes=[pltpu.VMEM((tm, d), F32), pltpu.VMEM((tm + BF16_ROWS, d), BF16),
                        pltpu.SemaphoreType.DMA(())],
        compiler_params=_params("arbitrary", "arbitrary"),
        name="ffn",
    )(g.reshape(1, d), final_g.reshape(1, d), xs, wg, wu, wd, x)


A_HALO = 32
B_HALO = 8


def _layer_norm_silu(x, g, b):
    mu = jnp.mean(x, axis=-1, keepdims=True)
    xc = x - mu
    y = xc * lax.rsqrt(jnp.mean(xc * xc, axis=-1, keepdims=True) + EPS) * g + b
    return y * _sigmoid(y)


def _conv_mix_kernel(av_ref, ag_ref, bh_ref, bb_ref, bc_ref, x_ref, adw_ref, adwb_ref, lng_ref, lnb_ref,
                     bdw_ref, wo_ref, y_ref, newa_ref, newb_ref, aext, cext, ashift, aconv, *, ta, tb, rc):
    l = pl.program_id(1)
    tl = av_ref.shape[1]
    wa = av_ref.shape[2]

    @pl.when(l == 0)
    def _():
        aext[0:A_HALO, :] = jnp.zeros((A_HALO, wa), F32)
        cext[0:B_HALO, :] = jnp.zeros((B_HALO, wa), F32)

    @pl.when(l > 0)
    def _():
        aext[0:A_HALO, :] = aext[tl:tl + A_HALO, :]
        cext[0:B_HALO, :] = cext[tl:tl + B_HALO, :]

    aext[A_HALO:A_HALO + tl, :] = av_ref[0] * _sigmoid(ag_ref[0])
    cext[B_HALO:B_HALO + tl, :] = bc_ref[0] * bh_ref[0]

    bconv = jnp.zeros((tl, wa), F32)
    for k in range(tb):
        off = B_HALO - (tb - 1) + k
        bconv = bconv + bdw_ref[k:k + 1, :] * cext[off:off + tl, :]
    b_out = (bb_ref[0] * bconv).astype(BF16)
    y_ref[0] = x_ref[0] + jnp.dot(b_out, wo_ref[wa:, :], preferred_element_type=F32)

    n_shift = ashift.shape[1]
    for s in range(1, SUBLANES):
        ashift[s - 1] = aext[s:s + n_shift, :]

    for r0 in range(0, tl, rc):
        for c0 in range(0, wa, LANES):
            acc = jnp.broadcast_to(adwb_ref[:, c0:c0 + LANES], (rc, LANES))
            for k in range(ta):
                off = A_HALO - (ta - 1) + r0 + k
                s = off % SUBLANES
                src = aext if s == 0 else ashift.at[s - 1]
                acc = acc + adw_ref[k:k + 1, c0:c0 + LANES] * src[off - s:off - s + rc, c0:c0 + LANES]
            aconv[r0:r0 + rc, c0:c0 + LANES] = acc
    a_out = _layer_norm_silu(aconv[...], lng_ref[...], lnb_ref[...]).astype(BF16)
    y_ref[0] += jnp.dot(a_out, wo_ref[0:wa, :], preferred_element_type=F32)

    @pl.when(l == pl.num_programs(1) - 1)
    def _():
        newa_ref[0] = aext[A_HALO + tl - (ta - 1):A_HALO + tl, :]
        newb_ref[0] = cext[B_HALO + tl - (tb - 1):B_HALO + tl, :]


def conv_mix(parts, x, a_dw, a_dw_b, ln_g, ln_b, b_dw, w_out, tl):
    bsz, seq, d = x.shape
    wa = parts[0].shape[-1]
    ta, tb = a_dw.shape[0], b_dw.shape[0]
    tl = min(tl, seq)
    assert seq % tl == 0 and tl >= A_HALO and ta - 1 <= A_HALO and tb - 1 <= B_HALO
    part_spec = pl.BlockSpec((1, tl, wa), lambda b, l: (b, l, 0))
    const = _resident
    return pl.pallas_call(
        functools.partial(_conv_mix_kernel, ta=ta, tb=tb, rc=min(128, tl)),
        grid=(bsz, seq // tl),
        in_specs=[part_spec] * 5 + [pl.BlockSpec((1, tl, d), lambda b, l: (b, l, 0)),
                                    const((ta, wa)), const((1, wa)), const((1, wa)), const((1, wa)),
                                    const((tb, wa)), const(w_out.shape)],
        out_specs=[pl.BlockSpec((1, tl, d), lambda b, l: (b, l, 0)),
                   pl.BlockSpec((1, ta - 1, wa), lambda b, l: (b, 0, 0)),
                   pl.BlockSpec((1, tb - 1, wa), lambda b, l: (b, 0, 0))],
        out_shape=[jax.ShapeDtypeStruct((bsz, seq, d), F32),
                   jax.ShapeDtypeStruct((bsz, ta - 1, wa), F32),
                   jax.ShapeDtypeStruct((bsz, tb - 1, wa), F32)],
        scratch_shapes=[pltpu.VMEM((A_HALO + tl, wa), F32), pltpu.VMEM((B_HALO + tl, wa), F32),
                        pltpu.VMEM((SUBLANES - 1, A_HALO + tl - SUBLANES, wa), F32),
                        pltpu.VMEM((tl, wa), F32)],
        compiler_params=_params("parallel", "arbitrary"),
        name="conv_mix",
    )(*parts, x, a_dw, a_dw_b.reshape(1, wa), ln_g.reshape(1, wa), ln_b.reshape(1, wa), b_dw, w_out)


def _conv_step_kernel(av_ref, ag_ref, bh_ref, bb_ref, bc_ref, x_ref, sta_ref, stb_ref, adw_ref, adwb_ref,
                      lng_ref, lnb_ref, bdw_ref, wo_ref, y_ref, anew_ref, cnew_ref, *, ta, tb):
    a = av_ref[...] * _sigmoid(ag_ref[...])
    c = bc_ref[...] * bh_ref[...]
    anew_ref[...] = a
    cnew_ref[...] = c
    acc = adwb_ref[...] + adw_ref[ta - 1:ta, :] * a
    for k in range(ta - 1):
        acc = acc + adw_ref[k:k + 1, :] * sta_ref[k]
    a_out = _layer_norm_silu(acc, lng_ref[...], lnb_ref[...])
    bconv = bdw_ref[tb - 1:tb, :] * c
    for k in range(tb - 1):
        bconv = bconv + bdw_ref[k:k + 1, :] * stb_ref[k]
    mix = jnp.concatenate([a_out, bb_ref[...] * bconv], axis=-1).astype(BF16)
    y_ref[...] = x_ref[...] + jnp.dot(mix, wo_ref[...], preferred_element_type=F32)


def conv_step(parts, x, st_a, st_b, a_dw, a_dw_b, ln_g, ln_b, b_dw, w_out):
    bd, d = x.shape
    wa = parts[0].shape[-1]
    ta, tb = a_dw.shape[0], b_dw.shape[0]
    return pl.pallas_call(
        functools.partial(_conv_step_kernel, ta=ta, tb=tb),
        out_shape=[jax.ShapeDtypeStruct((bd, d), F32), jax.ShapeDtypeStruct((bd, wa), F32),
                   jax.ShapeDtypeStruct((bd, wa), F32)],
        compiler_params=pltpu.CompilerParams(vmem_limit_bytes=VMEM_LIMIT),
        name="conv_step",
    )(*parts, x, st_a, st_b, a_dw, a_dw_b.reshape(1, wa), ln_g.reshape(1, wa), ln_b.reshape(1, wa), b_dw, w_out)


LOG2E = 1.4426950408889634
TQ = 2 * MOBA_BLOCK
DA = 2 * HEAD_DIM


def _split3(x):
    def top_bits(v):
        return lax.bitcast_convert_type(lax.bitcast_convert_type(v, jnp.uint32) & jnp.uint32(0xFFFF0000), F32)

    hi = top_bits(x)
    mid = top_bits(x - hi)
    return hi, mid, x - hi - mid


def _moba_setup(q_ref, k_ref, v_ref, qaug, kaug, vaug, kmean, slope2, nb):
    seq = k_ref.shape[1]
    bs = MOBA_BLOCK
    nbp = kmean.shape[0]
    k = k_ref[0]
    q = q_ref[0]
    kmean[...] = jnp.zeros_like(kmean)
    kmean[0:nb, :] = jnp.mean(k.reshape(nb, bs, HEAD_DIM), axis=1)

    gate = lax.dot_general(kmean[...], q, (((1,), (1,)), ((), ())), precision=lax.Precision.HIGHEST,
                           preferred_element_type=F32)
    blk = lax.broadcasted_iota(jnp.int32, (nbp, seq), 0)
    own = lax.broadcasted_iota(jnp.int32, (nbp, seq), 1) // bs
    gate = jnp.where(blk < own, gate, -jnp.inf)
    attend = blk == own
    for _ in range(MOBA_TOPK):
        m = jnp.max(gate, axis=0, keepdims=True)
        idx = jnp.min(jnp.where(gate == m, blk, nbp), axis=0, keepdims=True)
        hit = blk == idx
        attend = attend | (hit & (blk < own))
        gate = jnp.where(hit, -jnp.inf, gate)
    attend_t = jnp.where(attend, 1.0, 0.0)
    attend_r = jnp.concatenate([attend_t, jnp.zeros((LANES - nbp, seq), F32)], axis=0).T

    lane = lax.broadcasted_iota(jnp.int32, (seq, LANES), 1)
    pos_i = lax.broadcasted_iota(jnp.int32, (seq, LANES), 0)
    pos = pos_i.astype(F32)

    khi, kmid, klo = _split3(slope2 * pos)
    ek = jnp.where(lane == pos_i // bs, 1.0, 0.0)
    ek = jnp.where(lane == nb, khi, ek)
    ek = jnp.where(lane == nb + 1, kmid, ek)
    ek = jnp.where(lane == nb + 2, klo, ek)
    ek = jnp.where((lane >= nb + 3) & (lane < nb + 6), 1.0, ek)
    kaug[:, 0:HEAD_DIM] = k.astype(BF16)
    kaug[:, HEAD_DIM:] = ek.astype(BF16)

    qhi, qmid, qlo = _split3(-slope2 * pos)
    eq = jnp.where(lane < nb, jnp.where(attend_r > 0.5, 0.0, MASKED), 0.0)
    eq = jnp.where((lane >= nb) & (lane < nb + 3), 1.0, eq)
    eq = jnp.where(lane == nb + 3, qhi, eq)
    eq = jnp.where(lane == nb + 4, qmid, eq)
    eq = jnp.where(lane == nb + 5, qlo, eq)
    qaug[:, 0:HEAD_DIM] = (q * (HEAD_DIM ** -0.5 * LOG2E)).astype(BF16)
    qaug[:, HEAD_DIM:] = eq.astype(BF16)

    vaug[:, 0:HEAD_DIM] = v_ref[0].astype(BF16)
    vaug[:, HEAD_DIM:] = jnp.ones((seq, HEAD_DIM), BF16)


SUM_CHAINS = 16


def _page_sums_step(step, n_steps, pt_ref, kc_hbm, ksum_ref, ring, sem, *, layer, pages_per_block):
    total = pt_ref.shape[0]
    depth, page_rows, n_heads, _ = ring.shape

    def page_copy(t, slot):
        return pltpu.make_async_copy(kc_hbm.at[layer, pt_ref[t]], ring.at[slot], sem.at[slot])

    @pl.when(step == 0)
    def _():
        ksum_ref[...] = jnp.zeros_like(ksum_ref)
        for slot in range(min(depth, total)):
            page_copy(slot, slot).start()

    def fold(t, slot):
        page_copy(t, slot).wait()
        page = ring[slot].reshape(SUM_CHAINS, page_rows // SUM_CHAINS, n_heads, HEAD_DIM)
        ksum_ref[t // pages_per_block] += jnp.sum(jnp.sum(page, axis=1), axis=0)

    if n_steps * depth == total:
        for slot in range(depth):
            fold(step * depth + slot, slot)

        @pl.when(step + 1 < n_steps)
        def _():
            for slot in range(depth):
                page_copy((step + 1) * depth + slot, slot).start()
    else:
        for slot in range(depth):
            t = step * depth + slot

            @pl.when(t < total)
            def _(t=t, slot=slot):
                fold(t, slot)

                @pl.when(t + depth < total)
                def _():
                    page_copy(t + depth, slot).start()


def _moba_kernel(pt_ref, q_ref, k_ref, v_ref, kc_hbm, o_ref, ksum_ref, qaug, kaug, vaug, kmean, m_ref, acc_ref,
                 s_ref, rel_ref, ring, psem, *, n_heads, nb, layer, page_pairs, page_steps):
    h = pl.program_id(1)
    bs = MOBA_BLOCK
    nt = (((1,), (1,)), ((), ()))
    seq = qaug.shape[0]
    n_tiles = seq // TQ
    n_units = n_tiles * (n_tiles + 1) // 2
    n_pairs = n_units // 2
    bh = pl.program_id(0) * pl.num_programs(1) + h

    slope2 = jnp.exp2((h + 1).astype(F32) * (-8.0 / n_heads) + jnp.zeros((1, 1), F32)) * LOG2E
    _moba_setup(q_ref, k_ref, v_ref, qaug, kaug, vaug, kmean, slope2, nb)
    rel_ref[...] = (lax.broadcasted_iota(jnp.int32, (TQ, TQ), 1) - lax.broadcasted_iota(jnp.int32, (TQ, TQ), 0))

    halves = (slice(0, bs), slice(bs, TQ))

    def tile_rows(t):
        return pl.ds(pl.multiple_of(t * TQ, TQ), TQ)

    def successor(unit):
        i, g = unit
        wrap = g == i
        return jnp.where(wrap, jnp.minimum(i + 1, n_tiles - 1), i), jnp.where(wrap, 0, g + 1)

    def scores(unit):
        i, g = unit
        qa = qaug[tile_rows(i), :]
        kg = kaug[tile_rows(g), :]
        return jnp.concatenate([lax.dot_general(qa[r], kg, nt, preferred_element_type=F32) for r in halves], axis=0)

    def process(unit, slot):
        i, g = unit
        s = jnp.where(rel_ref[...] <= (i - g) * TQ, s_ref[slot], MASKED)
        vg = vaug[tile_rows(g), :]
        for r in halves:
            m = jnp.where(g == 0, MASKED, m_ref[r, :])
            m_new = jnp.maximum(m, jnp.max(s[r], axis=1, keepdims=True))
            p = jnp.exp2(s[r] - m_new).astype(BF16)
            pv = jnp.dot(p, vg, preferred_element_type=F32)
            acc = jnp.exp2(m - m_new) * acc_ref[r, :] + pv
            acc_ref[r, :] = acc
            m_ref[r, :] = m_new
            rows = pl.ds(pl.multiple_of(i * TQ + r.start, bs), bs)
            o_ref[0, rows, :] = acc[:, 0:HEAD_DIM] / acc[:, HEAD_DIM:]

    m_ref[...] = jnp.full(m_ref.shape, MASKED, F32)
    acc_ref[...] = jnp.zeros_like(acc_ref)
    first = (jnp.int32(0), jnp.int32(0))
    s_ref[0] = scores(first)

    def pair(p, unit):
        @pl.when(p < page_pairs)
        def _():
            _page_sums_step(bh * page_pairs + p, page_steps, pt_ref, kc_hbm, ksum_ref, ring, psem, layer=layer,
                            pages_per_block=bs // PAGE_SIZE)

        nxt = successor(unit)
        s_ref[1] = scores(nxt)
        process(unit, 0)
        nxt2 = successor(nxt)
        s_ref[0] = scores(nxt2)
        process(nxt, 1)
        return nxt2

    last = lax.fori_loop(0, n_pairs, pair, first)
    if n_units % 2:
        process(last, 0)


def moba_prompt(q, k, v, cache_k, layer, page_table):
    n_heads, bsz, seq, _ = q.shape
    width = n_heads * HEAD_DIM
    assert seq % TQ == 0
    nb = seq // MOBA_BLOCK
    nbp = -(-nb // 8) * 8
    assert nb + 6 <= LANES
    bd, n_pages = page_table.shape
    pages_per_block = MOBA_BLOCK // PAGE_SIZE
    assert n_pages % pages_per_block == 0
    n_tiles = seq // TQ
    n_pairs = n_tiles * (n_tiles + 1) // 4
    assert n_pairs >= 1
    total_pages = bd * n_pages
    even = [p for p in range(1, n_pairs + 1) if total_pages % (bsz * n_heads * p) == 0]
    page_pairs = max(even) if even else n_pairs
    page_steps = bsz * n_heads * page_pairs
    pages_per_step = -(-total_pages // page_steps)
    sums_shape = (total_pages // pages_per_block, n_heads, HEAD_DIM)
    slab = pl.BlockSpec((None, 1, seq, HEAD_DIM), lambda b, h: (h, b, 0, 0))
    return pl.pallas_call(
        functools.partial(_moba_kernel, n_heads=n_heads, nb=nb, layer=layer, page_pairs=page_pairs,
                          page_steps=page_steps),
        grid=(bsz, n_heads),
        in_specs=[pl.BlockSpec(memory_space=pltpu.SMEM), slab, slab, slab, pl.BlockSpec(memory_space=pl.ANY)],
        out_specs=[pl.BlockSpec((1, seq, HEAD_DIM), lambda b, h: (b, 0, h)),
                   pl.BlockSpec(sums_shape, lambda b, h: (0, 0, 0))],
        out_shape=[jax.ShapeDtypeStruct((bsz, seq, width), F32), jax.ShapeDtypeStruct(sums_shape, F32)],
        scratch_shapes=[pltpu.VMEM((seq, DA), BF16), pltpu.VMEM((seq, DA), BF16), pltpu.VMEM((seq, DA), BF16),
                        pltpu.VMEM((nbp, HEAD_DIM), F32), pltpu.VMEM((TQ, 1), F32), pltpu.VMEM((TQ, DA), F32),
                        pltpu.VMEM((2, TQ, TQ), F32), pltpu.VMEM((TQ, TQ), jnp.int32),
                        pltpu.VMEM((pages_per_step, PAGE_SIZE, n_heads, HEAD_DIM), F32),
                        pltpu.SemaphoreType.DMA((pages_per_step,))],
        compiler_params=_params("arbitrary", "arbitrary"),
        name="moba_prompt",
    )(page_table.reshape(-1), q, k, v, cache_k)


PAGES_IN_FLIGHT = 8


def _gate_step_kernel(q_ref, ksum_ref, sel_ref, *, n_heads):
    bd = q_ref.shape[0]
    nblk = ksum_ref.shape[0] // bd
    sel_ref[...] = jnp.zeros_like(sel_ref)
    for b in range(bd):
        kmean = ksum_ref[b * nblk:(b + 1) * nblk] * (1.0 / MOBA_BLOCK)
        gate = jnp.sum(kmean * q_ref[b][None], axis=-1)
        blk = lax.broadcasted_iota(jnp.int32, gate.shape, 0)
        for r in range(MOBA_TOPK):
            m = jnp.max(gate, axis=0, keepdims=True)
            idx = jnp.min(jnp.where(gate == m, blk, nblk), axis=0, keepdims=True)
            sel_ref[b, r:r + 1, 0:n_heads] = idx
            gate = jnp.where(blk == idx, -jnp.inf, gate)


SEL_ROWS = 8


def gate_step(block_sums, q):
    bd, n_heads, _ = q.shape
    assert MOBA_TOPK <= SEL_ROWS and n_heads <= LANES
    sel = pl.pallas_call(
        functools.partial(_gate_step_kernel, n_heads=n_heads),
        out_shape=jax.ShapeDtypeStruct((bd, SEL_ROWS, LANES), jnp.int32),
        compiler_params=pltpu.CompilerParams(vmem_limit_bytes=VMEM_LIMIT),
        name="gate_step",
    )(q, block_sums)
    return sel[:, :MOBA_TOPK, :n_heads]


def _attn_step_kernel(pt_ref, sel_ref, q_ref, kn_ref, vn_ref, k_hbm, v_hbm, o_ref, kbuf, vbuf, ksem, vsem,
                      m_ref, l_ref, acc_ref, *, layer, n_pages, q_pos, pages_per_block):
    bd, n_heads, _ = q_ref.shape
    per_head = MOBA_TOPK * pages_per_block
    total = bd * n_heads * per_head
    depth = kbuf.shape[0]
    scale = HEAD_DIM ** -0.5
    own = q_pos // MOBA_BLOCK

    def coords(t):
        bh, j = t // per_head, t % per_head
        b, h = bh // n_heads, bh % n_heads
        return b, h, j, sel_ref[(b * MOBA_TOPK + j // pages_per_block) * n_heads + h]

    def page_copies(t, slot):
        b, _, j, n = coords(t)
        page = pt_ref[b * n_pages + n * pages_per_block + j % pages_per_block]
        return (pltpu.make_async_copy(k_hbm.at[layer, page], kbuf.at[slot], ksem.at[slot]),
                pltpu.make_async_copy(v_hbm.at[layer, page], vbuf.at[slot], vsem.at[slot]))

    for t in range(min(depth, total)):
        for c in page_copies(t, t):
            c.start()

    @pl.loop(0, total)
    def _(t):
        slot = t % depth
        b, h, j, n = coords(t)
        for c in page_copies(t, slot):
            c.wait()
        q = q_ref[b, pl.ds(h, 1), :]

        @pl.when(j == 0)
        def _():
            m_ref[...] = jnp.sum(q * kn_ref[b, pl.ds(h, 1), :], axis=1, keepdims=True) * scale
            l_ref[...] = jnp.ones_like(l_ref)
            acc_ref[...] = vn_ref[b, pl.ds(h, 1), :]

        k = kbuf[slot, pl.ds(h, PAGE_SIZE, stride=n_heads), :]
        v = vbuf[slot, pl.ds(h, PAGE_SIZE, stride=n_heads), :]
        slope = jnp.exp2(jnp.asarray(h + 1, F32) * (-8.0 / n_heads) + jnp.zeros((1, 1), F32))
        kpos = (n * MOBA_BLOCK + (j % pages_per_block) * PAGE_SIZE
                + lax.broadcasted_iota(jnp.int32, (PAGE_SIZE, 1), 0))
        dist = (q_pos - kpos).astype(F32)
        s = jnp.sum(k * q, axis=1, keepdims=True) * scale - slope * dist
        s = jnp.where((dist >= 0.0) & (n < own), s, MASKED)
        m = m_ref[...]
        m_new = jnp.maximum(m, jnp.max(s, axis=0, keepdims=True))
        alpha = jnp.exp(m - m_new)
        p = jnp.exp(s - m_new)
        l_ref[...] = alpha * l_ref[...] + jnp.sum(p, axis=0, keepdims=True)
        acc_ref[...] = alpha * acc_ref[...] + jnp.sum(p * v, axis=0, keepdims=True)
        m_ref[...] = m_new

        @pl.when(j == per_head - 1)
        def _():
            o_ref[b, pl.ds(h, 1), :] = acc_ref[...] / l_ref[...]

        @pl.when(t + depth < total)
        def _():
            for c in page_copies(t + depth, slot):
                c.start()


def attn_step(cache_k, cache_v, layer, page_table, sel, q, k_new, v_new):
    bd, n_pages = page_table.shape
    n_heads = q.shape[1]
    pages_per_block = MOBA_BLOCK // PAGE_SIZE
    assert n_pages % pages_per_block == 0
    rows = PAGE_SIZE * n_heads
    as_rows = lambda c: c.reshape(c.shape[0], c.shape[1], rows, HEAD_DIM)
    vmem, smem = pl.BlockSpec(memory_space=pltpu.VMEM), pl.BlockSpec(memory_space=pltpu.SMEM)
    hbm = pl.BlockSpec(memory_space=pl.ANY)
    return pl.pallas_call(
        functools.partial(_attn_step_kernel, layer=layer, n_pages=n_pages, q_pos=n_pages * PAGE_SIZE,
                          pages_per_block=pages_per_block),
        in_specs=[smem, smem, vmem, vmem, vmem, hbm, hbm],
        out_specs=vmem,
        out_shape=jax.ShapeDtypeStruct((bd, n_heads, HEAD_DIM), F32),
        scratch_shapes=[pltpu.VMEM((PAGES_IN_FLIGHT, rows, HEAD_DIM), F32),
                        pltpu.VMEM((PAGES_IN_FLIGHT, rows, HEAD_DIM), F32),
                        pltpu.SemaphoreType.DMA((PAGES_IN_FLIGHT,)), pltpu.SemaphoreType.DMA((PAGES_IN_FLIGHT,)),
                        pltpu.VMEM((1, 1), F32), pltpu.VMEM((1, 1), F32), pltpu.VMEM((1, HEAD_DIM), F32)],
        compiler_params=pltpu.CompilerParams(vmem_limit_bytes=VMEM_LIMIT),
        name="attn_step",
    )(page_table.reshape(-1), sel.reshape(-1), q, k_new, v_new, as_rows(cache_k), as_rows(cache_v))


P_HALO = 16


def _pool_out_kernel(o_ref, u_ref, x_ref, pw_ref, ps_ref, wo_ref, y_ref, newp_ref, uext, mix):
    l = pl.program_id(1)
    tl = u_ref.shape[1]
    wp = u_ref.shape[2]
    wo_attn = o_ref.shape[2]
    pg = wp // len(POOL_WINDOWS)

    @pl.when(l == 0)
    def _():
        uext[0:P_HALO, :] = jnp.zeros((P_HALO, wp), F32)

    @pl.when(l > 0)
    def _():
        uext[0:P_HALO, :] = uext[tl:tl + P_HALO, :]

    uext[P_HALO:P_HALO + tl, :] = u_ref[0]
    o_bf16 = o_ref[0].astype(BF16)
    n_chunk = y_ref.shape[2] // len(POOL_WINDOWS)
    pos = l * tl + lax.broadcasted_iota(jnp.int32, (tl, pg), 0)
    for g, w in enumerate(POOL_WINDOWS):
        n0 = g * n_chunk
        y_ref[0, :, n0:n0 + n_chunk] = x_ref[0, :, n0:n0 + n_chunk] + jnp.dot(
            o_bf16, wo_ref[0:wo_attn, n0:n0 + n_chunk], preferred_element_type=F32)
        c0 = g * pg
        wsum = uext[P_HALO:P_HALO + tl, c0:c0 + pg]
        for j in range(1, w):
            wsum = wsum + uext[P_HALO - j:P_HALO - j + tl, c0:c0 + pg]
        count = jnp.minimum(pos + 1, w).astype(F32)
        dlt = (wsum / count - uext[P_HALO:P_HALO + tl, c0:c0 + pg]).astype(BF16)
        yp = jnp.dot(dlt, pw_ref[g], preferred_element_type=F32) * ps_ref[:, c0:c0 + pg]
        mix[:, c0:c0 + pg] = yp.astype(BF16)

    y_ref[0] += jnp.dot(mix[...], wo_ref[wo_attn:, :], preferred_element_type=F32)

    @pl.when(l == pl.num_programs(1) - 1)
    def _():
        newp_ref[0] = uext[P_HALO + tl - (POOL_MAX - 1):P_HALO + tl, :]


def pool_out(o, u, x, pool_w, pool_scale, w_out, tl):
    bsz, seq, d = x.shape
    wp = u.shape[-1]
    tl = min(tl, seq)
    assert seq % tl == 0 and tl >= P_HALO
    const = _resident
    return pl.pallas_call(
        _pool_out_kernel,
        grid=(bsz, seq // tl),
        in_specs=[pl.BlockSpec((1, tl, o.shape[-1]), lambda b, l: (b, l, 0)),
                  pl.BlockSpec((1, tl, wp), lambda b, l: (b, l, 0)),
                  pl.BlockSpec((1, tl, d), lambda b, l: (b, l, 0)),
                  const(pool_w.shape), const((1, wp)), const(w_out.shape)],
        out_specs=[pl.BlockSpec((1, tl, d), lambda b, l: (b, l, 0)),
                   pl.BlockSpec((1, POOL_MAX - 1, wp), lambda b, l: (b, 0, 0))],
        out_shape=[jax.ShapeDtypeStruct((bsz, seq, d), F32),
                   jax.ShapeDtypeStruct((bsz, POOL_MAX - 1, wp), F32)],
        scratch_shapes=[pltpu.VMEM((P_HALO + tl, wp), F32), pltpu.VMEM((tl, wp), BF16)],
        compiler_params=_params("parallel", "arbitrary"),
        name="pool_out",
    )(o, u, x, pool_w, pool_scale.reshape(1, wp), w_out)


def _pool_step_kernel(o_ref, u_ref, x_ref, stp_ref, pw_ref, ps_ref, wo_ref, y_ref, *, start_pos):
    wp = u_ref.shape[1]
    pg = wp // len(POOL_WINDOWS)
    u = u_ref[...]
    parts = [o_ref[...].astype(BF16)]
    for g, w in enumerate(POOL_WINDOWS):
        c0 = g * pg
        wsum = u[:, c0:c0 + pg]
        for j in range(1, w):
            wsum = wsum + stp_ref[POOL_MAX - 1 - j][:, c0:c0 + pg]
        count = float(min(start_pos + 1, w))
        dlt = (wsum / count - u[:, c0:c0 + pg]).astype(BF16)
        yp = jnp.dot(dlt, pw_ref[g], preferred_element_type=F32) * ps_ref[:, c0:c0 + pg]
        parts.append(yp.astype(BF16))
    mix = jnp.concatenate(parts, axis=-1)
    y_ref[...] = x_ref[...] + jnp.dot(mix, wo_ref[...], preferred_element_type=F32)


def pool_step(o, u, x, st_p, pool_w, pool_scale, w_out, start_pos):
    bd, d = x.shape
    return pl.pallas_call(
        functools.partial(_pool_step_kernel, start_pos=start_pos),
        out_shape=jax.ShapeDtypeStruct((bd, d), F32),
        compiler_params=pltpu.CompilerParams(vmem_limit_bytes=VMEM_LIMIT),
        name="pool_step",
    )(o, u, x, st_p, pool_w, pool_scale.reshape(1, -1), w_out)


TM_PROMPT = 512
TM_FFN = 1024
TF = 512
TL_MIX = 256


def kernel(x_prompt, x_sample, state_conv_a, state_conv_b, cache_k, cache_v, state_pool, page_table, conv_norm_g, conv_w_in, conv_a_dw, conv_a_dw_b, conv_a_ln_g, conv_a_ln_b, conv_b_dw, conv_w_out, attn_norm_g, attn_w_in, pool_w, pool_scale, attn_w_out, ffn_norm_g, ffn_w_gate, ffn_w_up, ffn_w_down, final_norm_g):
    bsz, seq, d = x_prompt.shape
    bd, dec_seq, _ = x_sample.shape
    assert dec_seq == 1
    depth = ffn_norm_g.shape[0]
    n_heads = cache_k.shape[3]
    wattn = n_heads * HEAD_DIM
    past_len = page_table.shape[1] * PAGE_SIZE

    layer_bf16 = lambda w, n: w[n].astype(BF16)

    xp = x_prompt.reshape(bsz * seq, d)
    xs = x_sample.reshape(bd, d)
    pa, pb, pk, pv, pp = [], [], [], [], []
    sa, sb, sk, sv, sp = [], [], [], [], []
    for layer in range(depth):
        i = layer // 2
        if layer % 2 == 0:
            w_in, w_out = layer_bf16(conv_w_in, i), layer_bf16(conv_w_out, i)
            parts = norm_matmul(xp, conv_norm_g[i], w_in, TM_PROMPT)
            wa = parts[0].shape[-1]
            y, na, nb_ = conv_mix([t.reshape(bsz, seq, wa) for t in parts], xp.reshape(bsz, seq, d),
                                  conv_a_dw[i], conv_a_dw_b[i], conv_a_ln_g[i], conv_a_ln_b[i], conv_b_dw[i],
                                  w_out, TL_MIX)
            xp = y.reshape(bsz * seq, d)
            pa.append(na)
            pb.append(nb_)

            parts = norm_matmul(xs, conv_norm_g[i], w_in, TM_PROMPT)
            st_a = jnp.swapaxes(state_conv_a[i], 0, 1)
            st_b = jnp.swapaxes(state_conv_b[i], 0, 1)
            xs, a_new, c_new = conv_step(parts, xs, st_a, st_b, conv_a_dw[i], conv_a_dw_b[i], conv_a_ln_g[i],
                                         conv_a_ln_b[i], conv_b_dw[i], w_out)
            sa.append(jnp.concatenate([state_conv_a[i][:, 1:], a_new[:, None]], axis=1))
            sb.append(jnp.concatenate([state_conv_b[i][:, 1:], c_new[:, None]], axis=1))
        else:
            w_in, w_out, w_pool = layer_bf16(attn_w_in, i), layer_bf16(attn_w_out, i), layer_bf16(pool_w, i)
            q, k, v, u = norm_matmul(xp, attn_norm_g[i], w_in, TM_PROMPT, head_major=3)
            shp = (n_heads, bsz, seq, HEAD_DIM)
            o, block_sums = moba_prompt(q.reshape(shp), k.reshape(shp), v.reshape(shp), cache_k, i, page_table)
            y, npool = pool_out(o, u.reshape(bsz, seq, -1), xp.reshape(bsz, seq, d), w_pool, pool_scale[i],
                                w_out, TL_MIX)
            xp = y.reshape(bsz * seq, d)
            pk.append(jnp.transpose(k.reshape(shp), (1, 2, 0, 3)))
            pv.append(jnp.transpose(v.reshape(shp), (1, 2, 0, 3)))
            pp.append(npool)

            q, k, v, u = norm_matmul(xs, attn_norm_g[i], w_in, TM_PROMPT, head_major=3)
            heads = lambda t: jnp.swapaxes(t, 0, 1)
            sel = gate_step(block_sums, heads(q))
            o = attn_step(cache_k, cache_v, i, page_table, sel, heads(q), heads(k), heads(v))
            st_p = jnp.swapaxes(state_pool[i], 0, 1)
            xs = pool_step(o.reshape(bd, wattn), u, xs, st_p, w_pool, pool_scale[i], w_out, past_len)
            sk.append(heads(k).reshape(bd, 1, n_heads, HEAD_DIM))
            sv.append(heads(v).reshape(bd, 1, n_heads, HEAD_DIM))
            sp.append(jnp.concatenate([state_pool[i][:, 1:], u[:, None]], axis=1))
        last = layer == depth - 1
        ffn_w = (ffn_w_gate, ffn_w_up, ffn_w_down, layer)
        xp, xs = ffn(xp, xs, ffn_norm_g[layer], *ffn_w, final_norm_g, TM_FFN, TF, last)
    return (xp.reshape(bsz, seq, d), xs.reshape(bd, 1, d), jnp.stack(pa), jnp.stack(sa), jnp.stack(pb),
            jnp.stack(sb), jnp.stack(pk), jnp.stack(sk), jnp.stack(pv), jnp.stack(sv), jnp.stack(pp), jnp.stack(sp))
```

```python
import functools

import jax
import jax.numpy as jnp
from jax import lax
from jax.experimental import pallas as pl
from jax.experimental.pallas import tpu as pltpu

EPS = 1e-6
PAGE_SIZE = 128
MOBA_BLOCK = 256
MOBA_TOPK = 3
HEAD_DIM = 128
POOL_WINDOWS = (2, 4, 8, 16)
POOL_MAX = max(POOL_WINDOWS)
MASKED = -1e30

V7X_VMEM_BYTES = 64 * 1024 * 1024
VMEM_LIMIT = V7X_VMEM_BYTES - 4 * 1024 * 1024
LANES = 128
SUBLANES = 8
COL_GROUP = 1024

BF16 = jnp.bfloat16
F32 = jnp.float32


def _params(*sem):
    return pltpu.CompilerParams(dimension_semantics=sem, vmem_limit_bytes=VMEM_LIMIT)


def _rms(x, g):
    ms = jnp.mean(x * x, axis=-1, keepdims=True)
    return x * lax.rsqrt(ms + EPS) * g


def _sigmoid(x):
    return 1.0 / (1.0 + jnp.exp(-x))


def _resident(shape):
    return pl.BlockSpec(shape, lambda *_: (0,) * len(shape), pipeline_mode=pl.Buffered(1))


def _norm_matmul_kernel(x_ref, g_ref, w_ref, *out_refs, head_major):
    h = _rms(x_ref[...], g_ref[...]).astype(BF16)
    for c, o_ref in enumerate(out_refs):
        r = jnp.dot(h, w_ref[:, c * COL_GROUP:(c + 1) * COL_GROUP], preferred_element_type=F32)
        if c < head_major:
            for hh in range(COL_GROUP // HEAD_DIM):
                o_ref[hh] = r[:, hh * HEAD_DIM:(hh + 1) * HEAD_DIM]
        else:
            o_ref[...] = r


def norm_matmul(x, g, w, tm, head_major=0):
    m, d = x.shape
    n_out = w.shape[1] // COL_GROUP
    tm = min(tm, m)
    heads = COL_GROUP // HEAD_DIM
    flat_spec = pl.BlockSpec((tm, COL_GROUP), lambda i: (i, 0))
    head_spec = pl.BlockSpec((heads, tm, HEAD_DIM), lambda i: (0, i, 0))
    flat_shape = jax.ShapeDtypeStruct((m, COL_GROUP), F32)
    head_shape = jax.ShapeDtypeStruct((heads, m, HEAD_DIM), F32)
    return pl.pallas_call(
        functools.partial(_norm_matmul_kernel, head_major=head_major),
        grid=(m // tm,),
        in_specs=[pl.BlockSpec((tm, d), lambda i: (i, 0)), _resident((1, d)), _resident(w.shape)],
        out_specs=[head_spec] * head_major + [flat_spec] * (n_out - head_major),
        out_shape=[head_shape] * head_major + [flat_shape] * (n_out - head_major),
        compiler_params=_params("parallel"),
        name="norm_matmul",
    )(x, g.reshape(1, d), w)


def _ffn_kernel(g_ref, fg_ref, xs_ref, wg_ref, wu_ref, wd_ref, x_hbm, o_ref, os_ref, xbuf, h_ref, sem, *, final_norm):
    i, f = pl.program_id(0), pl.program_id(1)
    tm = o_ref.shape[0]
    ns = xs_ref.shape[0]
    last_f = pl.num_programs(1) - 1

    def x_copy(tile):
        return pltpu.make_async_copy(x_hbm.at[pl.ds(pl.multiple_of(tile * tm, tm), tm), :], xbuf, sem)

    @pl.when((i == 0) & (f == 0))
    def _():
        x_copy(0).start()
        xs = xs_ref[...]
        h_ref[tm:, :] = jnp.zeros((h_ref.shape[0] - tm, h_ref.shape[1]), h_ref.dtype)
        h_ref[tm:tm + ns, :] = _rms(xs, g_ref[...]).astype(h_ref.dtype)
        os_ref[...] = xs

    @pl.when(f == 0)
    def _():
        x_copy(i).wait()
        x = xbuf[...]
        h_ref[0:tm, :] = _rms(x, g_ref[...]).astype(h_ref.dtype)
        o_ref[...] = x

    @pl.when((f == 1) & (i + 1 < pl.num_programs(0)))
    def _():
        x_copy(i + 1).start()

    h = h_ref[...]
    gate = jnp.dot(h, wg_ref[...].astype(BF16), preferred_element_type=F32)
    up = jnp.dot(h, wu_ref[...].astype(BF16), preferred_element_type=F32)
    act = (gate * _sigmoid(gate) * up).astype(BF16)
    down = jnp.dot(act, wd_ref[...].astype(BF16), preferred_element_type=F32)
    o_ref[...] += down[0:tm]

    @pl.when(i == 0)
    def _():
        os_ref[...] += down[tm:tm + ns]

    if final_norm:
        @pl.when(f == last_f)
        def _():
            o_ref[...] = _rms(o_ref[...], fg_ref[...])

        @pl.when((i == 0) & (f == last_f))
        def _():
            os_ref[...] = _rms(os_ref[...], fg_ref[...])


BF16_ROWS = 16


def ffn(x, xs, g, wg, wu, wd, layer, final_g, tm, tf, final_norm):
    m, d = x.shape
    ns = xs.shape[0]
    dff = wg.shape[2]
    tm = min(tm, m)
    assert m % tm == 0 and dff % tf == 0 and dff // tf >= 2 and ns <= BF16_ROWS and tm % BF16_ROWS == 0
    return pl.pallas_call(
        functools.partial(_ffn_kernel, final_norm=final_norm),
        grid=(m // tm, dff // tf),
        in_specs=[_resident((1, d)), _resident((1, d)), _resident((ns, d)),
                  pl.BlockSpec((None, d, tf), lambda i, f: (layer, 0, f)),
                  pl.BlockSpec((None, d, tf), lambda i, f: (layer, 0, f)),
                  pl.BlockSpec((None, tf, d), lambda i, f: (layer, f, 0)),
                  pl.BlockSpec(memory_space=pl.ANY)],
        out_specs=[pl.BlockSpec((tm, d), lambda i, f: (i, 0)), pl.BlockSpec((ns, d), lambda i, f: (0, 0))],
        out_shape=[jax.ShapeDtypeStruct((m, d), F32), jax.ShapeDtypeStruct((ns, d), F32)],
        scratch_shapes=[pltpu.VMEM((tm, d), F32), pltpu.VMEM((tm + BF16_ROWS, d), BF16),
                        pltpu.SemaphoreType.DMA(())],
        compiler_params=_params("arbitrary", "arbitrary"),
        name="ffn",
    )(g.reshape(1, d), final_g.reshape(1, d), xs, wg, wu, wd, x)


A_HALO = 32
B_HALO = 8


def _layer_norm_silu(x, g, b):
    mu = jnp.mean(x, axis=-1, keepdims=True)
    xc = x - mu
    y = xc * lax.rsqrt(jnp.mean(xc * xc, axis=-1, keepdims=True) + EPS) * g + b
    return y * _sigmoid(y)


def _conv_mix_kernel(av_ref, ag_ref, bh_ref, bb_ref, bc_ref, x_ref, adw_ref, adwb_ref, lng_ref, lnb_ref,
                     bdw_ref, wo_ref, y_ref, newa_ref, newb_ref, aext, cext, ashift, aconv, *, ta, tb, rc):
    l = pl.program_id(1)
    tl = av_ref.shape[1]
    wa = av_ref.shape[2]

    @pl.when(l == 0)
    def _():
        aext[0:A_HALO, :] = jnp.zeros((A_HALO, wa), F32)
        cext[0:B_HALO, :] = jnp.zeros((B_HALO, wa), F32)

    @pl.when(l > 0)
    def _():
        aext[0:A_HALO, :] = aext[tl:tl + A_HALO, :]
        cext[0:B_HALO, :] = cext[tl:tl + B_HALO, :]

    aext[A_HALO:A_HALO + tl, :] = av_ref[0] * _sigmoid(ag_ref[0])
    cext[B_HALO:B_HALO + tl, :] = bc_ref[0] * bh_ref[0]

    bconv = jnp.zeros((tl, wa), F32)
    for k in range(tb):
        off = B_HALO - (tb - 1) + k
        bconv = bconv + bdw_ref[k:k + 1, :] * cext[off:off + tl, :]
    b_out = (bb_ref[0] * bconv).astype(BF16)
    y_ref[0] = x_ref[0] + jnp.dot(b_out, wo_ref[wa:, :], preferred_element_type=F32)

    n_shift = ashift.shape[1]
    for s in range(1, SUBLANES):
        ashift[s - 1] = aext[s:s + n_shift, :]

    for r0 in range(0, tl, rc):
        for c0 in range(0, wa, LANES):
            acc = jnp.broadcast_to(adwb_ref[:, c0:c0 + LANES], (rc, LANES))
            for k in range(ta):
                off = A_HALO - (ta - 1) + r0 + k
                s = off % SUBLANES
                src = aext if s == 0 else ashift.at[s - 1]
                acc = acc + adw_ref[k:k + 1, c0:c0 + LANES] * src[off - s:off - s + rc, c0:c0 + LANES]
            aconv[r0:r0 + rc, c0:c0 + LANES] = acc
    a_out = _layer_norm_silu(aconv[...], lng_ref[...], lnb_ref[...]).astype(BF16)
    y_ref[0] += jnp.dot(a_out, wo_ref[0:wa, :], preferred_element_type=F32)

    @pl.when(l == pl.num_programs(1) - 1)
    def _():
        newa_ref[0] = aext[A_HALO + tl - (ta - 1):A_HALO + tl, :]
        newb_ref[0] = cext[B_HALO + tl - (tb - 1):B_HALO + tl, :]


def conv_mix(parts, x, a_dw, a_dw_b, ln_g, ln_b, b_dw, w_out, tl):
    bsz, seq, d = x.shape
    wa = parts[0].shape[-1]
    ta, tb = a_dw.shape[0], b_dw.shape[0]
    tl = min(tl, seq)
    assert seq % tl == 0 and tl >= A_HALO and ta - 1 <= A_HALO and tb - 1 <= B_HALO
    part_spec = pl.BlockSpec((1, tl, wa), lambda b, l: (b, l, 0))
    const = _resident
    return pl.pallas_call(
        functools.partial(_conv_mix_kernel, ta=ta, tb=tb, rc=min(128, tl)),
        grid=(bsz, seq // tl),
        in_specs=[part_spec] * 5 + [pl.BlockSpec((1, tl, d), lambda b, l: (b, l, 0)),
                                    const((ta, wa)), const((1, wa)), const((1, wa)), const((1, wa)),
                                    const((tb, wa)), const(w_out.shape)],
        out_specs=[pl.BlockSpec((1, tl, d), lambda b, l: (b, l, 0)),
                   pl.BlockSpec((1, ta - 1, wa), lambda b, l: (b, 0, 0)),
                   pl.BlockSpec((1, tb - 1, wa), lambda b, l: (b, 0, 0))],
        out_shape=[jax.ShapeDtypeStruct((bsz, seq, d), F32),
                   jax.ShapeDtypeStruct((bsz, ta - 1, wa), F32),
                   jax.ShapeDtypeStruct((bsz, tb - 1, wa), F32)],
        scratch_shapes=[pltpu.VMEM((A_HALO + tl, wa), F32), pltpu.VMEM((B_HALO + tl, wa), F32),
                        pltpu.VMEM((SUBLANES - 1, A_HALO + tl - SUBLANES, wa), F32),
                        pltpu.VMEM((tl, wa), F32)],
        compiler_params=_params("parallel", "arbitrary"),
        name="conv_mix",
    )(*parts, x, a_dw, a_dw_b.reshape(1, wa), ln_g.reshape(1, wa), ln_b.reshape(1, wa), b_dw, w_out)


def _conv_step_kernel(av_ref, ag_ref, bh_ref, bb_ref, bc_ref, x_ref, sta_ref, stb_ref, adw_ref, adwb_ref,
                      lng_ref, lnb_ref, bdw_ref, wo_ref, y_ref, anew_ref, cnew_ref, *, ta, tb):
    a = av_ref[...] * _sigmoid(ag_ref[...])
    c = bc_ref[...] * bh_ref[...]
    anew_ref[...] = a
    cnew_ref[...] = c
    acc = adwb_ref[...] + adw_ref[ta - 1:ta, :] * a
    for k in range(ta - 1):
        acc = acc + adw_ref[k:k + 1, :] * sta_ref[k]
    a_out = _layer_norm_silu(acc, lng_ref[...], lnb_ref[...])
    bconv = bdw_ref[tb - 1:tb, :] * c
    for k in range(tb - 1):
        bconv = bconv + bdw_ref[k:k + 1, :] * stb_ref[k]
    mix = jnp.concatenate([a_out, bb_ref[...] * bconv], axis=-1).astype(BF16)
    y_ref[...] = x_ref[...] + jnp.dot(mix, wo_ref[...], preferred_element_type=F32)


def conv_step(parts, x, st_a, st_b, a_dw, a_dw_b, ln_g, ln_b, b_dw, w_out):
    bd, d = x.shape
    wa = parts[0].shape[-1]
    ta, tb = a_dw.shape[0], b_dw.shape[0]
    return pl.pallas_call(
        functools.partial(_conv_step_kernel, ta=ta, tb=tb),
        out_shape=[jax.ShapeDtypeStruct((bd, d), F32), jax.ShapeDtypeStruct((bd, wa), F32),
                   jax.ShapeDtypeStruct((bd, wa), F32)],
        compiler_params=pltpu.CompilerParams(vmem_limit_bytes=VMEM_LIMIT),
        name="conv_step",
    )(*parts, x, st_a, st_b, a_dw, a_dw_b.reshape(1, wa), ln_g.reshape(1, wa), ln_b.reshape(1, wa), b_dw, w_out)


LOG2E = 1.4426950408889634
TQ = 2 * MOBA_BLOCK
DA = 2 * HEAD_DIM


def _split3(x):
    def top_bits(v):
        return lax.bitcast_convert_type(lax.bitcast_convert_type(v, jnp.uint32) & jnp.uint32(0xFFFF0000), F32)

    hi = top_bits(x)
    mid = top_bits(x - hi)
    return hi, mid, x - hi - mid


def _moba_setup(q_ref, k_ref, v_ref, qaug, kaug, vaug, kmean, slope2, nb):
    seq = k_ref.shape[1]
    bs = MOBA_BLOCK
    nbp = kmean.shape[0]
    k = k_ref[0]
    q = q_ref[0]
    kmean[...] = jnp.zeros_like(kmean)
    kmean[0:nb, :] = jnp.mean(k.reshape(nb, bs, HEAD_DIM), axis=1)

    gate = lax.dot_general(kmean[...], q, (((1,), (1,)), ((), ())), precision=lax.Precision.HIGHEST,
                           preferred_element_type=F32)
    blk = lax.broadcasted_iota(jnp.int32, (nbp, seq), 0)
    own = lax.broadcasted_iota(jnp.int32, (nbp, seq), 1) // bs
    gate = jnp.where(blk < own, gate, -jnp.inf)
    attend = blk == own
    for _ in range(MOBA_TOPK):
        m = jnp.max(gate, axis=0, keepdims=True)
        idx = jnp.min(jnp.where(gate == m, blk, nbp), axis=0, keepdims=True)
        hit = blk == idx
        attend = attend | (hit & (blk < own))
        gate = jnp.where(hit, -jnp.inf, gate)
    attend_t = jnp.where(attend, 1.0, 0.0)
    attend_r = jnp.concatenate([attend_t, jnp.zeros((LANES - nbp, seq), F32)], axis=0).T

    lane = lax.broadcasted_iota(jnp.int32, (seq, LANES), 1)
    pos_i = lax.broadcasted_iota(jnp.int32, (seq, LANES), 0)
    pos = pos_i.astype(F32)

    khi, kmid, klo = _split3(slope2 * pos)
    ek = jnp.where(lane == pos_i // bs, 1.0, 0.0)
    ek = jnp.where(lane == nb, khi, ek)
    ek = jnp.where(lane == nb + 1, kmid, ek)
    ek = jnp.where(lane == nb + 2, klo, ek)
    ek = jnp.where((lane >= nb + 3) & (lane < nb + 6), 1.0, ek)
    kaug[:, 0:HEAD_DIM] = k.astype(BF16)
    kaug[:, HEAD_DIM:] = ek.astype(BF16)

    qhi, qmid, qlo = _split3(-slope2 * pos)
    eq = jnp.where(lane < nb, jnp.where(attend_r > 0.5, 0.0, MASKED), 0.0)
    eq = jnp.where((lane >= nb) & (lane < nb + 3), 1.0, eq)
    eq = jnp.where(lane == nb + 3, qhi, eq)
    eq = jnp.where(lane == nb + 4, qmid, eq)
    eq = jnp.where(lane == nb + 5, qlo, eq)
    qaug[:, 0:HEAD_DIM] = (q * (HEAD_DIM ** -0.5 * LOG2E)).astype(BF16)
    qaug[:, HEAD_DIM:] = eq.astype(BF16)

    vaug[:, 0:HEAD_DIM] = v_ref[0].astype(BF16)
    vaug[:, HEAD_DIM:] = jnp.ones((seq, HEAD_DIM), BF16)


SUM_CHAINS = 16
PAGE_LOOKAHEAD = 2


def _page_sums_step(step, n_steps, pt_ref, kc_hbm, ksum_ref, ring, sem, *, layer, pages_per_block):
    total = pt_ref.shape[0]
    depth, page_rows, n_heads, _ = ring.shape
    per = depth // PAGE_LOOKAHEAD

    def page_copy(t, slot):
        return pltpu.make_async_copy(kc_hbm.at[layer, pt_ref[t]], ring.at[slot], sem.at[slot])

    @pl.when(step == 0)
    def _():
        ksum_ref[...] = jnp.zeros_like(ksum_ref)
        for slot in range(min(depth, total)):
            page_copy(slot, slot).start()

    def fold(t, slot):
        page_copy(t, slot).wait()
        page = ring[slot].reshape(SUM_CHAINS, page_rows // SUM_CHAINS, n_heads, HEAD_DIM)
        ksum_ref[t // pages_per_block] += jnp.sum(jnp.sum(page, axis=1), axis=0)

    first_slot = (step % PAGE_LOOKAHEAD) * per
    if n_steps * per == total:
        for j in range(per):
            fold(step * per + j, first_slot + j)

        @pl.when(step + PAGE_LOOKAHEAD < n_steps)
        def _():
            for j in range(per):
                page_copy(step * per + j + depth, first_slot + j).start()
    else:
        for j in range(per):
            t = step * per + j

            @pl.when(t < total)
            def _(t=t, j=j):
                fold(t, first_slot + j)

                @pl.when(t + depth < total)
                def _():
                    page_copy(t + depth, first_slot + j).start()


def _moba_kernel(pt_ref, q_ref, k_ref, v_ref, kc_hbm, o_ref, ksum_ref, qaug, kaug, vaug, kmean, m_ref, acc_ref,
                 s_ref, rel_ref, ring, psem, *, n_heads, nb, layer, page_pairs, page_steps):
    h = pl.program_id(1)
    bs = MOBA_BLOCK
    nt = (((1,), (1,)), ((), ()))
    seq = qaug.shape[0]
    n_tiles = seq // TQ
    n_units = n_tiles * (n_tiles + 1) // 2
    n_pairs = n_units // 2
    bh = pl.program_id(0) * pl.num_programs(1) + h

    slope2 = jnp.exp2((h + 1).astype(F32) * (-8.0 / n_heads) + jnp.zeros((1, 1), F32)) * LOG2E
    _moba_setup(q_ref, k_ref, v_ref, qaug, kaug, vaug, kmean, slope2, nb)
    rel_ref[...] = (lax.broadcasted_iota(jnp.int32, (TQ, TQ), 1) - lax.broadcasted_iota(jnp.int32, (TQ, TQ), 0))

    halves = (slice(0, bs), slice(bs, TQ))

    def tile_rows(t):
        return pl.ds(pl.multiple_of(t * TQ, TQ), TQ)

    def successor(unit):
        i, g = unit
        wrap = g == i
        return jnp.where(wrap, jnp.minimum(i + 1, n_tiles - 1), i), jnp.where(wrap, 0, g + 1)

    def scores(unit):
        i, g = unit
        qa = qaug[tile_rows(i), :]
        kg = kaug[tile_rows(g), :]
        return jnp.concatenate([lax.dot_general(qa[r], kg, nt, preferred_element_type=F32) for r in halves], axis=0)

    def process(unit, slot):
        i, g = unit
        s = jnp.where(rel_ref[...] <= (i - g) * TQ, s_ref[slot], MASKED)
        vg = vaug[tile_rows(g), :]
        for r in halves:
            m = jnp.where(g == 0, MASKED, m_ref[r, :])
            m_new = jnp.maximum(m, jnp.max(s[r], axis=1, keepdims=True))
            p = jnp.exp2(s[r] - m_new).astype(BF16)
            pv = jnp.dot(p, vg, preferred_element_type=F32)
            acc = jnp.exp2(m - m_new) * acc_ref[r, :] + pv
            acc_ref[r, :] = acc
            m_ref[r, :] = m_new
            rows = pl.ds(pl.multiple_of(i * TQ + r.start, bs), bs)
            o_ref[0, rows, :] = acc[:, 0:HEAD_DIM] / acc[:, HEAD_DIM:]

    m_ref[...] = jnp.full(m_ref.shape, MASKED, F32)
    acc_ref[...] = jnp.zeros_like(acc_ref)
    first = (jnp.int32(0), jnp.int32(0))
    s_ref[0] = scores(first)

    def pair(p, unit):
        @pl.when(p < page_pairs)
        def _():
            _page_sums_step(bh * page_pairs + p, page_steps, pt_ref, kc_hbm, ksum_ref, ring, psem, layer=layer,
                            pages_per_block=bs // PAGE_SIZE)

        nxt = successor(unit)
        s_ref[1] = scores(nxt)
        process(unit, 0)
        nxt2 = successor(nxt)
        s_ref[0] = scores(nxt2)
        process(nxt, 1)
        return nxt2

    last = lax.fori_loop(0, n_pairs, pair, first)
    if n_units % 2:
        process(last, 0)


def moba_prompt(q, k, v, cache_k, layer, page_table):
    n_heads, bsz, seq, _ = q.shape
    width = n_heads * HEAD_DIM
    assert seq % TQ == 0
    nb = seq // MOBA_BLOCK
    nbp = -(-nb // 8) * 8
    assert nb + 6 <= LANES
    bd, n_pages = page_table.shape
    pages_per_block = MOBA_BLOCK // PAGE_SIZE
    assert n_pages % pages_per_block == 0
    n_tiles = seq // TQ
    n_pairs = n_tiles * (n_tiles + 1) // 4
    assert n_pairs >= 1
    total_pages = bd * n_pages
    even = [p for p in range(1, n_pairs + 1) if total_pages % (bsz * n_heads * p) == 0]
    page_pairs = max(even) if even else n_pairs
    page_steps = bsz * n_heads * page_pairs
    pages_per_step = -(-total_pages // page_steps)
    sums_shape = (total_pages // pages_per_block, n_heads, HEAD_DIM)
    slab = pl.BlockSpec((None, 1, seq, HEAD_DIM), lambda b, h: (h, b, 0, 0))
    return pl.pallas_call(
        functools.partial(_moba_kernel, n_heads=n_heads, nb=nb, layer=layer, page_pairs=page_pairs,
                          page_steps=page_steps),
        grid=(bsz, n_heads),
        in_specs=[pl.BlockSpec(memory_space=pltpu.SMEM), slab, slab, slab, pl.BlockSpec(memory_space=pl.ANY)],
        out_specs=[pl.BlockSpec((1, seq, HEAD_DIM), lambda b, h: (b, 0, h)),
                   pl.BlockSpec(sums_shape, lambda b, h: (0, 0, 0))],
        out_shape=[jax.ShapeDtypeStruct((bsz, seq, width), F32), jax.ShapeDtypeStruct(sums_shape, F32)],
        scratch_shapes=[pltpu.VMEM((seq, DA), BF16), pltpu.VMEM((seq, DA), BF16), pltpu.VMEM((seq, DA), BF16),
                        pltpu.VMEM((nbp, HEAD_DIM), F32), pltpu.VMEM((TQ, 1), F32), pltpu.VMEM((TQ, DA), F32),
                        pltpu.VMEM((2, TQ, TQ), F32), pltpu.VMEM((TQ, TQ), jnp.int32),
                        pltpu.VMEM((PAGE_LOOKAHEAD * pages_per_step, PAGE_SIZE, n_heads, HEAD_DIM), F32),
                        pltpu.SemaphoreType.DMA((PAGE_LOOKAHEAD * pages_per_step,))],
        compiler_params=_params("arbitrary", "arbitrary"),
        name="moba_prompt",
    )(page_table.reshape(-1), q, k, v, cache_k)


PAGES_IN_FLIGHT = 8


def _gate_step_kernel(q_ref, ksum_ref, sel_ref, *, n_heads):
    bd = q_ref.shape[0]
    nblk = ksum_ref.shape[0] // bd
    sel_ref[...] = jnp.zeros_like(sel_ref)
    for b in range(bd):
        kmean = ksum_ref[b * nblk:(b + 1) * nblk] * (1.0 / MOBA_BLOCK)
        gate = jnp.sum(kmean * q_ref[b][None], axis=-1)
        blk = lax.broadcasted_iota(jnp.int32, gate.shape, 0)
        for r in range(MOBA_TOPK):
            m = jnp.max(gate, axis=0, keepdims=True)
            idx = jnp.min(jnp.where(gate == m, blk, nblk), axis=0, keepdims=True)
            sel_ref[b, r:r + 1, 0:n_heads] = idx
            gate = jnp.where(blk == idx, -jnp.inf, gate)


SEL_ROWS = 8


def gate_step(block_sums, q):
    bd, n_heads, _ = q.shape
    assert MOBA_TOPK <= SEL_ROWS and n_heads <= LANES
    sel = pl.pallas_call(
        functools.partial(_gate_step_kernel, n_heads=n_heads),
        out_shape=jax.ShapeDtypeStruct((bd, SEL_ROWS, LANES), jnp.int32),
        compiler_params=pltpu.CompilerParams(vmem_limit_bytes=VMEM_LIMIT),
        name="gate_step",
    )(q, block_sums)
    return sel[:, :MOBA_TOPK, :n_heads]


def _attn_step_kernel(pt_ref, sel_ref, q_ref, kn_ref, vn_ref, k_hbm, v_hbm, o_ref, kbuf, vbuf, ksem, vsem,
                      m_ref, l_ref, acc_ref, *, layer, n_pages, q_pos, pages_per_block):
    bd, n_heads, _ = q_ref.shape
    per_head = MOBA_TOPK * pages_per_block
    total = bd * n_heads * per_head
    depth = kbuf.shape[0]
    scale = HEAD_DIM ** -0.5
    own = q_pos // MOBA_BLOCK

    def coords(t):
        bh, j = t // per_head, t % per_head
        b, h = bh // n_heads, bh % n_heads
        return b, h, j, sel_ref[(b * MOBA_TOPK + j // pages_per_block) * n_heads + h]

    def page_copies(t, slot):
        b, _, j, n = coords(t)
        page = pt_ref[b * n_pages + n * pages_per_block + j % pages_per_block]
        return (pltpu.make_async_copy(k_hbm.at[layer, page], kbuf.at[slot], ksem.at[slot]),
                pltpu.make_async_copy(v_hbm.at[layer, page], vbuf.at[slot], vsem.at[slot]))

    for t in range(min(depth, total)):
        for c in page_copies(t, t):
            c.start()

    @pl.loop(0, total)
    def _(t):
        slot = t % depth
        b, h, j, n = coords(t)
        for c in page_copies(t, slot):
            c.wait()
        q = q_ref[b, pl.ds(h, 1), :]

        @pl.when(j == 0)
        def _():
            m_ref[...] = jnp.sum(q * kn_ref[b, pl.ds(h, 1), :], axis=1, keepdims=True) * scale
            l_ref[...] = jnp.ones_like(l_ref)
            acc_ref[...] = vn_ref[b, pl.ds(h, 1), :]

        k = kbuf[slot, pl.ds(h, PAGE_SIZE, stride=n_heads), :]
        v = vbuf[slot, pl.ds(h, PAGE_SIZE, stride=n_heads), :]
        slope = jnp.exp2(jnp.asarray(h + 1, F32) * (-8.0 / n_heads) + jnp.zeros((1, 1), F32))
        kpos = (n * MOBA_BLOCK + (j % pages_per_block) * PAGE_SIZE
                + lax.broadcasted_iota(jnp.int32, (PAGE_SIZE, 1), 0))
        dist = (q_pos - kpos).astype(F32)
        s = jnp.sum(k * q, axis=1, keepdims=True) * scale - slope * dist
        s = jnp.where((dist >= 0.0) & (n < own), s, MASKED)
        m = m_ref[...]
        m_new = jnp.maximum(m, jnp.max(s, axis=0, keepdims=True))
        alpha = jnp.exp(m - m_new)
        p = jnp.exp(s - m_new)
        l_ref[...] = alpha * l_ref[...] + jnp.sum(p, axis=0, keepdims=True)
        acc_ref[...] = alpha * acc_ref[...] + jnp.sum(p * v, axis=0, keepdims=True)
        m_ref[...] = m_new

        @pl.when(j == per_head - 1)
        def _():
            o_ref[b, pl.ds(h, 1), :] = acc_ref[...] / l_ref[...]

        @pl.when(t + depth < total)
        def _():
            for c in page_copies(t + depth, slot):
                c.start()


def attn_step(cache_k, cache_v, layer, page_table, sel, q, k_new, v_new):
    bd, n_pages = page_table.shape
    n_heads = q.shape[1]
    pages_per_block = MOBA_BLOCK // PAGE_SIZE
    assert n_pages % pages_per_block == 0
    rows = PAGE_SIZE * n_heads
    as_rows = lambda c: c.reshape(c.shape[0], c.shape[1], rows, HEAD_DIM)
    vmem, smem = pl.BlockSpec(memory_space=pltpu.VMEM), pl.BlockSpec(memory_space=pltpu.SMEM)
    hbm = pl.BlockSpec(memory_space=pl.ANY)
    return pl.pallas_call(
        functools.partial(_attn_step_kernel, layer=layer, n_pages=n_pages, q_pos=n_pages * PAGE_SIZE,
                          pages_per_block=pages_per_block),
        in_specs=[smem, smem, vmem, vmem, vmem, hbm, hbm],
        out_specs=vmem,
        out_shape=jax.ShapeDtypeStruct((bd, n_heads, HEAD_DIM), F32),
        scratch_shapes=[pltpu.VMEM((PAGES_IN_FLIGHT, rows, HEAD_DIM), F32),
                        pltpu.VMEM((PAGES_IN_FLIGHT, rows, HEAD_DIM), F32),
                        pltpu.SemaphoreType.DMA((PAGES_IN_FLIGHT,)), pltpu.SemaphoreType.DMA((PAGES_IN_FLIGHT,)),
                        pltpu.VMEM((1, 1), F32), pltpu.VMEM((1, 1), F32), pltpu.VMEM((1, HEAD_DIM), F32)],
        compiler_params=pltpu.CompilerParams(vmem_limit_bytes=VMEM_LIMIT),
        name="attn_step",
    )(page_table.reshape(-1), sel.reshape(-1), q, k_new, v_new, as_rows(cache_k), as_rows(cache_v))


P_HALO = 16


def _pool_out_kernel(o_ref, u_ref, x_ref, pw_ref, ps_ref, wo_ref, y_ref, newp_ref, uext, mix):
    l = pl.program_id(1)
    tl = u_ref.shape[1]
    wp = u_ref.shape[2]
    wo_attn = o_ref.shape[2]
    pg = wp // len(POOL_WINDOWS)

    @pl.when(l == 0)
    def _():
        uext[0:P_HALO, :] = jnp.zeros((P_HALO, wp), F32)

    @pl.when(l > 0)
    def _():
        uext[0:P_HALO, :] = uext[tl:tl + P_HALO, :]

    uext[P_HALO:P_HALO + tl, :] = u_ref[0]
    o_bf16 = o_ref[0].astype(BF16)
    n_chunk = y_ref.shape[2] // len(POOL_WINDOWS)
    pos = l * tl + lax.broadcasted_iota(jnp.int32, (tl, pg), 0)
    for g, w in enumerate(POOL_WINDOWS):
        n0 = g * n_chunk
        y_ref[0, :, n0:n0 + n_chunk] = x_ref[0, :, n0:n0 + n_chunk] + jnp.dot(
            o_bf16, wo_ref[0:wo_attn, n0:n0 + n_chunk], preferred_element_type=F32)
        c0 = g * pg
        wsum = uext[P_HALO:P_HALO + tl, c0:c0 + pg]
        for j in range(1, w):
            wsum = wsum + uext[P_HALO - j:P_HALO - j + tl, c0:c0 + pg]
        count = jnp.minimum(pos + 1, w).astype(F32)
        dlt = (wsum / count - uext[P_HALO:P_HALO + tl, c0:c0 + pg]).astype(BF16)
        yp = jnp.dot(dlt, pw_ref[g], preferred_element_type=F32) * ps_ref[:, c0:c0 + pg]
        mix[:, c0:c0 + pg] = yp.astype(BF16)

    y_ref[0] += jnp.dot(mix[...], wo_ref[wo_attn:, :], preferred_element_type=F32)

    @pl.when(l == pl.num_programs(1) - 1)
    def _():
        newp_ref[0] = uext[P_HALO + tl - (POOL_MAX - 1):P_HALO + tl, :]


def pool_out(o, u, x, pool_w, pool_scale, w_out, tl):
    bsz, seq, d = x.shape
    wp = u.shape[-1]
    tl = min(tl, seq)
    assert seq % tl == 0 and tl >= P_HALO
    const = _resident
    return pl.pallas_call(
        _pool_out_kernel,
        grid=(bsz, seq // tl),
        in_specs=[pl.BlockSpec((1, tl, o.shape[-1]), lambda b, l: (b, l, 0)),
                  pl.BlockSpec((1, tl, wp), lambda b, l: (b, l, 0)),
                  pl.BlockSpec((1, tl, d), lambda b, l: (b, l, 0)),
                  const(pool_w.shape), const((1, wp)), const(w_out.shape)],
        out_specs=[pl.BlockSpec((1, tl, d), lambda b, l: (b, l, 0)),
                   pl.BlockSpec((1, POOL_MAX - 1, wp), lambda b, l: (b, 0, 0))],
        out_shape=[jax.ShapeDtypeStruct((bsz, seq, d), F32),
                   jax.ShapeDtypeStruct((bsz, POOL_MAX - 1, wp), F32)],
        scratch_shapes=[pltpu.VMEM((P_HALO + tl, wp), F32), pltpu.VMEM((tl, wp), BF16)],
        compiler_params=_params("parallel", "arbitrary"),
        name="pool_out",
    )(o, u, x, pool_w, pool_scale.reshape(1, wp), w_out)


def _pool_step_kernel(o_ref, u_ref, x_ref, stp_ref, pw_ref, ps_ref, wo_ref, y_ref, *, start_pos):
    wp = u_ref.shape[1]
    pg = wp // len(POOL_WINDOWS)
    u = u_ref[...]
    parts = [o_ref[...].astype(BF16)]
    for g, w in enumerate(POOL_WINDOWS):
        c0 = g * pg
        wsum = u[:, c0:c0 + pg]
        for j in range(1, w):
            wsum = wsum + stp_ref[POOL_MAX - 1 - j][:, c0:c0 + pg]
        count = float(min(start_pos + 1, w))
        dlt = (wsum / count - u[:, c0:c0 + pg]).astype(BF16)
        yp = jnp.dot(dlt, pw_ref[g], preferred_element_type=F32) * ps_ref[:, c0:c0 + pg]
        parts.append(yp.astype(BF16))
    mix = jnp.concatenate(parts, axis=-1)
    y_ref[...] = x_ref[...] + jnp.dot(mix, wo_ref[...], preferred_element_type=F32)


def pool_step(o, u, x, st_p, pool_w, pool_scale, w_out, start_pos):
    bd, d = x.shape
    return pl.pallas_call(
        functools.partial(_pool_step_kernel, start_pos=start_pos),
        out_shape=jax.ShapeDtypeStruct((bd, d), F32),
        compiler_params=pltpu.CompilerParams(vmem_limit_bytes=VMEM_LIMIT),
        name="pool_step",
    )(o, u, x, st_p, pool_w, pool_scale.reshape(1, -1), w_out)


TM_PROMPT = 512
TM_FFN = 1024
TF = 512
TL_MIX = 256


def kernel(x_prompt, x_sample, state_conv_a, state_conv_b, cache_k, cache_v, state_pool, page_table, conv_norm_g, conv_w_in, conv_a_dw, conv_a_dw_b, conv_a_ln_g, conv_a_ln_b, conv_b_dw, conv_w_out, attn_norm_g, attn_w_in, pool_w, pool_scale, attn_w_out, ffn_norm_g, ffn_w_gate, ffn_w_up, ffn_w_down, final_norm_g):
    bsz, seq, d = x_prompt.shape
    bd, dec_seq, _ = x_sample.shape
    assert dec_seq == 1
    depth = ffn_norm_g.shape[0]
    n_heads = cache_k.shape[3]
    wattn = n_heads * HEAD_DIM
    past_len = page_table.shape[1] * PAGE_SIZE

    layer_bf16 = lambda w, n: w[n].astype(BF16)

    xp = x_prompt.reshape(bsz * seq, d)
    xs = x_sample.reshape(bd, d)
    pa, pb, pk, pv, pp = [], [], [], [], []
    sa, sb, sk, sv, sp = [], [], [], [], []
    for layer in range(depth):
        i = layer // 2
        if layer % 2 == 0:
            w_in, w_out = layer_bf16(conv_w_in, i), layer_bf16(conv_w_out, i)
            parts = norm_matmul(xp, conv_norm_g[i], w_in, TM_PROMPT)
            wa = parts[0].shape[-1]
            y, na, nb_ = conv_mix([t.reshape(bsz, seq, wa) for t in parts], xp.reshape(bsz, seq, d),
                                  conv_a_dw[i], conv_a_dw_b[i], conv_a_ln_g[i], conv_a_ln_b[i], conv_b_dw[i],
                                  w_out, TL_MIX)
            xp = y.reshape(bsz * seq, d)
            pa.append(na)
            pb.append(nb_)

            parts = norm_matmul(xs, conv_norm_g[i], w_in, TM_PROMPT)
            st_a = jnp.swapaxes(state_conv_a[i], 0, 1)
            st_b = jnp.swapaxes(state_conv_b[i], 0, 1)
            xs, a_new, c_new = conv_step(parts, xs, st_a, st_b, conv_a_dw[i], conv_a_dw_b[i], conv_a_ln_g[i],
                                         conv_a_ln_b[i], conv_b_dw[i], w_out)
            sa.append(jnp.concatenate([state_conv_a[i][:, 1:], a_new[:, None]], axis=1))
            sb.append(jnp.concatenate([state_conv_b[i][:, 1:], c_new[:, None]], axis=1))
        else:
            w_in, w_out, w_pool = layer_bf16(attn_w_in, i), layer_bf16(attn_w_out, i), layer_bf16(pool_w, i)
            q, k, v, u = norm_matmul(xp, attn_norm_g[i], w_in, TM_PROMPT, head_major=3)
            shp = (n_heads, bsz, seq, HEAD_DIM)
            o, block_sums = moba_prompt(q.reshape(shp), k.reshape(shp), v.reshape(shp), cache_k, i, page_table)
            y, npool = pool_out(o, u.reshape(bsz, seq, -1), xp.reshape(bsz, seq, d), w_pool, pool_scale[i],
                                w_out, TL_MIX)
            xp = y.reshape(bsz * seq, d)
            pk.append(jnp.transpose(k.reshape(shp), (1, 2, 0, 3)))
            pv.append(jnp.transpose(v.reshape(shp), (1, 2, 0, 3)))
            pp.append(npool)

            q, k, v, u = norm_matmul(xs, attn_norm_g[i], w_in, TM_PROMPT, head_major=3)
            heads = lambda t: jnp.swapaxes(t, 0, 1)
            sel = gate_step(block_sums, heads(q))
            o = attn_step(cache_k, cache_v, i, page_table, sel, heads(q), heads(k), heads(v))
            st_p = jnp.swapaxes(state_pool[i], 0, 1)
            xs = pool_step(o.reshape(bd, wattn), u, xs, st_p, w_pool, pool_scale[i], w_out, past_len)
            sk.append(heads(k).reshape(bd, 1, n_heads, HEAD_DIM))
            sv.append(heads(v).reshape(bd, 1, n_heads, HEAD_DIM))
            sp.append(jnp.concatenate([state_pool[i][:, 1:], u[:, None]], axis=1))
        last = layer == depth - 1
        ffn_w = (ffn_w_gate, ffn_w_up, ffn_w_down, layer)
        xp, xs = ffn(xp, xs, ffn_norm_g[layer], *ffn_w, final_norm_g, TM_FFN, TF, last)
    return (xp.reshape(bsz, seq, d), xs.reshape(bd, 1, d), jnp.stack(pa), jnp.stack(sa), jnp.stack(pb),
            jnp.stack(sb), jnp.stack(pk), jnp.stack(sk), jnp.stack(pv), jnp.stack(sv), jnp.stack(pp), jnp.stack(sp))
```

```python
import functools

import jax
import jax.numpy as jnp
from jax import lax
from jax.experimental import pallas as pl
from jax.experimental.pallas import tpu as pltpu

EPS = 1e-6
PAGE_SIZE = 128
MOBA_BLOCK = 256
MOBA_TOPK = 3
HEAD_DIM = 128
POOL_WINDOWS = (2, 4, 8, 16)
POOL_MAX = max(POOL_WINDOWS)
MASKED = -1e30

V7X_VMEM_BYTES = 64 * 1024 * 1024
VMEM_LIMIT = V7X_VMEM_BYTES - 4 * 1024 * 1024
LANES = 128
SUBLANES = 8
COL_GROUP = 1024

BF16 = jnp.bfloat16
F32 = jnp.float32


def _params(*sem):
    return pltpu.CompilerParams(dimension_semantics=sem, vmem_limit_bytes=VMEM_LIMIT)


def _rms(x, g):
    ms = jnp.mean(x * x, axis=-1, keepdims=True)
    return x * lax.rsqrt(ms + EPS) * g


def _sigmoid(x):
    return 1.0 / (1.0 + jnp.exp(-x))


def _resident(shape):
    return pl.BlockSpec(shape, lambda *_: (0,) * len(shape), pipeline_mode=pl.Buffered(1))


def _norm_matmul_kernel(x_ref, g_ref, w_ref, *out_refs, head_major):
    h = _rms(x_ref[...], g_ref[...]).astype(BF16)
    for c, o_ref in enumerate(out_refs):
        r = jnp.dot(h, w_ref[:, c * COL_GROUP:(c + 1) * COL_GROUP], preferred_element_type=F32)
        if c < head_major:
            for hh in range(COL_GROUP // HEAD_DIM):
                o_ref[hh] = r[:, hh * HEAD_DIM:(hh + 1) * HEAD_DIM]
        else:
            o_ref[...] = r


def norm_matmul(x, g, w, tm, head_major=0):
    m, d = x.shape
    n_out = w.shape[1] // COL_GROUP
    tm = min(tm, m)
    heads = COL_GROUP // HEAD_DIM
    flat_spec = pl.BlockSpec((tm, COL_GROUP), lambda i: (i, 0))
    head_spec = pl.BlockSpec((heads, tm, HEAD_DIM), lambda i: (0, i, 0))
    flat_shape = jax.ShapeDtypeStruct((m, COL_GROUP), F32)
    head_shape = jax.ShapeDtypeStruct((heads, m, HEAD_DIM), F32)
    return pl.pallas_call(
        functools.partial(_norm_matmul_kernel, head_major=head_major),
        grid=(m // tm,),
        in_specs=[pl.BlockSpec((tm, d), lambda i: (i, 0)), _resident((1, d)), _resident(w.shape)],
        out_specs=[head_spec] * head_major + [flat_spec] * (n_out - head_major),
        out_shape=[head_shape] * head_major + [flat_shape] * (n_out - head_major),
        compiler_params=_params("parallel"),
        name="norm_matmul",
    )(x, g.reshape(1, d), w)


def _ffn_kernel(g_ref, fg_ref, xs_ref, wg_ref, wu_ref, wd_ref, x_hbm, o_ref, os_ref, xbuf, h_ref, sem, *, final_norm):
    i, f = pl.program_id(0), pl.program_id(1)
    tm = o_ref.shape[0]
    ns = xs_ref.shape[0]
    last_f = pl.num_programs(1) - 1

    def x_copy(tile):
        return pltpu.make_async_copy(x_hbm.at[pl.ds(pl.multiple_of(tile * tm, tm), tm), :], xbuf, sem)

    @pl.when((i == 0) & (f == 0))
    def _():
        x_copy(0).start()
        xs = xs_ref[...]
        h_ref[tm:, :] = jnp.zeros((h_ref.shape[0] - tm, h_ref.shape[1]), h_ref.dtype)
        h_ref[tm:tm + ns, :] = _rms(xs, g_ref[...]).astype(h_ref.dtype)
        os_ref[...] = xs

    @pl.when(f == 0)
    def _():
        x_copy(i).wait()
        x = xbuf[...]
        h_ref[0:tm, :] = _rms(x, g_ref[...]).astype(h_ref.dtype)
        o_ref[...] = x

    @pl.when((f == 1) & (i + 1 < pl.num_programs(0)))
    def _():
        x_copy(i + 1).start()

    h = h_ref[...]
    gate = jnp.dot(h, wg_ref[...].astype(BF16), preferred_element_type=F32)
    up = jnp.dot(h, wu_ref[...].astype(BF16), preferred_element_type=F32)
    act = (gate * _sigmoid(gate) * up).astype(BF16)
    down = jnp.dot(act, wd_ref[...].astype(BF16), preferred_element_type=F32)
    o_ref[...] += down[0:tm]

    @pl.when(i == 0)
    def _():
        os_ref[...] += down[tm:tm + ns]

    if final_norm:
        @pl.when(f == last_f)
        def _():
            o_ref[...] = _rms(o_ref[...], fg_ref[...])

        @pl.when((i == 0) & (f == last_f))
        def _():
            os_ref[...] = _rms(os_ref[...], fg_ref[...])


BF16_ROWS = 16


def ffn(x, xs, g, wg, wu, wd, layer, final_g, tm, tf, final_norm):
    m, d = x.shape
    ns = xs.shape[0]
    dff = wg.shape[2]
    tm = min(tm, m)
    assert m % tm == 0 and dff % tf == 0 and dff // tf >= 2 and ns <= BF16_ROWS and tm % BF16_ROWS == 0
    return pl.pallas_call(
        functools.partial(_ffn_kernel, final_norm=final_norm),
        grid=(m // tm, dff // tf),
        in_specs=[_resident((1, d)), _resident((1, d)), _resident((ns, d)),
                  pl.BlockSpec((None, d, tf), lambda i, f: (layer, 0, f)),
                  pl.BlockSpec((None, d, tf), lambda i, f: (layer, 0, f)),
                  pl.BlockSpec((None, tf, d), lambda i, f: (layer, f, 0)),
                  pl.BlockSpec(memory_space=pl.ANY)],
        out_specs=[pl.BlockSpec((tm, d), lambda i, f: (i, 0)), pl.BlockSpec((ns, d), lambda i, f: (0, 0))],
        out_shape=[jax.ShapeDtypeStruct((m, d), F32), jax.ShapeDtypeStruct((ns, d), F32)],
        scratch_shapes=[pltpu.VMEM((tm, d), F32), pltpu.VMEM((tm + BF16_ROWS, d), BF16),
                        pltpu.SemaphoreType.DMA(())],
        compiler_params=_params("arbitrary", "arbitrary"),
        name="ffn",
    )(g.reshape(1, d), final_g.reshape(1, d), xs, wg, wu, wd, x)


A_HALO = 32
B_HALO = 8


def _layer_norm_silu(x, g, b):
    mu = jnp.mean(x, axis=-1, keepdims=True)
    xc = x - mu
    y = xc * lax.rsqrt(jnp.mean(xc * xc, axis=-1, keepdims=True) + EPS) * g + b
    return y * _sigmoid(y)


def _conv_mix_kernel(av_ref, ag_ref, bh_ref, bb_ref, bc_ref, x_ref, adw_ref, adwb_ref, lng_ref, lnb_ref,
                     bdw_ref, wo_ref, y_ref, newa_ref, newb_ref, aext, cext, ashift, aconv, *, ta, tb, rc):
    l = pl.program_id(1)
    tl = av_ref.shape[1]
    wa = av_ref.shape[2]

    @pl.when(l == 0)
    def _():
        aext[0:A_HALO, :] = jnp.zeros((A_HALO, wa), F32)
        cext[0:B_HALO, :] = jnp.zeros((B_HALO, wa), F32)

    @pl.when(l > 0)
    def _():
        aext[0:A_HALO, :] = aext[tl:tl + A_HALO, :]
        cext[0:B_HALO, :] = cext[tl:tl + B_HALO, :]

    aext[A_HALO:A_HALO + tl, :] = av_ref[0] * _sigmoid(ag_ref[0])
    cext[B_HALO:B_HALO + tl, :] = bc_ref[0] * bh_ref[0]

    bconv = jnp.zeros((tl, wa), F32)
    for k in range(tb):
        off = B_HALO - (tb - 1) + k
        bconv = bconv + bdw_ref[k:k + 1, :] * cext[off:off + tl, :]
    b_out = (bb_ref[0] * bconv).astype(BF16)
    y_ref[0] = x_ref[0] + jnp.dot(b_out, wo_ref[wa:, :], preferred_element_type=F32)

    n_shift = ashift.shape[1]
    for s in range(1, SUBLANES):
        ashift[s - 1] = aext[s:s + n_shift, :]

    for r0 in range(0, tl, rc):
        for c0 in range(0, wa, LANES):
            acc = jnp.broadcast_to(adwb_ref[:, c0:c0 + LANES], (rc, LANES))
            for k in range(ta):
                off = A_HALO - (ta - 1) + r0 + k
                s = off % SUBLANES
                src = aext if s == 0 else ashift.at[s - 1]
                acc = acc + adw_ref[k:k + 1, c0:c0 + LANES] * src[off - s:off - s + rc, c0:c0 + LANES]
            aconv[r0:r0 + rc, c0:c0 + LANES] = acc
    a_out = _layer_norm_silu(aconv[...], lng_ref[...], lnb_ref[...]).astype(BF16)
    y_ref[0] += jnp.dot(a_out, wo_ref[0:wa, :], preferred_element_type=F32)

    @pl.when(l == pl.num_programs(1) - 1)
    def _():
        newa_ref[0] = aext[A_HALO + tl - (ta - 1):A_HALO + tl, :]
        newb_ref[0] = cext[B_HALO + tl - (tb - 1):B_HALO + tl, :]


def conv_mix(parts, x, a_dw, a_dw_b, ln_g, ln_b, b_dw, w_out, tl):
    bsz, seq, d = x.shape
    wa = parts[0].shape[-1]
    ta, tb = a_dw.shape[0], b_dw.shape[0]
    tl = min(tl, seq)
    assert seq % tl == 0 and tl >= A_HALO and ta - 1 <= A_HALO and tb - 1 <= B_HALO
    part_spec = pl.BlockSpec((1, tl, wa), lambda b, l: (b, l, 0))
    const = _resident
    return pl.pallas_call(
        functools.partial(_conv_mix_kernel, ta=ta, tb=tb, rc=min(128, tl)),
        grid=(bsz, seq // tl),
        in_specs=[part_spec] * 5 + [pl.BlockSpec((1, tl, d), lambda b, l: (b, l, 0)),
                                    const((ta, wa)), const((1, wa)), const((1, wa)), const((1, wa)),
                                    const((tb, wa)), const(w_out.shape)],
        out_specs=[pl.BlockSpec((1, tl, d), lambda b, l: (b, l, 0)),
                   pl.BlockSpec((1, ta - 1, wa), lambda b, l: (b, 0, 0)),
                   pl.BlockSpec((1, tb - 1, wa), lambda b, l: (b, 0, 0))],
        out_shape=[jax.ShapeDtypeStruct((bsz, seq, d), F32),
                   jax.ShapeDtypeStruct((bsz, ta - 1, wa), F32),
                   jax.ShapeDtypeStruct((bsz, tb - 1, wa), F32)],
        scratch_shapes=[pltpu.VMEM((A_HALO + tl, wa), F32), pltpu.VMEM((B_HALO + tl, wa), F32),
                        pltpu.VMEM((SUBLANES - 1, A_HALO + tl - SUBLANES, wa), F32),
                        pltpu.VMEM((tl, wa), F32)],
        compiler_params=_params("parallel", "arbitrary"),
        name="conv_mix",
    )(*parts, x, a_dw, a_dw_b.reshape(1, wa), ln_g.reshape(1, wa), ln_b.reshape(1, wa), b_dw, w_out)


def _conv_step_kernel(av_ref, ag_ref, bh_ref, bb_ref, bc_ref, x_ref, sta_ref, stb_ref, adw_ref, adwb_ref,
                      lng_ref, lnb_ref, bdw_ref, wo_ref, y_ref, anew_ref, cnew_ref, *, ta, tb):
    a = av_ref[...] * _sigmoid(ag_ref[...])
    c = bc_ref[...] * bh_ref[...]
    anew_ref[...] = a
    cnew_ref[...] = c
    acc = adwb_ref[...] + adw_ref[ta - 1:ta, :] * a
    for k in range(ta - 1):
        acc = acc + adw_ref[k:k + 1, :] * sta_ref[k]
    a_out = _layer_norm_silu(acc, lng_ref[...], lnb_ref[...])
    bconv = bdw_ref[tb - 1:tb, :] * c
    for k in range(tb - 1):
        bconv = bconv + bdw_ref[k:k + 1, :] * stb_ref[k]
    mix = jnp.concatenate([a_out, bb_ref[...] * bconv], axis=-1).astype(BF16)
    y_ref[...] = x_ref[...] + jnp.dot(mix, wo_ref[...], preferred_element_type=F32)


def conv_step(parts, x, st_a, st_b, a_dw, a_dw_b, ln_g, ln_b, b_dw, w_out):
    bd, d = x.shape
    wa = parts[0].shape[-1]
    ta, tb = a_dw.shape[0], b_dw.shape[0]
    return pl.pallas_call(
        functools.partial(_conv_step_kernel, ta=ta, tb=tb),
        out_shape=[jax.ShapeDtypeStruct((bd, d), F32), jax.ShapeDtypeStruct((bd, wa), F32),
                   jax.ShapeDtypeStruct((bd, wa), F32)],
        compiler_params=pltpu.CompilerParams(vmem_limit_bytes=VMEM_LIMIT),
        name="conv_step",
    )(*parts, x, st_a, st_b, a_dw, a_dw_b.reshape(1, wa), ln_g.reshape(1, wa), ln_b.reshape(1, wa), b_dw, w_out)


LOG2E = 1.4426950408889634
TQ = 2 * MOBA_BLOCK
DA = 2 * HEAD_DIM


def _split3(x):
    def top_bits(v):
        return lax.bitcast_convert_type(lax.bitcast_convert_type(v, jnp.uint32) & jnp.uint32(0xFFFF0000), F32)

    hi = top_bits(x)
    mid = top_bits(x - hi)
    return hi, mid, x - hi - mid


def _moba_setup(q_ref, k_ref, v_ref, qaug, kaug, vaug, kmean, slope2, nb):
    seq = k_ref.shape[1]
    bs = MOBA_BLOCK
    nbp = kmean.shape[0]
    k = k_ref[0]
    q = q_ref[0]
    kmean[...] = jnp.zeros_like(kmean)
    kmean[0:nb, :] = jnp.mean(k.reshape(nb, bs, HEAD_DIM), axis=1)

    gate = lax.dot_general(kmean[...], q, (((1,), (1,)), ((), ())), precision=lax.Precision.HIGHEST,
                           preferred_element_type=F32)
    blk = lax.broadcasted_iota(jnp.int32, (nbp, seq), 0)
    own = lax.broadcasted_iota(jnp.int32, (nbp, seq), 1) // bs
    gate = jnp.where(blk < own, gate, -jnp.inf)
    attend = blk == own
    for _ in range(MOBA_TOPK):
        m = jnp.max(gate, axis=0, keepdims=True)
        idx = jnp.min(jnp.where(gate == m, blk, nbp), axis=0, keepdims=True)
        hit = blk == idx
        attend = attend | (hit & (blk < own))
        gate = jnp.where(hit, -jnp.inf, gate)
    attend_t = jnp.where(attend, 1.0, 0.0)
    attend_r = jnp.concatenate([attend_t, jnp.zeros((LANES - nbp, seq), F32)], axis=0).T

    lane = lax.broadcasted_iota(jnp.int32, (seq, LANES), 1)
    pos_i = lax.broadcasted_iota(jnp.int32, (seq, LANES), 0)
    pos = pos_i.astype(F32)

    khi, kmid, klo = _split3(slope2 * pos)
    ek = jnp.where(lane == pos_i // bs, 1.0, 0.0)
    ek = jnp.where(lane == nb, khi, ek)
    ek = jnp.where(lane == nb + 1, kmid, ek)
    ek = jnp.where(lane == nb + 2, klo, ek)
    ek = jnp.where((lane >= nb + 3) & (lane < nb + 6), 1.0, ek)
    kaug[:, 0:HEAD_DIM] = k.astype(BF16)
    kaug[:, HEAD_DIM:] = ek.astype(BF16)

    qhi, qmid, qlo = _split3(-slope2 * pos)
    eq = jnp.where(lane < nb, jnp.where(attend_r > 0.5, 0.0, MASKED), 0.0)
    eq = jnp.where((lane >= nb) & (lane < nb + 3), 1.0, eq)
    eq = jnp.where(lane == nb + 3, qhi, eq)
    eq = jnp.where(lane == nb + 4, qmid, eq)
    eq = jnp.where(lane == nb + 5, qlo, eq)
    qaug[:, 0:HEAD_DIM] = (q * (HEAD_DIM ** -0.5 * LOG2E)).astype(BF16)
    qaug[:, HEAD_DIM:] = eq.astype(BF16)

    vaug[:, 0:HEAD_DIM] = v_ref[0].astype(BF16)
    vaug[:, HEAD_DIM:] = jnp.ones((seq, HEAD_DIM), BF16)


SUM_CHAINS = 16
PAGE_LOOKAHEAD = 2


def _page_sums_step(step, n_steps, pt_ref, kc_hbm, ksum_ref, ring, sem, *, layer, pages_per_block):
    total = pt_ref.shape[0]
    depth, page_rows, n_heads, _ = ring.shape
    per = depth // PAGE_LOOKAHEAD

    def page_copy(t, slot):
        return pltpu.make_async_copy(kc_hbm.at[layer, pt_ref[t]], ring.at[slot], sem.at[slot])

    @pl.when(step == 0)
    def _():
        ksum_ref[...] = jnp.zeros_like(ksum_ref)
        for slot in range(min(depth, total)):
            page_copy(slot, slot).start()

    def fold(t, slot):
        page_copy(t, slot).wait()
        page = ring[slot].reshape(SUM_CHAINS, page_rows // SUM_CHAINS, n_heads, HEAD_DIM)
        ksum_ref[t // pages_per_block] += jnp.sum(jnp.sum(page, axis=1), axis=0)

    first_slot = (step % PAGE_LOOKAHEAD) * per
    if n_steps * per == total:
        for j in range(per):
            fold(step * per + j, first_slot + j)

        @pl.when(step + PAGE_LOOKAHEAD < n_steps)
        def _():
            for j in range(per):
                page_copy(step * per + j + depth, first_slot + j).start()
    else:
        for j in range(per):
            t = step * per + j

            @pl.when(t < total)
            def _(t=t, j=j):
                fold(t, first_slot + j)

                @pl.when(t + depth < total)
                def _():
                    page_copy(t + depth, first_slot + j).start()


def _moba_kernel(pt_ref, q_ref, k_ref, v_ref, kc_hbm, o_ref, ksum_ref, qaug, kaug, vaug, kmean, m_ref, acc_ref,
                 s_ref, rel_ref, ring, psem, *, n_heads, nb, layer, page_pairs, page_steps):
    h = pl.program_id(1)
    bs = MOBA_BLOCK
    nt = (((1,), (1,)), ((), ()))
    seq = qaug.shape[0]
    n_tiles = seq // TQ
    n_units = n_tiles * (n_tiles + 1) // 2
    n_pairs = n_units // 2
    bh = pl.program_id(0) * pl.num_programs(1) + h

    slope2 = jnp.exp2((h + 1).astype(F32) * (-8.0 / n_heads) + jnp.zeros((1, 1), F32)) * LOG2E
    _moba_setup(q_ref, k_ref, v_ref, qaug, kaug, vaug, kmean, slope2, nb)
    rel_ref[...] = (lax.broadcasted_iota(jnp.int32, (bs, bs), 1) - lax.broadcasted_iota(jnp.int32, (bs, bs), 0))

    halves = (slice(0, bs), slice(bs, TQ))

    def tile_rows(t):
        return pl.ds(pl.multiple_of(t * TQ, TQ), TQ)

    def successor(unit):
        i, g = unit
        wrap = g == i
        return jnp.where(wrap, jnp.minimum(i + 1, n_tiles - 1), i), jnp.where(wrap, 0, g + 1)

    def scores(unit):
        i, g = unit
        qa = qaug[tile_rows(i), :]
        kg = kaug[tile_rows(g), :]
        return jnp.concatenate([lax.dot_general(qa[r], kg, nt, preferred_element_type=F32) for r in halves], axis=0)

    def process(unit, slot):
        i, g = unit
        vg = vaug[tile_rows(g), :]
        for r in halves:
            diag = jnp.where(rel_ref[...] <= (i - g) * TQ, s_ref[slot, r, r], MASKED)
            sr = jnp.concatenate([diag, s_ref[slot, r, bs:]] if r.start == 0 else [s_ref[slot, r, 0:bs], diag], axis=1)
            m = jnp.where(g == 0, MASKED, m_ref[r, :])
            m_new = jnp.maximum(m, jnp.max(sr, axis=1, keepdims=True))
            p = jnp.exp2(sr - m_new).astype(BF16)
            pv = jnp.dot(p, vg, preferred_element_type=F32)
            acc = jnp.exp2(m - m_new) * acc_ref[r, :] + pv
            acc_ref[r, :] = acc
            m_ref[r, :] = m_new
            rows = pl.ds(pl.multiple_of(i * TQ + r.start, bs), bs)
            o_ref[0, rows, :] = acc[:, 0:HEAD_DIM] / acc[:, HEAD_DIM:]

    m_ref[...] = jnp.full(m_ref.shape, MASKED, F32)
    acc_ref[...] = jnp.zeros_like(acc_ref)
    first = (jnp.int32(0), jnp.int32(0))
    s_ref[0] = scores(first)

    def pair(p, unit):
        @pl.when(p < page_pairs)
        def _():
            _page_sums_step(bh * page_pairs + p, page_steps, pt_ref, kc_hbm, ksum_ref, ring, psem, layer=layer,
                            pages_per_block=bs // PAGE_SIZE)

        nxt = successor(unit)
        s_ref[1] = scores(nxt)
        process(unit, 0)
        nxt2 = successor(nxt)
        s_ref[0] = scores(nxt2)
        process(nxt, 1)
        return nxt2

    last = lax.fori_loop(0, n_pairs, pair, first)
    if n_units % 2:
        process(last, 0)


def moba_prompt(q, k, v, cache_k, layer, page_table):
    n_heads, bsz, seq, _ = q.shape
    width = n_heads * HEAD_DIM
    assert seq % TQ == 0
    nb = seq // MOBA_BLOCK
    nbp = -(-nb // 8) * 8
    assert nb + 6 <= LANES
    bd, n_pages = page_table.shape
    pages_per_block = MOBA_BLOCK // PAGE_SIZE
    assert n_pages % pages_per_block == 0
    n_tiles = seq // TQ
    n_pairs = n_tiles * (n_tiles + 1) // 4
    assert n_pairs >= 1
    total_pages = bd * n_pages
    even = [p for p in range(1, n_pairs + 1) if total_pages % (bsz * n_heads * p) == 0]
    page_pairs = max(even) if even else n_pairs
    page_steps = bsz * n_heads * page_pairs
    pages_per_step = -(-total_pages // page_steps)
    sums_shape = (total_pages // pages_per_block, n_heads, HEAD_DIM)
    slab = pl.BlockSpec((None, 1, seq, HEAD_DIM), lambda b, h: (h, b, 0, 0))
    return pl.pallas_call(
        functools.partial(_moba_kernel, n_heads=n_heads, nb=nb, layer=layer, page_pairs=page_pairs,
                          page_steps=page_steps),
        grid=(bsz, n_heads),
        in_specs=[pl.BlockSpec(memory_space=pltpu.SMEM), slab, slab, slab, pl.BlockSpec(memory_space=pl.ANY)],
        out_specs=[pl.BlockSpec((1, seq, HEAD_DIM), lambda b, h: (b, 0, h)),
                   pl.BlockSpec(sums_shape, lambda b, h: (0, 0, 0))],
        out_shape=[jax.ShapeDtypeStruct((bsz, seq, width), F32), jax.ShapeDtypeStruct(sums_shape, F32)],
        scratch_shapes=[pltpu.VMEM((seq, DA), BF16), pltpu.VMEM((seq, DA), BF16), pltpu.VMEM((seq, DA), BF16),
                        pltpu.VMEM((nbp, HEAD_DIM), F32), pltpu.VMEM((TQ, 1), F32), pltpu.VMEM((TQ, DA), F32),
                        pltpu.VMEM((2, TQ, TQ), F32), pltpu.VMEM((MOBA_BLOCK, MOBA_BLOCK), jnp.int32),
                        pltpu.VMEM((PAGE_LOOKAHEAD * pages_per_step, PAGE_SIZE, n_heads, HEAD_DIM), F32),
                        pltpu.SemaphoreType.DMA((PAGE_LOOKAHEAD * pages_per_step,))],
        compiler_params=_params("arbitrary", "arbitrary"),
        name="moba_prompt",
    )(page_table.reshape(-1), q, k, v, cache_k)


PAGES_IN_FLIGHT = 8


def _gate_step_kernel(q_ref, ksum_ref, sel_ref, *, n_heads):
    bd = q_ref.shape[0]
    nblk = ksum_ref.shape[0] // bd
    sel_ref[...] = jnp.zeros_like(sel_ref)
    for b in range(bd):
        kmean = ksum_ref[b * nblk:(b + 1) * nblk] * (1.0 / MOBA_BLOCK)
        gate = jnp.sum(kmean * q_ref[b][None], axis=-1)
        blk = lax.broadcasted_iota(jnp.int32, gate.shape, 0)
        for r in range(MOBA_TOPK):
            m = jnp.max(gate, axis=0, keepdims=True)
            idx = jnp.min(jnp.where(gate == m, blk, nblk), axis=0, keepdims=True)
            sel_ref[b, r:r + 1, 0:n_heads] = idx
            gate = jnp.where(blk == idx, -jnp.inf, gate)


SEL_ROWS = 8


def gate_step(block_sums, q):
    bd, n_heads, _ = q.shape
    assert MOBA_TOPK <= SEL_ROWS and n_heads <= LANES
    sel = pl.pallas_call(
        functools.partial(_gate_step_kernel, n_heads=n_heads),
        out_shape=jax.ShapeDtypeStruct((bd, SEL_ROWS, LANES), jnp.int32),
        compiler_params=pltpu.CompilerParams(vmem_limit_bytes=VMEM_LIMIT),
        name="gate_step",
    )(q, block_sums)
    return sel[:, :MOBA_TOPK, :n_heads]


def _attn_step_kernel(pt_ref, sel_ref, q_ref, kn_ref, vn_ref, k_hbm, v_hbm, o_ref, kbuf, vbuf, ksem, vsem,
                      m_ref, l_ref, acc_ref, *, layer, n_pages, q_pos, pages_per_block):
    bd, n_heads, _ = q_ref.shape
    per_head = MOBA_TOPK * pages_per_block
    total = bd * n_heads * per_head
    depth = kbuf.shape[0]
    scale = HEAD_DIM ** -0.5
    own = q_pos // MOBA_BLOCK

    def coords(t):
        bh, j = t // per_head, t % per_head
        b, h = bh // n_heads, bh % n_heads
        return b, h, j, sel_ref[(b * MOBA_TOPK + j // pages_per_block) * n_heads + h]

    def page_copies(t, slot):
        b, _, j, n = coords(t)
        page = pt_ref[b * n_pages + n * pages_per_block + j % pages_per_block]
        return (pltpu.make_async_copy(k_hbm.at[layer, page], kbuf.at[slot], ksem.at[slot]),
                pltpu.make_async_copy(v_hbm.at[layer, page], vbuf.at[slot], vsem.at[slot]))

    for t in range(min(depth, total)):
        for c in page_copies(t, t):
            c.start()

    @pl.loop(0, total)
    def _(t):
        slot = t % depth
        b, h, j, n = coords(t)
        for c in page_copies(t, slot):
            c.wait()
        q = q_ref[b, pl.ds(h, 1), :]

        @pl.when(j == 0)
        def _():
            m_ref[...] = jnp.sum(q * kn_ref[b, pl.ds(h, 1), :], axis=1, keepdims=True) * scale
            l_ref[...] = jnp.ones_like(l_ref)
            acc_ref[...] = vn_ref[b, pl.ds(h, 1), :]

        k = kbuf[slot, pl.ds(h, PAGE_SIZE, stride=n_heads), :]
        v = vbuf[slot, pl.ds(h, PAGE_SIZE, stride=n_heads), :]
        slope = jnp.exp2(jnp.asarray(h + 1, F32) * (-8.0 / n_heads) + jnp.zeros((1, 1), F32))
        kpos = (n * MOBA_BLOCK + (j % pages_per_block) * PAGE_SIZE
                + lax.broadcasted_iota(jnp.int32, (PAGE_SIZE, 1), 0))
        dist = (q_pos - kpos).astype(F32)
        s = jnp.sum(k * q, axis=1, keepdims=True) * scale - slope * dist
        s = jnp.where((dist >= 0.0) & (n < own), s, MASKED)
        m = m_ref[...]
        m_new = jnp.maximum(m, jnp.max(s, axis=0, keepdims=True))
        alpha = jnp.exp(m - m_new)
        p = jnp.exp(s - m_new)
        l_ref[...] = alpha * l_ref[...] + jnp.sum(p, axis=0, keepdims=True)
        acc_ref[...] = alpha * acc_ref[...] + jnp.sum(p * v, axis=0, keepdims=True)
        m_ref[...] = m_new

        @pl.when(j == per_head - 1)
        def _():
            o_ref[b, pl.ds(h, 1), :] = acc_ref[...] / l_ref[...]

        @pl.when(t + depth < total)
        def _():
            for c in page_copies(t + depth, slot):
                c.start()


def attn_step(cache_k, cache_v, layer, page_table, sel, q, k_new, v_new):
    bd, n_pages = page_table.shape
    n_heads = q.shape[1]
    pages_per_block = MOBA_BLOCK // PAGE_SIZE
    assert n_pages % pages_per_block == 0
    rows = PAGE_SIZE * n_heads
    as_rows = lambda c: c.reshape(c.shape[0], c.shape[1], rows, HEAD_DIM)
    vmem, smem = pl.BlockSpec(memory_space=pltpu.VMEM), pl.BlockSpec(memory_space=pltpu.SMEM)
    hbm = pl.BlockSpec(memory_space=pl.ANY)
    return pl.pallas_call(
        functools.partial(_attn_step_kernel, layer=layer, n_pages=n_pages, q_pos=n_pages * PAGE_SIZE,
                          pages_per_block=pages_per_block),
        in_specs=[smem, smem, vmem, vmem, vmem, hbm, hbm],
        out_specs=vmem,
        out_shape=jax.ShapeDtypeStruct((bd, n_heads, HEAD_DIM), F32),
        scratch_shapes=[pltpu.VMEM((PAGES_IN_FLIGHT, rows, HEAD_DIM), F32),
                        pltpu.VMEM((PAGES_IN_FLIGHT, rows, HEAD_DIM), F32),
                        pltpu.SemaphoreType.DMA((PAGES_IN_FLIGHT,)), pltpu.SemaphoreType.DMA((PAGES_IN_FLIGHT,)),
                        pltpu.VMEM((1, 1), F32), pltpu.VMEM((1, 1), F32), pltpu.VMEM((1, HEAD_DIM), F32)],
        compiler_params=pltpu.CompilerParams(vmem_limit_bytes=VMEM_LIMIT),
        name="attn_step",
    )(page_table.reshape(-1), sel.reshape(-1), q, k_new, v_new, as_rows(cache_k), as_rows(cache_v))


P_HALO = 16


def _pool_out_kernel(o_ref, u_ref, x_ref, pw_ref, ps_ref, wo_ref, y_ref, newp_ref, uext, mix):
    l = pl.program_id(1)
    tl = u_ref.shape[1]
    wp = u_ref.shape[2]
    wo_attn = o_ref.shape[2]
    pg = wp // len(POOL_WINDOWS)

    @pl.when(l == 0)
    def _():
        uext[0:P_HALO, :] = jnp.zeros((P_HALO, wp), F32)

    @pl.when(l > 0)
    def _():
        uext[0:P_HALO, :] = uext[tl:tl + P_HALO, :]

    uext[P_HALO:P_HALO + tl, :] = u_ref[0]
    o_bf16 = o_ref[0].astype(BF16)
    n_chunk = y_ref.shape[2] // len(POOL_WINDOWS)
    pos = l * tl + lax.broadcasted_iota(jnp.int32, (tl, pg), 0)
    for g, w in enumerate(POOL_WINDOWS):
        n0 = g * n_chunk
        y_ref[0, :, n0:n0 + n_chunk] = x_ref[0, :, n0:n0 + n_chunk] + jnp.dot(
            o_bf16, wo_ref[0:wo_attn, n0:n0 + n_chunk], preferred_element_type=F32)
        c0 = g * pg
        wsum = uext[P_HALO:P_HALO + tl, c0:c0 + pg]
        for j in range(1, w):
            wsum = wsum + uext[P_HALO - j:P_HALO - j + tl, c0:c0 + pg]
        count = jnp.minimum(pos + 1, w).astype(F32)
        dlt = (wsum / count - uext[P_HALO:P_HALO + tl, c0:c0 + pg]).astype(BF16)
        yp = jnp.dot(dlt, pw_ref[g], preferred_element_type=F32) * ps_ref[:, c0:c0 + pg]
        mix[:, c0:c0 + pg] = yp.astype(BF16)

    y_ref[0] += jnp.dot(mix[...], wo_ref[wo_attn:, :], preferred_element_type=F32)

    @pl.when(l == pl.num_programs(1) - 1)
    def _():
        newp_ref[0] = uext[P_HALO + tl - (POOL_MAX - 1):P_HALO + tl, :]


def pool_out(o, u, x, pool_w, pool_scale, w_out, tl):
    bsz, seq, d = x.shape
    wp = u.shape[-1]
    tl = min(tl, seq)
    assert seq % tl == 0 and tl >= P_HALO
    const = _resident
    return pl.pallas_call(
        _pool_out_kernel,
        grid=(bsz, seq // tl),
        in_specs=[pl.BlockSpec((1, tl, o.shape[-1]), lambda b, l: (b, l, 0)),
                  pl.BlockSpec((1, tl, wp), lambda b, l: (b, l, 0)),
                  pl.BlockSpec((1, tl, d), lambda b, l: (b, l, 0)),
                  const(pool_w.shape), const((1, wp)), const(w_out.shape)],
        out_specs=[pl.BlockSpec((1, tl, d), lambda b, l: (b, l, 0)),
                   pl.BlockSpec((1, POOL_MAX - 1, wp), lambda b, l: (b, 0, 0))],
        out_shape=[jax.ShapeDtypeStruct((bsz, seq, d), F32),
                   jax.ShapeDtypeStruct((bsz, POOL_MAX - 1, wp), F32)],
        scratch_shapes=[pltpu.VMEM((P_HALO + tl, wp), F32), pltpu.VMEM((tl, wp), BF16)],
        compiler_params=_params("parallel", "arbitrary"),
        name="pool_out",
    )(o, u, x, pool_w, pool_scale.reshape(1, wp), w_out)


def _pool_step_kernel(o_ref, u_ref, x_ref, stp_ref, pw_ref, ps_ref, wo_ref, y_ref, *, start_pos):
    wp = u_ref.shape[1]
    pg = wp // len(POOL_WINDOWS)
    u = u_ref[...]
    parts = [o_ref[...].astype(BF16)]
    for g, w in enumerate(POOL_WINDOWS):
        c0 = g * pg
        wsum = u[:, c0:c0 + pg]
        for j in range(1, w):
            wsum = wsum + stp_ref[POOL_MAX - 1 - j][:, c0:c0 + pg]
        count = float(min(start_pos + 1, w))
        dlt = (wsum / count - u[:, c0:c0 + pg]).astype(BF16)
        yp = jnp.dot(dlt, pw_ref[g], preferred_element_type=F32) * ps_ref[:, c0:c0 + pg]
        parts.append(yp.astype(BF16))
    mix = jnp.concatenate(parts, axis=-1)
    y_ref[...] = x_ref[...] + jnp.dot(mix, wo_ref[...], preferred_element_type=F32)


def pool_step(o, u, x, st_p, pool_w, pool_scale, w_out, start_pos):
    bd, d = x.shape
    return pl.pallas_call(
        functools.partial(_pool_step_kernel, start_pos=start_pos),
        out_shape=jax.ShapeDtypeStruct((bd, d), F32),
        compiler_params=pltpu.CompilerParams(vmem_limit_bytes=VMEM_LIMIT),
        name="pool_step",
    )(o, u, x, st_p, pool_w, pool_scale.reshape(1, -1), w_out)


TM_PROMPT = 512
TM_FFN = 1024
TF = 512
TL_MIX = 256


def kernel(x_prompt, x_sample, state_conv_a, state_conv_b, cache_k, cache_v, state_pool, page_table, conv_norm_g, conv_w_in, conv_a_dw, conv_a_dw_b, conv_a_ln_g, conv_a_ln_b, conv_b_dw, conv_w_out, attn_norm_g, attn_w_in, pool_w, pool_scale, attn_w_out, ffn_norm_g, ffn_w_gate, ffn_w_up, ffn_w_down, final_norm_g):
    bsz, seq, d = x_prompt.shape
    bd, dec_seq, _ = x_sample.shape
    assert dec_seq == 1
    depth = ffn_norm_g.shape[0]
    n_heads = cache_k.shape[3]
    wattn = n_heads * HEAD_DIM
    past_len = page_table.shape[1] * PAGE_SIZE

    layer_bf16 = lambda w, n: w[n].astype(BF16)

    xp = x_prompt.reshape(bsz * seq, d)
    xs = x_sample.reshape(bd, d)
    pa, pb, pk, pv, pp = [], [], [], [], []
    sa, sb, sk, sv, sp = [], [], [], [], []
    for layer in range(depth):
        i = layer // 2
        if layer % 2 == 0:
            w_in, w_out = layer_bf16(conv_w_in, i), layer_bf16(conv_w_out, i)
            parts = norm_matmul(xp, conv_norm_g[i], w_in, TM_PROMPT)
            wa = parts[0].shape[-1]
            y, na, nb_ = conv_mix([t.reshape(bsz, seq, wa) for t in parts], xp.reshape(bsz, seq, d),
                                  conv_a_dw[i], conv_a_dw_b[i], conv_a_ln_g[i], conv_a_ln_b[i], conv_b_dw[i],
                                  w_out, TL_MIX)
            xp = y.reshape(bsz * seq, d)
            pa.append(na)
            pb.append(nb_)

            parts = norm_matmul(xs, conv_norm_g[i], w_in, TM_PROMPT)
            st_a = jnp.swapaxes(state_conv_a[i], 0, 1)
            st_b = jnp.swapaxes(state_conv_b[i], 0, 1)
            xs, a_new, c_new = conv_step(parts, xs, st_a, st_b, conv_a_dw[i], conv_a_dw_b[i], conv_a_ln_g[i],
                                         conv_a_ln_b[i], conv_b_dw[i], w_out)
            sa.append(jnp.concatenate([state_conv_a[i][:, 1:], a_new[:, None]], axis=1))
            sb.append(jnp.concatenate([state_conv_b[i][:, 1:], c_new[:, None]], axis=1))
        else:
            w_in, w_out, w_pool = layer_bf16(attn_w_in, i), layer_bf16(attn_w_out, i), layer_bf16(pool_w, i)
            q, k, v, u = norm_matmul(xp, attn_norm_g[i], w_in, TM_PROMPT, head_major=3)
            shp = (n_heads, bsz, seq, HEAD_DIM)
            o, block_sums = moba_prompt(q.reshape(shp), k.reshape(shp), v.reshape(shp), cache_k, i, page_table)
            y, npool = pool_out(o, u.reshape(bsz, seq, -1), xp.reshape(bsz, seq, d), w_pool, pool_scale[i],
                                w_out, TL_MIX)
            xp = y.reshape(bsz * seq, d)
            pk.append(jnp.transpose(k.reshape(shp), (1, 2, 0, 3)))
            pv.append(jnp.transpose(v.reshape(shp), (1, 2, 0, 3)))
            pp.append(npool)

            q, k, v, u = norm_matmul(xs, attn_norm_g[i], w_in, TM_PROMPT, head_major=3)
            heads = lambda t: jnp.swapaxes(t, 0, 1)
            sel = gate_step(block_sums, heads(q))
            o = attn_step(cache_k, cache_v, i, page_table, sel, heads(q), heads(k), heads(v))
            st_p = jnp.swapaxes(state_pool[i], 0, 1)
            xs = pool_step(o.reshape(bd, wattn), u, xs, st_p, w_pool, pool_scale[i], w_out, past_len)
            sk.append(heads(k).reshape(bd, 1, n_heads, HEAD_DIM))
            sv.append(heads(v).reshape(bd, 1, n_heads, HEAD_DIM))
            sp.append(jnp.concatenate([state_pool[i][:, 1:], u[:, None]], axis=1))
        last = layer == depth - 1
        ffn_w = (ffn_w_gate, ffn_w_up, ffn_w_down, layer)
        xp, xs = ffn(xp, xs, ffn_norm_g[layer], *ffn_w, final_norm_g, TM_FFN, TF, last)
    return (xp.reshape(bsz, seq, d), xs.reshape(bd, 1, d), jnp.stack(pa), jnp.stack(sa), jnp.stack(pb),
            jnp.stack(sb), jnp.stack(pk), jnp.stack(sk), jnp.stack(pv), jnp.stack(sv), jnp.stack(pp), jnp.stack(sp))
```

```python
import functools

import jax
import jax.numpy as jnp
from jax import lax
from jax.experimental import pallas as pl
from jax.experimental.pallas import tpu as pltpu

EPS = 1e-6
PAGE_SIZE = 128
MOBA_BLOCK = 256
MOBA_TOPK = 3
HEAD_DIM = 128
POOL_WINDOWS = (2, 4, 8, 16)
POOL_MAX = max(POOL_WINDOWS)
MASKED = -1e30

V7X_VMEM_BYTES = 64 * 1024 * 1024
VMEM_LIMIT = V7X_VMEM_BYTES - 4 * 1024 * 1024
LANES = 128
SUBLANES = 8
COL_GROUP = 1024

BF16 = jnp.bfloat16
F32 = jnp.float32


def _params(*sem):
    return pltpu.CompilerParams(dimension_semantics=sem, vmem_limit_bytes=VMEM_LIMIT)


def _rms(x, g):
    ms = jnp.mean(x * x, axis=-1, keepdims=True)
    return x * lax.rsqrt(ms + EPS) * g


def _sigmoid(x):
    return 1.0 / (1.0 + jnp.exp(-x))


def _resident(shape):
    return pl.BlockSpec(shape, lambda *_: (0,) * len(shape), pipeline_mode=pl.Buffered(1))


def _norm_matmul_kernel(x_ref, g_ref, w_ref, *out_refs, head_major):
    h = _rms(x_ref[...], g_ref[...]).astype(BF16)
    for c, o_ref in enumerate(out_refs):
        r = jnp.dot(h, w_ref[:, c * COL_GROUP:(c + 1) * COL_GROUP], preferred_element_type=F32)
        if c < head_major:
            for hh in range(COL_GROUP // HEAD_DIM):
                o_ref[hh] = r[:, hh * HEAD_DIM:(hh + 1) * HEAD_DIM]
        else:
            o_ref[...] = r


def norm_matmul(x, g, w, tm, head_major=0):
    m, d = x.shape
    n_out = w.shape[1] // COL_GROUP
    tm = min(tm, m)
    heads = COL_GROUP // HEAD_DIM
    flat_spec = pl.BlockSpec((tm, COL_GROUP), lambda i: (i, 0))
    head_spec = pl.BlockSpec((heads, tm, HEAD_DIM), lambda i: (0, i, 0))
    flat_shape = jax.ShapeDtypeStruct((m, COL_GROUP), F32)
    head_shape = jax.ShapeDtypeStruct((heads, m, HEAD_DIM), F32)
    return pl.pallas_call(
        functools.partial(_norm_matmul_kernel, head_major=head_major),
        grid=(m // tm,),
        in_specs=[pl.BlockSpec((tm, d), lambda i: (i, 0)), _resident((1, d)), _resident(w.shape)],
        out_specs=[head_spec] * head_major + [flat_spec] * (n_out - head_major),
        out_shape=[head_shape] * head_major + [flat_shape] * (n_out - head_major),
        compiler_params=_params("parallel"),
        name="norm_matmul",
    )(x, g.reshape(1, d), w)


def _ffn_kernel(g_ref, fg_ref, xs_ref, wg_ref, wu_ref, wd_ref, x_hbm, o_ref, os_ref, xbuf, h_ref, sem, *, final_norm):
    i, f = pl.program_id(0), pl.program_id(1)
    tm = o_ref.shape[0]
    ns = xs_ref.shape[0]
    last_f = pl.num_programs(1) - 1

    def x_copy(tile):
        return pltpu.make_async_copy(x_hbm.at[pl.ds(pl.multiple_of(tile * tm, tm), tm), :], xbuf, sem)

    @pl.when((i == 0) & (f == 0))
    def _():
        x_copy(0).start()
        xs = xs_ref[...]
        h_ref[tm:, :] = jnp.zeros((h_ref.shape[0] - tm, h_ref.shape[1]), h_ref.dtype)
        h_ref[tm:tm + ns, :] = _rms(xs, g_ref[...]).astype(h_ref.dtype)
        os_ref[...] = xs

    @pl.when(f == 0)
    def _():
        x_copy(i).wait()
        x = xbuf[...]
        h_ref[0:tm, :] = _rms(x, g_ref[...]).astype(h_ref.dtype)
        o_ref[...] = x

    @pl.when((f == 1) & (i + 1 < pl.num_programs(0)))
    def _():
        x_copy(i + 1).start()

    h = h_ref[...]
    gate = jnp.dot(h, wg_ref[...].astype(BF16), preferred_element_type=F32)
    up = jnp.dot(h, wu_ref[...].astype(BF16), preferred_element_type=F32)
    act = (gate * _sigmoid(gate) * up).astype(BF16)
    down = jnp.dot(act, wd_ref[...].astype(BF16), preferred_element_type=F32)
    o_ref[...] += down[0:tm]

    @pl.when(i == 0)
    def _():
        os_ref[...] += down[tm:tm + ns]

    if final_norm:
        @pl.when(f == last_f)
        def _():
            o_ref[...] = _rms(o_ref[...], fg_ref[...])

        @pl.when((i == 0) & (f == last_f))
        def _():
            os_ref[...] = _rms(os_ref[...], fg_ref[...])


BF16_ROWS = 16


def ffn(x, xs, g, wg, wu, wd, layer, final_g, tm, tf, final_norm):
    m, d = x.shape
    ns = xs.shape[0]
    dff = wg.shape[2]
    tm = min(tm, m)
    assert m % tm == 0 and dff % tf == 0 and dff // tf >= 2 and ns <= BF16_ROWS and tm % BF16_ROWS == 0
    return pl.pallas_call(
        functools.partial(_ffn_kernel, final_norm=final_norm),
        grid=(m // tm, dff // tf),
        in_specs=[_resident((1, d)), _resident((1, d)), _resident((ns, d)),
                  pl.BlockSpec((None, d, tf), lambda i, f: (layer, 0, f)),
                  pl.BlockSpec((None, d, tf), lambda i, f: (layer, 0, f)),
                  pl.BlockSpec((None, tf, d), lambda i, f: (layer, f, 0)),
                  pl.BlockSpec(memory_space=pl.ANY)],
        out_specs=[pl.BlockSpec((tm, d), lambda i, f: (i, 0)), pl.BlockSpec((ns, d), lambda i, f: (0, 0))],
        out_shape=[jax.ShapeDtypeStruct((m, d), F32), jax.ShapeDtypeStruct((ns, d), F32)],
        scratch_shapes=[pltpu.VMEM((tm, d), F32), pltpu.VMEM((tm + BF16_ROWS, d), BF16),
                        pltpu.SemaphoreType.DMA(())],
        compiler_params=_params("arbitrary", "arbitrary"),
        name="ffn",
    )(g.reshape(1, d), final_g.reshape(1, d), xs, wg, wu, wd, x)


A_HALO = 32
B_HALO = 8


def _layer_norm_silu(x, g, b):
    mu = jnp.mean(x, axis=-1, keepdims=True)
    xc = x - mu
    y = xc * lax.rsqrt(jnp.mean(xc * xc, axis=-1, keepdims=True) + EPS) * g + b
    return y * _sigmoid(y)


def _conv_mix_kernel(av_ref, ag_ref, bh_ref, bb_ref, bc_ref, x_ref, adw_ref, adwb_ref, lng_ref, lnb_ref,
                     bdw_ref, wo_ref, y_ref, newa_ref, newb_ref, aext, cext, ashift, aconv, *, ta, tb, rc):
    l = pl.program_id(1)
    tl = av_ref.shape[1]
    wa = av_ref.shape[2]

    @pl.when(l == 0)
    def _():
        aext[0:A_HALO, :] = jnp.zeros((A_HALO, wa), F32)
        cext[0:B_HALO, :] = jnp.zeros((B_HALO, wa), F32)

    @pl.when(l > 0)
    def _():
        aext[0:A_HALO, :] = aext[tl:tl + A_HALO, :]
        cext[0:B_HALO, :] = cext[tl:tl + B_HALO, :]

    aext[A_HALO:A_HALO + tl, :] = av_ref[0] * _sigmoid(ag_ref[0])
    cext[B_HALO:B_HALO + tl, :] = bc_ref[0] * bh_ref[0]

    bconv = jnp.zeros((tl, wa), F32)
    for k in range(tb):
        off = B_HALO - (tb - 1) + k
        bconv = bconv + bdw_ref[k:k + 1, :] * cext[off:off + tl, :]
    b_out = (bb_ref[0] * bconv).astype(BF16)
    y_ref[0] = x_ref[0] + jnp.dot(b_out, wo_ref[wa:, :], preferred_element_type=F32)

    n_shift = ashift.shape[1]
    for s in range(1, SUBLANES):
        ashift[s - 1] = aext[s:s + n_shift, :]

    for r0 in range(0, tl, rc):
        for c0 in range(0, wa, LANES):
            acc = jnp.broadcast_to(adwb_ref[:, c0:c0 + LANES], (rc, LANES))
            for k in range(ta):
                off = A_HALO - (ta - 1) + r0 + k
                s = off % SUBLANES
                src = aext if s == 0 else ashift.at[s - 1]
                acc = acc + adw_ref[k:k + 1, c0:c0 + LANES] * src[off - s:off - s + rc, c0:c0 + LANES]
            aconv[r0:r0 + rc, c0:c0 + LANES] = acc
    a_out = _layer_norm_silu(aconv[...], lng_ref[...], lnb_ref[...]).astype(BF16)
    y_ref[0] += jnp.dot(a_out, wo_ref[0:wa, :], preferred_element_type=F32)

    @pl.when(l == pl.num_programs(1) - 1)
    def _():
        newa_ref[0] = aext[A_HALO + tl - (ta - 1):A_HALO + tl, :]
        newb_ref[0] = cext[B_HALO + tl - (tb - 1):B_HALO + tl, :]


def conv_mix(parts, x, a_dw, a_dw_b, ln_g, ln_b, b_dw, w_out, tl):
    bsz, seq, d = x.shape
    wa = parts[0].shape[-1]
    ta, tb = a_dw.shape[0], b_dw.shape[0]
    tl = min(tl, seq)
    assert seq % tl == 0 and tl >= A_HALO and ta - 1 <= A_HALO and tb - 1 <= B_HALO
    part_spec = pl.BlockSpec((1, tl, wa), lambda b, l: (b, l, 0))
    const = _resident
    return pl.pallas_call(
        functools.partial(_conv_mix_kernel, ta=ta, tb=tb, rc=min(128, tl)),
        grid=(bsz, seq // tl),
        in_specs=[part_spec] * 5 + [pl.BlockSpec((1, tl, d), lambda b, l: (b, l, 0)),
                                    const((ta, wa)), const((1, wa)), const((1, wa)), const((1, wa)),
                                    const((tb, wa)), const(w_out.shape)],
        out_specs=[pl.BlockSpec((1, tl, d), lambda b, l: (b, l, 0)),
                   pl.BlockSpec((1, ta - 1, wa), lambda b, l: (b, 0, 0)),
                   pl.BlockSpec((1, tb - 1, wa), lambda b, l: (b, 0, 0))],
        out_shape=[jax.ShapeDtypeStruct((bsz, seq, d), F32),
                   jax.ShapeDtypeStruct((bsz, ta - 1, wa), F32),
                   jax.ShapeDtypeStruct((bsz, tb - 1, wa), F32)],
        scratch_shapes=[pltpu.VMEM((A_HALO + tl, wa), F32), pltpu.VMEM((B_HALO + tl, wa), F32),
                        pltpu.VMEM((SUBLANES - 1, A_HALO + tl - SUBLANES, wa), F32),
                        pltpu.VMEM((tl, wa), F32)],
        compiler_params=_params("parallel", "arbitrary"),
        name="conv_mix",
    )(*parts, x, a_dw, a_dw_b.reshape(1, wa), ln_g.reshape(1, wa), ln_b.reshape(1, wa), b_dw, w_out)


def _conv_step_kernel(av_ref, ag_ref, bh_ref, bb_ref, bc_ref, x_ref, sta_ref, stb_ref, adw_ref, adwb_ref,
                      lng_ref, lnb_ref, bdw_ref, wo_ref, y_ref, anew_ref, cnew_ref, *, ta, tb):
    a = av_ref[...] * _sigmoid(ag_ref[...])
    c = bc_ref[...] * bh_ref[...]
    anew_ref[...] = a
    cnew_ref[...] = c
    acc = adwb_ref[...] + adw_ref[ta - 1:ta, :] * a
    for k in range(ta - 1):
        acc = acc + adw_ref[k:k + 1, :] * sta_ref[k]
    a_out = _layer_norm_silu(acc, lng_ref[...], lnb_ref[...])
    bconv = bdw_ref[tb - 1:tb, :] * c
    for k in range(tb - 1):
        bconv = bconv + bdw_ref[k:k + 1, :] * stb_ref[k]
    mix = jnp.concatenate([a_out, bb_ref[...] * bconv], axis=-1).astype(BF16)
    y_ref[...] = x_ref[...] + jnp.dot(mix, wo_ref[...], preferred_element_type=F32)


def conv_step(parts, x, st_a, st_b, a_dw, a_dw_b, ln_g, ln_b, b_dw, w_out):
    bd, d = x.shape
    wa = parts[0].shape[-1]
    ta, tb = a_dw.shape[0], b_dw.shape[0]
    return pl.pallas_call(
        functools.partial(_conv_step_kernel, ta=ta, tb=tb),
        out_shape=[jax.ShapeDtypeStruct((bd, d), F32), jax.ShapeDtypeStruct((bd, wa), F32),
                   jax.ShapeDtypeStruct((bd, wa), F32)],
        compiler_params=pltpu.CompilerParams(vmem_limit_bytes=VMEM_LIMIT),
        name="conv_step",
    )(*parts, x, st_a, st_b, a_dw, a_dw_b.reshape(1, wa), ln_g.reshape(1, wa), ln_b.reshape(1, wa), b_dw, w_out)


LOG2E = 1.4426950408889634
TQ = 2 * MOBA_BLOCK
DA = 2 * HEAD_DIM


def _split3(x):
    def top_bits(v):
        return lax.bitcast_convert_type(lax.bitcast_convert_type(v, jnp.uint32) & jnp.uint32(0xFFFF0000), F32)

    hi = top_bits(x)
    mid = top_bits(x - hi)
    return hi, mid, x - hi - mid


def _moba_setup(q_ref, k_ref, v_ref, qaug, kaug, vaug, kmean, slope2, nb):
    seq = k_ref.shape[1]
    bs = MOBA_BLOCK
    nbp = kmean.shape[0]
    k = k_ref[0]
    q = q_ref[0]
    kmean[...] = jnp.zeros_like(kmean)
    kmean[0:nb, :] = jnp.mean(k.reshape(nb, bs, HEAD_DIM), axis=1)

    gate = lax.dot_general(kmean[...], q, (((1,), (1,)), ((), ())), precision=lax.Precision.HIGHEST,
                           preferred_element_type=F32)
    blk = lax.broadcasted_iota(jnp.int32, (nbp, seq), 0)
    own = lax.broadcasted_iota(jnp.int32, (nbp, seq), 1) // bs
    gate = jnp.where(blk < own, gate, -jnp.inf)
    attend = blk == own
    for _ in range(MOBA_TOPK):
        m = jnp.max(gate, axis=0, keepdims=True)
        idx = jnp.min(jnp.where(gate == m, blk, nbp), axis=0, keepdims=True)
        hit = blk == idx
        attend = attend | (hit & (blk < own))
        gate = jnp.where(hit, -jnp.inf, gate)
    attend_t = jnp.where(attend, 1.0, 0.0)
    attend_r = jnp.concatenate([attend_t, jnp.zeros((LANES - nbp, seq), F32)], axis=0).T

    lane = lax.broadcasted_iota(jnp.int32, (seq, LANES), 1)
    pos_i = lax.broadcasted_iota(jnp.int32, (seq, LANES), 0)
    pos = pos_i.astype(F32)

    khi, kmid, klo = _split3(slope2 * pos)
    ek = jnp.where(lane == pos_i // bs, 1.0, 0.0)
    ek = jnp.where(lane == nb, khi, ek)
    ek = jnp.where(lane == nb + 1, kmid, ek)
    ek = jnp.where(lane == nb + 2, klo, ek)
    ek = jnp.where((lane >= nb + 3) & (lane < nb + 6), 1.0, ek)
    kaug[:, 0:HEAD_DIM] = k.astype(BF16)
    kaug[:, HEAD_DIM:] = ek.astype(BF16)

    qhi, qmid, qlo = _split3(-slope2 * pos)
    eq = jnp.where(lane < nb, jnp.where(attend_r > 0.5, 0.0, MASKED), 0.0)
    eq = jnp.where((lane >= nb) & (lane < nb + 3), 1.0, eq)
    eq = jnp.where(lane == nb + 3, qhi, eq)
    eq = jnp.where(lane == nb + 4, qmid, eq)
    eq = jnp.where(lane == nb + 5, qlo, eq)
    qaug[:, 0:HEAD_DIM] = (q * (HEAD_DIM ** -0.5 * LOG2E)).astype(BF16)
    qaug[:, HEAD_DIM:] = eq.astype(BF16)

    vaug[:, 0:HEAD_DIM] = v_ref[0].astype(BF16)
    vaug[:, HEAD_DIM:] = jnp.ones((seq, HEAD_DIM), BF16)


SUM_CHAINS = 16
PAGE_LOOKAHEAD = 2


class _PageSums:
    def __init__(self, pt_ref, kc_hbm, ksum_ref, ring, sem, *, layer, pages_per_block, n_steps):
        self.pt_ref, self.kc_hbm, self.ksum_ref, self.ring, self.sem = pt_ref, kc_hbm, ksum_ref, ring, sem
        self.layer, self.pages_per_block, self.n_steps = layer, pages_per_block, n_steps
        self.total = pt_ref.shape[0]
        self.depth = ring.shape[0]
        self.per = self.depth // PAGE_LOOKAHEAD
        self.even = n_steps * self.per == self.total

    def _copy(self, t, slot):
        return pltpu.make_async_copy(self.kc_hbm.at[self.layer, self.pt_ref[t]], self.ring.at[slot],
                                     self.sem.at[slot])

    def begin(self):
        self.ksum_ref[...] = jnp.zeros_like(self.ksum_ref)
        for slot in range(min(self.depth, self.total)):
            self._copy(slot, slot).start()

    def _fold_page(self, t, slot):
        _, page_rows, n_heads, _ = self.ring.shape
        self._copy(t, slot).wait()
        page = self.ring[slot].reshape(SUM_CHAINS, page_rows // SUM_CHAINS, n_heads, HEAD_DIM)
        self.ksum_ref[t // self.pages_per_block] += jnp.sum(jnp.sum(page, axis=1), axis=0)

    def _first_slot(self, step):
        return (step % PAGE_LOOKAHEAD) * self.per

    def fold(self, step):
        assert self.even
        for j in range(self.per):
            self._fold_page(step * self.per + j, self._first_slot(step) + j)

    def refill(self, step):
        assert self.even

        @pl.when(step + PAGE_LOOKAHEAD < self.n_steps)
        def _():
            for j in range(self.per):
                self._copy(step * self.per + j + self.depth, self._first_slot(step) + j).start()

    def step(self, step):
        for j in range(self.per):
            t = step * self.per + j

            @pl.when(t < self.total)
            def _(t=t, j=j):
                self._fold_page(t, self._first_slot(step) + j)

                @pl.when(t + self.depth < self.total)
                def _():
                    self._copy(t + self.depth, self._first_slot(step) + j).start()


def _moba_kernel(pt_ref, q_ref, k_ref, v_ref, kc_hbm, o_ref, ksum_ref, qaug, kaug, vaug, kmean, m_ref, acc_ref,
                 s_ref, rel_ref, ring, psem, *, n_heads, nb, layer, page_pairs, page_steps):
    h = pl.program_id(1)
    bs = MOBA_BLOCK
    nt = (((1,), (1,)), ((), ()))
    seq = qaug.shape[0]
    n_tiles = seq // TQ
    n_units = n_tiles * (n_tiles + 1) // 2
    n_pairs = n_units // 2
    bh = pl.program_id(0) * pl.num_programs(1) + h

    slope2 = jnp.exp2((h + 1).astype(F32) * (-8.0 / n_heads) + jnp.zeros((1, 1), F32)) * LOG2E
    _moba_setup(q_ref, k_ref, v_ref, qaug, kaug, vaug, kmean, slope2, nb)
    rel_ref[...] = (lax.broadcasted_iota(jnp.int32, (bs, bs), 1) - lax.broadcasted_iota(jnp.int32, (bs, bs), 0))

    halves = (slice(0, bs), slice(bs, TQ))

    def tile_rows(t):
        return pl.ds(pl.multiple_of(t * TQ, TQ), TQ)

    def successor(unit):
        i, g = unit
        wrap = g == i
        return jnp.where(wrap, jnp.minimum(i + 1, n_tiles - 1), i), jnp.where(wrap, 0, g + 1)

    def scores(unit):
        i, g = unit
        qa = qaug[tile_rows(i), :]
        kg = kaug[tile_rows(g), :]
        return jnp.concatenate([lax.dot_general(qa[r], kg, nt, preferred_element_type=F32) for r in halves], axis=0)

    def process(unit, slot):
        i, g = unit
        vg = vaug[tile_rows(g), :]
        for r in halves:
            diag = jnp.where(rel_ref[...] <= (i - g) * TQ, s_ref[slot, r, r], MASKED)
            sr = jnp.concatenate([diag, s_ref[slot, r, bs:]] if r.start == 0 else [s_ref[slot, r, 0:bs], diag], axis=1)
            m = jnp.where(g == 0, MASKED, m_ref[r, :])
            m_new = jnp.maximum(m, jnp.max(sr, axis=1, keepdims=True))
            p = jnp.exp2(sr - m_new).astype(BF16)
            pv = jnp.dot(p, vg, preferred_element_type=F32)
            acc = jnp.exp2(m - m_new) * acc_ref[r, :] + pv
            acc_ref[r, :] = acc
            m_ref[r, :] = m_new
            rows = pl.ds(pl.multiple_of(i * TQ + r.start, bs), bs)
            o_ref[0, rows, :] = acc[:, 0:HEAD_DIM] / acc[:, HEAD_DIM:]

    m_ref[...] = jnp.full(m_ref.shape, MASKED, F32)
    acc_ref[...] = jnp.zeros_like(acc_ref)
    first = (jnp.int32(0), jnp.int32(0))
    s_ref[0] = scores(first)

    pages = _PageSums(pt_ref, kc_hbm, ksum_ref, ring, psem, layer=layer, pages_per_block=bs // PAGE_SIZE,
                      n_steps=page_steps)

    @pl.when(bh == 0)
    def _():
        pages.begin()

    def pair(p, unit, with_pages):
        page_step = bh * page_pairs + p
        if with_pages and pages.even:
            pages.fold(page_step)
        elif with_pages:
            pages.step(page_step)
        nxt = successor(unit)
        s_ref[1] = scores(nxt)
        process(unit, 0)
        nxt2 = successor(nxt)
        s_ref[0] = scores(nxt2)
        process(nxt, 1)
        if with_pages and pages.even:
            pages.refill(page_step)
        return nxt2

    unit = lax.fori_loop(0, page_pairs, functools.partial(pair, with_pages=True), first)
    if page_pairs < n_pairs:
        unit = lax.fori_loop(page_pairs, n_pairs, functools.partial(pair, with_pages=False), unit)
    if n_units % 2:
        process(unit, 0)


def moba_prompt(q, k, v, cache_k, layer, page_table):
    n_heads, bsz, seq, _ = q.shape
    width = n_heads * HEAD_DIM
    assert seq % TQ == 0
    nb = seq // MOBA_BLOCK
    nbp = -(-nb // 8) * 8
    assert nb + 6 <= LANES
    bd, n_pages = page_table.shape
    pages_per_block = MOBA_BLOCK // PAGE_SIZE
    assert n_pages % pages_per_block == 0
    n_tiles = seq // TQ
    n_pairs = n_tiles * (n_tiles + 1) // 4
    assert n_pairs >= 1
    total_pages = bd * n_pages
    even = [p for p in range(1, n_pairs + 1) if total_pages % (bsz * n_heads * p) == 0]
    page_pairs = max(even) if even else n_pairs
    page_steps = bsz * n_heads * page_pairs
    pages_per_step = -(-total_pages // page_steps)
    sums_shape = (total_pages // pages_per_block, n_heads, HEAD_DIM)
    slab = pl.BlockSpec((None, 1, seq, HEAD_DIM), lambda b, h: (h, b, 0, 0))
    return pl.pallas_call(
        functools.partial(_moba_kernel, n_heads=n_heads, nb=nb, layer=layer, page_pairs=page_pairs,
                          page_steps=page_steps),
        grid=(bsz, n_heads),
        in_specs=[pl.BlockSpec(memory_space=pltpu.SMEM), slab, slab, slab, pl.BlockSpec(memory_space=pl.ANY)],
        out_specs=[pl.BlockSpec((1, seq, HEAD_DIM), lambda b, h: (b, 0, h)),
                   pl.BlockSpec(sums_shape, lambda b, h: (0, 0, 0))],
        out_shape=[jax.ShapeDtypeStruct((bsz, seq, width), F32), jax.ShapeDtypeStruct(sums_shape, F32)],
        scratch_shapes=[pltpu.VMEM((seq, DA), BF16), pltpu.VMEM((seq, DA), BF16), pltpu.VMEM((seq, DA), BF16),
                        pltpu.VMEM((nbp, HEAD_DIM), F32), pltpu.VMEM((TQ, 1), F32), pltpu.VMEM((TQ, DA), F32),
                        pltpu.VMEM((2, TQ, TQ), F32), pltpu.VMEM((MOBA_BLOCK, MOBA_BLOCK), jnp.int32),
                        pltpu.VMEM((PAGE_LOOKAHEAD * pages_per_step, PAGE_SIZE, n_heads, HEAD_DIM), F32),
                        pltpu.SemaphoreType.DMA((PAGE_LOOKAHEAD * pages_per_step,))],
        compiler_params=_params("arbitrary", "arbitrary"),
        name="moba_prompt",
    )(page_table.reshape(-1), q, k, v, cache_k)


PAGES_IN_FLIGHT = 8


def _gate_step_kernel(q_ref, ksum_ref, sel_ref, *, n_heads):
    bd = q_ref.shape[0]
    nblk = ksum_ref.shape[0] // bd
    sel_ref[...] = jnp.zeros_like(sel_ref)
    for b in range(bd):
        kmean = ksum_ref[b * nblk:(b + 1) * nblk] * (1.0 / MOBA_BLOCK)
        gate = jnp.sum(kmean * q_ref[b][None], axis=-1)
        blk = lax.broadcasted_iota(jnp.int32, gate.shape, 0)
        for r in range(MOBA_TOPK):
            m = jnp.max(gate, axis=0, keepdims=True)
            idx = jnp.min(jnp.where(gate == m, blk, nblk), axis=0, keepdims=True)
            sel_ref[b, r:r + 1, 0:n_heads] = idx
            gate = jnp.where(blk == idx, -jnp.inf, gate)


SEL_ROWS = 8


def gate_step(block_sums, q):
    bd, n_heads, _ = q.shape
    assert MOBA_TOPK <= SEL_ROWS and n_heads <= LANES
    sel = pl.pallas_call(
        functools.partial(_gate_step_kernel, n_heads=n_heads),
        out_shape=jax.ShapeDtypeStruct((bd, SEL_ROWS, LANES), jnp.int32),
        compiler_params=pltpu.CompilerParams(vmem_limit_bytes=VMEM_LIMIT),
        name="gate_step",
    )(q, block_sums)
    return sel[:, :MOBA_TOPK, :n_heads]


def _attn_step_kernel(pt_ref, sel_ref, q_ref, kn_ref, vn_ref, k_hbm, v_hbm, o_ref, kbuf, vbuf, ksem, vsem,
                      m_ref, l_ref, acc_ref, *, layer, n_pages, q_pos, pages_per_block):
    bd, n_heads, _ = q_ref.shape
    per_head = MOBA_TOPK * pages_per_block
    total = bd * n_heads * per_head
    depth = kbuf.shape[0]
    scale = HEAD_DIM ** -0.5
    own = q_pos // MOBA_BLOCK

    def coords(t):
        bh, j = t // per_head, t % per_head
        b, h = bh // n_heads, bh % n_heads
        return b, h, j, sel_ref[(b * MOBA_TOPK + j // pages_per_block) * n_heads + h]

    def page_copies(t, slot):
        b, _, j, n = coords(t)
        page = pt_ref[b * n_pages + n * pages_per_block + j % pages_per_block]
        return (pltpu.make_async_copy(k_hbm.at[layer, page], kbuf.at[slot], ksem.at[slot]),
                pltpu.make_async_copy(v_hbm.at[layer, page], vbuf.at[slot], vsem.at[slot]))

    for t in range(min(depth, total)):
        for c in page_copies(t, t):
            c.start()

    @pl.loop(0, total)
    def _(t):
        slot = t % depth
        b, h, j, n = coords(t)
        for c in page_copies(t, slot):
            c.wait()
        q = q_ref[b, pl.ds(h, 1), :]

        @pl.when(j == 0)
        def _():
            m_ref[...] = jnp.sum(q * kn_ref[b, pl.ds(h, 1), :], axis=1, keepdims=True) * scale
            l_ref[...] = jnp.ones_like(l_ref)
            acc_ref[...] = vn_ref[b, pl.ds(h, 1), :]

        k = kbuf[slot, pl.ds(h, PAGE_SIZE, stride=n_heads), :]
        v = vbuf[slot, pl.ds(h, PAGE_SIZE, stride=n_heads), :]
        slope = jnp.exp2(jnp.asarray(h + 1, F32) * (-8.0 / n_heads) + jnp.zeros((1, 1), F32))
        kpos = (n * MOBA_BLOCK + (j % pages_per_block) * PAGE_SIZE
                + lax.broadcasted_iota(jnp.int32, (PAGE_SIZE, 1), 0))
        dist = (q_pos - kpos).astype(F32)
        s = jnp.sum(k * q, axis=1, keepdims=True) * scale - slope * dist
        s = jnp.where((dist >= 0.0) & (n < own), s, MASKED)
        m = m_ref[...]
        m_new = jnp.maximum(m, jnp.max(s, axis=0, keepdims=True))
        alpha = jnp.exp(m - m_new)
        p = jnp.exp(s - m_new)
        l_ref[...] = alpha * l_ref[...] + jnp.sum(p, axis=0, keepdims=True)
        acc_ref[...] = alpha * acc_ref[...] + jnp.sum(p * v, axis=0, keepdims=True)
        m_ref[...] = m_new

        @pl.when(j == per_head - 1)
        def _():
            o_ref[b, pl.ds(h, 1), :] = acc_ref[...] / l_ref[...]

        @pl.when(t + depth < total)
        def _():
            for c in page_copies(t + depth, slot):
                c.start()


def attn_step(cache_k, cache_v, layer, page_table, sel, q, k_new, v_new):
    bd, n_pages = page_table.shape
    n_heads = q.shape[1]
    pages_per_block = MOBA_BLOCK // PAGE_SIZE
    assert n_pages % pages_per_block == 0
    rows = PAGE_SIZE * n_heads
    as_rows = lambda c: c.reshape(c.shape[0], c.shape[1], rows, HEAD_DIM)
    vmem, smem = pl.BlockSpec(memory_space=pltpu.VMEM), pl.BlockSpec(memory_space=pltpu.SMEM)
    hbm = pl.BlockSpec(memory_space=pl.ANY)
    return pl.pallas_call(
        functools.partial(_attn_step_kernel, layer=layer, n_pages=n_pages, q_pos=n_pages * PAGE_SIZE,
                          pages_per_block=pages_per_block),
        in_specs=[smem, smem, vmem, vmem, vmem, hbm, hbm],
        out_specs=vmem,
        out_shape=jax.ShapeDtypeStruct((bd, n_heads, HEAD_DIM), F32),
        scratch_shapes=[pltpu.VMEM((PAGES_IN_FLIGHT, rows, HEAD_DIM), F32),
                        pltpu.VMEM((PAGES_IN_FLIGHT, rows, HEAD_DIM), F32),
                        pltpu.SemaphoreType.DMA((PAGES_IN_FLIGHT,)), pltpu.SemaphoreType.DMA((PAGES_IN_FLIGHT,)),
                        pltpu.VMEM((1, 1), F32), pltpu.VMEM((1, 1), F32), pltpu.VMEM((1, HEAD_DIM), F32)],
        compiler_params=pltpu.CompilerParams(vmem_limit_bytes=VMEM_LIMIT),
        name="attn_step",
    )(page_table.reshape(-1), sel.reshape(-1), q, k_new, v_new, as_rows(cache_k), as_rows(cache_v))


P_HALO = 16


def _pool_out_kernel(o_ref, u_ref, x_ref, pw_ref, ps_ref, wo_ref, y_ref, newp_ref, uext, mix):
    l = pl.program_id(1)
    tl = u_ref.shape[1]
    wp = u_ref.shape[2]
    wo_attn = o_ref.shape[2]
    pg = wp // len(POOL_WINDOWS)

    @pl.when(l == 0)
    def _():
        uext[0:P_HALO, :] = jnp.zeros((P_HALO, wp), F32)

    @pl.when(l > 0)
    def _():
        uext[0:P_HALO, :] = uext[tl:tl + P_HALO, :]

    uext[P_HALO:P_HALO + tl, :] = u_ref[0]
    o_bf16 = o_ref[0].astype(BF16)
    n_chunk = y_ref.shape[2] // len(POOL_WINDOWS)
    pos = l * tl + lax.broadcasted_iota(jnp.int32, (tl, pg), 0)
    for g, w in enumerate(POOL_WINDOWS):
        n0 = g * n_chunk
        y_ref[0, :, n0:n0 + n_chunk] = x_ref[0, :, n0:n0 + n_chunk] + jnp.dot(
            o_bf16, wo_ref[0:wo_attn, n0:n0 + n_chunk], preferred_element_type=F32)
        c0 = g * pg
        wsum = uext[P_HALO:P_HALO + tl, c0:c0 + pg]
        for j in range(1, w):
            wsum = wsum + uext[P_HALO - j:P_HALO - j + tl, c0:c0 + pg]
        count = jnp.minimum(pos + 1, w).astype(F32)
        dlt = (wsum / count - uext[P_HALO:P_HALO + tl, c0:c0 + pg]).astype(BF16)
        yp = jnp.dot(dlt, pw_ref[g], preferred_element_type=F32) * ps_ref[:, c0:c0 + pg]
        mix[:, c0:c0 + pg] = yp.astype(BF16)

    y_ref[0] += jnp.dot(mix[...], wo_ref[wo_attn:, :], preferred_element_type=F32)

    @pl.when(l == pl.num_programs(1) - 1)
    def _():
        newp_ref[0] = uext[P_HALO + tl - (POOL_MAX - 1):P_HALO + tl, :]


def pool_out(o, u, x, pool_w, pool_scale, w_out, tl):
    bsz, seq, d = x.shape
    wp = u.shape[-1]
    tl = min(tl, seq)
    assert seq % tl == 0 and tl >= P_HALO
    const = _resident
    return pl.pallas_call(
        _pool_out_kernel,
        grid=(bsz, seq // tl),
        in_specs=[pl.BlockSpec((1, tl, o.shape[-1]), lambda b, l: (b, l, 0)),
                  pl.BlockSpec((1, tl, wp), lambda b, l: (b, l, 0)),
                  pl.BlockSpec((1, tl, d), lambda b, l: (b, l, 0)),
                  const(pool_w.shape), const((1, wp)), const(w_out.shape)],
        out_specs=[pl.BlockSpec((1, tl, d), lambda b, l: (b, l, 0)),
                   pl.BlockSpec((1, POOL_MAX - 1, wp), lambda b, l: (b, 0, 0))],
        out_shape=[jax.ShapeDtypeStruct((bsz, seq, d), F32),
                   jax.ShapeDtypeStruct((bsz, POOL_MAX - 1, wp), F32)],
        scratch_shapes=[pltpu.VMEM((P_HALO + tl, wp), F32), pltpu.VMEM((tl, wp), BF16)],
        compiler_params=_params("parallel", "arbitrary"),
        name="pool_out",
    )(o, u, x, pool_w, pool_scale.reshape(1, wp), w_out)


def _pool_step_kernel(o_ref, u_ref, x_ref, stp_ref, pw_ref, ps_ref, wo_ref, y_ref, *, start_pos):
    wp = u_ref.shape[1]
    pg = wp // len(POOL_WINDOWS)
    u = u_ref[...]
    parts = [o_ref[...].astype(BF16)]
    for g, w in enumerate(POOL_WINDOWS):
        c0 = g * pg
        wsum = u[:, c0:c0 + pg]
        for j in range(1, w):
            wsum = wsum + stp_ref[POOL_MAX - 1 - j][:, c0:c0 + pg]
        count = float(min(start_pos + 1, w))
        dlt = (wsum / count - u[:, c0:c0 + pg]).astype(BF16)
        yp = jnp.dot(dlt, pw_ref[g], preferred_element_type=F32) * ps_ref[:, c0:c0 + pg]
        parts.append(yp.astype(BF16))
    mix = jnp.concatenate(parts, axis=-1)
    y_ref[...] = x_ref[...] + jnp.dot(mix, wo_ref[...], preferred_element_type=F32)


def pool_step(o, u, x, st_p, pool_w, pool_scale, w_out, start_pos):
    bd, d = x.shape
    return pl.pallas_call(
        functools.partial(_pool_step_kernel, start_pos=start_pos),
        out_shape=jax.ShapeDtypeStruct((bd, d), F32),
        compiler_params=pltpu.CompilerParams(vmem_limit_bytes=VMEM_LIMIT),
        name="pool_step",
    )(o, u, x, st_p, pool_w, pool_scale.reshape(1, -1), w_out)


TM_PROMPT = 512
TM_FFN = 1024
TF = 512
TL_MIX = 256


def kernel(x_prompt, x_sample, state_conv_a, state_conv_b, cache_k, cache_v, state_pool, page_table, conv_norm_g, conv_w_in, conv_a_dw, conv_a_dw_b, conv_a_ln_g, conv_a_ln_b, conv_b_dw, conv_w_out, attn_norm_g, attn_w_in, pool_w, pool_scale, attn_w_out, ffn_norm_g, ffn_w_gate, ffn_w_up, ffn_w_down, final_norm_g):
    bsz, seq, d = x_prompt.shape
    bd, dec_seq, _ = x_sample.shape
    assert dec_seq == 1
    depth = ffn_norm_g.shape[0]
    n_heads = cache_k.shape[3]
    wattn = n_heads * HEAD_DIM
    past_len = page_table.shape[1] * PAGE_SIZE

    layer_bf16 = lambda w, n: w[n].astype(BF16)

    xp = x_prompt.reshape(bsz * seq, d)
    xs = x_sample.reshape(bd, d)
    pa, pb, pk, pv, pp = [], [], [], [], []
    sa, sb, sk, sv, sp = [], [], [], [], []
    for layer in range(depth):
        i = layer // 2
        if layer % 2 == 0:
            w_in, w_out = layer_bf16(conv_w_in, i), layer_bf16(conv_w_out, i)
            parts = norm_matmul(xp, conv_norm_g[i], w_in, TM_PROMPT)
            wa = parts[0].shape[-1]
            y, na, nb_ = conv_mix([t.reshape(bsz, seq, wa) for t in parts], xp.reshape(bsz, seq, d),
                                  conv_a_dw[i], conv_a_dw_b[i], conv_a_ln_g[i], conv_a_ln_b[i], conv_b_dw[i],
                                  w_out, TL_MIX)
            xp = y.reshape(bsz * seq, d)
            pa.append(na)
            pb.append(nb_)

            parts = norm_matmul(xs, conv_norm_g[i], w_in, TM_PROMPT)
            st_a = jnp.swapaxes(state_conv_a[i], 0, 1)
            st_b = jnp.swapaxes(state_conv_b[i], 0, 1)
            xs, a_new, c_new = conv_step(parts, xs, st_a, st_b, conv_a_dw[i], conv_a_dw_b[i], conv_a_ln_g[i],
                                         conv_a_ln_b[i], conv_b_dw[i], w_out)
            sa.append(jnp.concatenate([state_conv_a[i][:, 1:], a_new[:, None]], axis=1))
            sb.append(jnp.concatenate([state_conv_b[i][:, 1:], c_new[:, None]], axis=1))
        else:
            w_in, w_out, w_pool = layer_bf16(attn_w_in, i), layer_bf16(attn_w_out, i), layer_bf16(pool_w, i)
            q, k, v, u = norm_matmul(xp, attn_norm_g[i], w_in, TM_PROMPT, head_major=3)
            shp = (n_heads, bsz, seq, HEAD_DIM)
            o, block_sums = moba_prompt(q.reshape(shp), k.reshape(shp), v.reshape(shp), cache_k, i, page_table)
            y, npool = pool_out(o, u.reshape(bsz, seq, -1), xp.reshape(bsz, seq, d), w_pool, pool_scale[i],
                                w_out, TL_MIX)
            xp = y.reshape(bsz * seq, d)
            pk.append(jnp.transpose(k.reshape(shp), (1, 2, 0, 3)))
            pv.append(jnp.transpose(v.reshape(shp), (1, 2, 0, 3)))
            pp.append(npool)

            q, k, v, u = norm_matmul(xs, attn_norm_g[i], w_in, TM_PROMPT, head_major=3)
            heads = lambda t: jnp.swapaxes(t, 0, 1)
            sel = gate_step(block_sums, heads(q))
            o = attn_step(cache_k, cache_v, i, page_table, sel, heads(q), heads(k), heads(v))
            st_p = jnp.swapaxes(state_pool[i], 0, 1)
            xs = pool_step(o.reshape(bd, wattn), u, xs, st_p, w_pool, pool_scale[i], w_out, past_len)
            sk.append(heads(k).reshape(bd, 1, n_heads, HEAD_DIM))
            sv.append(heads(v).reshape(bd, 1, n_heads, HEAD_DIM))
            sp.append(jnp.concatenate([state_pool[i][:, 1:], u[:, None]], axis=1))
        last = layer == depth - 1
        ffn_w = (ffn_w_gate, ffn_w_up, ffn_w_down, layer)
        xp, xs = ffn(xp, xs, ffn_norm_g[layer], *ffn_w, final_norm_g, TM_FFN, TF, last)
    return (xp.reshape(bsz, seq, d), xs.reshape(bd, 1, d), jnp.stack(pa), jnp.stack(sa), jnp.stack(pb),
            jnp.stack(sb), jnp.stack(pk), jnp.stack(sk), jnp.stack(pv), jnp.stack(sv), jnp.stack(pp), jnp.stack(sp))
```

```python
import functools

import jax
import jax.numpy as jnp
from jax import lax
from jax.experimental import pallas as pl
from jax.experimental.pallas import tpu as pltpu

EPS = 1e-6
PAGE_SIZE = 128
MOBA_BLOCK = 256
MOBA_TOPK = 3
HEAD_DIM = 128
POOL_WINDOWS = (2, 4, 8, 16)
POOL_MAX = max(POOL_WINDOWS)
MASKED = -1e30

V7X_VMEM_BYTES = 64 * 1024 * 1024
VMEM_LIMIT = V7X_VMEM_BYTES - 4 * 1024 * 1024
LANES = 128
SUBLANES = 8
COL_GROUP = 1024

BF16 = jnp.bfloat16
F32 = jnp.float32


def _params(*sem):
    return pltpu.CompilerParams(dimension_semantics=sem, vmem_limit_bytes=VMEM_LIMIT)


def _rms(x, g):
    ms = jnp.mean(x * x, axis=-1, keepdims=True)
    return x * lax.rsqrt(ms + EPS) * g


def _sigmoid(x):
    return 1.0 / (1.0 + jnp.exp(-x))


def _resident(shape):
    return pl.BlockSpec(shape, lambda *_: (0,) * len(shape), pipeline_mode=pl.Buffered(1))


def _norm_matmul_kernel(x_ref, g_ref, w_ref, *out_refs, head_major):
    h = _rms(x_ref[...], g_ref[...]).astype(BF16)
    for c, o_ref in enumerate(out_refs):
        r = jnp.dot(h, w_ref[:, c * COL_GROUP:(c + 1) * COL_GROUP], preferred_element_type=F32)
        if c < head_major:
            for hh in range(COL_GROUP // HEAD_DIM):
                o_ref[hh] = r[:, hh * HEAD_DIM:(hh + 1) * HEAD_DIM]
        else:
            o_ref[...] = r


def norm_matmul(x, g, w, tm, head_major=0):
    m, d = x.shape
    n_out = w.shape[1] // COL_GROUP
    tm = min(tm, m)
    heads = COL_GROUP // HEAD_DIM
    flat_spec = pl.BlockSpec((tm, COL_GROUP), lambda i: (i, 0))
    head_spec = pl.BlockSpec((heads, tm, HEAD_DIM), lambda i: (0, i, 0))
    flat_shape = jax.ShapeDtypeStruct((m, COL_GROUP), F32)
    head_shape = jax.ShapeDtypeStruct((heads, m, HEAD_DIM), F32)
    return pl.pallas_call(
        functools.partial(_norm_matmul_kernel, head_major=head_major),
        grid=(m // tm,),
        in_specs=[pl.BlockSpec((tm, d), lambda i: (i, 0)), _resident((1, d)), _resident(w.shape)],
        out_specs=[head_spec] * head_major + [flat_spec] * (n_out - head_major),
        out_shape=[head_shape] * head_major + [flat_shape] * (n_out - head_major),
        compiler_params=_params("parallel"),
        name="norm_matmul",
    )(x, g.reshape(1, d), w)


def _ffn_kernel(g_ref, fg_ref, xs_ref, wg_ref, wu_ref, wd_ref, x_hbm, o_ref, os_ref, xbuf, h_ref, sem, *, final_norm):
    i, f = pl.program_id(0), pl.program_id(1)
    tm = o_ref.shape[0]
    ns = xs_ref.shape[0]
    last_f = pl.num_programs(1) - 1

    def x_copy(tile):
        return pltpu.make_async_copy(x_hbm.at[pl.ds(pl.multiple_of(tile * tm, tm), tm), :], xbuf, sem)

    @pl.when((i == 0) & (f == 0))
    def _():
        x_copy(0).start()
        xs = xs_ref[...]
        h_ref[tm:, :] = jnp.zeros((h_ref.shape[0] - tm, h_ref.shape[1]), h_ref.dtype)
        h_ref[tm:tm + ns, :] = _rms(xs, g_ref[...]).astype(h_ref.dtype)
        os_ref[...] = xs

    @pl.when(f == 0)
    def _():
        x_copy(i).wait()
        x = xbuf[...]
        h_ref[0:tm, :] = _rms(x, g_ref[...]).astype(h_ref.dtype)
        o_ref[...] = x

    @pl.when((f == 1) & (i + 1 < pl.num_programs(0)))
    def _():
        x_copy(i + 1).start()

    h = h_ref[...]
    gate = jnp.dot(h, wg_ref[...].astype(BF16), preferred_element_type=F32)
    up = jnp.dot(h, wu_ref[...].astype(BF16), preferred_element_type=F32)
    act = (gate * _sigmoid(gate) * up).astype(BF16)
    down = jnp.dot(act, wd_ref[...].astype(BF16), preferred_element_type=F32)
    o_ref[...] += down[0:tm]

    @pl.when(i == 0)
    def _():
        os_ref[...] += down[tm:tm + ns]

    if final_norm:
        @pl.when(f == last_f)
        def _():
            o_ref[...] = _rms(o_ref[...], fg_ref[...])

        @pl.when((i == 0) & (f == last_f))
        def _():
            os_ref[...] = _rms(os_ref[...], fg_ref[...])


BF16_ROWS = 16


def ffn(x, xs, g, wg, wu, wd, layer, final_g, tm, tf, final_norm):
    m, d = x.shape
    ns = xs.shape[0]
    dff = wg.shape[2]
    tm = min(tm, m)
    assert m % tm == 0 and dff % tf == 0 and dff // tf >= 2 and ns <= BF16_ROWS and tm % BF16_ROWS == 0
    return pl.pallas_call(
        functools.partial(_ffn_kernel, final_norm=final_norm),
        grid=(m // tm, dff // tf),
        in_specs=[_resident((1, d)), _resident((1, d)), _resident((ns, d)),
                  pl.BlockSpec((None, d, tf), lambda i, f: (layer, 0, f)),
                  pl.BlockSpec((None, d, tf), lambda i, f: (layer, 0, f)),
                  pl.BlockSpec((None, tf, d), lambda i, f: (layer, f, 0)),
                  pl.BlockSpec(memory_space=pl.ANY)],
        out_specs=[pl.BlockSpec((tm, d), lambda i, f: (i, 0)), pl.BlockSpec((ns, d), lambda i, f: (0, 0))],
        out_shape=[jax.ShapeDtypeStruct((m, d), F32), jax.ShapeDtypeStruct((ns, d), F32)],
        scratch_shapes=[pltpu.VMEM((tm, d), F32), pltpu.VMEM((tm + BF16_ROWS, d), BF16),
                        pltpu.SemaphoreType.DMA(())],
        compiler_params=_params("arbitrary", "arbitrary"),
        name="ffn",
    )(g.reshape(1, d), final_g.reshape(1, d), xs, wg, wu, wd, x)


A_HALO = 32
B_HALO = 8


def _layer_norm_silu(x, g, b):
    mu = jnp.mean(x, axis=-1, keepdims=True)
    xc = x - mu
    y = xc * lax.rsqrt(jnp.mean(xc * xc, axis=-1, keepdims=True) + EPS) * g + b
    return y * _sigmoid(y)


def _conv_mix_kernel(av_ref, ag_ref, bh_ref, bb_ref, bc_ref, x_ref, adw_ref, adwb_ref, lng_ref, lnb_ref,
                     bdw_ref, wo_ref, y_ref, newa_ref, newb_ref, aext, cext, ashift, aconv, *, ta, tb, rc):
    l = pl.program_id(1)
    tl = av_ref.shape[1]
    wa = av_ref.shape[2]

    @pl.when(l == 0)
    def _():
        aext[0:A_HALO, :] = jnp.zeros((A_HALO, wa), F32)
        cext[0:B_HALO, :] = jnp.zeros((B_HALO, wa), F32)

    @pl.when(l > 0)
    def _():
        aext[0:A_HALO, :] = aext[tl:tl + A_HALO, :]
        cext[0:B_HALO, :] = cext[tl:tl + B_HALO, :]

    aext[A_HALO:A_HALO + tl, :] = av_ref[0] * _sigmoid(ag_ref[0])
    cext[B_HALO:B_HALO + tl, :] = bc_ref[0] * bh_ref[0]

    bconv = jnp.zeros((tl, wa), F32)
    for k in range(tb):
        off = B_HALO - (tb - 1) + k
        bconv = bconv + bdw_ref[k:k + 1, :] * cext[off:off + tl, :]
    b_out = (bb_ref[0] * bconv).astype(BF16)
    y_ref[0] = x_ref[0] + jnp.dot(b_out, wo_ref[wa:, :], preferred_element_type=F32)

    n_shift = ashift.shape[1]
    for s in range(1, SUBLANES):
        ashift[s - 1] = aext[s:s + n_shift, :]

    for r0 in range(0, tl, rc):
        for c0 in range(0, wa, LANES):
            acc = jnp.broadcast_to(adwb_ref[:, c0:c0 + LANES], (rc, LANES))
            for k in range(ta):
                off = A_HALO - (ta - 1) + r0 + k
                s = off % SUBLANES
                src = aext if s == 0 else ashift.at[s - 1]
                acc = acc + adw_ref[k:k + 1, c0:c0 + LANES] * src[off - s:off - s + rc, c0:c0 + LANES]
            aconv[r0:r0 + rc, c0:c0 + LANES] = acc
    a_out = _layer_norm_silu(aconv[...], lng_ref[...], lnb_ref[...]).astype(BF16)
    y_ref[0] += jnp.dot(a_out, wo_ref[0:wa, :], preferred_element_type=F32)

    @pl.when(l == pl.num_programs(1) - 1)
    def _():
        newa_ref[0] = aext[A_HALO + tl - (ta - 1):A_HALO + tl, :]
        newb_ref[0] = cext[B_HALO + tl - (tb - 1):B_HALO + tl, :]


def conv_mix(parts, x, a_dw, a_dw_b, ln_g, ln_b, b_dw, w_out, tl):
    bsz, seq, d = x.shape
    wa = parts[0].shape[-1]
    ta, tb = a_dw.shape[0], b_dw.shape[0]
    tl = min(tl, seq)
    assert seq % tl == 0 and tl >= A_HALO and ta - 1 <= A_HALO and tb - 1 <= B_HALO
    part_spec = pl.BlockSpec((1, tl, wa), lambda b, l: (b, l, 0))
    const = _resident
    return pl.pallas_call(
        functools.partial(_conv_mix_kernel, ta=ta, tb=tb, rc=min(128, tl)),
        grid=(bsz, seq // tl),
        in_specs=[part_spec] * 5 + [pl.BlockSpec((1, tl, d), lambda b, l: (b, l, 0)),
                                    const((ta, wa)), const((1, wa)), const((1, wa)), const((1, wa)),
                                    const((tb, wa)), const(w_out.shape)],
        out_specs=[pl.BlockSpec((1, tl, d), lambda b, l: (b, l, 0)),
                   pl.BlockSpec((1, ta - 1, wa), lambda b, l: (b, 0, 0)),
                   pl.BlockSpec((1, tb - 1, wa), lambda b, l: (b, 0, 0))],
        out_shape=[jax.ShapeDtypeStruct((bsz, seq, d), F32),
                   jax.ShapeDtypeStruct((bsz, ta - 1, wa), F32),
                   jax.ShapeDtypeStruct((bsz, tb - 1, wa), F32)],
        scratch_shapes=[pltpu.VMEM((A_HALO + tl, wa), F32), pltpu.VMEM((B_HALO + tl, wa), F32),
                        pltpu.VMEM((SUBLANES - 1, A_HALO + tl - SUBLANES, wa), F32),
                        pltpu.VMEM((tl, wa), F32)],
        compiler_params=_params("parallel", "arbitrary"),
        name="conv_mix",
    )(*parts, x, a_dw, a_dw_b.reshape(1, wa), ln_g.reshape(1, wa), ln_b.reshape(1, wa), b_dw, w_out)


def _conv_step_kernel(av_ref, ag_ref, bh_ref, bb_ref, bc_ref, x_ref, sta_ref, stb_ref, adw_ref, adwb_ref,
                      lng_ref, lnb_ref, bdw_ref, wo_ref, y_ref, anew_ref, cnew_ref, *, ta, tb):
    a = av_ref[...] * _sigmoid(ag_ref[...])
    c = bc_ref[...] * bh_ref[...]
    anew_ref[...] = a
    cnew_ref[...] = c
    acc = adwb_ref[...] + adw_ref[ta - 1:ta, :] * a
    for k in range(ta - 1):
        acc = acc + adw_ref[k:k + 1, :] * sta_ref[k]
    a_out = _layer_norm_silu(acc, lng_ref[...], lnb_ref[...])
    bconv = bdw_ref[tb - 1:tb, :] * c
    for k in range(tb - 1):
        bconv = bconv + bdw_ref[k:k + 1, :] * stb_ref[k]
    mix = jnp.concatenate([a_out, bb_ref[...] * bconv], axis=-1).astype(BF16)
    y_ref[...] = x_ref[...] + jnp.dot(mix, wo_ref[...], preferred_element_type=F32)


def conv_step(parts, x, st_a, st_b, a_dw, a_dw_b, ln_g, ln_b, b_dw, w_out):
    bd, d = x.shape
    wa = parts[0].shape[-1]
    ta, tb = a_dw.shape[0], b_dw.shape[0]
    return pl.pallas_call(
        functools.partial(_conv_step_kernel, ta=ta, tb=tb),
        out_shape=[jax.ShapeDtypeStruct((bd, d), F32), jax.ShapeDtypeStruct((bd, wa), F32),
                   jax.ShapeDtypeStruct((bd, wa), F32)],
        compiler_params=pltpu.CompilerParams(vmem_limit_bytes=VMEM_LIMIT),
        name="conv_step",
    )(*parts, x, st_a, st_b, a_dw, a_dw_b.reshape(1, wa), ln_g.reshape(1, wa), ln_b.reshape(1, wa), b_dw, w_out)


LOG2E = 1.4426950408889634
TQ = 2 * MOBA_BLOCK
DA = 2 * HEAD_DIM


def _split3(x):
    def top_bits(v):
        return lax.bitcast_convert_type(lax.bitcast_convert_type(v, jnp.uint32) & jnp.uint32(0xFFFF0000), F32)

    hi = top_bits(x)
    mid = top_bits(x - hi)
    return hi, mid, x - hi - mid


def _moba_setup(q_ref, k_ref, v_ref, qaug, kaug, vaug, kmean, slope2, nb):
    seq = k_ref.shape[1]
    bs = MOBA_BLOCK
    nbp = kmean.shape[0]
    k = k_ref[0]
    q = q_ref[0]
    kmean[...] = jnp.zeros_like(kmean)
    kmean[0:nb, :] = jnp.mean(k.reshape(nb, bs, HEAD_DIM), axis=1)

    gate = lax.dot_general(kmean[...], q, (((1,), (1,)), ((), ())), precision=lax.Precision.HIGHEST,
                           preferred_element_type=F32)
    blk = lax.broadcasted_iota(jnp.int32, (nbp, seq), 0)
    own = lax.broadcasted_iota(jnp.int32, (nbp, seq), 1) // bs
    gate = jnp.where(blk < own, gate, -jnp.inf)
    attend = blk == own
    for _ in range(MOBA_TOPK):
        m = jnp.max(gate, axis=0, keepdims=True)
        idx = jnp.min(jnp.where(gate == m, blk, nbp), axis=0, keepdims=True)
        hit = blk == idx
        attend = attend | (hit & (blk < own))
        gate = jnp.where(hit, -jnp.inf, gate)
    attend_t = jnp.where(attend, 1.0, 0.0)
    attend_r = jnp.concatenate([attend_t, jnp.zeros((LANES - nbp, seq), F32)], axis=0).T

    lane = lax.broadcasted_iota(jnp.int32, (seq, LANES), 1)
    pos_i = lax.broadcasted_iota(jnp.int32, (seq, LANES), 0)
    pos = pos_i.astype(F32)

    khi, kmid, klo = _split3(slope2 * pos)
    ek = jnp.where(lane == pos_i // bs, 1.0, 0.0)
    ek = jnp.where(lane == nb, khi, ek)
    ek = jnp.where(lane == nb + 1, kmid, ek)
    ek = jnp.where(lane == nb + 2, klo, ek)
    ek = jnp.where((lane >= nb + 3) & (lane < nb + 6), 1.0, ek)
    kaug[:, 0:HEAD_DIM] = k.astype(BF16)
    kaug[:, HEAD_DIM:] = ek.astype(BF16)

    qhi, qmid, qlo = _split3(-slope2 * pos)
    eq = jnp.where(lane < nb, jnp.where(attend_r > 0.5, 0.0, MASKED), 0.0)
    eq = jnp.where((lane >= nb) & (lane < nb + 3), 1.0, eq)
    eq = jnp.where(lane == nb + 3, qhi, eq)
    eq = jnp.where(lane == nb + 4, qmid, eq)
    eq = jnp.where(lane == nb + 5, qlo, eq)
    qaug[:, 0:HEAD_DIM] = (q * (HEAD_DIM ** -0.5 * LOG2E)).astype(BF16)
    qaug[:, HEAD_DIM:] = eq.astype(BF16)

    vaug[:, 0:HEAD_DIM] = v_ref[0].astype(BF16)
    vaug[:, HEAD_DIM:] = jnp.ones((seq, HEAD_DIM), BF16)


SUM_CHAINS = 16
PAGE_LOOKAHEAD = 3


class _PageSums:
    def __init__(self, pt_ref, kc_hbm, ksum_ref, ring, sem, *, layer, pages_per_block, n_steps):
        self.pt_ref, self.kc_hbm, self.ksum_ref, self.ring, self.sem = pt_ref, kc_hbm, ksum_ref, ring, sem
        self.layer, self.pages_per_block, self.n_steps = layer, pages_per_block, n_steps
        self.total = pt_ref.shape[0]
        self.depth = ring.shape[0]
        self.per = self.depth // PAGE_LOOKAHEAD
        self.even = n_steps * self.per == self.total

    def _copy(self, t, slot):
        return pltpu.make_async_copy(self.kc_hbm.at[self.layer, self.pt_ref[t]], self.ring.at[slot],
                                     self.sem.at[slot])

    def begin(self):
        self.ksum_ref[...] = jnp.zeros_like(self.ksum_ref)
        for slot in range(min(self.depth, self.total)):
            self._copy(slot, slot).start()

    def _fold_page(self, t, slot):
        _, page_rows, n_heads, _ = self.ring.shape
        self._copy(t, slot).wait()
        page = self.ring[slot].reshape(SUM_CHAINS, page_rows // SUM_CHAINS, n_heads, HEAD_DIM)
        self.ksum_ref[t // self.pages_per_block] += jnp.sum(jnp.sum(page, axis=1), axis=0)

    def _first_slot(self, step):
        return (step % PAGE_LOOKAHEAD) * self.per

    def fold(self, step):
        assert self.even
        for j in range(self.per):
            self._fold_page(step * self.per + j, self._first_slot(step) + j)

    def refill(self, step):
        assert self.even

        @pl.when(step + PAGE_LOOKAHEAD < self.n_steps)
        def _():
            for j in range(self.per):
                self._copy(step * self.per + j + self.depth, self._first_slot(step) + j).start()

    def step(self, step):
        for j in range(self.per):
            t = step * self.per + j

            @pl.when(t < self.total)
            def _(t=t, j=j):
                self._fold_page(t, self._first_slot(step) + j)

                @pl.when(t + self.depth < self.total)
                def _():
                    self._copy(t + self.depth, self._first_slot(step) + j).start()


def _moba_kernel(pt_ref, q_ref, k_ref, v_ref, kc_hbm, o_ref, ksum_ref, qaug, kaug, vaug, kmean, m_ref, acc_ref,
                 s_ref, rel_ref, ring, psem, *, n_heads, nb, layer, page_pairs, page_steps):
    h = pl.program_id(1)
    bs = MOBA_BLOCK
    nt = (((1,), (1,)), ((), ()))
    seq = qaug.shape[0]
    n_tiles = seq // TQ
    n_units = n_tiles * (n_tiles + 1) // 2
    n_pairs = n_units // 2
    bh = pl.program_id(0) * pl.num_programs(1) + h

    slope2 = jnp.exp2((h + 1).astype(F32) * (-8.0 / n_heads) + jnp.zeros((1, 1), F32)) * LOG2E
    _moba_setup(q_ref, k_ref, v_ref, qaug, kaug, vaug, kmean, slope2, nb)
    rel_ref[...] = (lax.broadcasted_iota(jnp.int32, (bs, bs), 1) - lax.broadcasted_iota(jnp.int32, (bs, bs), 0))

    halves = (slice(0, bs), slice(bs, TQ))

    def tile_rows(t):
        return pl.ds(pl.multiple_of(t * TQ, TQ), TQ)

    def successor(unit):
        i, g = unit
        wrap = g == i
        return jnp.where(wrap, jnp.minimum(i + 1, n_tiles - 1), i), jnp.where(wrap, 0, g + 1)

    def scores(unit):
        i, g = unit
        qa = qaug[tile_rows(i), :]
        kg = kaug[tile_rows(g), :]
        return jnp.concatenate([lax.dot_general(qa[r], kg, nt, preferred_element_type=F32) for r in halves], axis=0)

    def process(unit, slot):
        i, g = unit
        vg = vaug[tile_rows(g), :]
        for r in halves:
            diag = jnp.where(rel_ref[...] <= (i - g) * TQ, s_ref[slot, r, r], MASKED)
            sr = jnp.concatenate([diag, s_ref[slot, r, bs:]] if r.start == 0 else [s_ref[slot, r, 0:bs], diag], axis=1)
            m = jnp.where(g == 0, MASKED, m_ref[r, :])
            m_new = jnp.maximum(m, jnp.max(sr, axis=1, keepdims=True))
            p = jnp.exp2(sr - m_new).astype(BF16)
            pv = jnp.dot(p, vg, preferred_element_type=F32)
            acc = jnp.exp2(m - m_new) * acc_ref[r, :] + pv
            acc_ref[r, :] = acc
            m_ref[r, :] = m_new
            rows = pl.ds(pl.multiple_of(i * TQ + r.start, bs), bs)
            o_ref[0, rows, :] = acc[:, 0:HEAD_DIM] / acc[:, HEAD_DIM:]

    m_ref[...] = jnp.full(m_ref.shape, MASKED, F32)
    acc_ref[...] = jnp.zeros_like(acc_ref)
    first = (jnp.int32(0), jnp.int32(0))
    s_ref[0] = scores(first)

    pages = _PageSums(pt_ref, kc_hbm, ksum_ref, ring, psem, layer=layer, pages_per_block=bs // PAGE_SIZE,
                      n_steps=page_steps)

    @pl.when(bh == 0)
    def _():
        pages.begin()

    def pair(p, unit, with_pages):
        page_step = bh * page_pairs + p
        if with_pages and pages.even:
            pages.fold(page_step)
        elif with_pages:
            pages.step(page_step)
        nxt = successor(unit)
        s_ref[1] = scores(nxt)
        process(unit, 0)
        nxt2 = successor(nxt)
        s_ref[0] = scores(nxt2)
        process(nxt, 1)
        if with_pages and pages.even:
            pages.refill(page_step)
        return nxt2

    unit = lax.fori_loop(0, page_pairs, functools.partial(pair, with_pages=True), first)
    if page_pairs < n_pairs:
        unit = lax.fori_loop(page_pairs, n_pairs, functools.partial(pair, with_pages=False), unit)
    if n_units % 2:
        process(unit, 0)


def moba_prompt(q, k, v, cache_k, layer, page_table):
    n_heads, bsz, seq, _ = q.shape
    width = n_heads * HEAD_DIM
    assert seq % TQ == 0
    nb = seq // MOBA_BLOCK
    nbp = -(-nb // 8) * 8
    assert nb + 6 <= LANES
    bd, n_pages = page_table.shape
    pages_per_block = MOBA_BLOCK // PAGE_SIZE
    assert n_pages % pages_per_block == 0
    n_tiles = seq // TQ
    n_pairs = n_tiles * (n_tiles + 1) // 4
    assert n_pairs >= 1
    total_pages = bd * n_pages
    even = [p for p in range(1, n_pairs + 1) if total_pages % (bsz * n_heads * p) == 0]
    page_pairs = max(even) if even else n_pairs
    page_steps = bsz * n_heads * page_pairs
    pages_per_step = -(-total_pages // page_steps)
    sums_shape = (total_pages // pages_per_block, n_heads, HEAD_DIM)
    slab = pl.BlockSpec((None, 1, seq, HEAD_DIM), lambda b, h: (h, b, 0, 0))
    return pl.pallas_call(
        functools.partial(_moba_kernel, n_heads=n_heads, nb=nb, layer=layer, page_pairs=page_pairs,
                          page_steps=page_steps),
        grid=(bsz, n_heads),
        in_specs=[pl.BlockSpec(memory_space=pltpu.SMEM), slab, slab, slab, pl.BlockSpec(memory_space=pl.ANY)],
        out_specs=[pl.BlockSpec((1, seq, HEAD_DIM), lambda b, h: (b, 0, h)),
                   pl.BlockSpec(sums_shape, lambda b, h: (0, 0, 0))],
        out_shape=[jax.ShapeDtypeStruct((bsz, seq, width), F32), jax.ShapeDtypeStruct(sums_shape, F32)],
        scratch_shapes=[pltpu.VMEM((seq, DA), BF16), pltpu.VMEM((seq, DA), BF16), pltpu.VMEM((seq, DA), BF16),
                        pltpu.VMEM((nbp, HEAD_DIM), F32), pltpu.VMEM((TQ, 1), F32), pltpu.VMEM((TQ, DA), F32),
                        pltpu.VMEM((2, TQ, TQ), F32), pltpu.VMEM((MOBA_BLOCK, MOBA_BLOCK), jnp.int32),
                        pltpu.VMEM((PAGE_LOOKAHEAD * pages_per_step, PAGE_SIZE, n_heads, HEAD_DIM), F32),
                        pltpu.SemaphoreType.DMA((PAGE_LOOKAHEAD * pages_per_step,))],
        compiler_params=_params("arbitrary", "arbitrary"),
        name="moba_prompt",
    )(page_table.reshape(-1), q, k, v, cache_k)


PAGES_IN_FLIGHT = 8


def _gate_step_kernel(q_ref, ksum_ref, sel_ref, *, n_heads):
    bd = q_ref.shape[0]
    nblk = ksum_ref.shape[0] // bd
    sel_ref[...] = jnp.zeros_like(sel_ref)
    for b in range(bd):
        kmean = ksum_ref[b * nblk:(b + 1) * nblk] * (1.0 / MOBA_BLOCK)
        gate = jnp.sum(kmean * q_ref[b][None], axis=-1)
        blk = lax.broadcasted_iota(jnp.int32, gate.shape, 0)
        for r in range(MOBA_TOPK):
            m = jnp.max(gate, axis=0, keepdims=True)
            idx = jnp.min(jnp.where(gate == m, blk, nblk), axis=0, keepdims=True)
            sel_ref[b, r:r + 1, 0:n_heads] = idx
            gate = jnp.where(blk == idx, -jnp.inf, gate)


SEL_ROWS = 8


def gate_step(block_sums, q):
    bd, n_heads, _ = q.shape
    assert MOBA_TOPK <= SEL_ROWS and n_heads <= LANES
    sel = pl.pallas_call(
        functools.partial(_gate_step_kernel, n_heads=n_heads),
        out_shape=jax.ShapeDtypeStruct((bd, SEL_ROWS, LANES), jnp.int32),
        compiler_params=pltpu.CompilerParams(vmem_limit_bytes=VMEM_LIMIT),
        name="gate_step",
    )(q, block_sums)
    return sel[:, :MOBA_TOPK, :n_heads]


def _attn_step_kernel(pt_ref, sel_ref, q_ref, kn_ref, vn_ref, k_hbm, v_hbm, o_ref, kbuf, vbuf, ksem, vsem,
                      m_ref, l_ref, acc_ref, *, layer, n_pages, q_pos, pages_per_block):
    bd, n_heads, _ = q_ref.shape
    per_head = MOBA_TOPK * pages_per_block
    total = bd * n_heads * per_head
    depth = kbuf.shape[0]
    scale = HEAD_DIM ** -0.5
    own = q_pos // MOBA_BLOCK

    def coords(t):
        bh, j = t // per_head, t % per_head
        b, h = bh // n_heads, bh % n_heads
        return b, h, j, sel_ref[(b * MOBA_TOPK + j // pages_per_block) * n_heads + h]

    def page_copies(t, slot):
        b, _, j, n = coords(t)
        page = pt_ref[b * n_pages + n * pages_per_block + j % pages_per_block]
        return (pltpu.make_async_copy(k_hbm.at[layer, page], kbuf.at[slot], ksem.at[slot]),
                pltpu.make_async_copy(v_hbm.at[layer, page], vbuf.at[slot], vsem.at[slot]))

    for t in range(min(depth, total)):
        for c in page_copies(t, t):
            c.start()

    @pl.loop(0, total)
    def _(t):
        slot = t % depth
        b, h, j, n = coords(t)
        for c in page_copies(t, slot):
            c.wait()
        q = q_ref[b, pl.ds(h, 1), :]

        @pl.when(j == 0)
        def _():
            m_ref[...] = jnp.sum(q * kn_ref[b, pl.ds(h, 1), :], axis=1, keepdims=True) * scale
            l_ref[...] = jnp.ones_like(l_ref)
            acc_ref[...] = vn_ref[b, pl.ds(h, 1), :]

        k = kbuf[slot, pl.ds(h, PAGE_SIZE, stride=n_heads), :]
        v = vbuf[slot, pl.ds(h, PAGE_SIZE, stride=n_heads), :]
        slope = jnp.exp2(jnp.asarray(h + 1, F32) * (-8.0 / n_heads) + jnp.zeros((1, 1), F32))
        kpos = (n * MOBA_BLOCK + (j % pages_per_block) * PAGE_SIZE
                + lax.broadcasted_iota(jnp.int32, (PAGE_SIZE, 1), 0))
        dist = (q_pos - kpos).astype(F32)
        s = jnp.sum(k * q, axis=1, keepdims=True) * scale - slope * dist
        s = jnp.where((dist >= 0.0) & (n < own), s, MASKED)
        m = m_ref[...]
        m_new = jnp.maximum(m, jnp.max(s, axis=0, keepdims=True))
        alpha = jnp.exp(m - m_new)
        p = jnp.exp(s - m_new)
        l_ref[...] = alpha * l_ref[...] + jnp.sum(p, axis=0, keepdims=True)
        acc_ref[...] = alpha * acc_ref[...] + jnp.sum(p * v, axis=0, keepdims=True)
        m_ref[...] = m_new

        @pl.when(j == per_head - 1)
        def _():
            o_ref[b, pl.ds(h, 1), :] = acc_ref[...] / l_ref[...]

        @pl.when(t + depth < total)
        def _():
            for c in page_copies(t + depth, slot):
                c.start()


def attn_step(cache_k, cache_v, layer, page_table, sel, q, k_new, v_new):
    bd, n_pages = page_table.shape
    n_heads = q.shape[1]
    pages_per_block = MOBA_BLOCK // PAGE_SIZE
    assert n_pages % pages_per_block == 0
    rows = PAGE_SIZE * n_heads
    as_rows = lambda c: c.reshape(c.shape[0], c.shape[1], rows, HEAD_DIM)
    vmem, smem = pl.BlockSpec(memory_space=pltpu.VMEM), pl.BlockSpec(memory_space=pltpu.SMEM)
    hbm = pl.BlockSpec(memory_space=pl.ANY)
    return pl.pallas_call(
        functools.partial(_attn_step_kernel, layer=layer, n_pages=n_pages, q_pos=n_pages * PAGE_SIZE,
                          pages_per_block=pages_per_block),
        in_specs=[smem, smem, vmem, vmem, vmem, hbm, hbm],
        out_specs=vmem,
        out_shape=jax.ShapeDtypeStruct((bd, n_heads, HEAD_DIM), F32),
        scratch_shapes=[pltpu.VMEM((PAGES_IN_FLIGHT, rows, HEAD_DIM), F32),
                        pltpu.VMEM((PAGES_IN_FLIGHT, rows, HEAD_DIM), F32),
                        pltpu.SemaphoreType.DMA((PAGES_IN_FLIGHT,)), pltpu.SemaphoreType.DMA((PAGES_IN_FLIGHT,)),
                        pltpu.VMEM((1, 1), F32), pltpu.VMEM((1, 1), F32), pltpu.VMEM((1, HEAD_DIM), F32)],
        compiler_params=pltpu.CompilerParams(vmem_limit_bytes=VMEM_LIMIT),
        name="attn_step",
    )(page_table.reshape(-1), sel.reshape(-1), q, k_new, v_new, as_rows(cache_k), as_rows(cache_v))


P_HALO = 16


def _pool_out_kernel(o_ref, u_ref, x_ref, pw_ref, ps_ref, wo_ref, y_ref, newp_ref, uext, mix):
    l = pl.program_id(1)
    tl = u_ref.shape[1]
    wp = u_ref.shape[2]
    wo_attn = o_ref.shape[2]
    pg = wp // len(POOL_WINDOWS)

    @pl.when(l == 0)
    def _():
        uext[0:P_HALO, :] = jnp.zeros((P_HALO, wp), F32)

    @pl.when(l > 0)
    def _():
        uext[0:P_HALO, :] = uext[tl:tl + P_HALO, :]

    uext[P_HALO:P_HALO + tl, :] = u_ref[0]
    o_bf16 = o_ref[0].astype(BF16)
    n_chunk = y_ref.shape[2] // len(POOL_WINDOWS)
    pos = l * tl + lax.broadcasted_iota(jnp.int32, (tl, pg), 0)
    for g, w in enumerate(POOL_WINDOWS):
        n0 = g * n_chunk
        y_ref[0, :, n0:n0 + n_chunk] = x_ref[0, :, n0:n0 + n_chunk] + jnp.dot(
            o_bf16, wo_ref[0:wo_attn, n0:n0 + n_chunk], preferred_element_type=F32)
        c0 = g * pg
        wsum = uext[P_HALO:P_HALO + tl, c0:c0 + pg]
        for j in range(1, w):
            wsum = wsum + uext[P_HALO - j:P_HALO - j + tl, c0:c0 + pg]
        count = jnp.minimum(pos + 1, w).astype(F32)
        dlt = (wsum / count - uext[P_HALO:P_HALO + tl, c0:c0 + pg]).astype(BF16)
        yp = jnp.dot(dlt, pw_ref[g], preferred_element_type=F32) * ps_ref[:, c0:c0 + pg]
        mix[:, c0:c0 + pg] = yp.astype(BF16)

    y_ref[0] += jnp.dot(mix[...], wo_ref[wo_attn:, :], preferred_element_type=F32)

    @pl.when(l == pl.num_programs(1) - 1)
    def _():
        newp_ref[0] = uext[P_HALO + tl - (POOL_MAX - 1):P_HALO + tl, :]


def pool_out(o, u, x, pool_w, pool_scale, w_out, tl):
    bsz, seq, d = x.shape
    wp = u.shape[-1]
    tl = min(tl, seq)
    assert seq % tl == 0 and tl >= P_HALO
    const = _resident
    return pl.pallas_call(
        _pool_out_kernel,
        grid=(bsz, seq // tl),
        in_specs=[pl.BlockSpec((1, tl, o.shape[-1]), lambda b, l: (b, l, 0)),
                  pl.BlockSpec((1, tl, wp), lambda b, l: (b, l, 0)),
                  pl.BlockSpec((1, tl, d), lambda b, l: (b, l, 0)),
                  const(pool_w.shape), const((1, wp)), const(w_out.shape)],
        out_specs=[pl.BlockSpec((1, tl, d), lambda b, l: (b, l, 0)),
                   pl.BlockSpec((1, POOL_MAX - 1, wp), lambda b, l: (b, 0, 0))],
        out_shape=[jax.ShapeDtypeStruct((bsz, seq, d), F32),
                   jax.ShapeDtypeStruct((bsz, POOL_MAX - 1, wp), F32)],
        scratch_shapes=[pltpu.VMEM((P_HALO + tl, wp), F32), pltpu.VMEM((tl, wp), BF16)],
        compiler_params=_params("parallel", "arbitrary"),
        name="pool_out",
    )(o, u, x, pool_w, pool_scale.reshape(1, wp), w_out)


def _pool_step_kernel(o_ref, u_ref, x_ref, stp_ref, pw_ref, ps_ref, wo_ref, y_ref, *, start_pos):
    wp = u_ref.shape[1]
    pg = wp // len(POOL_WINDOWS)
    u = u_ref[...]
    parts = [o_ref[...].astype(BF16)]
    for g, w in enumerate(POOL_WINDOWS):
        c0 = g * pg
        wsum = u[:, c0:c0 + pg]
        for j in range(1, w):
            wsum = wsum + stp_ref[POOL_MAX - 1 - j][:, c0:c0 + pg]
        count = float(min(start_pos + 1, w))
        dlt = (wsum / count - u[:, c0:c0 + pg]).astype(BF16)
        yp = jnp.dot(dlt, pw_ref[g], preferred_element_type=F32) * ps_ref[:, c0:c0 + pg]
        parts.append(yp.astype(BF16))
    mix = jnp.concatenate(parts, axis=-1)
    y_ref[...] = x_ref[...] + jnp.dot(mix, wo_ref[...], preferred_element_type=F32)


def pool_step(o, u, x, st_p, pool_w, pool_scale, w_out, start_pos):
    bd, d = x.shape
    return pl.pallas_call(
        functools.partial(_pool_step_kernel, start_pos=start_pos),
        out_shape=jax.ShapeDtypeStruct((bd, d), F32),
        compiler_params=pltpu.CompilerParams(vmem_limit_bytes=VMEM_LIMIT),
        name="pool_step",
    )(o, u, x, st_p, pool_w, pool_scale.reshape(1, -1), w_out)


TM_PROMPT = 512
TM_FFN = 1024
TF = 512
TL_MIX = 256


def kernel(x_prompt, x_sample, state_conv_a, state_conv_b, cache_k, cache_v, state_pool, page_table, conv_norm_g, conv_w_in, conv_a_dw, conv_a_dw_b, conv_a_ln_g, conv_a_ln_b, conv_b_dw, conv_w_out, attn_norm_g, attn_w_in, pool_w, pool_scale, attn_w_out, ffn_norm_g, ffn_w_gate, ffn_w_up, ffn_w_down, final_norm_g):
    bsz, seq, d = x_prompt.shape
    bd, dec_seq, _ = x_sample.shape
    assert dec_seq == 1
    depth = ffn_norm_g.shape[0]
    n_heads = cache_k.shape[3]
    wattn = n_heads * HEAD_DIM
    past_len = page_table.shape[1] * PAGE_SIZE

    layer_bf16 = lambda w, n: w[n].astype(BF16)

    xp = x_prompt.reshape(bsz * seq, d)
    xs = x_sample.reshape(bd, d)
    pa, pb, pk, pv, pp = [], [], [], [], []
    sa, sb, sk, sv, sp = [], [], [], [], []
    for layer in range(depth):
        i = layer // 2
        if layer % 2 == 0:
            w_in, w_out = layer_bf16(conv_w_in, i), layer_bf16(conv_w_out, i)
            parts = norm_matmul(xp, conv_norm_g[i], w_in, TM_PROMPT)
            wa = parts[0].shape[-1]
            y, na, nb_ = conv_mix([t.reshape(bsz, seq, wa) for t in parts], xp.reshape(bsz, seq, d),
                                  conv_a_dw[i], conv_a_dw_b[i], conv_a_ln_g[i], conv_a_ln_b[i], conv_b_dw[i],
                                  w_out, TL_MIX)
            xp = y.reshape(bsz * seq, d)
            pa.append(na)
            pb.append(nb_)

            parts = norm_matmul(xs, conv_norm_g[i], w_in, TM_PROMPT)
            st_a = jnp.swapaxes(state_conv_a[i], 0, 1)
            st_b = jnp.swapaxes(state_conv_b[i], 0, 1)
            xs, a_new, c_new = conv_step(parts, xs, st_a, st_b, conv_a_dw[i], conv_a_dw_b[i], conv_a_ln_g[i],
                                         conv_a_ln_b[i], conv_b_dw[i], w_out)
            sa.append(jnp.concatenate([state_conv_a[i][:, 1:], a_new[:, None]], axis=1))
            sb.append(jnp.concatenate([state_conv_b[i][:, 1:], c_new[:, None]], axis=1))
        else:
            w_in, w_out, w_pool = layer_bf16(attn_w_in, i), layer_bf16(attn_w_out, i), layer_bf16(pool_w, i)
            q, k, v, u = norm_matmul(xp, attn_norm_g[i], w_in, TM_PROMPT, head_major=3)
            shp = (n_heads, bsz, seq, HEAD_DIM)
            o, block_sums = moba_prompt(q.reshape(shp), k.reshape(shp), v.reshape(shp), cache_k, i, page_table)
            y, npool = pool_out(o, u.reshape(bsz, seq, -1), xp.reshape(bsz, seq, d), w_pool, pool_scale[i],
                                w_out, TL_MIX)
            xp = y.reshape(bsz * seq, d)
            pk.append(jnp.transpose(k.reshape(shp), (1, 2, 0, 3)))
            pv.append(jnp.transpose(v.reshape(shp), (1, 2, 0, 3)))
            pp.append(npool)

            q, k, v, u = norm_matmul(xs, attn_norm_g[i], w_in, TM_PROMPT, head_major=3)
            heads = lambda t: jnp.swapaxes(t, 0, 1)
            sel = gate_step(block_sums, heads(q))
            o = attn_step(cache_k, cache_v, i, page_table, sel, heads(q), heads(k), heads(v))
            st_p = jnp.swapaxes(state_pool[i], 0, 1)
            xs = pool_step(o.reshape(bd, wattn), u, xs, st_p, w_pool, pool_scale[i], w_out, past_len)
            sk.append(heads(k).reshape(bd, 1, n_heads, HEAD_DIM))
            sv.append(heads(v).reshape(bd, 1, n_heads, HEAD_DIM))
            sp.append(jnp.concatenate([state_pool[i][:, 1:], u[:, None]], axis=1))
        last = layer == depth - 1
        ffn_w = (ffn_w_gate, ffn_w_up, ffn_w_down, layer)
        xp, xs = ffn(xp, xs, ffn_norm_g[layer], *ffn_w, final_norm_g, TM_FFN, TF, last)
    return (xp.reshape(bsz, seq, d), xs.reshape(bd, 1, d), jnp.stack(pa), jnp.stack(sa), jnp.stack(pb),
            jnp.stack(sb), jnp.stack(pk), jnp.stack(sk), jnp.stack(pv), jnp.stack(sv), jnp.stack(pp), jnp.stack(sp))
```

```python
import functools

import jax
import jax.numpy as jnp
from jax import lax
from jax.experimental import pallas as pl
from jax.experimental.pallas import tpu as pltpu

EPS = 1e-6
PAGE_SIZE = 128
MOBA_BLOCK = 256
MOBA_TOPK = 3
HEAD_DIM = 128
POOL_WINDOWS = (2, 4, 8, 16)
POOL_MAX = max(POOL_WINDOWS)
MASKED = -1e30

V7X_VMEM_BYTES = 64 * 1024 * 1024
VMEM_LIMIT = V7X_VMEM_BYTES - 4 * 1024 * 1024
LANES = 128
SUBLANES = 8
COL_GROUP = 1024

BF16 = jnp.bfloat16
F32 = jnp.float32


def _params(*sem):
    return pltpu.CompilerParams(dimension_semantics=sem, vmem_limit_bytes=VMEM_LIMIT)


def _rms(x, g):
    ms = jnp.mean(x * x, axis=-1, keepdims=True)
    return x * lax.rsqrt(ms + EPS) * g


def _sigmoid(x):
    return 1.0 / (1.0 + jnp.exp(-x))


def _resident(shape):
    return pl.BlockSpec(shape, lambda *_: (0,) * len(shape), pipeline_mode=pl.Buffered(1))


def _norm_matmul_kernel(x_ref, g_ref, w_ref, *out_refs, head_major):
    h = _rms(x_ref[...], g_ref[...]).astype(BF16)
    for c, o_ref in enumerate(out_refs):
        r = jnp.dot(h, w_ref[:, c * COL_GROUP:(c + 1) * COL_GROUP], preferred_element_type=F32)
        if c < head_major:
            for hh in range(COL_GROUP // HEAD_DIM):
                o_ref[hh] = r[:, hh * HEAD_DIM:(hh + 1) * HEAD_DIM]
        else:
            o_ref[...] = r


def norm_matmul(x, g, w, tm, head_major=0):
    m, d = x.shape
    n_out = w.shape[1] // COL_GROUP
    tm = min(tm, m)
    heads = COL_GROUP // HEAD_DIM
    flat_spec = pl.BlockSpec((tm, COL_GROUP), lambda i: (i, 0))
    head_spec = pl.BlockSpec((heads, tm, HEAD_DIM), lambda i: (0, i, 0))
    flat_shape = jax.ShapeDtypeStruct((m, COL_GROUP), F32)
    head_shape = jax.ShapeDtypeStruct((heads, m, HEAD_DIM), F32)
    return pl.pallas_call(
        functools.partial(_norm_matmul_kernel, head_major=head_major),
        grid=(m // tm,),
        in_specs=[pl.BlockSpec((tm, d), lambda i: (i, 0)), _resident((1, d)), _resident(w.shape)],
        out_specs=[head_spec] * head_major + [flat_spec] * (n_out - head_major),
        out_shape=[head_shape] * head_major + [flat_shape] * (n_out - head_major),
        compiler_params=_params("parallel"),
        name="norm_matmul",
    )(x, g.reshape(1, d), w)


def _ffn_kernel(g_ref, fg_ref, xs_ref, wg_ref, wu_ref, wd_ref, x_hbm, o_ref, os_ref, xbuf, h_ref, sem, *, final_norm):
    i, f = pl.program_id(0), pl.program_id(1)
    tm = o_ref.shape[0]
    ns = xs_ref.shape[0]
    last_f = pl.num_programs(1) - 1

    def x_copy(tile):
        return pltpu.make_async_copy(x_hbm.at[pl.ds(pl.multiple_of(tile * tm, tm), tm), :], xbuf, sem)

    @pl.when((i == 0) & (f == 0))
    def _():
        x_copy(0).start()
        xs = xs_ref[...]
        h_ref[tm:, :] = jnp.zeros((h_ref.shape[0] - tm, h_ref.shape[1]), h_ref.dtype)
        h_ref[tm:tm + ns, :] = _rms(xs, g_ref[...]).astype(h_ref.dtype)
        os_ref[...] = xs

    @pl.when(f == 0)
    def _():
        x_copy(i).wait()
        x = xbuf[...]
        h_ref[0:tm, :] = _rms(x, g_ref[...]).astype(h_ref.dtype)
        o_ref[...] = x

    @pl.when((f == 1) & (i + 1 < pl.num_programs(0)))
    def _():
        x_copy(i + 1).start()

    h = h_ref[...]
    gate = jnp.dot(h, wg_ref[...].astype(BF16), preferred_element_type=F32)
    up = jnp.dot(h, wu_ref[...].astype(BF16), preferred_element_type=F32)
    act = (gate * _sigmoid(gate) * up).astype(BF16)
    down = jnp.dot(act, wd_ref[...].astype(BF16), preferred_element_type=F32)
    o_ref[...] += down[0:tm]

    @pl.when(i == 0)
    def _():
        os_ref[...] += down[tm:tm + ns]

    if final_norm:
        @pl.when(f == last_f)
        def _():
            o_ref[...] = _rms(o_ref[...], fg_ref[...])

        @pl.when((i == 0) & (f == last_f))
        def _():
            os_ref[...] = _rms(os_ref[...], fg_ref[...])


BF16_ROWS = 16


def ffn(x, xs, g, wg, wu, wd, layer, final_g, tm, tf, final_norm):
    m, d = x.shape
    ns = xs.shape[0]
    dff = wg.shape[2]
    tm = min(tm, m)
    assert m % tm == 0 and dff % tf == 0 and dff // tf >= 2 and ns <= BF16_ROWS and tm % BF16_ROWS == 0
    return pl.pallas_call(
        functools.partial(_ffn_kernel, final_norm=final_norm),
        grid=(m // tm, dff // tf),
        in_specs=[_resident((1, d)), _resident((1, d)), _resident((ns, d)),
                  pl.BlockSpec((None, d, tf), lambda i, f: (layer, 0, f)),
                  pl.BlockSpec((None, d, tf), lambda i, f: (layer, 0, f)),
                  pl.BlockSpec((None, tf, d), lambda i, f: (layer, f, 0)),
                  pl.BlockSpec(memory_space=pl.ANY)],
        out_specs=[pl.BlockSpec((tm, d), lambda i, f: (i, 0)), pl.BlockSpec((ns, d), lambda i, f: (0, 0))],
        out_shape=[jax.ShapeDtypeStruct((m, d), F32), jax.ShapeDtypeStruct((ns, d), F32)],
        scratch_shapes=[pltpu.VMEM((tm, d), F32), pltpu.VMEM((tm + BF16_ROWS, d), BF16),
                        pltpu.SemaphoreType.DMA(())],
        compiler_params=_params("arbitrary", "arbitrary"),
        name="ffn",
    )(g.reshape(1, d), final_g.reshape(1, d), xs, wg, wu, wd, x)


A_HALO = 32
B_HALO = 8


def _layer_norm_silu(x, g, b):
    mu = jnp.mean(x, axis=-1, keepdims=True)
    xc = x - mu
    y = xc * lax.rsqrt(jnp.mean(xc * xc, axis=-1, keepdims=True) + EPS) * g + b
    return y * _sigmoid(y)


def _conv_mix_kernel(av_ref, ag_ref, bh_ref, bb_ref, bc_ref, x_ref, adw_ref, adwb_ref, lng_ref, lnb_ref,
                     bdw_ref, wo_ref, y_ref, newa_ref, newb_ref, aext, cext, ashift, aconv, *, ta, tb, rc):
    l = pl.program_id(1)
    tl = av_ref.shape[1]
    wa = av_ref.shape[2]

    @pl.when(l == 0)
    def _():
        aext[0:A_HALO, :] = jnp.zeros((A_HALO, wa), F32)
        cext[0:B_HALO, :] = jnp.zeros((B_HALO, wa), F32)

    @pl.when(l > 0)
    def _():
        aext[0:A_HALO, :] = aext[tl:tl + A_HALO, :]
        cext[0:B_HALO, :] = cext[tl:tl + B_HALO, :]

    aext[A_HALO:A_HALO + tl, :] = av_ref[0] * _sigmoid(ag_ref[0])
    cext[B_HALO:B_HALO + tl, :] = bc_ref[0] * bh_ref[0]

    bconv = jnp.zeros((tl, wa), F32)
    for k in range(tb):
        off = B_HALO - (tb - 1) + k
        bconv = bconv + bdw_ref[k:k + 1, :] * cext[off:off + tl, :]
    b_out = (bb_ref[0] * bconv).astype(BF16)
    y_ref[0] = x_ref[0] + jnp.dot(b_out, wo_ref[wa:, :], preferred_element_type=F32)

    n_shift = ashift.shape[1]
    for s in range(1, SUBLANES):
        ashift[s - 1] = aext[s:s + n_shift, :]

    for r0 in range(0, tl, rc):
        for c0 in range(0, wa, LANES):
            acc = jnp.broadcast_to(adwb_ref[:, c0:c0 + LANES], (rc, LANES))
            for k in range(ta):
                off = A_HALO - (ta - 1) + r0 + k
                s = off % SUBLANES
                src = aext if s == 0 else ashift.at[s - 1]
                acc = acc + adw_ref[k:k + 1, c0:c0 + LANES] * src[off - s:off - s + rc, c0:c0 + LANES]
            aconv[r0:r0 + rc, c0:c0 + LANES] = acc
    a_out = _layer_norm_silu(aconv[...], lng_ref[...], lnb_ref[...]).astype(BF16)
    y_ref[0] += jnp.dot(a_out, wo_ref[0:wa, :], preferred_element_type=F32)

    @pl.when(l == pl.num_programs(1) - 1)
    def _():
        newa_ref[0] = aext[A_HALO + tl - (ta - 1):A_HALO + tl, :]
        newb_ref[0] = cext[B_HALO + tl - (tb - 1):B_HALO + tl, :]


def conv_mix(parts, x, a_dw, a_dw_b, ln_g, ln_b, b_dw, w_out, tl):
    bsz, seq, d = x.shape
    wa = parts[0].shape[-1]
    ta, tb = a_dw.shape[0], b_dw.shape[0]
    tl = min(tl, seq)
    assert seq % tl == 0 and tl >= A_HALO and ta - 1 <= A_HALO and tb - 1 <= B_HALO
    part_spec = pl.BlockSpec((1, tl, wa), lambda b, l: (b, l, 0))
    const = _resident
    return pl.pallas_call(
        functools.partial(_conv_mix_kernel, ta=ta, tb=tb, rc=min(128, tl)),
        grid=(bsz, seq // tl),
        in_specs=[part_spec] * 5 + [pl.BlockSpec((1, tl, d), lambda b, l: (b, l, 0)),
                                    const((ta, wa)), const((1, wa)), const((1, wa)), const((1, wa)),
                                    const((tb, wa)), const(w_out.shape)],
        out_specs=[pl.BlockSpec((1, tl, d), lambda b, l: (b, l, 0)),
                   pl.BlockSpec((1, ta - 1, wa), lambda b, l: (b, 0, 0)),
                   pl.BlockSpec((1, tb - 1, wa), lambda b, l: (b, 0, 0))],
        out_shape=[jax.ShapeDtypeStruct((bsz, seq, d), F32),
                   jax.ShapeDtypeStruct((bsz, ta - 1, wa), F32),
                   jax.ShapeDtypeStruct((bsz, tb - 1, wa), F32)],
        scratch_shapes=[pltpu.VMEM((A_HALO + tl, wa), F32), pltpu.VMEM((B_HALO + tl, wa), F32),
                        pltpu.VMEM((SUBLANES - 1, A_HALO + tl - SUBLANES, wa), F32),
                        pltpu.VMEM((tl, wa), F32)],
        compiler_params=_params("parallel", "arbitrary"),
        name="conv_mix",
    )(*parts, x, a_dw, a_dw_b.reshape(1, wa), ln_g.reshape(1, wa), ln_b.reshape(1, wa), b_dw, w_out)


def _conv_step_kernel(av_ref, ag_ref, bh_ref, bb_ref, bc_ref, x_ref, sta_ref, stb_ref, adw_ref, adwb_ref,
                      lng_ref, lnb_ref, bdw_ref, wo_ref, y_ref, anew_ref, cnew_ref, *, ta, tb):
    a = av_ref[...] * _sigmoid(ag_ref[...])
    c = bc_ref[...] * bh_ref[...]
    anew_ref[...] = a
    cnew_ref[...] = c
    acc = adwb_ref[...] + adw_ref[ta - 1:ta, :] * a
    for k in range(ta - 1):
        acc = acc + adw_ref[k:k + 1, :] * sta_ref[k]
    a_out = _layer_norm_silu(acc, lng_ref[...], lnb_ref[...])
    bconv = bdw_ref[tb - 1:tb, :] * c
    for k in range(tb - 1):
        bconv = bconv + bdw_ref[k:k + 1, :] * stb_ref[k]
    mix = jnp.concatenate([a_out, bb_ref[...] * bconv], axis=-1).astype(BF16)
    y_ref[...] = x_ref[...] + jnp.dot(mix, wo_ref[...], preferred_element_type=F32)


def conv_step(parts, x, st_a, st_b, a_dw, a_dw_b, ln_g, ln_b, b_dw, w_out):
    bd, d = x.shape
    wa = parts[0].shape[-1]
    ta, tb = a_dw.shape[0], b_dw.shape[0]
    return pl.pallas_call(
        functools.partial(_conv_step_kernel, ta=ta, tb=tb),
        out_shape=[jax.ShapeDtypeStruct((bd, d), F32), jax.ShapeDtypeStruct((bd, wa), F32),
                   jax.ShapeDtypeStruct((bd, wa), F32)],
        compiler_params=pltpu.CompilerParams(vmem_limit_bytes=VMEM_LIMIT),
        name="conv_step",
    )(*parts, x, st_a, st_b, a_dw, a_dw_b.reshape(1, wa), ln_g.reshape(1, wa), ln_b.reshape(1, wa), b_dw, w_out)


LOG2E = 1.4426950408889634
TQ = 2 * MOBA_BLOCK
DA = 2 * HEAD_DIM


def _split3(x):
    def top_bits(v):
        return lax.bitcast_convert_type(lax.bitcast_convert_type(v, jnp.uint32) & jnp.uint32(0xFFFF0000), F32)

    hi = top_bits(x)
    mid = top_bits(x - hi)
    return hi, mid, x - hi - mid


def _moba_setup(q_ref, k_ref, v_ref, qaug, kaug, vaug, kmean, slope2, nb):
    seq = k_ref.shape[1]
    bs = MOBA_BLOCK
    nbp = kmean.shape[0]
    k = k_ref[0]
    q = q_ref[0]
    kmean[...] = jnp.zeros_like(kmean)
    kmean[0:nb, :] = jnp.mean(k.reshape(nb, bs, HEAD_DIM), axis=1)

    gate = lax.dot_general(kmean[...], q, (((1,), (1,)), ((), ())), precision=lax.Precision.HIGHEST,
                           preferred_element_type=F32)
    blk = lax.broadcasted_iota(jnp.int32, (nbp, seq), 0)
    own = lax.broadcasted_iota(jnp.int32, (nbp, seq), 1) // bs
    gate = jnp.where(blk < own, gate, -jnp.inf)
    attend = blk == own
    for _ in range(MOBA_TOPK):
        m = jnp.max(gate, axis=0, keepdims=True)
        idx = jnp.min(jnp.where(gate == m, blk, nbp), axis=0, keepdims=True)
        hit = blk == idx
        attend = attend | (hit & (blk < own))
        gate = jnp.where(hit, -jnp.inf, gate)
    attend_t = jnp.where(attend, 1.0, 0.0)
    attend_r = jnp.concatenate([attend_t, jnp.zeros((LANES - nbp, seq), F32)], axis=0).T

    lane = lax.broadcasted_iota(jnp.int32, (seq, LANES), 1)
    pos_i = lax.broadcasted_iota(jnp.int32, (seq, LANES), 0)
    pos = pos_i.astype(F32)

    khi, kmid, klo = _split3(slope2 * pos)
    ek = jnp.where(lane == pos_i // bs, 1.0, 0.0)
    ek = jnp.where(lane == nb, khi, ek)
    ek = jnp.where(lane == nb + 1, kmid, ek)
    ek = jnp.where(lane == nb + 2, klo, ek)
    ek = jnp.where((lane >= nb + 3) & (lane < nb + 6), 1.0, ek)
    kaug[:, 0:HEAD_DIM] = k.astype(BF16)
    kaug[:, HEAD_DIM:] = ek.astype(BF16)

    qhi, qmid, qlo = _split3(-slope2 * pos)
    eq = jnp.where(lane < nb, jnp.where(attend_r > 0.5, 0.0, MASKED), 0.0)
    eq = jnp.where((lane >= nb) & (lane < nb + 3), 1.0, eq)
    eq = jnp.where(lane == nb + 3, qhi, eq)
    eq = jnp.where(lane == nb + 4, qmid, eq)
    eq = jnp.where(lane == nb + 5, qlo, eq)
    qaug[:, 0:HEAD_DIM] = (q * (HEAD_DIM ** -0.5 * LOG2E)).astype(BF16)
    qaug[:, HEAD_DIM:] = eq.astype(BF16)

    vaug[:, 0:HEAD_DIM] = v_ref[0].astype(BF16)
    vaug[:, HEAD_DIM:] = jnp.ones((seq, HEAD_DIM), BF16)


SUM_CHAINS = 16
PAGE_LOOKAHEAD = 4


class _PageSums:
    def __init__(self, pt_ref, kc_hbm, ksum_ref, ring, sem, *, layer, pages_per_block, n_steps):
        self.pt_ref, self.kc_hbm, self.ksum_ref, self.ring, self.sem = pt_ref, kc_hbm, ksum_ref, ring, sem
        self.layer, self.pages_per_block, self.n_steps = layer, pages_per_block, n_steps
        self.total = pt_ref.shape[0]
        self.depth = ring.shape[0]
        self.per = self.depth // PAGE_LOOKAHEAD
        self.even = n_steps * self.per == self.total

    def _copy(self, t, slot):
        return pltpu.make_async_copy(self.kc_hbm.at[self.layer, self.pt_ref[t]], self.ring.at[slot],
                                     self.sem.at[slot])

    def begin(self):
        self.ksum_ref[...] = jnp.zeros_like(self.ksum_ref)
        for slot in range(min(self.depth, self.total)):
            self._copy(slot, slot).start()

    def _fold_page(self, t, slot):
        _, page_rows, n_heads, _ = self.ring.shape
        self._copy(t, slot).wait()
        page = self.ring[slot].reshape(SUM_CHAINS, page_rows // SUM_CHAINS, n_heads, HEAD_DIM)
        self.ksum_ref[t // self.pages_per_block] += jnp.sum(jnp.sum(page, axis=1), axis=0)

    def _first_slot(self, step):
        return (step % PAGE_LOOKAHEAD) * self.per

    def fold(self, step):
        assert self.even
        for j in range(self.per):
            self._fold_page(step * self.per + j, self._first_slot(step) + j)

    def refill(self, step):
        assert self.even

        @pl.when(step + PAGE_LOOKAHEAD < self.n_steps)
        def _():
            for j in range(self.per):
                self._copy(step * self.per + j + self.depth, self._first_slot(step) + j).start()

    def step(self, step):
        for j in range(self.per):
            t = step * self.per + j

            @pl.when(t < self.total)
            def _(t=t, j=j):
                self._fold_page(t, self._first_slot(step) + j)

                @pl.when(t + self.depth < self.total)
                def _():
                    self._copy(t + self.depth, self._first_slot(step) + j).start()


def _moba_kernel(pt_ref, q_ref, k_ref, v_ref, kc_hbm, o_ref, ksum_ref, qaug, kaug, vaug, kmean, m_ref, acc_ref,
                 s_ref, rel_ref, ring, psem, *, n_heads, nb, layer, page_pairs, page_steps):
    h = pl.program_id(1)
    bs = MOBA_BLOCK
    nt = (((1,), (1,)), ((), ()))
    seq = qaug.shape[0]
    n_tiles = seq // TQ
    n_units = n_tiles * (n_tiles + 1) // 2
    n_pairs = n_units // 2
    bh = pl.program_id(0) * pl.num_programs(1) + h

    slope2 = jnp.exp2((h + 1).astype(F32) * (-8.0 / n_heads) + jnp.zeros((1, 1), F32)) * LOG2E
    _moba_setup(q_ref, k_ref, v_ref, qaug, kaug, vaug, kmean, slope2, nb)
    rel_ref[...] = (lax.broadcasted_iota(jnp.int32, (bs, bs), 1) - lax.broadcasted_iota(jnp.int32, (bs, bs), 0))

    halves = (slice(0, bs), slice(bs, TQ))

    def tile_rows(t):
        return pl.ds(pl.multiple_of(t * TQ, TQ), TQ)

    def successor(unit):
        i, g = unit
        wrap = g == i
        return jnp.where(wrap, jnp.minimum(i + 1, n_tiles - 1), i), jnp.where(wrap, 0, g + 1)

    def scores(unit):
        i, g = unit
        qa = qaug[tile_rows(i), :]
        kg = kaug[tile_rows(g), :]
        return jnp.concatenate([lax.dot_general(qa[r], kg, nt, preferred_element_type=F32) for r in halves], axis=0)

    def process(unit, slot):
        i, g = unit
        vg = vaug[tile_rows(g), :]
        for r in halves:
            diag = jnp.where(rel_ref[...] <= (i - g) * TQ, s_ref[slot, r, r], MASKED)
            sr = jnp.concatenate([diag, s_ref[slot, r, bs:]] if r.start == 0 else [s_ref[slot, r, 0:bs], diag], axis=1)
            m = jnp.where(g == 0, MASKED, m_ref[r, :])
            m_new = jnp.maximum(m, jnp.max(sr, axis=1, keepdims=True))
            p = jnp.exp2(sr - m_new).astype(BF16)
            pv = jnp.dot(p, vg, preferred_element_type=F32)
            acc = jnp.exp2(m - m_new) * acc_ref[r, :] + pv
            acc_ref[r, :] = acc
            m_ref[r, :] = m_new
            rows = pl.ds(pl.multiple_of(i * TQ + r.start, bs), bs)
            o_ref[0, rows, :] = acc[:, 0:HEAD_DIM] / acc[:, HEAD_DIM:]

    m_ref[...] = jnp.full(m_ref.shape, MASKED, F32)
    acc_ref[...] = jnp.zeros_like(acc_ref)
    first = (jnp.int32(0), jnp.int32(0))
    s_ref[0] = scores(first)

    pages = _PageSums(pt_ref, kc_hbm, ksum_ref, ring, psem, layer=layer, pages_per_block=bs // PAGE_SIZE,
                      n_steps=page_steps)

    @pl.when(bh == 0)
    def _():
        pages.begin()

    def pair(p, unit, with_pages):
        page_step = bh * page_pairs + p
        if with_pages and pages.even:
            pages.fold(page_step)
        elif with_pages:
            pages.step(page_step)
        nxt = successor(unit)
        s_ref[1] = scores(nxt)
        process(unit, 0)
        nxt2 = successor(nxt)
        s_ref[0] = scores(nxt2)
        process(nxt, 1)
        if with_pages and pages.even:
            pages.refill(page_step)
        return nxt2

    unit = lax.fori_loop(0, page_pairs, functools.partial(pair, with_pages=True), first)
    if page_pairs < n_pairs:
        unit = lax.fori_loop(page_pairs, n_pairs, functools.partial(pair, with_pages=False), unit)
    if n_units % 2:
        process(unit, 0)


def moba_prompt(q, k, v, cache_k, layer, page_table):
    n_heads, bsz, seq, _ = q.shape
    width = n_heads * HEAD_DIM
    assert seq % TQ == 0
    nb = seq // MOBA_BLOCK
    nbp = -(-nb // 8) * 8
    assert nb + 6 <= LANES
    bd, n_pages = page_table.shape
    pages_per_block = MOBA_BLOCK // PAGE_SIZE
    assert n_pages % pages_per_block == 0
    n_tiles = seq // TQ
    n_pairs = n_tiles * (n_tiles + 1) // 4
    assert n_pairs >= 1
    total_pages = bd * n_pages
    even = [p for p in range(1, n_pairs + 1) if total_pages % (bsz * n_heads * p) == 0]
    page_pairs = max(even) if even else n_pairs
    page_steps = bsz * n_heads * page_pairs
    pages_per_step = -(-total_pages // page_steps)
    sums_shape = (total_pages // pages_per_block, n_heads, HEAD_DIM)
    slab = pl.BlockSpec((None, 1, seq, HEAD_DIM), lambda b, h: (h, b, 0, 0))
    return pl.pallas_call(
        functools.partial(_moba_kernel, n_heads=n_heads, nb=nb, layer=layer, page_pairs=page_pairs,
                          page_steps=page_steps),
        grid=(bsz, n_heads),
        in_specs=[pl.BlockSpec(memory_space=pltpu.SMEM), slab, slab, slab, pl.BlockSpec(memory_space=pl.ANY)],
        out_specs=[pl.BlockSpec((1, seq, HEAD_DIM), lambda b, h: (b, 0, h)),
                   pl.BlockSpec(sums_shape, lambda b, h: (0, 0, 0))],
        out_shape=[jax.ShapeDtypeStruct((bsz, seq, width), F32), jax.ShapeDtypeStruct(sums_shape, F32)],
        scratch_shapes=[pltpu.VMEM((seq, DA), BF16), pltpu.VMEM((seq, DA), BF16), pltpu.VMEM((seq, DA), BF16),
                        pltpu.VMEM((nbp, HEAD_DIM), F32), pltpu.VMEM((TQ, 1), F32), pltpu.VMEM((TQ, DA), F32),
                        pltpu.VMEM((2, TQ, TQ), F32), pltpu.VMEM((MOBA_BLOCK, MOBA_BLOCK), jnp.int32),
                        pltpu.VMEM((PAGE_LOOKAHEAD * pages_per_step, PAGE_SIZE, n_heads, HEAD_DIM), F32),
                        pltpu.SemaphoreType.DMA((PAGE_LOOKAHEAD * pages_per_step,))],
        compiler_params=_params("arbitrary", "arbitrary"),
        name="moba_prompt",
    )(page_table.reshape(-1), q, k, v, cache_k)


PAGES_IN_FLIGHT = 8


def _gate_step_kernel(q_ref, ksum_ref, sel_ref, *, n_heads):
    bd = q_ref.shape[0]
    nblk = ksum_ref.shape[0] // bd
    sel_ref[...] = jnp.zeros_like(sel_ref)
    for b in range(bd):
        kmean = ksum_ref[b * nblk:(b + 1) * nblk] * (1.0 / MOBA_BLOCK)
        gate = jnp.sum(kmean * q_ref[b][None], axis=-1)
        blk = lax.broadcasted_iota(jnp.int32, gate.shape, 0)
        for r in range(MOBA_TOPK):
            m = jnp.max(gate, axis=0, keepdims=True)
            idx = jnp.min(jnp.where(gate == m, blk, nblk), axis=0, keepdims=True)
            sel_ref[b, r:r + 1, 0:n_heads] = idx
            gate = jnp.where(blk == idx, -jnp.inf, gate)


SEL_ROWS = 8


def gate_step(block_sums, q):
    bd, n_heads, _ = q.shape
    assert MOBA_TOPK <= SEL_ROWS and n_heads <= LANES
    sel = pl.pallas_call(
        functools.partial(_gate_step_kernel, n_heads=n_heads),
        out_shape=jax.ShapeDtypeStruct((bd, SEL_ROWS, LANES), jnp.int32),
        compiler_params=pltpu.CompilerParams(vmem_limit_bytes=VMEM_LIMIT),
        name="gate_step",
    )(q, block_sums)
    return sel[:, :MOBA_TOPK, :n_heads]


def _attn_step_kernel(pt_ref, sel_ref, q_ref, kn_ref, vn_ref, k_hbm, v_hbm, o_ref, kbuf, vbuf, ksem, vsem,
                      m_ref, l_ref, acc_ref, *, layer, n_pages, q_pos, pages_per_block):
    bd, n_heads, _ = q_ref.shape
    per_head = MOBA_TOPK * pages_per_block
    total = bd * n_heads * per_head
    depth = kbuf.shape[0]
    scale = HEAD_DIM ** -0.5
    own = q_pos // MOBA_BLOCK

    def coords(t):
        bh, j = t // per_head, t % per_head
        b, h = bh // n_heads, bh % n_heads
        return b, h, j, sel_ref[(b * MOBA_TOPK + j // pages_per_block) * n_heads + h]

    def page_copies(t, slot):
        b, _, j, n = coords(t)
        page = pt_ref[b * n_pages + n * pages_per_block + j % pages_per_block]
        return (pltpu.make_async_copy(k_hbm.at[layer, page], kbuf.at[slot], ksem.at[slot]),
                pltpu.make_async_copy(v_hbm.at[layer, page], vbuf.at[slot], vsem.at[slot]))

    for t in range(min(depth, total)):
        for c in page_copies(t, t):
            c.start()

    @pl.loop(0, total)
    def _(t):
        slot = t % depth
        b, h, j, n = coords(t)
        for c in page_copies(t, slot):
            c.wait()
        q = q_ref[b, pl.ds(h, 1), :]

        @pl.when(j == 0)
        def _():
            m_ref[...] = jnp.sum(q * kn_ref[b, pl.ds(h, 1), :], axis=1, keepdims=True) * scale
            l_ref[...] = jnp.ones_like(l_ref)
            acc_ref[...] = vn_ref[b, pl.ds(h, 1), :]

        k = kbuf[slot, pl.ds(h, PAGE_SIZE, stride=n_heads), :]
        v = vbuf[slot, pl.ds(h, PAGE_SIZE, stride=n_heads), :]
        slope = jnp.exp2(jnp.asarray(h + 1, F32) * (-8.0 / n_heads) + jnp.zeros((1, 1), F32))
        kpos = (n * MOBA_BLOCK + (j % pages_per_block) * PAGE_SIZE
                + lax.broadcasted_iota(jnp.int32, (PAGE_SIZE, 1), 0))
        dist = (q_pos - kpos).astype(F32)
        s = jnp.sum(k * q, axis=1, keepdims=True) * scale - slope * dist
        s = jnp.where((dist >= 0.0) & (n < own), s, MASKED)
        m = m_ref[...]
        m_new = jnp.maximum(m, jnp.max(s, axis=0, keepdims=True))
        alpha = jnp.exp(m - m_new)
        p = jnp.exp(s - m_new)
        l_ref[...] = alpha * l_ref[...] + jnp.sum(p, axis=0, keepdims=True)
        acc_ref[...] = alpha * acc_ref[...] + jnp.sum(p * v, axis=0, keepdims=True)
        m_ref[...] = m_new

        @pl.when(j == per_head - 1)
        def _():
            o_ref[b, pl.ds(h, 1), :] = acc_ref[...] / l_ref[...]

        @pl.when(t + depth < total)
        def _():
            for c in page_copies(t + depth, slot):
                c.start()


def attn_step(cache_k, cache_v, layer, page_table, sel, q, k_new, v_new):
    bd, n_pages = page_table.shape
    n_heads = q.shape[1]
    pages_per_block = MOBA_BLOCK // PAGE_SIZE
    assert n_pages % pages_per_block == 0
    rows = PAGE_SIZE * n_heads
    as_rows = lambda c: c.reshape(c.shape[0], c.shape[1], rows, HEAD_DIM)
    vmem, smem = pl.BlockSpec(memory_space=pltpu.VMEM), pl.BlockSpec(memory_space=pltpu.SMEM)
    hbm = pl.BlockSpec(memory_space=pl.ANY)
    return pl.pallas_call(
        functools.partial(_attn_step_kernel, layer=layer, n_pages=n_pages, q_pos=n_pages * PAGE_SIZE,
                          pages_per_block=pages_per_block),
        in_specs=[smem, smem, vmem, vmem, vmem, hbm, hbm],
        out_specs=vmem,
        out_shape=jax.ShapeDtypeStruct((bd, n_heads, HEAD_DIM), F32),
        scratch_shapes=[pltpu.VMEM((PAGES_IN_FLIGHT, rows, HEAD_DIM), F32),
                        pltpu.VMEM((PAGES_IN_FLIGHT, rows, HEAD_DIM), F32),
                        pltpu.SemaphoreType.DMA((PAGES_IN_FLIGHT,)), pltpu.SemaphoreType.DMA((PAGES_IN_FLIGHT,)),
                        pltpu.VMEM((1, 1), F32), pltpu.VMEM((1, 1), F32), pltpu.VMEM((1, HEAD_DIM), F32)],
        compiler_params=pltpu.CompilerParams(vmem_limit_bytes=VMEM_LIMIT),
        name="attn_step",
    )(page_table.reshape(-1), sel.reshape(-1), q, k_new, v_new, as_rows(cache_k), as_rows(cache_v))


P_HALO = 16


def _pool_out_kernel(o_ref, u_ref, x_ref, pw_ref, ps_ref, wo_ref, y_ref, newp_ref, uext, mix):
    l = pl.program_id(1)
    tl = u_ref.shape[1]
    wp = u_ref.shape[2]
    wo_attn = o_ref.shape[2]
    pg = wp // len(POOL_WINDOWS)

    @pl.when(l == 0)
    def _():
        uext[0:P_HALO, :] = jnp.zeros((P_HALO, wp), F32)

    @pl.when(l > 0)
    def _():
        uext[0:P_HALO, :] = uext[tl:tl + P_HALO, :]

    uext[P_HALO:P_HALO + tl, :] = u_ref[0]
    o_bf16 = o_ref[0].astype(BF16)
    n_chunk = y_ref.shape[2] // len(POOL_WINDOWS)
    pos = l * tl + lax.broadcasted_iota(jnp.int32, (tl, pg), 0)
    for g, w in enumerate(POOL_WINDOWS):
        n0 = g * n_chunk
        y_ref[0, :, n0:n0 + n_chunk] = x_ref[0, :, n0:n0 + n_chunk] + jnp.dot(
            o_bf16, wo_ref[0:wo_attn, n0:n0 + n_chunk], preferred_element_type=F32)
        c0 = g * pg
        wsum = uext[P_HALO:P_HALO + tl, c0:c0 + pg]
        for j in range(1, w):
            wsum = wsum + uext[P_HALO - j:P_HALO - j + tl, c0:c0 + pg]
        count = jnp.minimum(pos + 1, w).astype(F32)
        dlt = (wsum / count - uext[P_HALO:P_HALO + tl, c0:c0 + pg]).astype(BF16)
        yp = jnp.dot(dlt, pw_ref[g], preferred_element_type=F32) * ps_ref[:, c0:c0 + pg]
        mix[:, c0:c0 + pg] = yp.astype(BF16)

    y_ref[0] += jnp.dot(mix[...], wo_ref[wo_attn:, :], preferred_element_type=F32)

    @pl.when(l == pl.num_programs(1) - 1)
    def _():
        newp_ref[0] = uext[P_HALO + tl - (POOL_MAX - 1):P_HALO + tl, :]


def pool_out(o, u, x, pool_w, pool_scale, w_out, tl):
    bsz, seq, d = x.shape
    wp = u.shape[-1]
    tl = min(tl, seq)
    assert seq % tl == 0 and tl >= P_HALO
    const = _resident
    return pl.pallas_call(
        _pool_out_kernel,
        grid=(bsz, seq // tl),
        in_specs=[pl.BlockSpec((1, tl, o.shape[-1]), lambda b, l: (b, l, 0)),
                  pl.BlockSpec((1, tl, wp), lambda b, l: (b, l, 0)),
                  pl.BlockSpec((1, tl, d), lambda b, l: (b, l, 0)),
                  const(pool_w.shape), const((1, wp)), const(w_out.shape)],
        out_specs=[pl.BlockSpec((1, tl, d), lambda b, l: (b, l, 0)),
                   pl.BlockSpec((1, POOL_MAX - 1, wp), lambda b, l: (b, 0, 0))],
        out_shape=[jax.ShapeDtypeStruct((bsz, seq, d), F32),
                   jax.ShapeDtypeStruct((bsz, POOL_MAX - 1, wp), F32)],
        scratch_shapes=[pltpu.VMEM((P_HALO + tl, wp), F32), pltpu.VMEM((tl, wp), BF16)],
        compiler_params=_params("parallel", "arbitrary"),
        name="pool_out",
    )(o, u, x, pool_w, pool_scale.reshape(1, wp), w_out)


def _pool_step_kernel(o_ref, u_ref, x_ref, stp_ref, pw_ref, ps_ref, wo_ref, y_ref, *, start_pos):
    wp = u_ref.shape[1]
    pg = wp // len(POOL_WINDOWS)
    u = u_ref[...]
    parts = [o_ref[...].astype(BF16)]
    for g, w in enumerate(POOL_WINDOWS):
        c0 = g * pg
        wsum = u[:, c0:c0 + pg]
        for j in range(1, w):
            wsum = wsum + stp_ref[POOL_MAX - 1 - j][:, c0:c0 + pg]
        count = float(min(start_pos + 1, w))
        dlt = (wsum / count - u[:, c0:c0 + pg]).astype(BF16)
        yp = jnp.dot(dlt, pw_ref[g], preferred_element_type=F32) * ps_ref[:, c0:c0 + pg]
        parts.append(yp.astype(BF16))
    mix = jnp.concatenate(parts, axis=-1)
    y_ref[...] = x_ref[...] + jnp.dot(mix, wo_ref[...], preferred_element_type=F32)


def pool_step(o, u, x, st_p, pool_w, pool_scale, w_out, start_pos):
    bd, d = x.shape
    return pl.pallas_call(
        functools.partial(_pool_step_kernel, start_pos=start_pos),
        out_shape=jax.ShapeDtypeStruct((bd, d), F32),
        compiler_params=pltpu.CompilerParams(vmem_limit_bytes=VMEM_LIMIT),
        name="pool_step",
    )(o, u, x, st_p, pool_w, pool_scale.reshape(1, -1), w_out)


TM_PROMPT = 512
TM_FFN = 1024
TF = 512
TL_MIX = 256


def kernel(x_prompt, x_sample, state_conv_a, state_conv_b, cache_k, cache_v, state_pool, page_table, conv_norm_g, conv_w_in, conv_a_dw, conv_a_dw_b, conv_a_ln_g, conv_a_ln_b, conv_b_dw, conv_w_out, attn_norm_g, attn_w_in, pool_w, pool_scale, attn_w_out, ffn_norm_g, ffn_w_gate, ffn_w_up, ffn_w_down, final_norm_g):
    bsz, seq, d = x_prompt.shape
    bd, dec_seq, _ = x_sample.shape
    assert dec_seq == 1
    depth = ffn_norm_g.shape[0]
    n_heads = cache_k.shape[3]
    wattn = n_heads * HEAD_DIM
    past_len = page_table.shape[1] * PAGE_SIZE

    layer_bf16 = lambda w, n: w[n].astype(BF16)

    xp = x_prompt.reshape(bsz * seq, d)
    xs = x_sample.reshape(bd, d)
    pa, pb, pk, pv, pp = [], [], [], [], []
    sa, sb, sk, sv, sp = [], [], [], [], []
    for layer in range(depth):
        i = layer // 2
        if layer % 2 == 0:
            w_in, w_out = layer_bf16(conv_w_in, i), layer_bf16(conv_w_out, i)
            parts = norm_matmul(xp, conv_norm_g[i], w_in, TM_PROMPT)
            wa = parts[0].shape[-1]
            y, na, nb_ = conv_mix([t.reshape(bsz, seq, wa) for t in parts], xp.reshape(bsz, seq, d),
                                  conv_a_dw[i], conv_a_dw_b[i], conv_a_ln_g[i], conv_a_ln_b[i], conv_b_dw[i],
                                  w_out, TL_MIX)
            xp = y.reshape(bsz * seq, d)
            pa.append(na)
            pb.append(nb_)

            parts = norm_matmul(xs, conv_norm_g[i], w_in, TM_PROMPT)
            st_a = jnp.swapaxes(state_conv_a[i], 0, 1)
            st_b = jnp.swapaxes(state_conv_b[i], 0, 1)
            xs, a_new, c_new = conv_step(parts, xs, st_a, st_b, conv_a_dw[i], conv_a_dw_b[i], conv_a_ln_g[i],
                                         conv_a_ln_b[i], conv_b_dw[i], w_out)
            sa.append(jnp.concatenate([state_conv_a[i][:, 1:], a_new[:, None]], axis=1))
            sb.append(jnp.concatenate([state_conv_b[i][:, 1:], c_new[:, None]], axis=1))
        else:
            w_in, w_out, w_pool = layer_bf16(attn_w_in, i), layer_bf16(attn_w_out, i), layer_bf16(pool_w, i)
            q, k, v, u = norm_matmul(xp, attn_norm_g[i], w_in, TM_PROMPT, head_major=3)
            shp = (n_heads, bsz, seq, HEAD_DIM)
            o, block_sums = moba_prompt(q.reshape(shp), k.reshape(shp), v.reshape(shp), cache_k, i, page_table)
            y, npool = pool_out(o, u.reshape(bsz, seq, -1), xp.reshape(bsz, seq, d), w_pool, pool_scale[i],
                                w_out, TL_MIX)
            xp = y.reshape(bsz * seq, d)
            pk.append(jnp.transpose(k.reshape(shp), (1, 2, 0, 3)))
            pv.append(jnp.transpose(v.reshape(shp), (1, 2, 0, 3)))
            pp.append(npool)

            q, k, v, u = norm_matmul(xs, attn_norm_g[i], w_in, TM_PROMPT, head_major=3)
            heads = lambda t: jnp.swapaxes(t, 0, 1)
            sel = gate_step(block_sums, heads(q))
            o = attn_step(cache_k, cache_v, i, page_table, sel, heads(q), heads(k), heads(v))
            st_p = jnp.swapaxes(state_pool[i], 0, 1)
            xs = pool_step(o.reshape(bd, wattn), u, xs, st_p, w_pool, pool_scale[i], w_out, past_len)
            sk.append(heads(k).reshape(bd, 1, n_heads, HEAD_DIM))
            sv.append(heads(v).reshape(bd, 1, n_heads, HEAD_DIM))
            sp.append(jnp.concatenate([state_pool[i][:, 1:], u[:, None]], axis=1))
        last = layer == depth - 1
        ffn_w = (ffn_w_gate, ffn_w_up, ffn_w_down, layer)
        xp, xs = ffn(xp, xs, ffn_norm_g[layer], *ffn_w, final_norm_g, TM_FFN, TF, last)
    return (xp.reshape(bsz, seq, d), xs.reshape(bd, 1, d), jnp.stack(pa), jnp.stack(sa), jnp.stack(pb),
            jnp.stack(sb), jnp.stack(pk), jnp.stack(sk), jnp.stack(pv), jnp.stack(sv), jnp.stack(pp), jnp.stack(sp))
```

```python
import functools

import jax
import jax.numpy as jnp
from jax import lax
from jax.experimental import pallas as pl
from jax.experimental.pallas import tpu as pltpu

EPS = 1e-6
PAGE_SIZE = 128
MOBA_BLOCK = 256
MOBA_TOPK = 3
HEAD_DIM = 128
POOL_WINDOWS = (2, 4, 8, 16)
POOL_MAX = max(POOL_WINDOWS)
MASKED = -1e30

V7X_VMEM_BYTES = 64 * 1024 * 1024
VMEM_LIMIT = V7X_VMEM_BYTES - 4 * 1024 * 1024
LANES = 128
SUBLANES = 8
COL_GROUP = 1024

BF16 = jnp.bfloat16
F32 = jnp.float32


def _params(*sem):
    return pltpu.CompilerParams(dimension_semantics=sem, vmem_limit_bytes=VMEM_LIMIT)


def _rms(x, g):
    ms = jnp.mean(x * x, axis=-1, keepdims=True)
    return x * lax.rsqrt(ms + EPS) * g


def _sigmoid(x):
    return 1.0 / (1.0 + jnp.exp(-x))


def _resident(shape):
    return pl.BlockSpec(shape, lambda *_: (0,) * len(shape), pipeline_mode=pl.Buffered(1))


def _norm_matmul_kernel(x_ref, g_ref, w_ref, *out_refs, head_major):
    h = _rms(x_ref[...], g_ref[...]).astype(BF16)
    for c, o_ref in enumerate(out_refs):
        r = jnp.dot(h, w_ref[:, c * COL_GROUP:(c + 1) * COL_GROUP], preferred_element_type=F32)
        if c < head_major:
            for hh in range(COL_GROUP // HEAD_DIM):
                o_ref[hh] = r[:, hh * HEAD_DIM:(hh + 1) * HEAD_DIM]
        else:
            o_ref[...] = r


def norm_matmul(x, g, w, tm, head_major=0):
    m, d = x.shape
    n_out = w.shape[1] // COL_GROUP
    tm = min(tm, m)
    heads = COL_GROUP // HEAD_DIM
    flat_spec = pl.BlockSpec((tm, COL_GROUP), lambda i: (i, 0))
    head_spec = pl.BlockSpec((heads, tm, HEAD_DIM), lambda i: (0, i, 0))
    flat_shape = jax.ShapeDtypeStruct((m, COL_GROUP), F32)
    head_shape = jax.ShapeDtypeStruct((heads, m, HEAD_DIM), F32)
    return pl.pallas_call(
        functools.partial(_norm_matmul_kernel, head_major=head_major),
        grid=(m // tm,),
        in_specs=[pl.BlockSpec((tm, d), lambda i: (i, 0)), _resident((1, d)), _resident(w.shape)],
        out_specs=[head_spec] * head_major + [flat_spec] * (n_out - head_major),
        out_shape=[head_shape] * head_major + [flat_shape] * (n_out - head_major),
        compiler_params=_params("parallel"),
        name="norm_matmul",
    )(x, g.reshape(1, d), w)


def _ffn_kernel(g_ref, fg_ref, xs_ref, wg_ref, wu_ref, wd_ref, x_hbm, o_ref, os_ref, xbuf, h_ref, sem, *, final_norm):
    i, f = pl.program_id(0), pl.program_id(1)
    tm = o_ref.shape[0]
    ns = xs_ref.shape[0]
    last_f = pl.num_programs(1) - 1

    def x_copy(tile):
        return pltpu.make_async_copy(x_hbm.at[pl.ds(pl.multiple_of(tile * tm, tm), tm), :], xbuf, sem)

    @pl.when((i == 0) & (f == 0))
    def _():
        x_copy(0).start()
        xs = xs_ref[...]
        h_ref[tm:, :] = jnp.zeros((h_ref.shape[0] - tm, h_ref.shape[1]), h_ref.dtype)
        h_ref[tm:tm + ns, :] = _rms(xs, g_ref[...]).astype(h_ref.dtype)
        os_ref[...] = xs

    @pl.when(f == 0)
    def _():
        x_copy(i).wait()
        x = xbuf[...]
        h_ref[0:tm, :] = _rms(x, g_ref[...]).astype(h_ref.dtype)
        o_ref[...] = x

    @pl.when((f == 1) & (i + 1 < pl.num_programs(0)))
    def _():
        x_copy(i + 1).start()

    h = h_ref[...]
    gate = jnp.dot(h, wg_ref[...].astype(BF16), preferred_element_type=F32)
    up = jnp.dot(h, wu_ref[...].astype(BF16), preferred_element_type=F32)
    act = (gate * _sigmoid(gate) * up).astype(BF16)
    down = jnp.dot(act, wd_ref[...].astype(BF16), preferred_element_type=F32)
    o_ref[...] += down[0:tm]

    @pl.when(i == 0)
    def _():
        os_ref[...] += down[tm:tm + ns]

    if final_norm:
        @pl.when(f == last_f)
        def _():
            o_ref[...] = _rms(o_ref[...], fg_ref[...])

        @pl.when((i == 0) & (f == last_f))
        def _():
            os_ref[...] = _rms(os_ref[...], fg_ref[...])


BF16_ROWS = 16


def ffn(x, xs, g, wg, wu, wd, layer, final_g, tm, tf, final_norm):
    m, d = x.shape
    ns = xs.shape[0]
    dff = wg.shape[2]
    tm = min(tm, m)
    assert m % tm == 0 and dff % tf == 0 and dff // tf >= 2 and ns <= BF16_ROWS and tm % BF16_ROWS == 0
    return pl.pallas_call(
        functools.partial(_ffn_kernel, final_norm=final_norm),
        grid=(m // tm, dff // tf),
        in_specs=[_resident((1, d)), _resident((1, d)), _resident((ns, d)),
                  pl.BlockSpec((None, d, tf), lambda i, f: (layer, 0, f)),
                  pl.BlockSpec((None, d, tf), lambda i, f: (layer, 0, f)),
                  pl.BlockSpec((None, tf, d), lambda i, f: (layer, f, 0)),
                  pl.BlockSpec(memory_space=pl.ANY)],
        out_specs=[pl.BlockSpec((tm, d), lambda i, f: (i, 0)), pl.BlockSpec((ns, d), lambda i, f: (0, 0))],
        out_shape=[jax.ShapeDtypeStruct((m, d), F32), jax.ShapeDtypeStruct((ns, d), F32)],
        scratch_shapes=[pltpu.VMEM((tm, d), F32), pltpu.VMEM((tm + BF16_ROWS, d), BF16),
                        pltpu.SemaphoreType.DMA(())],
        compiler_params=_params("arbitrary", "arbitrary"),
        name="ffn",
    )(g.reshape(1, d), final_g.reshape(1, d), xs, wg, wu, wd, x)


A_HALO = 32
B_HALO = 8


def _layer_norm_silu(x, g, b):
    mu = jnp.mean(x, axis=-1, keepdims=True)
    xc = x - mu
    y = xc * lax.rsqrt(jnp.mean(xc * xc, axis=-1, keepdims=True) + EPS) * g + b
    return y * _sigmoid(y)


def _conv_mix_kernel(av_ref, ag_ref, bh_ref, bb_ref, bc_ref, x_ref, adw_ref, adwb_ref, lng_ref, lnb_ref,
                     bdw_ref, wo_ref, y_ref, newa_ref, newb_ref, aext, cext, ashift, aconv, *, ta, tb, rc):
    l = pl.program_id(1)
    tl = av_ref.shape[1]
    wa = av_ref.shape[2]

    @pl.when(l == 0)
    def _():
        aext[0:A_HALO, :] = jnp.zeros((A_HALO, wa), F32)
        cext[0:B_HALO, :] = jnp.zeros((B_HALO, wa), F32)

    @pl.when(l > 0)
    def _():
        aext[0:A_HALO, :] = aext[tl:tl + A_HALO, :]
        cext[0:B_HALO, :] = cext[tl:tl + B_HALO, :]

    aext[A_HALO:A_HALO + tl, :] = av_ref[0] * _sigmoid(ag_ref[0])
    cext[B_HALO:B_HALO + tl, :] = bc_ref[0] * bh_ref[0]

    bconv = jnp.zeros((tl, wa), F32)
    for k in range(tb):
        off = B_HALO - (tb - 1) + k
        bconv = bconv + bdw_ref[k:k + 1, :] * cext[off:off + tl, :]
    b_out = (bb_ref[0] * bconv).astype(BF16)
    y_ref[0] = x_ref[0] + jnp.dot(b_out, wo_ref[wa:, :], preferred_element_type=F32)

    n_shift = ashift.shape[1]
    for s in range(1, SUBLANES):
        ashift[s - 1] = aext[s:s + n_shift, :]

    for r0 in range(0, tl, rc):
        for c0 in range(0, wa, LANES):
            acc = jnp.broadcast_to(adwb_ref[:, c0:c0 + LANES], (rc, LANES))
            for k in range(ta):
                off = A_HALO - (ta - 1) + r0 + k
                s = off % SUBLANES
                src = aext if s == 0 else ashift.at[s - 1]
                acc = acc + adw_ref[k:k + 1, c0:c0 + LANES] * src[off - s:off - s + rc, c0:c0 + LANES]
            aconv[r0:r0 + rc, c0:c0 + LANES] = acc
    a_out = _layer_norm_silu(aconv[...], lng_ref[...], lnb_ref[...]).astype(BF16)
    y_ref[0] += jnp.dot(a_out, wo_ref[0:wa, :], preferred_element_type=F32)

    @pl.when(l == pl.num_programs(1) - 1)
    def _():
        newa_ref[0] = aext[A_HALO + tl - (ta - 1):A_HALO + tl, :]
        newb_ref[0] = cext[B_HALO + tl - (tb - 1):B_HALO + tl, :]


def conv_mix(parts, x, a_dw, a_dw_b, ln_g, ln_b, b_dw, w_out, tl):
    bsz, seq, d = x.shape
    wa = parts[0].shape[-1]
    ta, tb = a_dw.shape[0], b_dw.shape[0]
    tl = min(tl, seq)
    assert seq % tl == 0 and tl >= A_HALO and ta - 1 <= A_HALO and tb - 1 <= B_HALO
    part_spec = pl.BlockSpec((1, tl, wa), lambda b, l: (b, l, 0))
    const = _resident
    return pl.pallas_call(
        functools.partial(_conv_mix_kernel, ta=ta, tb=tb, rc=min(128, tl)),
        grid=(bsz, seq // tl),
        in_specs=[part_spec] * 5 + [pl.BlockSpec((1, tl, d), lambda b, l: (b, l, 0)),
                                    const((ta, wa)), const((1, wa)), const((1, wa)), const((1, wa)),
                                    const((tb, wa)), const(w_out.shape)],
        out_specs=[pl.BlockSpec((1, tl, d), lambda b, l: (b, l, 0)),
                   pl.BlockSpec((1, ta - 1, wa), lambda b, l: (b, 0, 0)),
                   pl.BlockSpec((1, tb - 1, wa), lambda b, l: (b, 0, 0))],
        out_shape=[jax.ShapeDtypeStruct((bsz, seq, d), F32),
                   jax.ShapeDtypeStruct((bsz, ta - 1, wa), F32),
                   jax.ShapeDtypeStruct((bsz, tb - 1, wa), F32)],
        scratch_shapes=[pltpu.VMEM((A_HALO + tl, wa), F32), pltpu.VMEM((B_HALO + tl, wa), F32),
                        pltpu.VMEM((SUBLANES - 1, A_HALO + tl - SUBLANES, wa), F32),
                        pltpu.VMEM((tl, wa), F32)],
        compiler_params=_params("parallel", "arbitrary"),
        name="conv_mix",
    )(*parts, x, a_dw, a_dw_b.reshape(1, wa), ln_g.reshape(1, wa), ln_b.reshape(1, wa), b_dw, w_out)


def _conv_step_kernel(av_ref, ag_ref, bh_ref, bb_ref, bc_ref, x_ref, sta_ref, stb_ref, adw_ref, adwb_ref,
                      lng_ref, lnb_ref, bdw_ref, wo_ref, y_ref, anew_ref, cnew_ref, *, ta, tb):
    a = av_ref[...] * _sigmoid(ag_ref[...])
    c = bc_ref[...] * bh_ref[...]
    anew_ref[...] = a
    cnew_ref[...] = c
    acc = adwb_ref[...] + adw_ref[ta - 1:ta, :] * a
    for k in range(ta - 1):
        acc = acc + adw_ref[k:k + 1, :] * sta_ref[k]
    a_out = _layer_norm_silu(acc, lng_ref[...], lnb_ref[...])
    bconv = bdw_ref[tb - 1:tb, :] * c
    for k in range(tb - 1):
        bconv = bconv + bdw_ref[k:k + 1, :] * stb_ref[k]
    mix = jnp.concatenate([a_out, bb_ref[...] * bconv], axis=-1).astype(BF16)
    y_ref[...] = x_ref[...] + jnp.dot(mix, wo_ref[...], preferred_element_type=F32)


def conv_step(parts, x, st_a, st_b, a_dw, a_dw_b, ln_g, ln_b, b_dw, w_out):
    bd, d = x.shape
    wa = parts[0].shape[-1]
    ta, tb = a_dw.shape[0], b_dw.shape[0]
    return pl.pallas_call(
        functools.partial(_conv_step_kernel, ta=ta, tb=tb),
        out_shape=[jax.ShapeDtypeStruct((bd, d), F32), jax.ShapeDtypeStruct((bd, wa), F32),
                   jax.ShapeDtypeStruct((bd, wa), F32)],
        compiler_params=pltpu.CompilerParams(vmem_limit_bytes=VMEM_LIMIT),
        name="conv_step",
    )(*parts, x, st_a, st_b, a_dw, a_dw_b.reshape(1, wa), ln_g.reshape(1, wa), ln_b.reshape(1, wa), b_dw, w_out)


LOG2E = 1.4426950408889634
TQ = 2 * MOBA_BLOCK
DA = 2 * HEAD_DIM


def _split3(x):
    def top_bits(v):
        return lax.bitcast_convert_type(lax.bitcast_convert_type(v, jnp.uint32) & jnp.uint32(0xFFFF0000), F32)

    hi = top_bits(x)
    mid = top_bits(x - hi)
    return hi, mid, x - hi - mid


def _moba_setup(q_ref, k_ref, v_ref, qaug, kaug, vaug, kmean, slope2, nb):
    seq = k_ref.shape[1]
    bs = MOBA_BLOCK
    nbp = kmean.shape[0]
    k = k_ref[0]
    q = q_ref[0]
    kmean[...] = jnp.zeros_like(kmean)
    kmean[0:nb, :] = jnp.mean(k.reshape(nb, bs, HEAD_DIM), axis=1)

    gate = lax.dot_general(kmean[...], q, (((1,), (1,)), ((), ())), precision=lax.Precision.HIGHEST,
                           preferred_element_type=F32)
    blk = lax.broadcasted_iota(jnp.int32, (nbp, seq), 0)
    own = lax.broadcasted_iota(jnp.int32, (nbp, seq), 1) // bs
    gate = jnp.where(blk < own, gate, -jnp.inf)
    attend = blk == own
    for _ in range(MOBA_TOPK):
        m = jnp.max(gate, axis=0, keepdims=True)
        idx = jnp.min(jnp.where(gate == m, blk, nbp), axis=0, keepdims=True)
        hit = blk == idx
        attend = attend | (hit & (blk < own))
        gate = jnp.where(hit, -jnp.inf, gate)
    attend_t = jnp.where(attend, 1.0, 0.0)
    attend_r = jnp.concatenate([attend_t, jnp.zeros((LANES - nbp, seq), F32)], axis=0).T

    lane = lax.broadcasted_iota(jnp.int32, (seq, LANES), 1)
    pos_i = lax.broadcasted_iota(jnp.int32, (seq, LANES), 0)
    pos = pos_i.astype(F32)

    khi, kmid, klo = _split3(slope2 * pos)
    ek = jnp.where(lane == pos_i // bs, 1.0, 0.0)
    ek = jnp.where(lane == nb, khi, ek)
    ek = jnp.where(lane == nb + 1, kmid, ek)
    ek = jnp.where(lane == nb + 2, klo, ek)
    ek = jnp.where((lane >= nb + 3) & (lane < nb + 6), 1.0, ek)
    kaug[:, 0:HEAD_DIM] = k.astype(BF16)
    kaug[:, HEAD_DIM:] = ek.astype(BF16)

    qhi, qmid, qlo = _split3(-slope2 * pos)
    eq = jnp.where(lane < nb, jnp.where(attend_r > 0.5, 0.0, MASKED), 0.0)
    eq = jnp.where((lane >= nb) & (lane < nb + 3), 1.0, eq)
    eq = jnp.where(lane == nb + 3, qhi, eq)
    eq = jnp.where(lane == nb + 4, qmid, eq)
    eq = jnp.where(lane == nb + 5, qlo, eq)
    qaug[:, 0:HEAD_DIM] = (q * (HEAD_DIM ** -0.5 * LOG2E)).astype(BF16)
    qaug[:, HEAD_DIM:] = eq.astype(BF16)

    vaug[:, 0:HEAD_DIM] = v_ref[0].astype(BF16)
    vaug[:, HEAD_DIM:] = jnp.ones((seq, HEAD_DIM), BF16)


SUM_CHAINS = 16
PAGE_LOOKAHEAD = 3


class _PageSums:
    def __init__(self, pt_ref, kc_hbm, ksum_ref, ring, sem, *, layer, pages_per_block, n_steps):
        self.pt_ref, self.kc_hbm, self.ksum_ref, self.ring, self.sem = pt_ref, kc_hbm, ksum_ref, ring, sem
        self.layer, self.pages_per_block, self.n_steps = layer, pages_per_block, n_steps
        self.total = pt_ref.shape[0]
        self.depth = ring.shape[0]
        self.per = self.depth // PAGE_LOOKAHEAD
        self.even = n_steps * self.per == self.total

    def _copy(self, t, slot):
        return pltpu.make_async_copy(self.kc_hbm.at[self.layer, self.pt_ref[t]], self.ring.at[slot],
                                     self.sem.at[slot])

    def begin(self):
        self.ksum_ref[...] = jnp.zeros_like(self.ksum_ref)
        for slot in range(min(self.depth, self.total)):
            self._copy(slot, slot).start()

    def _fold_page(self, t, slot):
        _, page_rows, n_heads, _ = self.ring.shape
        self._copy(t, slot).wait()
        page = self.ring[slot].reshape(SUM_CHAINS, page_rows // SUM_CHAINS, n_heads, HEAD_DIM)
        self.ksum_ref[t // self.pages_per_block] += jnp.sum(jnp.sum(page, axis=1), axis=0)

    def _first_slot(self, step):
        return (step % PAGE_LOOKAHEAD) * self.per

    def fold(self, step):
        assert self.even
        for j in range(self.per):
            self._fold_page(step * self.per + j, self._first_slot(step) + j)

    def refill(self, step):
        assert self.even

        @pl.when(step + PAGE_LOOKAHEAD < self.n_steps)
        def _():
            for j in range(self.per):
                self._copy(step * self.per + j + self.depth, self._first_slot(step) + j).start()

    def step(self, step):
        for j in range(self.per):
            t = step * self.per + j

            @pl.when(t < self.total)
            def _(t=t, j=j):
                self._fold_page(t, self._first_slot(step) + j)

                @pl.when(t + self.depth < self.total)
                def _():
                    self._copy(t + self.depth, self._first_slot(step) + j).start()


def _moba_kernel(pt_ref, q_ref, k_ref, v_ref, kc_hbm, o_ref, ksum_ref, qaug, kaug, vaug, kmean, m_ref, acc2_ref,
                 s_ref, rel_ref, ring, psem, *, n_heads, nb, layer, page_pairs, page_steps):
    h = pl.program_id(1)
    bs = MOBA_BLOCK
    nt = (((1,), (1,)), ((), ()))
    seq = qaug.shape[0]
    n_tiles = seq // TQ
    n_units = n_tiles * (n_tiles + 1) // 2
    n_pairs = n_units // 2
    bh = pl.program_id(0) * pl.num_programs(1) + h

    slope2 = jnp.exp2((h + 1).astype(F32) * (-8.0 / n_heads) + jnp.zeros((1, 1), F32)) * LOG2E
    _moba_setup(q_ref, k_ref, v_ref, qaug, kaug, vaug, kmean, slope2, nb)
    rel_ref[...] = (lax.broadcasted_iota(jnp.int32, (bs, bs), 1) - lax.broadcasted_iota(jnp.int32, (bs, bs), 0))

    halves = (slice(0, bs), slice(bs, TQ))

    def tile_rows(t):
        return pl.ds(pl.multiple_of(t * TQ, TQ), TQ)

    def successor(unit):
        i, g = unit
        wrap = g == i
        return jnp.where(wrap, jnp.minimum(i + 1, n_tiles - 1), i), jnp.where(wrap, 0, g + 1)

    def scores(unit):
        i, g = unit
        qa = qaug[tile_rows(i), :]
        kg = kaug[tile_rows(g), :]
        return jnp.concatenate([lax.dot_general(qa[r], kg, nt, preferred_element_type=F32) for r in halves], axis=0)

    def process(unit, slot):
        i, g = unit
        vg = vaug[tile_rows(g), :]
        acc_ref = acc2_ref.at[i % 2]
        for r in halves:
            diag = jnp.where(rel_ref[...] <= (i - g) * TQ, s_ref[slot, r, r], MASKED)
            sr = jnp.concatenate([diag, s_ref[slot, r, bs:]] if r.start == 0 else [s_ref[slot, r, 0:bs], diag], axis=1)
            m = jnp.where(g == 0, MASKED, m_ref[r, :])
            m_new = jnp.maximum(m, jnp.max(sr, axis=1, keepdims=True))
            p = jnp.exp2(sr - m_new).astype(BF16)
            pv = jnp.dot(p, vg, preferred_element_type=F32)
            acc_ref[r, :] = jnp.exp2(m - m_new) * acc_ref[r, :] + pv
            m_ref[r, :] = m_new

    def finish(unit):
        i, g = unit

        @pl.when(g == i)
        def _():
            acc = acc2_ref[i % 2]
            o_ref[0, tile_rows(i), :] = acc[:, 0:HEAD_DIM] / acc[:, HEAD_DIM:]

    m_ref[...] = jnp.full(m_ref.shape, MASKED, F32)
    acc2_ref[...] = jnp.zeros_like(acc2_ref)
    first = (jnp.int32(0), jnp.int32(0))
    s_ref[0] = scores(first)

    pages = _PageSums(pt_ref, kc_hbm, ksum_ref, ring, psem, layer=layer, pages_per_block=bs // PAGE_SIZE,
                      n_steps=page_steps)

    @pl.when(bh == 0)
    def _():
        pages.begin()

    def pair(p, unit, with_pages):
        page_step = bh * page_pairs + p
        if with_pages and pages.even:
            pages.fold(page_step)
        elif with_pages:
            pages.step(page_step)
        nxt = successor(unit)
        s_ref[1] = scores(nxt)
        process(unit, 0)
        nxt2 = successor(nxt)
        s_ref[0] = scores(nxt2)
        process(nxt, 1)
        if with_pages and pages.even:
            pages.refill(page_step)
        finish(unit)
        finish(nxt)
        return nxt2

    unit = lax.fori_loop(0, page_pairs, functools.partial(pair, with_pages=True), first)
    if page_pairs < n_pairs:
        unit = lax.fori_loop(page_pairs, n_pairs, functools.partial(pair, with_pages=False), unit)
    if n_units % 2:
        process(unit, 0)
        finish(unit)


def moba_prompt(q, k, v, cache_k, layer, page_table):
    n_heads, bsz, seq, _ = q.shape
    width = n_heads * HEAD_DIM
    assert seq % TQ == 0
    nb = seq // MOBA_BLOCK
    nbp = -(-nb // 8) * 8
    assert nb + 6 <= LANES
    bd, n_pages = page_table.shape
    pages_per_block = MOBA_BLOCK // PAGE_SIZE
    assert n_pages % pages_per_block == 0
    n_tiles = seq // TQ
    n_pairs = n_tiles * (n_tiles + 1) // 4
    assert n_pairs >= 1
    total_pages = bd * n_pages
    even = [p for p in range(1, n_pairs + 1) if total_pages % (bsz * n_heads * p) == 0]
    page_pairs = max(even) if even else n_pairs
    page_steps = bsz * n_heads * page_pairs
    pages_per_step = -(-total_pages // page_steps)
    sums_shape = (total_pages // pages_per_block, n_heads, HEAD_DIM)
    slab = pl.BlockSpec((None, 1, seq, HEAD_DIM), lambda b, h: (h, b, 0, 0))
    return pl.pallas_call(
        functools.partial(_moba_kernel, n_heads=n_heads, nb=nb, layer=layer, page_pairs=page_pairs,
                          page_steps=page_steps),
        grid=(bsz, n_heads),
        in_specs=[pl.BlockSpec(memory_space=pltpu.SMEM), slab, slab, slab, pl.BlockSpec(memory_space=pl.ANY)],
        out_specs=[pl.BlockSpec((1, seq, HEAD_DIM), lambda b, h: (b, 0, h)),
                   pl.BlockSpec(sums_shape, lambda b, h: (0, 0, 0))],
        out_shape=[jax.ShapeDtypeStruct((bsz, seq, width), F32), jax.ShapeDtypeStruct(sums_shape, F32)],
        scratch_shapes=[pltpu.VMEM((seq, DA), BF16), pltpu.VMEM((seq, DA), BF16), pltpu.VMEM((seq, DA), BF16),
                        pltpu.VMEM((nbp, HEAD_DIM), F32), pltpu.VMEM((TQ, 1), F32), pltpu.VMEM((2, TQ, DA), F32),
                        pltpu.VMEM((2, TQ, TQ), F32), pltpu.VMEM((MOBA_BLOCK, MOBA_BLOCK), jnp.int32),
                        pltpu.VMEM((PAGE_LOOKAHEAD * pages_per_step, PAGE_SIZE, n_heads, HEAD_DIM), F32),
                        pltpu.SemaphoreType.DMA((PAGE_LOOKAHEAD * pages_per_step,))],
        compiler_params=_params("arbitrary", "arbitrary"),
        name="moba_prompt",
    )(page_table.reshape(-1), q, k, v, cache_k)


PAGES_IN_FLIGHT = 8


def _gate_step_kernel(q_ref, ksum_ref, sel_ref, *, n_heads):
    bd = q_ref.shape[0]
    nblk = ksum_ref.shape[0] // bd
    sel_ref[...] = jnp.zeros_like(sel_ref)
    for b in range(bd):
        kmean = ksum_ref[b * nblk:(b + 1) * nblk] * (1.0 / MOBA_BLOCK)
        gate = jnp.sum(kmean * q_ref[b][None], axis=-1)
        blk = lax.broadcasted_iota(jnp.int32, gate.shape, 0)
        for r in range(MOBA_TOPK):
            m = jnp.max(gate, axis=0, keepdims=True)
            idx = jnp.min(jnp.where(gate == m, blk, nblk), axis=0, keepdims=True)
            sel_ref[b, r:r + 1, 0:n_heads] = idx
            gate = jnp.where(blk == idx, -jnp.inf, gate)


SEL_ROWS = 8


def gate_step(block_sums, q):
    bd, n_heads, _ = q.shape
    assert MOBA_TOPK <= SEL_ROWS and n_heads <= LANES
    sel = pl.pallas_call(
        functools.partial(_gate_step_kernel, n_heads=n_heads),
        out_shape=jax.ShapeDtypeStruct((bd, SEL_ROWS, LANES), jnp.int32),
        compiler_params=pltpu.CompilerParams(vmem_limit_bytes=VMEM_LIMIT),
        name="gate_step",
    )(q, block_sums)
    return sel[:, :MOBA_TOPK, :n_heads]


def _attn_step_kernel(pt_ref, sel_ref, q_ref, kn_ref, vn_ref, k_hbm, v_hbm, o_ref, kbuf, vbuf, ksem, vsem,
                      m_ref, l_ref, acc_ref, *, layer, n_pages, q_pos, pages_per_block):
    bd, n_heads, _ = q_ref.shape
    per_head = MOBA_TOPK * pages_per_block
    total = bd * n_heads * per_head
    depth = kbuf.shape[0]
    scale = HEAD_DIM ** -0.5
    own = q_pos // MOBA_BLOCK

    def coords(t):
        bh, j = t // per_head, t % per_head
        b, h = bh // n_heads, bh % n_heads
        return b, h, j, sel_ref[(b * MOBA_TOPK + j // pages_per_block) * n_heads + h]

    def page_copies(t, slot):
        b, _, j, n = coords(t)
        page = pt_ref[b * n_pages + n * pages_per_block + j % pages_per_block]
        return (pltpu.make_async_copy(k_hbm.at[layer, page], kbuf.at[slot], ksem.at[slot]),
                pltpu.make_async_copy(v_hbm.at[layer, page], vbuf.at[slot], vsem.at[slot]))

    for t in range(min(depth, total)):
        for c in page_copies(t, t):
            c.start()

    @pl.loop(0, total)
    def _(t):
        slot = t % depth
        b, h, j, n = coords(t)
        for c in page_copies(t, slot):
            c.wait()
        q = q_ref[b, pl.ds(h, 1), :]

        @pl.when(j == 0)
        def _():
            m_ref[...] = jnp.sum(q * kn_ref[b, pl.ds(h, 1), :], axis=1, keepdims=True) * scale
            l_ref[...] = jnp.ones_like(l_ref)
            acc_ref[...] = vn_ref[b, pl.ds(h, 1), :]

        k = kbuf[slot, pl.ds(h, PAGE_SIZE, stride=n_heads), :]
        v = vbuf[slot, pl.ds(h, PAGE_SIZE, stride=n_heads), :]
        slope = jnp.exp2(jnp.asarray(h + 1, F32) * (-8.0 / n_heads) + jnp.zeros((1, 1), F32))
        kpos = (n * MOBA_BLOCK + (j % pages_per_block) * PAGE_SIZE
                + lax.broadcasted_iota(jnp.int32, (PAGE_SIZE, 1), 0))
        dist = (q_pos - kpos).astype(F32)
        s = jnp.sum(k * q, axis=1, keepdims=True) * scale - slope * dist
        s = jnp.where((dist >= 0.0) & (n < own), s, MASKED)
        m = m_ref[...]
        m_new = jnp.maximum(m, jnp.max(s, axis=0, keepdims=True))
        alpha = jnp.exp(m - m_new)
        p = jnp.exp(s - m_new)
        l_ref[...] = alpha * l_ref[...] + jnp.sum(p, axis=0, keepdims=True)
        acc_ref[...] = alpha * acc_ref[...] + jnp.sum(p * v, axis=0, keepdims=True)
        m_ref[...] = m_new

        @pl.when(j == per_head - 1)
        def _():
            o_ref[b, pl.ds(h, 1), :] = acc_ref[...] / l_ref[...]

        @pl.when(t + depth < total)
        def _():
            for c in page_copies(t + depth, slot):
                c.start()


def attn_step(cache_k, cache_v, layer, page_table, sel, q, k_new, v_new):
    bd, n_pages = page_table.shape
    n_heads = q.shape[1]
    pages_per_block = MOBA_BLOCK // PAGE_SIZE
    assert n_pages % pages_per_block == 0
    rows = PAGE_SIZE * n_heads
    as_rows = lambda c: c.reshape(c.shape[0], c.shape[1], rows, HEAD_DIM)
    vmem, smem = pl.BlockSpec(memory_space=pltpu.VMEM), pl.BlockSpec(memory_space=pltpu.SMEM)
    hbm = pl.BlockSpec(memory_space=pl.ANY)
    return pl.pallas_call(
        functools.partial(_attn_step_kernel, layer=layer, n_pages=n_pages, q_pos=n_pages * PAGE_SIZE,
                          pages_per_block=pages_per_block),
        in_specs=[smem, smem, vmem, vmem, vmem, hbm, hbm],
        out_specs=vmem,
        out_shape=jax.ShapeDtypeStruct((bd, n_heads, HEAD_DIM), F32),
        scratch_shapes=[pltpu.VMEM((PAGES_IN_FLIGHT, rows, HEAD_DIM), F32),
                        pltpu.VMEM((PAGES_IN_FLIGHT, rows, HEAD_DIM), F32),
                        pltpu.SemaphoreType.DMA((PAGES_IN_FLIGHT,)), pltpu.SemaphoreType.DMA((PAGES_IN_FLIGHT,)),
                        pltpu.VMEM((1, 1), F32), pltpu.VMEM((1, 1), F32), pltpu.VMEM((1, HEAD_DIM), F32)],
        compiler_params=pltpu.CompilerParams(vmem_limit_bytes=VMEM_LIMIT),
        name="attn_step",
    )(page_table.reshape(-1), sel.reshape(-1), q, k_new, v_new, as_rows(cache_k), as_rows(cache_v))


P_HALO = 16


def _pool_out_kernel(o_ref, u_ref, x_ref, pw_ref, ps_ref, wo_ref, y_ref, newp_ref, uext, mix):
    l = pl.program_id(1)
    tl = u_ref.shape[1]
    wp = u_ref.shape[2]
    wo_attn = o_ref.shape[2]
    pg = wp // len(POOL_WINDOWS)

    @pl.when(l == 0)
    def _():
        uext[0:P_HALO, :] = jnp.zeros((P_HALO, wp), F32)

    @pl.when(l > 0)
    def _():
        uext[0:P_HALO, :] = uext[tl:tl + P_HALO, :]

    uext[P_HALO:P_HALO + tl, :] = u_ref[0]
    o_bf16 = o_ref[0].astype(BF16)
    n_chunk = y_ref.shape[2] // len(POOL_WINDOWS)
    pos = l * tl + lax.broadcasted_iota(jnp.int32, (tl, pg), 0)
    for g, w in enumerate(POOL_WINDOWS):
        n0 = g * n_chunk
        y_ref[0, :, n0:n0 + n_chunk] = x_ref[0, :, n0:n0 + n_chunk] + jnp.dot(
            o_bf16, wo_ref[0:wo_attn, n0:n0 + n_chunk], preferred_element_type=F32)
        c0 = g * pg
        wsum = uext[P_HALO:P_HALO + tl, c0:c0 + pg]
        for j in range(1, w):
            wsum = wsum + uext[P_HALO - j:P_HALO - j + tl, c0:c0 + pg]
        count = jnp.minimum(pos + 1, w).astype(F32)
        dlt = (wsum / count - uext[P_HALO:P_HALO + tl, c0:c0 + pg]).astype(BF16)
        yp = jnp.dot(dlt, pw_ref[g], preferred_element_type=F32) * ps_ref[:, c0:c0 + pg]
        mix[:, c0:c0 + pg] = yp.astype(BF16)

    y_ref[0] += jnp.dot(mix[...], wo_ref[wo_attn:, :], preferred_element_type=F32)

    @pl.when(l == pl.num_programs(1) - 1)
    def _():
        newp_ref[0] = uext[P_HALO + tl - (POOL_MAX - 1):P_HALO + tl, :]


def pool_out(o, u, x, pool_w, pool_scale, w_out, tl):
    bsz, seq, d = x.shape
    wp = u.shape[-1]
    tl = min(tl, seq)
    assert seq % tl == 0 and tl >= P_HALO
    const = _resident
    return pl.pallas_call(
        _pool_out_kernel,
        grid=(bsz, seq // tl),
        in_specs=[pl.BlockSpec((1, tl, o.shape[-1]), lambda b, l: (b, l, 0)),
                  pl.BlockSpec((1, tl, wp), lambda b, l: (b, l, 0)),
                  pl.BlockSpec((1, tl, d), lambda b, l: (b, l, 0)),
                  const(pool_w.shape), const((1, wp)), const(w_out.shape)],
        out_specs=[pl.BlockSpec((1, tl, d), lambda b, l: (b, l, 0)),
                   pl.BlockSpec((1, POOL_MAX - 1, wp), lambda b, l: (b, 0, 0))],
        out_shape=[jax.ShapeDtypeStruct((bsz, seq, d), F32),
                   jax.ShapeDtypeStruct((bsz, POOL_MAX - 1, wp), F32)],
        scratch_shapes=[pltpu.VMEM((P_HALO + tl, wp), F32), pltpu.VMEM((tl, wp), BF16)],
        compiler_params=_params("parallel", "arbitrary"),
        name="pool_out",
    )(o, u, x, pool_w, pool_scale.reshape(1, wp), w_out)


def _pool_step_kernel(o_ref, u_ref, x_ref, stp_ref, pw_ref, ps_ref, wo_ref, y_ref, *, start_pos):
    wp = u_ref.shape[1]
    pg = wp // len(POOL_WINDOWS)
    u = u_ref[...]
    parts = [o_ref[...].astype(BF16)]
    for g, w in enumerate(POOL_WINDOWS):
        c0 = g * pg
        wsum = u[:, c0:c0 + pg]
        for j in range(1, w):
            wsum = wsum + stp_ref[POOL_MAX - 1 - j][:, c0:c0 + pg]
        count = float(min(start_pos + 1, w))
        dlt = (wsum / count - u[:, c0:c0 + pg]).astype(BF16)
        yp = jnp.dot(dlt, pw_ref[g], preferred_element_type=F32) * ps_ref[:, c0:c0 + pg]
        parts.append(yp.astype(BF16))
    mix = jnp.concatenate(parts, axis=-1)
    y_ref[...] = x_ref[...] + jnp.dot(mix, wo_ref[...], preferred_element_type=F32)


def pool_step(o, u, x, st_p, pool_w, pool_scale, w_out, start_pos):
    bd, d = x.shape
    return pl.pallas_call(
        functools.partial(_pool_step_kernel, start_pos=start_pos),
        out_shape=jax.ShapeDtypeStruct((bd, d), F32),
        compiler_params=pltpu.CompilerParams(vmem_limit_bytes=VMEM_LIMIT),
        name="pool_step",
    )(o, u, x, st_p, pool_w, pool_scale.reshape(1, -1), w_out)


TM_PROMPT = 512
TM_FFN = 1024
TF = 512
TL_MIX = 256


def kernel(x_prompt, x_sample, state_conv_a, state_conv_b, cache_k, cache_v, state_pool, page_table, conv_norm_g, conv_w_in, conv_a_dw, conv_a_dw_b, conv_a_ln_g, conv_a_ln_b, conv_b_dw, conv_w_out, attn_norm_g, attn_w_in, pool_w, pool_scale, attn_w_out, ffn_norm_g, ffn_w_gate, ffn_w_up, ffn_w_down, final_norm_g):
    bsz, seq, d = x_prompt.shape
    bd, dec_seq, _ = x_sample.shape
    assert dec_seq == 1
    depth = ffn_norm_g.shape[0]
    n_heads = cache_k.shape[3]
    wattn = n_heads * HEAD_DIM
    past_len = page_table.shape[1] * PAGE_SIZE

    layer_bf16 = lambda w, n: w[n].astype(BF16)

    xp = x_prompt.reshape(bsz * seq, d)
    xs = x_sample.reshape(bd, d)
    pa, pb, pk, pv, pp = [], [], [], [], []
    sa, sb, sk, sv, sp = [], [], [], [], []
    for layer in range(depth):
        i = layer // 2
        if layer % 2 == 0:
            w_in, w_out = layer_bf16(conv_w_in, i), layer_bf16(conv_w_out, i)
            parts = norm_matmul(xp, conv_norm_g[i], w_in, TM_PROMPT)
            wa = parts[0].shape[-1]
            y, na, nb_ = conv_mix([t.reshape(bsz, seq, wa) for t in parts], xp.reshape(bsz, seq, d),
                                  conv_a_dw[i], conv_a_dw_b[i], conv_a_ln_g[i], conv_a_ln_b[i], conv_b_dw[i],
                                  w_out, TL_MIX)
            xp = y.reshape(bsz * seq, d)
            pa.append(na)
            pb.append(nb_)

            parts = norm_matmul(xs, conv_norm_g[i], w_in, TM_PROMPT)
            st_a = jnp.swapaxes(state_conv_a[i], 0, 1)
            st_b = jnp.swapaxes(state_conv_b[i], 0, 1)
            xs, a_new, c_new = conv_step(parts, xs, st_a, st_b, conv_a_dw[i], conv_a_dw_b[i], conv_a_ln_g[i],
                                         conv_a_ln_b[i], conv_b_dw[i], w_out)
            sa.append(jnp.concatenate([state_conv_a[i][:, 1:], a_new[:, None]], axis=1))
            sb.append(jnp.concatenate([state_conv_b[i][:, 1:], c_new[:, None]], axis=1))
        else:
            w_in, w_out, w_pool = layer_bf16(attn_w_in, i), layer_bf16(attn_w_out, i), layer_bf16(pool_w, i)
            q, k, v, u = norm_matmul(xp, attn_norm_g[i], w_in, TM_PROMPT, head_major=3)
            shp = (n_heads, bsz, seq, HEAD_DIM)
            o, block_sums = moba_prompt(q.reshape(shp), k.reshape(shp), v.reshape(shp), cache_k, i, page_table)
            y, npool = pool_out(o, u.reshape(bsz, seq, -1), xp.reshape(bsz, seq, d), w_pool, pool_scale[i],
                                w_out, TL_MIX)
            xp = y.reshape(bsz * seq, d)
            pk.append(jnp.transpose(k.reshape(shp), (1, 2, 0, 3)))
            pv.append(jnp.transpose(v.reshape(shp), (1, 2, 0, 3)))
            pp.append(npool)

            q, k, v, u = norm_matmul(xs, attn_norm_g[i], w_in, TM_PROMPT, head_major=3)
            heads = lambda t: jnp.swapaxes(t, 0, 1)
            sel = gate_step(block_sums, heads(q))
            o = attn_step(cache_k, cache_v, i, page_table, sel, heads(q), heads(k), heads(v))
            st_p = jnp.swapaxes(state_pool[i], 0, 1)
            xs = pool_step(o.reshape(bd, wattn), u, xs, st_p, w_pool, pool_scale[i], w_out, past_len)
            sk.append(heads(k).reshape(bd, 1, n_heads, HEAD_DIM))
            sv.append(heads(v).reshape(bd, 1, n_heads, HEAD_DIM))
            sp.append(jnp.concatenate([state_pool[i][:, 1:], u[:, None]], axis=1))
        last = layer == depth - 1
        ffn_w = (ffn_w_gate, ffn_w_up, ffn_w_down, layer)
        xp, xs = ffn(xp, xs, ffn_norm_g[layer], *ffn_w, final_norm_g, TM_FFN, TF, last)
    return (xp.reshape(bsz, seq, d), xs.reshape(bd, 1, d), jnp.stack(pa), jnp.stack(sa), jnp.stack(pb),
            jnp.stack(sb), jnp.stack(pk), jnp.stack(sk), jnp.stack(pv), jnp.stack(sv), jnp.stack(pp), jnp.stack(sp))
```

```python
import functools

import jax
import jax.numpy as jnp
from jax import lax
from jax.experimental import pallas as pl
from jax.experimental.pallas import tpu as pltpu

EPS = 1e-6
PAGE_SIZE = 128
MOBA_BLOCK = 256
MOBA_TOPK = 3
HEAD_DIM = 128
POOL_WINDOWS = (2, 4, 8, 16)
POOL_MAX = max(POOL_WINDOWS)
MASKED = -1e30

V7X_VMEM_BYTES = 64 * 1024 * 1024
VMEM_LIMIT = V7X_VMEM_BYTES - 4 * 1024 * 1024
LANES = 128
SUBLANES = 8
COL_GROUP = 1024

BF16 = jnp.bfloat16
F32 = jnp.float32


def _params(*sem):
    return pltpu.CompilerParams(dimension_semantics=sem, vmem_limit_bytes=VMEM_LIMIT)


def _rms(x, g):
    ms = jnp.mean(x * x, axis=-1, keepdims=True)
    return x * lax.rsqrt(ms + EPS) * g


def _sigmoid(x):
    return 1.0 / (1.0 + jnp.exp(-x))


def _resident(shape):
    return pl.BlockSpec(shape, lambda *_: (0,) * len(shape), pipeline_mode=pl.Buffered(1))


def _norm_matmul_kernel(x_ref, xs_ref, g_ref, w_ref, *out_refs, head_major):
    n_out = len(out_refs) // 2

    def project(t, refs):
        h = _rms(t, g_ref[...]).astype(BF16)
        for c, o_ref in enumerate(refs):
            r = jnp.dot(h, w_ref[:, c * COL_GROUP:(c + 1) * COL_GROUP], preferred_element_type=F32)
            if c < head_major:
                for hh in range(COL_GROUP // HEAD_DIM):
                    o_ref[hh] = r[:, hh * HEAD_DIM:(hh + 1) * HEAD_DIM]
            else:
                o_ref[...] = r

    project(x_ref[...], out_refs[:n_out])

    @pl.when(pl.program_id(0) == 0)
    def _():
        project(xs_ref[...], out_refs[n_out:])


def norm_matmul(x, xs, g, w, tm, head_major=0):
    m, d = x.shape
    ns = xs.shape[0]
    n_out = w.shape[1] // COL_GROUP
    tm = min(tm, m)
    heads = COL_GROUP // HEAD_DIM
    n_flat = n_out - head_major
    specs = ([pl.BlockSpec((heads, tm, HEAD_DIM), lambda i: (0, i, 0))] * head_major
             + [pl.BlockSpec((tm, COL_GROUP), lambda i: (i, 0))] * n_flat
             + [pl.BlockSpec((heads, ns, HEAD_DIM), lambda i: (0, 0, 0))] * head_major
             + [pl.BlockSpec((ns, COL_GROUP), lambda i: (0, 0))] * n_flat)
    shapes = ([jax.ShapeDtypeStruct((heads, m, HEAD_DIM), F32)] * head_major
              + [jax.ShapeDtypeStruct((m, COL_GROUP), F32)] * n_flat
              + [jax.ShapeDtypeStruct((heads, ns, HEAD_DIM), F32)] * head_major
              + [jax.ShapeDtypeStruct((ns, COL_GROUP), F32)] * n_flat)
    outs = pl.pallas_call(
        functools.partial(_norm_matmul_kernel, head_major=head_major),
        grid=(m // tm,),
        in_specs=[pl.BlockSpec((tm, d), lambda i: (i, 0)), _resident((ns, d)), _resident((1, d)),
                  _resident(w.shape)],
        out_specs=specs,
        out_shape=shapes,
        compiler_params=_params("arbitrary"),
        name="norm_matmul",
    )(x, xs, g.reshape(1, d), w)
    return outs[:n_out], outs[n_out:]


def _ffn_kernel(g_ref, fg_ref, xs_ref, wg_ref, wu_ref, wd_ref, x_hbm, o_ref, os_ref, xbuf, h_ref, sem, *, final_norm):
    i, f = pl.program_id(0), pl.program_id(1)
    tm = o_ref.shape[0]
    ns = xs_ref.shape[0]
    last_f = pl.num_programs(1) - 1

    def x_copy(tile):
        return pltpu.make_async_copy(x_hbm.at[pl.ds(pl.multiple_of(tile * tm, tm), tm), :], xbuf, sem)

    @pl.when((i == 0) & (f == 0))
    def _():
        x_copy(0).start()
        xs = xs_ref[...]
        h_ref[tm:, :] = jnp.zeros((h_ref.shape[0] - tm, h_ref.shape[1]), h_ref.dtype)
        h_ref[tm:tm + ns, :] = _rms(xs, g_ref[...]).astype(h_ref.dtype)
        os_ref[...] = xs

    @pl.when(f == 0)
    def _():
        x_copy(i).wait()
        x = xbuf[...]
        h_ref[0:tm, :] = _rms(x, g_ref[...]).astype(h_ref.dtype)
        o_ref[...] = x

    @pl.when((f == 1) & (i + 1 < pl.num_programs(0)))
    def _():
        x_copy(i + 1).start()

    h = h_ref[...]
    gate = jnp.dot(h, wg_ref[...].astype(BF16), preferred_element_type=F32)
    up = jnp.dot(h, wu_ref[...].astype(BF16), preferred_element_type=F32)
    act = (gate * _sigmoid(gate) * up).astype(BF16)
    down = jnp.dot(act, wd_ref[...].astype(BF16), preferred_element_type=F32)
    o_ref[...] += down[0:tm]

    @pl.when(i == 0)
    def _():
        os_ref[...] += down[tm:tm + ns]

    if final_norm:
        @pl.when(f == last_f)
        def _():
            o_ref[...] = _rms(o_ref[...], fg_ref[...])

        @pl.when((i == 0) & (f == last_f))
        def _():
            os_ref[...] = _rms(os_ref[...], fg_ref[...])


BF16_ROWS = 16


def ffn(x, xs, g, wg, wu, wd, layer, final_g, tm, tf, final_norm):
    m, d = x.shape
    ns = xs.shape[0]
    dff = wg.shape[2]
    tm = min(tm, m)
    assert m % tm == 0 and dff % tf == 0 and dff // tf >= 2 and ns <= BF16_ROWS and tm % BF16_ROWS == 0
    return pl.pallas_call(
        functools.partial(_ffn_kernel, final_norm=final_norm),
        grid=(m // tm, dff // tf),
        in_specs=[_resident((1, d)), _resident((1, d)), _resident((ns, d)),
                  pl.BlockSpec((None, d, tf), lambda i, f: (layer, 0, f)),
                  pl.BlockSpec((None, d, tf), lambda i, f: (layer, 0, f)),
                  pl.BlockSpec((None, tf, d), lambda i, f: (layer, f, 0)),
                  pl.BlockSpec(memory_space=pl.ANY)],
        out_specs=[pl.BlockSpec((tm, d), lambda i, f: (i, 0)), pl.BlockSpec((ns, d), lambda i, f: (0, 0))],
        out_shape=[jax.ShapeDtypeStruct((m, d), F32), jax.ShapeDtypeStruct((ns, d), F32)],
        scratch_shapes=[pltpu.VMEM((tm, d), F32), pltpu.VMEM((tm + BF16_ROWS, d), BF16),
                        pltpu.SemaphoreType.DMA(())],
        compiler_params=_params("arbitrary", "arbitrary"),
        name="ffn",
    )(g.reshape(1, d), final_g.reshape(1, d), xs, wg, wu, wd, x)


A_HALO = 32
B_HALO = 8


def _layer_norm_silu(x, g, b):
    mu = jnp.mean(x, axis=-1, keepdims=True)
    xc = x - mu
    y = xc * lax.rsqrt(jnp.mean(xc * xc, axis=-1, keepdims=True) + EPS) * g + b
    return y * _sigmoid(y)


def _conv_mix_kernel(av_ref, ag_ref, bh_ref, bb_ref, bc_ref, x_ref, adw_ref, adwb_ref, lng_ref, lnb_ref,
                     bdw_ref, wo_ref, y_ref, newa_ref, newb_ref, aext, cext, ashift, aconv, *, ta, tb, rc):
    l = pl.program_id(1)
    tl = av_ref.shape[1]
    wa = av_ref.shape[2]

    @pl.when(l == 0)
    def _():
        aext[0:A_HALO, :] = jnp.zeros((A_HALO, wa), F32)
        cext[0:B_HALO, :] = jnp.zeros((B_HALO, wa), F32)

    @pl.when(l > 0)
    def _():
        aext[0:A_HALO, :] = aext[tl:tl + A_HALO, :]
        cext[0:B_HALO, :] = cext[tl:tl + B_HALO, :]

    aext[A_HALO:A_HALO + tl, :] = av_ref[0] * _sigmoid(ag_ref[0])
    cext[B_HALO:B_HALO + tl, :] = bc_ref[0] * bh_ref[0]

    bconv = jnp.zeros((tl, wa), F32)
    for k in range(tb):
        off = B_HALO - (tb - 1) + k
        bconv = bconv + bdw_ref[k:k + 1, :] * cext[off:off + tl, :]
    b_out = (bb_ref[0] * bconv).astype(BF16)
    y_ref[0] = x_ref[0] + jnp.dot(b_out, wo_ref[wa:, :], preferred_element_type=F32)

    n_shift = ashift.shape[1]
    for s in range(1, SUBLANES):
        ashift[s - 1] = aext[s:s + n_shift, :]

    for r0 in range(0, tl, rc):
        for c0 in range(0, wa, LANES):
            acc = jnp.broadcast_to(adwb_ref[:, c0:c0 + LANES], (rc, LANES))
            for k in range(ta):
                off = A_HALO - (ta - 1) + r0 + k
                s = off % SUBLANES
                src = aext if s == 0 else ashift.at[s - 1]
                acc = acc + adw_ref[k:k + 1, c0:c0 + LANES] * src[off - s:off - s + rc, c0:c0 + LANES]
            aconv[r0:r0 + rc, c0:c0 + LANES] = acc
    a_out = _layer_norm_silu(aconv[...], lng_ref[...], lnb_ref[...]).astype(BF16)
    y_ref[0] += jnp.dot(a_out, wo_ref[0:wa, :], preferred_element_type=F32)

    @pl.when(l == pl.num_programs(1) - 1)
    def _():
        newa_ref[0] = aext[A_HALO + tl - (ta - 1):A_HALO + tl, :]
        newb_ref[0] = cext[B_HALO + tl - (tb - 1):B_HALO + tl, :]


def conv_mix(parts, x, a_dw, a_dw_b, ln_g, ln_b, b_dw, w_out, tl):
    bsz, seq, d = x.shape
    wa = parts[0].shape[-1]
    ta, tb = a_dw.shape[0], b_dw.shape[0]
    tl = min(tl, seq)
    assert seq % tl == 0 and tl >= A_HALO and ta - 1 <= A_HALO and tb - 1 <= B_HALO
    part_spec = pl.BlockSpec((1, tl, wa), lambda b, l: (b, l, 0))
    const = _resident
    return pl.pallas_call(
        functools.partial(_conv_mix_kernel, ta=ta, tb=tb, rc=min(128, tl)),
        grid=(bsz, seq // tl),
        in_specs=[part_spec] * 5 + [pl.BlockSpec((1, tl, d), lambda b, l: (b, l, 0)),
                                    const((ta, wa)), const((1, wa)), const((1, wa)), const((1, wa)),
                                    const((tb, wa)), const(w_out.shape)],
        out_specs=[pl.BlockSpec((1, tl, d), lambda b, l: (b, l, 0)),
                   pl.BlockSpec((1, ta - 1, wa), lambda b, l: (b, 0, 0)),
                   pl.BlockSpec((1, tb - 1, wa), lambda b, l: (b, 0, 0))],
        out_shape=[jax.ShapeDtypeStruct((bsz, seq, d), F32),
                   jax.ShapeDtypeStruct((bsz, ta - 1, wa), F32),
                   jax.ShapeDtypeStruct((bsz, tb - 1, wa), F32)],
        scratch_shapes=[pltpu.VMEM((A_HALO + tl, wa), F32), pltpu.VMEM((B_HALO + tl, wa), F32),
                        pltpu.VMEM((SUBLANES - 1, A_HALO + tl - SUBLANES, wa), F32),
                        pltpu.VMEM((tl, wa), F32)],
        compiler_params=_params("parallel", "arbitrary"),
        name="conv_mix",
    )(*parts, x, a_dw, a_dw_b.reshape(1, wa), ln_g.reshape(1, wa), ln_b.reshape(1, wa), b_dw, w_out)


def _conv_step_kernel(av_ref, ag_ref, bh_ref, bb_ref, bc_ref, x_ref, sta_ref, stb_ref, adw_ref, adwb_ref,
                      lng_ref, lnb_ref, bdw_ref, wo_ref, y_ref, anew_ref, cnew_ref, *, ta, tb):
    a = av_ref[...] * _sigmoid(ag_ref[...])
    c = bc_ref[...] * bh_ref[...]
    anew_ref[...] = a
    cnew_ref[...] = c
    acc = adwb_ref[...] + adw_ref[ta - 1:ta, :] * a
    for k in range(ta - 1):
        acc = acc + adw_ref[k:k + 1, :] * sta_ref[k]
    a_out = _layer_norm_silu(acc, lng_ref[...], lnb_ref[...])
    bconv = bdw_ref[tb - 1:tb, :] * c
    for k in range(tb - 1):
        bconv = bconv + bdw_ref[k:k + 1, :] * stb_ref[k]
    mix = jnp.concatenate([a_out, bb_ref[...] * bconv], axis=-1).astype(BF16)
    y_ref[...] = x_ref[...] + jnp.dot(mix, wo_ref[...], preferred_element_type=F32)


def conv_step(parts, x, st_a, st_b, a_dw, a_dw_b, ln_g, ln_b, b_dw, w_out):
    bd, d = x.shape
    wa = parts[0].shape[-1]
    ta, tb = a_dw.shape[0], b_dw.shape[0]
    return pl.pallas_call(
        functools.partial(_conv_step_kernel, ta=ta, tb=tb),
        out_shape=[jax.ShapeDtypeStruct((bd, d), F32), jax.ShapeDtypeStruct((bd, wa), F32),
                   jax.ShapeDtypeStruct((bd, wa), F32)],
        compiler_params=pltpu.CompilerParams(vmem_limit_bytes=VMEM_LIMIT),
        name="conv_step",
    )(*parts, x, st_a, st_b, a_dw, a_dw_b.reshape(1, wa), ln_g.reshape(1, wa), ln_b.reshape(1, wa), b_dw, w_out)


LOG2E = 1.4426950408889634
TQ = 2 * MOBA_BLOCK
DA = 2 * HEAD_DIM


def _split3(x):
    def top_bits(v):
        return lax.bitcast_convert_type(lax.bitcast_convert_type(v, jnp.uint32) & jnp.uint32(0xFFFF0000), F32)

    hi = top_bits(x)
    mid = top_bits(x - hi)
    return hi, mid, x - hi - mid


def _moba_setup(q_ref, k_ref, v_ref, qaug, kaug, vaug, kmean, slope2, nb):
    seq = k_ref.shape[1]
    bs = MOBA_BLOCK
    nbp = kmean.shape[0]
    k = k_ref[0]
    q = q_ref[0]
    kmean[...] = jnp.zeros_like(kmean)
    kmean[0:nb, :] = jnp.mean(k.reshape(nb, bs, HEAD_DIM), axis=1)

    gate = lax.dot_general(kmean[...], q, (((1,), (1,)), ((), ())), precision=lax.Precision.HIGHEST,
                           preferred_element_type=F32)
    blk = lax.broadcasted_iota(jnp.int32, (nbp, seq), 0)
    own = lax.broadcasted_iota(jnp.int32, (nbp, seq), 1) // bs
    gate = jnp.where(blk < own, gate, -jnp.inf)
    attend = blk == own
    for _ in range(MOBA_TOPK):
        m = jnp.max(gate, axis=0, keepdims=True)
        idx = jnp.min(jnp.where(gate == m, blk, nbp), axis=0, keepdims=True)
        hit = blk == idx
        attend = attend | (hit & (blk < own))
        gate = jnp.where(hit, -jnp.inf, gate)
    attend_t = jnp.where(attend, 1.0, 0.0)
    attend_r = jnp.concatenate([attend_t, jnp.zeros((LANES - nbp, seq), F32)], axis=0).T

    lane = lax.broadcasted_iota(jnp.int32, (seq, LANES), 1)
    pos_i = lax.broadcasted_iota(jnp.int32, (seq, LANES), 0)
    pos = pos_i.astype(F32)

    khi, kmid, klo = _split3(slope2 * pos)
    ek = jnp.where(lane == pos_i // bs, 1.0, 0.0)
    ek = jnp.where(lane == nb, khi, ek)
    ek = jnp.where(lane == nb + 1, kmid, ek)
    ek = jnp.where(lane == nb + 2, klo, ek)
    ek = jnp.where((lane >= nb + 3) & (lane < nb + 6), 1.0, ek)
    kaug[:, 0:HEAD_DIM] = k.astype(BF16)
    kaug[:, HEAD_DIM:] = ek.astype(BF16)

    qhi, qmid, qlo = _split3(-slope2 * pos)
    eq = jnp.where(lane < nb, jnp.where(attend_r > 0.5, 0.0, MASKED), 0.0)
    eq = jnp.where((lane >= nb) & (lane < nb + 3), 1.0, eq)
    eq = jnp.where(lane == nb + 3, qhi, eq)
    eq = jnp.where(lane == nb + 4, qmid, eq)
    eq = jnp.where(lane == nb + 5, qlo, eq)
    qaug[:, 0:HEAD_DIM] = (q * (HEAD_DIM ** -0.5 * LOG2E)).astype(BF16)
    qaug[:, HEAD_DIM:] = eq.astype(BF16)

    vaug[:, 0:HEAD_DIM] = v_ref[0].astype(BF16)
    vaug[:, HEAD_DIM:] = jnp.ones((seq, HEAD_DIM), BF16)


SUM_CHAINS = 16
PAGE_LOOKAHEAD = 3


class _PageSums:
    def __init__(self, pt_ref, kc_hbm, ksum_ref, ring, sem, *, layer, pages_per_block, n_steps):
        self.pt_ref, self.kc_hbm, self.ksum_ref, self.ring, self.sem = pt_ref, kc_hbm, ksum_ref, ring, sem
        self.layer, self.pages_per_block, self.n_steps = layer, pages_per_block, n_steps
        self.total = pt_ref.shape[0]
        self.depth = ring.shape[0]
        self.per = self.depth // PAGE_LOOKAHEAD
        self.even = n_steps * self.per == self.total

    def _copy(self, t, slot):
        return pltpu.make_async_copy(self.kc_hbm.at[self.layer, self.pt_ref[t]], self.ring.at[slot],
                                     self.sem.at[slot])

    def begin(self):
        self.ksum_ref[...] = jnp.zeros_like(self.ksum_ref)
        for slot in range(min(self.depth, self.total)):
            self._copy(slot, slot).start()

    def _fold_page(self, t, slot):
        _, page_rows, n_heads, _ = self.ring.shape
        self._copy(t, slot).wait()
        page = self.ring[slot].reshape(SUM_CHAINS, page_rows // SUM_CHAINS, n_heads, HEAD_DIM)
        self.ksum_ref[t // self.pages_per_block] += jnp.sum(jnp.sum(page, axis=1), axis=0)

    def _first_slot(self, step):
        return (step % PAGE_LOOKAHEAD) * self.per

    def fold(self, step):
        assert self.even
        for j in range(self.per):
            self._fold_page(step * self.per + j, self._first_slot(step) + j)

    def refill(self, step):
        assert self.even

        @pl.when(step + PAGE_LOOKAHEAD < self.n_steps)
        def _():
            for j in range(self.per):
                self._copy(step * self.per + j + self.depth, self._first_slot(step) + j).start()

    def step(self, step):
        for j in range(self.per):
            t = step * self.per + j

            @pl.when(t < self.total)
            def _(t=t, j=j):
                self._fold_page(t, self._first_slot(step) + j)

                @pl.when(t + self.depth < self.total)
                def _():
                    self._copy(t + self.depth, self._first_slot(step) + j).start()


def _moba_kernel(pt_ref, q_ref, k_ref, v_ref, kc_hbm, o_ref, ksum_ref, qaug, kaug, vaug, kmean, m_ref, acc_ref,
                 s_ref, rel_ref, ring, psem, *, n_heads, nb, layer, page_pairs, page_steps):
    h = pl.program_id(1)
    bs = MOBA_BLOCK
    nt = (((1,), (1,)), ((), ()))
    seq = qaug.shape[0]
    n_tiles = seq // TQ
    n_units = n_tiles * (n_tiles + 1) // 2
    n_pairs = n_units // 2
    bh = pl.program_id(0) * pl.num_programs(1) + h

    slope2 = jnp.exp2((h + 1).astype(F32) * (-8.0 / n_heads) + jnp.zeros((1, 1), F32)) * LOG2E
    _moba_setup(q_ref, k_ref, v_ref, qaug, kaug, vaug, kmean, slope2, nb)
    rel_ref[...] = (lax.broadcasted_iota(jnp.int32, (bs, bs), 1) - lax.broadcasted_iota(jnp.int32, (bs, bs), 0))

    halves = (slice(0, bs), slice(bs, TQ))

    def tile_rows(t):
        return pl.ds(pl.multiple_of(t * TQ, TQ), TQ)

    def successor(unit):
        i, g = unit
        wrap = g == i
        return jnp.where(wrap, jnp.minimum(i + 1, n_tiles - 1), i), jnp.where(wrap, 0, g + 1)

    def scores(unit):
        i, g = unit
        qa = qaug[tile_rows(i), :]
        kg = kaug[tile_rows(g), :]
        return jnp.concatenate([lax.dot_general(qa[r], kg, nt, preferred_element_type=F32) for r in halves], axis=0)

    def process(unit, slot):
        i, g = unit
        vg = vaug[tile_rows(g), :]
        for r in halves:
            diag = jnp.where(rel_ref[...] <= (i - g) * TQ, s_ref[slot, r, r], MASKED)
            sr = jnp.concatenate([diag, s_ref[slot, r, bs:]] if r.start == 0 else [s_ref[slot, r, 0:bs], diag], axis=1)
            m = jnp.where(g == 0, MASKED, m_ref[r, :])
            m_new = jnp.maximum(m, jnp.max(sr, axis=1, keepdims=True))
            p = jnp.exp2(sr - m_new).astype(BF16)
            pv = jnp.dot(p, vg, preferred_element_type=F32)
            acc = jnp.exp2(m - m_new) * acc_ref[r, :] + pv
            acc_ref[r, :] = acc
            m_ref[r, :] = m_new
            rows = pl.ds(pl.multiple_of(i * TQ + r.start, bs), bs)
            o_ref[0, rows, :] = acc[:, 0:HEAD_DIM] / acc[:, HEAD_DIM:]

    m_ref[...] = jnp.full(m_ref.shape, MASKED, F32)
    acc_ref[...] = jnp.zeros_like(acc_ref)
    first = (jnp.int32(0), jnp.int32(0))
    s_ref[0] = scores(first)

    pages = _PageSums(pt_ref, kc_hbm, ksum_ref, ring, psem, layer=layer, pages_per_block=bs // PAGE_SIZE,
                      n_steps=page_steps)

    @pl.when(bh == 0)
    def _():
        pages.begin()

    def pair(p, unit, with_pages):
        page_step = bh * page_pairs + p
        if with_pages and pages.even:
            pages.fold(page_step)
        elif with_pages:
            pages.step(page_step)
        nxt = successor(unit)
        s_ref[1] = scores(nxt)
        process(unit, 0)
        nxt2 = successor(nxt)
        s_ref[0] = scores(nxt2)
        process(nxt, 1)
        if with_pages and pages.even:
            pages.refill(page_step)
        return nxt2

    unit = lax.fori_loop(0, page_pairs, functools.partial(pair, with_pages=True), first)
    if page_pairs < n_pairs:
        unit = lax.fori_loop(page_pairs, n_pairs, functools.partial(pair, with_pages=False), unit)
    if n_units % 2:
        process(unit, 0)


def moba_prompt(q, k, v, cache_k, layer, page_table):
    n_heads, bsz, seq, _ = q.shape
    width = n_heads * HEAD_DIM
    assert seq % TQ == 0
    nb = seq // MOBA_BLOCK
    nbp = -(-nb // 8) * 8
    assert nb + 6 <= LANES
    bd, n_pages = page_table.shape
    pages_per_block = MOBA_BLOCK // PAGE_SIZE
    assert n_pages % pages_per_block == 0
    n_tiles = seq // TQ
    n_pairs = n_tiles * (n_tiles + 1) // 4
    assert n_pairs >= 1
    total_pages = bd * n_pages
    even = [p for p in range(1, n_pairs + 1) if total_pages % (bsz * n_heads * p) == 0]
    page_pairs = max(even) if even else n_pairs
    page_steps = bsz * n_heads * page_pairs
    pages_per_step = -(-total_pages // page_steps)
    sums_shape = (total_pages // pages_per_block, n_heads, HEAD_DIM)
    slab = pl.BlockSpec((None, 1, seq, HEAD_DIM), lambda b, h: (h, b, 0, 0))
    return pl.pallas_call(
        functools.partial(_moba_kernel, n_heads=n_heads, nb=nb, layer=layer, page_pairs=page_pairs,
                          page_steps=page_steps),
        grid=(bsz, n_heads),
        in_specs=[pl.BlockSpec(memory_space=pltpu.SMEM), slab, slab, slab, pl.BlockSpec(memory_space=pl.ANY)],
        out_specs=[pl.BlockSpec((1, seq, HEAD_DIM), lambda b, h: (b, 0, h)),
                   pl.BlockSpec(sums_shape, lambda b, h: (0, 0, 0))],
        out_shape=[jax.ShapeDtypeStruct((bsz, seq, width), F32), jax.ShapeDtypeStruct(sums_shape, F32)],
        scratch_shapes=[pltpu.VMEM((seq, DA), BF16), pltpu.VMEM((seq, DA), BF16), pltpu.VMEM((seq, DA), BF16),
                        pltpu.VMEM((nbp, HEAD_DIM), F32), pltpu.VMEM((TQ, 1), F32), pltpu.VMEM((TQ, DA), F32),
                        pltpu.VMEM((2, TQ, TQ), F32), pltpu.VMEM((MOBA_BLOCK, MOBA_BLOCK), jnp.int32),
                        pltpu.VMEM((PAGE_LOOKAHEAD * pages_per_step, PAGE_SIZE, n_heads, HEAD_DIM), F32),
                        pltpu.SemaphoreType.DMA((PAGE_LOOKAHEAD * pages_per_step,))],
        compiler_params=_params("arbitrary", "arbitrary"),
        name="moba_prompt",
    )(page_table.reshape(-1), q, k, v, cache_k)


PAGES_IN_FLIGHT = 8


def _gate_step_kernel(q_ref, ksum_ref, sel_ref, *, n_heads):
    bd = q_ref.shape[0]
    nblk = ksum_ref.shape[0] // bd
    sel_ref[...] = jnp.zeros_like(sel_ref)
    for b in range(bd):
        kmean = ksum_ref[b * nblk:(b + 1) * nblk] * (1.0 / MOBA_BLOCK)
        gate = jnp.sum(kmean * q_ref[b][None], axis=-1)
        blk = lax.broadcasted_iota(jnp.int32, gate.shape, 0)
        for r in range(MOBA_TOPK):
            m = jnp.max(gate, axis=0, keepdims=True)
            idx = jnp.min(jnp.where(gate == m, blk, nblk), axis=0, keepdims=True)
            sel_ref[b, r:r + 1, 0:n_heads] = idx
            gate = jnp.where(blk == idx, -jnp.inf, gate)


SEL_ROWS = 8


def gate_step(block_sums, q):
    bd, n_heads, _ = q.shape
    assert MOBA_TOPK <= SEL_ROWS and n_heads <= LANES
    sel = pl.pallas_call(
        functools.partial(_gate_step_kernel, n_heads=n_heads),
        out_shape=jax.ShapeDtypeStruct((bd, SEL_ROWS, LANES), jnp.int32),
        compiler_params=pltpu.CompilerParams(vmem_limit_bytes=VMEM_LIMIT),
        name="gate_step",
    )(q, block_sums)
    return sel[:, :MOBA_TOPK, :n_heads]


def _attn_step_kernel(pt_ref, sel_ref, q_ref, kn_ref, vn_ref, k_hbm, v_hbm, o_ref, kbuf, vbuf, ksem, vsem,
                      m_ref, l_ref, acc_ref, *, layer, n_pages, q_pos, pages_per_block):
    bd, n_heads, _ = q_ref.shape
    per_head = MOBA_TOPK * pages_per_block
    total = bd * n_heads * per_head
    depth = kbuf.shape[0]
    scale = HEAD_DIM ** -0.5
    own = q_pos // MOBA_BLOCK

    def coords(t):
        bh, j = t // per_head, t % per_head
        b, h = bh // n_heads, bh % n_heads
        return b, h, j, sel_ref[(b * MOBA_TOPK + j // pages_per_block) * n_heads + h]

    def page_copies(t, slot):
        b, _, j, n = coords(t)
        page = pt_ref[b * n_pages + n * pages_per_block + j % pages_per_block]
        return (pltpu.make_async_copy(k_hbm.at[layer, page], kbuf.at[slot], ksem.at[slot]),
                pltpu.make_async_copy(v_hbm.at[layer, page], vbuf.at[slot], vsem.at[slot]))

    for t in range(min(depth, total)):
        for c in page_copies(t, t):
            c.start()

    @pl.loop(0, total)
    def _(t):
        slot = t % depth
        b, h, j, n = coords(t)
        for c in page_copies(t, slot):
            c.wait()
        q = q_ref[b, pl.ds(h, 1), :]

        @pl.when(j == 0)
        def _():
            m_ref[...] = jnp.sum(q * kn_ref[b, pl.ds(h, 1), :], axis=1, keepdims=True) * scale
            l_ref[...] = jnp.ones_like(l_ref)
            acc_ref[...] = vn_ref[b, pl.ds(h, 1), :]

        k = kbuf[slot, pl.ds(h, PAGE_SIZE, stride=n_heads), :]
        v = vbuf[slot, pl.ds(h, PAGE_SIZE, stride=n_heads), :]
        slope = jnp.exp2(jnp.asarray(h + 1, F32) * (-8.0 / n_heads) + jnp.zeros((1, 1), F32))
        kpos = (n * MOBA_BLOCK + (j % pages_per_block) * PAGE_SIZE
                + lax.broadcasted_iota(jnp.int32, (PAGE_SIZE, 1), 0))
        dist = (q_pos - kpos).astype(F32)
        s = jnp.sum(k * q, axis=1, keepdims=True) * scale - slope * dist
        s = jnp.where((dist >= 0.0) & (n < own), s, MASKED)
        m = m_ref[...]
        m_new = jnp.maximum(m, jnp.max(s, axis=0, keepdims=True))
        alpha = jnp.exp(m - m_new)
        p = jnp.exp(s - m_new)
        l_ref[...] = alpha * l_ref[...] + jnp.sum(p, axis=0, keepdims=True)
        acc_ref[...] = alpha * acc_ref[...] + jnp.sum(p * v, axis=0, keepdims=True)
        m_ref[...] = m_new

        @pl.when(j == per_head - 1)
        def _():
            o_ref[b, pl.ds(h, 1), :] = acc_ref[...] / l_ref[...]

        @pl.when(t + depth < total)
        def _():
            for c in page_copies(t + depth, slot):
                c.start()


def attn_step(cache_k, cache_v, layer, page_table, sel, q, k_new, v_new):
    bd, n_pages = page_table.shape
    n_heads = q.shape[1]
    pages_per_block = MOBA_BLOCK // PAGE_SIZE
    assert n_pages % pages_per_block == 0
    rows = PAGE_SIZE * n_heads
    as_rows = lambda c: c.reshape(c.shape[0], c.shape[1], rows, HEAD_DIM)
    vmem, smem = pl.BlockSpec(memory_space=pltpu.VMEM), pl.BlockSpec(memory_space=pltpu.SMEM)
    hbm = pl.BlockSpec(memory_space=pl.ANY)
    return pl.pallas_call(
        functools.partial(_attn_step_kernel, layer=layer, n_pages=n_pages, q_pos=n_pages * PAGE_SIZE,
                          pages_per_block=pages_per_block),
        in_specs=[smem, smem, vmem, vmem, vmem, hbm, hbm],
        out_specs=vmem,
        out_shape=jax.ShapeDtypeStruct((bd, n_heads, HEAD_DIM), F32),
        scratch_shapes=[pltpu.VMEM((PAGES_IN_FLIGHT, rows, HEAD_DIM), F32),
                        pltpu.VMEM((PAGES_IN_FLIGHT, rows, HEAD_DIM), F32),
                        pltpu.SemaphoreType.DMA((PAGES_IN_FLIGHT,)), pltpu.SemaphoreType.DMA((PAGES_IN_FLIGHT,)),
                        pltpu.VMEM((1, 1), F32), pltpu.VMEM((1, 1), F32), pltpu.VMEM((1, HEAD_DIM), F32)],
        compiler_params=pltpu.CompilerParams(vmem_limit_bytes=VMEM_LIMIT),
        name="attn_step",
    )(page_table.reshape(-1), sel.reshape(-1), q, k_new, v_new, as_rows(cache_k), as_rows(cache_v))


P_HALO = 16


def _pool_out_kernel(o_ref, u_ref, x_ref, pw_ref, ps_ref, wo_ref, y_ref, newp_ref, uext, mix):
    l = pl.program_id(1)
    tl = u_ref.shape[1]
    wp = u_ref.shape[2]
    wo_attn = o_ref.shape[2]
    pg = wp // len(POOL_WINDOWS)

    @pl.when(l == 0)
    def _():
        uext[0:P_HALO, :] = jnp.zeros((P_HALO, wp), F32)

    @pl.when(l > 0)
    def _():
        uext[0:P_HALO, :] = uext[tl:tl + P_HALO, :]

    uext[P_HALO:P_HALO + tl, :] = u_ref[0]
    o_bf16 = o_ref[0].astype(BF16)
    n_chunk = y_ref.shape[2] // len(POOL_WINDOWS)
    pos = l * tl + lax.broadcasted_iota(jnp.int32, (tl, pg), 0)
    for g, w in enumerate(POOL_WINDOWS):
        n0 = g * n_chunk
        y_ref[0, :, n0:n0 + n_chunk] = x_ref[0, :, n0:n0 + n_chunk] + jnp.dot(
            o_bf16, wo_ref[0:wo_attn, n0:n0 + n_chunk], preferred_element_type=F32)
        c0 = g * pg
        wsum = uext[P_HALO:P_HALO + tl, c0:c0 + pg]
        for j in range(1, w):
            wsum = wsum + uext[P_HALO - j:P_HALO - j + tl, c0:c0 + pg]
        count = jnp.minimum(pos + 1, w).astype(F32)
        dlt = (wsum / count - uext[P_HALO:P_HALO + tl, c0:c0 + pg]).astype(BF16)
        yp = jnp.dot(dlt, pw_ref[g], preferred_element_type=F32) * ps_ref[:, c0:c0 + pg]
        mix[:, c0:c0 + pg] = yp.astype(BF16)

    y_ref[0] += jnp.dot(mix[...], wo_ref[wo_attn:, :], preferred_element_type=F32)

    @pl.when(l == pl.num_programs(1) - 1)
    def _():
        newp_ref[0] = uext[P_HALO + tl - (POOL_MAX - 1):P_HALO + tl, :]


def pool_out(o, u, x, pool_w, pool_scale, w_out, tl):
    bsz, seq, d = x.shape
    wp = u.shape[-1]
    tl = min(tl, seq)
    assert seq % tl == 0 and tl >= P_HALO
    const = _resident
    return pl.pallas_call(
        _pool_out_kernel,
        grid=(bsz, seq // tl),
        in_specs=[pl.BlockSpec((1, tl, o.shape[-1]), lambda b, l: (b, l, 0)),
                  pl.BlockSpec((1, tl, wp), lambda b, l: (b, l, 0)),
                  pl.BlockSpec((1, tl, d), lambda b, l: (b, l, 0)),
                  const(pool_w.shape), const((1, wp)), const(w_out.shape)],
        out_specs=[pl.BlockSpec((1, tl, d), lambda b, l: (b, l, 0)),
                   pl.BlockSpec((1, POOL_MAX - 1, wp), lambda b, l: (b, 0, 0))],
        out_shape=[jax.ShapeDtypeStruct((bsz, seq, d), F32),
                   jax.ShapeDtypeStruct((bsz, POOL_MAX - 1, wp), F32)],
        scratch_shapes=[pltpu.VMEM((P_HALO + tl, wp), F32), pltpu.VMEM((tl, wp), BF16)],
        compiler_params=_params("parallel", "arbitrary"),
        name="pool_out",
    )(o, u, x, pool_w, pool_scale.reshape(1, wp), w_out)


def _pool_step_kernel(o_ref, u_ref, x_ref, stp_ref, pw_ref, ps_ref, wo_ref, y_ref, *, start_pos):
    wp = u_ref.shape[1]
    pg = wp // len(POOL_WINDOWS)
    u = u_ref[...]
    parts = [o_ref[...].astype(BF16)]
    for g, w in enumerate(POOL_WINDOWS):
        c0 = g * pg
        wsum = u[:, c0:c0 + pg]
        for j in range(1, w):
            wsum = wsum + stp_ref[POOL_MAX - 1 - j][:, c0:c0 + pg]
        count = float(min(start_pos + 1, w))
        dlt = (wsum / count - u[:, c0:c0 + pg]).astype(BF16)
        yp = jnp.dot(dlt, pw_ref[g], preferred_element_type=F32) * ps_ref[:, c0:c0 + pg]
        parts.append(yp.astype(BF16))
    mix = jnp.concatenate(parts, axis=-1)
    y_ref[...] = x_ref[...] + jnp.dot(mix, wo_ref[...], preferred_element_type=F32)


def pool_step(o, u, x, st_p, pool_w, pool_scale, w_out, start_pos):
    bd, d = x.shape
    return pl.pallas_call(
        functools.partial(_pool_step_kernel, start_pos=start_pos),
        out_shape=jax.ShapeDtypeStruct((bd, d), F32),
        compiler_params=pltpu.CompilerParams(vmem_limit_bytes=VMEM_LIMIT),
        name="pool_step",
    )(o, u, x, st_p, pool_w, pool_scale.reshape(1, -1), w_out)


TM_PROMPT = 512
TM_FFN = 1024
TF = 512
TL_MIX = 256


def kernel(x_prompt, x_sample, state_conv_a, state_conv_b, cache_k, cache_v, state_pool, page_table, conv_norm_g, conv_w_in, conv_a_dw, conv_a_dw_b, conv_a_ln_g, conv_a_ln_b, conv_b_dw, conv_w_out, attn_norm_g, attn_w_in, pool_w, pool_scale, attn_w_out, ffn_norm_g, ffn_w_gate, ffn_w_up, ffn_w_down, final_norm_g):
    bsz, seq, d = x_prompt.shape
    bd, dec_seq, _ = x_sample.shape
    assert dec_seq == 1
    depth = ffn_norm_g.shape[0]
    n_heads = cache_k.shape[3]
    wattn = n_heads * HEAD_DIM
    past_len = page_table.shape[1] * PAGE_SIZE

    layer_bf16 = lambda w, n: w[n].astype(BF16)

    xp = x_prompt.reshape(bsz * seq, d)
    xs = x_sample.reshape(bd, d)
    pa, pb, pk, pv, pp = [], [], [], [], []
    sa, sb, sk, sv, sp = [], [], [], [], []
    for layer in range(depth):
        i = layer // 2
        if layer % 2 == 0:
            w_in, w_out = layer_bf16(conv_w_in, i), layer_bf16(conv_w_out, i)
            parts, parts_s = norm_matmul(xp, xs, conv_norm_g[i], w_in, TM_PROMPT)
            wa = parts[0].shape[-1]
            y, na, nb_ = conv_mix([t.reshape(bsz, seq, wa) for t in parts], xp.reshape(bsz, seq, d),
                                  conv_a_dw[i], conv_a_dw_b[i], conv_a_ln_g[i], conv_a_ln_b[i], conv_b_dw[i],
                                  w_out, TL_MIX)
            xp = y.reshape(bsz * seq, d)
            pa.append(na)
            pb.append(nb_)

            st_a = jnp.swapaxes(state_conv_a[i], 0, 1)
            st_b = jnp.swapaxes(state_conv_b[i], 0, 1)
            xs, a_new, c_new = conv_step(parts_s, xs, st_a, st_b, conv_a_dw[i], conv_a_dw_b[i], conv_a_ln_g[i],
                                         conv_a_ln_b[i], conv_b_dw[i], w_out)
            sa.append(jnp.concatenate([state_conv_a[i][:, 1:], a_new[:, None]], axis=1))
            sb.append(jnp.concatenate([state_conv_b[i][:, 1:], c_new[:, None]], axis=1))
        else:
            w_in, w_out, w_pool = layer_bf16(attn_w_in, i), layer_bf16(attn_w_out, i), layer_bf16(pool_w, i)
            (q, k, v, u), proj_s = norm_matmul(xp, xs, attn_norm_g[i], w_in, TM_PROMPT, head_major=3)
            shp = (n_heads, bsz, seq, HEAD_DIM)
            o, block_sums = moba_prompt(q.reshape(shp), k.reshape(shp), v.reshape(shp), cache_k, i, page_table)
            y, npool = pool_out(o, u.reshape(bsz, seq, -1), xp.reshape(bsz, seq, d), w_pool, pool_scale[i],
                                w_out, TL_MIX)
            xp = y.reshape(bsz * seq, d)
            pk.append(jnp.transpose(k.reshape(shp), (1, 2, 0, 3)))
            pv.append(jnp.transpose(v.reshape(shp), (1, 2, 0, 3)))
            pp.append(npool)

            q, k, v, u = proj_s
            heads = lambda t: jnp.swapaxes(t, 0, 1)
            sel = gate_step(block_sums, heads(q))
            o = attn_step(cache_k, cache_v, i, page_table, sel, heads(q), heads(k), heads(v))
            st_p = jnp.swapaxes(state_pool[i], 0, 1)
            xs = pool_step(o.reshape(bd, wattn), u, xs, st_p, w_pool, pool_scale[i], w_out, past_len)
            sk.append(heads(k).reshape(bd, 1, n_heads, HEAD_DIM))
            sv.append(heads(v).reshape(bd, 1, n_heads, HEAD_DIM))
            sp.append(jnp.concatenate([state_pool[i][:, 1:], u[:, None]], axis=1))
        last = layer == depth - 1
        ffn_w = (ffn_w_gate, ffn_w_up, ffn_w_down, layer)
        xp, xs = ffn(xp, xs, ffn_norm_g[layer], *ffn_w, final_norm_g, TM_FFN, TF, last)
    return (xp.reshape(bsz, seq, d), xs.reshape(bd, 1, d), jnp.stack(pa), jnp.stack(sa), jnp.stack(pb),
            jnp.stack(sb), jnp.stack(pk), jnp.stack(sk), jnp.stack(pv), jnp.stack(sv), jnp.stack(pp), jnp.stack(sp))
```

```python
import functools

import jax
import jax.numpy as jnp
from jax import lax
from jax.experimental import pallas as pl
from jax.experimental.pallas import tpu as pltpu

EPS = 1e-6
PAGE_SIZE = 128
MOBA_BLOCK = 256
MOBA_TOPK = 3
HEAD_DIM = 128
POOL_WINDOWS = (2, 4, 8, 16)
POOL_MAX = max(POOL_WINDOWS)
MASKED = -1e30

V7X_VMEM_BYTES = 64 * 1024 * 1024
VMEM_LIMIT = V7X_VMEM_BYTES - 4 * 1024 * 1024
LANES = 128
SUBLANES = 8
COL_GROUP = 1024

BF16 = jnp.bfloat16
F32 = jnp.float32


def _params(*sem):
    return pltpu.CompilerParams(dimension_semantics=sem, vmem_limit_bytes=VMEM_LIMIT)


def _rms(x, g):
    ms = jnp.mean(x * x, axis=-1, keepdims=True)
    return x * lax.rsqrt(ms + EPS) * g


def _sigmoid(x):
    return 1.0 / (1.0 + jnp.exp(-x))


def _resident(shape):
    return pl.BlockSpec(shape, lambda *_: (0,) * len(shape), pipeline_mode=pl.Buffered(1))


def _norm_matmul_kernel(x_ref, xs_ref, g_ref, w_ref, *out_refs, head_major):
    n_out = len(out_refs) // 2

    def project(t, refs):
        h = _rms(t, g_ref[...]).astype(BF16)
        for c, o_ref in enumerate(refs):
            r = jnp.dot(h, w_ref[:, c * COL_GROUP:(c + 1) * COL_GROUP], preferred_element_type=F32)
            if c < head_major:
                for hh in range(COL_GROUP // HEAD_DIM):
                    o_ref[hh] = r[:, hh * HEAD_DIM:(hh + 1) * HEAD_DIM]
            else:
                o_ref[...] = r

    project(x_ref[...], out_refs[:n_out])

    @pl.when(pl.program_id(0) == 0)
    def _():
        project(xs_ref[...], out_refs[n_out:])


def norm_matmul(x, xs, g, w, tm, head_major=0):
    m, d = x.shape
    ns = xs.shape[0]
    n_out = w.shape[1] // COL_GROUP
    tm = min(tm, m)
    heads = COL_GROUP // HEAD_DIM
    n_flat = n_out - head_major
    specs = ([pl.BlockSpec((heads, tm, HEAD_DIM), lambda i: (0, i, 0))] * head_major
             + [pl.BlockSpec((tm, COL_GROUP), lambda i: (i, 0))] * n_flat
             + [pl.BlockSpec((heads, ns, HEAD_DIM), lambda i: (0, 0, 0))] * head_major
             + [pl.BlockSpec((ns, COL_GROUP), lambda i: (0, 0))] * n_flat)
    shapes = ([jax.ShapeDtypeStruct((heads, m, HEAD_DIM), F32)] * head_major
              + [jax.ShapeDtypeStruct((m, COL_GROUP), F32)] * n_flat
              + [jax.ShapeDtypeStruct((heads, ns, HEAD_DIM), F32)] * head_major
              + [jax.ShapeDtypeStruct((ns, COL_GROUP), F32)] * n_flat)
    outs = pl.pallas_call(
        functools.partial(_norm_matmul_kernel, head_major=head_major),
        grid=(m // tm,),
        in_specs=[pl.BlockSpec((tm, d), lambda i: (i, 0)), _resident((ns, d)), _resident((1, d)),
                  _resident(w.shape)],
        out_specs=specs,
        out_shape=shapes,
        compiler_params=_params("arbitrary"),
        name="norm_matmul",
    )(x, xs, g.reshape(1, d), w)
    return outs[:n_out], outs[n_out:]


def _ffn_kernel(g_ref, fg_ref, xs_ref, wg_ref, wu_ref, wd_ref, x_hbm, o_ref, os_ref, xbuf, h_ref, sem, *, final_norm):
    i, f = pl.program_id(0), pl.program_id(1)
    tm = o_ref.shape[0]
    ns = xs_ref.shape[0]
    last_f = pl.num_programs(1) - 1

    def x_copy(tile):
        return pltpu.make_async_copy(x_hbm.at[pl.ds(pl.multiple_of(tile * tm, tm), tm), :], xbuf, sem)

    @pl.when((i == 0) & (f == 0))
    def _():
        x_copy(0).start()
        xs = xs_ref[...]
        h_ref[tm:, :] = jnp.zeros((h_ref.shape[0] - tm, h_ref.shape[1]), h_ref.dtype)
        h_ref[tm:tm + ns, :] = _rms(xs, g_ref[...]).astype(h_ref.dtype)
        os_ref[...] = xs

    @pl.when(f == 0)
    def _():
        x_copy(i).wait()
        x = xbuf[...]
        h_ref[0:tm, :] = _rms(x, g_ref[...]).astype(h_ref.dtype)
        o_ref[...] = x

    @pl.when((f == 1) & (i + 1 < pl.num_programs(0)))
    def _():
        x_copy(i + 1).start()

    h = h_ref[...]
    gate = jnp.dot(h, wg_ref[...].astype(BF16), preferred_element_type=F32)
    up = jnp.dot(h, wu_ref[...].astype(BF16), preferred_element_type=F32)
    act = (gate * _sigmoid(gate) * up).astype(BF16)
    down = jnp.dot(act, wd_ref[...].astype(BF16), preferred_element_type=F32)
    o_ref[...] += down[0:tm]

    @pl.when(i == 0)
    def _():
        os_ref[...] += down[tm:tm + ns]

    if final_norm:
        @pl.when(f == last_f)
        def _():
            o_ref[...] = _rms(o_ref[...], fg_ref[...])

        @pl.when((i == 0) & (f == last_f))
        def _():
            os_ref[...] = _rms(os_ref[...], fg_ref[...])


BF16_ROWS = 16


def ffn(x, xs, g, wg, wu, wd, layer, final_g, tm, tf, final_norm):
    m, d = x.shape
    ns = xs.shape[0]
    dff = wg.shape[2]
    tm = min(tm, m)
    assert m % tm == 0 and dff % tf == 0 and dff // tf >= 2 and ns <= BF16_ROWS and tm % BF16_ROWS == 0
    return pl.pallas_call(
        functools.partial(_ffn_kernel, final_norm=final_norm),
        grid=(m // tm, dff // tf),
        in_specs=[_resident((1, d)), _resident((1, d)), _resident((ns, d)),
                  pl.BlockSpec((None, d, tf), lambda i, f: (layer, 0, f)),
                  pl.BlockSpec((None, d, tf), lambda i, f: (layer, 0, f)),
                  pl.BlockSpec((None, tf, d), lambda i, f: (layer, f, 0)),
                  pl.BlockSpec(memory_space=pl.ANY)],
        out_specs=[pl.BlockSpec((tm, d), lambda i, f: (i, 0)), pl.BlockSpec((ns, d), lambda i, f: (0, 0))],
        out_shape=[jax.ShapeDtypeStruct((m, d), F32), jax.ShapeDtypeStruct((ns, d), F32)],
        scratch_shapes=[pltpu.VMEM((tm, d), F32), pltpu.VMEM((tm + BF16_ROWS, d), BF16),
                        pltpu.SemaphoreType.DMA(())],
        compiler_params=_params("arbitrary", "arbitrary"),
        name="ffn",
    )(g.reshape(1, d), final_g.reshape(1, d), xs, wg, wu, wd, x)


A_HALO = 32
B_HALO = 8


def _layer_norm_silu(x, g, b):
    mu = jnp.mean(x, axis=-1, keepdims=True)
    xc = x - mu
    y = xc * lax.rsqrt(jnp.mean(xc * xc, axis=-1, keepdims=True) + EPS) * g + b
    return y * _sigmoid(y)


def _conv_mix_kernel(av_ref, ag_ref, bh_ref, bb_ref, bc_ref, x_ref, adw_ref, adwb_ref, lng_ref, lnb_ref,
                     bdw_ref, wo_ref, y_ref, newa_ref, newb_ref, aext, cext, ashift, aconv, *, ta, tb, rc):
    l = pl.program_id(1)
    tl = av_ref.shape[1]
    wa = av_ref.shape[2]

    @pl.when(l == 0)
    def _():
        aext[0:A_HALO, :] = jnp.zeros((A_HALO, wa), F32)
        cext[0:B_HALO, :] = jnp.zeros((B_HALO, wa), F32)

    @pl.when(l > 0)
    def _():
        aext[0:A_HALO, :] = aext[tl:tl + A_HALO, :]
        cext[0:B_HALO, :] = cext[tl:tl + B_HALO, :]

    aext[A_HALO:A_HALO + tl, :] = av_ref[0] * _sigmoid(ag_ref[0])
    cext[B_HALO:B_HALO + tl, :] = bc_ref[0] * bh_ref[0]

    bconv = jnp.zeros((tl, wa), F32)
    for k in range(tb):
        off = B_HALO - (tb - 1) + k
        bconv = bconv + bdw_ref[k:k + 1, :] * cext[off:off + tl, :]
    b_out = (bb_ref[0] * bconv).astype(BF16)
    y_ref[0] = x_ref[0] + jnp.dot(b_out, wo_ref[wa:, :], preferred_element_type=F32)

    n_shift = ashift.shape[1]
    for s in range(1, SUBLANES):
        ashift[s - 1] = aext[s:s + n_shift, :]

    for r0 in range(0, tl, rc):
        for c0 in range(0, wa, LANES):
            acc = jnp.broadcast_to(adwb_ref[:, c0:c0 + LANES], (rc, LANES))
            for k in range(ta):
                off = A_HALO - (ta - 1) + r0 + k
                s = off % SUBLANES
                src = aext if s == 0 else ashift.at[s - 1]
                acc = acc + adw_ref[k:k + 1, c0:c0 + LANES] * src[off - s:off - s + rc, c0:c0 + LANES]
            aconv[r0:r0 + rc, c0:c0 + LANES] = acc
    a_out = _layer_norm_silu(aconv[...], lng_ref[...], lnb_ref[...]).astype(BF16)
    y_ref[0] += jnp.dot(a_out, wo_ref[0:wa, :], preferred_element_type=F32)

    @pl.when(l == pl.num_programs(1) - 1)
    def _():
        newa_ref[0] = aext[A_HALO + tl - (ta - 1):A_HALO + tl, :]
        newb_ref[0] = cext[B_HALO + tl - (tb - 1):B_HALO + tl, :]


def conv_mix(parts, x, a_dw, a_dw_b, ln_g, ln_b, b_dw, w_out, tl):
    bsz, seq, d = x.shape
    wa = parts[0].shape[-1]
    ta, tb = a_dw.shape[0], b_dw.shape[0]
    tl = min(tl, seq)
    assert seq % tl == 0 and tl >= A_HALO and ta - 1 <= A_HALO and tb - 1 <= B_HALO
    part_spec = pl.BlockSpec((1, tl, wa), lambda b, l: (b, l, 0))
    const = _resident
    return pl.pallas_call(
        functools.partial(_conv_mix_kernel, ta=ta, tb=tb, rc=min(128, tl)),
        grid=(bsz, seq // tl),
        in_specs=[part_spec] * 5 + [pl.BlockSpec((1, tl, d), lambda b, l: (b, l, 0)),
                                    const((ta, wa)), const((1, wa)), const((1, wa)), const((1, wa)),
                                    const((tb, wa)), const(w_out.shape)],
        out_specs=[pl.BlockSpec((1, tl, d), lambda b, l: (b, l, 0)),
                   pl.BlockSpec((1, ta - 1, wa), lambda b, l: (b, 0, 0)),
                   pl.BlockSpec((1, tb - 1, wa), lambda b, l: (b, 0, 0))],
        out_shape=[jax.ShapeDtypeStruct((bsz, seq, d), F32),
                   jax.ShapeDtypeStruct((bsz, ta - 1, wa), F32),
                   jax.ShapeDtypeStruct((bsz, tb - 1, wa), F32)],
        scratch_shapes=[pltpu.VMEM((A_HALO + tl, wa), F32), pltpu.VMEM((B_HALO + tl, wa), F32),
                        pltpu.VMEM((SUBLANES - 1, A_HALO + tl - SUBLANES, wa), F32),
                        pltpu.VMEM((tl, wa), F32)],
        compiler_params=_params("parallel", "arbitrary"),
        name="conv_mix",
    )(*parts, x, a_dw, a_dw_b.reshape(1, wa), ln_g.reshape(1, wa), ln_b.reshape(1, wa), b_dw, w_out)


def _conv_step_kernel(av_ref, ag_ref, bh_ref, bb_ref, bc_ref, x_ref, sta_ref, stb_ref, adw_ref, adwb_ref,
                      lng_ref, lnb_ref, bdw_ref, wo_ref, y_ref, anew_ref, cnew_ref, *, ta, tb):
    a = av_ref[...] * _sigmoid(ag_ref[...])
    c = bc_ref[...] * bh_ref[...]
    anew_ref[...] = a
    cnew_ref[...] = c
    acc = adwb_ref[...] + adw_ref[ta - 1:ta, :] * a
    for k in range(ta - 1):
        acc = acc + adw_ref[k:k + 1, :] * sta_ref[k]
    a_out = _layer_norm_silu(acc, lng_ref[...], lnb_ref[...])
    bconv = bdw_ref[tb - 1:tb, :] * c
    for k in range(tb - 1):
        bconv = bconv + bdw_ref[k:k + 1, :] * stb_ref[k]
    mix = jnp.concatenate([a_out, bb_ref[...] * bconv], axis=-1).astype(BF16)
    y_ref[...] = x_ref[...] + jnp.dot(mix, wo_ref[...], preferred_element_type=F32)


def conv_step(parts, x, st_a, st_b, a_dw, a_dw_b, ln_g, ln_b, b_dw, w_out):
    bd, d = x.shape
    wa = parts[0].shape[-1]
    ta, tb = a_dw.shape[0], b_dw.shape[0]
    return pl.pallas_call(
        functools.partial(_conv_step_kernel, ta=ta, tb=tb),
        out_shape=[jax.ShapeDtypeStruct((bd, d), F32), jax.ShapeDtypeStruct((bd, wa), F32),
                   jax.ShapeDtypeStruct((bd, wa), F32)],
        compiler_params=pltpu.CompilerParams(vmem_limit_bytes=VMEM_LIMIT),
        name="conv_step",
    )(*parts, x, st_a, st_b, a_dw, a_dw_b.reshape(1, wa), ln_g.reshape(1, wa), ln_b.reshape(1, wa), b_dw, w_out)


LOG2E = 1.4426950408889634
TQ = 2 * MOBA_BLOCK
DA = 2 * HEAD_DIM


def _split3(x):
    def top_bits(v):
        return lax.bitcast_convert_type(lax.bitcast_convert_type(v, jnp.uint32) & jnp.uint32(0xFFFF0000), F32)

    hi = top_bits(x)
    mid = top_bits(x - hi)
    return hi, mid, x - hi - mid


def _moba_setup(q_ref, k_ref, v_ref, qaug, kaug, vaug, kmean, slope2, nb):
    seq = k_ref.shape[1]
    bs = MOBA_BLOCK
    nbp = kmean.shape[0]
    k = k_ref[0]
    q = q_ref[0]
    kmean[...] = jnp.zeros_like(kmean)
    kmean[0:nb, :] = jnp.mean(k.reshape(nb, bs, HEAD_DIM), axis=1)

    gate = lax.dot_general(kmean[...], q, (((1,), (1,)), ((), ())), precision=lax.Precision.HIGHEST,
                           preferred_element_type=F32)
    blk = lax.broadcasted_iota(jnp.int32, (nbp, seq), 0)
    own = lax.broadcasted_iota(jnp.int32, (nbp, seq), 1) // bs
    gate = jnp.where(blk < own, gate, -jnp.inf)
    attend = blk == own
    for _ in range(MOBA_TOPK):
        m = jnp.max(gate, axis=0, keepdims=True)
        idx = jnp.min(jnp.where(gate == m, blk, nbp), axis=0, keepdims=True)
        hit = blk == idx
        attend = attend | (hit & (blk < own))
        gate = jnp.where(hit, -jnp.inf, gate)
    attend_t = jnp.where(attend, 1.0, 0.0)
    attend_r = jnp.concatenate([attend_t, jnp.zeros((LANES - nbp, seq), F32)], axis=0).T

    lane = lax.broadcasted_iota(jnp.int32, (seq, LANES), 1)
    pos_i = lax.broadcasted_iota(jnp.int32, (seq, LANES), 0)
    pos = pos_i.astype(F32)

    khi, kmid, klo = _split3(slope2 * pos)
    ek = jnp.where(lane == pos_i // bs, 1.0, 0.0)
    ek = jnp.where(lane == nb, khi, ek)
    ek = jnp.where(lane == nb + 1, kmid, ek)
    ek = jnp.where(lane == nb + 2, klo, ek)
    ek = jnp.where((lane >= nb + 3) & (lane < nb + 6), 1.0, ek)
    kaug[:, 0:HEAD_DIM] = k.astype(BF16)
    kaug[:, HEAD_DIM:] = ek.astype(BF16)

    qhi, qmid, qlo = _split3(-slope2 * pos)
    eq = jnp.where(lane < nb, jnp.where(attend_r > 0.5, 0.0, MASKED), 0.0)
    eq = jnp.where((lane >= nb) & (lane < nb + 3), 1.0, eq)
    eq = jnp.where(lane == nb + 3, qhi, eq)
    eq = jnp.where(lane == nb + 4, qmid, eq)
    eq = jnp.where(lane == nb + 5, qlo, eq)
    qaug[:, 0:HEAD_DIM] = (q * (HEAD_DIM ** -0.5 * LOG2E)).astype(BF16)
    qaug[:, HEAD_DIM:] = eq.astype(BF16)

    vaug[:, 0:HEAD_DIM] = v_ref[0].astype(BF16)
    vaug[:, HEAD_DIM:] = jnp.ones((seq, HEAD_DIM), BF16)


SUM_CHAINS = 16
PAGE_LOOKAHEAD = 3


class _PageSums:
    def __init__(self, pt_ref, kc_hbm, ksum_ref, ring, sem, *, layer, pages_per_block, n_steps):
        self.pt_ref, self.kc_hbm, self.ksum_ref, self.ring, self.sem = pt_ref, kc_hbm, ksum_ref, ring, sem
        self.layer, self.pages_per_block, self.n_steps = layer, pages_per_block, n_steps
        self.total = pt_ref.shape[0]
        self.depth = ring.shape[0]
        self.per = self.depth // PAGE_LOOKAHEAD
        self.even = n_steps * self.per == self.total

    def _copy(self, t, slot):
        return pltpu.make_async_copy(self.kc_hbm.at[self.layer, self.pt_ref[t]], self.ring.at[slot],
                                     self.sem.at[slot])

    def begin(self):
        self.ksum_ref[...] = jnp.zeros_like(self.ksum_ref)
        for slot in range(min(self.depth, self.total)):
            self._copy(slot, slot).start()

    def _fold_page(self, t, slot):
        _, page_rows, n_heads, _ = self.ring.shape
        self._copy(t, slot).wait()
        page = self.ring[slot].reshape(SUM_CHAINS, page_rows // SUM_CHAINS, n_heads, HEAD_DIM)
        self.ksum_ref[t // self.pages_per_block] += jnp.sum(jnp.sum(page, axis=1), axis=0)

    def _first_slot(self, step):
        return (step % PAGE_LOOKAHEAD) * self.per

    def fold(self, step):
        assert self.even
        for j in range(self.per):
            self._fold_page(step * self.per + j, self._first_slot(step) + j)

    def refill(self, step):
        assert self.even

        @pl.when(step + PAGE_LOOKAHEAD < self.n_steps)
        def _():
            for j in range(self.per):
                self._copy(step * self.per + j + self.depth, self._first_slot(step) + j).start()

    def step(self, step):
        for j in range(self.per):
            t = step * self.per + j

            @pl.when(t < self.total)
            def _(t=t, j=j):
                self._fold_page(t, self._first_slot(step) + j)

                @pl.when(t + self.depth < self.total)
                def _():
                    self._copy(t + self.depth, self._first_slot(step) + j).start()


def _moba_kernel(pt_ref, q_ref, k_ref, v_ref, kc_hbm, o_ref, ksum_ref, qaug, kaug, vaug, kmean, m_ref, acc_ref,
                 s_ref, rel_ref, ring, psem, *, n_heads, nb, layer, page_pairs, page_steps):
    h = pl.program_id(1)
    bs = MOBA_BLOCK
    nt = (((1,), (1,)), ((), ()))
    seq = qaug.shape[0]
    n_tiles = seq // TQ
    n_units = n_tiles * (n_tiles + 1) // 2
    n_pairs = n_units // 2
    bh = pl.program_id(0) * pl.num_programs(1) + h

    slope2 = jnp.exp2((h + 1).astype(F32) * (-8.0 / n_heads) + jnp.zeros((1, 1), F32)) * LOG2E
    _moba_setup(q_ref, k_ref, v_ref, qaug, kaug, vaug, kmean, slope2, nb)
    rel_ref[...] = (lax.broadcasted_iota(jnp.int32, (bs, bs), 1) - lax.broadcasted_iota(jnp.int32, (bs, bs), 0))

    halves = (slice(0, bs), slice(bs, TQ))

    def tile_rows(t):
        return pl.ds(pl.multiple_of(t * TQ, TQ), TQ)

    def successor(unit):
        i, g = unit
        wrap = g == i
        return jnp.where(wrap, jnp.minimum(i + 1, n_tiles - 1), i), jnp.where(wrap, 0, g + 1)

    def scores(unit):
        i, g = unit
        qa = qaug[tile_rows(i), :]
        kg = kaug[tile_rows(g), :]
        return jnp.concatenate([lax.dot_general(qa[r], kg, nt, preferred_element_type=F32) for r in halves], axis=0)

    def process(unit, slot):
        i, g = unit
        vg = vaug[tile_rows(g), :]
        for r in halves:
            diag = jnp.where(rel_ref[...] <= (i - g) * TQ, s_ref[slot, r, r], MASKED)
            sr = jnp.concatenate([diag, s_ref[slot, r, bs:]] if r.start == 0 else [s_ref[slot, r, 0:bs], diag], axis=1)
            m = jnp.where(g == 0, MASKED, m_ref[r, :])
            m_new = jnp.maximum(m, jnp.max(sr, axis=1, keepdims=True))
            p = jnp.exp2(sr - m_new).astype(BF16)
            pv = jnp.dot(p, vg, preferred_element_type=F32)
            acc = jnp.exp2(m - m_new) * acc_ref[r, :] + pv
            acc_ref[r, :] = acc
            m_ref[r, :] = m_new
            rows = pl.ds(pl.multiple_of(i * TQ + r.start, bs), bs)
            o_ref[0, rows, :] = acc[:, 0:HEAD_DIM] / acc[:, HEAD_DIM:]

    m_ref[...] = jnp.full(m_ref.shape, MASKED, F32)
    acc_ref[...] = jnp.zeros_like(acc_ref)
    first = (jnp.int32(0), jnp.int32(0))
    s_ref[0] = scores(first)

    pages = _PageSums(pt_ref, kc_hbm, ksum_ref, ring, psem, layer=layer, pages_per_block=bs // PAGE_SIZE,
                      n_steps=page_steps)

    @pl.when(bh == 0)
    def _():
        pages.begin()

    def pair(p, unit, with_pages):
        page_step = bh * page_pairs + p
        if with_pages and pages.even:
            pages.fold(page_step)
        elif with_pages:
            pages.step(page_step)
        nxt = successor(unit)
        s_ref[1] = scores(nxt)
        process(unit, 0)
        nxt2 = successor(nxt)
        s_ref[0] = scores(nxt2)
        process(nxt, 1)
        if with_pages and pages.even:
            pages.refill(page_step)
        return nxt2

    unit = lax.fori_loop(0, page_pairs, functools.partial(pair, with_pages=True), first)
    if page_pairs < n_pairs:
        unit = lax.fori_loop(page_pairs, n_pairs, functools.partial(pair, with_pages=False), unit)
    if n_units % 2:
        process(unit, 0)


def moba_prompt(q, k, v, cache_k, layer, page_table):
    n_heads, bsz, seq, _ = q.shape
    width = n_heads * HEAD_DIM
    assert seq % TQ == 0
    nb = seq // MOBA_BLOCK
    nbp = -(-nb // 8) * 8
    assert nb + 6 <= LANES
    bd, n_pages = page_table.shape
    pages_per_block = MOBA_BLOCK // PAGE_SIZE
    assert n_pages % pages_per_block == 0
    n_tiles = seq // TQ
    n_pairs = n_tiles * (n_tiles + 1) // 4
    assert n_pairs >= 1
    total_pages = bd * n_pages
    even = [p for p in range(1, n_pairs + 1) if total_pages % (bsz * n_heads * p) == 0]
    page_pairs = max(even) if even else n_pairs
    page_steps = bsz * n_heads * page_pairs
    pages_per_step = -(-total_pages // page_steps)
    sums_shape = (total_pages // pages_per_block, n_heads, HEAD_DIM)
    slab = pl.BlockSpec((None, 1, seq, HEAD_DIM), lambda b, h: (h, b, 0, 0))
    return pl.pallas_call(
        functools.partial(_moba_kernel, n_heads=n_heads, nb=nb, layer=layer, page_pairs=page_pairs,
                          page_steps=page_steps),
        grid=(bsz, n_heads),
        in_specs=[pl.BlockSpec(memory_space=pltpu.SMEM), slab, slab, slab, pl.BlockSpec(memory_space=pl.ANY)],
        out_specs=[pl.BlockSpec((1, seq, HEAD_DIM), lambda b, h: (b, 0, h)),
                   pl.BlockSpec(sums_shape, lambda b, h: (0, 0, 0))],
        out_shape=[jax.ShapeDtypeStruct((bsz, seq, width), F32), jax.ShapeDtypeStruct(sums_shape, F32)],
        scratch_shapes=[pltpu.VMEM((seq, DA), BF16), pltpu.VMEM((seq, DA), BF16), pltpu.VMEM((seq, DA), BF16),
                        pltpu.VMEM((nbp, HEAD_DIM), F32), pltpu.VMEM((TQ, 1), F32), pltpu.VMEM((TQ, DA), F32),
                        pltpu.VMEM((2, TQ, TQ), F32), pltpu.VMEM((MOBA_BLOCK, MOBA_BLOCK), jnp.int32),
                        pltpu.VMEM((PAGE_LOOKAHEAD * pages_per_step, PAGE_SIZE, n_heads, HEAD_DIM), F32),
                        pltpu.SemaphoreType.DMA((PAGE_LOOKAHEAD * pages_per_step,))],
        compiler_params=_params("arbitrary", "arbitrary"),
        name="moba_prompt",
    )(page_table.reshape(-1), q, k, v, cache_k)


PAGES_IN_FLIGHT = 12


def _gate_step_kernel(q_ref, ksum_ref, sel_ref, *, n_heads):
    bd = q_ref.shape[0]
    nblk = ksum_ref.shape[0] // bd
    sel_ref[...] = jnp.zeros_like(sel_ref)
    for b in range(bd):
        kmean = ksum_ref[b * nblk:(b + 1) * nblk] * (1.0 / MOBA_BLOCK)
        gate = jnp.sum(kmean * q_ref[b][None], axis=-1)
        blk = lax.broadcasted_iota(jnp.int32, gate.shape, 0)
        for r in range(MOBA_TOPK):
            m = jnp.max(gate, axis=0, keepdims=True)
            idx = jnp.min(jnp.where(gate == m, blk, nblk), axis=0, keepdims=True)
            sel_ref[b, r:r + 1, 0:n_heads] = idx
            gate = jnp.where(blk == idx, -jnp.inf, gate)


SEL_ROWS = 8


def gate_step(block_sums, q):
    bd, n_heads, _ = q.shape
    assert MOBA_TOPK <= SEL_ROWS and n_heads <= LANES
    sel = pl.pallas_call(
        functools.partial(_gate_step_kernel, n_heads=n_heads),
        out_shape=jax.ShapeDtypeStruct((bd, SEL_ROWS, LANES), jnp.int32),
        compiler_params=pltpu.CompilerParams(vmem_limit_bytes=VMEM_LIMIT),
        name="gate_step",
    )(q, block_sums)
    return sel[:, :MOBA_TOPK, :n_heads]


def _attn_step_kernel(pt_ref, sel_ref, q_ref, kn_ref, vn_ref, k_hbm, v_hbm, o_ref, kbuf, vbuf, ksem, vsem,
                      m_ref, l_ref, acc_ref, *, layer, n_pages, q_pos, pages_per_block):
    bd, n_heads, _ = q_ref.shape
    per_head = MOBA_TOPK * pages_per_block
    total = bd * n_heads * per_head
    depth = kbuf.shape[0]
    scale = HEAD_DIM ** -0.5
    own = q_pos // MOBA_BLOCK

    def coords(t):
        bh, j = t // per_head, t % per_head
        b, h = bh // n_heads, bh % n_heads
        return b, h, j, sel_ref[(b * MOBA_TOPK + j // pages_per_block) * n_heads + h]

    def page_copies(t, slot):
        b, _, j, n = coords(t)
        page = pt_ref[b * n_pages + n * pages_per_block + j % pages_per_block]
        return (pltpu.make_async_copy(k_hbm.at[layer, page], kbuf.at[slot], ksem.at[slot]),
                pltpu.make_async_copy(v_hbm.at[layer, page], vbuf.at[slot], vsem.at[slot]))

    for t in range(min(depth, total)):
        for c in page_copies(t, t):
            c.start()

    @pl.loop(0, total)
    def _(t):
        slot = t % depth
        b, h, j, n = coords(t)
        for c in page_copies(t, slot):
            c.wait()
        q = q_ref[b, pl.ds(h, 1), :]

        @pl.when(j == 0)
        def _():
            m_ref[...] = jnp.sum(q * kn_ref[b, pl.ds(h, 1), :], axis=1, keepdims=True) * scale
            l_ref[...] = jnp.ones_like(l_ref)
            acc_ref[...] = vn_ref[b, pl.ds(h, 1), :]

        k = kbuf[slot, pl.ds(h, PAGE_SIZE, stride=n_heads), :]
        v = vbuf[slot, pl.ds(h, PAGE_SIZE, stride=n_heads), :]
        slope = jnp.exp2(jnp.asarray(h + 1, F32) * (-8.0 / n_heads) + jnp.zeros((1, 1), F32))
        kpos = (n * MOBA_BLOCK + (j % pages_per_block) * PAGE_SIZE
                + lax.broadcasted_iota(jnp.int32, (PAGE_SIZE, 1), 0))
        dist = (q_pos - kpos).astype(F32)
        s = jnp.sum(k * q, axis=1, keepdims=True) * scale - slope * dist
        s = jnp.where((dist >= 0.0) & (n < own), s, MASKED)
        m = m_ref[...]
        m_new = jnp.maximum(m, jnp.max(s, axis=0, keepdims=True))
        alpha = jnp.exp(m - m_new)
        p = jnp.exp(s - m_new)
        l_ref[...] = alpha * l_ref[...] + jnp.sum(p, axis=0, keepdims=True)
        acc_ref[...] = alpha * acc_ref[...] + jnp.sum(p * v, axis=0, keepdims=True)
        m_ref[...] = m_new

        @pl.when(j == per_head - 1)
        def _():
            o_ref[b, pl.ds(h, 1), :] = acc_ref[...] / l_ref[...]

        @pl.when(t + depth < total)
        def _():
            for c in page_copies(t + depth, slot):
                c.start()


def attn_step(cache_k, cache_v, layer, page_table, sel, q, k_new, v_new):
    bd, n_pages = page_table.shape
    n_heads = q.shape[1]
    pages_per_block = MOBA_BLOCK // PAGE_SIZE
    assert n_pages % pages_per_block == 0
    rows = PAGE_SIZE * n_heads
    as_rows = lambda c: c.reshape(c.shape[0], c.shape[1], rows, HEAD_DIM)
    vmem, smem = pl.BlockSpec(memory_space=pltpu.VMEM), pl.BlockSpec(memory_space=pltpu.SMEM)
    hbm = pl.BlockSpec(memory_space=pl.ANY)
    return pl.pallas_call(
        functools.partial(_attn_step_kernel, layer=layer, n_pages=n_pages, q_pos=n_pages * PAGE_SIZE,
                          pages_per_block=pages_per_block),
        in_specs=[smem, smem, vmem, vmem, vmem, hbm, hbm],
        out_specs=vmem,
        out_shape=jax.ShapeDtypeStruct((bd, n_heads, HEAD_DIM), F32),
        scratch_shapes=[pltpu.VMEM((PAGES_IN_FLIGHT, rows, HEAD_DIM), F32),
                        pltpu.VMEM((PAGES_IN_FLIGHT, rows, HEAD_DIM), F32),
                        pltpu.SemaphoreType.DMA((PAGES_IN_FLIGHT,)), pltpu.SemaphoreType.DMA((PAGES_IN_FLIGHT,)),
                        pltpu.VMEM((1, 1), F32), pltpu.VMEM((1, 1), F32), pltpu.VMEM((1, HEAD_DIM), F32)],
        compiler_params=pltpu.CompilerParams(vmem_limit_bytes=VMEM_LIMIT),
        name="attn_step",
    )(page_table.reshape(-1), sel.reshape(-1), q, k_new, v_new, as_rows(cache_k), as_rows(cache_v))


P_HALO = 16


def _pool_out_kernel(o_ref, u_ref, x_ref, pw_ref, ps_ref, wo_ref, y_ref, newp_ref, uext, mix):
    l = pl.program_id(1)
    tl = u_ref.shape[1]
    wp = u_ref.shape[2]
    wo_attn = o_ref.shape[2]
    pg = wp // len(POOL_WINDOWS)

    @pl.when(l == 0)
    def _():
        uext[0:P_HALO, :] = jnp.zeros((P_HALO, wp), F32)

    @pl.when(l > 0)
    def _():
        uext[0:P_HALO, :] = uext[tl:tl + P_HALO, :]

    uext[P_HALO:P_HALO + tl, :] = u_ref[0]
    o_bf16 = o_ref[0].astype(BF16)
    n_chunk = y_ref.shape[2] // len(POOL_WINDOWS)
    pos = l * tl + lax.broadcasted_iota(jnp.int32, (tl, pg), 0)
    for g, w in enumerate(POOL_WINDOWS):
        n0 = g * n_chunk
        y_ref[0, :, n0:n0 + n_chunk] = x_ref[0, :, n0:n0 + n_chunk] + jnp.dot(
            o_bf16, wo_ref[0:wo_attn, n0:n0 + n_chunk], preferred_element_type=F32)
        c0 = g * pg
        wsum = uext[P_HALO:P_HALO + tl, c0:c0 + pg]
        for j in range(1, w):
            wsum = wsum + uext[P_HALO - j:P_HALO - j + tl, c0:c0 + pg]
        count = jnp.minimum(pos + 1, w).astype(F32)
        dlt = (wsum / count - uext[P_HALO:P_HALO + tl, c0:c0 + pg]).astype(BF16)
        yp = jnp.dot(dlt, pw_ref[g], preferred_element_type=F32) * ps_ref[:, c0:c0 + pg]
        mix[:, c0:c0 + pg] = yp.astype(BF16)

    y_ref[0] += jnp.dot(mix[...], wo_ref[wo_attn:, :], preferred_element_type=F32)

    @pl.when(l == pl.num_programs(1) - 1)
    def _():
        newp_ref[0] = uext[P_HALO + tl - (POOL_MAX - 1):P_HALO + tl, :]


def pool_out(o, u, x, pool_w, pool_scale, w_out, tl):
    bsz, seq, d = x.shape
    wp = u.shape[-1]
    tl = min(tl, seq)
    assert seq % tl == 0 and tl >= P_HALO
    const = _resident
    return pl.pallas_call(
        _pool_out_kernel,
        grid=(bsz, seq // tl),
        in_specs=[pl.BlockSpec((1, tl, o.shape[-1]), lambda b, l: (b, l, 0)),
                  pl.BlockSpec((1, tl, wp), lambda b, l: (b, l, 0)),
                  pl.BlockSpec((1, tl, d), lambda b, l: (b, l, 0)),
                  const(pool_w.shape), const((1, wp)), const(w_out.shape)],
        out_specs=[pl.BlockSpec((1, tl, d), lambda b, l: (b, l, 0)),
                   pl.BlockSpec((1, POOL_MAX - 1, wp), lambda b, l: (b, 0, 0))],
        out_shape=[jax.ShapeDtypeStruct((bsz, seq, d), F32),
                   jax.ShapeDtypeStruct((bsz, POOL_MAX - 1, wp), F32)],
        scratch_shapes=[pltpu.VMEM((P_HALO + tl, wp), F32), pltpu.VMEM((tl, wp), BF16)],
        compiler_params=_params("parallel", "arbitrary"),
        name="pool_out",
    )(o, u, x, pool_w, pool_scale.reshape(1, wp), w_out)


def _pool_step_kernel(o_ref, u_ref, x_ref, stp_ref, pw_ref, ps_ref, wo_ref, y_ref, *, start_pos):
    wp = u_ref.shape[1]
    pg = wp // len(POOL_WINDOWS)
    u = u_ref[...]
    parts = [o_ref[...].astype(BF16)]
    for g, w in enumerate(POOL_WINDOWS):
        c0 = g * pg
        wsum = u[:, c0:c0 + pg]
        for j in range(1, w):
            wsum = wsum + stp_ref[POOL_MAX - 1 - j][:, c0:c0 + pg]
        count = float(min(start_pos + 1, w))
        dlt = (wsum / count - u[:, c0:c0 + pg]).astype(BF16)
        yp = jnp.dot(dlt, pw_ref[g], preferred_element_type=F32) * ps_ref[:, c0:c0 + pg]
        parts.append(yp.astype(BF16))
    mix = jnp.concatenate(parts, axis=-1)
    y_ref[...] = x_ref[...] + jnp.dot(mix, wo_ref[...], preferred_element_type=F32)


def pool_step(o, u, x, st_p, pool_w, pool_scale, w_out, start_pos):
    bd, d = x.shape
    return pl.pallas_call(
        functools.partial(_pool_step_kernel, start_pos=start_pos),
        out_shape=jax.ShapeDtypeStruct((bd, d), F32),
        compiler_params=pltpu.CompilerParams(vmem_limit_bytes=VMEM_LIMIT),
        name="pool_step",
    )(o, u, x, st_p, pool_w, pool_scale.reshape(1, -1), w_out)


TM_PROMPT = 512
TM_FFN = 1024
TF = 512
TL_MIX = 256


def kernel(x_prompt, x_sample, state_conv_a, state_conv_b, cache_k, cache_v, state_pool, page_table, conv_norm_g, conv_w_in, conv_a_dw, conv_a_dw_b, conv_a_ln_g, conv_a_ln_b, conv_b_dw, conv_w_out, attn_norm_g, attn_w_in, pool_w, pool_scale, attn_w_out, ffn_norm_g, ffn_w_gate, ffn_w_up, ffn_w_down, final_norm_g):
    bsz, seq, d = x_prompt.shape
    bd, dec_seq, _ = x_sample.shape
    assert dec_seq == 1
    depth = ffn_norm_g.shape[0]
    n_heads = cache_k.shape[3]
    wattn = n_heads * HEAD_DIM
    past_len = page_table.shape[1] * PAGE_SIZE

    layer_bf16 = lambda w, n: w[n].astype(BF16)

    xp = x_prompt.reshape(bsz * seq, d)
    xs = x_sample.reshape(bd, d)
    pa, pb, pk, pv, pp = [], [], [], [], []
    sa, sb, sk, sv, sp = [], [], [], [], []
    for layer in range(depth):
        i = layer // 2
        if layer % 2 == 0:
            w_in, w_out = layer_bf16(conv_w_in, i), layer_bf16(conv_w_out, i)
            parts, parts_s = norm_matmul(xp, xs, conv_norm_g[i], w_in, TM_PROMPT)
            wa = parts[0].shape[-1]
            y, na, nb_ = conv_mix([t.reshape(bsz, seq, wa) for t in parts], xp.reshape(bsz, seq, d),
                                  conv_a_dw[i], conv_a_dw_b[i], conv_a_ln_g[i], conv_a_ln_b[i], conv_b_dw[i],
                                  w_out, TL_MIX)
            xp = y.reshape(bsz * seq, d)
            pa.append(na)
            pb.append(nb_)

            st_a = jnp.swapaxes(state_conv_a[i], 0, 1)
            st_b = jnp.swapaxes(state_conv_b[i], 0, 1)
            xs, a_new, c_new = conv_step(parts_s, xs, st_a, st_b, conv_a_dw[i], conv_a_dw_b[i], conv_a_ln_g[i],
                                         conv_a_ln_b[i], conv_b_dw[i], w_out)
            sa.append(jnp.concatenate([state_conv_a[i][:, 1:], a_new[:, None]], axis=1))
            sb.append(jnp.concatenate([state_conv_b[i][:, 1:], c_new[:, None]], axis=1))
        else:
            w_in, w_out, w_pool = layer_bf16(attn_w_in, i), layer_bf16(attn_w_out, i), layer_bf16(pool_w, i)
            (q, k, v, u), proj_s = norm_matmul(xp, xs, attn_norm_g[i], w_in, TM_PROMPT, head_major=3)
            shp = (n_heads, bsz, seq, HEAD_DIM)
            o, block_sums = moba_prompt(q.reshape(shp), k.reshape(shp), v.reshape(shp), cache_k, i, page_table)
            y, npool = pool_out(o, u.reshape(bsz, seq, -1), xp.reshape(bsz, seq, d), w_pool, pool_scale[i],
                                w_out, TL_MIX)
            xp = y.reshape(bsz * seq, d)
            pk.append(jnp.transpose(k.reshape(shp), (1, 2, 0, 3)))
            pv.append(jnp.transpose(v.reshape(shp), (1, 2, 0, 3)))
            pp.append(npool)

            q, k, v, u = proj_s
            heads = lambda t: jnp.swapaxes(t, 0, 1)
            sel = gate_step(block_sums, heads(q))
            o = attn_step(cache_k, cache_v, i, page_table, sel, heads(q), heads(k), heads(v))
            st_p = jnp.swapaxes(state_pool[i], 0, 1)
            xs = pool_step(o.reshape(bd, wattn), u, xs, st_p, w_pool, pool_scale[i], w_out, past_len)
            sk.append(heads(k).reshape(bd, 1, n_heads, HEAD_DIM))
            sv.append(heads(v).reshape(bd, 1, n_heads, HEAD_DIM))
            sp.append(jnp.concatenate([state_pool[i][:, 1:], u[:, None]], axis=1))
        last = layer == depth - 1
        ffn_w = (ffn_w_gate, ffn_w_up, ffn_w_down, layer)
        xp, xs = ffn(xp, xs, ffn_norm_g[layer], *ffn_w, final_norm_g, TM_FFN, TF, last)
    return (xp.reshape(bsz, seq, d), xs.reshape(bd, 1, d), jnp.stack(pa), jnp.stack(sa), jnp.stack(pb),
            jnp.stack(sb), jnp.stack(pk), jnp.stack(sk), jnp.stack(pv), jnp.stack(sv), jnp.stack(pp), jnp.stack(sp))
```

```python
import functools

import jax
import jax.numpy as jnp
from jax import lax
from jax.experimental import pallas as pl
from jax.experimental.pallas import tpu as pltpu

EPS = 1e-6
PAGE_SIZE = 128
MOBA_BLOCK = 256
MOBA_TOPK = 3
HEAD_DIM = 128
POOL_WINDOWS = (2, 4, 8, 16)
POOL_MAX = max(POOL_WINDOWS)
MASKED = -1e30

V7X_VMEM_BYTES = 64 * 1024 * 1024
VMEM_LIMIT = V7X_VMEM_BYTES - 4 * 1024 * 1024
LANES = 128
SUBLANES = 8
COL_GROUP = 1024

BF16 = jnp.bfloat16
F32 = jnp.float32


def _params(*sem):
    return pltpu.CompilerParams(dimension_semantics=sem, vmem_limit_bytes=VMEM_LIMIT)


def _rms(x, g):
    ms = jnp.mean(x * x, axis=-1, keepdims=True)
    return x * lax.rsqrt(ms + EPS) * g


def _sigmoid(x):
    return 1.0 / (1.0 + jnp.exp(-x))


def _resident(shape):
    return pl.BlockSpec(shape, lambda *_: (0,) * len(shape), pipeline_mode=pl.Buffered(1))


def _norm_matmul_kernel(x_ref, xs_ref, g_ref, w_ref, *out_refs, head_major):
    n_out = len(out_refs) // 2

    def project(t, refs):
        h = _rms(t, g_ref[...]).astype(BF16)
        for c, o_ref in enumerate(refs):
            r = jnp.dot(h, w_ref[:, c * COL_GROUP:(c + 1) * COL_GROUP], preferred_element_type=F32)
            if c < head_major:
                for hh in range(COL_GROUP // HEAD_DIM):
                    o_ref[hh] = r[:, hh * HEAD_DIM:(hh + 1) * HEAD_DIM]
            else:
                o_ref[...] = r

    project(x_ref[...], out_refs[:n_out])

    @pl.when(pl.program_id(0) == 0)
    def _():
        project(xs_ref[...], out_refs[n_out:])


def norm_matmul(x, xs, g, w, tm, head_major=0):
    m, d = x.shape
    ns = xs.shape[0]
    n_out = w.shape[1] // COL_GROUP
    tm = min(tm, m)
    heads = COL_GROUP // HEAD_DIM
    n_flat = n_out - head_major
    specs = ([pl.BlockSpec((heads, tm, HEAD_DIM), lambda i: (0, i, 0))] * head_major
             + [pl.BlockSpec((tm, COL_GROUP), lambda i: (i, 0))] * n_flat
             + [pl.BlockSpec((heads, ns, HEAD_DIM), lambda i: (0, 0, 0))] * head_major
             + [pl.BlockSpec((ns, COL_GROUP), lambda i: (0, 0))] * n_flat)
    shapes = ([jax.ShapeDtypeStruct((heads, m, HEAD_DIM), F32)] * head_major
              + [jax.ShapeDtypeStruct((m, COL_GROUP), F32)] * n_flat
              + [jax.ShapeDtypeStruct((heads, ns, HEAD_DIM), F32)] * head_major
              + [jax.ShapeDtypeStruct((ns, COL_GROUP), F32)] * n_flat)
    outs = pl.pallas_call(
        functools.partial(_norm_matmul_kernel, head_major=head_major),
        grid=(m // tm,),
        in_specs=[pl.BlockSpec((tm, d), lambda i: (i, 0)), _resident((ns, d)), _resident((1, d)),
                  _resident(w.shape)],
        out_specs=specs,
        out_shape=shapes,
        compiler_params=_params("arbitrary"),
        name="norm_matmul",
    )(x, xs, g.reshape(1, d), w)
    return outs[:n_out], outs[n_out:]


def _ffn_kernel(g_ref, fg_ref, xs_ref, wg_ref, wu_ref, wd_ref, x_hbm, o_ref, os_ref, xbuf, h_ref, sem, *, final_norm):
    i, f = pl.program_id(0), pl.program_id(1)
    tm = o_ref.shape[0]
    ns = xs_ref.shape[0]
    last_f = pl.num_programs(1) - 1

    def x_copy(tile):
        return pltpu.make_async_copy(x_hbm.at[pl.ds(pl.multiple_of(tile * tm, tm), tm), :], xbuf, sem)

    @pl.when((i == 0) & (f == 0))
    def _():
        x_copy(0).start()
        xs = xs_ref[...]
        h_ref[tm:, :] = jnp.zeros((h_ref.shape[0] - tm, h_ref.shape[1]), h_ref.dtype)
        h_ref[tm:tm + ns, :] = _rms(xs, g_ref[...]).astype(h_ref.dtype)
        os_ref[...] = xs

    @pl.when(f == 0)
    def _():
        x_copy(i).wait()
        x = xbuf[...]
        h_ref[0:tm, :] = _rms(x, g_ref[...]).astype(h_ref.dtype)
        o_ref[...] = x

    @pl.when((f == 1) & (i + 1 < pl.num_programs(0)))
    def _():
        x_copy(i + 1).start()

    h = h_ref[...]
    gate = jnp.dot(h, wg_ref[...].astype(BF16), preferred_element_type=F32)
    up = jnp.dot(h, wu_ref[...].astype(BF16), preferred_element_type=F32)
    act = (gate * _sigmoid(gate) * up).astype(BF16)
    down = jnp.dot(act, wd_ref[...].astype(BF16), preferred_element_type=F32)
    o_ref[...] += down[0:tm]

    @pl.when(i == 0)
    def _():
        os_ref[...] += down[tm:tm + ns]

    if final_norm:
        @pl.when(f == last_f)
        def _():
            o_ref[...] = _rms(o_ref[...], fg_ref[...])

        @pl.when((i == 0) & (f == last_f))
        def _():
            os_ref[...] = _rms(os_ref[...], fg_ref[...])


BF16_ROWS = 16


def ffn(x, xs, g, wg, wu, wd, layer, final_g, tm, tf, final_norm):
    m, d = x.shape
    ns = xs.shape[0]
    dff = wg.shape[2]
    tm = min(tm, m)
    assert m % tm == 0 and dff % tf == 0 and dff // tf >= 2 and ns <= BF16_ROWS and tm % BF16_ROWS == 0
    return pl.pallas_call(
        functools.partial(_ffn_kernel, final_norm=final_norm),
        grid=(m // tm, dff // tf),
        in_specs=[_resident((1, d)), _resident((1, d)), _resident((ns, d)),
                  pl.BlockSpec((None, d, tf), lambda i, f: (layer, 0, f)),
                  pl.BlockSpec((None, d, tf), lambda i, f: (layer, 0, f)),
                  pl.BlockSpec((None, tf, d), lambda i, f: (layer, f, 0)),
                  pl.BlockSpec(memory_space=pl.ANY)],
        out_specs=[pl.BlockSpec((tm, d), lambda i, f: (i, 0)), pl.BlockSpec((ns, d), lambda i, f: (0, 0))],
        out_shape=[jax.ShapeDtypeStruct((m, d), F32), jax.ShapeDtypeStruct((ns, d), F32)],
        scratch_shapes=[pltpu.VMEM((tm, d), F32), pltpu.VMEM((tm + BF16_ROWS, d), BF16),
                        pltpu.SemaphoreType.DMA(())],
        compiler_params=_params("arbitrary", "arbitrary"),
        name="ffn",
    )(g.reshape(1, d), final_g.reshape(1, d), xs, wg, wu, wd, x)


A_HALO = 32
B_HALO = 8


def _layer_norm_silu(x, g, b):
    mu = jnp.mean(x, axis=-1, keepdims=True)
    xc = x - mu
    y = xc * lax.rsqrt(jnp.mean(xc * xc, axis=-1, keepdims=True) + EPS) * g + b
    return y * _sigmoid(y)


def _conv_mix_kernel(av_ref, ag_ref, bh_ref, bb_ref, bc_ref, x_ref, adw_ref, adwb_ref, lng_ref, lnb_ref,
                     bdw_ref, wo_ref, y_ref, newa_ref, newb_ref, aext, cext, ashift, aconv, *, ta, tb, rc):
    l = pl.program_id(1)
    tl = av_ref.shape[1]
    wa = av_ref.shape[2]

    @pl.when(l == 0)
    def _():
        aext[0:A_HALO, :] = jnp.zeros((A_HALO, wa), F32)
        cext[0:B_HALO, :] = jnp.zeros((B_HALO, wa), F32)

    @pl.when(l > 0)
    def _():
        aext[0:A_HALO, :] = aext[tl:tl + A_HALO, :]
        cext[0:B_HALO, :] = cext[tl:tl + B_HALO, :]

    aext[A_HALO:A_HALO + tl, :] = av_ref[0] * _sigmoid(ag_ref[0])
    cext[B_HALO:B_HALO + tl, :] = bc_ref[0] * bh_ref[0]

    bconv = jnp.zeros((tl, wa), F32)
    for k in range(tb):
        off = B_HALO - (tb - 1) + k
        bconv = bconv + bdw_ref[k:k + 1, :] * cext[off:off + tl, :]
    b_out = (bb_ref[0] * bconv).astype(BF16)
    y_ref[0] = x_ref[0] + jnp.dot(b_out, wo_ref[wa:, :], preferred_element_type=F32)

    n_shift = ashift.shape[1]
    for s in range(1, SUBLANES):
        ashift[s - 1] = aext[s:s + n_shift, :]

    for r0 in range(0, tl, rc):
        for c0 in range(0, wa, LANES):
            acc = jnp.broadcast_to(adwb_ref[:, c0:c0 + LANES], (rc, LANES))
            for k in range(ta):
                off = A_HALO - (ta - 1) + r0 + k
                s = off % SUBLANES
                src = aext if s == 0 else ashift.at[s - 1]
                acc = acc + adw_ref[k:k + 1, c0:c0 + LANES] * src[off - s:off - s + rc, c0:c0 + LANES]
            aconv[r0:r0 + rc, c0:c0 + LANES] = acc
    a_out = _layer_norm_silu(aconv[...], lng_ref[...], lnb_ref[...]).astype(BF16)
    y_ref[0] += jnp.dot(a_out, wo_ref[0:wa, :], preferred_element_type=F32)

    @pl.when(l == pl.num_programs(1) - 1)
    def _():
        newa_ref[0] = aext[A_HALO + tl - (ta - 1):A_HALO + tl, :]
        newb_ref[0] = cext[B_HALO + tl - (tb - 1):B_HALO + tl, :]


def conv_mix(parts, x, a_dw, a_dw_b, ln_g, ln_b, b_dw, w_out, tl):
    bsz, seq, d = x.shape
    wa = parts[0].shape[-1]
    ta, tb = a_dw.shape[0], b_dw.shape[0]
    tl = min(tl, seq)
    assert seq % tl == 0 and tl >= A_HALO and ta - 1 <= A_HALO and tb - 1 <= B_HALO
    part_spec = pl.BlockSpec((1, tl, wa), lambda b, l: (b, l, 0))
    const = _resident
    return pl.pallas_call(
        functools.partial(_conv_mix_kernel, ta=ta, tb=tb, rc=min(128, tl)),
        grid=(bsz, seq // tl),
        in_specs=[part_spec] * 5 + [pl.BlockSpec((1, tl, d), lambda b, l: (b, l, 0)),
                                    const((ta, wa)), const((1, wa)), const((1, wa)), const((1, wa)),
                                    const((tb, wa)), const(w_out.shape)],
        out_specs=[pl.BlockSpec((1, tl, d), lambda b, l: (b, l, 0)),
                   pl.BlockSpec((1, ta - 1, wa), lambda b, l: (b, 0, 0)),
                   pl.BlockSpec((1, tb - 1, wa), lambda b, l: (b, 0, 0))],
        out_shape=[jax.ShapeDtypeStruct((bsz, seq, d), F32),
                   jax.ShapeDtypeStruct((bsz, ta - 1, wa), F32),
                   jax.ShapeDtypeStruct((bsz, tb - 1, wa), F32)],
        scratch_shapes=[pltpu.VMEM((A_HALO + tl, wa), F32), pltpu.VMEM((B_HALO + tl, wa), F32),
                        pltpu.VMEM((SUBLANES - 1, A_HALO + tl - SUBLANES, wa), F32),
                        pltpu.VMEM((tl, wa), F32)],
        compiler_params=_params("parallel", "arbitrary"),
        name="conv_mix",
    )(*parts, x, a_dw, a_dw_b.reshape(1, wa), ln_g.reshape(1, wa), ln_b.reshape(1, wa), b_dw, w_out)


def _conv_step_kernel(av_ref, ag_ref, bh_ref, bb_ref, bc_ref, x_ref, sta_ref, stb_ref, adw_ref, adwb_ref,
                      lng_ref, lnb_ref, bdw_ref, wo_ref, y_ref, anew_ref, cnew_ref, *, ta, tb):
    a = av_ref[...] * _sigmoid(ag_ref[...])
    c = bc_ref[...] * bh_ref[...]
    anew_ref[...] = a
    cnew_ref[...] = c
    acc = adwb_ref[...] + adw_ref[ta - 1:ta, :] * a
    for k in range(ta - 1):
        acc = acc + adw_ref[k:k + 1, :] * sta_ref[k]
    a_out = _layer_norm_silu(acc, lng_ref[...], lnb_ref[...])
    bconv = bdw_ref[tb - 1:tb, :] * c
    for k in range(tb - 1):
        bconv = bconv + bdw_ref[k:k + 1, :] * stb_ref[k]
    mix = jnp.concatenate([a_out, bb_ref[...] * bconv], axis=-1).astype(BF16)
    y_ref[...] = x_ref[...] + jnp.dot(mix, wo_ref[...], preferred_element_type=F32)


def conv_step(parts, x, st_a, st_b, a_dw, a_dw_b, ln_g, ln_b, b_dw, w_out):
    bd, d = x.shape
    wa = parts[0].shape[-1]
    ta, tb = a_dw.shape[0], b_dw.shape[0]
    return pl.pallas_call(
        functools.partial(_conv_step_kernel, ta=ta, tb=tb),
        out_shape=[jax.ShapeDtypeStruct((bd, d), F32), jax.ShapeDtypeStruct((bd, wa), F32),
                   jax.ShapeDtypeStruct((bd, wa), F32)],
        compiler_params=pltpu.CompilerParams(vmem_limit_bytes=VMEM_LIMIT),
        name="conv_step",
    )(*parts, x, st_a, st_b, a_dw, a_dw_b.reshape(1, wa), ln_g.reshape(1, wa), ln_b.reshape(1, wa), b_dw, w_out)


LOG2E = 1.4426950408889634
TQ = 2 * MOBA_BLOCK
DA = 2 * HEAD_DIM


def _split3(x):
    def top_bits(v):
        return lax.bitcast_convert_type(lax.bitcast_convert_type(v, jnp.uint32) & jnp.uint32(0xFFFF0000), F32)

    hi = top_bits(x)
    mid = top_bits(x - hi)
    return hi, mid, x - hi - mid


def _moba_setup(q_ref, k_ref, v_ref, qaug, kaug, vaug, kmean, slope2, nb):
    seq = k_ref.shape[1]
    bs = MOBA_BLOCK
    nbp = kmean.shape[0]
    k = k_ref[0]
    q = q_ref[0]
    kmean[...] = jnp.zeros_like(kmean)
    kmean[0:nb, :] = jnp.mean(k.reshape(nb, bs, HEAD_DIM), axis=1)

    gate = lax.dot_general(kmean[...], q, (((1,), (1,)), ((), ())), precision=lax.Precision.HIGHEST,
                           preferred_element_type=F32)
    blk = lax.broadcasted_iota(jnp.int32, (nbp, seq), 0)
    own = lax.broadcasted_iota(jnp.int32, (nbp, seq), 1) // bs
    gate = jnp.where(blk < own, gate, -jnp.inf)
    attend = blk == own
    for _ in range(MOBA_TOPK):
        m = jnp.max(gate, axis=0, keepdims=True)
        idx = jnp.min(jnp.where(gate == m, blk, nbp), axis=0, keepdims=True)
        hit = blk == idx
        attend = attend | (hit & (blk < own))
        gate = jnp.where(hit, -jnp.inf, gate)
    attend_t = jnp.where(attend, 1.0, 0.0)
    attend_r = jnp.concatenate([attend_t, jnp.zeros((LANES - nbp, seq), F32)], axis=0).T

    lane = lax.broadcasted_iota(jnp.int32, (seq, LANES), 1)
    pos_i = lax.broadcasted_iota(jnp.int32, (seq, LANES), 0)
    pos = pos_i.astype(F32)

    khi, kmid, klo = _split3(slope2 * pos)
    ek = jnp.where(lane == pos_i // bs, 1.0, 0.0)
    ek = jnp.where(lane == nb, khi, ek)
    ek = jnp.where(lane == nb + 1, kmid, ek)
    ek = jnp.where(lane == nb + 2, klo, ek)
    ek = jnp.where((lane >= nb + 3) & (lane < nb + 6), 1.0, ek)
    kaug[:, 0:HEAD_DIM] = k.astype(BF16)
    kaug[:, HEAD_DIM:] = ek.astype(BF16)

    qhi, qmid, qlo = _split3(-slope2 * pos)
    eq = jnp.where(lane < nb, jnp.where(attend_r > 0.5, 0.0, MASKED), 0.0)
    eq = jnp.where((lane >= nb) & (lane < nb + 3), 1.0, eq)
    eq = jnp.where(lane == nb + 3, qhi, eq)
    eq = jnp.where(lane == nb + 4, qmid, eq)
    eq = jnp.where(lane == nb + 5, qlo, eq)
    qaug[:, 0:HEAD_DIM] = (q * (HEAD_DIM ** -0.5 * LOG2E)).astype(BF16)
    qaug[:, HEAD_DIM:] = eq.astype(BF16)

    vaug[:, 0:HEAD_DIM] = v_ref[0].astype(BF16)
    vaug[:, HEAD_DIM:] = jnp.ones((seq, HEAD_DIM), BF16)


SUM_CHAINS = 16
PAGE_LOOKAHEAD = 3


class _PageSums:
    def __init__(self, pt_ref, kc_hbm, ksum_ref, ring, sem, *, layer, pages_per_block, n_steps):
        self.pt_ref, self.kc_hbm, self.ksum_ref, self.ring, self.sem = pt_ref, kc_hbm, ksum_ref, ring, sem
        self.layer, self.pages_per_block, self.n_steps = layer, pages_per_block, n_steps
        self.total = pt_ref.shape[0]
        self.depth = ring.shape[0]
        self.per = self.depth // PAGE_LOOKAHEAD
        self.even = n_steps * self.per == self.total

    def _copy(self, t, slot):
        return pltpu.make_async_copy(self.kc_hbm.at[self.layer, self.pt_ref[t]], self.ring.at[slot],
                                     self.sem.at[slot])

    def begin(self):
        self.ksum_ref[...] = jnp.zeros_like(self.ksum_ref)
        for slot in range(min(self.depth, self.total)):
            self._copy(slot, slot).start()

    def _fold_page(self, t, slot):
        _, page_rows, n_heads, _ = self.ring.shape
        self._copy(t, slot).wait()
        page = self.ring[slot].reshape(SUM_CHAINS, page_rows // SUM_CHAINS, n_heads, HEAD_DIM)
        self.ksum_ref[t // self.pages_per_block] += jnp.sum(jnp.sum(page, axis=1), axis=0)

    def _first_slot(self, step):
        return (step % PAGE_LOOKAHEAD) * self.per

    def fold(self, step):
        assert self.even
        for j in range(self.per):
            self._fold_page(step * self.per + j, self._first_slot(step) + j)

    def refill(self, step):
        assert self.even

        @pl.when(step + PAGE_LOOKAHEAD < self.n_steps)
        def _():
            for j in range(self.per):
                self._copy(step * self.per + j + self.depth, self._first_slot(step) + j).start()

    def step(self, step):
        for j in range(self.per):
            t = step * self.per + j

            @pl.when(t < self.total)
            def _(t=t, j=j):
                self._fold_page(t, self._first_slot(step) + j)

                @pl.when(t + self.depth < self.total)
                def _():
                    self._copy(t + self.depth, self._first_slot(step) + j).start()


def _moba_kernel(pt_ref, q_ref, k_ref, v_ref, kc_hbm, o_ref, ksum_ref, qaug, kaug, vaug, kmean, m_ref, acc_ref,
                 s_ref, rel_ref, ring, psem, *, n_heads, nb, layer, page_pairs, page_steps):
    h = pl.program_id(1)
    bs = MOBA_BLOCK
    nt = (((1,), (1,)), ((), ()))
    seq = qaug.shape[0]
    n_tiles = seq // TQ
    n_units = n_tiles * (n_tiles + 1) // 2
    n_pairs = n_units // 2
    bh = pl.program_id(0) * pl.num_programs(1) + h

    slope2 = jnp.exp2((h + 1).astype(F32) * (-8.0 / n_heads) + jnp.zeros((1, 1), F32)) * LOG2E
    _moba_setup(q_ref, k_ref, v_ref, qaug, kaug, vaug, kmean, slope2, nb)
    rel_ref[...] = (lax.broadcasted_iota(jnp.int32, (bs, bs), 1) - lax.broadcasted_iota(jnp.int32, (bs, bs), 0))

    halves = (slice(0, bs), slice(bs, TQ))

    def tile_rows(t):
        return pl.ds(pl.multiple_of(t * TQ, TQ), TQ)

    def successor(unit):
        i, g = unit
        wrap = g == i
        return jnp.where(wrap, jnp.minimum(i + 1, n_tiles - 1), i), jnp.where(wrap, 0, g + 1)

    def scores(unit):
        i, g = unit
        qa = qaug[tile_rows(i), :]
        kg = kaug[tile_rows(g), :]
        return jnp.concatenate([lax.dot_general(qa[r], kg, nt, preferred_element_type=F32) for r in halves], axis=0)

    def process(unit, slot):
        i, g = unit
        vg = vaug[tile_rows(g), :]
        for r in halves:
            diag = jnp.where(rel_ref[...] <= (i - g) * TQ, s_ref[slot, r, r], MASKED)
            sr = jnp.concatenate([diag, s_ref[slot, r, bs:]] if r.start == 0 else [s_ref[slot, r, 0:bs], diag], axis=1)
            m = jnp.where(g == 0, MASKED, m_ref[r, :])
            m_new = jnp.maximum(m, jnp.max(sr, axis=1, keepdims=True))
            p = jnp.exp2(sr - m_new).astype(BF16)
            pv = jnp.dot(p, vg, preferred_element_type=F32)
            acc = jnp.exp2(m - m_new) * acc_ref[r, :] + pv
            acc_ref[r, :] = acc
            m_ref[r, :] = m_new
            rows = pl.ds(pl.multiple_of(i * TQ + r.start, bs), bs)
            o_ref[0, rows, :] = acc[:, 0:HEAD_DIM] / acc[:, HEAD_DIM:]

    m_ref[...] = jnp.full(m_ref.shape, MASKED, F32)
    acc_ref[...] = jnp.zeros_like(acc_ref)
    first = (jnp.int32(0), jnp.int32(0))
    s_ref[0] = scores(first)

    pages = _PageSums(pt_ref, kc_hbm, ksum_ref, ring, psem, layer=layer, pages_per_block=bs // PAGE_SIZE,
                      n_steps=page_steps)

    @pl.when(bh == 0)
    def _():
        pages.begin()

    def pair(p, unit, with_pages):
        page_step = bh * page_pairs + p
        if with_pages and pages.even:
            pages.fold(page_step)
        elif with_pages:
            pages.step(page_step)
        nxt = successor(unit)
        s_ref[1] = scores(nxt)
        process(unit, 0)
        nxt2 = successor(nxt)
        s_ref[0] = scores(nxt2)
        process(nxt, 1)
        if with_pages and pages.even:
            pages.refill(page_step)
        return nxt2

    unit = lax.fori_loop(0, page_pairs, functools.partial(pair, with_pages=True), first)
    if page_pairs < n_pairs:
        unit = lax.fori_loop(page_pairs, n_pairs, functools.partial(pair, with_pages=False), unit)
    if n_units % 2:
        process(unit, 0)


def moba_prompt(q, k, v, cache_k, layer, page_table):
    n_heads, bsz, seq, _ = q.shape
    width = n_heads * HEAD_DIM
    assert seq % TQ == 0
    nb = seq // MOBA_BLOCK
    nbp = -(-nb // 8) * 8
    assert nb + 6 <= LANES
    bd, n_pages = page_table.shape
    pages_per_block = MOBA_BLOCK // PAGE_SIZE
    assert n_pages % pages_per_block == 0
    n_tiles = seq // TQ
    n_pairs = n_tiles * (n_tiles + 1) // 4
    assert n_pairs >= 1
    total_pages = bd * n_pages
    even = [p for p in range(1, n_pairs + 1) if total_pages % (bsz * n_heads * p) == 0]
    page_pairs = max(even) if even else n_pairs
    page_steps = bsz * n_heads * page_pairs
    pages_per_step = -(-total_pages // page_steps)
    sums_shape = (total_pages // pages_per_block, n_heads, HEAD_DIM)
    slab = pl.BlockSpec((None, 1, seq, HEAD_DIM), lambda b, h: (h, b, 0, 0))
    return pl.pallas_call(
        functools.partial(_moba_kernel, n_heads=n_heads, nb=nb, layer=layer, page_pairs=page_pairs,
                          page_steps=page_steps),
        grid=(bsz, n_heads),
        in_specs=[pl.BlockSpec(memory_space=pltpu.SMEM), slab, slab, slab, pl.BlockSpec(memory_space=pl.ANY)],
        out_specs=[pl.BlockSpec((1, seq, HEAD_DIM), lambda b, h: (b, 0, h)),
                   pl.BlockSpec(sums_shape, lambda b, h: (0, 0, 0))],
        out_shape=[jax.ShapeDtypeStruct((bsz, seq, width), F32), jax.ShapeDtypeStruct(sums_shape, F32)],
        scratch_shapes=[pltpu.VMEM((seq, DA), BF16), pltpu.VMEM((seq, DA), BF16), pltpu.VMEM((seq, DA), BF16),
                        pltpu.VMEM((nbp, HEAD_DIM), F32), pltpu.VMEM((TQ, 1), F32), pltpu.VMEM((TQ, DA), F32),
                        pltpu.VMEM((2, TQ, TQ), F32), pltpu.VMEM((MOBA_BLOCK, MOBA_BLOCK), jnp.int32),
                        pltpu.VMEM((PAGE_LOOKAHEAD * pages_per_step, PAGE_SIZE, n_heads, HEAD_DIM), F32),
                        pltpu.SemaphoreType.DMA((PAGE_LOOKAHEAD * pages_per_step,))],
        compiler_params=_params("arbitrary", "arbitrary"),
        name="moba_prompt",
    )(page_table.reshape(-1), q, k, v, cache_k)


PAGES_IN_FLIGHT = 12


def _gate_step_kernel(q_ref, ksum_ref, sel_ref, *, n_heads):
    bd = q_ref.shape[0]
    nblk = ksum_ref.shape[0] // bd
    sel_ref[...] = jnp.zeros_like(sel_ref)
    for b in range(bd):
        kmean = ksum_ref[b * nblk:(b + 1) * nblk] * (1.0 / MOBA_BLOCK)
        gate = jnp.sum(kmean * q_ref[b][None], axis=-1)
        blk = lax.broadcasted_iota(jnp.int32, gate.shape, 0)
        for r in range(MOBA_TOPK):
            m = jnp.max(gate, axis=0, keepdims=True)
            idx = jnp.min(jnp.where(gate == m, blk, nblk), axis=0, keepdims=True)
            sel_ref[b, r:r + 1, 0:n_heads] = idx
            gate = jnp.where(blk == idx, -jnp.inf, gate)


SEL_ROWS = 8


def gate_step(block_sums, q):
    bd, n_heads, _ = q.shape
    assert MOBA_TOPK <= SEL_ROWS and n_heads <= LANES
    sel = pl.pallas_call(
        functools.partial(_gate_step_kernel, n_heads=n_heads),
        out_shape=jax.ShapeDtypeStruct((bd, SEL_ROWS, LANES), jnp.int32),
        compiler_params=pltpu.CompilerParams(vmem_limit_bytes=VMEM_LIMIT),
        name="gate_step",
    )(q, block_sums)
    return sel[:, :MOBA_TOPK, :n_heads]


def _attn_step_kernel(pt_ref, sel_ref, q_ref, kn_ref, vn_ref, k_hbm, v_hbm, o_ref, kbuf, vbuf, ksem, vsem,
                      m_ref, l_ref, acc_ref, *, layer, n_pages, q_pos, pages_per_block):
    bd, n_heads, _ = q_ref.shape
    per_head = MOBA_TOPK * pages_per_block
    total = bd * n_heads * per_head
    depth = kbuf.shape[0]
    scale = HEAD_DIM ** -0.5
    own = q_pos // MOBA_BLOCK

    def coords(t):
        bh, j = t // per_head, t % per_head
        b, h = bh // n_heads, bh % n_heads
        return b, h, j, sel_ref[(b * MOBA_TOPK + j // pages_per_block) * n_heads + h]

    def page_copies(t, slot):
        b, _, j, n = coords(t)
        page = pt_ref[b * n_pages + n * pages_per_block + j % pages_per_block]
        return (pltpu.make_async_copy(k_hbm.at[layer, page], kbuf.at[slot], ksem.at[slot]),
                pltpu.make_async_copy(v_hbm.at[layer, page], vbuf.at[slot], vsem.at[slot]))

    for t in range(min(depth, total)):
        for c in page_copies(t, t):
            c.start()

    @pl.loop(0, total)
    def _(t):
        slot = t % depth
        b, h, j, n = coords(t)
        for c in page_copies(t, slot):
            c.wait()
        q = q_ref[b, pl.ds(h, 1), :]

        @pl.when(j == 0)
        def _():
            m_ref[...] = jnp.sum(q * kn_ref[b, pl.ds(h, 1), :], axis=1, keepdims=True) * scale
            l_ref[...] = jnp.ones_like(l_ref)
            acc_ref[...] = vn_ref[b, pl.ds(h, 1), :]

        k = kbuf[slot, pl.ds(h, PAGE_SIZE, stride=n_heads), :]
        v = vbuf[slot, pl.ds(h, PAGE_SIZE, stride=n_heads), :]
        slope = jnp.exp2(jnp.asarray(h + 1, F32) * (-8.0 / n_heads) + jnp.zeros((1, 1), F32))
        kpos = (n * MOBA_BLOCK + (j % pages_per_block) * PAGE_SIZE
                + lax.broadcasted_iota(jnp.int32, (PAGE_SIZE, 1), 0))
        dist = (q_pos - kpos).astype(F32)
        s = jnp.sum(k * q, axis=1, keepdims=True) * scale - slope * dist
        s = jnp.where((dist >= 0.0) & (n < own), s, MASKED)
        m = m_ref[...]
        m_new = jnp.maximum(m, jnp.max(s, axis=0, keepdims=True))
        alpha = jnp.exp(m - m_new)
        p = jnp.exp(s - m_new)
        l_ref[...] = alpha * l_ref[...] + jnp.sum(p, axis=0, keepdims=True)
        acc_ref[...] = alpha * acc_ref[...] + jnp.sum(p * v, axis=0, keepdims=True)
        m_ref[...] = m_new

        @pl.when(j == per_head - 1)
        def _():
            o_ref[b, pl.ds(h, 1), :] = acc_ref[...] / l_ref[...]

        @pl.when(t + depth < total)
        def _():
            for c in page_copies(t + depth, slot):
                c.start()


def attn_step(cache_k, cache_v, layer, page_table, sel, q, k_new, v_new):
    bd, n_pages = page_table.shape
    n_heads = q.shape[1]
    pages_per_block = MOBA_BLOCK // PAGE_SIZE
    assert n_pages % pages_per_block == 0
    rows = PAGE_SIZE * n_heads
    as_rows = lambda c: c.reshape(c.shape[0], c.shape[1], rows, HEAD_DIM)
    vmem, smem = pl.BlockSpec(memory_space=pltpu.VMEM), pl.BlockSpec(memory_space=pltpu.SMEM)
    hbm = pl.BlockSpec(memory_space=pl.ANY)
    return pl.pallas_call(
        functools.partial(_attn_step_kernel, layer=layer, n_pages=n_pages, q_pos=n_pages * PAGE_SIZE,
                          pages_per_block=pages_per_block),
        in_specs=[smem, smem, vmem, vmem, vmem, hbm, hbm],
        out_specs=vmem,
        out_shape=jax.ShapeDtypeStruct((bd, n_heads, HEAD_DIM), F32),
        scratch_shapes=[pltpu.VMEM((PAGES_IN_FLIGHT, rows, HEAD_DIM), F32),
                        pltpu.VMEM((PAGES_IN_FLIGHT, rows, HEAD_DIM), F32),
                        pltpu.SemaphoreType.DMA((PAGES_IN_FLIGHT,)), pltpu.SemaphoreType.DMA((PAGES_IN_FLIGHT,)),
                        pltpu.VMEM((1, 1), F32), pltpu.VMEM((1, 1), F32), pltpu.VMEM((1, HEAD_DIM), F32)],
        compiler_params=pltpu.CompilerParams(vmem_limit_bytes=VMEM_LIMIT),
        name="attn_step",
    )(page_table.reshape(-1), sel.reshape(-1), q, k_new, v_new, as_rows(cache_k), as_rows(cache_v))


P_HALO = 16


def _pool_out_kernel(o_ref, u_ref, x_ref, pw_ref, ps_ref, wo_ref, y_ref, newp_ref, uext, mix):
    l = pl.program_id(1)
    tl = u_ref.shape[1]
    wp = u_ref.shape[2]
    wo_attn = o_ref.shape[2]
    pg = wp // len(POOL_WINDOWS)

    @pl.when(l == 0)
    def _():
        uext[0:P_HALO, :] = jnp.zeros((P_HALO, wp), F32)

    @pl.when(l > 0)
    def _():
        uext[0:P_HALO, :] = uext[tl:tl + P_HALO, :]

    uext[P_HALO:P_HALO + tl, :] = u_ref[0]
    o_bf16 = o_ref[0].astype(BF16)
    n_chunk = y_ref.shape[2] // len(POOL_WINDOWS)
    pos = l * tl + lax.broadcasted_iota(jnp.int32, (tl, pg), 0)
    for g, w in enumerate(POOL_WINDOWS):
        n0 = g * n_chunk
        y_ref[0, :, n0:n0 + n_chunk] = x_ref[0, :, n0:n0 + n_chunk] + jnp.dot(
            o_bf16, wo_ref[0:wo_attn, n0:n0 + n_chunk], preferred_element_type=F32)
        c0 = g * pg
        wsum = uext[P_HALO:P_HALO + tl, c0:c0 + pg]
        for j in range(1, w):
            wsum = wsum + uext[P_HALO - j:P_HALO - j + tl, c0:c0 + pg]
        count = jnp.minimum(pos + 1, w).astype(F32)
        dlt = (wsum / count - uext[P_HALO:P_HALO + tl, c0:c0 + pg]).astype(BF16)
        yp = jnp.dot(dlt, pw_ref[g], preferred_element_type=F32) * ps_ref[:, c0:c0 + pg]
        mix[:, c0:c0 + pg] = yp.astype(BF16)

    y_ref[0] += jnp.dot(mix[...], wo_ref[wo_attn:, :], preferred_element_type=F32)

    @pl.when(l == pl.num_programs(1) - 1)
    def _():
        newp_ref[0] = uext[P_HALO + tl - (POOL_MAX - 1):P_HALO + tl, :]


def pool_out(o, u, x, pool_w, pool_scale, w_out, tl):
    bsz, seq, d = x.shape
    wp = u.shape[-1]
    tl = min(tl, seq)
    assert seq % tl == 0 and tl >= P_HALO
    const = _resident
    return pl.pallas_call(
        _pool_out_kernel,
        grid=(bsz, seq // tl),
        in_specs=[pl.BlockSpec((1, tl, o.shape[-1]), lambda b, l: (b, l, 0)),
                  pl.BlockSpec((1, tl, wp), lambda b, l: (b, l, 0)),
                  pl.BlockSpec((1, tl, d), lambda b, l: (b, l, 0)),
                  const(pool_w.shape), const((1, wp)), const(w_out.shape)],
        out_specs=[pl.BlockSpec((1, tl, d), lambda b, l: (b, l, 0)),
                   pl.BlockSpec((1, POOL_MAX - 1, wp), lambda b, l: (b, 0, 0))],
        out_shape=[jax.ShapeDtypeStruct((bsz, seq, d), F32),
                   jax.ShapeDtypeStruct((bsz, POOL_MAX - 1, wp), F32)],
        scratch_shapes=[pltpu.VMEM((P_HALO + tl, wp), F32), pltpu.VMEM((tl, wp), BF16)],
        compiler_params=_params("parallel", "arbitrary"),
        name="pool_out",
    )(o, u, x, pool_w, pool_scale.reshape(1, wp), w_out)


def _pool_step_kernel(o_ref, u_ref, x_ref, stp_ref, pw_ref, ps_ref, wo_ref, y_ref, *, start_pos):
    wp = u_ref.shape[1]
    pg = wp // len(POOL_WINDOWS)
    u = u_ref[...]
    parts = [o_ref[...].astype(BF16)]
    for g, w in enumerate(POOL_WINDOWS):
        c0 = g * pg
        wsum = u[:, c0:c0 + pg]
        for j in range(1, w):
            wsum = wsum + stp_ref[POOL_MAX - 1 - j][:, c0:c0 + pg]
        count = float(min(start_pos + 1, w))
        dlt = (wsum / count - u[:, c0:c0 + pg]).astype(BF16)
        yp = jnp.dot(dlt, pw_ref[g], preferred_element_type=F32) * ps_ref[:, c0:c0 + pg]
        parts.append(yp.astype(BF16))
    mix = jnp.concatenate(parts, axis=-1)
    y_ref[...] = x_ref[...] + jnp.dot(mix, wo_ref[...], preferred_element_type=F32)


def pool_step(o, u, x, st_p, pool_w, pool_scale, w_out, start_pos):
    bd, d = x.shape
    return pl.pallas_call(
        functools.partial(_pool_step_kernel, start_pos=start_pos),
        out_shape=jax.ShapeDtypeStruct((bd, d), F32),
        compiler_params=pltpu.CompilerParams(vmem_limit_bytes=VMEM_LIMIT),
        name="pool_step",
    )(o, u, x, st_p, pool_w, pool_scale.reshape(1, -1), w_out)


TM_PROMPT = 512
TM_FFN = 1024
TF = 512
TL_MIX = 256
TL_POOL = 512


def kernel(x_prompt, x_sample, state_conv_a, state_conv_b, cache_k, cache_v, state_pool, page_table, conv_norm_g, conv_w_in, conv_a_dw, conv_a_dw_b, conv_a_ln_g, conv_a_ln_b, conv_b_dw, conv_w_out, attn_norm_g, attn_w_in, pool_w, pool_scale, attn_w_out, ffn_norm_g, ffn_w_gate, ffn_w_up, ffn_w_down, final_norm_g):
    bsz, seq, d = x_prompt.shape
    bd, dec_seq, _ = x_sample.shape
    assert dec_seq == 1
    depth = ffn_norm_g.shape[0]
    n_heads = cache_k.shape[3]
    wattn = n_heads * HEAD_DIM
    past_len = page_table.shape[1] * PAGE_SIZE

    layer_bf16 = lambda w, n: w[n].astype(BF16)

    xp = x_prompt.reshape(bsz * seq, d)
    xs = x_sample.reshape(bd, d)
    pa, pb, pk, pv, pp = [], [], [], [], []
    sa, sb, sk, sv, sp = [], [], [], [], []
    for layer in range(depth):
        i = layer // 2
        if layer % 2 == 0:
            w_in, w_out = layer_bf16(conv_w_in, i), layer_bf16(conv_w_out, i)
            parts, parts_s = norm_matmul(xp, xs, conv_norm_g[i], w_in, TM_PROMPT)
            wa = parts[0].shape[-1]
            y, na, nb_ = conv_mix([t.reshape(bsz, seq, wa) for t in parts], xp.reshape(bsz, seq, d),
                                  conv_a_dw[i], conv_a_dw_b[i], conv_a_ln_g[i], conv_a_ln_b[i], conv_b_dw[i],
                                  w_out, TL_MIX)
            xp = y.reshape(bsz * seq, d)
            pa.append(na)
            pb.append(nb_)

            st_a = jnp.swapaxes(state_conv_a[i], 0, 1)
            st_b = jnp.swapaxes(state_conv_b[i], 0, 1)
            xs, a_new, c_new = conv_step(parts_s, xs, st_a, st_b, conv_a_dw[i], conv_a_dw_b[i], conv_a_ln_g[i],
                                         conv_a_ln_b[i], conv_b_dw[i], w_out)
            sa.append(jnp.concatenate([state_conv_a[i][:, 1:], a_new[:, None]], axis=1))
            sb.append(jnp.concatenate([state_conv_b[i][:, 1:], c_new[:, None]], axis=1))
        else:
            w_in, w_out, w_pool = layer_bf16(attn_w_in, i), layer_bf16(attn_w_out, i), layer_bf16(pool_w, i)
            (q, k, v, u), proj_s = norm_matmul(xp, xs, attn_norm_g[i], w_in, TM_PROMPT, head_major=3)
            shp = (n_heads, bsz, seq, HEAD_DIM)
            o, block_sums = moba_prompt(q.reshape(shp), k.reshape(shp), v.reshape(shp), cache_k, i, page_table)
            y, npool = pool_out(o, u.reshape(bsz, seq, -1), xp.reshape(bsz, seq, d), w_pool, pool_scale[i],
                                w_out, TL_POOL)
            xp = y.reshape(bsz * seq, d)
            pk.append(jnp.transpose(k.reshape(shp), (1, 2, 0, 3)))
            pv.append(jnp.transpose(v.reshape(shp), (1, 2, 0, 3)))
            pp.append(npool)

            q, k, v, u = proj_s
            heads = lambda t: jnp.swapaxes(t, 0, 1)
            sel = gate_step(block_sums, heads(q))
            o = attn_step(cache_k, cache_v, i, page_table, sel, heads(q), heads(k), heads(v))
            st_p = jnp.swapaxes(state_pool[i], 0, 1)
            xs = pool_step(o.reshape(bd, wattn), u, xs, st_p, w_pool, pool_scale[i], w_out, past_len)
            sk.append(heads(k).reshape(bd, 1, n_heads, HEAD_DIM))
            sv.append(heads(v).reshape(bd, 1, n_heads, HEAD_DIM))
            sp.append(jnp.concatenate([state_pool[i][:, 1:], u[:, None]], axis=1))
        last = layer == depth - 1
        ffn_w = (ffn_w_gate, ffn_w_up, ffn_w_down, layer)
        xp, xs = ffn(xp, xs, ffn_norm_g[layer], *ffn_w, final_norm_g, TM_FFN, TF, last)
    return (xp.reshape(bsz, seq, d), xs.reshape(bd, 1, d), jnp.stack(pa), jnp.stack(sa), jnp.stack(pb),
            jnp.stack(sb), jnp.stack(pk), jnp.stack(sk), jnp.stack(pv), jnp.stack(sv), jnp.stack(pp), jnp.stack(sp))
```
